```python
import jax, jax.numpy as jnp
from jax import lax
import numpy as np

D_MODEL = 1024
BATCH = 8
SEQ = 4096
DEPTH = 1

MIX_WIDTH = D_MODEL
HG_WIDTH = MIX_WIDTH // 2
HG_HEAD_DIM = 128
HG_HEADS = HG_WIDTH // HG_HEAD_DIM
HG_CHUNK = 64
RW_WIDTH = MIX_WIDTH - HG_WIDTH
RW_HEAD_DIM = 64
RW_HEADS = RW_WIDTH // RW_HEAD_DIM
RW_DECAY_LORA = 64
RW_AAA_LORA = 64
RW_GATE_LORA = 128
RW_COLS = 3 * RW_WIDTH + RW_DECAY_LORA + RW_AAA_LORA + RW_GATE_LORA
HG_COLS = 4 * HG_WIDTH
IN_COLS = HG_COLS + RW_COLS
D_FF = 2816
CONV_WIDTH = 3
NORM_EPS = 1e-6
RW_GN_EPS = 64e-5
L2_EPS = 1e-12

kernel_name = "hybrid_hgrn2_rwkv7_convffn"


def _rmsnorm(x, w):
    xf = x.astype(jnp.float32)
    y = xf * lax.rsqrt(jnp.mean(xf * xf, axis=-1, keepdims=True) + NORM_EPS)
    return (y * w.astype(jnp.float32)).astype(x.dtype)


def _token_shift(z):
    return jnp.pad(z, ((0, 0), (1, 0), (0, 0)))[:, :-1]


def _hgrn2_chunk_scan(q, k, v, logf):
    B, T, H, K = q.shape
    V = v.shape[-1]
    C = HG_CHUNK
    N = T // C

    def to_chunks(z):
        return z.reshape(B, N, C, H, z.shape[-1]).transpose(1, 0, 3, 2, 4)

    qc, kc, vc, gc = to_chunks(q), to_chunks(k), to_chunks(v), to_chunks(logf)
    causal = jnp.tril(jnp.ones((C, C), dtype=bool))[:, :, None]

    def step(S, inp):
        qb, kb, vb, gb = inp
        b = jnp.cumsum(gb, axis=2)
        diff = b[:, :, :, None, :] - b[:, :, None, :, :]
        dec = jnp.exp(jnp.where(causal, diff, -jnp.inf))
        A = jnp.einsum('bhtk,bhsk,bhtsk->bhts', qb, kb, dec)
        o = jnp.einsum('bhts,bhsv->bhtv', A, vb) + jnp.einsum('bhtk,bhkv->bhtv', qb * jnp.exp(b), S)
        b_last = b[:, :, -1:, :]
        S = jnp.exp(b_last[:, :, 0, :])[..., None] * S + jnp.einsum(
            'bhsk,bhsv->bhkv', kb * jnp.exp(b_last - b), vb)
        return S, o

    S0 = jnp.zeros((B, H, K, V), jnp.float32)
    _, o = lax.scan(step, S0, (qc, kc, vc, gc))
    return o.transpose(1, 0, 3, 2, 4).reshape(B, T, H, V)


def _hgrn2_mixer(q_raw, f_raw, i_raw, g_raw, lb, norm_w):
    B, T, _ = q_raw.shape
    f32 = jnp.float32

    def heads(z):
        return z.reshape(B, T, HG_HEADS, HG_HEAD_DIM)

    f = lb + (1.0 - lb) * jax.nn.sigmoid(f_raw.astype(f32))
    q = heads(jax.nn.silu(q_raw.astype(f32))) * (HG_HEAD_DIM ** -0.5)
    k = heads(1.0 - f)
    logf = heads(jnp.log(f))
    v = heads(i_raw.astype(f32))
    o = _hgrn2_chunk_scan(q, k, v, logf)
    o = o * lax.rsqrt(jnp.mean(o * o, axis=-1, keepdims=True) + NORM_EPS)
    o = o.reshape(B, T, HG_WIDTH) * norm_w.astype(f32) * jax.nn.silu(g_raw.astype(f32))
    return o.astype(q_raw.dtype)


def _rwkv7_scan(r, w, k, v, a, b):
    B, T, H, N = r.shape

    def step(S, inp):
        rt, wt, kt, vt, at, bt = inp
        sa = jnp.einsum('bhvk,bhk->bhv', S, at)
        S = S * wt[:, :, None, :] + sa[..., None] * bt[:, :, None, :] + vt[..., None] * kt[:, :, None, :]
        y = jnp.einsum('bhvk,bhk->bhv', S, rt)
        return S, y

    xs = tuple(z.transpose(1, 0, 2, 3) for z in (r, w, k, v, a, b))
    S0 = jnp.zeros((B, H, N, N), jnp.float32)
    _, y = lax.scan(step, S0, xs)
    return y.transpose(1, 0, 2, 3)


def _rwkv7_mixer(r, k, v, wd, ad, gd, w0, w2, a0, a2, g2, k_k, k_a, r_k, ln_w, ln_b):
    B, T, _ = r.shape
    out_dtype = r.dtype
    f32 = jnp.float32

    def heads(z):
        return z.reshape(B, T, RW_HEADS, RW_HEAD_DIM)

    r, k, v = r.astype(f32), k.astype(f32), v.astype(f32)
    w = -jax.nn.softplus(-(w0.astype(f32) + jnp.tanh(wd.astype(f32)) @ w2.astype(f32))) - 0.5
    decay = jnp.exp(-jnp.exp(w))
    a = jax.nn.sigmoid(a0.astype(f32) + ad.astype(f32) @ a2.astype(f32))
    g = jax.nn.sigmoid(gd.astype(f32)) @ g2.astype(f32)
    kk = heads(k * k_k.astype(f32))
    kk = kk / jnp.maximum(jnp.sqrt(jnp.sum(kk * kk, axis=-1, keepdims=True)), L2_EPS)
    k = k * (1.0 + (a - 1.0) * k_a.astype(f32))
    y = _rwkv7_scan(heads(r), heads(decay), heads(k), heads(v), -kk, kk * heads(a))
    mu = jnp.mean(y, axis=-1, keepdims=True)
    var = jnp.mean(jnp.square(y - mu), axis=-1, keepdims=True)
    y = ((y - mu) * lax.rsqrt(var + RW_GN_EPS)).reshape(B, T, RW_WIDTH)
    y = y * ln_w.astype(f32) + ln_b.astype(f32)
    bonus = jnp.sum(heads(r * k * r_k.astype(f32)), axis=-1, keepdims=True) * heads(v)
    y = y + bonus.reshape(B, T, RW_WIDTH)
    return (y * g).astype(out_dtype)


def _conv_ffn(h, w_up, conv_w, conv_b, w_down):
    u = h @ w_up
    C = u.shape[-1]
    u = lax.conv_general_dilated(
        u, conv_w[:, None, :], window_strides=(1,), padding=[(CONV_WIDTH - 1, 0)],
        dimension_numbers=('NWC', 'WIO', 'NWC'), feature_group_count=C) + conv_b
    gate, val = jnp.split(u, 2, axis=-1)
    return (jax.nn.silu(gate) * val) @ w_down


def _fwd_setup_inputs(seed: int = 0) -> dict:
    key = jax.random.key(seed)
    ks = jax.random.split(key, 24)
    f32 = jnp.float32
    nrm = lambda k, s: jax.random.normal(k, s, f32)
    L = DEPTH
    return {
        "x": nrm(ks[0], (BATCH, SEQ, D_MODEL)),
        "norm1_w": 1.0 + 0.02 * nrm(ks[1], (L, D_MODEL)),
        "w_in": nrm(ks[2], (L, D_MODEL, IN_COLS)) * D_MODEL ** -0.5,
        "hg_lb_logits": 0.1 * nrm(ks[3], (L + 1, HG_WIDTH)),
        "hg_norm_w": 1.0 + 0.02 * nrm(ks[4], (L, HG_WIDTH)),
        "rw_shift_mu": jax.random.uniform(ks[5], (L, RW_COLS), f32),
        "rw_w0": jax.random.uniform(ks[6], (L, RW_WIDTH), f32, -6.5, -1.5),
        "rw_w2": 0.1 * nrm(ks[7], (L, RW_DECAY_LORA, RW_WIDTH)) * RW_DECAY_LORA ** -0.5,
        "rw_a0": 0.1 * nrm(ks[8], (L, RW_WIDTH)),
        "rw_a2": 0.1 * nrm(ks[9], (L, RW_AAA_LORA, RW_WIDTH)) * RW_AAA_LORA ** -0.5,
        "rw_g2": nrm(ks[10], (L, RW_GATE_LORA, RW_WIDTH)) * RW_GATE_LORA ** -0.5,
        "rw_k_k": 0.85 + 0.02 * nrm(ks[11], (L, RW_WIDTH)),
        "rw_k_a": 1.0 + 0.02 * nrm(ks[12], (L, RW_WIDTH)),
        "rw_r_k": 0.1 * nrm(ks[13], (L, RW_WIDTH)),
        "rw_ln_w": 1.0 + 0.02 * nrm(ks[14], (L, RW_WIDTH)),
        "rw_ln_b": 0.02 * nrm(ks[15], (L, RW_WIDTH)),
        "w_out": nrm(ks[16], (L, MIX_WIDTH, D_MODEL)) * MIX_WIDTH ** -0.5,
        "norm2_w": 1.0 + 0.02 * nrm(ks[17], (L, D_MODEL)),
        "w_up": nrm(ks[18], (L, D_MODEL, 2 * D_FF)) * D_MODEL ** -0.5,
        "conv_w": nrm(ks[19], (L, CONV_WIDTH, 2 * D_FF)) * CONV_WIDTH ** -0.5,
        "conv_b": 0.02 * nrm(ks[20], (L, 2 * D_FF)),
        "w_down": nrm(ks[21], (L, D_FF, D_MODEL)) * D_FF ** -0.5,
        "final_norm_w": 1.0 + 0.02 * nrm(ks[22], (D_MODEL,)),
    }


def _fwd_reference(x, norm1_w, w_in, hg_lb_logits, hg_norm_w, rw_shift_mu, rw_w0, rw_w2, rw_a0, rw_a2,
              rw_g2, rw_k_k, rw_k_a, rw_r_k, rw_ln_w, rw_ln_b, w_out, norm2_w, w_up, conv_w,
              conv_b, w_down, final_norm_w):
    lower_bounds = jnp.cumsum(jax.nn.softmax(hg_lb_logits.astype(jnp.float32), axis=0), axis=0)
    rw_split = np.cumsum([RW_WIDTH, RW_WIDTH, RW_WIDTH, RW_DECAY_LORA, RW_AAA_LORA]).tolist()
    for l in range(DEPTH):
        h = _rmsnorm(x, norm1_w[l])
        proj = h @ w_in[l]
        hg_p, rw_p = proj[..., :HG_COLS], proj[..., HG_COLS:]
        q_raw, f_raw, i_raw, g_raw = jnp.split(hg_p, 4, axis=-1)
        o_hg = _hgrn2_mixer(q_raw, f_raw, i_raw, g_raw, lower_bounds[l], hg_norm_w[l])
        rw_p = rw_p + (_token_shift(rw_p) - rw_p) * rw_shift_mu[l]
        r, k, v, wd, ad, gd = jnp.split(rw_p, rw_split, axis=-1)
        o_rw = _rwkv7_mixer(r, k, v, wd, ad, gd, rw_w0[l], rw_w2[l], rw_a0[l], rw_a2[l], rw_g2[l],
                            rw_k_k[l], rw_k_a[l], rw_r_k[l], rw_ln_w[l], rw_ln_b[l])
        x = x + jnp.concatenate([o_hg, o_rw], axis=-1) @ w_out[l]
        x = x + _conv_ffn(_rmsnorm(x, norm2_w[l]), w_up[l], conv_w[l], conv_b[l], w_down[l])
    return _rmsnorm(x, final_norm_w)


import jax as _jax
import jax.numpy as _jnp

TWIN_FORMAT = 'train_step'
FWD_PARAMS = ['x', 'norm1_w', 'w_in', 'hg_lb_logits', 'hg_norm_w', 'rw_shift_mu', 'rw_w0', 'rw_w2', 'rw_a0', 'rw_a2', 'rw_g2', 'rw_k_k', 'rw_k_a', 'rw_r_k', 'rw_ln_w', 'rw_ln_b', 'w_out', 'norm2_w', 'w_up', 'conv_w', 'conv_b', 'w_down', 'final_norm_w']
TWIN_WEIGHTS = ['norm1_w', 'w_in', 'hg_lb_logits', 'hg_norm_w', 'rw_shift_mu', 'rw_w0', 'rw_w2', 'rw_a0', 'rw_a2', 'rw_g2', 'rw_k_k', 'rw_k_a', 'rw_r_k', 'rw_ln_w', 'rw_ln_b', 'w_out', 'norm2_w', 'w_up', 'conv_w', 'conv_b', 'w_down', 'final_norm_w']
TWIN_DIFF_INPUT = 'x'
TWIN_INPUTS = ['x', 'norm1_w', 'w_in', 'hg_lb_logits', 'hg_norm_w', 'rw_shift_mu', 'rw_w0', 'rw_w2', 'rw_a0', 'rw_a2', 'rw_g2', 'rw_k_k', 'rw_k_a', 'rw_r_k', 'rw_ln_w', 'rw_ln_b', 'w_out', 'norm2_w', 'w_up', 'conv_w', 'conv_b', 'w_down', 'final_norm_w', 'loss_target', 'm_norm1_w', 'm_w_in', 'm_hg_lb_logits', 'm_hg_norm_w', 'm_rw_shift_mu', 'm_rw_w0', 'm_rw_w2', 'm_rw_a0', 'm_rw_a2', 'm_rw_g2', 'm_rw_k_k', 'm_rw_k_a', 'm_rw_r_k', 'm_rw_ln_w', 'm_rw_ln_b', 'm_w_out', 'm_norm2_w', 'm_w_up', 'm_conv_w', 'm_conv_b', 'm_w_down', 'm_final_norm_w', 'v_norm1_w', 'v_w_in', 'v_hg_lb_logits', 'v_hg_norm_w', 'v_rw_shift_mu', 'v_rw_w0', 'v_rw_w2', 'v_rw_a0', 'v_rw_a2', 'v_rw_g2', 'v_rw_k_k', 'v_rw_k_a', 'v_rw_r_k', 'v_rw_ln_w', 'v_rw_ln_b', 'v_w_out', 'v_norm2_w', 'v_w_up', 'v_conv_w', 'v_conv_b', 'v_w_down', 'v_final_norm_w']
TWIN_OUTPUTS = ['loss', 'grad_x', 'grad_norm1_w', 'grad_w_in', 'grad_hg_lb_logits', 'grad_hg_norm_w', 'grad_rw_shift_mu', 'grad_rw_w0', 'grad_rw_w2', 'grad_rw_a0', 'grad_rw_a2', 'grad_rw_g2', 'grad_rw_k_k', 'grad_rw_k_a', 'grad_rw_r_k', 'grad_rw_ln_w', 'grad_rw_ln_b', 'grad_w_out', 'grad_norm2_w', 'grad_w_up', 'grad_conv_w', 'grad_conv_b', 'grad_w_down', 'grad_final_norm_w', 'delta_norm1_w', 'delta_w_in', 'delta_hg_lb_logits', 'delta_hg_norm_w', 'delta_rw_shift_mu', 'delta_rw_w0', 'delta_rw_w2', 'delta_rw_a0', 'delta_rw_a2', 'delta_rw_g2', 'delta_rw_k_k', 'delta_rw_k_a', 'delta_rw_r_k', 'delta_rw_ln_w', 'delta_rw_ln_b', 'delta_w_out', 'delta_norm2_w', 'delta_w_up', 'delta_conv_w', 'delta_conv_b', 'delta_w_down', 'delta_final_norm_w', 'new_m_norm1_w', 'new_m_w_in', 'new_m_hg_lb_logits', 'new_m_hg_norm_w', 'new_m_rw_shift_mu', 'new_m_rw_w0', 'new_m_rw_w2', 'new_m_rw_a0', 'new_m_rw_a2', 'new_m_rw_g2', 'new_m_rw_k_k', 'new_m_rw_k_a', 'new_m_rw_r_k', 'new_m_rw_ln_w', 'new_m_rw_ln_b', 'new_m_w_out', 'new_m_norm2_w', 'new_m_w_up', 'new_m_conv_w', 'new_m_conv_b', 'new_m_w_down', 'new_m_final_norm_w', 'new_v_norm1_w', 'new_v_w_in', 'new_v_hg_lb_logits', 'new_v_hg_norm_w', 'new_v_rw_shift_mu', 'new_v_rw_w0', 'new_v_rw_w2', 'new_v_rw_a0', 'new_v_rw_a2', 'new_v_rw_g2', 'new_v_rw_k_k', 'new_v_rw_k_a', 'new_v_rw_r_k', 'new_v_rw_ln_w', 'new_v_rw_ln_b', 'new_v_w_out', 'new_v_norm2_w', 'new_v_w_up', 'new_v_conv_w', 'new_v_conv_b', 'new_v_w_down', 'new_v_final_norm_w']
TWIN_LEAF_KINDS = {'loss': 'loss', 'grad_x': 'grad_x', 'grad_norm1_w': 'grad_w', 'grad_w_in': 'grad_w', 'grad_hg_lb_logits': 'grad_w', 'grad_hg_norm_w': 'grad_w', 'grad_rw_shift_mu': 'grad_w', 'grad_rw_w0': 'grad_w', 'grad_rw_w2': 'grad_w', 'grad_rw_a0': 'grad_w', 'grad_rw_a2': 'grad_w', 'grad_rw_g2': 'grad_w', 'grad_rw_k_k': 'grad_w', 'grad_rw_k_a': 'grad_w', 'grad_rw_r_k': 'grad_w', 'grad_rw_ln_w': 'grad_w', 'grad_rw_ln_b': 'grad_w', 'grad_w_out': 'grad_w', 'grad_norm2_w': 'grad_w', 'grad_w_up': 'grad_w', 'grad_conv_w': 'grad_w', 'grad_conv_b': 'grad_w', 'grad_w_down': 'grad_w', 'grad_final_norm_w': 'grad_w', 'delta_norm1_w': 'delta_w', 'delta_w_in': 'delta_w', 'delta_hg_lb_logits': 'delta_w', 'delta_hg_norm_w': 'delta_w', 'delta_rw_shift_mu': 'delta_w', 'delta_rw_w0': 'delta_w', 'delta_rw_w2': 'delta_w', 'delta_rw_a0': 'delta_w', 'delta_rw_a2': 'delta_w', 'delta_rw_g2': 'delta_w', 'delta_rw_k_k': 'delta_w', 'delta_rw_k_a': 'delta_w', 'delta_rw_r_k': 'delta_w', 'delta_rw_ln_w': 'delta_w', 'delta_rw_ln_b': 'delta_w', 'delta_w_out': 'delta_w', 'delta_norm2_w': 'delta_w', 'delta_w_up': 'delta_w', 'delta_conv_w': 'delta_w', 'delta_conv_b': 'delta_w', 'delta_w_down': 'delta_w', 'delta_final_norm_w': 'delta_w', 'new_m_norm1_w': 'new_m', 'new_m_w_in': 'new_m', 'new_m_hg_lb_logits': 'new_m', 'new_m_hg_norm_w': 'new_m', 'new_m_rw_shift_mu': 'new_m', 'new_m_rw_w0': 'new_m', 'new_m_rw_w2': 'new_m', 'new_m_rw_a0': 'new_m', 'new_m_rw_a2': 'new_m', 'new_m_rw_g2': 'new_m', 'new_m_rw_k_k': 'new_m', 'new_m_rw_k_a': 'new_m', 'new_m_rw_r_k': 'new_m', 'new_m_rw_ln_w': 'new_m', 'new_m_rw_ln_b': 'new_m', 'new_m_w_out': 'new_m', 'new_m_norm2_w': 'new_m', 'new_m_w_up': 'new_m', 'new_m_conv_w': 'new_m', 'new_m_conv_b': 'new_m', 'new_m_w_down': 'new_m', 'new_m_final_norm_w': 'new_m', 'new_v_norm1_w': 'new_v', 'new_v_w_in': 'new_v', 'new_v_hg_lb_logits': 'new_v', 'new_v_hg_norm_w': 'new_v', 'new_v_rw_shift_mu': 'new_v', 'new_v_rw_w0': 'new_v', 'new_v_rw_w2': 'new_v', 'new_v_rw_a0': 'new_v', 'new_v_rw_a2': 'new_v', 'new_v_rw_g2': 'new_v', 'new_v_rw_k_k': 'new_v', 'new_v_rw_k_a': 'new_v', 'new_v_rw_r_k': 'new_v', 'new_v_rw_ln_w': 'new_v', 'new_v_rw_ln_b': 'new_v', 'new_v_w_out': 'new_v', 'new_v_norm2_w': 'new_v', 'new_v_w_up': 'new_v', 'new_v_conv_w': 'new_v', 'new_v_conv_b': 'new_v', 'new_v_w_down': 'new_v', 'new_v_final_norm_w': 'new_v'}


def _forward(args):
    return _fwd_reference(*[args[k] for k in FWD_PARAMS])


def _output_shape():
    def fwd():
        inp = _fwd_setup_inputs(0)
        return _fwd_reference(*[inp[k] for k in FWD_PARAMS])
    out = _jax.eval_shape(fwd)
    return out.shape, out.dtype

N_MICROBATCH = 1
ADAM_LR = 0.001
ADAM_B1 = 0.9
ADAM_B2 = 0.999
ADAM_EPS = 1e-08
ADAM_WD = 0.01
ADAM_STEP = 10
PER_EXAMPLE_BATCH_AXIS = {'x': 0, 'loss_target': 0}
SHARED_INPUTS = []
_WEIGHT_DTYPES = {'norm1_w': _jnp.float32, 'w_in': _jnp.float32, 'hg_lb_logits': _jnp.float32, 'hg_norm_w': _jnp.float32, 'rw_shift_mu': _jnp.float32, 'rw_w0': _jnp.float32, 'rw_w2': _jnp.float32, 'rw_a0': _jnp.float32, 'rw_a2': _jnp.float32, 'rw_g2': _jnp.float32, 'rw_k_k': _jnp.float32, 'rw_k_a': _jnp.float32, 'rw_r_k': _jnp.float32, 'rw_ln_w': _jnp.float32, 'rw_ln_b': _jnp.float32, 'w_out': _jnp.float32, 'norm2_w': _jnp.float32, 'w_up': _jnp.float32, 'conv_w': _jnp.float32, 'conv_b': _jnp.float32, 'w_down': _jnp.float32, 'final_norm_w': _jnp.float32}
MOMENT_SCALE = {'norm1_w': 1.586169e-01, 'w_in': 8.363014e-02, 'hg_lb_logits': 1.020911e-02, 'hg_norm_w': 1.105587e-01, 'rw_shift_mu': 1.514875e-01, 'rw_w0': 3.072542e-02, 'rw_w2': 3.676941e-03, 'rw_a0': 4.089872e-02, 'rw_a2': 3.882272e-02, 'rw_g2': 9.262219e-02, 'rw_k_k': 1.058116e-01, 'rw_k_a': 9.883808e-02, 'rw_r_k': 1.888228e-01, 'rw_ln_w': 8.553671e-02, 'rw_ln_b': 9.406582e-02, 'w_out': 9.667264e-02, 'norm2_w': 1.198107e-01, 'w_up': 5.097652e-02, 'conv_w': 5.113604e-02, 'conv_b': 5.053875e-02, 'w_down': 8.355831e-02, 'final_norm_w': 3.201203e+01}


def _to_microbatches(a, axis):
    t = _jnp.moveaxis(a, axis, 0)
    t = t.reshape((N_MICROBATCH, t.shape[0] // N_MICROBATCH) + t.shape[1:])
    return _jnp.moveaxis(t, 1, axis + 1)


def setup_inputs(seed: int = 0) -> dict:
    inp = _fwd_setup_inputs(seed)
    key = _jax.random.fold_in(_jax.random.key(seed), 7919)
    shape, _ = _output_shape()
    out = dict(inp)
    out["loss_target"] = _jax.random.normal(_jax.random.fold_in(key, 0), shape, _jnp.float32)
    for i, name in enumerate(TWIN_WEIGHTS):
        w = inp[name].astype(_jnp.float32)
        if MOMENT_SCALE is None:
            s = _jnp.sqrt(_jnp.mean(_jnp.square(w)) + 1e-30)
        else:
            s = MOMENT_SCALE[name]
        km, kv = _jax.random.split(_jax.random.fold_in(key, i + 1))
        out[name] = w
        out["m_" + name] = s * _jax.random.normal(km, w.shape, _jnp.float32)
        out["v_" + name] = (s * s) * _jax.random.uniform(kv, w.shape, _jnp.float32, 0.5, 1.5)
    if N_MICROBATCH > 1:
        for name, axis in PER_EXAMPLE_BATCH_AXIS.items():
            out[name] = _to_microbatches(out[name], axis)
    return {'x': out['x'], 'norm1_w': out['norm1_w'], 'w_in': out['w_in'], 'hg_lb_logits': out['hg_lb_logits'], 'hg_norm_w': out['hg_norm_w'], 'rw_shift_mu': out['rw_shift_mu'], 'rw_w0': out['rw_w0'], 'rw_w2': out['rw_w2'], 'rw_a0': out['rw_a0'], 'rw_a2': out['rw_a2'], 'rw_g2': out['rw_g2'], 'rw_k_k': out['rw_k_k'], 'rw_k_a': out['rw_k_a'], 'rw_r_k': out['rw_r_k'], 'rw_ln_w': out['rw_ln_w'], 'rw_ln_b': out['rw_ln_b'], 'w_out': out['w_out'], 'norm2_w': out['norm2_w'], 'w_up': out['w_up'], 'conv_w': out['conv_w'], 'conv_b': out['conv_b'], 'w_down': out['w_down'], 'final_norm_w': out['final_norm_w'], 'loss_target': out['loss_target'], 'm_norm1_w': out['m_norm1_w'], 'm_w_in': out['m_w_in'], 'm_hg_lb_logits': out['m_hg_lb_logits'], 'm_hg_norm_w': out['m_hg_norm_w'], 'm_rw_shift_mu': out['m_rw_shift_mu'], 'm_rw_w0': out['m_rw_w0'], 'm_rw_w2': out['m_rw_w2'], 'm_rw_a0': out['m_rw_a0'], 'm_rw_a2': out['m_rw_a2'], 'm_rw_g2': out['m_rw_g2'], 'm_rw_k_k': out['m_rw_k_k'], 'm_rw_k_a': out['m_rw_k_a'], 'm_rw_r_k': out['m_rw_r_k'], 'm_rw_ln_w': out['m_rw_ln_w'], 'm_rw_ln_b': out['m_rw_ln_b'], 'm_w_out': out['m_w_out'], 'm_norm2_w': out['m_norm2_w'], 'm_w_up': out['m_w_up'], 'm_conv_w': out['m_conv_w'], 'm_conv_b': out['m_conv_b'], 'm_w_down': out['m_w_down'], 'm_final_norm_w': out['m_final_norm_w'], 'v_norm1_w': out['v_norm1_w'], 'v_w_in': out['v_w_in'], 'v_hg_lb_logits': out['v_hg_lb_logits'], 'v_hg_norm_w': out['v_hg_norm_w'], 'v_rw_shift_mu': out['v_rw_shift_mu'], 'v_rw_w0': out['v_rw_w0'], 'v_rw_w2': out['v_rw_w2'], 'v_rw_a0': out['v_rw_a0'], 'v_rw_a2': out['v_rw_a2'], 'v_rw_g2': out['v_rw_g2'], 'v_rw_k_k': out['v_rw_k_k'], 'v_rw_k_a': out['v_rw_k_a'], 'v_rw_r_k': out['v_rw_r_k'], 'v_rw_ln_w': out['v_rw_ln_w'], 'v_rw_ln_b': out['v_rw_ln_b'], 'v_w_out': out['v_w_out'], 'v_norm2_w': out['v_norm2_w'], 'v_w_up': out['v_w_up'], 'v_conv_w': out['v_conv_w'], 'v_conv_b': out['v_conv_b'], 'v_w_down': out['v_w_down'], 'v_final_norm_w': out['v_final_norm_w']}


def _loss(weights, diff, rest, loss_target):
    with _jax.named_scope("forward"):
        args = {**rest, TWIN_DIFF_INPUT: diff, **{k: w.astype(_WEIGHT_DTYPES[k]) for k, w in weights.items()}}
        y = _forward(args)
    with _jax.named_scope("loss_head"):
        err = _jnp.square(y.astype(_jnp.float32) - loss_target)
        return 0.5 * _jnp.sum(_jnp.mean(err, axis=-1)) if err.ndim else 0.5 * err


def _adamw(w, g, m, v):
    m = ADAM_B1 * m + (1.0 - ADAM_B1) * g
    v = ADAM_B2 * v + (1.0 - ADAM_B2) * _jnp.square(g)
    m_hat = m / (1.0 - ADAM_B1 ** ADAM_STEP)
    v_hat = v / (1.0 - ADAM_B2 ** ADAM_STEP)
    delta = -ADAM_LR * (m_hat / (_jnp.sqrt(v_hat) + ADAM_EPS) + ADAM_WD * w)
    return delta, m, v


def reference(x, norm1_w, w_in, hg_lb_logits, hg_norm_w, rw_shift_mu, rw_w0, rw_w2, rw_a0, rw_a2, rw_g2, rw_k_k, rw_k_a, rw_r_k, rw_ln_w, rw_ln_b, w_out, norm2_w, w_up, conv_w, conv_b, w_down, final_norm_w, loss_target, m_norm1_w, m_w_in, m_hg_lb_logits, m_hg_norm_w, m_rw_shift_mu, m_rw_w0, m_rw_w2, m_rw_a0, m_rw_a2, m_rw_g2, m_rw_k_k, m_rw_k_a, m_rw_r_k, m_rw_ln_w, m_rw_ln_b, m_w_out, m_norm2_w, m_w_up, m_conv_w, m_conv_b, m_w_down, m_final_norm_w, v_norm1_w, v_w_in, v_hg_lb_logits, v_hg_norm_w, v_rw_shift_mu, v_rw_w0, v_rw_w2, v_rw_a0, v_rw_a2, v_rw_g2, v_rw_k_k, v_rw_k_a, v_rw_r_k, v_rw_ln_w, v_rw_ln_b, v_w_out, v_norm2_w, v_w_up, v_conv_w, v_conv_b, v_w_down, v_final_norm_w):
    given = dict(x=x, norm1_w=norm1_w, w_in=w_in, hg_lb_logits=hg_lb_logits, hg_norm_w=hg_norm_w, rw_shift_mu=rw_shift_mu, rw_w0=rw_w0, rw_w2=rw_w2, rw_a0=rw_a0, rw_a2=rw_a2, rw_g2=rw_g2, rw_k_k=rw_k_k, rw_k_a=rw_k_a, rw_r_k=rw_r_k, rw_ln_w=rw_ln_w, rw_ln_b=rw_ln_b, w_out=w_out, norm2_w=norm2_w, w_up=w_up, conv_w=conv_w, conv_b=conv_b, w_down=w_down, final_norm_w=final_norm_w, loss_target=loss_target, m_norm1_w=m_norm1_w, m_w_in=m_w_in, m_hg_lb_logits=m_hg_lb_logits, m_hg_norm_w=m_hg_norm_w, m_rw_shift_mu=m_rw_shift_mu, m_rw_w0=m_rw_w0, m_rw_w2=m_rw_w2, m_rw_a0=m_rw_a0, m_rw_a2=m_rw_a2, m_rw_g2=m_rw_g2, m_rw_k_k=m_rw_k_k, m_rw_k_a=m_rw_k_a, m_rw_r_k=m_rw_r_k, m_rw_ln_w=m_rw_ln_w, m_rw_ln_b=m_rw_ln_b, m_w_out=m_w_out, m_norm2_w=m_norm2_w, m_w_up=m_w_up, m_conv_w=m_conv_w, m_conv_b=m_conv_b, m_w_down=m_w_down, m_final_norm_w=m_final_norm_w, v_norm1_w=v_norm1_w, v_w_in=v_w_in, v_hg_lb_logits=v_hg_lb_logits, v_hg_norm_w=v_hg_norm_w, v_rw_shift_mu=v_rw_shift_mu, v_rw_w0=v_rw_w0, v_rw_w2=v_rw_w2, v_rw_a0=v_rw_a0, v_rw_a2=v_rw_a2, v_rw_g2=v_rw_g2, v_rw_k_k=v_rw_k_k, v_rw_k_a=v_rw_k_a, v_rw_r_k=v_rw_r_k, v_rw_ln_w=v_rw_ln_w, v_rw_ln_b=v_rw_ln_b, v_w_out=v_w_out, v_norm2_w=v_norm2_w, v_w_up=v_w_up, v_conv_w=v_conv_w, v_conv_b=v_conv_b, v_w_down=v_w_down, v_final_norm_w=v_final_norm_w)
    weights = {n: given[n] for n in TWIN_WEIGHTS}
    shared = {n: given[n] for n in SHARED_INPUTS}
    per_example = {n: given[n] for n in ['x']}
    grad_fn = _jax.value_and_grad(_loss, argnums=(0, 1))

    def one_microbatch(ex, loss_target):
        ex = dict(ex)
        diff = ex.pop(TWIN_DIFF_INPUT)
        return grad_fn(weights, diff, {**shared, **ex}, loss_target)

    if N_MICROBATCH == 1:
        loss, (grad_w, grad_x) = one_microbatch(per_example, given["loss_target"])
    else:
        def body(carry, xs):
            loss_sum, grad_sum = carry
            l_k, (gw_k, gx_k) = one_microbatch(xs[0], xs[1])
            with _jax.named_scope("update"):
                return (loss_sum + l_k, _jax.tree.map(_jnp.add, grad_sum, gw_k)), gx_k

        init = (_jnp.zeros((), _jnp.float32), _jax.tree.map(_jnp.zeros_like, weights))
        (loss, grad_w), grad_x = _jax.lax.scan(body, init, (per_example, given["loss_target"]))
    with _jax.named_scope("update"):
        delta_w, new_m, new_v = {}, {}, {}
        for n in TWIN_WEIGHTS:
            delta_w[n], new_m[n], new_v[n] = _adamw(weights[n], grad_w[n], given["m_" + n], given["v_" + n])
    return (loss, grad_x, *[grad_w[n] for n in TWIN_WEIGHTS], *[delta_w[n] for n in TWIN_WEIGHTS],
            *[new_m[n] for n in TWIN_WEIGHTS], *[new_v[n] for n in TWIN_WEIGHTS])
```

```python
import functools

import jax
import jax.numpy as jnp
from jax import lax
from jax.experimental import pallas as pl
from jax.experimental.pallas import tpu as pltpu

F32 = jnp.float32
BF16 = jnp.bfloat16
HIGHEST = lax.Precision.HIGHEST
MESH_ID = pl.DeviceIdType.MESH

N_DEV = 8
D_MODEL = 1024
HG_WIDTH = 512
HG_HEAD_DIM = 128
HG_HEADS = 4
RW_WIDTH = 512
RW_PAIRS = 4
RW_HEAD_DIM = 64
HG_COLS = 2048
RW_COLS = 1792
D_FF = 2816
NORM_EPS = 1e-6
RW_GN_EPS = 64e-5
L2_EPS = 1e-12
ADAM_LR, ADAM_B1, ADAM_B2, ADAM_EPS, ADAM_WD, ADAM_STEP = 0.001, 0.9, 0.999, 1e-08, 0.01, 10

HG_CHUNK = 16
RW_CHUNK = 32
SCAN_ROWS = 256
LANES = 128

NN = ((1,), (0,))
NT = ((1,), (1,))
TN = ((0,), (0,))


def _dot(a, b, dims=NN, precision=HIGHEST):
    return lax.dot_general(a, b, (dims, ((), ())), precision=precision, preferred_element_type=F32)


def _iota2(shape, dim):
    return lax.broadcasted_iota(jnp.int32, shape, dim)


def _sigmoid(z):
    return 1.0 / (1.0 + jnp.exp(-z))


def _row_tile(n, want):
    t = min(n, want)
    while n % t:
        t //= 2
    return t


def _rms_fwd(x, w, name):
    T, D = x.shape
    tb = _row_tile(T, 512)

    def body(x_ref, w_ref, h_ref):
        xv = x_ref[...]
        r = lax.rsqrt(jnp.mean(xv * xv, axis=-1, keepdims=True) + NORM_EPS)
        h_ref[...] = (xv * r * w_ref[...]).astype(h_ref.dtype)

    return pl.pallas_call(
        body, name=name, grid=(T // tb,),
        in_specs=[pl.BlockSpec((tb, D), lambda i: (i, 0)), pl.BlockSpec((1, D), lambda i: (0, 0))],
        out_specs=pl.BlockSpec((tb, D), lambda i: (i, 0)),
        out_shape=jax.ShapeDtypeStruct((T, D), BF16),
    )(x, w)


def _rms_bwd(dh, x, w, dres, name):
    T, D = x.shape
    tb = _row_tile(T, 256)

    def body(dh_ref, x_ref, w_ref, dres_ref, dx_ref, dw_ref):
        @pl.when(pl.program_id(0) == 0)
        def _():
            dw_ref[...] = jnp.zeros_like(dw_ref)

        xv = x_ref[...]
        r = lax.rsqrt(jnp.mean(xv * xv, axis=-1, keepdims=True) + NORM_EPS)
        xn = xv * r
        dy = dh_ref[...].astype(F32)
        dxn = dy * w_ref[...]
        dx_ref[...] = dres_ref[...] + r * (dxn - xn * jnp.mean(dxn * xn, axis=-1, keepdims=True))
        dw_ref[...] += jnp.sum(dy * xn, axis=0, keepdims=True)

    row = pl.BlockSpec((tb, D), lambda i: (i, 0))
    vec = pl.BlockSpec((1, D), lambda i: (0, 0))
    return pl.pallas_call(
        body, name=name, grid=(T // tb,),
        in_specs=[row, row, vec, row], out_specs=[row, vec],
        out_shape=[jax.ShapeDtypeStruct((T, D), F32), jax.ShapeDtypeStruct((1, D), F32)],
    )(dh, x, w, dres)


def _mm_nt(a, bt, name, out_dtype=F32):
    T, K = a.shape
    N = bt.shape[0]
    tm = _row_tile(T, 256)

    def body(a_ref, b_ref, o_ref):
        o_ref[...] = _dot(a_ref[...].astype(BF16), b_ref[...].astype(BF16), NT, None).astype(o_ref.dtype)

    return pl.pallas_call(
        body, name=name, grid=(T // tm,),
        in_specs=[pl.BlockSpec((tm, K), lambda i: (i, 0)), pl.BlockSpec((N, K), lambda i: (0, 0))],
        out_specs=pl.BlockSpec((tm, N), lambda i: (i, 0)),
        out_shape=jax.ShapeDtypeStruct((T, N), out_dtype),
    )(a, bt)


def _mm_nn(a, b, res, name, out_dtype=F32):
    T, K = a.shape
    N = b.shape[1]
    tm = _row_tile(T, 256)

    def body(a_ref, b_ref, *rest):
        o_ref = rest[-1]
        acc = _dot(a_ref[...].astype(BF16), b_ref[...].astype(BF16), NN, None)
        if res is not None:
            acc = acc + rest[0][...]
        o_ref[...] = acc.astype(o_ref.dtype)

    in_specs = [pl.BlockSpec((tm, K), lambda i: (i, 0)), pl.BlockSpec((K, N), lambda i: (0, 0))]
    args = [a, b]
    if res is not None:
        in_specs.append(pl.BlockSpec((tm, N), lambda i: (i, 0)))
        args.append(res)
    return pl.pallas_call(
        body, name=name, grid=(T // tm,), in_specs=in_specs,
        out_specs=pl.BlockSpec((tm, N), lambda i: (i, 0)),
        out_shape=jax.ShapeDtypeStruct((T, N), out_dtype),
    )(*args)


def _mm_tn(a, b, tmm, name):
    T, M = a.shape
    N = b.shape[1]
    tk = _row_tile(T, 512)

    def body(a_ref, b_ref, o_ref):
        @pl.when(pl.program_id(1) == 0)
        def _():
            o_ref[...] = jnp.zeros_like(o_ref)

        o_ref[...] += _dot(a_ref[...].astype(BF16), b_ref[...].astype(BF16), TN, None)

    return pl.pallas_call(
        body, name=name, grid=(M // tmm, T // tk),
        in_specs=[pl.BlockSpec((tk, tmm), lambda m, k: (k, m)), pl.BlockSpec((tk, N), lambda m, k: (k, 0))],
        out_specs=pl.BlockSpec((tmm, N), lambda m, k: (m, 0)),
        out_shape=jax.ShapeDtypeStruct((M, N), F32),
    )(a, b)


def _shift_rows_down(z, n):
    rows = _iota2(z.shape, 0)
    return jnp.where(rows < n, 0.0, pltpu.roll(z, n, 0))


def _shift_rows_up(z, n):
    T = z.shape[0]
    rows = _iota2(z.shape, 0)
    return jnp.where(rows >= T - n, 0.0, pltpu.roll(z, T - n, 0))


def _shift_fwd(proj, mu, name):
    T = proj.shape[0]
    nblk = RW_COLS // LANES
    first = HG_COLS // LANES

    def body(p_ref, mu_ref, o_ref):
        p = p_ref[...]
        o_ref[...] = p + (_shift_rows_down(p, 1) - p) * mu_ref[...]

    return pl.pallas_call(
        body, name=name, grid=(nblk,),
        in_specs=[pl.BlockSpec((T, LANES), lambda j: (0, first + j)), pl.BlockSpec((1, LANES), lambda j: (0, j))],
        out_specs=pl.BlockSpec((T, LANES), lambda j: (0, j)),
        out_shape=jax.ShapeDtypeStruct((T, RW_COLS), F32),
    )(proj, mu)


def _shift_bwd(parts, proj, mu, col0, name):
    T = proj.shape[0]
    width = parts[0][2]
    nblk = width // LANES
    first = (HG_COLS + col0) // LANES
    mu0 = col0 // LANES
    n = len(parts)

    def body(*refs):
        p_ref, mu_ref = refs[n], refs[n + 1]
        dp_ref, dmu_ref = refs[n + 2], refs[n + 3]
        load = lambda r: r[0] if len(r.shape) == 3 else r[...]
        ds = load(refs[0])
        for r in refs[1:n]:
            ds = ds + load(r)
        p = p_ref[...]
        m = mu_ref[...]
        dp_ref[...] = (ds * (1.0 - m) + _shift_rows_up(ds * m, 1)).astype(dp_ref.dtype)
        dmu_ref[...] = jnp.sum(ds * (_shift_rows_down(p, 1) - p), axis=0, keepdims=True)

    in_specs = [pl.BlockSpec(blk, imap) for (_, (blk, imap), _) in parts]
    in_specs += [pl.BlockSpec((T, LANES), lambda j: (0, first + j)), pl.BlockSpec((1, LANES), lambda j: (0, mu0 + j))]
    return pl.pallas_call(
        body, name=name, grid=(nblk,), in_specs=in_specs,
        out_specs=[pl.BlockSpec((T, LANES), lambda j: (0, j)), pl.BlockSpec((1, LANES), lambda j: (0, j))],
        out_shape=[jax.ShapeDtypeStruct((T, width), BF16), jax.ShapeDtypeStruct((1, width), F32)],
    )(*[p[0] for p in parts], proj, mu)


def _conv3(z, w_ref):
    return w_ref[0:1, :] * _shift_rows_down(z, 2) + w_ref[1:2, :] * _shift_rows_down(z, 1) + w_ref[2:3, :] * z


def _ffn_act_fwd(u, conv_w, conv_b, name):
    T = u.shape[0]
    nblk = D_FF // LANES

    def body(ug_ref, uv_ref, wg_ref, wv_ref, bg_ref, bv_ref, act_ref):
        gate = _conv3(ug_ref[...], wg_ref) + bg_ref[...]
        val = _conv3(uv_ref[...], wv_ref) + bv_ref[...]
        act_ref[...] = (gate * _sigmoid(gate) * val).astype(act_ref.dtype)

    col = lambda off: pl.BlockSpec((T, LANES), lambda j: (0, off + j))
    wsp = lambda off: pl.BlockSpec((3, LANES), lambda j: (0, off + j))
    bsp = lambda off: pl.BlockSpec((1, LANES), lambda j: (0, off + j))
    return pl.pallas_call(
        body, name=name, grid=(nblk,),
        in_specs=[col(0), col(nblk), wsp(0), wsp(nblk), bsp(0), bsp(nblk)],
        out_specs=pl.BlockSpec((T, LANES), lambda j: (0, j)),
        out_shape=jax.ShapeDtypeStruct((T, D_FF), BF16),
    )(u, u, conv_w, conv_w, conv_b, conv_b)


def _ffn_act_bwd(u, dact, conv_w, conv_b, name):
    T = u.shape[0]
    nblk = D_FF // LANES

    def conv_bwd(z, dzc, w_ref, du_ref, dw_ref, db_ref):
        du = w_ref[2:3, :] * dzc + w_ref[1:2, :] * _shift_rows_up(dzc, 1) + w_ref[0:1, :] * _shift_rows_up(dzc, 2)
        du_ref[...] = du.astype(du_ref.dtype)
        dw_ref[0:1, :] = jnp.sum(dzc * _shift_rows_down(z, 2), axis=0, keepdims=True)
        dw_ref[1:2, :] = jnp.sum(dzc * _shift_rows_down(z, 1), axis=0, keepdims=True)
        dw_ref[2:3, :] = jnp.sum(dzc * z, axis=0, keepdims=True)
        db_ref[...] = jnp.sum(dzc, axis=0, keepdims=True)

    def body(ug_ref, uv_ref, da_ref, wg_ref, wv_ref, bg_ref, bv_ref,
             dug_ref, duv_ref, dwg_ref, dwv_ref, dbg_ref, dbv_ref):
        ug, uv = ug_ref[...], uv_ref[...]
        gate = _conv3(ug, wg_ref) + bg_ref[...]
        val = _conv3(uv, wv_ref) + bv_ref[...]
        da = da_ref[...].astype(F32)
        sg = _sigmoid(gate)
        dgate = da * val * (sg * (1.0 + gate * (1.0 - sg)))
        dval = da * gate * sg
        conv_bwd(ug, dgate, wg_ref, dug_ref, dwg_ref, dbg_ref)
        conv_bwd(uv, dval, wv_ref, duv_ref, dwv_ref, dbv_ref)

    col = lambda off: pl.BlockSpec((T, LANES), lambda j: (0, off + j))
    wsp = lambda off: pl.BlockSpec((3, LANES), lambda j: (0, off + j))
    bsp = lambda off: pl.BlockSpec((1, LANES), lambda j: (0, off + j))
    half = lambda r, dt: jax.ShapeDtypeStruct((r, D_FF), dt)
    return pl.pallas_call(
        body, name=name, grid=(nblk,),
        in_specs=[col(0), col(nblk), col(0), wsp(0), wsp(nblk), bsp(0), bsp(nblk)],
        out_specs=[col(0), col(0), wsp(0), wsp(0), bsp(0), bsp(0)],
        out_shape=[half(T, BF16), half(T, BF16), half(3, F32), half(3, F32), half(1, F32), half(1, F32)],
    )(u, u, dact, conv_w, conv_w, conv_b, conv_b)


def _loss_head(x2, w, target, name):
    T, D = x2.shape
    tb = _row_tile(T, 256)

    def body(x_ref, w_ref, t_ref, loss_ref, dx_ref, dw_ref):
        @pl.when(pl.program_id(0) == 0)
        def _():
            loss_ref[...] = jnp.zeros_like(loss_ref)
            dw_ref[...] = jnp.zeros_like(dw_ref)

        xv = x_ref[...]
        r = lax.rsqrt(jnp.mean(xv * xv, axis=-1, keepdims=True) + NORM_EPS)
        xn = xv * r
        err = xn * w_ref[...] - t_ref[...]
        row_loss = jnp.sum(err * err, axis=-1, keepdims=True) * (0.5 / D)
        loss_ref[...] += jnp.sum(row_loss, axis=0, keepdims=True)
        dy = err * (1.0 / D)
        dxn = dy * w_ref[...]
        dx_ref[...] = r * (dxn - xn * jnp.mean(dxn * xn, axis=-1, keepdims=True))
        dw_ref[...] += jnp.sum(dy * xn, axis=0, keepdims=True)

    row = pl.BlockSpec((tb, D), lambda i: (i, 0))
    vec = pl.BlockSpec((1, D), lambda i: (0, 0))
    return pl.pallas_call(
        body, name=name, grid=(T // tb,),
        in_specs=[row, vec, row],
        out_specs=[pl.BlockSpec((1, 1), lambda i: (0, 0)), row, vec],
        out_shape=[jax.ShapeDtypeStruct((1, 1), F32), jax.ShapeDtypeStruct((T, D), F32),
                   jax.ShapeDtypeStruct((1, D), F32)],
    )(x2, w, target)


def _adamw(w, g, m, v, name):
    R, C = w.shape
    tb = _row_tile(R, 256) if R % 8 == 0 else R

    def body(w_ref, g_ref, m_ref, v_ref, d_ref, nm_ref, nv_ref):
        gv = g_ref[...]
        nm = ADAM_B1 * m_ref[...] + (1.0 - ADAM_B1) * gv
        nv = ADAM_B2 * v_ref[...] + (1.0 - ADAM_B2) * (gv * gv)
        m_hat = nm / (1.0 - ADAM_B1 ** ADAM_STEP)
        v_hat = nv / (1.0 - ADAM_B2 ** ADAM_STEP)
        d_ref[...] = -ADAM_LR * (m_hat / (jnp.sqrt(v_hat) + ADAM_EPS) + ADAM_WD * w_ref[...])
        nm_ref[...] = nm
        nv_ref[...] = nv

    blk = pl.BlockSpec((tb, C), lambda i: (i, 0))
    sd = jax.ShapeDtypeStruct((R, C), F32)
    return pl.pallas_call(
        body, name=name, grid=(R // tb,), in_specs=[blk] * 4, out_specs=[blk] * 3, out_shape=[sd] * 3,
    )(w, g, m, v)


def _hg_consts():
    C = HG_CHUNK
    r, c = _iota2((C, C), 0), _iota2((C, C), 1)
    tril = (c <= r).astype(F32)
    ones = jnp.ones((C, C), F32)
    sel_mid = (c == C // 2 - 1).astype(F32)
    ones_s = jnp.ones((HG_HEAD_DIM, C), F32)
    return tril, ones, sel_mid, ones_s


def _hg_chunk(consts, S, qr, fr, ir, gr, l0, l1, nw):
    tril, ones, sel_mid, ones_s = consts
    lb = _sigmoid(l0 - l1)
    f = lb + (1.0 - lb) * _sigmoid(fr)
    q = qr * _sigmoid(qr) * (HG_HEAD_DIM ** -0.5)
    k = 1.0 - f
    g = jnp.log(f)
    a = _dot(tril, g)
    tot = _dot(ones, g)
    mid = _dot(sel_mid, a)
    o = _dot(q * jnp.exp(a), S, NT)
    att = _dot(q * jnp.exp(a - mid), k * jnp.exp(mid - a), NT) * tril
    o = o + _dot(att, ir)
    S_new = S * jnp.exp(_dot(ones_s, g)) + _dot(ir, k * jnp.exp(tot - a), TN)
    on = o * lax.rsqrt(jnp.mean(o * o, axis=-1, keepdims=True) + NORM_EPS)
    out = on * nw * (gr * _sigmoid(gr))
    return S_new, out


def _hg_specs(T, tb, rev):
    nT = T // tb
    tix = (lambda t: nT - 1 - t) if rev else (lambda t: t)
    col = lambda off: pl.BlockSpec((tb, LANES), lambda h, t: (tix(t), off + h))
    vec = pl.BlockSpec((1, LANES), lambda h, t: (0, h))
    st = pl.BlockSpec((1, tb // HG_CHUNK, HG_HEAD_DIM, HG_HEAD_DIM), lambda h, t: (h, tix(t), 0, 0))
    return nT, col, vec, st


def _hg_fwd(proj, l0, l1, nw, name):
    T = proj.shape[0]
    tb = _row_tile(T, SCAN_ROWS)
    nsub = tb // HG_CHUNK
    nT, col, vec, st = _hg_specs(T, tb, False)

    def body(q_ref, f_ref, i_ref, g_ref, l0_ref, l1_ref, nw_ref, o_ref, st_ref, s_ref):
        @pl.when(pl.program_id(1) == 0)
        def _():
            s_ref[...] = jnp.zeros_like(s_ref)

        consts = _hg_consts()

        def step(j, carry):
            rows = pl.ds(pl.multiple_of(j * HG_CHUNK, HG_CHUNK), HG_CHUNK)
            S = s_ref[...]
            st_ref[0, j] = S
            S_new, out = _hg_chunk(consts, S, q_ref[rows, :], f_ref[rows, :], i_ref[rows, :], g_ref[rows, :],
                                   l0_ref[...], l1_ref[...], nw_ref[...])
            s_ref[...] = S_new
            o_ref[rows, :] = out
            return carry

        lax.fori_loop(0, nsub, step, 0)

    nh = HG_HEADS
    return pl.pallas_call(
        body, name=name, grid=(nh, nT),
        in_specs=[col(0), col(nh), col(2 * nh), col(3 * nh), vec, vec, vec],
        out_specs=[col(0), st],
        out_shape=[jax.ShapeDtypeStruct((T, HG_WIDTH), F32),
                   jax.ShapeDtypeStruct((nh, T // HG_CHUNK, HG_HEAD_DIM, HG_HEAD_DIM), F32)],
        scratch_shapes=[pltpu.VMEM((HG_HEAD_DIM, HG_HEAD_DIM), F32)],
    )(proj, proj, proj, proj, l0, l1, nw)


def _hg_bwd(proj, states, do, l0, l1, nw, name):
    T = proj.shape[0]
    tb = _row_tile(T, SCAN_ROWS)
    nsub = tb // HG_CHUNK
    nT, col, vec, st = _hg_specs(T, tb, True)

    def body(q_ref, f_ref, i_ref, g_ref, st_ref, do_ref, l0_ref, l1_ref, nw_ref,
             dq_ref, df_ref, di_ref, dg_ref, dl0_ref, dl1_ref, dnw_ref, ds_ref):
        @pl.when(pl.program_id(1) == 0)
        def _():
            ds_ref[...] = jnp.zeros_like(ds_ref)
            dl0_ref[...] = jnp.zeros_like(dl0_ref)
            dl1_ref[...] = jnp.zeros_like(dl1_ref)
            dnw_ref[...] = jnp.zeros_like(dnw_ref)

        consts = _hg_consts()

        def step(i, carry):
            j = nsub - 1 - i
            rows = pl.ds(pl.multiple_of(j * HG_CHUNK, HG_CHUNK), HG_CHUNK)
            args = (st_ref[0, j], q_ref[rows, :], f_ref[rows, :], i_ref[rows, :], g_ref[rows, :],
                    l0_ref[...], l1_ref[...], nw_ref[...])
            _, vjp = jax.vjp(functools.partial(_hg_chunk, consts), *args)
            dS, dq, df, di, dg, dl0, dl1, dnw = vjp((ds_ref[...], do_ref[rows, :]))
            ds_ref[...] = dS
            dq_ref[rows, :] = dq.astype(dq_ref.dtype)
            df_ref[rows, :] = df.astype(df_ref.dtype)
            di_ref[rows, :] = di.astype(di_ref.dtype)
            dg_ref[rows, :] = dg.astype(dg_ref.dtype)
            dl0_ref[...] += dl0
            dl1_ref[...] += dl1
            dnw_ref[...] += dnw
            return carry

        lax.fori_loop(0, nsub, step, 0)

    nh = HG_HEADS
    dcol = jax.ShapeDtypeStruct((T, HG_WIDTH), BF16)
    dvec = jax.ShapeDtypeStruct((1, HG_WIDTH), F32)
    return pl.pallas_call(
        body, name=name, grid=(nh, nT),
        in_specs=[col(0), col(nh), col(2 * nh), col(3 * nh), st, col(0), vec, vec, vec],
        out_specs=[col(0)] * 4 + [vec] * 3,
        out_shape=[dcol] * 4 + [dvec] * 3,
        scratch_shapes=[pltpu.VMEM((HG_HEAD_DIM, HG_HEAD_DIM), F32)],
    )(proj, proj, proj, proj, states, do, l0, l1, nw)


def _rw_consts():
    C = RW_CHUNK
    r, c = _iota2((C, C), 0), _iota2((C, C), 1)
    tril = (c <= r).astype(F32)
    stril = (c < r).astype(F32)
    eye = (c == r).astype(F32)
    ones = jnp.ones((C, C), F32)
    ones_s = jnp.ones((LANES, C), F32)
    br, bc = _iota2((LANES, LANES), 0), _iota2((LANES, LANES), 1)
    blockdiag = ((br < RW_HEAD_DIM) == (bc < RW_HEAD_DIM)).astype(F32)
    lane = _iota2((1, LANES), 1)
    m0 = (lane < RW_HEAD_DIM).astype(F32)
    return tril, stril, eye, ones, ones_s, blockdiag, m0, 1.0 - m0


def _unit_lower_inverse(low, eye):
    x = eye + low
    p = low
    n = 2
    while n < RW_CHUNK:
        p = _dot(p, p)
        x = x + _dot(x, p)
        n *= 2
    return x


def _rw_chunk(consts, S, r, kx, v, lw, gd, w0, a0, k_k, k_a, r_k, ln_w, ln_b, w2p, a2p, g2):
    tril, stril, eye, ones, ones_s, blockdiag, m0, m1 = consts
    xw = w0 + _dot(jnp.tanh(lw), w2p)
    w = jnp.minimum(xw, 0.0) - jnp.log(1.0 + jnp.exp(-jnp.abs(xw))) - 0.5
    ld = -jnp.exp(w)
    a_s = _sigmoid(a0 + _dot(lw, a2p))
    g = _dot(_sigmoid(gd), g2)
    kk = kx * k_k
    kk = kk / jnp.maximum(jnp.sqrt(_dot(kk * kk, blockdiag)), L2_EPS)
    k2 = kx * (1.0 + (a_s - 1.0) * k_a)
    av = -kk
    bv = kk * a_s
    cum = _dot(tril, ld)
    tot = _dot(ones, ld)
    ecn = jnp.exp(-cum)
    a_t = av * jnp.exp(cum - ld)
    b_h = bv * ecn
    k_h = k2 * ecn
    r_t = r * jnp.exp(cum)
    rem = jnp.exp(tot - cum)
    heads = []
    for m in (m0, m1):
        a_m = a_t * m
        r_m = r_t * m
        heads.append((m,
                      _dot(a_m, k_h, NT) * stril,
                      _dot(r_m, b_h, NT) * tril,
                      _dot(r_m, k_h, NT) * tril,
                      _unit_lower_inverse(_dot(a_m, b_h, NT) * stril, eye)))
    rhs = _dot(a_t, S, NT)
    for m, lak, _, _, _ in heads:
        rhs = rhs + m * _dot(lak, v)
    p = 0.0
    for m, _, _, _, tinv in heads:
        p = p + m * _dot(tinv, rhs)
    y = _dot(r_t, S, NT)
    for m, _, mrb, mrk, _ in heads:
        y = y + m * (_dot(mrb, p) + _dot(mrk, v))
    S_new = (S * jnp.exp(_dot(ones_s, ld)) + _dot(p, bv * rem, TN) + _dot(v, k2 * rem, TN)) * blockdiag
    inv_n = 1.0 / RW_HEAD_DIM
    yc = y - _dot(y, blockdiag) * inv_n
    var = _dot(yc * yc, blockdiag) * inv_n
    yn = yc * lax.rsqrt(var + RW_GN_EPS) * ln_w + ln_b
    bonus = _dot(r * k2 * r_k, blockdiag) * v
    return S_new, (yn + bonus) * g


N_RW_VEC = 7
N_RW_MAT = 3


def _rw_specs(T, tb, rev):
    nT = T // tb
    tix = (lambda t: nT - 1 - t) if rev else (lambda t: t)
    np_ = RW_PAIRS
    col = lambda off: pl.BlockSpec((tb, LANES), lambda p, t: (tix(t), off + p))
    shared = lambda blk: pl.BlockSpec((tb, LANES), lambda p, t: (tix(t), blk))
    vec = pl.BlockSpec((1, LANES), lambda p, t: (0, p))
    mat = pl.BlockSpec((1, LANES, LANES), lambda p, t: (p, 0, 0))
    st = pl.BlockSpec((1, tb // RW_CHUNK, LANES, LANES), lambda p, t: (p, tix(t), 0, 0))
    ins = [col(0), col(np_), col(2 * np_), shared(3 * np_), shared(3 * np_ + 1)]
    return nT, col, vec, mat, st, ins


def _rw_fwd(rws, vecs, mats, name):
    T = rws.shape[0]
    tb = _row_tile(T, SCAN_ROWS)
    nsub = tb // RW_CHUNK
    nT, col, vec, mat, st, ins = _rw_specs(T, tb, False)

    def body(*refs):
        act = refs[:5]
        vrefs = refs[5:5 + N_RW_VEC]
        mrefs = refs[5 + N_RW_VEC:5 + N_RW_VEC + N_RW_MAT]
        o_ref, st_ref, s_ref = refs[-3:]

        @pl.when(pl.program_id(1) == 0)
        def _():
            s_ref[...] = jnp.zeros_like(s_ref)

        consts = _rw_consts()

        def step(j, carry):
            rows = pl.ds(pl.multiple_of(j * RW_CHUNK, RW_CHUNK), RW_CHUNK)
            S = s_ref[...]
            st_ref[0, j] = S
            S_new, out = _rw_chunk(consts, S, *[a[rows, :] for a in act], *[x[...] for x in vrefs],
                                   *[x[0] for x in mrefs])
            s_ref[...] = S_new
            o_ref[rows, :] = out
            return carry

        lax.fori_loop(0, nsub, step, 0)

    return pl.pallas_call(
        body, name=name, grid=(RW_PAIRS, nT),
        in_specs=ins + [vec] * N_RW_VEC + [mat] * N_RW_MAT,
        out_specs=[col(0), st],
        out_shape=[jax.ShapeDtypeStruct((T, RW_WIDTH), F32),
                   jax.ShapeDtypeStruct((RW_PAIRS, T // RW_CHUNK, LANES, LANES), F32)],
        scratch_shapes=[pltpu.VMEM((LANES, LANES), F32)],
    )(rws, rws, rws, rws, rws, *vecs, *mats)


def _rw_bwd(rws, states, do, vecs, mats, name):
    T = rws.shape[0]
    tb = _row_tile(T, SCAN_ROWS)
    nsub = tb // RW_CHUNK
    nT, col, vec, mat, st, ins = _rw_specs(T, tb, True)
    nin = 5 + 1 + 1 + N_RW_VEC + N_RW_MAT

    def body(*refs):
        act = refs[:5]
        st_ref, do_ref = refs[5], refs[6]
        vrefs = refs[7:7 + N_RW_VEC]
        mrefs = refs[7 + N_RW_VEC:nin]
        dr_ref, dk_ref, dv_ref, dlo_ref = refs[nin:nin + 4]
        dvec = refs[nin + 4:nin + 4 + N_RW_VEC]
        dmat = refs[nin + 4 + N_RW_VEC:nin + 4 + N_RW_VEC + N_RW_MAT]
        ds_ref = refs[-1]

        @pl.when(pl.program_id(1) == 0)
        def _():
            ds_ref[...] = jnp.zeros_like(ds_ref)
            for x in dvec + dmat:
                x[...] = jnp.zeros_like(x)

        consts = _rw_consts()

        def step(i, carry):
            j = nsub - 1 - i
            rows = pl.ds(pl.multiple_of(j * RW_CHUNK, RW_CHUNK), RW_CHUNK)
            args = (st_ref[0, j], *[a[rows, :] for a in act], *[x[...] for x in vrefs], *[x[0] for x in mrefs])
            _, vjp = jax.vjp(functools.partial(_rw_chunk, consts), *args)
            grads = vjp((ds_ref[...], do_ref[rows, :]))
            ds_ref[...] = grads[0]
            dr_ref[rows, :] = grads[1]
            dk_ref[rows, :] = grads[2]
            dv_ref[rows, :] = grads[3]
            dlo_ref[0, rows, 0:LANES] = grads[4]
            dlo_ref[0, rows, LANES:2 * LANES] = grads[5]
            for x, gx in zip(dvec, grads[6:6 + N_RW_VEC]):
                x[...] += gx
            for x, gx in zip(dmat, grads[6 + N_RW_VEC:]):
                x[0] += gx
            return carry

        lax.fori_loop(0, nsub, step, 0)

    dcol = jax.ShapeDtypeStruct((T, RW_WIDTH), F32)
    dlo_spec = pl.BlockSpec((1, tb, 2 * LANES), lambda p, t: (p, nT - 1 - t, 0))
    return pl.pallas_call(
        body, name=name, grid=(RW_PAIRS, nT),
        in_specs=ins + [st, col(0)] + [vec] * N_RW_VEC + [mat] * N_RW_MAT,
        out_specs=[col(0)] * 3 + [dlo_spec] + [vec] * N_RW_VEC + [mat] * N_RW_MAT,
        out_shape=[dcol] * 3 + [jax.ShapeDtypeStruct((RW_PAIRS, T, 2 * LANES), F32)]
        + [jax.ShapeDtypeStruct((1, RW_WIDTH), F32)] * N_RW_VEC
        + [jax.ShapeDtypeStruct((RW_PAIRS, LANES, LANES), F32)] * N_RW_MAT,
        scratch_shapes=[pltpu.VMEM((LANES, LANES), F32)],
    )(rws, rws, rws, rws, rws, states, do, *vecs, *mats)


def _my_index():
    return 4 * lax.axis_index("x") + 2 * lax.axis_index("y") + lax.axis_index("c")


def _peer(bits):
    pos = []
    for name, flip in zip(("x", "y", "c"), bits):
        i = lax.axis_index(name)
        pos.append(1 - i if flip else i)
    return tuple(pos)


def _peer_index(bits):
    x, y, c = _peer(bits)
    return 4 * x + 2 * y + c


def _all_gather(shards, name):
    n = len(shards)
    chips = [(1, 0, 0), (0, 1, 0), (1, 1, 0)]
    sib = (0, 0, 1)

    def body(*refs):
        ins, outs = refs[:n], refs[n:2 * n]
        send_sems, recv_sems, local_sems = refs[2 * n:]

        def rows(k, dev):
            r = ins[k].shape[0]
            return outs[k].at[pl.ds(dev * r, r), :]

        def copy(k, slot, block_dev, to_bits, src=None):
            return pltpu.make_async_remote_copy(
                src_ref=rows(k, block_dev) if src is None else src, dst_ref=rows(k, block_dev),
                send_sem=send_sems.at[k, slot], recv_sem=recv_sems.at[k, slot],
                device_id=_peer(to_bits), device_id_type=MESH_ID)

        me = _my_index()
        started = []
        for k in range(n):
            mine = pltpu.make_async_copy(ins[k], rows(k, me), local_sems.at[k])
            mine.start()
            started.append(mine)
        sends = []
        for k in range(n):
            first = [copy(k, 0, me, sib, src=ins[k])]
            first += [copy(k, 1 + j, me, chip, src=ins[k]) for j, chip in enumerate(chips)]
            for cp in first:
                cp.start()
            sends += first
        for k in range(n):
            for j, chip in enumerate(chips):
                copy(k, 1 + j, _peer_index(chip), chip).wait_recv()
                fwd = copy(k, 4 + j, _peer_index(chip), sib)
                fwd.start()
                sends.append(fwd)
        for k in range(n):
            copy(k, 0, _peer_index(sib), sib).wait_recv()
            for j, chip in enumerate(chips):
                both = (chip[0], chip[1], 1)
                copy(k, 4 + j, _peer_index(both), sib).wait_recv()
        for cp in sends:
            cp.wait_send()
        for cp in started:
            cp.wait()

    any_spec = pl.BlockSpec(memory_space=pl.ANY)
    return pl.pallas_call(
        body, name=name,
        in_specs=[any_spec] * n, out_specs=[any_spec] * n,
        out_shape=[jax.ShapeDtypeStruct((N_DEV * s.shape[0], s.shape[1]), s.dtype) for s in shards],
        scratch_shapes=[pltpu.SemaphoreType.DMA((n, 7)), pltpu.SemaphoreType.DMA((n, 7)),
                        pltpu.SemaphoreType.DMA((n,))],
    )(*shards)


def _exchange(partials, name):
    n = len(partials)
    flips = [(dx, dy, dc) for dx in (0, 1) for dy in (0, 1) for dc in (0, 1)][1:]

    def body(*refs):
        ins, outs = refs[:n], refs[n:2 * n]
        send_sems, recv_sems, local_sems = refs[2 * n:]
        me = _my_index()
        local = []
        for k in range(n):
            cp = pltpu.make_async_copy(ins[k].at[me], outs[k].at[me], local_sems.at[k])
            cp.start()
            local.append(cp)
        copies = []
        for k in range(n):
            for d, bits in enumerate(flips):
                cp = pltpu.make_async_remote_copy(
                    src_ref=ins[k].at[_peer_index(bits)], dst_ref=outs[k].at[me],
                    send_sem=send_sems.at[k, d], recv_sem=recv_sems.at[k, d],
                    device_id=_peer(bits), device_id_type=MESH_ID)
                cp.start()
                copies.append(cp)
        for cp in copies:
            cp.wait_recv()
        for cp in copies:
            cp.wait_send()
        for cp in local:
            cp.wait()

    any_spec = pl.BlockSpec(memory_space=pl.ANY)
    return pl.pallas_call(
        body, name=name,
        in_specs=[any_spec] * n, out_specs=[any_spec] * n,
        out_shape=[jax.ShapeDtypeStruct(p.shape, p.dtype) for p in partials],
        scratch_shapes=[pltpu.SemaphoreType.DMA((n, 7)), pltpu.SemaphoreType.DMA((n, 7)),
                        pltpu.SemaphoreType.DMA((n,))],
    )(*partials)


def _sum_slots(landed, name):
    _, R, C = landed.shape
    tb = _row_tile(R, 128)

    def body(l_ref, o_ref):
        acc = l_ref[0].astype(F32)
        for s in range(1, N_DEV):
            acc = acc + l_ref[s].astype(F32)
        o_ref[...] = acc

    return pl.pallas_call(
        body, name=name, grid=(R // tb,),
        in_specs=[pl.BlockSpec((N_DEV, tb, C), lambda i: (0, i, 0))],
        out_specs=pl.BlockSpec((tb, C), lambda i: (i, 0)),
        out_shape=jax.ShapeDtypeStruct((R, C), F32),
    )(landed)


def _pack_rows(flat_list, width=LANES):
    flat = jnp.concatenate([a.reshape(-1) for a in flat_list])
    n = flat.shape[0]
    rows = -(-n // width)
    rows = -(-rows // 8) * 8
    return jnp.pad(flat, (0, rows * width - n)).reshape(rows, width)


def _unpack(packed, shapes):
    flat = packed.reshape(-1)
    out, off = [], 0
    for s in shapes:
        n = 1
        for d in s:
            n *= d
        out.append(flat[off:off + n].reshape(s))
        off += n
    return out


def kernel(x, norm1_w, w_in, hg_lb_logits, hg_norm_w, rw_shift_mu, rw_w0, rw_w2, rw_a0, rw_a2, rw_g2, rw_k_k, rw_k_a, rw_r_k, rw_ln_w, rw_ln_b, w_out, norm2_w, w_up, conv_w, conv_b, w_down, final_norm_w, loss_target, m_norm1_w, m_w_in, m_hg_lb_logits, m_hg_norm_w, m_rw_shift_mu, m_rw_w0, m_rw_w2, m_rw_a0, m_rw_a2, m_rw_g2, m_rw_k_k, m_rw_k_a, m_rw_r_k, m_rw_ln_w, m_rw_ln_b, m_w_out, m_norm2_w, m_w_up, m_conv_w, m_conv_b, m_w_down, m_final_norm_w, v_norm1_w, v_w_in, v_hg_lb_logits, v_hg_norm_w, v_rw_shift_mu, v_rw_w0, v_rw_w2, v_rw_a0, v_rw_a2, v_rw_g2, v_rw_k_k, v_rw_k_a, v_rw_r_k, v_rw_ln_w, v_rw_ln_b, v_w_out, v_norm2_w, v_w_up, v_conv_w, v_conv_b, v_w_down, v_final_norm_w):
    weights = dict(norm1_w=norm1_w, w_in=w_in, hg_lb_logits=hg_lb_logits, hg_norm_w=hg_norm_w,
                   rw_shift_mu=rw_shift_mu, rw_w0=rw_w0, rw_w2=rw_w2, rw_a0=rw_a0, rw_a2=rw_a2, rw_g2=rw_g2,
                   rw_k_k=rw_k_k, rw_k_a=rw_k_a, rw_r_k=rw_r_k, rw_ln_w=rw_ln_w, rw_ln_b=rw_ln_b, w_out=w_out,
                   norm2_w=norm2_w, w_up=w_up, conv_w=conv_w, conv_b=conv_b, w_down=w_down,
                   final_norm_w=final_norm_w)
    m_in = dict(norm1_w=m_norm1_w, w_in=m_w_in, hg_lb_logits=m_hg_lb_logits, hg_norm_w=m_hg_norm_w,
                rw_shift_mu=m_rw_shift_mu, rw_w0=m_rw_w0, rw_w2=m_rw_w2, rw_a0=m_rw_a0, rw_a2=m_rw_a2,
                rw_g2=m_rw_g2, rw_k_k=m_rw_k_k, rw_k_a=m_rw_k_a, rw_r_k=m_rw_r_k, rw_ln_w=m_rw_ln_w,
                rw_ln_b=m_rw_ln_b, w_out=m_w_out, norm2_w=m_norm2_w, w_up=m_w_up, conv_w=m_conv_w,
                conv_b=m_conv_b, w_down=m_w_down, final_norm_w=m_final_norm_w)
    v_in = dict(norm1_w=v_norm1_w, w_in=v_w_in, hg_lb_logits=v_hg_lb_logits, hg_norm_w=v_hg_norm_w,
                rw_shift_mu=v_rw_shift_mu, rw_w0=v_rw_w0, rw_w2=v_rw_w2, rw_a0=v_rw_a0, rw_a2=v_rw_a2,
                rw_g2=v_rw_g2, rw_k_k=v_rw_k_k, rw_k_a=v_rw_k_a, rw_r_k=v_rw_r_k, rw_ln_w=v_rw_ln_w,
                rw_ln_b=v_rw_ln_b, w_out=v_w_out, norm2_w=v_norm2_w, w_up=v_w_up, conv_w=v_conv_w,
                conv_b=v_conv_b, w_down=v_w_down, final_norm_w=v_final_norm_w)
    names = list(weights)
    sharded_small = ["rw_w2", "rw_a2", "rw_g2", "conv_w"]
    sharded_big = ["w_in", "w_out", "w_up", "w_down"]
    replicated = [n for n in names if n not in sharded_small + sharded_big]

    xs = x[0]
    tgt = loss_target[0]
    T = xs.shape[0]

    small_shard = _pack_rows([weights[n] for n in sharded_small])
    g_win_t, g_wup_t, g_wout, g_wdown, g_small = _all_gather(
        [w_in[0].T.astype(BF16), w_up[0].T.astype(BF16), w_out[0].astype(BF16), w_down[0].astype(BF16),
         small_shard], "gather_weights")
    small_shapes = [weights[n].shape for n in sharded_small]
    per_dev = [_unpack(g_small.reshape(N_DEV, -1)[j], small_shapes) for j in range(N_DEV)]
    w2_full, a2_full, g2_full, convw_full = [jnp.concatenate([per_dev[j][i][0] for j in range(N_DEV)], axis=-1)
                                             for i in range(4)]
    zeros64 = jnp.zeros((RW_PAIRS, 64, LANES), F32)
    by_pair = lambda z: z.reshape(z.shape[0], RW_PAIRS, LANES).transpose(1, 0, 2)
    w2p = jnp.concatenate([by_pair(w2_full), zeros64], axis=1)
    a2p = jnp.concatenate([zeros64, by_pair(a2_full)], axis=1)
    g2p = by_pair(g2_full)

    l0, l1 = hg_lb_logits[0:1], hg_lb_logits[1:2]
    h1 = _rms_fwd(xs, norm1_w, "norm1")
    proj = _mm_nt(h1, g_win_t, "proj_in")
    o_hg, hg_states = _hg_fwd(proj, l0, l1, hg_norm_w, "hgrn2_fwd")
    rws = _shift_fwd(proj, rw_shift_mu, "token_shift")
    rw_vecs = [rw_w0, rw_a0, rw_k_k, rw_k_a, rw_r_k, rw_ln_w, rw_ln_b]
    rw_mats = [w2p, a2p, g2p]
    o_rw, rw_states = _rw_fwd(rws, rw_vecs, rw_mats, "rwkv7_fwd")
    o_mix = jnp.concatenate([o_hg, o_rw], axis=-1).astype(BF16)
    x1 = _mm_nn(o_mix, g_wout, xs, "proj_out")
    h2 = _rms_fwd(x1, norm2_w, "norm2")
    u = _mm_nt(h2, g_wup_t, "ffn_up")
    act = _ffn_act_fwd(u, convw_full, conv_b, "ffn_act")
    x2 = _mm_nn(act, g_wdown, x1, "ffn_down")
    loss_part, dx2, d_final_w = _loss_head(x2, final_norm_w.reshape(1, -1), tgt, "loss_head")

    d_wdown = _mm_tn(act, dx2, 1408,"ffn_down_dw")
    dact = _mm_nt(dx2, g_wdown, "ffn_down_dx", BF16)
    du_g, du_v, dcw_g, dcw_v, dcb_g, dcb_v = _ffn_act_bwd(u, dact, convw_full, conv_b, "ffn_act_bwd")
    du = jnp.concatenate([du_g, du_v], axis=-1)
    d_convw = jnp.concatenate([dcw_g, dcw_v], axis=-1)
    d_convb = jnp.concatenate([dcb_g, dcb_v], axis=-1)
    d_wup_t = _mm_tn(du, h2, 1408,"ffn_up_dw")
    dh2 = _mm_nn(du, g_wup_t, None, "ffn_up_dx")
    dx1, d_norm2 = _rms_bwd(dh2, x1, norm2_w, dx2, "norm2_bwd")
    d_wout = _mm_tn(o_mix, dx1, 512, "proj_out_dw")
    do = _mm_nt(dx1, g_wout, "proj_out_dx")
    do_hg, do_rw = do[:, :HG_WIDTH], do[:, HG_WIDTH:]
    dq, df, di, dg, d_l0, d_l1, d_hg_nw = _hg_bwd(proj, hg_states, do_hg, l0, l1, hg_norm_w, "hgrn2_bwd")
    rw_out = _rw_bwd(rws, rw_states, do_rw, rw_vecs, rw_mats, "rwkv7_bwd")
    d_r, d_k, d_v, d_lora = rw_out[:4]
    d_rw_vecs = rw_out[4:4 + N_RW_VEC]
    d_w2p, d_a2p, d_g2p = rw_out[4 + N_RW_VEC:]
    dp_parts, dmu_parts = [], []
    for i, z in enumerate((d_r, d_k, d_v)):
        dp, dmu = _shift_bwd([(z, ((T, LANES), lambda j: (0, j)), RW_WIDTH)], proj, rw_shift_mu, i * RW_WIDTH,
                             "token_shift_bwd_%d" % i)
        dp_parts.append(dp)
        dmu_parts.append(dmu)
    lora_parts = [(d_lora, ((1, T, LANES), functools.partial(lambda p, j: (p, 0, j), p)), 2 * LANES)
                  for p in range(RW_PAIRS)]
    dp, dmu = _shift_bwd(lora_parts, proj, rw_shift_mu, 3 * RW_WIDTH, "token_shift_bwd_lora")
    dp_parts.append(dp)
    dmu_parts.append(dmu)
    d_mu = jnp.concatenate(dmu_parts, axis=-1)
    dproj = jnp.concatenate([dq, df, di, dg] + dp_parts, axis=-1)
    d_win_t = _mm_tn(dproj, h1, 768, "proj_in_dw")
    dh1 = _mm_nn(dproj, g_win_t, None, "proj_in_dx")
    grad_x, d_norm1 = _rms_bwd(dh1, xs, norm1_w, dx1, "norm1_bwd")

    from_pairs = lambda z: z.transpose(1, 0, 2).reshape(z.shape[1], RW_WIDTH)
    d_w2 = from_pairs(d_w2p[:, :64])
    d_a2 = from_pairs(d_a2p[:, 64:])
    d_g2 = from_pairs(d_g2p)
    col_blocks = lambda z: z.reshape(z.shape[0], N_DEV, -1).transpose(1, 0, 2)
    small_part = jnp.stack([
        _pack_rows([col_blocks(d_w2)[j], col_blocks(d_a2)[j], col_blocks(d_g2)[j], col_blocks(d_convw)[j]])
        for j in range(N_DEV)])
    rep_grads = dict(norm1_w=d_norm1, hg_lb_logits=jnp.concatenate([d_l0, d_l1], axis=0), hg_norm_w=d_hg_nw,
                     rw_shift_mu=d_mu, rw_w0=d_rw_vecs[0], rw_a0=d_rw_vecs[1], rw_k_k=d_rw_vecs[2],
                     rw_k_a=d_rw_vecs[3], rw_r_k=d_rw_vecs[4], rw_ln_w=d_rw_vecs[5], rw_ln_b=d_rw_vecs[6],
                     norm2_w=d_norm2, conv_b=d_convb, final_norm_w=d_final_w)
    rep_pack = _pack_rows([loss_part] + [rep_grads[n] for n in replicated])
    rep_part = jnp.broadcast_to(rep_pack[None], (N_DEV,) + rep_pack.shape)
    blocks = lambda z: z.reshape(N_DEV, z.shape[0] // N_DEV, z.shape[1])
    landed = _exchange([blocks(d_win_t), blocks(d_wup_t), blocks(d_wout), blocks(d_wdown), small_part, rep_part],
                       "exchange_grads")
    sums = [_sum_slots(z, "sum_grads_%d" % i) for i, z in enumerate(landed)]
    g_small_sum = _unpack(sums[4], small_shapes)
    rep_sum = _unpack(sums[5], [(1, 1)] + [weights[n].shape for n in replicated])
    loss = rep_sum[0].reshape(())
    grads = dict(zip(replicated, rep_sum[1:]))
    grads.update(dict(zip(sharded_small, g_small_sum)))
    grads["w_in"] = sums[0].T[None]
    grads["w_up"] = sums[1].T[None]
    grads["w_out"] = sums[2][None]
    grads["w_down"] = sums[3][None]

    delta, new_m, new_v = {}, {}, {}
    for n in sharded_big:
        shp = weights[n].shape
        as2d = lambda z: z.reshape(shp[1], shp[2])
        d, nm, nv = _adamw(as2d(weights[n]), as2d(grads[n]), as2d(m_in[n]), as2d(v_in[n]), "adamw_" + n)
        delta[n], new_m[n], new_v[n] = d.reshape(shp), nm.reshape(shp), nv.reshape(shp)
    small_names = replicated + sharded_small
    packs = [_pack_rows([src[n] for n in small_names]) for src in (weights, grads, m_in, v_in)]
    outs = _adamw(*packs, "adamw_small")
    small_shapes_all = [weights[n].shape for n in small_names]
    for dst, packed in zip((delta, new_m, new_v), outs):
        dst.update(dict(zip(small_names, _unpack(packed, small_shapes_all))))

    return (loss, grad_x[None], *[grads[n] for n in names], *[delta[n] for n in names],
            *[new_m[n] for n in names], *[new_v[n] for n in names])
```

```python
import functools

import jax
import jax.numpy as jnp
from jax import lax
from jax.experimental import pallas as pl
from jax.experimental.pallas import tpu as pltpu

F32 = jnp.float32
BF16 = jnp.bfloat16
HIGHEST = lax.Precision.HIGHEST
SCAN_PRECISION = lax.Precision.HIGH
MESH_ID = pl.DeviceIdType.MESH

N_DEV = 8
D_MODEL = 1024
HG_WIDTH = 512
HG_HEAD_DIM = 128
HG_HEADS = 4
RW_WIDTH = 512
RW_PAIRS = 4
RW_HEAD_DIM = 64
HG_COLS = 2048
RW_COLS = 1792
D_FF = 2816
NORM_EPS = 1e-6
RW_GN_EPS = 64e-5
L2_EPS = 1e-12
ADAM_LR, ADAM_B1, ADAM_B2, ADAM_EPS, ADAM_WD, ADAM_STEP = 0.001, 0.9, 0.999, 1e-08, 0.01, 10

HG_CHUNK = 16
RW_CHUNK = 32
SCAN_ROWS = 256
LANES = 128

NN = ((1,), (0,))
NT = ((1,), (1,))
TN = ((0,), (0,))


def _dot(a, b, dims=NN, precision=SCAN_PRECISION):
    return lax.dot_general(a, b, (dims, ((), ())), precision=precision, preferred_element_type=F32)


def _iota2(shape, dim):
    return lax.broadcasted_iota(jnp.int32, shape, dim)


def _sigmoid(z):
    return 1.0 / (1.0 + jnp.exp(-z))


def _row_tile(n, want):
    t = min(n, want)
    while n % t:
        t //= 2
    return t


def _rms_fwd(x, w, name):
    T, D = x.shape
    tb = _row_tile(T, 512)

    def body(x_ref, w_ref, h_ref):
        xv = x_ref[...]
        r = lax.rsqrt(jnp.mean(xv * xv, axis=-1, keepdims=True) + NORM_EPS)
        h_ref[...] = (xv * r * w_ref[...]).astype(h_ref.dtype)

    return pl.pallas_call(
        body, name=name, grid=(T // tb,),
        in_specs=[pl.BlockSpec((tb, D), lambda i: (i, 0)), pl.BlockSpec((1, D), lambda i: (0, 0))],
        out_specs=pl.BlockSpec((tb, D), lambda i: (i, 0)),
        out_shape=jax.ShapeDtypeStruct((T, D), BF16),
    )(x, w)


def _rms_bwd(dh, x, w, dres, name):
    T, D = x.shape
    tb = _row_tile(T, 256)

    def body(dh_ref, x_ref, w_ref, dres_ref, dx_ref, dw_ref):
        @pl.when(pl.program_id(0) == 0)
        def _():
            dw_ref[...] = jnp.zeros_like(dw_ref)

        xv = x_ref[...]
        r = lax.rsqrt(jnp.mean(xv * xv, axis=-1, keepdims=True) + NORM_EPS)
        xn = xv * r
        dy = dh_ref[...].astype(F32)
        dxn = dy * w_ref[...]
        dx_ref[...] = dres_ref[...] + r * (dxn - xn * jnp.mean(dxn * xn, axis=-1, keepdims=True))
        dw_ref[...] += jnp.sum(dy * xn, axis=0, keepdims=True)

    row = pl.BlockSpec((tb, D), lambda i: (i, 0))
    vec = pl.BlockSpec((1, D), lambda i: (0, 0))
    return pl.pallas_call(
        body, name=name, grid=(T // tb,),
        in_specs=[row, row, vec, row], out_specs=[row, vec],
        out_shape=[jax.ShapeDtypeStruct((T, D), F32), jax.ShapeDtypeStruct((1, D), F32)],
    )(dh, x, w, dres)


def _mm_nt(a, bt, name, out_dtype=F32):
    T, K = a.shape
    N = bt.shape[0]
    tm = _row_tile(T, 256)

    def body(a_ref, b_ref, o_ref):
        o_ref[...] = _dot(a_ref[...].astype(BF16), b_ref[...].astype(BF16), NT, None).astype(o_ref.dtype)

    return pl.pallas_call(
        body, name=name, grid=(T // tm,),
        in_specs=[pl.BlockSpec((tm, K), lambda i: (i, 0)), pl.BlockSpec((N, K), lambda i: (0, 0))],
        out_specs=pl.BlockSpec((tm, N), lambda i: (i, 0)),
        out_shape=jax.ShapeDtypeStruct((T, N), out_dtype),
    )(a, bt)


def _mm_nn(a, b, res, name, out_dtype=F32):
    T, K = a.shape
    N = b.shape[1]
    tm = _row_tile(T, 256)

    def body(a_ref, b_ref, *rest):
        o_ref = rest[-1]
        acc = _dot(a_ref[...].astype(BF16), b_ref[...].astype(BF16), NN, None)
        if res is not None:
            acc = acc + rest[0][...]
        o_ref[...] = acc.astype(o_ref.dtype)

    in_specs = [pl.BlockSpec((tm, K), lambda i: (i, 0)), pl.BlockSpec((K, N), lambda i: (0, 0))]
    args = [a, b]
    if res is not None:
        in_specs.append(pl.BlockSpec((tm, N), lambda i: (i, 0)))
        args.append(res)
    return pl.pallas_call(
        body, name=name, grid=(T // tm,), in_specs=in_specs,
        out_specs=pl.BlockSpec((tm, N), lambda i: (i, 0)),
        out_shape=jax.ShapeDtypeStruct((T, N), out_dtype),
    )(*args)


def _mm_tn(a, b, tmm, name):
    T, M = a.shape
    N = b.shape[1]
    tk = _row_tile(T, 512)

    def body(a_ref, b_ref, o_ref):
        @pl.when(pl.program_id(1) == 0)
        def _():
            o_ref[...] = jnp.zeros_like(o_ref)

        o_ref[...] += _dot(a_ref[...].astype(BF16), b_ref[...].astype(BF16), TN, None)

    return pl.pallas_call(
        body, name=name, grid=(M // tmm, T // tk),
        in_specs=[pl.BlockSpec((tk, tmm), lambda m, k: (k, m)), pl.BlockSpec((tk, N), lambda m, k: (k, 0))],
        out_specs=pl.BlockSpec((tmm, N), lambda m, k: (m, 0)),
        out_shape=jax.ShapeDtypeStruct((M, N), F32),
    )(a, b)


def _shift_rows_down(z, n):
    rows = _iota2(z.shape, 0)
    return jnp.where(rows < n, 0.0, pltpu.roll(z, n, 0))


def _shift_rows_up(z, n):
    T = z.shape[0]
    rows = _iota2(z.shape, 0)
    return jnp.where(rows >= T - n, 0.0, pltpu.roll(z, T - n, 0))


def _shift_fwd(proj, mu, name):
    T = proj.shape[0]
    nblk = RW_COLS // LANES
    first = HG_COLS // LANES

    def body(p_ref, mu_ref, o_ref):
        p = p_ref[...]
        o_ref[...] = p + (_shift_rows_down(p, 1) - p) * mu_ref[...]

    return pl.pallas_call(
        body, name=name, grid=(nblk,),
        in_specs=[pl.BlockSpec((T, LANES), lambda j: (0, first + j)), pl.BlockSpec((1, LANES), lambda j: (0, j))],
        out_specs=pl.BlockSpec((T, LANES), lambda j: (0, j)),
        out_shape=jax.ShapeDtypeStruct((T, RW_COLS), F32),
    )(proj, mu)


def _shift_bwd(ds, proj, mu, col0, name):
    T, width = ds.shape
    nblk = width // LANES
    first = (HG_COLS + col0) // LANES
    mu0 = col0 // LANES

    def body(ds_ref, p_ref, mu_ref, dp_ref, dmu_ref):
        dsv = ds_ref[...]
        p = p_ref[...]
        m = mu_ref[...]
        dp_ref[...] = (dsv * (1.0 - m) + _shift_rows_up(dsv * m, 1)).astype(dp_ref.dtype)
        dmu_ref[...] = jnp.sum(dsv * (_shift_rows_down(p, 1) - p), axis=0, keepdims=True)

    return pl.pallas_call(
        body, name=name, grid=(nblk,),
        in_specs=[pl.BlockSpec((T, LANES), lambda j: (0, j)),
                  pl.BlockSpec((T, LANES), lambda j: (0, first + j)),
                  pl.BlockSpec((1, LANES), lambda j: (0, mu0 + j))],
        out_specs=[pl.BlockSpec((T, LANES), lambda j: (0, j)), pl.BlockSpec((1, LANES), lambda j: (0, j))],
        out_shape=[jax.ShapeDtypeStruct((T, width), BF16), jax.ShapeDtypeStruct((1, width), F32)],
    )(ds, proj, mu)


def _conv3(z, w_ref):
    return w_ref[0:1, :] * _shift_rows_down(z, 2) + w_ref[1:2, :] * _shift_rows_down(z, 1) + w_ref[2:3, :] * z


def _ffn_act_fwd(u, conv_w, conv_b, name):
    T = u.shape[0]
    nblk = D_FF // LANES

    def body(ug_ref, uv_ref, wg_ref, wv_ref, bg_ref, bv_ref, act_ref):
        gate = _conv3(ug_ref[...], wg_ref) + bg_ref[...]
        val = _conv3(uv_ref[...], wv_ref) + bv_ref[...]
        act_ref[...] = (gate * _sigmoid(gate) * val).astype(act_ref.dtype)

    col = lambda off: pl.BlockSpec((T, LANES), lambda j: (0, off + j))
    wsp = lambda off: pl.BlockSpec((3, LANES), lambda j: (0, off + j))
    bsp = lambda off: pl.BlockSpec((1, LANES), lambda j: (0, off + j))
    return pl.pallas_call(
        body, name=name, grid=(nblk,),
        in_specs=[col(0), col(nblk), wsp(0), wsp(nblk), bsp(0), bsp(nblk)],
        out_specs=pl.BlockSpec((T, LANES), lambda j: (0, j)),
        out_shape=jax.ShapeDtypeStruct((T, D_FF), BF16),
    )(u, u, conv_w, conv_w, conv_b, conv_b)


def _ffn_act_bwd(u, dact, conv_w, conv_b, name):
    T = u.shape[0]
    nblk = D_FF // LANES

    def conv_bwd(z, dzc, w_ref, du_ref, dw_ref, db_ref):
        du = w_ref[2:3, :] * dzc + w_ref[1:2, :] * _shift_rows_up(dzc, 1) + w_ref[0:1, :] * _shift_rows_up(dzc, 2)
        du_ref[...] = du.astype(du_ref.dtype)
        dw_ref[0:1, :] = jnp.sum(dzc * _shift_rows_down(z, 2), axis=0, keepdims=True)
        dw_ref[1:2, :] = jnp.sum(dzc * _shift_rows_down(z, 1), axis=0, keepdims=True)
        dw_ref[2:3, :] = jnp.sum(dzc * z, axis=0, keepdims=True)
        db_ref[...] = jnp.sum(dzc, axis=0, keepdims=True)

    def body(ug_ref, uv_ref, da_ref, wg_ref, wv_ref, bg_ref, bv_ref,
             dug_ref, duv_ref, dwg_ref, dwv_ref, dbg_ref, dbv_ref):
        ug, uv = ug_ref[...], uv_ref[...]
        gate = _conv3(ug, wg_ref) + bg_ref[...]
        val = _conv3(uv, wv_ref) + bv_ref[...]
        da = da_ref[...].astype(F32)
        sg = _sigmoid(gate)
        dgate = da * val * (sg * (1.0 + gate * (1.0 - sg)))
        dval = da * gate * sg
        conv_bwd(ug, dgate, wg_ref, dug_ref, dwg_ref, dbg_ref)
        conv_bwd(uv, dval, wv_ref, duv_ref, dwv_ref, dbv_ref)

    col = lambda off: pl.BlockSpec((T, LANES), lambda j: (0, off + j))
    wsp = lambda off: pl.BlockSpec((3, LANES), lambda j: (0, off + j))
    bsp = lambda off: pl.BlockSpec((1, LANES), lambda j: (0, off + j))
    half = lambda r, dt: jax.ShapeDtypeStruct((r, D_FF), dt)
    return pl.pallas_call(
        body, name=name, grid=(nblk,),
        in_specs=[col(0), col(nblk), col(0), wsp(0), wsp(nblk), bsp(0), bsp(nblk)],
        out_specs=[col(0), col(0), wsp(0), wsp(0), bsp(0), bsp(0)],
        out_shape=[half(T, BF16), half(T, BF16), half(3, F32), half(3, F32), half(1, F32), half(1, F32)],
    )(u, u, dact, conv_w, conv_w, conv_b, conv_b)


def _loss_head(x2, w, target, name):
    T, D = x2.shape
    tb = _row_tile(T, 256)

    def body(x_ref, w_ref, t_ref, loss_ref, dx_ref, dw_ref):
        @pl.when(pl.program_id(0) == 0)
        def _():
            loss_ref[...] = jnp.zeros_like(loss_ref)
            dw_ref[...] = jnp.zeros_like(dw_ref)

        xv = x_ref[...]
        r = lax.rsqrt(jnp.mean(xv * xv, axis=-1, keepdims=True) + NORM_EPS)
        xn = xv * r
        err = xn * w_ref[...] - t_ref[...]
        row_loss = jnp.sum(err * err, axis=-1, keepdims=True) * (0.5 / D)
        loss_ref[...] += jnp.sum(row_loss, axis=0, keepdims=True)
        dy = err * (1.0 / D)
        dxn = dy * w_ref[...]
        dx_ref[...] = r * (dxn - xn * jnp.mean(dxn * xn, axis=-1, keepdims=True))
        dw_ref[...] += jnp.sum(dy * xn, axis=0, keepdims=True)

    row = pl.BlockSpec((tb, D), lambda i: (i, 0))
    vec = pl.BlockSpec((1, D), lambda i: (0, 0))
    return pl.pallas_call(
        body, name=name, grid=(T // tb,),
        in_specs=[row, vec, row],
        out_specs=[pl.BlockSpec((1, 1), lambda i: (0, 0)), row, vec],
        out_shape=[jax.ShapeDtypeStruct((1, 1), F32), jax.ShapeDtypeStruct((T, D), F32),
                   jax.ShapeDtypeStruct((1, D), F32)],
    )(x2, w, target)


def _adamw(w, g, m, v, name):
    R, C = w.shape
    tb = _row_tile(R, 256) if R % 8 == 0 else R

    def body(w_ref, g_ref, m_ref, v_ref, d_ref, nm_ref, nv_ref):
        gv = g_ref[...]
        nm = ADAM_B1 * m_ref[...] + (1.0 - ADAM_B1) * gv
        nv = ADAM_B2 * v_ref[...] + (1.0 - ADAM_B2) * (gv * gv)
        m_hat = nm / (1.0 - ADAM_B1 ** ADAM_STEP)
        v_hat = nv / (1.0 - ADAM_B2 ** ADAM_STEP)
        d_ref[...] = -ADAM_LR * (m_hat / (jnp.sqrt(v_hat) + ADAM_EPS) + ADAM_WD * w_ref[...])
        nm_ref[...] = nm
        nv_ref[...] = nv

    blk = pl.BlockSpec((tb, C), lambda i: (i, 0))
    sd = jax.ShapeDtypeStruct((R, C), F32)
    return pl.pallas_call(
        body, name=name, grid=(R // tb,), in_specs=[blk] * 4, out_specs=[blk] * 3, out_shape=[sd] * 3,
    )(w, g, m, v)


def _chunk_masks(rows, chunk):
    shift = chunk.bit_length() - 1
    i, j = _iota2((rows, rows), 0), _iota2((rows, rows), 1)
    same = jnp.right_shift(i, shift) == jnp.right_shift(j, shift)
    return same.astype(F32), (same & (j <= i)).astype(F32), (same & (j < i)).astype(F32)


def _head_lanes(h):
    return slice(h * LANES, (h + 1) * LANES)


def _chunk_rows(c, chunk):
    return pl.ds(pl.multiple_of(c * chunk, chunk), chunk)


def _hg_consts(rows):
    same, tril, _ = _chunk_masks(rows, HG_CHUNK)
    shift = HG_CHUNK.bit_length() - 1
    i, j = _iota2((rows, rows), 0), _iota2((rows, rows), 1)
    mid_row = jnp.left_shift(jnp.right_shift(i, shift), shift) + (HG_CHUNK // 2 - 1)
    return same, tril, (j == mid_row).astype(F32)


def _hg_prep(consts, qr, fr, ir, l0, l1):
    same, tril, sel_mid = consts
    lb = _sigmoid(l0 - l1)
    f = lb + (1.0 - lb) * _sigmoid(fr)
    q = qr * _sigmoid(qr) * (HG_HEAD_DIM ** -0.5)
    k = 1.0 - f
    g = jnp.log(f)
    a = _dot(tril, g, NN, HIGHEST)
    tot = _dot(same, g, NN, HIGHEST)
    mid = _dot(sel_mid, a, NN, HIGHEST)
    att = _dot(q * jnp.exp(a - mid), k * jnp.exp(mid - a), NT) * tril
    return q * jnp.exp(a), _dot(att, ir), k * jnp.exp(tot - a), jnp.exp(tot)


def _hg_post(o, gr, nw):
    on = o * lax.rsqrt(jnp.mean(o * o, axis=-1, keepdims=True) + NORM_EPS)
    return on * nw * (gr * _sigmoid(gr))


def _hg_specs(T, tb, rev):
    nT = T // tb
    tix = (lambda t: nT - 1 - t) if rev else (lambda t: t)
    col = lambda blk: pl.BlockSpec((tb, HG_WIDTH), lambda t: (tix(t), blk))
    vec = pl.BlockSpec((1, HG_WIDTH), lambda t: (0, 0))
    st = pl.BlockSpec((HG_HEADS, tb // HG_CHUNK, HG_HEAD_DIM, HG_HEAD_DIM), lambda t: (0, tix(t), 0, 0))
    return nT, col, vec, st


def _hg_fwd(proj, l0, l1, nw, name):
    T = proj.shape[0]
    tb = _row_tile(T, SCAN_ROWS)
    nsub = tb // HG_CHUNK
    nT, col, vec, st = _hg_specs(T, tb, False)

    def body(q_ref, f_ref, i_ref, g_ref, l0_ref, l1_ref, nw_ref, o_ref, st_ref, s_ref, qe_ref, kd_ref, dec_ref):
        @pl.when(pl.program_id(0) == 0)
        def _():
            s_ref[...] = jnp.zeros_like(s_ref)

        consts = _hg_consts(tb)
        for h in range(HG_HEADS):
            ln = _head_lanes(h)
            qe, o_intra, kd, dec = _hg_prep(consts, q_ref[:, ln], f_ref[:, ln], i_ref[:, ln],
                                            l0_ref[:, ln], l1_ref[:, ln])
            qe_ref[h], kd_ref[h], dec_ref[h] = qe, kd, dec
            o_ref[:, ln] = o_intra

        def step(c, carry):
            rows = _chunk_rows(c, HG_CHUNK)
            for h in range(HG_HEADS):
                ln = _head_lanes(h)
                S = s_ref[h]
                st_ref[h, c] = S
                o_ref[rows, ln] += _dot(qe_ref[h, rows, :], S, NT)
                s_ref[h] = S * dec_ref[h, pl.ds(c * HG_CHUNK, 1), :] + _dot(i_ref[rows, ln], kd_ref[h, rows, :], TN)
            return carry

        lax.fori_loop(0, nsub, step, 0)
        for h in range(HG_HEADS):
            ln = _head_lanes(h)
            o_ref[:, ln] = _hg_post(o_ref[:, ln], g_ref[:, ln], nw_ref[:, ln])

    blk = pltpu.VMEM((HG_HEADS, tb, LANES), F32)
    return pl.pallas_call(
        body, name=name, grid=(nT,),
        in_specs=[col(0), col(1), col(2), col(3), vec, vec, vec],
        out_specs=[col(0), st],
        out_shape=[jax.ShapeDtypeStruct((T, HG_WIDTH), F32),
                   jax.ShapeDtypeStruct((HG_HEADS, T // HG_CHUNK, HG_HEAD_DIM, HG_HEAD_DIM), F32)],
        scratch_shapes=[pltpu.VMEM((HG_HEADS, HG_HEAD_DIM, HG_HEAD_DIM), F32), blk, blk, blk],
    )(proj, proj, proj, proj, l0, l1, nw)


def _hg_bwd(proj, states, do, do_blk, l0, l1, nw, name):
    T = proj.shape[0]
    tb = _row_tile(T, SCAN_ROWS)
    nsub = tb // HG_CHUNK
    nT, col, vec, st = _hg_specs(T, tb, True)

    def body(q_ref, f_ref, i_ref, g_ref, st_ref, do_ref, l0_ref, l1_ref, nw_ref,
             dq_ref, df_ref, di_ref, dg_ref, dl0_ref, dl1_ref, dnw_ref,
             ds_ref, qe_ref, kd_ref, dec_ref, o_ref, dqe_ref, dkd_ref, ddec_ref, dis_ref):
        @pl.when(pl.program_id(0) == 0)
        def _():
            ds_ref[...] = jnp.zeros_like(ds_ref)
            dl0_ref[...] = jnp.zeros_like(dl0_ref)
            dl1_ref[...] = jnp.zeros_like(dl1_ref)
            dnw_ref[...] = jnp.zeros_like(dnw_ref)

        consts = _hg_consts(tb)
        prep_vjps = []
        for h in range(HG_HEADS):
            ln = _head_lanes(h)
            (qe, o_intra, kd, dec), vjp = jax.vjp(
                functools.partial(_hg_prep, consts), q_ref[:, ln], f_ref[:, ln], i_ref[:, ln],
                l0_ref[:, ln], l1_ref[:, ln])
            prep_vjps.append(vjp)
            qe_ref[h], kd_ref[h], dec_ref[h], o_ref[h] = qe, kd, dec, o_intra

        def redo(c, carry):
            rows = _chunk_rows(c, HG_CHUNK)
            for h in range(HG_HEADS):
                o_ref[h, rows, :] += _dot(qe_ref[h, rows, :], st_ref[h, c], NT)
            return carry

        lax.fori_loop(0, nsub, redo, 0)
        for h in range(HG_HEADS):
            ln = _head_lanes(h)
            _, vjp = jax.vjp(_hg_post, o_ref[h], g_ref[:, ln], nw_ref[:, ln])
            d_o, dgr, dnw = vjp(do_ref[:, ln])
            o_ref[h] = d_o
            dg_ref[:, ln] = dgr.astype(dg_ref.dtype)
            dnw_ref[:, ln] += dnw
        ddec_ref[...] = jnp.zeros_like(ddec_ref)

        def step(i, carry):
            c = nsub - 1 - i
            rows = _chunk_rows(c, HG_CHUNK)
            row0 = pl.ds(c * HG_CHUNK, 1)
            for h in range(HG_HEADS):
                ln = _head_lanes(h)
                G = ds_ref[h]
                S = st_ref[h, c]
                d_o = o_ref[h, rows, :]
                dqe_ref[h, rows, :] = _dot(d_o, S)
                dkd_ref[h, rows, :] = _dot(i_ref[rows, ln], G)
                dis_ref[h, rows, :] = _dot(kd_ref[h, rows, :], G, NT)
                ddec_ref[h, row0, :] = jnp.sum(S * G, axis=0, keepdims=True)
                ds_ref[h] = G * dec_ref[h, row0, :] + _dot(d_o, qe_ref[h, rows, :], TN)
            return carry

        lax.fori_loop(0, nsub, step, 0)
        for h in range(HG_HEADS):
            ln = _head_lanes(h)
            dq, df, di, dl0, dl1 = prep_vjps[h]((dqe_ref[h], o_ref[h], dkd_ref[h], ddec_ref[h]))
            dq_ref[:, ln] = dq.astype(dq_ref.dtype)
            df_ref[:, ln] = df.astype(df_ref.dtype)
            di_ref[:, ln] = (di + dis_ref[h]).astype(di_ref.dtype)
            dl0_ref[:, ln] += dl0
            dl1_ref[:, ln] += dl1

    dcol = jax.ShapeDtypeStruct((T, HG_WIDTH), BF16)
    dvec = jax.ShapeDtypeStruct((1, HG_WIDTH), F32)
    blk = pltpu.VMEM((HG_HEADS, tb, LANES), F32)
    return pl.pallas_call(
        body, name=name, grid=(nT,),
        in_specs=[col(0), col(1), col(2), col(3), st, col(do_blk), vec, vec, vec],
        out_specs=[col(0)] * 4 + [vec] * 3,
        out_shape=[dcol] * 4 + [dvec] * 3,
        scratch_shapes=[pltpu.VMEM((HG_HEADS, HG_HEAD_DIM, HG_HEAD_DIM), F32)] + [blk] * 8,
    )(proj, proj, proj, proj, states, do, l0, l1, nw)


def _rw_consts(rows):
    same, tril, stril = _chunk_masks(rows, RW_CHUNK)
    eye = (_iota2((rows, rows), 0) == _iota2((rows, rows), 1)).astype(F32)
    br, bc = _iota2((LANES, LANES), 0), _iota2((LANES, LANES), 1)
    blockdiag = ((br < RW_HEAD_DIM) == (bc < RW_HEAD_DIM)).astype(F32)
    m0 = (_iota2((1, LANES), 1) < RW_HEAD_DIM).astype(F32)
    return same, tril, stril, eye, blockdiag, m0, 1.0 - m0


def _unit_lower_inverse(low, eye):
    x = eye + low
    p = low
    n = 2
    while n < RW_CHUNK:
        p = _dot(p, p)
        x = x + _dot(x, p)
        n *= 2
    return x


N_PREP_OUT = 9


def _rw_prep(consts, r, kx, v, lw, gd, w0, a0, k_k, k_a, w2p, a2p, g2):
    same, tril, stril, eye, blockdiag, m0, m1 = consts
    xw = w0 + _dot(jnp.tanh(lw), w2p)
    w = jnp.minimum(xw, 0.0) - jnp.log(1.0 + jnp.exp(-jnp.abs(xw))) - 0.5
    ld = -jnp.exp(w)
    a_s = _sigmoid(a0 + _dot(lw, a2p))
    g = _dot(_sigmoid(gd), g2)
    kk = kx * k_k
    kk = kk / jnp.maximum(jnp.sqrt(_dot(kk * kk, blockdiag)), L2_EPS)
    k2 = kx * (1.0 + (a_s - 1.0) * k_a)
    bv = kk * a_s
    cum = _dot(tril, ld, NN, HIGHEST)
    tot = _dot(same, ld, NN, HIGHEST)
    ecn = jnp.exp(-cum)
    a_t = -kk * jnp.exp(cum - ld)
    b_h = bv * ecn
    k_h = k2 * ecn
    r_t = r * jnp.exp(cum)
    rem = jnp.exp(tot - cum)
    W = U = Q = Y0 = 0.0
    for m in (m0, m1):
        a_m = a_t * m
        r_m = r_t * m
        lak = _dot(a_m, k_h, NT) * stril
        mrb = _dot(r_m, b_h, NT) * tril
        mrk = _dot(r_m, k_h, NT) * tril
        tinv = _unit_lower_inverse(_dot(a_m, b_h, NT) * stril, eye)
        w_m = _dot(tinv, a_m)
        u_m = m * _dot(tinv, _dot(lak, v))
        W = W + w_m
        U = U + u_m
        Q = Q + r_m + _dot(mrb, w_m)
        Y0 = Y0 + _dot(mrb, u_m) + m * _dot(mrk, v)
    return W, U, Q, Y0, bv * rem, k2 * rem, jnp.exp(tot), k2, g


def _rw_post(blockdiag, y, r, v, k2, g, r_k, ln_w, ln_b):
    inv_n = 1.0 / RW_HEAD_DIM
    yc = y - _dot(y, blockdiag) * inv_n
    var = _dot(yc * yc, blockdiag) * inv_n
    yn = yc * lax.rsqrt(var + RW_GN_EPS) * ln_w + ln_b
    bonus = _dot(r * k2 * r_k, blockdiag) * v
    return (yn + bonus) * g


N_RW_VEC = 7
N_RW_MAT = 3


def _rw_specs(T, tb, rev):
    nT = T // tb
    tix = (lambda t: nT - 1 - t) if rev else (lambda t: t)
    wide = lambda blk: pl.BlockSpec((tb, RW_WIDTH), lambda t: (tix(t), blk))
    narrow = lambda blk: pl.BlockSpec((tb, LANES), lambda t: (tix(t), blk))
    vec = pl.BlockSpec((1, RW_WIDTH), lambda t: (0, 0))
    mat = pl.BlockSpec((RW_PAIRS, LANES, LANES), lambda t: (0, 0, 0))
    st = pl.BlockSpec((RW_PAIRS, tb // RW_CHUNK, LANES, LANES), lambda t: (0, tix(t), 0, 0))
    lora0 = 3 * RW_WIDTH // LANES
    ins = [wide(0), wide(1), wide(2), narrow(lora0), narrow(lora0 + 1)]
    return nT, wide, vec, mat, st, ins


def _rw_prep_args(p, r_ref, k_ref, v_ref, lw_ref, gd_ref, vrefs, mrefs):
    ln = _head_lanes(p)
    w0, a0, k_k, k_a = [x[:, ln] for x in vrefs[:4]]
    return (r_ref[:, ln], k_ref[:, ln], v_ref[:, ln], lw_ref[...], gd_ref[...], w0, a0, k_k, k_a,
            *[x[p] for x in mrefs])


def _rw_fwd(rws, vecs, mats, name):
    T = rws.shape[0]
    tb = _row_tile(T, SCAN_ROWS)
    nsub = tb // RW_CHUNK
    nT, wide, vec, mat, st, ins = _rw_specs(T, tb, False)

    def body(*refs):
        r_ref, k_ref, v_ref, lw_ref, gd_ref = refs[:5]
        vrefs = refs[5:5 + N_RW_VEC]
        mrefs = refs[5 + N_RW_VEC:5 + N_RW_VEC + N_RW_MAT]
        o_ref, st_ref, s_ref, pre_ref, y_ref = refs[-5:]

        @pl.when(pl.program_id(0) == 0)
        def _():
            s_ref[...] = jnp.zeros_like(s_ref)

        consts = _rw_consts(tb)
        blockdiag = consts[4]
        for p in range(RW_PAIRS):
            outs = _rw_prep(consts, *_rw_prep_args(p, r_ref, k_ref, v_ref, lw_ref, gd_ref, vrefs, mrefs))
            for i, z in enumerate(outs):
                pre_ref[i, p] = z

        def step(c, carry):
            rows = _chunk_rows(c, RW_CHUNK)
            for p in range(RW_PAIRS):
                ln = _head_lanes(p)
                S = s_ref[p]
                st_ref[p, c] = S
                P = _dot(pre_ref[0, p, rows, :], S, NT) + pre_ref[1, p, rows, :]
                y_ref[p, rows, :] = _dot(pre_ref[2, p, rows, :], S, NT) + pre_ref[3, p, rows, :]
                s_ref[p] = (S * pre_ref[6, p, pl.ds(c * RW_CHUNK, 1), :] + _dot(P, pre_ref[4, p, rows, :], TN)
                            + _dot(v_ref[rows, ln], pre_ref[5, p, rows, :], TN)) * blockdiag
            return carry

        lax.fori_loop(0, nsub, step, 0)
        for p in range(RW_PAIRS):
            ln = _head_lanes(p)
            r_k, ln_w, ln_b = [x[:, ln] for x in vrefs[4:]]
            o_ref[:, ln] = _rw_post(blockdiag, y_ref[p], r_ref[:, ln], v_ref[:, ln], pre_ref[7, p], pre_ref[8, p],
                                    r_k, ln_w, ln_b)

    return pl.pallas_call(
        body, name=name, grid=(nT,),
        in_specs=ins + [vec] * N_RW_VEC + [mat] * N_RW_MAT,
        out_specs=[wide(0), st],
        out_shape=[jax.ShapeDtypeStruct((T, RW_WIDTH), F32),
                   jax.ShapeDtypeStruct((RW_PAIRS, T // RW_CHUNK, LANES, LANES), F32)],
        scratch_shapes=[pltpu.VMEM((RW_PAIRS, LANES, LANES), F32),
                        pltpu.VMEM((N_PREP_OUT, RW_PAIRS, tb, LANES), F32),
                        pltpu.VMEM((RW_PAIRS, tb, LANES), F32)],
    )(rws, rws, rws, rws, rws, *vecs, *mats)


def _rw_bwd(rws, states, do, do_blk, vecs, mats, name):
    T = rws.shape[0]
    tb = _row_tile(T, SCAN_ROWS)
    nsub = tb // RW_CHUNK
    nT, wide, vec, mat, st, ins = _rw_specs(T, tb, True)
    nin = 5 + 1 + 1 + N_RW_VEC + N_RW_MAT

    def body(*refs):
        r_ref, k_ref, v_ref, lw_ref, gd_ref = refs[:5]
        st_ref, do_ref = refs[5], refs[6]
        vrefs = refs[7:7 + N_RW_VEC]
        mrefs = refs[7 + N_RW_VEC:nin]
        dr_ref, dk_ref, dv_ref, dlo_ref = refs[nin:nin + 4]
        dvec = refs[nin + 4:nin + 4 + N_RW_VEC]
        dmat = refs[nin + 4 + N_RW_VEC:nin + 4 + N_RW_VEC + N_RW_MAT]
        ds_ref, pre_ref, y_ref, p_ref, dpre_ref, dvs_ref = refs[-6:]

        @pl.when(pl.program_id(0) == 0)
        def _():
            ds_ref[...] = jnp.zeros_like(ds_ref)
            for x in dvec + dmat:
                x[...] = jnp.zeros_like(x)

        consts = _rw_consts(tb)
        blockdiag = consts[4]
        dlw, dgd = 0.0, 0.0
        for p in range(RW_PAIRS):
            ln = _head_lanes(p)
            outs, prep_vjp = jax.vjp(functools.partial(_rw_prep, consts),
                                     *_rw_prep_args(p, r_ref, k_ref, v_ref, lw_ref, gd_ref, vrefs, mrefs))
            for i, z in enumerate(outs):
                pre_ref[i] = z

            def redo(c, carry, p=p):
                rows = _chunk_rows(c, RW_CHUNK)
                S = st_ref[p, c]
                p_ref[rows, :] = _dot(pre_ref[0, rows, :], S, NT) + pre_ref[1, rows, :]
                y_ref[rows, :] = _dot(pre_ref[2, rows, :], S, NT) + pre_ref[3, rows, :]
                return carry

            lax.fori_loop(0, nsub, redo, 0)
            r_k, ln_w, ln_b = [x[:, ln] for x in vrefs[4:]]
            _, post_vjp = jax.vjp(functools.partial(_rw_post, blockdiag), y_ref[...], r_ref[:, ln], v_ref[:, ln],
                                  pre_ref[7], pre_ref[8], r_k, ln_w, ln_b)
            dy, dr2, dv2, dk2, dg, dr_k, dln_w, dln_b = post_vjp(do_ref[:, ln])
            dpre_ref[3] = dy
            dpre_ref[7] = dk2
            dpre_ref[8] = dg
            dvs_ref[...] = dv2
            for x, gx in zip(dvec[4:], (dr_k, dln_w, dln_b)):
                x[:, ln] += gx
            dpre_ref[6] = jnp.zeros_like(dpre_ref[6])

            def step(i, carry, p=p, ln=ln):
                c = nsub - 1 - i
                rows = _chunk_rows(c, RW_CHUNK)
                row0 = pl.ds(c * RW_CHUNK, 1)
                G = ds_ref[p] * blockdiag
                S = st_ref[p, c]
                dyc = dpre_ref[3, rows, :]
                dP = _dot(pre_ref[4, rows, :], G, NT)
                dpre_ref[0, rows, :] = _dot(dP, S)
                dpre_ref[1, rows, :] = dP
                dpre_ref[2, rows, :] = _dot(dyc, S)
                dpre_ref[4, rows, :] = _dot(p_ref[rows, :], G)
                dpre_ref[5, rows, :] = _dot(v_ref[rows, ln], G)
                dpre_ref[6, row0, :] = jnp.sum(S * G, axis=0, keepdims=True)
                dvs_ref[rows, :] += _dot(pre_ref[5, rows, :], G, NT)
                ds_ref[p] = (G * pre_ref[6, row0, :] + _dot(dP, pre_ref[0, rows, :], TN)
                             + _dot(dyc, pre_ref[2, rows, :], TN))
                return carry

            lax.fori_loop(0, nsub, step, 0)
            grads = prep_vjp(tuple(dpre_ref[i] for i in range(N_PREP_OUT)))
            dr_ref[:, ln] = grads[0] + dr2
            dk_ref[:, ln] = grads[1]
            dv_ref[:, ln] = grads[2] + dvs_ref[...]
            dlw = dlw + grads[3]
            dgd = dgd + grads[4]
            for x, gx in zip(dvec[:4], grads[5:9]):
                x[:, ln] += gx
            for x, gx in zip(dmat, grads[9:]):
                x[p] += gx
        dlo_ref[:, 0:LANES] = dlw
        dlo_ref[:, LANES:2 * LANES] = dgd

    dcol = jax.ShapeDtypeStruct((T, RW_WIDTH), F32)
    dlo_spec = pl.BlockSpec((tb, 2 * LANES), lambda t: (nT - 1 - t, 0))
    blk = pltpu.VMEM((tb, LANES), F32)
    many = pltpu.VMEM((N_PREP_OUT, tb, LANES), F32)
    return pl.pallas_call(
        body, name=name, grid=(nT,),
        in_specs=ins + [st, wide(do_blk)] + [vec] * N_RW_VEC + [mat] * N_RW_MAT,
        out_specs=[wide(0)] * 3 + [dlo_spec] + [vec] * N_RW_VEC + [mat] * N_RW_MAT,
        out_shape=[dcol] * 3 + [jax.ShapeDtypeStruct((T, 2 * LANES), F32)]
        + [jax.ShapeDtypeStruct((1, RW_WIDTH), F32)] * N_RW_VEC
        + [jax.ShapeDtypeStruct((RW_PAIRS, LANES, LANES), F32)] * N_RW_MAT,
        scratch_shapes=[pltpu.VMEM((RW_PAIRS, LANES, LANES), F32), many, blk, blk, many, blk],
    )(rws, rws, rws, rws, rws, states, do, *vecs, *mats)


def _my_index():
    return 4 * lax.axis_index("x") + 2 * lax.axis_index("y") + lax.axis_index("c")


def _peer(bits):
    pos = []
    for name, flip in zip(("x", "y", "c"), bits):
        i = lax.axis_index(name)
        pos.append(1 - i if flip else i)
    return tuple(pos)


def _peer_index(bits):
    x, y, c = _peer(bits)
    return 4 * x + 2 * y + c


def _all_gather(shards, name):
    n = len(shards)
    chips = [(1, 0, 0), (0, 1, 0), (1, 1, 0)]
    sib = (0, 0, 1)

    def body(*refs):
        ins, outs = refs[:n], refs[n:2 * n]
        send_sems, recv_sems, local_sems = refs[2 * n:]

        def rows(k, dev):
            r = ins[k].shape[0]
            return outs[k].at[pl.ds(dev * r, r), :]

        def copy(k, slot, block_dev, to_bits, src=None):
            return pltpu.make_async_remote_copy(
                src_ref=rows(k, block_dev) if src is None else src, dst_ref=rows(k, block_dev),
                send_sem=send_sems.at[k, slot], recv_sem=recv_sems.at[k, slot],
                device_id=_peer(to_bits), device_id_type=MESH_ID)

        me = _my_index()
        started = []
        for k in range(n):
            mine = pltpu.make_async_copy(ins[k], rows(k, me), local_sems.at[k])
            mine.start()
            started.append(mine)
        sends = []
        for k in range(n):
            first = [copy(k, 0, me, sib, src=ins[k])]
            first += [copy(k, 1 + j, me, chip, src=ins[k]) for j, chip in enumerate(chips)]
            for cp in first:
                cp.start()
            sends += first
        for k in range(n):
            for j, chip in enumerate(chips):
                copy(k, 1 + j, _peer_index(chip), chip).wait_recv()
                fwd = copy(k, 4 + j, _peer_index(chip), sib)
                fwd.start()
                sends.append(fwd)
        for k in range(n):
            copy(k, 0, _peer_index(sib), sib).wait_recv()
            for j, chip in enumerate(chips):
                both = (chip[0], chip[1], 1)
                copy(k, 4 + j, _peer_index(both), sib).wait_recv()
        for cp in sends:
            cp.wait_send()
        for cp in started:
            cp.wait()

    any_spec = pl.BlockSpec(memory_space=pl.ANY)
    return pl.pallas_call(
        body, name=name,
        in_specs=[any_spec] * n, out_specs=[any_spec] * n,
        out_shape=[jax.ShapeDtypeStruct((N_DEV * s.shape[0], s.shape[1]), s.dtype) for s in shards],
        scratch_shapes=[pltpu.SemaphoreType.DMA((n, 7)), pltpu.SemaphoreType.DMA((n, 7)),
                        pltpu.SemaphoreType.DMA((n,))],
    )(*shards)


def _exchange(partials, name):
    n = len(partials)
    flips = [(dx, dy, dc) for dx in (0, 1) for dy in (0, 1) for dc in (0, 1)][1:]

    def body(*refs):
        ins, outs = refs[:n], refs[n:2 * n]
        send_sems, recv_sems, local_sems = refs[2 * n:]
        me = _my_index()
        local = []
        for k in range(n):
            cp = pltpu.make_async_copy(ins[k].at[me], outs[k].at[me], local_sems.at[k])
            cp.start()
            local.append(cp)
        copies = []
        for k in range(n):
            for d, bits in enumerate(flips):
                cp = pltpu.make_async_remote_copy(
                    src_ref=ins[k].at[_peer_index(bits)], dst_ref=outs[k].at[me],
                    send_sem=send_sems.at[k, d], recv_sem=recv_sems.at[k, d],
                    device_id=_peer(bits), device_id_type=MESH_ID)
                cp.start()
                copies.append(cp)
        for cp in copies:
            cp.wait_recv()
        for cp in copies:
            cp.wait_send()
        for cp in local:
            cp.wait()

    any_spec = pl.BlockSpec(memory_space=pl.ANY)
    return pl.pallas_call(
        body, name=name,
        in_specs=[any_spec] * n, out_specs=[any_spec] * n,
        out_shape=[jax.ShapeDtypeStruct(p.shape, p.dtype) for p in partials],
        scratch_shapes=[pltpu.SemaphoreType.DMA((n, 7)), pltpu.SemaphoreType.DMA((n, 7)),
                        pltpu.SemaphoreType.DMA((n,))],
    )(*partials)


def _sum_slots(landed, name):
    _, R, C = landed.shape
    tb = _row_tile(R, 128)

    def body(l_ref, o_ref):
        acc = l_ref[0].astype(F32)
        for s in range(1, N_DEV):
            acc = acc + l_ref[s].astype(F32)
        o_ref[...] = acc

    return pl.pallas_call(
        body, name=name, grid=(R // tb,),
        in_specs=[pl.BlockSpec((N_DEV, tb, C), lambda i: (0, i, 0))],
        out_specs=pl.BlockSpec((tb, C), lambda i: (i, 0)),
        out_shape=jax.ShapeDtypeStruct((R, C), F32),
    )(landed)


def _pack_rows(flat_list, width=LANES):
    flat = jnp.concatenate([a.reshape(-1) for a in flat_list])
    n = flat.shape[0]
    rows = -(-n // width)
    rows = -(-rows // 8) * 8
    return jnp.pad(flat, (0, rows * width - n)).reshape(rows, width)


def _unpack(packed, shapes):
    flat = packed.reshape(-1)
    out, off = [], 0
    for s in shapes:
        n = 1
        for d in s:
            n *= d
        out.append(flat[off:off + n].reshape(s))
        off += n
    return out


def kernel(x, norm1_w, w_in, hg_lb_logits, hg_norm_w, rw_shift_mu, rw_w0, rw_w2, rw_a0, rw_a2, rw_g2, rw_k_k, rw_k_a, rw_r_k, rw_ln_w, rw_ln_b, w_out, norm2_w, w_up, conv_w, conv_b, w_down, final_norm_w, loss_target, m_norm1_w, m_w_in, m_hg_lb_logits, m_hg_norm_w, m_rw_shift_mu, m_rw_w0, m_rw_w2, m_rw_a0, m_rw_a2, m_rw_g2, m_rw_k_k, m_rw_k_a, m_rw_r_k, m_rw_ln_w, m_rw_ln_b, m_w_out, m_norm2_w, m_w_up, m_conv_w, m_conv_b, m_w_down, m_final_norm_w, v_norm1_w, v_w_in, v_hg_lb_logits, v_hg_norm_w, v_rw_shift_mu, v_rw_w0, v_rw_w2, v_rw_a0, v_rw_a2, v_rw_g2, v_rw_k_k, v_rw_k_a, v_rw_r_k, v_rw_ln_w, v_rw_ln_b, v_w_out, v_norm2_w, v_w_up, v_conv_w, v_conv_b, v_w_down, v_final_norm_w):
    weights = dict(norm1_w=norm1_w, w_in=w_in, hg_lb_logits=hg_lb_logits, hg_norm_w=hg_norm_w,
                   rw_shift_mu=rw_shift_mu, rw_w0=rw_w0, rw_w2=rw_w2, rw_a0=rw_a0, rw_a2=rw_a2, rw_g2=rw_g2,
                   rw_k_k=rw_k_k, rw_k_a=rw_k_a, rw_r_k=rw_r_k, rw_ln_w=rw_ln_w, rw_ln_b=rw_ln_b, w_out=w_out,
                   norm2_w=norm2_w, w_up=w_up, conv_w=conv_w, conv_b=conv_b, w_down=w_down,
                   final_norm_w=final_norm_w)
    m_in = dict(norm1_w=m_norm1_w, w_in=m_w_in, hg_lb_logits=m_hg_lb_logits, hg_norm_w=m_hg_norm_w,
                rw_shift_mu=m_rw_shift_mu, rw_w0=m_rw_w0, rw_w2=m_rw_w2, rw_a0=m_rw_a0, rw_a2=m_rw_a2,
                rw_g2=m_rw_g2, rw_k_k=m_rw_k_k, rw_k_a=m_rw_k_a, rw_r_k=m_rw_r_k, rw_ln_w=m_rw_ln_w,
                rw_ln_b=m_rw_ln_b, w_out=m_w_out, norm2_w=m_norm2_w, w_up=m_w_up, conv_w=m_conv_w,
                conv_b=m_conv_b, w_down=m_w_down, final_norm_w=m_final_norm_w)
    v_in = dict(norm1_w=v_norm1_w, w_in=v_w_in, hg_lb_logits=v_hg_lb_logits, hg_norm_w=v_hg_norm_w,
                rw_shift_mu=v_rw_shift_mu, rw_w0=v_rw_w0, rw_w2=v_rw_w2, rw_a0=v_rw_a0, rw_a2=v_rw_a2,
                rw_g2=v_rw_g2, rw_k_k=v_rw_k_k, rw_k_a=v_rw_k_a, rw_r_k=v_rw_r_k, rw_ln_w=v_rw_ln_w,
                rw_ln_b=v_rw_ln_b, w_out=v_w_out, norm2_w=v_norm2_w, w_up=v_w_up, conv_w=v_conv_w,
                conv_b=v_conv_b, w_down=v_w_down, final_norm_w=v_final_norm_w)
    names = list(weights)
    sharded_small = ["rw_w2", "rw_a2", "rw_g2", "conv_w"]
    sharded_big = ["w_in", "w_out", "w_up", "w_down"]
    replicated = [n for n in names if n not in sharded_small + sharded_big]

    xs = x[0]
    tgt = loss_target[0]

    small_shard = _pack_rows([weights[n] for n in sharded_small])
    g_win_t, g_wup_t, g_wout, g_wdown, g_small = _all_gather(
        [w_in[0].T.astype(BF16), w_up[0].T.astype(BF16), w_out[0].astype(BF16), w_down[0].astype(BF16),
         small_shard], "gather_weights")
    small_shapes = [weights[n].shape for n in sharded_small]
    per_dev = [_unpack(g_small.reshape(N_DEV, -1)[j], small_shapes) for j in range(N_DEV)]
    w2_full, a2_full, g2_full, convw_full = [jnp.concatenate([per_dev[j][i][0] for j in range(N_DEV)], axis=-1)
                                             for i in range(4)]
    zeros64 = jnp.zeros((RW_PAIRS, 64, LANES), F32)
    by_pair = lambda z: z.reshape(z.shape[0], RW_PAIRS, LANES).transpose(1, 0, 2)
    w2p = jnp.concatenate([by_pair(w2_full), zeros64], axis=1)
    a2p = jnp.concatenate([zeros64, by_pair(a2_full)], axis=1)
    g2p = by_pair(g2_full)

    l0, l1 = hg_lb_logits[0:1], hg_lb_logits[1:2]
    h1 = _rms_fwd(xs, norm1_w, "norm1")
    proj = _mm_nt(h1, g_win_t, "proj_in")
    o_hg, hg_states = _hg_fwd(proj, l0, l1, hg_norm_w, "hgrn2_fwd")
    rws = _shift_fwd(proj, rw_shift_mu, "token_shift")
    rw_vecs = [rw_w0, rw_a0, rw_k_k, rw_k_a, rw_r_k, rw_ln_w, rw_ln_b]
    rw_mats = [w2p, a2p, g2p]
    o_rw, rw_states = _rw_fwd(rws, rw_vecs, rw_mats, "rwkv7_fwd")
    o_mix = jnp.concatenate([o_hg, o_rw], axis=-1).astype(BF16)
    x1 = _mm_nn(o_mix, g_wout, xs, "proj_out")
    h2 = _rms_fwd(x1, norm2_w, "norm2")
    u = _mm_nt(h2, g_wup_t, "ffn_up")
    act = _ffn_act_fwd(u, convw_full, conv_b, "ffn_act")
    x2 = _mm_nn(act, g_wdown, x1, "ffn_down")
    loss_part, dx2, d_final_w = _loss_head(x2, final_norm_w.reshape(1, -1), tgt, "loss_head")

    d_wdown = _mm_tn(act, dx2, 1408, "ffn_down_dw")
    dact = _mm_nt(dx2, g_wdown, "ffn_down_dx", BF16)
    du_g, du_v, dcw_g, dcw_v, dcb_g, dcb_v = _ffn_act_bwd(u, dact, convw_full, conv_b, "ffn_act_bwd")
    du = jnp.concatenate([du_g, du_v], axis=-1)
    d_convw = jnp.concatenate([dcw_g, dcw_v], axis=-1)
    d_convb = jnp.concatenate([dcb_g, dcb_v], axis=-1)
    d_wup_t = _mm_tn(du, h2, 1408, "ffn_up_dw")
    dh2 = _mm_nn(du, g_wup_t, None, "ffn_up_dx")
    dx1, d_norm2 = _rms_bwd(dh2, x1, norm2_w, dx2, "norm2_bwd")
    d_wout = _mm_tn(o_mix, dx1, 512, "proj_out_dw")
    do = _mm_nt(dx1, g_wout, "proj_out_dx")
    dq, df, di, dg, d_l0, d_l1, d_hg_nw = _hg_bwd(proj, hg_states, do, 0, l0, l1, hg_norm_w, "hgrn2_bwd")
    rw_out = _rw_bwd(rws, rw_states, do, 1, rw_vecs, rw_mats, "rwkv7_bwd")
    d_rw_vecs = rw_out[4:4 + N_RW_VEC]
    d_w2p, d_a2p, d_g2p = rw_out[4 + N_RW_VEC:]
    dp_parts, dmu_parts = [], []
    for i, z in enumerate(rw_out[:4]):
        dp, dmu = _shift_bwd(z, proj, rw_shift_mu, i * RW_WIDTH, "token_shift_bwd_%d" % i)
        dp_parts.append(dp)
        dmu_parts.append(dmu)
    d_mu = jnp.concatenate(dmu_parts, axis=-1)
    dproj = jnp.concatenate([dq, df, di, dg] + dp_parts, axis=-1)
    d_win_t = _mm_tn(dproj, h1, 768, "proj_in_dw")
    dh1 = _mm_nn(dproj, g_win_t, None, "proj_in_dx")
    grad_x, d_norm1 = _rms_bwd(dh1, xs, norm1_w, dx1, "norm1_bwd")

    from_pairs = lambda z: z.transpose(1, 0, 2).reshape(z.shape[1], RW_WIDTH)
    d_w2 = from_pairs(d_w2p[:, :64])
    d_a2 = from_pairs(d_a2p[:, 64:])
    d_g2 = from_pairs(d_g2p)
    col_blocks = lambda z: z.reshape(z.shape[0], N_DEV, -1).transpose(1, 0, 2)
    small_part = jnp.stack([
        _pack_rows([col_blocks(d_w2)[j], col_blocks(d_a2)[j], col_blocks(d_g2)[j], col_blocks(d_convw)[j]])
        for j in range(N_DEV)])
    rep_grads = dict(norm1_w=d_norm1, hg_lb_logits=jnp.concatenate([d_l0, d_l1], axis=0), hg_norm_w=d_hg_nw,
                     rw_shift_mu=d_mu, rw_w0=d_rw_vecs[0], rw_a0=d_rw_vecs[1], rw_k_k=d_rw_vecs[2],
                     rw_k_a=d_rw_vecs[3], rw_r_k=d_rw_vecs[4], rw_ln_w=d_rw_vecs[5], rw_ln_b=d_rw_vecs[6],
                     norm2_w=d_norm2, conv_b=d_convb, final_norm_w=d_final_w)
    rep_pack = _pack_rows([loss_part] + [rep_grads[n] for n in replicated])
    rep_part = jnp.broadcast_to(rep_pack[None], (N_DEV,) + rep_pack.shape)
    blocks = lambda z: z.reshape(N_DEV, z.shape[0] // N_DEV, z.shape[1])
    big_parts = [blocks(z).astype(BF16) for z in (d_win_t, d_wup_t, d_wout, d_wdown)]
    landed = _exchange(big_parts + [small_part, rep_part], "exchange_grads")
    sums = [_sum_slots(z, "sum_grads_%d" % i) for i, z in enumerate(landed)]
    g_small_sum = _unpack(sums[4], small_shapes)
    rep_sum = _unpack(sums[5], [(1, 1)] + [weights[n].shape for n in replicated])
    loss = rep_sum[0].reshape(())
    grads = dict(zip(replicated, rep_sum[1:]))
    grads.update(dict(zip(sharded_small, g_small_sum)))
    grads["w_in"] = sums[0].T[None]
    grads["w_up"] = sums[1].T[None]
    grads["w_out"] = sums[2][None]
    grads["w_down"] = sums[3][None]

    delta, new_m, new_v = {}, {}, {}
    for n in sharded_big:
        shp = weights[n].shape
        as2d = lambda z: z.reshape(shp[1], shp[2])
        d, nm, nv = _adamw(as2d(weights[n]), as2d(grads[n]), as2d(m_in[n]), as2d(v_in[n]), "adamw_" + n)
        delta[n], new_m[n], new_v[n] = d.reshape(shp), nm.reshape(shp), nv.reshape(shp)
    small_names = replicated + sharded_small
    packs = [_pack_rows([src[n] for n in small_names]) for src in (weights, grads, m_in, v_in)]
    outs = _adamw(*packs, "adamw_small")
    small_shapes_all = [weights[n].shape for n in small_names]
    for dst, packed in zip((delta, new_m, new_v), outs):
        dst.update(dict(zip(small_names, _unpack(packed, small_shapes_all))))

    return (loss, grad_x[None], *[grads[n] for n in names], *[delta[n] for n in names],
            *[new_m[n] for n in names], *[new_v[n] for n in names])
```

```python
import functools

import jax
import jax.numpy as jnp
from jax import lax
from jax.experimental import pallas as pl
from jax.experimental.pallas import tpu as pltpu

F32 = jnp.float32
BF16 = jnp.bfloat16
HIGHEST = lax.Precision.HIGHEST
SCAN_PRECISION = None
MESH_ID = pl.DeviceIdType.MESH

N_DEV = 8
D_MODEL = 1024
HG_WIDTH = 512
HG_HEAD_DIM = 128
HG_HEADS = 4
RW_WIDTH = 512
RW_PAIRS = 4
RW_HEAD_DIM = 64
HG_COLS = 2048
RW_COLS = 1792
D_FF = 2816
NORM_EPS = 1e-6
RW_GN_EPS = 64e-5
L2_EPS = 1e-12
ADAM_LR, ADAM_B1, ADAM_B2, ADAM_EPS, ADAM_WD, ADAM_STEP = 0.001, 0.9, 0.999, 1e-08, 0.01, 10

HG_CHUNK = 16
RW_CHUNK = 32
SCAN_ROWS = 256
LANES = 128

NN = ((1,), (0,))
NT = ((1,), (1,))
TN = ((0,), (0,))


def _dot(a, b, dims=NN, precision=SCAN_PRECISION):
    if precision is None:
        a, b = a.astype(BF16), b.astype(BF16)
    return lax.dot_general(a, b, (dims, ((), ())), precision=precision, preferred_element_type=F32)


def _iota2(shape, dim):
    return lax.broadcasted_iota(jnp.int32, shape, dim)


def _sigmoid(z):
    return 1.0 / (1.0 + jnp.exp(-z))


def _row_tile(n, want):
    t = min(n, want)
    while n % t:
        t //= 2
    return t


def _rms_fwd(x, w, name):
    T, D = x.shape
    tb = _row_tile(T, 512)

    def body(x_ref, w_ref, h_ref):
        xv = x_ref[...]
        r = lax.rsqrt(jnp.mean(xv * xv, axis=-1, keepdims=True) + NORM_EPS)
        h_ref[...] = (xv * r * w_ref[...]).astype(h_ref.dtype)

    return pl.pallas_call(
        body, name=name, grid=(T // tb,),
        in_specs=[pl.BlockSpec((tb, D), lambda i: (i, 0)), pl.BlockSpec((1, D), lambda i: (0, 0))],
        out_specs=pl.BlockSpec((tb, D), lambda i: (i, 0)),
        out_shape=jax.ShapeDtypeStruct((T, D), BF16),
    )(x, w)


def _rms_bwd(dh, x, w, dres, name):
    T, D = x.shape
    tb = _row_tile(T, 256)

    def body(dh_ref, x_ref, w_ref, dres_ref, dx_ref, dw_ref):
        @pl.when(pl.program_id(0) == 0)
        def _():
            dw_ref[...] = jnp.zeros_like(dw_ref)

        xv = x_ref[...]
        r = lax.rsqrt(jnp.mean(xv * xv, axis=-1, keepdims=True) + NORM_EPS)
        xn = xv * r
        dy = dh_ref[...].astype(F32)
        dxn = dy * w_ref[...]
        dx_ref[...] = dres_ref[...] + r * (dxn - xn * jnp.mean(dxn * xn, axis=-1, keepdims=True))
        dw_ref[...] += jnp.sum(dy * xn, axis=0, keepdims=True)

    row = pl.BlockSpec((tb, D), lambda i: (i, 0))
    vec = pl.BlockSpec((1, D), lambda i: (0, 0))
    return pl.pallas_call(
        body, name=name, grid=(T // tb,),
        in_specs=[row, row, vec, row], out_specs=[row, vec],
        out_shape=[jax.ShapeDtypeStruct((T, D), F32), jax.ShapeDtypeStruct((1, D), F32)],
    )(dh, x, w, dres)


def _mm_nt(a, bt, name, out_dtype=F32):
    T, K = a.shape
    N = bt.shape[0]
    tm = _row_tile(T, 256)

    def body(a_ref, b_ref, o_ref):
        o_ref[...] = _dot(a_ref[...].astype(BF16), b_ref[...].astype(BF16), NT, None).astype(o_ref.dtype)

    return pl.pallas_call(
        body, name=name, grid=(T // tm,),
        in_specs=[pl.BlockSpec((tm, K), lambda i: (i, 0)), pl.BlockSpec((N, K), lambda i: (0, 0))],
        out_specs=pl.BlockSpec((tm, N), lambda i: (i, 0)),
        out_shape=jax.ShapeDtypeStruct((T, N), out_dtype),
    )(a, bt)


def _mm_nn(a, b, res, name, out_dtype=F32):
    T, K = a.shape
    N = b.shape[1]
    tm = _row_tile(T, 256)

    def body(a_ref, b_ref, *rest):
        o_ref = rest[-1]
        acc = _dot(a_ref[...].astype(BF16), b_ref[...].astype(BF16), NN, None)
        if res is not None:
            acc = acc + rest[0][...]
        o_ref[...] = acc.astype(o_ref.dtype)

    in_specs = [pl.BlockSpec((tm, K), lambda i: (i, 0)), pl.BlockSpec((K, N), lambda i: (0, 0))]
    args = [a, b]
    if res is not None:
        in_specs.append(pl.BlockSpec((tm, N), lambda i: (i, 0)))
        args.append(res)
    return pl.pallas_call(
        body, name=name, grid=(T // tm,), in_specs=in_specs,
        out_specs=pl.BlockSpec((tm, N), lambda i: (i, 0)),
        out_shape=jax.ShapeDtypeStruct((T, N), out_dtype),
    )(*args)


def _mm_tn(a, b, tmm, name):
    T, M = a.shape
    N = b.shape[1]
    tk = _row_tile(T, 512)

    def body(a_ref, b_ref, o_ref):
        @pl.when(pl.program_id(1) == 0)
        def _():
            o_ref[...] = jnp.zeros_like(o_ref)

        o_ref[...] += _dot(a_ref[...].astype(BF16), b_ref[...].astype(BF16), TN, None)

    return pl.pallas_call(
        body, name=name, grid=(M // tmm, T // tk),
        in_specs=[pl.BlockSpec((tk, tmm), lambda m, k: (k, m)), pl.BlockSpec((tk, N), lambda m, k: (k, 0))],
        out_specs=pl.BlockSpec((tmm, N), lambda m, k: (m, 0)),
        out_shape=jax.ShapeDtypeStruct((M, N), F32),
    )(a, b)


def _shift_rows_down(z, n):
    rows = _iota2(z.shape, 0)
    return jnp.where(rows < n, 0.0, pltpu.roll(z, n, 0))


def _shift_rows_up(z, n):
    T = z.shape[0]
    rows = _iota2(z.shape, 0)
    return jnp.where(rows >= T - n, 0.0, pltpu.roll(z, T - n, 0))


def _shift_fwd(proj, mu, name):
    T = proj.shape[0]
    nblk = RW_COLS // LANES
    first = HG_COLS // LANES

    def body(p_ref, mu_ref, o_ref):
        p = p_ref[...]
        o_ref[...] = p + (_shift_rows_down(p, 1) - p) * mu_ref[...]

    return pl.pallas_call(
        body, name=name, grid=(nblk,),
        in_specs=[pl.BlockSpec((T, LANES), lambda j: (0, first + j)), pl.BlockSpec((1, LANES), lambda j: (0, j))],
        out_specs=pl.BlockSpec((T, LANES), lambda j: (0, j)),
        out_shape=jax.ShapeDtypeStruct((T, RW_COLS), F32),
    )(proj, mu)


def _shift_bwd(ds, proj, mu, col0, name):
    T, width = ds.shape
    nblk = width // LANES
    first = (HG_COLS + col0) // LANES
    mu0 = col0 // LANES

    def body(ds_ref, p_ref, mu_ref, dp_ref, dmu_ref):
        dsv = ds_ref[...]
        p = p_ref[...]
        m = mu_ref[...]
        dp_ref[...] = (dsv * (1.0 - m) + _shift_rows_up(dsv * m, 1)).astype(dp_ref.dtype)
        dmu_ref[...] = jnp.sum(dsv * (_shift_rows_down(p, 1) - p), axis=0, keepdims=True)

    return pl.pallas_call(
        body, name=name, grid=(nblk,),
        in_specs=[pl.BlockSpec((T, LANES), lambda j: (0, j)),
                  pl.BlockSpec((T, LANES), lambda j: (0, first + j)),
                  pl.BlockSpec((1, LANES), lambda j: (0, mu0 + j))],
        out_specs=[pl.BlockSpec((T, LANES), lambda j: (0, j)), pl.BlockSpec((1, LANES), lambda j: (0, j))],
        out_shape=[jax.ShapeDtypeStruct((T, width), BF16), jax.ShapeDtypeStruct((1, width), F32)],
    )(ds, proj, mu)


def _conv3(z, w_ref):
    return w_ref[0:1, :] * _shift_rows_down(z, 2) + w_ref[1:2, :] * _shift_rows_down(z, 1) + w_ref[2:3, :] * z


def _ffn_act_fwd(u, conv_w, conv_b, name):
    T = u.shape[0]
    nblk = D_FF // LANES

    def body(ug_ref, uv_ref, wg_ref, wv_ref, bg_ref, bv_ref, act_ref):
        gate = _conv3(ug_ref[...], wg_ref) + bg_ref[...]
        val = _conv3(uv_ref[...], wv_ref) + bv_ref[...]
        act_ref[...] = (gate * _sigmoid(gate) * val).astype(act_ref.dtype)

    col = lambda off: pl.BlockSpec((T, LANES), lambda j: (0, off + j))
    wsp = lambda off: pl.BlockSpec((3, LANES), lambda j: (0, off + j))
    bsp = lambda off: pl.BlockSpec((1, LANES), lambda j: (0, off + j))
    return pl.pallas_call(
        body, name=name, grid=(nblk,),
        in_specs=[col(0), col(nblk), wsp(0), wsp(nblk), bsp(0), bsp(nblk)],
        out_specs=pl.BlockSpec((T, LANES), lambda j: (0, j)),
        out_shape=jax.ShapeDtypeStruct((T, D_FF), BF16),
    )(u, u, conv_w, conv_w, conv_b, conv_b)


def _ffn_act_bwd(u, dact, conv_w, conv_b, name):
    T = u.shape[0]
    nblk = D_FF // LANES

    def conv_bwd(z, dzc, w_ref, du_ref, dw_ref, db_ref):
        du = w_ref[2:3, :] * dzc + w_ref[1:2, :] * _shift_rows_up(dzc, 1) + w_ref[0:1, :] * _shift_rows_up(dzc, 2)
        du_ref[...] = du.astype(du_ref.dtype)
        dw_ref[0:1, :] = jnp.sum(dzc * _shift_rows_down(z, 2), axis=0, keepdims=True)
        dw_ref[1:2, :] = jnp.sum(dzc * _shift_rows_down(z, 1), axis=0, keepdims=True)
        dw_ref[2:3, :] = jnp.sum(dzc * z, axis=0, keepdims=True)
        db_ref[...] = jnp.sum(dzc, axis=0, keepdims=True)

    def body(ug_ref, uv_ref, da_ref, wg_ref, wv_ref, bg_ref, bv_ref,
             dug_ref, duv_ref, dwg_ref, dwv_ref, dbg_ref, dbv_ref):
        ug, uv = ug_ref[...], uv_ref[...]
        gate = _conv3(ug, wg_ref) + bg_ref[...]
        val = _conv3(uv, wv_ref) + bv_ref[...]
        da = da_ref[...].astype(F32)
        sg = _sigmoid(gate)
        dgate = da * val * (sg * (1.0 + gate * (1.0 - sg)))
        dval = da * gate * sg
        conv_bwd(ug, dgate, wg_ref, dug_ref, dwg_ref, dbg_ref)
        conv_bwd(uv, dval, wv_ref, duv_ref, dwv_ref, dbv_ref)

    col = lambda off: pl.BlockSpec((T, LANES), lambda j: (0, off + j))
    wsp = lambda off: pl.BlockSpec((3, LANES), lambda j: (0, off + j))
    bsp = lambda off: pl.BlockSpec((1, LANES), lambda j: (0, off + j))
    half = lambda r, dt: jax.ShapeDtypeStruct((r, D_FF), dt)
    return pl.pallas_call(
        body, name=name, grid=(nblk,),
        in_specs=[col(0), col(nblk), col(0), wsp(0), wsp(nblk), bsp(0), bsp(nblk)],
        out_specs=[col(0), col(0), wsp(0), wsp(0), bsp(0), bsp(0)],
        out_shape=[half(T, BF16), half(T, BF16), half(3, F32), half(3, F32), half(1, F32), half(1, F32)],
    )(u, u, dact, conv_w, conv_w, conv_b, conv_b)


def _loss_head(x2, w, target, name):
    T, D = x2.shape
    tb = _row_tile(T, 256)

    def body(x_ref, w_ref, t_ref, loss_ref, dx_ref, dw_ref):
        @pl.when(pl.program_id(0) == 0)
        def _():
            loss_ref[...] = jnp.zeros_like(loss_ref)
            dw_ref[...] = jnp.zeros_like(dw_ref)

        xv = x_ref[...]
        r = lax.rsqrt(jnp.mean(xv * xv, axis=-1, keepdims=True) + NORM_EPS)
        xn = xv * r
        err = xn * w_ref[...] - t_ref[...]
        row_loss = jnp.sum(err * err, axis=-1, keepdims=True) * (0.5 / D)
        loss_ref[...] += jnp.sum(row_loss, axis=0, keepdims=True)
        dy = err * (1.0 / D)
        dxn = dy * w_ref[...]
        dx_ref[...] = r * (dxn - xn * jnp.mean(dxn * xn, axis=-1, keepdims=True))
        dw_ref[...] += jnp.sum(dy * xn, axis=0, keepdims=True)

    row = pl.BlockSpec((tb, D), lambda i: (i, 0))
    vec = pl.BlockSpec((1, D), lambda i: (0, 0))
    return pl.pallas_call(
        body, name=name, grid=(T // tb,),
        in_specs=[row, vec, row],
        out_specs=[pl.BlockSpec((1, 1), lambda i: (0, 0)), row, vec],
        out_shape=[jax.ShapeDtypeStruct((1, 1), F32), jax.ShapeDtypeStruct((T, D), F32),
                   jax.ShapeDtypeStruct((1, D), F32)],
    )(x2, w, target)


def _adamw(w, g, m, v, name):
    R, C = w.shape
    tb = _row_tile(R, 256) if R % 8 == 0 else R

    def body(w_ref, g_ref, m_ref, v_ref, d_ref, nm_ref, nv_ref):
        gv = g_ref[...]
        nm = ADAM_B1 * m_ref[...] + (1.0 - ADAM_B1) * gv
        nv = ADAM_B2 * v_ref[...] + (1.0 - ADAM_B2) * (gv * gv)
        m_hat = nm / (1.0 - ADAM_B1 ** ADAM_STEP)
        v_hat = nv / (1.0 - ADAM_B2 ** ADAM_STEP)
        d_ref[...] = -ADAM_LR * (m_hat / (jnp.sqrt(v_hat) + ADAM_EPS) + ADAM_WD * w_ref[...])
        nm_ref[...] = nm
        nv_ref[...] = nv

    blk = pl.BlockSpec((tb, C), lambda i: (i, 0))
    sd = jax.ShapeDtypeStruct((R, C), F32)
    return pl.pallas_call(
        body, name=name, grid=(R // tb,), in_specs=[blk] * 4, out_specs=[blk] * 3, out_shape=[sd] * 3,
    )(w, g, m, v)


def _chunk_masks(rows, chunk):
    shift = chunk.bit_length() - 1
    i, j = _iota2((rows, rows), 0), _iota2((rows, rows), 1)
    same = jnp.right_shift(i, shift) == jnp.right_shift(j, shift)
    return same.astype(F32), (same & (j <= i)).astype(F32), (same & (j < i)).astype(F32)


def _head_lanes(h):
    return slice(h * LANES, (h + 1) * LANES)


def _chunk_rows(c, chunk):
    return pl.ds(pl.multiple_of(c * chunk, chunk), chunk)


def _hg_consts(rows):
    same, tril, _ = _chunk_masks(rows, HG_CHUNK)
    shift = HG_CHUNK.bit_length() - 1
    i, j = _iota2((rows, rows), 0), _iota2((rows, rows), 1)
    mid_row = jnp.left_shift(jnp.right_shift(i, shift), shift) + (HG_CHUNK // 2 - 1)
    return same, tril, (j == mid_row).astype(F32)


def _hg_prep(consts, qr, fr, ir, l0, l1):
    same, tril, sel_mid = consts
    lb = _sigmoid(l0 - l1)
    f = lb + (1.0 - lb) * _sigmoid(fr)
    q = qr * _sigmoid(qr) * (HG_HEAD_DIM ** -0.5)
    k = 1.0 - f
    g = jnp.log(f)
    a = _dot(tril, g, NN, HIGHEST)
    tot = _dot(same, g, NN, HIGHEST)
    mid = _dot(sel_mid, a, NN, HIGHEST)
    att = _dot(q * jnp.exp(a - mid), k * jnp.exp(mid - a), NT) * tril
    return q * jnp.exp(a), _dot(att, ir), k * jnp.exp(tot - a), jnp.exp(tot)


def _hg_post(o, gr, nw):
    on = o * lax.rsqrt(jnp.mean(o * o, axis=-1, keepdims=True) + NORM_EPS)
    return on * nw * (gr * _sigmoid(gr))


def _hg_specs(T, tb, rev):
    nT = T // tb
    tix = (lambda t: nT - 1 - t) if rev else (lambda t: t)
    col = lambda blk: pl.BlockSpec((tb, HG_WIDTH), lambda t: (tix(t), blk))
    vec = pl.BlockSpec((1, HG_WIDTH), lambda t: (0, 0))
    st = pl.BlockSpec((HG_HEADS, tb // HG_CHUNK, HG_HEAD_DIM, HG_HEAD_DIM), lambda t: (0, tix(t), 0, 0))
    return nT, col, vec, st


def _hg_fwd(proj, l0, l1, nw, name):
    T = proj.shape[0]
    tb = _row_tile(T, SCAN_ROWS)
    nsub = tb // HG_CHUNK
    nT, col, vec, st = _hg_specs(T, tb, False)

    def body(q_ref, f_ref, i_ref, g_ref, l0_ref, l1_ref, nw_ref, o_ref, st_ref, s_ref, qe_ref, kd_ref, dec_ref):
        @pl.when(pl.program_id(0) == 0)
        def _():
            s_ref[...] = jnp.zeros_like(s_ref)

        consts = _hg_consts(tb)
        for h in range(HG_HEADS):
            ln = _head_lanes(h)
            qe, o_intra, kd, dec = _hg_prep(consts, q_ref[:, ln], f_ref[:, ln], i_ref[:, ln],
                                            l0_ref[:, ln], l1_ref[:, ln])
            qe_ref[h], kd_ref[h], dec_ref[h] = qe, kd, dec
            o_ref[:, ln] = o_intra

        def step(c, carry):
            rows = _chunk_rows(c, HG_CHUNK)
            for h in range(HG_HEADS):
                ln = _head_lanes(h)
                S = s_ref[h]
                st_ref[h, c] = S
                o_ref[rows, ln] += _dot(qe_ref[h, rows, :], S, NT)
                s_ref[h] = S * dec_ref[h, pl.ds(c * HG_CHUNK, 1), :] + _dot(i_ref[rows, ln], kd_ref[h, rows, :], TN)
            return carry

        lax.fori_loop(0, nsub, step, 0)
        for h in range(HG_HEADS):
            ln = _head_lanes(h)
            o_ref[:, ln] = _hg_post(o_ref[:, ln], g_ref[:, ln], nw_ref[:, ln])

    blk = pltpu.VMEM((HG_HEADS, tb, LANES), F32)
    return pl.pallas_call(
        body, name=name, grid=(nT,),
        in_specs=[col(0), col(1), col(2), col(3), vec, vec, vec],
        out_specs=[col(0), st],
        out_shape=[jax.ShapeDtypeStruct((T, HG_WIDTH), F32),
                   jax.ShapeDtypeStruct((HG_HEADS, T // HG_CHUNK, HG_HEAD_DIM, HG_HEAD_DIM), F32)],
        scratch_shapes=[pltpu.VMEM((HG_HEADS, HG_HEAD_DIM, HG_HEAD_DIM), F32), blk, blk, blk],
    )(proj, proj, proj, proj, l0, l1, nw)


def _hg_bwd(proj, states, do, do_blk, l0, l1, nw, name):
    T = proj.shape[0]
    tb = _row_tile(T, SCAN_ROWS)
    nsub = tb // HG_CHUNK
    nT, col, vec, st = _hg_specs(T, tb, True)

    def body(q_ref, f_ref, i_ref, g_ref, st_ref, do_ref, l0_ref, l1_ref, nw_ref,
             dq_ref, df_ref, di_ref, dg_ref, dl0_ref, dl1_ref, dnw_ref,
             ds_ref, qe_ref, kd_ref, dec_ref, o_ref, dqe_ref, dkd_ref, ddec_ref, dis_ref):
        @pl.when(pl.program_id(0) == 0)
        def _():
            ds_ref[...] = jnp.zeros_like(ds_ref)
            dl0_ref[...] = jnp.zeros_like(dl0_ref)
            dl1_ref[...] = jnp.zeros_like(dl1_ref)
            dnw_ref[...] = jnp.zeros_like(dnw_ref)

        consts = _hg_consts(tb)
        prep_vjps = []
        for h in range(HG_HEADS):
            ln = _head_lanes(h)
            (qe, o_intra, kd, dec), vjp = jax.vjp(
                functools.partial(_hg_prep, consts), q_ref[:, ln], f_ref[:, ln], i_ref[:, ln],
                l0_ref[:, ln], l1_ref[:, ln])
            prep_vjps.append(vjp)
            qe_ref[h], kd_ref[h], dec_ref[h], o_ref[h] = qe, kd, dec, o_intra

        def redo(c, carry):
            rows = _chunk_rows(c, HG_CHUNK)
            for h in range(HG_HEADS):
                o_ref[h, rows, :] += _dot(qe_ref[h, rows, :], st_ref[h, c], NT)
            return carry

        lax.fori_loop(0, nsub, redo, 0)
        for h in range(HG_HEADS):
            ln = _head_lanes(h)
            _, vjp = jax.vjp(_hg_post, o_ref[h], g_ref[:, ln], nw_ref[:, ln])
            d_o, dgr, dnw = vjp(do_ref[:, ln])
            o_ref[h] = d_o
            dg_ref[:, ln] = dgr.astype(dg_ref.dtype)
            dnw_ref[:, ln] += dnw
        ddec_ref[...] = jnp.zeros_like(ddec_ref)

        def step(i, carry):
            c = nsub - 1 - i
            rows = _chunk_rows(c, HG_CHUNK)
            row0 = pl.ds(c * HG_CHUNK, 1)
            for h in range(HG_HEADS):
                ln = _head_lanes(h)
                G = ds_ref[h]
                S = st_ref[h, c]
                d_o = o_ref[h, rows, :]
                dqe_ref[h, rows, :] = _dot(d_o, S)
                dkd_ref[h, rows, :] = _dot(i_ref[rows, ln], G)
                dis_ref[h, rows, :] = _dot(kd_ref[h, rows, :], G, NT)
                ddec_ref[h, row0, :] = jnp.sum(S * G, axis=0, keepdims=True)
                ds_ref[h] = G * dec_ref[h, row0, :] + _dot(d_o, qe_ref[h, rows, :], TN)
            return carry

        lax.fori_loop(0, nsub, step, 0)
        for h in range(HG_HEADS):
            ln = _head_lanes(h)
            dq, df, di, dl0, dl1 = prep_vjps[h]((dqe_ref[h], o_ref[h], dkd_ref[h], ddec_ref[h]))
            dq_ref[:, ln] = dq.astype(dq_ref.dtype)
            df_ref[:, ln] = df.astype(df_ref.dtype)
            di_ref[:, ln] = (di + dis_ref[h]).astype(di_ref.dtype)
            dl0_ref[:, ln] += dl0
            dl1_ref[:, ln] += dl1

    dcol = jax.ShapeDtypeStruct((T, HG_WIDTH), BF16)
    dvec = jax.ShapeDtypeStruct((1, HG_WIDTH), F32)
    blk = pltpu.VMEM((HG_HEADS, tb, LANES), F32)
    return pl.pallas_call(
        body, name=name, grid=(nT,),
        in_specs=[col(0), col(1), col(2), col(3), st, col(do_blk), vec, vec, vec],
        out_specs=[col(0)] * 4 + [vec] * 3,
        out_shape=[dcol] * 4 + [dvec] * 3,
        scratch_shapes=[pltpu.VMEM((HG_HEADS, HG_HEAD_DIM, HG_HEAD_DIM), F32)] + [blk] * 8,
    )(proj, proj, proj, proj, states, do, l0, l1, nw)


def _rw_consts(rows):
    same, tril, stril = _chunk_masks(rows, RW_CHUNK)
    eye = (_iota2((rows, rows), 0) == _iota2((rows, rows), 1)).astype(F32)
    br, bc = _iota2((LANES, LANES), 0), _iota2((LANES, LANES), 1)
    blockdiag = ((br < RW_HEAD_DIM) == (bc < RW_HEAD_DIM)).astype(F32)
    m0 = (_iota2((1, LANES), 1) < RW_HEAD_DIM).astype(F32)
    return same, tril, stril, eye, blockdiag, m0, 1.0 - m0


def _unit_lower_inverse(low, eye):
    x = eye + low
    p = low
    n = 2
    while n < RW_CHUNK:
        p = _dot(p, p)
        x = x + _dot(x, p)
        n *= 2
    return x


N_PREP_OUT = 9


def _rw_prep(consts, r, kx, v, lw, gd, w0, a0, k_k, k_a, w2p, a2p, g2):
    same, tril, stril, eye, blockdiag, m0, m1 = consts
    xw = w0 + _dot(jnp.tanh(lw), w2p)
    w = jnp.minimum(xw, 0.0) - jnp.log(1.0 + jnp.exp(-jnp.abs(xw))) - 0.5
    ld = -jnp.exp(w)
    a_s = _sigmoid(a0 + _dot(lw, a2p))
    g = _dot(_sigmoid(gd), g2)
    kk = kx * k_k
    kk = kk / jnp.maximum(jnp.sqrt(_dot(kk * kk, blockdiag)), L2_EPS)
    k2 = kx * (1.0 + (a_s - 1.0) * k_a)
    bv = kk * a_s
    cum = _dot(tril, ld, NN, HIGHEST)
    tot = _dot(same, ld, NN, HIGHEST)
    ecn = jnp.exp(-cum)
    a_t = -kk * jnp.exp(cum - ld)
    b_h = bv * ecn
    k_h = k2 * ecn
    r_t = r * jnp.exp(cum)
    rem = jnp.exp(tot - cum)
    W = U = Q = Y0 = 0.0
    for m in (m0, m1):
        a_m = a_t * m
        r_m = r_t * m
        lak = _dot(a_m, k_h, NT) * stril
        mrb = _dot(r_m, b_h, NT) * tril
        mrk = _dot(r_m, k_h, NT) * tril
        tinv = _unit_lower_inverse(_dot(a_m, b_h, NT) * stril, eye)
        w_m = _dot(tinv, a_m)
        u_m = m * _dot(tinv, _dot(lak, v))
        W = W + w_m
        U = U + u_m
        Q = Q + r_m + _dot(mrb, w_m)
        Y0 = Y0 + _dot(mrb, u_m) + m * _dot(mrk, v)
    return W, U, Q, Y0, bv * rem, k2 * rem, jnp.exp(tot), k2, g


def _rw_post(blockdiag, y, r, v, k2, g, r_k, ln_w, ln_b):
    inv_n = 1.0 / RW_HEAD_DIM
    yc = y - _dot(y, blockdiag) * inv_n
    var = _dot(yc * yc, blockdiag) * inv_n
    yn = yc * lax.rsqrt(var + RW_GN_EPS) * ln_w + ln_b
    bonus = _dot(r * k2 * r_k, blockdiag) * v
    return (yn + bonus) * g


N_RW_VEC = 7
N_RW_MAT = 3


def _rw_specs(T, tb, rev):
    nT = T // tb
    tix = (lambda t: nT - 1 - t) if rev else (lambda t: t)
    wide = lambda blk: pl.BlockSpec((tb, RW_WIDTH), lambda t: (tix(t), blk))
    narrow = lambda blk: pl.BlockSpec((tb, LANES), lambda t: (tix(t), blk))
    vec = pl.BlockSpec((1, RW_WIDTH), lambda t: (0, 0))
    mat = pl.BlockSpec((RW_PAIRS, LANES, LANES), lambda t: (0, 0, 0))
    st = pl.BlockSpec((RW_PAIRS, tb // RW_CHUNK, LANES, LANES), lambda t: (0, tix(t), 0, 0))
    lora0 = 3 * RW_WIDTH // LANES
    ins = [wide(0), wide(1), wide(2), narrow(lora0), narrow(lora0 + 1)]
    return nT, wide, vec, mat, st, ins


def _rw_prep_args(p, r_ref, k_ref, v_ref, lw_ref, gd_ref, vrefs, mrefs):
    ln = _head_lanes(p)
    w0, a0, k_k, k_a = [x[:, ln] for x in vrefs[:4]]
    return (r_ref[:, ln], k_ref[:, ln], v_ref[:, ln], lw_ref[...], gd_ref[...], w0, a0, k_k, k_a,
            *[x[p] for x in mrefs])


def _rw_fwd(rws, vecs, mats, name):
    T = rws.shape[0]
    tb = _row_tile(T, SCAN_ROWS)
    nsub = tb // RW_CHUNK
    nT, wide, vec, mat, st, ins = _rw_specs(T, tb, False)

    def body(*refs):
        r_ref, k_ref, v_ref, lw_ref, gd_ref = refs[:5]
        vrefs = refs[5:5 + N_RW_VEC]
        mrefs = refs[5 + N_RW_VEC:5 + N_RW_VEC + N_RW_MAT]
        o_ref, st_ref, s_ref, pre_ref, y_ref = refs[-5:]

        @pl.when(pl.program_id(0) == 0)
        def _():
            s_ref[...] = jnp.zeros_like(s_ref)

        consts = _rw_consts(tb)
        blockdiag = consts[4]
        for p in range(RW_PAIRS):
            outs = _rw_prep(consts, *_rw_prep_args(p, r_ref, k_ref, v_ref, lw_ref, gd_ref, vrefs, mrefs))
            for i, z in enumerate(outs):
                pre_ref[i, p] = z

        def step(c, carry):
            rows = _chunk_rows(c, RW_CHUNK)
            for p in range(RW_PAIRS):
                ln = _head_lanes(p)
                S = s_ref[p]
                st_ref[p, c] = S
                P = _dot(pre_ref[0, p, rows, :], S, NT) + pre_ref[1, p, rows, :]
                y_ref[p, rows, :] = _dot(pre_ref[2, p, rows, :], S, NT) + pre_ref[3, p, rows, :]
                s_ref[p] = (S * pre_ref[6, p, pl.ds(c * RW_CHUNK, 1), :] + _dot(P, pre_ref[4, p, rows, :], TN)
                            + _dot(v_ref[rows, ln], pre_ref[5, p, rows, :], TN)) * blockdiag
            return carry

        lax.fori_loop(0, nsub, step, 0)
        for p in range(RW_PAIRS):
            ln = _head_lanes(p)
            r_k, ln_w, ln_b = [x[:, ln] for x in vrefs[4:]]
            o_ref[:, ln] = _rw_post(blockdiag, y_ref[p], r_ref[:, ln], v_ref[:, ln], pre_ref[7, p], pre_ref[8, p],
                                    r_k, ln_w, ln_b)

    return pl.pallas_call(
        body, name=name, grid=(nT,),
        in_specs=ins + [vec] * N_RW_VEC + [mat] * N_RW_MAT,
        out_specs=[wide(0), st],
        out_shape=[jax.ShapeDtypeStruct((T, RW_WIDTH), F32),
                   jax.ShapeDtypeStruct((RW_PAIRS, T // RW_CHUNK, LANES, LANES), F32)],
        scratch_shapes=[pltpu.VMEM((RW_PAIRS, LANES, LANES), F32),
                        pltpu.VMEM((N_PREP_OUT, RW_PAIRS, tb, LANES), F32),
                        pltpu.VMEM((RW_PAIRS, tb, LANES), F32)],
    )(rws, rws, rws, rws, rws, *vecs, *mats)


def _rw_bwd(rws, states, do, do_blk, vecs, mats, name):
    T = rws.shape[0]
    tb = _row_tile(T, SCAN_ROWS)
    nsub = tb // RW_CHUNK
    nT, wide, vec, mat, st, ins = _rw_specs(T, tb, True)
    nin = 5 + 1 + 1 + N_RW_VEC + N_RW_MAT

    def body(*refs):
        r_ref, k_ref, v_ref, lw_ref, gd_ref = refs[:5]
        st_ref, do_ref = refs[5], refs[6]
        vrefs = refs[7:7 + N_RW_VEC]
        mrefs = refs[7 + N_RW_VEC:nin]
        dr_ref, dk_ref, dv_ref, dlo_ref = refs[nin:nin + 4]
        dvec = refs[nin + 4:nin + 4 + N_RW_VEC]
        dmat = refs[nin + 4 + N_RW_VEC:nin + 4 + N_RW_VEC + N_RW_MAT]
        ds_ref, pre_ref, y_ref, p_ref, dpre_ref, dvs_ref = refs[-6:]

        @pl.when(pl.program_id(0) == 0)
        def _():
            ds_ref[...] = jnp.zeros_like(ds_ref)
            for x in dvec + dmat:
                x[...] = jnp.zeros_like(x)

        consts = _rw_consts(tb)
        blockdiag = consts[4]
        dlw, dgd = 0.0, 0.0
        for p in range(RW_PAIRS):
            ln = _head_lanes(p)
            outs, prep_vjp = jax.vjp(functools.partial(_rw_prep, consts),
                                     *_rw_prep_args(p, r_ref, k_ref, v_ref, lw_ref, gd_ref, vrefs, mrefs))
            for i, z in enumerate(outs):
                pre_ref[i] = z

            def redo(c, carry, p=p):
                rows = _chunk_rows(c, RW_CHUNK)
                S = st_ref[p, c]
                p_ref[rows, :] = _dot(pre_ref[0, rows, :], S, NT) + pre_ref[1, rows, :]
                y_ref[rows, :] = _dot(pre_ref[2, rows, :], S, NT) + pre_ref[3, rows, :]
                return carry

            lax.fori_loop(0, nsub, redo, 0)
            r_k, ln_w, ln_b = [x[:, ln] for x in vrefs[4:]]
            _, post_vjp = jax.vjp(functools.partial(_rw_post, blockdiag), y_ref[...], r_ref[:, ln], v_ref[:, ln],
                                  pre_ref[7], pre_ref[8], r_k, ln_w, ln_b)
            dy, dr2, dv2, dk2, dg, dr_k, dln_w, dln_b = post_vjp(do_ref[:, ln])
            dpre_ref[3] = dy
            dpre_ref[7] = dk2
            dpre_ref[8] = dg
            dvs_ref[...] = dv2
            for x, gx in zip(dvec[4:], (dr_k, dln_w, dln_b)):
                x[:, ln] += gx
            dpre_ref[6] = jnp.zeros_like(dpre_ref[6])

            def step(i, carry, p=p, ln=ln):
                c = nsub - 1 - i
                rows = _chunk_rows(c, RW_CHUNK)
                row0 = pl.ds(c * RW_CHUNK, 1)
                G = ds_ref[p] * blockdiag
                S = st_ref[p, c]
                dyc = dpre_ref[3, rows, :]
                dP = _dot(pre_ref[4, rows, :], G, NT)
                dpre_ref[0, rows, :] = _dot(dP, S)
                dpre_ref[1, rows, :] = dP
                dpre_ref[2, rows, :] = _dot(dyc, S)
                dpre_ref[4, rows, :] = _dot(p_ref[rows, :], G)
                dpre_ref[5, rows, :] = _dot(v_ref[rows, ln], G)
                dpre_ref[6, row0, :] = jnp.sum(S * G, axis=0, keepdims=True)
                dvs_ref[rows, :] += _dot(pre_ref[5, rows, :], G, NT)
                ds_ref[p] = (G * pre_ref[6, row0, :] + _dot(dP, pre_ref[0, rows, :], TN)
                             + _dot(dyc, pre_ref[2, rows, :], TN))
                return carry

            lax.fori_loop(0, nsub, step, 0)
            grads = prep_vjp(tuple(dpre_ref[i] for i in range(N_PREP_OUT)))
            dr_ref[:, ln] = grads[0] + dr2
            dk_ref[:, ln] = grads[1]
            dv_ref[:, ln] = grads[2] + dvs_ref[...]
            dlw = dlw + grads[3]
            dgd = dgd + grads[4]
            for x, gx in zip(dvec[:4], grads[5:9]):
                x[:, ln] += gx
            for x, gx in zip(dmat, grads[9:]):
                x[p] += gx
        dlo_ref[:, 0:LANES] = dlw
        dlo_ref[:, LANES:2 * LANES] = dgd

    dcol = jax.ShapeDtypeStruct((T, RW_WIDTH), F32)
    dlo_spec = pl.BlockSpec((tb, 2 * LANES), lambda t: (nT - 1 - t, 0))
    blk = pltpu.VMEM((tb, LANES), F32)
    many = pltpu.VMEM((N_PREP_OUT, tb, LANES), F32)
    return pl.pallas_call(
        body, name=name, grid=(nT,),
        in_specs=ins + [st, wide(do_blk)] + [vec] * N_RW_VEC + [mat] * N_RW_MAT,
        out_specs=[wide(0)] * 3 + [dlo_spec] + [vec] * N_RW_VEC + [mat] * N_RW_MAT,
        out_shape=[dcol] * 3 + [jax.ShapeDtypeStruct((T, 2 * LANES), F32)]
        + [jax.ShapeDtypeStruct((1, RW_WIDTH), F32)] * N_RW_VEC
        + [jax.ShapeDtypeStruct((RW_PAIRS, LANES, LANES), F32)] * N_RW_MAT,
        scratch_shapes=[pltpu.VMEM((RW_PAIRS, LANES, LANES), F32), many, blk, blk, many, blk],
    )(rws, rws, rws, rws, rws, states, do, *vecs, *mats)


def _my_index():
    return 4 * lax.axis_index("x") + 2 * lax.axis_index("y") + lax.axis_index("c")


def _peer(bits):
    pos = []
    for name, flip in zip(("x", "y", "c"), bits):
        i = lax.axis_index(name)
        pos.append(1 - i if flip else i)
    return tuple(pos)


def _peer_index(bits):
    x, y, c = _peer(bits)
    return 4 * x + 2 * y + c


def _all_gather(shards, name):
    n = len(shards)
    chips = [(1, 0, 0), (0, 1, 0), (1, 1, 0)]
    sib = (0, 0, 1)

    def body(*refs):
        ins, outs = refs[:n], refs[n:2 * n]
        send_sems, recv_sems, local_sems = refs[2 * n:]

        def rows(k, dev):
            r = ins[k].shape[0]
            return outs[k].at[pl.ds(dev * r, r), :]

        def copy(k, slot, block_dev, to_bits, src=None):
            return pltpu.make_async_remote_copy(
                src_ref=rows(k, block_dev) if src is None else src, dst_ref=rows(k, block_dev),
                send_sem=send_sems.at[k, slot], recv_sem=recv_sems.at[k, slot],
                device_id=_peer(to_bits), device_id_type=MESH_ID)

        me = _my_index()
        started = []
        for k in range(n):
            mine = pltpu.make_async_copy(ins[k], rows(k, me), local_sems.at[k])
            mine.start()
            started.append(mine)
        sends = []
        for k in range(n):
            first = [copy(k, 0, me, sib, src=ins[k])]
            first += [copy(k, 1 + j, me, chip, src=ins[k]) for j, chip in enumerate(chips)]
            for cp in first:
                cp.start()
            sends += first
        for k in range(n):
            for j, chip in enumerate(chips):
                copy(k, 1 + j, _peer_index(chip), chip).wait_recv()
                fwd = copy(k, 4 + j, _peer_index(chip), sib)
                fwd.start()
                sends.append(fwd)
        for k in range(n):
            copy(k, 0, _peer_index(sib), sib).wait_recv()
            for j, chip in enumerate(chips):
                both = (chip[0], chip[1], 1)
                copy(k, 4 + j, _peer_index(both), sib).wait_recv()
        for cp in sends:
            cp.wait_send()
        for cp in started:
            cp.wait()

    any_spec = pl.BlockSpec(memory_space=pl.ANY)
    return pl.pallas_call(
        body, name=name,
        in_specs=[any_spec] * n, out_specs=[any_spec] * n,
        out_shape=[jax.ShapeDtypeStruct((N_DEV * s.shape[0], s.shape[1]), s.dtype) for s in shards],
        scratch_shapes=[pltpu.SemaphoreType.DMA((n, 7)), pltpu.SemaphoreType.DMA((n, 7)),
                        pltpu.SemaphoreType.DMA((n,))],
    )(*shards)


def _exchange(partials, name):
    n = len(partials)
    flips = [(dx, dy, dc) for dx in (0, 1) for dy in (0, 1) for dc in (0, 1)][1:]

    def body(*refs):
        ins, outs = refs[:n], refs[n:2 * n]
        send_sems, recv_sems, local_sems = refs[2 * n:]
        me = _my_index()
        local = []
        for k in range(n):
            cp = pltpu.make_async_copy(ins[k].at[me], outs[k].at[me], local_sems.at[k])
            cp.start()
            local.append(cp)
        copies = []
        for k in range(n):
            for d, bits in enumerate(flips):
                cp = pltpu.make_async_remote_copy(
                    src_ref=ins[k].at[_peer_index(bits)], dst_ref=outs[k].at[me],
                    send_sem=send_sems.at[k, d], recv_sem=recv_sems.at[k, d],
                    device_id=_peer(bits), device_id_type=MESH_ID)
                cp.start()
                copies.append(cp)
        for cp in copies:
            cp.wait_recv()
        for cp in copies:
            cp.wait_send()
        for cp in local:
            cp.wait()

    any_spec = pl.BlockSpec(memory_space=pl.ANY)
    return pl.pallas_call(
        body, name=name,
        in_specs=[any_spec] * n, out_specs=[any_spec] * n,
        out_shape=[jax.ShapeDtypeStruct(p.shape, p.dtype) for p in partials],
        scratch_shapes=[pltpu.SemaphoreType.DMA((n, 7)), pltpu.SemaphoreType.DMA((n, 7)),
                        pltpu.SemaphoreType.DMA((n,))],
    )(*partials)


def _sum_slots(landed, name):
    _, R, C = landed.shape
    tb = _row_tile(R, 128)

    def body(l_ref, o_ref):
        acc = l_ref[0].astype(F32)
        for s in range(1, N_DEV):
            acc = acc + l_ref[s].astype(F32)
        o_ref[...] = acc

    return pl.pallas_call(
        body, name=name, grid=(R // tb,),
        in_specs=[pl.BlockSpec((N_DEV, tb, C), lambda i: (0, i, 0))],
        out_specs=pl.BlockSpec((tb, C), lambda i: (i, 0)),
        out_shape=jax.ShapeDtypeStruct((R, C), F32),
    )(landed)


def _pack_rows(flat_list, width=LANES):
    flat = jnp.concatenate([a.reshape(-1) for a in flat_list])
    n = flat.shape[0]
    rows = -(-n // width)
    rows = -(-rows // 8) * 8
    return jnp.pad(flat, (0, rows * width - n)).reshape(rows, width)


def _unpack(packed, shapes):
    flat = packed.reshape(-1)
    out, off = [], 0
    for s in shapes:
        n = 1
        for d in s:
            n *= d
        out.append(flat[off:off + n].reshape(s))
        off += n
    return out


def kernel(x, norm1_w, w_in, hg_lb_logits, hg_norm_w, rw_shift_mu, rw_w0, rw_w2, rw_a0, rw_a2, rw_g2, rw_k_k, rw_k_a, rw_r_k, rw_ln_w, rw_ln_b, w_out, norm2_w, w_up, conv_w, conv_b, w_down, final_norm_w, loss_target, m_norm1_w, m_w_in, m_hg_lb_logits, m_hg_norm_w, m_rw_shift_mu, m_rw_w0, m_rw_w2, m_rw_a0, m_rw_a2, m_rw_g2, m_rw_k_k, m_rw_k_a, m_rw_r_k, m_rw_ln_w, m_rw_ln_b, m_w_out, m_norm2_w, m_w_up, m_conv_w, m_conv_b, m_w_down, m_final_norm_w, v_norm1_w, v_w_in, v_hg_lb_logits, v_hg_norm_w, v_rw_shift_mu, v_rw_w0, v_rw_w2, v_rw_a0, v_rw_a2, v_rw_g2, v_rw_k_k, v_rw_k_a, v_rw_r_k, v_rw_ln_w, v_rw_ln_b, v_w_out, v_norm2_w, v_w_up, v_conv_w, v_conv_b, v_w_down, v_final_norm_w):
    weights = dict(norm1_w=norm1_w, w_in=w_in, hg_lb_logits=hg_lb_logits, hg_norm_w=hg_norm_w,
                   rw_shift_mu=rw_shift_mu, rw_w0=rw_w0, rw_w2=rw_w2, rw_a0=rw_a0, rw_a2=rw_a2, rw_g2=rw_g2,
                   rw_k_k=rw_k_k, rw_k_a=rw_k_a, rw_r_k=rw_r_k, rw_ln_w=rw_ln_w, rw_ln_b=rw_ln_b, w_out=w_out,
                   norm2_w=norm2_w, w_up=w_up, conv_w=conv_w, conv_b=conv_b, w_down=w_down,
                   final_norm_w=final_norm_w)
    m_in = dict(norm1_w=m_norm1_w, w_in=m_w_in, hg_lb_logits=m_hg_lb_logits, hg_norm_w=m_hg_norm_w,
                rw_shift_mu=m_rw_shift_mu, rw_w0=m_rw_w0, rw_w2=m_rw_w2, rw_a0=m_rw_a0, rw_a2=m_rw_a2,
                rw_g2=m_rw_g2, rw_k_k=m_rw_k_k, rw_k_a=m_rw_k_a, rw_r_k=m_rw_r_k, rw_ln_w=m_rw_ln_w,
                rw_ln_b=m_rw_ln_b, w_out=m_w_out, norm2_w=m_norm2_w, w_up=m_w_up, conv_w=m_conv_w,
                conv_b=m_conv_b, w_down=m_w_down, final_norm_w=m_final_norm_w)
    v_in = dict(norm1_w=v_norm1_w, w_in=v_w_in, hg_lb_logits=v_hg_lb_logits, hg_norm_w=v_hg_norm_w,
                rw_shift_mu=v_rw_shift_mu, rw_w0=v_rw_w0, rw_w2=v_rw_w2, rw_a0=v_rw_a0, rw_a2=v_rw_a2,
                rw_g2=v_rw_g2, rw_k_k=v_rw_k_k, rw_k_a=v_rw_k_a, rw_r_k=v_rw_r_k, rw_ln_w=v_rw_ln_w,
                rw_ln_b=v_rw_ln_b, w_out=v_w_out, norm2_w=v_norm2_w, w_up=v_w_up, conv_w=v_conv_w,
                conv_b=v_conv_b, w_down=v_w_down, final_norm_w=v_final_norm_w)
    names = list(weights)
    sharded_small = ["rw_w2", "rw_a2", "rw_g2", "conv_w"]
    sharded_big = ["w_in", "w_out", "w_up", "w_down"]
    replicated = [n for n in names if n not in sharded_small + sharded_big]

    xs = x[0]
    tgt = loss_target[0]

    small_shard = _pack_rows([weights[n] for n in sharded_small])
    g_win_t, g_wup_t, g_wout, g_wdown, g_small = _all_gather(
        [w_in[0].T.astype(BF16), w_up[0].T.astype(BF16), w_out[0].astype(BF16), w_down[0].astype(BF16),
         small_shard], "gather_weights")
    small_shapes = [weights[n].shape for n in sharded_small]
    per_dev = [_unpack(g_small.reshape(N_DEV, -1)[j], small_shapes) for j in range(N_DEV)]
    w2_full, a2_full, g2_full, convw_full = [jnp.concatenate([per_dev[j][i][0] for j in range(N_DEV)], axis=-1)
                                             for i in range(4)]
    zeros64 = jnp.zeros((RW_PAIRS, 64, LANES), F32)
    by_pair = lambda z: z.reshape(z.shape[0], RW_PAIRS, LANES).transpose(1, 0, 2)
    w2p = jnp.concatenate([by_pair(w2_full), zeros64], axis=1)
    a2p = jnp.concatenate([zeros64, by_pair(a2_full)], axis=1)
    g2p = by_pair(g2_full)

    l0, l1 = hg_lb_logits[0:1], hg_lb_logits[1:2]
    h1 = _rms_fwd(xs, norm1_w, "norm1")
    proj = _mm_nt(h1, g_win_t, "proj_in")
    o_hg, hg_states = _hg_fwd(proj, l0, l1, hg_norm_w, "hgrn2_fwd")
    rws = _shift_fwd(proj, rw_shift_mu, "token_shift")
    rw_vecs = [rw_w0, rw_a0, rw_k_k, rw_k_a, rw_r_k, rw_ln_w, rw_ln_b]
    rw_mats = [w2p, a2p, g2p]
    o_rw, rw_states = _rw_fwd(rws, rw_vecs, rw_mats, "rwkv7_fwd")
    o_mix = jnp.concatenate([o_hg, o_rw], axis=-1).astype(BF16)
    x1 = _mm_nn(o_mix, g_wout, xs, "proj_out")
    h2 = _rms_fwd(x1, norm2_w, "norm2")
    u = _mm_nt(h2, g_wup_t, "ffn_up")
    act = _ffn_act_fwd(u, convw_full, conv_b, "ffn_act")
    x2 = _mm_nn(act, g_wdown, x1, "ffn_down")
    loss_part, dx2, d_final_w = _loss_head(x2, final_norm_w.reshape(1, -1), tgt, "loss_head")

    d_wdown = _mm_tn(act, dx2, 1408, "ffn_down_dw")
    dact = _mm_nt(dx2, g_wdown, "ffn_down_dx", BF16)
    du_g, du_v, dcw_g, dcw_v, dcb_g, dcb_v = _ffn_act_bwd(u, dact, convw_full, conv_b, "ffn_act_bwd")
    du = jnp.concatenate([du_g, du_v], axis=-1)
    d_convw = jnp.concatenate([dcw_g, dcw_v], axis=-1)
    d_convb = jnp.concatenate([dcb_g, dcb_v], axis=-1)
    d_wup_t = _mm_tn(du, h2, 1408, "ffn_up_dw")
    dh2 = _mm_nn(du, g_wup_t, None, "ffn_up_dx")
    dx1, d_norm2 = _rms_bwd(dh2, x1, norm2_w, dx2, "norm2_bwd")
    d_wout = _mm_tn(o_mix, dx1, 512, "proj_out_dw")
    do = _mm_nt(dx1, g_wout, "proj_out_dx")
    dq, df, di, dg, d_l0, d_l1, d_hg_nw = _hg_bwd(proj, hg_states, do, 0, l0, l1, hg_norm_w, "hgrn2_bwd")
    rw_out = _rw_bwd(rws, rw_states, do, 1, rw_vecs, rw_mats, "rwkv7_bwd")
    d_rw_vecs = rw_out[4:4 + N_RW_VEC]
    d_w2p, d_a2p, d_g2p = rw_out[4 + N_RW_VEC:]
    dp_parts, dmu_parts = [], []
    for i, z in enumerate(rw_out[:4]):
        dp, dmu = _shift_bwd(z, proj, rw_shift_mu, i * RW_WIDTH, "token_shift_bwd_%d" % i)
        dp_parts.append(dp)
        dmu_parts.append(dmu)
    d_mu = jnp.concatenate(dmu_parts, axis=-1)
    dproj = jnp.concatenate([dq, df, di, dg] + dp_parts, axis=-1)
    d_win_t = _mm_tn(dproj, h1, 768, "proj_in_dw")
    dh1 = _mm_nn(dproj, g_win_t, None, "proj_in_dx")
    grad_x, d_norm1 = _rms_bwd(dh1, xs, norm1_w, dx1, "norm1_bwd")

    from_pairs = lambda z: z.transpose(1, 0, 2).reshape(z.shape[1], RW_WIDTH)
    d_w2 = from_pairs(d_w2p[:, :64])
    d_a2 = from_pairs(d_a2p[:, 64:])
    d_g2 = from_pairs(d_g2p)
    col_blocks = lambda z: z.reshape(z.shape[0], N_DEV, -1).transpose(1, 0, 2)
    small_part = jnp.stack([
        _pack_rows([col_blocks(d_w2)[j], col_blocks(d_a2)[j], col_blocks(d_g2)[j], col_blocks(d_convw)[j]])
        for j in range(N_DEV)])
    rep_grads = dict(norm1_w=d_norm1, hg_lb_logits=jnp.concatenate([d_l0, d_l1], axis=0), hg_norm_w=d_hg_nw,
                     rw_shift_mu=d_mu, rw_w0=d_rw_vecs[0], rw_a0=d_rw_vecs[1], rw_k_k=d_rw_vecs[2],
                     rw_k_a=d_rw_vecs[3], rw_r_k=d_rw_vecs[4], rw_ln_w=d_rw_vecs[5], rw_ln_b=d_rw_vecs[6],
                     norm2_w=d_norm2, conv_b=d_convb, final_norm_w=d_final_w)
    rep_pack = _pack_rows([loss_part] + [rep_grads[n] for n in replicated])
    rep_part = jnp.broadcast_to(rep_pack[None], (N_DEV,) + rep_pack.shape)
    blocks = lambda z: z.reshape(N_DEV, z.shape[0] // N_DEV, z.shape[1])
    big_parts = [blocks(z).astype(BF16) for z in (d_win_t, d_wup_t, d_wout, d_wdown)]
    landed = _exchange(big_parts + [small_part, rep_part], "exchange_grads")
    sums = [_sum_slots(z, "sum_grads_%d" % i) for i, z in enumerate(landed)]
    g_small_sum = _unpack(sums[4], small_shapes)
    rep_sum = _unpack(sums[5], [(1, 1)] + [weights[n].shape for n in replicated])
    loss = rep_sum[0].reshape(())
    grads = dict(zip(replicated, rep_sum[1:]))
    grads.update(dict(zip(sharded_small, g_small_sum)))
    grads["w_in"] = sums[0].T[None]
    grads["w_up"] = sums[1].T[None]
    grads["w_out"] = sums[2][None]
    grads["w_down"] = sums[3][None]

    delta, new_m, new_v = {}, {}, {}
    for n in sharded_big:
        shp = weights[n].shape
        as2d = lambda z: z.reshape(shp[1], shp[2])
        d, nm, nv = _adamw(as2d(weights[n]), as2d(grads[n]), as2d(m_in[n]), as2d(v_in[n]), "adamw_" + n)
        delta[n], new_m[n], new_v[n] = d.reshape(shp), nm.reshape(shp), nv.reshape(shp)
    small_names = replicated + sharded_small
    packs = [_pack_rows([src[n] for n in small_names]) for src in (weights, grads, m_in, v_in)]
    outs = _adamw(*packs, "adamw_small")
    small_shapes_all = [weights[n].shape for n in small_names]
    for dst, packed in zip((delta, new_m, new_v), outs):
        dst.update(dict(zip(small_names, _unpack(packed, small_shapes_all))))

    return (loss, grad_x[None], *[grads[n] for n in names], *[delta[n] for n in names],
            *[new_m[n] for n in names], *[new_v[n] for n in names])
```

```python
import functools

import jax
import jax.numpy as jnp
from jax import lax
from jax.experimental import pallas as pl
from jax.experimental.pallas import tpu as pltpu

F32 = jnp.float32
BF16 = jnp.bfloat16
HIGHEST = lax.Precision.HIGHEST
SCAN_PRECISION = None
MESH_ID = pl.DeviceIdType.MESH

N_DEV = 8
D_MODEL = 1024
HG_WIDTH = 512
HG_HEAD_DIM = 128
HG_HEADS = 4
RW_WIDTH = 512
RW_PAIRS = 4
RW_HEAD_DIM = 64
HG_COLS = 2048
RW_COLS = 1792
D_FF = 2816
NORM_EPS = 1e-6
RW_GN_EPS = 64e-5
L2_EPS = 1e-12
ADAM_LR, ADAM_B1, ADAM_B2, ADAM_EPS, ADAM_WD, ADAM_STEP = 0.001, 0.9, 0.999, 1e-08, 0.01, 10

HG_CHUNK = 16
RW_CHUNK = 64
SCAN_ROWS = 256
LANES = 128

NN = ((1,), (0,))
NT = ((1,), (1,))
TN = ((0,), (0,))


def _dot(a, b, dims=NN, precision=SCAN_PRECISION):
    if precision is None:
        a, b = a.astype(BF16), b.astype(BF16)
    return lax.dot_general(a, b, (dims, ((), ())), precision=precision, preferred_element_type=F32)


def _iota2(shape, dim):
    return lax.broadcasted_iota(jnp.int32, shape, dim)


def _sigmoid(z):
    return 1.0 / (1.0 + jnp.exp(-z))


def _row_tile(n, want):
    t = min(n, want)
    while n % t:
        t //= 2
    return t


def _rms_fwd(x, w, name):
    T, D = x.shape
    tb = _row_tile(T, 512)

    def body(x_ref, w_ref, h_ref):
        xv = x_ref[...]
        r = lax.rsqrt(jnp.mean(xv * xv, axis=-1, keepdims=True) + NORM_EPS)
        h_ref[...] = (xv * r * w_ref[...]).astype(h_ref.dtype)

    return pl.pallas_call(
        body, name=name, grid=(T // tb,),
        in_specs=[pl.BlockSpec((tb, D), lambda i: (i, 0)), pl.BlockSpec((1, D), lambda i: (0, 0))],
        out_specs=pl.BlockSpec((tb, D), lambda i: (i, 0)),
        out_shape=jax.ShapeDtypeStruct((T, D), BF16),
    )(x, w)


def _rms_bwd(dh, x, w, dres, name):
    T, D = x.shape
    tb = _row_tile(T, 256)

    def body(dh_ref, x_ref, w_ref, dres_ref, dx_ref, dw_ref):
        @pl.when(pl.program_id(0) == 0)
        def _():
            dw_ref[...] = jnp.zeros_like(dw_ref)

        xv = x_ref[...]
        r = lax.rsqrt(jnp.mean(xv * xv, axis=-1, keepdims=True) + NORM_EPS)
        xn = xv * r
        dy = dh_ref[...].astype(F32)
        dxn = dy * w_ref[...]
        dx_ref[...] = dres_ref[...] + r * (dxn - xn * jnp.mean(dxn * xn, axis=-1, keepdims=True))
        dw_ref[...] += jnp.sum(dy * xn, axis=0, keepdims=True)

    row = pl.BlockSpec((tb, D), lambda i: (i, 0))
    vec = pl.BlockSpec((1, D), lambda i: (0, 0))
    return pl.pallas_call(
        body, name=name, grid=(T // tb,),
        in_specs=[row, row, vec, row], out_specs=[row, vec],
        out_shape=[jax.ShapeDtypeStruct((T, D), F32), jax.ShapeDtypeStruct((1, D), F32)],
    )(dh, x, w, dres)


def _mm_nt(a, bt, name, out_dtype=F32):
    T, K = a.shape
    N = bt.shape[0]
    tm = _row_tile(T, 256)

    def body(a_ref, b_ref, o_ref):
        o_ref[...] = _dot(a_ref[...].astype(BF16), b_ref[...].astype(BF16), NT, None).astype(o_ref.dtype)

    return pl.pallas_call(
        body, name=name, grid=(T // tm,),
        in_specs=[pl.BlockSpec((tm, K), lambda i: (i, 0)), pl.BlockSpec((N, K), lambda i: (0, 0))],
        out_specs=pl.BlockSpec((tm, N), lambda i: (i, 0)),
        out_shape=jax.ShapeDtypeStruct((T, N), out_dtype),
    )(a, bt)


def _mm_nn(a, b, res, name, out_dtype=F32):
    T, K = a.shape
    N = b.shape[1]
    tm = _row_tile(T, 256)

    def body(a_ref, b_ref, *rest):
        o_ref = rest[-1]
        acc = _dot(a_ref[...].astype(BF16), b_ref[...].astype(BF16), NN, None)
        if res is not None:
            acc = acc + rest[0][...]
        o_ref[...] = acc.astype(o_ref.dtype)

    in_specs = [pl.BlockSpec((tm, K), lambda i: (i, 0)), pl.BlockSpec((K, N), lambda i: (0, 0))]
    args = [a, b]
    if res is not None:
        in_specs.append(pl.BlockSpec((tm, N), lambda i: (i, 0)))
        args.append(res)
    return pl.pallas_call(
        body, name=name, grid=(T // tm,), in_specs=in_specs,
        out_specs=pl.BlockSpec((tm, N), lambda i: (i, 0)),
        out_shape=jax.ShapeDtypeStruct((T, N), out_dtype),
    )(*args)


def _mm_tn(a, b, tmm, name, out_dtype=F32):
    T, M = a.shape
    N = b.shape[1]
    tk = _row_tile(T, 512)
    nk = T // tk

    def body(a_ref, b_ref, o_ref, acc_ref):
        @pl.when(pl.program_id(1) == 0)
        def _():
            acc_ref[...] = jnp.zeros_like(acc_ref)

        acc_ref[...] += _dot(a_ref[...].astype(BF16), b_ref[...].astype(BF16), TN, None)

        @pl.when(pl.program_id(1) == nk - 1)
        def _():
            o_ref[...] = acc_ref[...].astype(o_ref.dtype)

    return pl.pallas_call(
        body, name=name, grid=(M // tmm, nk),
        in_specs=[pl.BlockSpec((tk, tmm), lambda m, k: (k, m)), pl.BlockSpec((tk, N), lambda m, k: (k, 0))],
        out_specs=pl.BlockSpec((tmm, N), lambda m, k: (m, 0)),
        out_shape=jax.ShapeDtypeStruct((M, N), out_dtype),
        scratch_shapes=[pltpu.VMEM((tmm, N), F32)],
    )(a, b)


def _shift_rows_down(z, n):
    rows = _iota2(z.shape, 0)
    return jnp.where(rows < n, 0.0, pltpu.roll(z, n, 0))


def _shift_rows_up(z, n):
    T = z.shape[0]
    rows = _iota2(z.shape, 0)
    return jnp.where(rows >= T - n, 0.0, pltpu.roll(z, T - n, 0))


def _shift_fwd(proj, mu, name):
    T = proj.shape[0]
    nblk = RW_COLS // LANES
    first = HG_COLS // LANES

    def body(p_ref, mu_ref, o_ref):
        p = p_ref[...]
        o_ref[...] = p + (_shift_rows_down(p, 1) - p) * mu_ref[...]

    return pl.pallas_call(
        body, name=name, grid=(nblk,),
        in_specs=[pl.BlockSpec((T, LANES), lambda j: (0, first + j)), pl.BlockSpec((1, LANES), lambda j: (0, j))],
        out_specs=pl.BlockSpec((T, LANES), lambda j: (0, j)),
        out_shape=jax.ShapeDtypeStruct((T, RW_COLS), F32),
    )(proj, mu)


def _shift_bwd(ds, proj, mu, col0, name):
    T, width = ds.shape
    nblk = width // LANES
    first = (HG_COLS + col0) // LANES
    mu0 = col0 // LANES

    def body(ds_ref, p_ref, mu_ref, dp_ref, dmu_ref):
        dsv = ds_ref[...]
        p = p_ref[...]
        m = mu_ref[...]
        dp_ref[...] = (dsv * (1.0 - m) + _shift_rows_up(dsv * m, 1)).astype(dp_ref.dtype)
        dmu_ref[...] = jnp.sum(dsv * (_shift_rows_down(p, 1) - p), axis=0, keepdims=True)

    return pl.pallas_call(
        body, name=name, grid=(nblk,),
        in_specs=[pl.BlockSpec((T, LANES), lambda j: (0, j)),
                  pl.BlockSpec((T, LANES), lambda j: (0, first + j)),
                  pl.BlockSpec((1, LANES), lambda j: (0, mu0 + j))],
        out_specs=[pl.BlockSpec((T, LANES), lambda j: (0, j)), pl.BlockSpec((1, LANES), lambda j: (0, j))],
        out_shape=[jax.ShapeDtypeStruct((T, width), BF16), jax.ShapeDtypeStruct((1, width), F32)],
    )(ds, proj, mu)


def _conv3(z, w_ref):
    return w_ref[0:1, :] * _shift_rows_down(z, 2) + w_ref[1:2, :] * _shift_rows_down(z, 1) + w_ref[2:3, :] * z


def _ffn_act_fwd(u, conv_w, conv_b, name):
    T = u.shape[0]
    nblk = D_FF // LANES

    def body(ug_ref, uv_ref, wg_ref, wv_ref, bg_ref, bv_ref, act_ref):
        gate = _conv3(ug_ref[...], wg_ref) + bg_ref[...]
        val = _conv3(uv_ref[...], wv_ref) + bv_ref[...]
        act_ref[...] = (gate * _sigmoid(gate) * val).astype(act_ref.dtype)

    col = lambda off: pl.BlockSpec((T, LANES), lambda j: (0, off + j))
    wsp = lambda off: pl.BlockSpec((3, LANES), lambda j: (0, off + j))
    bsp = lambda off: pl.BlockSpec((1, LANES), lambda j: (0, off + j))
    return pl.pallas_call(
        body, name=name, grid=(nblk,),
        in_specs=[col(0), col(nblk), wsp(0), wsp(nblk), bsp(0), bsp(nblk)],
        out_specs=pl.BlockSpec((T, LANES), lambda j: (0, j)),
        out_shape=jax.ShapeDtypeStruct((T, D_FF), BF16),
    )(u, u, conv_w, conv_w, conv_b, conv_b)


def _ffn_act_bwd(u, dact, conv_w, conv_b, name):
    T = u.shape[0]
    nblk = D_FF // LANES

    def conv_bwd(z, dzc, w_ref, du_ref, dw_ref, db_ref):
        du = w_ref[2:3, :] * dzc + w_ref[1:2, :] * _shift_rows_up(dzc, 1) + w_ref[0:1, :] * _shift_rows_up(dzc, 2)
        du_ref[...] = du.astype(du_ref.dtype)
        dw_ref[0:1, :] = jnp.sum(dzc * _shift_rows_down(z, 2), axis=0, keepdims=True)
        dw_ref[1:2, :] = jnp.sum(dzc * _shift_rows_down(z, 1), axis=0, keepdims=True)
        dw_ref[2:3, :] = jnp.sum(dzc * z, axis=0, keepdims=True)
        db_ref[...] = jnp.sum(dzc, axis=0, keepdims=True)

    def body(ug_ref, uv_ref, da_ref, wg_ref, wv_ref, bg_ref, bv_ref,
             dug_ref, duv_ref, dwg_ref, dwv_ref, dbg_ref, dbv_ref):
        ug, uv = ug_ref[...], uv_ref[...]
        gate = _conv3(ug, wg_ref) + bg_ref[...]
        val = _conv3(uv, wv_ref) + bv_ref[...]
        da = da_ref[...].astype(F32)
        sg = _sigmoid(gate)
        dgate = da * val * (sg * (1.0 + gate * (1.0 - sg)))
        dval = da * gate * sg
        conv_bwd(ug, dgate, wg_ref, dug_ref, dwg_ref, dbg_ref)
        conv_bwd(uv, dval, wv_ref, duv_ref, dwv_ref, dbv_ref)

    col = lambda off: pl.BlockSpec((T, LANES), lambda j: (0, off + j))
    wsp = lambda off: pl.BlockSpec((3, LANES), lambda j: (0, off + j))
    bsp = lambda off: pl.BlockSpec((1, LANES), lambda j: (0, off + j))
    half = lambda r, dt: jax.ShapeDtypeStruct((r, D_FF), dt)
    return pl.pallas_call(
        body, name=name, grid=(nblk,),
        in_specs=[col(0), col(nblk), col(0), wsp(0), wsp(nblk), bsp(0), bsp(nblk)],
        out_specs=[col(0), col(0), wsp(0), wsp(0), bsp(0), bsp(0)],
        out_shape=[half(T, BF16), half(T, BF16), half(3, F32), half(3, F32), half(1, F32), half(1, F32)],
    )(u, u, dact, conv_w, conv_w, conv_b, conv_b)


def _loss_head(x2, w, target, name):
    T, D = x2.shape
    tb = _row_tile(T, 256)

    def body(x_ref, w_ref, t_ref, loss_ref, dx_ref, dw_ref):
        @pl.when(pl.program_id(0) == 0)
        def _():
            loss_ref[...] = jnp.zeros_like(loss_ref)
            dw_ref[...] = jnp.zeros_like(dw_ref)

        xv = x_ref[...]
        r = lax.rsqrt(jnp.mean(xv * xv, axis=-1, keepdims=True) + NORM_EPS)
        xn = xv * r
        err = xn * w_ref[...] - t_ref[...]
        row_loss = jnp.sum(err * err, axis=-1, keepdims=True) * (0.5 / D)
        loss_ref[...] += jnp.sum(row_loss, axis=0, keepdims=True)
        dy = err * (1.0 / D)
        dxn = dy * w_ref[...]
        dx_ref[...] = r * (dxn - xn * jnp.mean(dxn * xn, axis=-1, keepdims=True))
        dw_ref[...] += jnp.sum(dy * xn, axis=0, keepdims=True)

    row = pl.BlockSpec((tb, D), lambda i: (i, 0))
    vec = pl.BlockSpec((1, D), lambda i: (0, 0))
    return pl.pallas_call(
        body, name=name, grid=(T // tb,),
        in_specs=[row, vec, row],
        out_specs=[pl.BlockSpec((1, 1), lambda i: (0, 0)), row, vec],
        out_shape=[jax.ShapeDtypeStruct((1, 1), F32), jax.ShapeDtypeStruct((T, D), F32),
                   jax.ShapeDtypeStruct((1, D), F32)],
    )(x2, w, target)


def _adamw(w, g, m, v, name):
    R, C = w.shape
    tb = _row_tile(R, 256) if R % 8 == 0 else R

    def body(w_ref, g_ref, m_ref, v_ref, d_ref, nm_ref, nv_ref):
        gv = g_ref[...]
        nm = ADAM_B1 * m_ref[...] + (1.0 - ADAM_B1) * gv
        nv = ADAM_B2 * v_ref[...] + (1.0 - ADAM_B2) * (gv * gv)
        m_hat = nm / (1.0 - ADAM_B1 ** ADAM_STEP)
        v_hat = nv / (1.0 - ADAM_B2 ** ADAM_STEP)
        d_ref[...] = -ADAM_LR * (m_hat / (jnp.sqrt(v_hat) + ADAM_EPS) + ADAM_WD * w_ref[...])
        nm_ref[...] = nm
        nv_ref[...] = nv

    blk = pl.BlockSpec((tb, C), lambda i: (i, 0))
    sd = jax.ShapeDtypeStruct((R, C), F32)
    return pl.pallas_call(
        body, name=name, grid=(R // tb,), in_specs=[blk] * 4, out_specs=[blk] * 3, out_shape=[sd] * 3,
    )(w, g, m, v)


def _chunk_masks(rows, chunk):
    shift = chunk.bit_length() - 1
    i, j = _iota2((rows, rows), 0), _iota2((rows, rows), 1)
    same = jnp.right_shift(i, shift) == jnp.right_shift(j, shift)
    return same.astype(F32), (same & (j <= i)).astype(F32), (same & (j < i)).astype(F32)


def _head_lanes(h):
    return slice(h * LANES, (h + 1) * LANES)


def _chunk_rows(c, chunk):
    return pl.ds(pl.multiple_of(c * chunk, chunk), chunk)


def _hg_consts(rows):
    same, tril, _ = _chunk_masks(rows, HG_CHUNK)
    shift = HG_CHUNK.bit_length() - 1
    i, j = _iota2((rows, rows), 0), _iota2((rows, rows), 1)
    mid_row = jnp.left_shift(jnp.right_shift(i, shift), shift) + (HG_CHUNK // 2 - 1)
    upto_mid = (same > 0) & (j <= mid_row)
    return jnp.concatenate([tril, same, upto_mid.astype(F32)], axis=0), tril


def _hg_prep(consts, qr, fr, ir, l0, l1):
    sums, tril = consts
    rows = tril.shape[0]
    lb = _sigmoid(l0 - l1)
    f = lb + (1.0 - lb) * _sigmoid(fr)
    q = qr * _sigmoid(qr) * (HG_HEAD_DIM ** -0.5)
    k = 1.0 - f
    g = jnp.log(f)
    acc = _dot(sums, g, NN, HIGHEST)
    a = acc[:rows]
    tot = acc[rows:2 * rows]
    mid = acc[2 * rows:]
    att = _dot(q * jnp.exp(a - mid), k * jnp.exp(mid - a), NT) * tril
    return q * jnp.exp(a), _dot(att, ir), k * jnp.exp(tot - a), jnp.exp(tot)


def _hg_post(o, gr, nw):
    on = o * lax.rsqrt(jnp.mean(o * o, axis=-1, keepdims=True) + NORM_EPS)
    return on * nw * (gr * _sigmoid(gr))


def _hg_specs(T, tb, rev):
    nT = T // tb
    tix = (lambda t: nT - 1 - t) if rev else (lambda t: t)
    col = lambda blk: pl.BlockSpec((tb, HG_WIDTH), lambda t: (tix(t), blk))
    vec = pl.BlockSpec((1, HG_WIDTH), lambda t: (0, 0))
    st = pl.BlockSpec((HG_HEADS, tb // HG_CHUNK, HG_HEAD_DIM, HG_HEAD_DIM), lambda t: (0, tix(t), 0, 0))
    return nT, col, vec, st


def _hg_fwd(proj, l0, l1, nw, name):
    T = proj.shape[0]
    tb = _row_tile(T, SCAN_ROWS)
    nsub = tb // HG_CHUNK
    nT, col, vec, st = _hg_specs(T, tb, False)

    def body(q_ref, f_ref, i_ref, g_ref, l0_ref, l1_ref, nw_ref, o_ref, st_ref, s_ref, qe_ref, kd_ref, dec_ref):
        @pl.when(pl.program_id(0) == 0)
        def _():
            s_ref[...] = jnp.zeros_like(s_ref)

        consts = _hg_consts(tb)
        for h in range(HG_HEADS):
            ln = _head_lanes(h)
            qe, o_intra, kd, dec = _hg_prep(consts, q_ref[:, ln], f_ref[:, ln], i_ref[:, ln],
                                            l0_ref[:, ln], l1_ref[:, ln])
            qe_ref[h], kd_ref[h], dec_ref[h] = qe, kd, dec
            o_ref[:, ln] = o_intra

        def step(c, carry):
            rows = _chunk_rows(c, HG_CHUNK)
            for h in range(HG_HEADS):
                ln = _head_lanes(h)
                S = s_ref[h]
                st_ref[h, c] = S
                o_ref[rows, ln] += _dot(qe_ref[h, rows, :], S, NT)
                s_ref[h] = S * dec_ref[h, pl.ds(c * HG_CHUNK, 1), :] + _dot(i_ref[rows, ln], kd_ref[h, rows, :], TN)
            return carry

        lax.fori_loop(0, nsub, step, 0)
        for h in range(HG_HEADS):
            ln = _head_lanes(h)
            o_ref[:, ln] = _hg_post(o_ref[:, ln], g_ref[:, ln], nw_ref[:, ln])

    blk = pltpu.VMEM((HG_HEADS, tb, LANES), F32)
    return pl.pallas_call(
        body, name=name, grid=(nT,),
        in_specs=[col(0), col(1), col(2), col(3), vec, vec, vec],
        out_specs=[col(0), st],
        out_shape=[jax.ShapeDtypeStruct((T, HG_WIDTH), F32),
                   jax.ShapeDtypeStruct((HG_HEADS, T // HG_CHUNK, HG_HEAD_DIM, HG_HEAD_DIM), F32)],
        scratch_shapes=[pltpu.VMEM((HG_HEADS, HG_HEAD_DIM, HG_HEAD_DIM), F32), blk, blk, blk],
    )(proj, proj, proj, proj, l0, l1, nw)


def _hg_bwd(proj, states, do, do_blk, l0, l1, nw, name):
    T = proj.shape[0]
    tb = _row_tile(T, SCAN_ROWS)
    nsub = tb // HG_CHUNK
    nT, col, vec, st = _hg_specs(T, tb, True)

    def body(q_ref, f_ref, i_ref, g_ref, st_ref, do_ref, l0_ref, l1_ref, nw_ref,
             dq_ref, df_ref, di_ref, dg_ref, dl0_ref, dl1_ref, dnw_ref,
             ds_ref, qe_ref, kd_ref, dec_ref, o_ref, dqe_ref, dkd_ref, ddec_ref, dis_ref):
        @pl.when(pl.program_id(0) == 0)
        def _():
            ds_ref[...] = jnp.zeros_like(ds_ref)
            dl0_ref[...] = jnp.zeros_like(dl0_ref)
            dl1_ref[...] = jnp.zeros_like(dl1_ref)
            dnw_ref[...] = jnp.zeros_like(dnw_ref)

        consts = _hg_consts(tb)
        prep_vjps = []
        for h in range(HG_HEADS):
            ln = _head_lanes(h)
            (qe, o_intra, kd, dec), vjp = jax.vjp(
                functools.partial(_hg_prep, consts), q_ref[:, ln], f_ref[:, ln], i_ref[:, ln],
                l0_ref[:, ln], l1_ref[:, ln])
            prep_vjps.append(vjp)
            qe_ref[h], kd_ref[h], dec_ref[h], o_ref[h] = qe, kd, dec, o_intra

        def redo(c, carry):
            rows = _chunk_rows(c, HG_CHUNK)
            for h in range(HG_HEADS):
                o_ref[h, rows, :] += _dot(qe_ref[h, rows, :], st_ref[h, c], NT)
            return carry

        lax.fori_loop(0, nsub, redo, 0)
        for h in range(HG_HEADS):
            ln = _head_lanes(h)
            _, vjp = jax.vjp(_hg_post, o_ref[h], g_ref[:, ln], nw_ref[:, ln])
            d_o, dgr, dnw = vjp(do_ref[:, ln])
            o_ref[h] = d_o
            dg_ref[:, ln] = dgr.astype(dg_ref.dtype)
            dnw_ref[:, ln] += dnw
        ddec_ref[...] = jnp.zeros_like(ddec_ref)

        def step(i, carry):
            c = nsub - 1 - i
            rows = _chunk_rows(c, HG_CHUNK)
            row0 = pl.ds(c * HG_CHUNK, 1)
            for h in range(HG_HEADS):
                ln = _head_lanes(h)
                G = ds_ref[h]
                S = st_ref[h, c]
                d_o = o_ref[h, rows, :]
                dqe_ref[h, rows, :] = _dot(d_o, S)
                dkd_ref[h, rows, :] = _dot(i_ref[rows, ln], G)
                dis_ref[h, rows, :] = _dot(kd_ref[h, rows, :], G, NT)
                ddec_ref[h, row0, :] = jnp.sum(S * G, axis=0, keepdims=True)
                ds_ref[h] = G * dec_ref[h, row0, :] + _dot(d_o, qe_ref[h, rows, :], TN)
            return carry

        lax.fori_loop(0, nsub, step, 0)
        for h in range(HG_HEADS):
            ln = _head_lanes(h)
            dq, df, di, dl0, dl1 = prep_vjps[h]((dqe_ref[h], o_ref[h], dkd_ref[h], ddec_ref[h]))
            dq_ref[:, ln] = dq.astype(dq_ref.dtype)
            df_ref[:, ln] = df.astype(df_ref.dtype)
            di_ref[:, ln] = (di + dis_ref[h]).astype(di_ref.dtype)
            dl0_ref[:, ln] += dl0
            dl1_ref[:, ln] += dl1

    dcol = jax.ShapeDtypeStruct((T, HG_WIDTH), BF16)
    dvec = jax.ShapeDtypeStruct((1, HG_WIDTH), F32)
    blk = pltpu.VMEM((HG_HEADS, tb, LANES), F32)
    return pl.pallas_call(
        body, name=name, grid=(nT,),
        in_specs=[col(0), col(1), col(2), col(3), st, col(do_blk), vec, vec, vec],
        out_specs=[col(0)] * 4 + [vec] * 3,
        out_shape=[dcol] * 4 + [dvec] * 3,
        scratch_shapes=[pltpu.VMEM((HG_HEADS, HG_HEAD_DIM, HG_HEAD_DIM), F32)] + [blk] * 8,
    )(proj, proj, proj, proj, states, do, l0, l1, nw)


def _rw_consts(rows):
    same, tril, stril = _chunk_masks(rows, RW_CHUNK)
    br, bc = _iota2((LANES, LANES), 0), _iota2((LANES, LANES), 1)
    blockdiag = ((br < RW_HEAD_DIM) == (bc < RW_HEAD_DIM)).astype(F32)
    m0 = (_iota2((1, LANES), 1) < RW_HEAD_DIM).astype(F32)
    return same, tril, stril, blockdiag, m0, 1.0 - m0


def _unit_lower_inverse_impl(low):
    rows = low.shape[0]
    x = low + (_iota2(low.shape, 0) == _iota2(low.shape, 1)).astype(F32)
    p = _dot(low, low)
    n = 4
    while n < RW_CHUNK:
        z = _dot(jnp.concatenate([p, x], axis=0), p)
        p, x = z[:rows], x + z[rows:]
        n *= 2
    return x + _dot(x, p)


@jax.custom_vjp
def _unit_lower_inverse(low):
    return _unit_lower_inverse_impl(low)


def _unit_lower_inverse_fwd(low):
    x = _unit_lower_inverse_impl(low)
    return x, x


def _unit_lower_inverse_bwd(x, dx):
    return (_dot(_dot(x, dx, TN), x, NT),)


_unit_lower_inverse.defvjp(_unit_lower_inverse_fwd, _unit_lower_inverse_bwd)


N_PREP_OUT = 9


def _rw_prep(consts, r, kx, v, lw, gd, w0, a0, k_k, k_a, w2p, a2p, g2):
    same, tril, stril, blockdiag, m0, m1 = consts
    xw = w0 + _dot(jnp.tanh(lw), w2p)
    w = jnp.minimum(xw, 0.0) - jnp.log(1.0 + jnp.exp(-jnp.abs(xw))) - 0.5
    ld = -jnp.exp(w)
    a_s = _sigmoid(a0 + _dot(lw, a2p))
    g = _dot(_sigmoid(gd), g2)
    kk = kx * k_k
    kk = kk / jnp.maximum(jnp.sqrt(_dot(kk * kk, blockdiag)), L2_EPS)
    k2 = kx * (1.0 + (a_s - 1.0) * k_a)
    bv = kk * a_s
    rows = tril.shape[0]
    acc = _dot(jnp.concatenate([tril, same], axis=0), ld, NN, HIGHEST)
    cum, tot = acc[:rows], acc[rows:]
    ecn = jnp.exp(-cum)
    a_t = -kk * jnp.exp(cum - ld)
    b_h = bv * ecn
    k_h = k2 * ecn
    r_t = r * jnp.exp(cum)
    rem = jnp.exp(tot - cum)
    z = _dot(jnp.concatenate([a_t * m0, a_t * m1, r_t * m0, r_t * m1], axis=0),
             jnp.concatenate([b_h, k_h], axis=0), NT)
    W = U = Q = Y0 = 0.0
    for h, m in enumerate((m0, m1)):
        za, zr = z[h * rows:(h + 1) * rows], z[(2 + h) * rows:(3 + h) * rows]
        lak = za[:, rows:] * stril
        mrb = zr[:, :rows] * tril
        mrk = zr[:, rows:] * tril
        tinv = _unit_lower_inverse(za[:, :rows] * stril)
        lv = _dot(jnp.concatenate([lak, mrk], axis=0), v)
        wu = _dot(tinv, jnp.concatenate([a_t * m, lv[:rows]], axis=1))
        w_m, u_m = wu[:, :LANES], m * wu[:, LANES:]
        qy = _dot(mrb, jnp.concatenate([w_m, u_m], axis=1))
        W = W + w_m
        U = U + u_m
        Q = Q + r_t * m + qy[:, :LANES]
        Y0 = Y0 + qy[:, LANES:] + m * lv[rows:]
    return W, U, Q, Y0, bv * rem, k2 * rem, jnp.exp(tot), k2, g


def _rw_post(blockdiag, y, r, v, k2, g, r_k, ln_w, ln_b):
    inv_n = 1.0 / RW_HEAD_DIM
    yc = y - _dot(y, blockdiag) * inv_n
    var = _dot(yc * yc, blockdiag) * inv_n
    yn = yc * lax.rsqrt(var + RW_GN_EPS) * ln_w + ln_b
    bonus = _dot(r * k2 * r_k, blockdiag) * v
    return (yn + bonus) * g


N_RW_VEC = 7
N_RW_MAT = 3


def _rw_specs(T, tb, rev):
    nT = T // tb
    tix = (lambda t: nT - 1 - t) if rev else (lambda t: t)
    wide = lambda blk: pl.BlockSpec((tb, RW_WIDTH), lambda t: (tix(t), blk))
    narrow = lambda blk: pl.BlockSpec((tb, LANES), lambda t: (tix(t), blk))
    vec = pl.BlockSpec((1, RW_WIDTH), lambda t: (0, 0))
    mat = pl.BlockSpec((RW_PAIRS, LANES, LANES), lambda t: (0, 0, 0))
    st = pl.BlockSpec((RW_PAIRS, tb // RW_CHUNK, LANES, LANES), lambda t: (0, tix(t), 0, 0))
    lora0 = 3 * RW_WIDTH // LANES
    ins = [wide(0), wide(1), wide(2), narrow(lora0), narrow(lora0 + 1)]
    return nT, wide, vec, mat, st, ins


def _rw_prep_args(p, r_ref, k_ref, v_ref, lw_ref, gd_ref, vrefs, mrefs):
    ln = _head_lanes(p)
    w0, a0, k_k, k_a = [x[:, ln] for x in vrefs[:4]]
    return (r_ref[:, ln], k_ref[:, ln], v_ref[:, ln], lw_ref[...], gd_ref[...], w0, a0, k_k, k_a,
            *[x[p] for x in mrefs])


def _stack_chunks(ref, top, bottom):
    C = RW_CHUNK
    for c in range(ref.shape[0]):
        ref[c, 0:C, :] = top[c * C:(c + 1) * C]
        ref[c, C:2 * C, :] = bottom[c * C:(c + 1) * C]


def _rw_fwd(rws, vecs, mats, name):
    T = rws.shape[0]
    tb = _row_tile(T, SCAN_ROWS)
    nsub = tb // RW_CHUNK
    C = RW_CHUNK
    nT, wide, vec, mat, st, ins = _rw_specs(T, tb, False)

    def body(*refs):
        r_ref, k_ref, v_ref, lw_ref, gd_ref = refs[:5]
        vrefs = refs[5:5 + N_RW_VEC]
        mrefs = refs[5 + N_RW_VEC:5 + N_RW_VEC + N_RW_MAT]
        o_ref, st_ref, s_ref, wq_ref, uy_ref, bk_ref, misc_ref, y_ref = refs[-8:]

        @pl.when(pl.program_id(0) == 0)
        def _():
            s_ref[...] = jnp.zeros_like(s_ref)

        consts = _rw_consts(tb)
        blockdiag = consts[3]
        for p in range(RW_PAIRS):
            W, U, Q, Y0, Bg, Kg, dec, k2, g = _rw_prep(
                consts, *_rw_prep_args(p, r_ref, k_ref, v_ref, lw_ref, gd_ref, vrefs, mrefs))
            _stack_chunks(wq_ref.at[p], W, Q)
            _stack_chunks(uy_ref.at[p], U, Y0)
            _stack_chunks(bk_ref.at[p], Bg, Kg)
            misc_ref[0, p], misc_ref[1, p], misc_ref[2, p] = dec, k2, g

        def step(c, carry):
            rows = _chunk_rows(c, C)
            for p in range(RW_PAIRS):
                S = s_ref[p]
                st_ref[p, c] = S
                py = _dot(wq_ref[p, c], S, NT) + uy_ref[p, c]
                y_ref[p, rows, :] = py[C:]
                pv = jnp.concatenate([py[:C], v_ref[rows, _head_lanes(p)]], axis=0)
                s_ref[p] = (S * misc_ref[0, p, pl.ds(c * C, 1), :] + _dot(pv, bk_ref[p, c], TN)) * blockdiag
            return carry

        lax.fori_loop(0, nsub, step, 0)
        for p in range(RW_PAIRS):
            ln = _head_lanes(p)
            r_k, ln_w, ln_b = [x[:, ln] for x in vrefs[4:]]
            o_ref[:, ln] = _rw_post(blockdiag, y_ref[p], r_ref[:, ln], v_ref[:, ln], misc_ref[1, p], misc_ref[2, p],
                                    r_k, ln_w, ln_b)

    stacked = pltpu.VMEM((RW_PAIRS, nsub, 2 * C, LANES), F32)
    return pl.pallas_call(
        body, name=name, grid=(nT,),
        in_specs=ins + [vec] * N_RW_VEC + [mat] * N_RW_MAT,
        out_specs=[wide(0), st],
        out_shape=[jax.ShapeDtypeStruct((T, RW_WIDTH), F32),
                   jax.ShapeDtypeStruct((RW_PAIRS, T // RW_CHUNK, LANES, LANES), F32)],
        scratch_shapes=[pltpu.VMEM((RW_PAIRS, LANES, LANES), F32), stacked, stacked, stacked,
                        pltpu.VMEM((3, RW_PAIRS, tb, LANES), F32), pltpu.VMEM((RW_PAIRS, tb, LANES), F32)],
    )(rws, rws, rws, rws, rws, *vecs, *mats)


def _rw_bwd(rws, states, do, do_blk, vecs, mats, name):
    T = rws.shape[0]
    tb = _row_tile(T, SCAN_ROWS)
    nsub = tb // RW_CHUNK
    C = RW_CHUNK
    nT, wide, vec, mat, st, ins = _rw_specs(T, tb, True)
    nin = 5 + 1 + 1 + N_RW_VEC + N_RW_MAT

    def body(*refs):
        r_ref, k_ref, v_ref, lw_ref, gd_ref = refs[:5]
        st_ref, do_ref = refs[5], refs[6]
        vrefs = refs[7:7 + N_RW_VEC]
        mrefs = refs[7 + N_RW_VEC:nin]
        dr_ref, dk_ref, dv_ref, dlo_ref = refs[nin:nin + 4]
        dvec = refs[nin + 4:nin + 4 + N_RW_VEC]
        dmat = refs[nin + 4 + N_RW_VEC:nin + 4 + N_RW_VEC + N_RW_MAT]
        ds_ref, wq_ref, uy_ref, bk_ref, pv_ref, dec_ref, y_ref, dpre_ref, dvs_ref = refs[-9:]

        @pl.when(pl.program_id(0) == 0)
        def _():
            ds_ref[...] = jnp.zeros_like(ds_ref)
            for x in dvec + dmat:
                x[...] = jnp.zeros_like(x)

        consts = _rw_consts(tb)
        blockdiag = consts[3]
        dlw, dgd = 0.0, 0.0
        for p in range(RW_PAIRS):
            ln = _head_lanes(p)
            (W, U, Q, Y0, Bg, Kg, dec, k2, g), prep_vjp = jax.vjp(
                functools.partial(_rw_prep, consts),
                *_rw_prep_args(p, r_ref, k_ref, v_ref, lw_ref, gd_ref, vrefs, mrefs))
            _stack_chunks(wq_ref, W, Q)
            _stack_chunks(uy_ref, U, Y0)
            _stack_chunks(bk_ref, Bg, Kg)
            dec_ref[...] = dec

            def redo(c, carry, p=p, ln=ln):
                rows = _chunk_rows(c, C)
                py = _dot(wq_ref[c], st_ref[p, c], NT) + uy_ref[c]
                y_ref[rows, :] = py[C:]
                pv_ref[c, 0:C, :] = py[:C]
                pv_ref[c, C:2 * C, :] = v_ref[rows, ln]
                return carry

            lax.fori_loop(0, nsub, redo, 0)
            r_k, ln_w, ln_b = [x[:, ln] for x in vrefs[4:]]
            _, post_vjp = jax.vjp(functools.partial(_rw_post, blockdiag), y_ref[...], r_ref[:, ln], v_ref[:, ln],
                                  k2, g, r_k, ln_w, ln_b)
            dy, dr2, dv2, dk2, dg, dr_k, dln_w, dln_b = post_vjp(do_ref[:, ln])
            dpre_ref[3] = dy
            dvs_ref[...] = dv2
            for x, gx in zip(dvec[4:], (dr_k, dln_w, dln_b)):
                x[:, ln] += gx
            dpre_ref[6] = jnp.zeros_like(dpre_ref[6])

            def step(i, carry, p=p):
                c = nsub - 1 - i
                rows = _chunk_rows(c, C)
                row0 = pl.ds(c * C, 1)
                G = ds_ref[p] * blockdiag
                S = st_ref[p, c]
                t1 = _dot(bk_ref[c], G, NT)
                dpy = jnp.concatenate([t1[:C], dpre_ref[3, rows, :]], axis=0)
                t2 = _dot(dpy, S)
                t3 = _dot(pv_ref[c], G)
                dvs_ref[rows, :] += t1[C:]
                dpre_ref[0, rows, :] = t2[:C]
                dpre_ref[1, rows, :] = t1[:C]
                dpre_ref[2, rows, :] = t2[C:]
                dpre_ref[4, rows, :] = t3[:C]
                dpre_ref[5, rows, :] = t3[C:]
                dpre_ref[6, row0, :] = jnp.sum(S * G, axis=0, keepdims=True)
                ds_ref[p] = G * dec_ref[row0, :] + _dot(dpy, wq_ref[c], TN)
                return carry

            lax.fori_loop(0, nsub, step, 0)
            grads = prep_vjp(tuple(dpre_ref[i] for i in range(7)) + (dk2, dg))
            dr_ref[:, ln] = grads[0] + dr2
            dk_ref[:, ln] = grads[1]
            dv_ref[:, ln] = grads[2] + dvs_ref[...]
            dlw = dlw + grads[3]
            dgd = dgd + grads[4]
            for x, gx in zip(dvec[:4], grads[5:9]):
                x[:, ln] += gx
            for x, gx in zip(dmat, grads[9:]):
                x[p] += gx
        dlo_ref[:, 0:LANES] = dlw
        dlo_ref[:, LANES:2 * LANES] = dgd

    dcol = jax.ShapeDtypeStruct((T, RW_WIDTH), F32)
    dlo_spec = pl.BlockSpec((tb, 2 * LANES), lambda t: (nT - 1 - t, 0))
    blk = pltpu.VMEM((tb, LANES), F32)
    stacked = pltpu.VMEM((nsub, 2 * C, LANES), F32)
    return pl.pallas_call(
        body, name=name, grid=(nT,),
        in_specs=ins + [st, wide(do_blk)] + [vec] * N_RW_VEC + [mat] * N_RW_MAT,
        out_specs=[wide(0)] * 3 + [dlo_spec] + [vec] * N_RW_VEC + [mat] * N_RW_MAT,
        out_shape=[dcol] * 3 + [jax.ShapeDtypeStruct((T, 2 * LANES), F32)]
        + [jax.ShapeDtypeStruct((1, RW_WIDTH), F32)] * N_RW_VEC
        + [jax.ShapeDtypeStruct((RW_PAIRS, LANES, LANES), F32)] * N_RW_MAT,
        scratch_shapes=[pltpu.VMEM((RW_PAIRS, LANES, LANES), F32), stacked, stacked, stacked, stacked, blk, blk,
                        pltpu.VMEM((7, tb, LANES), F32), blk],
    )(rws, rws, rws, rws, rws, states, do, *vecs, *mats)


def _my_index():
    return 4 * lax.axis_index("x") + 2 * lax.axis_index("y") + lax.axis_index("c")


def _peer(bits):
    pos = []
    for name, flip in zip(("x", "y", "c"), bits):
        i = lax.axis_index(name)
        pos.append(1 - i if flip else i)
    return tuple(pos)


def _peer_index(bits):
    x, y, c = _peer(bits)
    return 4 * x + 2 * y + c


def _all_gather(shards, name):
    n = len(shards)
    chips = [(1, 0, 0), (0, 1, 0), (1, 1, 0)]
    sib = (0, 0, 1)

    def body(*refs):
        ins, outs = refs[:n], refs[n:2 * n]
        send_sems, recv_sems, local_sems = refs[2 * n:]

        def rows(k, dev):
            r = ins[k].shape[0]
            return outs[k].at[pl.ds(dev * r, r), :]

        def copy(k, slot, block_dev, to_bits, src=None):
            return pltpu.make_async_remote_copy(
                src_ref=rows(k, block_dev) if src is None else src, dst_ref=rows(k, block_dev),
                send_sem=send_sems.at[k, slot], recv_sem=recv_sems.at[k, slot],
                device_id=_peer(to_bits), device_id_type=MESH_ID)

        me = _my_index()
        started = []
        for k in range(n):
            mine = pltpu.make_async_copy(ins[k], rows(k, me), local_sems.at[k])
            mine.start()
            started.append(mine)
        sends = []
        for k in range(n):
            first = [copy(k, 0, me, sib, src=ins[k])]
            first += [copy(k, 1 + j, me, chip, src=ins[k]) for j, chip in enumerate(chips)]
            for cp in first:
                cp.start()
            sends += first
        for k in range(n):
            for j, chip in enumerate(chips):
                copy(k, 1 + j, _peer_index(chip), chip).wait_recv()
                fwd = copy(k, 4 + j, _peer_index(chip), sib)
                fwd.start()
                sends.append(fwd)
        for k in range(n):
            copy(k, 0, _peer_index(sib), sib).wait_recv()
            for j, chip in enumerate(chips):
                both = (chip[0], chip[1], 1)
                copy(k, 4 + j, _peer_index(both), sib).wait_recv()
        for cp in sends:
            cp.wait_send()
        for cp in started:
            cp.wait()

    any_spec = pl.BlockSpec(memory_space=pl.ANY)
    return pl.pallas_call(
        body, name=name,
        in_specs=[any_spec] * n, out_specs=[any_spec] * n,
        out_shape=[jax.ShapeDtypeStruct((N_DEV * s.shape[0], s.shape[1]), s.dtype) for s in shards],
        scratch_shapes=[pltpu.SemaphoreType.DMA((n, 7)), pltpu.SemaphoreType.DMA((n, 7)),
                        pltpu.SemaphoreType.DMA((n,))],
    )(*shards)


def _exchange(partials, name):
    n = len(partials)
    flips = [(dx, dy, dc) for dx in (0, 1) for dy in (0, 1) for dc in (0, 1)][1:]

    def body(*refs):
        ins, outs = refs[:n], refs[n:2 * n]
        send_sems, recv_sems, local_sems = refs[2 * n:]
        me = _my_index()
        local = []
        for k in range(n):
            cp = pltpu.make_async_copy(ins[k].at[me], outs[k].at[me], local_sems.at[k])
            cp.start()
            local.append(cp)
        copies = []
        for k in range(n):
            for d, bits in enumerate(flips):
                cp = pltpu.make_async_remote_copy(
                    src_ref=ins[k].at[_peer_index(bits)], dst_ref=outs[k].at[me],
                    send_sem=send_sems.at[k, d], recv_sem=recv_sems.at[k, d],
                    device_id=_peer(bits), device_id_type=MESH_ID)
                cp.start()
                copies.append(cp)
        for cp in copies:
            cp.wait_recv()
        for cp in copies:
            cp.wait_send()
        for cp in local:
            cp.wait()

    any_spec = pl.BlockSpec(memory_space=pl.ANY)
    return pl.pallas_call(
        body, name=name,
        in_specs=[any_spec] * n, out_specs=[any_spec] * n,
        out_shape=[jax.ShapeDtypeStruct(p.shape, p.dtype) for p in partials],
        scratch_shapes=[pltpu.SemaphoreType.DMA((n, 7)), pltpu.SemaphoreType.DMA((n, 7)),
                        pltpu.SemaphoreType.DMA((n,))],
    )(*partials)


def _sum_slots(landed, name):
    _, R, C = landed.shape
    tb = _row_tile(R, 128)

    def body(l_ref, o_ref):
        acc = l_ref[0].astype(F32)
        for s in range(1, N_DEV):
            acc = acc + l_ref[s].astype(F32)
        o_ref[...] = acc

    return pl.pallas_call(
        body, name=name, grid=(R // tb,),
        in_specs=[pl.BlockSpec((N_DEV, tb, C), lambda i: (0, i, 0))],
        out_specs=pl.BlockSpec((tb, C), lambda i: (i, 0)),
        out_shape=jax.ShapeDtypeStruct((R, C), F32),
    )(landed)


def _pack_rows(flat_list, width=LANES):
    flat = jnp.concatenate([a.reshape(-1) for a in flat_list])
    n = flat.shape[0]
    rows = -(-n // width)
    rows = -(-rows // 8) * 8
    return jnp.pad(flat, (0, rows * width - n)).reshape(rows, width)


def _unpack(packed, shapes):
    flat = packed.reshape(-1)
    out, off = [], 0
    for s in shapes:
        n = 1
        for d in s:
            n *= d
        out.append(flat[off:off + n].reshape(s))
        off += n
    return out


def kernel(x, norm1_w, w_in, hg_lb_logits, hg_norm_w, rw_shift_mu, rw_w0, rw_w2, rw_a0, rw_a2, rw_g2, rw_k_k, rw_k_a, rw_r_k, rw_ln_w, rw_ln_b, w_out, norm2_w, w_up, conv_w, conv_b, w_down, final_norm_w, loss_target, m_norm1_w, m_w_in, m_hg_lb_logits, m_hg_norm_w, m_rw_shift_mu, m_rw_w0, m_rw_w2, m_rw_a0, m_rw_a2, m_rw_g2, m_rw_k_k, m_rw_k_a, m_rw_r_k, m_rw_ln_w, m_rw_ln_b, m_w_out, m_norm2_w, m_w_up, m_conv_w, m_conv_b, m_w_down, m_final_norm_w, v_norm1_w, v_w_in, v_hg_lb_logits, v_hg_norm_w, v_rw_shift_mu, v_rw_w0, v_rw_w2, v_rw_a0, v_rw_a2, v_rw_g2, v_rw_k_k, v_rw_k_a, v_rw_r_k, v_rw_ln_w, v_rw_ln_b, v_w_out, v_norm2_w, v_w_up, v_conv_w, v_conv_b, v_w_down, v_final_norm_w):
    weights = dict(norm1_w=norm1_w, w_in=w_in, hg_lb_logits=hg_lb_logits, hg_norm_w=hg_norm_w,
                   rw_shift_mu=rw_shift_mu, rw_w0=rw_w0, rw_w2=rw_w2, rw_a0=rw_a0, rw_a2=rw_a2, rw_g2=rw_g2,
                   rw_k_k=rw_k_k, rw_k_a=rw_k_a, rw_r_k=rw_r_k, rw_ln_w=rw_ln_w, rw_ln_b=rw_ln_b, w_out=w_out,
                   norm2_w=norm2_w, w_up=w_up, conv_w=conv_w, conv_b=conv_b, w_down=w_down,
                   final_norm_w=final_norm_w)
    m_in = dict(norm1_w=m_norm1_w, w_in=m_w_in, hg_lb_logits=m_hg_lb_logits, hg_norm_w=m_hg_norm_w,
                rw_shift_mu=m_rw_shift_mu, rw_w0=m_rw_w0, rw_w2=m_rw_w2, rw_a0=m_rw_a0, rw_a2=m_rw_a2,
                rw_g2=m_rw_g2, rw_k_k=m_rw_k_k, rw_k_a=m_rw_k_a, rw_r_k=m_rw_r_k, rw_ln_w=m_rw_ln_w,
                rw_ln_b=m_rw_ln_b, w_out=m_w_out, norm2_w=m_norm2_w, w_up=m_w_up, conv_w=m_conv_w,
                conv_b=m_conv_b, w_down=m_w_down, final_norm_w=m_final_norm_w)
    v_in = dict(norm1_w=v_norm1_w, w_in=v_w_in, hg_lb_logits=v_hg_lb_logits, hg_norm_w=v_hg_norm_w,
                rw_shift_mu=v_rw_shift_mu, rw_w0=v_rw_w0, rw_w2=v_rw_w2, rw_a0=v_rw_a0, rw_a2=v_rw_a2,
                rw_g2=v_rw_g2, rw_k_k=v_rw_k_k, rw_k_a=v_rw_k_a, rw_r_k=v_rw_r_k, rw_ln_w=v_rw_ln_w,
                rw_ln_b=v_rw_ln_b, w_out=v_w_out, norm2_w=v_norm2_w, w_up=v_w_up, conv_w=v_conv_w,
                conv_b=v_conv_b, w_down=v_w_down, final_norm_w=v_final_norm_w)
    names = list(weights)
    sharded_small = ["rw_w2", "rw_a2", "rw_g2", "conv_w"]
    sharded_big = ["w_in", "w_out", "w_up", "w_down"]
    replicated = [n for n in names if n not in sharded_small + sharded_big]

    xs = x[0]
    tgt = loss_target[0]

    small_shard = _pack_rows([weights[n] for n in sharded_small])
    g_win_t, g_wup_t, g_wout, g_wdown, g_small = _all_gather(
        [w_in[0].T.astype(BF16), w_up[0].T.astype(BF16), w_out[0].astype(BF16), w_down[0].astype(BF16),
         small_shard], "gather_weights")
    small_shapes = [weights[n].shape for n in sharded_small]
    per_dev = [_unpack(g_small.reshape(N_DEV, -1)[j], small_shapes) for j in range(N_DEV)]
    w2_full, a2_full, g2_full, convw_full = [jnp.concatenate([per_dev[j][i][0] for j in range(N_DEV)], axis=-1)
                                             for i in range(4)]
    zeros64 = jnp.zeros((RW_PAIRS, 64, LANES), F32)
    by_pair = lambda z: z.reshape(z.shape[0], RW_PAIRS, LANES).transpose(1, 0, 2)
    w2p = jnp.concatenate([by_pair(w2_full), zeros64], axis=1)
    a2p = jnp.concatenate([zeros64, by_pair(a2_full)], axis=1)
    g2p = by_pair(g2_full)

    l0, l1 = hg_lb_logits[0:1], hg_lb_logits[1:2]
    h1 = _rms_fwd(xs, norm1_w, "norm1")
    proj = _mm_nt(h1, g_win_t, "proj_in")
    o_hg, hg_states = _hg_fwd(proj, l0, l1, hg_norm_w, "hgrn2_fwd")
    rws = _shift_fwd(proj, rw_shift_mu, "token_shift")
    rw_vecs = [rw_w0, rw_a0, rw_k_k, rw_k_a, rw_r_k, rw_ln_w, rw_ln_b]
    rw_mats = [w2p, a2p, g2p]
    o_rw, rw_states = _rw_fwd(rws, rw_vecs, rw_mats, "rwkv7_fwd")
    o_mix = jnp.concatenate([o_hg, o_rw], axis=-1).astype(BF16)
    x1 = _mm_nn(o_mix, g_wout, xs, "proj_out")
    h2 = _rms_fwd(x1, norm2_w, "norm2")
    u = _mm_nt(h2, g_wup_t, "ffn_up")
    act = _ffn_act_fwd(u, convw_full, conv_b, "ffn_act")
    x2 = _mm_nn(act, g_wdown, x1, "ffn_down")
    loss_part, dx2, d_final_w = _loss_head(x2, final_norm_w.reshape(1, -1), tgt, "loss_head")

    d_wdown = _mm_tn(act, dx2, 1408, "ffn_down_dw", BF16)
    dact = _mm_nt(dx2, g_wdown, "ffn_down_dx", BF16)
    du_g, du_v, dcw_g, dcw_v, dcb_g, dcb_v = _ffn_act_bwd(u, dact, convw_full, conv_b, "ffn_act_bwd")
    du = jnp.concatenate([du_g, du_v], axis=-1)
    d_convw = jnp.concatenate([dcw_g, dcw_v], axis=-1)
    d_convb = jnp.concatenate([dcb_g, dcb_v], axis=-1)
    d_wup_t = _mm_tn(du, h2, 1408, "ffn_up_dw", BF16)
    dh2 = _mm_nn(du, g_wup_t, None, "ffn_up_dx")
    dx1, d_norm2 = _rms_bwd(dh2, x1, norm2_w, dx2, "norm2_bwd")
    d_wout = _mm_tn(o_mix, dx1, 512, "proj_out_dw", BF16)
    do = _mm_nt(dx1, g_wout, "proj_out_dx")
    dq, df, di, dg, d_l0, d_l1, d_hg_nw = _hg_bwd(proj, hg_states, do, 0, l0, l1, hg_norm_w, "hgrn2_bwd")
    rw_out = _rw_bwd(rws, rw_states, do, 1, rw_vecs, rw_mats, "rwkv7_bwd")
    d_rw_vecs = rw_out[4:4 + N_RW_VEC]
    d_w2p, d_a2p, d_g2p = rw_out[4 + N_RW_VEC:]
    dp_parts, dmu_parts = [], []
    for i, z in enumerate(rw_out[:4]):
        dp, dmu = _shift_bwd(z, proj, rw_shift_mu, i * RW_WIDTH, "token_shift_bwd_%d" % i)
        dp_parts.append(dp)
        dmu_parts.append(dmu)
    d_mu = jnp.concatenate(dmu_parts, axis=-1)
    dproj = jnp.concatenate([dq, df, di, dg] + dp_parts, axis=-1)
    d_win_t = _mm_tn(dproj, h1, 768, "proj_in_dw", BF16)
    dh1 = _mm_nn(dproj, g_win_t, None, "proj_in_dx")
    grad_x, d_norm1 = _rms_bwd(dh1, xs, norm1_w, dx1, "norm1_bwd")

    from_pairs = lambda z: z.transpose(1, 0, 2).reshape(z.shape[1], RW_WIDTH)
    d_w2 = from_pairs(d_w2p[:, :64])
    d_a2 = from_pairs(d_a2p[:, 64:])
    d_g2 = from_pairs(d_g2p)
    col_blocks = lambda z: z.reshape(z.shape[0], N_DEV, -1).transpose(1, 0, 2)
    small_part = jnp.stack([
        _pack_rows([col_blocks(d_w2)[j], col_blocks(d_a2)[j], col_blocks(d_g2)[j], col_blocks(d_convw)[j]])
        for j in range(N_DEV)])
    rep_grads = dict(norm1_w=d_norm1, hg_lb_logits=jnp.concatenate([d_l0, d_l1], axis=0), hg_norm_w=d_hg_nw,
                     rw_shift_mu=d_mu, rw_w0=d_rw_vecs[0], rw_a0=d_rw_vecs[1], rw_k_k=d_rw_vecs[2],
                     rw_k_a=d_rw_vecs[3], rw_r_k=d_rw_vecs[4], rw_ln_w=d_rw_vecs[5], rw_ln_b=d_rw_vecs[6],
                     norm2_w=d_norm2, conv_b=d_convb, final_norm_w=d_final_w)
    rep_pack = _pack_rows([loss_part] + [rep_grads[n] for n in replicated])
    rep_part = jnp.broadcast_to(rep_pack[None], (N_DEV,) + rep_pack.shape)
    blocks = lambda z: z.reshape(N_DEV, z.shape[0] // N_DEV, z.shape[1])
    big_parts = [blocks(z) for z in (d_win_t, d_wup_t, d_wout, d_wdown)]
    landed = _exchange(big_parts + [small_part, rep_part], "exchange_grads")
    sums = [_sum_slots(z, "sum_grads_%d" % i) for i, z in enumerate(landed)]
    g_small_sum = _unpack(sums[4], small_shapes)
    rep_sum = _unpack(sums[5], [(1, 1)] + [weights[n].shape for n in replicated])
    loss = rep_sum[0].reshape(())
    grads = dict(zip(replicated, rep_sum[1:]))
    grads.update(dict(zip(sharded_small, g_small_sum)))
    grads["w_in"] = sums[0].T[None]
    grads["w_up"] = sums[1].T[None]
    grads["w_out"] = sums[2][None]
    grads["w_down"] = sums[3][None]

    delta, new_m, new_v = {}, {}, {}
    for n in sharded_big:
        shp = weights[n].shape
        as2d = lambda z: z.reshape(shp[1], shp[2])
        d, nm, nv = _adamw(as2d(weights[n]), as2d(grads[n]), as2d(m_in[n]), as2d(v_in[n]), "adamw_" + n)
        delta[n], new_m[n], new_v[n] = d.reshape(shp), nm.reshape(shp), nv.reshape(shp)
    small_names = replicated + sharded_small
    packs = [_pack_rows([src[n] for n in small_names]) for src in (weights, grads, m_in, v_in)]
    outs = _adamw(*packs, "adamw_small")
    small_shapes_all = [weights[n].shape for n in small_names]
    for dst, packed in zip((delta, new_m, new_v), outs):
        dst.update(dict(zip(small_names, _unpack(packed, small_shapes_all))))

    return (loss, grad_x[None], *[grads[n] for n in names], *[delta[n] for n in names],
            *[new_m[n] for n in names], *[new_v[n] for n in names])
```

```python
import functools

import jax
import jax.numpy as jnp
from jax import lax
from jax.experimental import pallas as pl
from jax.experimental.pallas import tpu as pltpu

F32 = jnp.float32
BF16 = jnp.bfloat16
HIGHEST = lax.Precision.HIGHEST
SCAN_PRECISION = None
MESH_ID = pl.DeviceIdType.MESH

N_DEV = 8
D_MODEL = 1024
HG_WIDTH = 512
HG_HEAD_DIM = 128
HG_HEADS = 4
RW_WIDTH = 512
RW_PAIRS = 4
RW_HEAD_DIM = 64
HG_COLS = 2048
RW_COLS = 1792
D_FF = 2816
NORM_EPS = 1e-6
RW_GN_EPS = 64e-5
L2_EPS = 1e-12
ADAM_LR, ADAM_B1, ADAM_B2, ADAM_EPS, ADAM_WD, ADAM_STEP = 0.001, 0.9, 0.999, 1e-08, 0.01, 10

HG_CHUNK = 16
RW_CHUNK = 64
SCAN_ROWS = 256
LANES = 128

NN = ((1,), (0,))
NT = ((1,), (1,))
TN = ((0,), (0,))


def _dot(a, b, dims=NN, precision=SCAN_PRECISION):
    if precision is None:
        a, b = a.astype(BF16), b.astype(BF16)
    return lax.dot_general(a, b, (dims, ((), ())), precision=precision, preferred_element_type=F32)


def _iota2(shape, dim):
    return lax.broadcasted_iota(jnp.int32, shape, dim)


def _sigmoid(z):
    return 1.0 / (1.0 + jnp.exp(-z))


def _row_tile(n, want):
    t = min(n, want)
    while n % t:
        t //= 2
    return t


def _rms_fwd(x, w, name):
    T, D = x.shape
    tb = _row_tile(T, 512)

    def body(x_ref, w_ref, h_ref):
        xv = x_ref[...]
        r = lax.rsqrt(jnp.mean(xv * xv, axis=-1, keepdims=True) + NORM_EPS)
        h_ref[...] = (xv * r * w_ref[...]).astype(h_ref.dtype)

    return pl.pallas_call(
        body, name=name, grid=(T // tb,),
        in_specs=[pl.BlockSpec((tb, D), lambda i: (i, 0)), pl.BlockSpec((1, D), lambda i: (0, 0))],
        out_specs=pl.BlockSpec((tb, D), lambda i: (i, 0)),
        out_shape=jax.ShapeDtypeStruct((T, D), BF16),
    )(x, w)


def _rms_bwd(dh, x, w, dres, name):
    T, D = x.shape
    tb = _row_tile(T, 256)

    def body(dh_ref, x_ref, w_ref, dres_ref, dx_ref, dw_ref):
        @pl.when(pl.program_id(0) == 0)
        def _():
            dw_ref[...] = jnp.zeros_like(dw_ref)

        xv = x_ref[...]
        r = lax.rsqrt(jnp.mean(xv * xv, axis=-1, keepdims=True) + NORM_EPS)
        xn = xv * r
        dy = dh_ref[...].astype(F32)
        dxn = dy * w_ref[...]
        dx_ref[...] = dres_ref[...] + r * (dxn - xn * jnp.mean(dxn * xn, axis=-1, keepdims=True))
        dw_ref[...] += jnp.sum(dy * xn, axis=0, keepdims=True)

    row = pl.BlockSpec((tb, D), lambda i: (i, 0))
    vec = pl.BlockSpec((1, D), lambda i: (0, 0))
    return pl.pallas_call(
        body, name=name, grid=(T // tb,),
        in_specs=[row, row, vec, row], out_specs=[row, vec],
        out_shape=[jax.ShapeDtypeStruct((T, D), F32), jax.ShapeDtypeStruct((1, D), F32)],
    )(dh, x, w, dres)


def _mm_nt(a, bt, name, out_dtype=F32):
    T, K = a.shape
    N = bt.shape[0]
    tm = _row_tile(T, 256)

    def body(a_ref, b_ref, o_ref):
        o_ref[...] = _dot(a_ref[...].astype(BF16), b_ref[...].astype(BF16), NT, None).astype(o_ref.dtype)

    return pl.pallas_call(
        body, name=name, grid=(T // tm,),
        in_specs=[pl.BlockSpec((tm, K), lambda i: (i, 0)), pl.BlockSpec((N, K), lambda i: (0, 0))],
        out_specs=pl.BlockSpec((tm, N), lambda i: (i, 0)),
        out_shape=jax.ShapeDtypeStruct((T, N), out_dtype),
    )(a, bt)


def _mm_nn(a, b, res, name, out_dtype=F32):
    T, K = a.shape
    N = b.shape[1]
    tm = _row_tile(T, 256)

    def body(a_ref, b_ref, *rest):
        o_ref = rest[-1]
        acc = _dot(a_ref[...].astype(BF16), b_ref[...].astype(BF16), NN, None)
        if res is not None:
            acc = acc + rest[0][...]
        o_ref[...] = acc.astype(o_ref.dtype)

    in_specs = [pl.BlockSpec((tm, K), lambda i: (i, 0)), pl.BlockSpec((K, N), lambda i: (0, 0))]
    args = [a, b]
    if res is not None:
        in_specs.append(pl.BlockSpec((tm, N), lambda i: (i, 0)))
        args.append(res)
    return pl.pallas_call(
        body, name=name, grid=(T // tm,), in_specs=in_specs,
        out_specs=pl.BlockSpec((tm, N), lambda i: (i, 0)),
        out_shape=jax.ShapeDtypeStruct((T, N), out_dtype),
    )(*args)


def _mm_tn(a, b, tmm, name, out_dtype=F32):
    T, M = a.shape
    N = b.shape[1]
    tk = _row_tile(T, 512)
    nk = T // tk

    def body(a_ref, b_ref, o_ref, acc_ref):
        @pl.when(pl.program_id(1) == 0)
        def _():
            acc_ref[...] = jnp.zeros_like(acc_ref)

        acc_ref[...] += _dot(a_ref[...].astype(BF16), b_ref[...].astype(BF16), TN, None)

        @pl.when(pl.program_id(1) == nk - 1)
        def _():
            o_ref[...] = acc_ref[...].astype(o_ref.dtype)

    return pl.pallas_call(
        body, name=name, grid=(M // tmm, nk),
        in_specs=[pl.BlockSpec((tk, tmm), lambda m, k: (k, m)), pl.BlockSpec((tk, N), lambda m, k: (k, 0))],
        out_specs=pl.BlockSpec((tmm, N), lambda m, k: (m, 0)),
        out_shape=jax.ShapeDtypeStruct((M, N), out_dtype),
        scratch_shapes=[pltpu.VMEM((tmm, N), F32)],
    )(a, b)


def _shift_rows_down(z, n):
    rows = _iota2(z.shape, 0)
    return jnp.where(rows < n, 0.0, pltpu.roll(z, n, 0))


def _shift_rows_up(z, n):
    T = z.shape[0]
    rows = _iota2(z.shape, 0)
    return jnp.where(rows >= T - n, 0.0, pltpu.roll(z, T - n, 0))


def _shift_fwd(proj, mu, name):
    T = proj.shape[0]
    nblk = RW_COLS // LANES
    first = HG_COLS // LANES

    def body(p_ref, mu_ref, o_ref):
        p = p_ref[...]
        o_ref[...] = p + (_shift_rows_down(p, 1) - p) * mu_ref[...]

    return pl.pallas_call(
        body, name=name, grid=(nblk,),
        in_specs=[pl.BlockSpec((T, LANES), lambda j: (0, first + j)), pl.BlockSpec((1, LANES), lambda j: (0, j))],
        out_specs=pl.BlockSpec((T, LANES), lambda j: (0, j)),
        out_shape=jax.ShapeDtypeStruct((T, RW_COLS), F32),
    )(proj, mu)


def _shift_bwd(ds, proj, mu, col0, name):
    T, width = ds.shape
    nblk = width // LANES
    first = (HG_COLS + col0) // LANES
    mu0 = col0 // LANES

    def body(ds_ref, p_ref, mu_ref, dp_ref, dmu_ref):
        dsv = ds_ref[...]
        p = p_ref[...]
        m = mu_ref[...]
        dp_ref[...] = (dsv * (1.0 - m) + _shift_rows_up(dsv * m, 1)).astype(dp_ref.dtype)
        dmu_ref[...] = jnp.sum(dsv * (_shift_rows_down(p, 1) - p), axis=0, keepdims=True)

    return pl.pallas_call(
        body, name=name, grid=(nblk,),
        in_specs=[pl.BlockSpec((T, LANES), lambda j: (0, j)),
                  pl.BlockSpec((T, LANES), lambda j: (0, first + j)),
                  pl.BlockSpec((1, LANES), lambda j: (0, mu0 + j))],
        out_specs=[pl.BlockSpec((T, LANES), lambda j: (0, j)), pl.BlockSpec((1, LANES), lambda j: (0, j))],
        out_shape=[jax.ShapeDtypeStruct((T, width), BF16), jax.ShapeDtypeStruct((1, width), F32)],
    )(ds, proj, mu)


def _conv3(z, w_ref):
    return w_ref[0:1, :] * _shift_rows_down(z, 2) + w_ref[1:2, :] * _shift_rows_down(z, 1) + w_ref[2:3, :] * z


def _ffn_act_fwd(u, conv_w, conv_b, name):
    T = u.shape[0]
    nblk = D_FF // LANES

    def body(ug_ref, uv_ref, wg_ref, wv_ref, bg_ref, bv_ref, act_ref):
        gate = _conv3(ug_ref[...], wg_ref) + bg_ref[...]
        val = _conv3(uv_ref[...], wv_ref) + bv_ref[...]
        act_ref[...] = (gate * _sigmoid(gate) * val).astype(act_ref.dtype)

    col = lambda off: pl.BlockSpec((T, LANES), lambda j: (0, off + j))
    wsp = lambda off: pl.BlockSpec((3, LANES), lambda j: (0, off + j))
    bsp = lambda off: pl.BlockSpec((1, LANES), lambda j: (0, off + j))
    return pl.pallas_call(
        body, name=name, grid=(nblk,),
        in_specs=[col(0), col(nblk), wsp(0), wsp(nblk), bsp(0), bsp(nblk)],
        out_specs=pl.BlockSpec((T, LANES), lambda j: (0, j)),
        out_shape=jax.ShapeDtypeStruct((T, D_FF), BF16),
    )(u, u, conv_w, conv_w, conv_b, conv_b)


def _ffn_act_bwd(u, dact, conv_w, conv_b, name):
    T = u.shape[0]
    nblk = D_FF // LANES

    def conv_bwd(z, dzc, w_ref, du_ref, dw_ref, db_ref):
        du = w_ref[2:3, :] * dzc + w_ref[1:2, :] * _shift_rows_up(dzc, 1) + w_ref[0:1, :] * _shift_rows_up(dzc, 2)
        du_ref[...] = du.astype(du_ref.dtype)
        dw_ref[0:1, :] = jnp.sum(dzc * _shift_rows_down(z, 2), axis=0, keepdims=True)
        dw_ref[1:2, :] = jnp.sum(dzc * _shift_rows_down(z, 1), axis=0, keepdims=True)
        dw_ref[2:3, :] = jnp.sum(dzc * z, axis=0, keepdims=True)
        db_ref[...] = jnp.sum(dzc, axis=0, keepdims=True)

    def body(ug_ref, uv_ref, da_ref, wg_ref, wv_ref, bg_ref, bv_ref,
             dug_ref, duv_ref, dwg_ref, dwv_ref, dbg_ref, dbv_ref):
        ug, uv = ug_ref[...], uv_ref[...]
        gate = _conv3(ug, wg_ref) + bg_ref[...]
        val = _conv3(uv, wv_ref) + bv_ref[...]
        da = da_ref[...].astype(F32)
        sg = _sigmoid(gate)
        dgate = da * val * (sg * (1.0 + gate * (1.0 - sg)))
        dval = da * gate * sg
        conv_bwd(ug, dgate, wg_ref, dug_ref, dwg_ref, dbg_ref)
        conv_bwd(uv, dval, wv_ref, duv_ref, dwv_ref, dbv_ref)

    col = lambda off: pl.BlockSpec((T, LANES), lambda j: (0, off + j))
    wsp = lambda off: pl.BlockSpec((3, LANES), lambda j: (0, off + j))
    bsp = lambda off: pl.BlockSpec((1, LANES), lambda j: (0, off + j))
    half = lambda r, dt: jax.ShapeDtypeStruct((r, D_FF), dt)
    return pl.pallas_call(
        body, name=name, grid=(nblk,),
        in_specs=[col(0), col(nblk), col(0), wsp(0), wsp(nblk), bsp(0), bsp(nblk)],
        out_specs=[col(0), col(0), wsp(0), wsp(0), bsp(0), bsp(0)],
        out_shape=[half(T, BF16), half(T, BF16), half(3, F32), half(3, F32), half(1, F32), half(1, F32)],
    )(u, u, dact, conv_w, conv_w, conv_b, conv_b)


def _loss_head(x2, w, target, name):
    T, D = x2.shape
    tb = _row_tile(T, 256)

    def body(x_ref, w_ref, t_ref, loss_ref, dx_ref, dw_ref):
        @pl.when(pl.program_id(0) == 0)
        def _():
            loss_ref[...] = jnp.zeros_like(loss_ref)
            dw_ref[...] = jnp.zeros_like(dw_ref)

        xv = x_ref[...]
        r = lax.rsqrt(jnp.mean(xv * xv, axis=-1, keepdims=True) + NORM_EPS)
        xn = xv * r
        err = xn * w_ref[...] - t_ref[...]
        row_loss = jnp.sum(err * err, axis=-1, keepdims=True) * (0.5 / D)
        loss_ref[...] += jnp.sum(row_loss, axis=0, keepdims=True)
        dy = err * (1.0 / D)
        dxn = dy * w_ref[...]
        dx_ref[...] = r * (dxn - xn * jnp.mean(dxn * xn, axis=-1, keepdims=True))
        dw_ref[...] += jnp.sum(dy * xn, axis=0, keepdims=True)

    row = pl.BlockSpec((tb, D), lambda i: (i, 0))
    vec = pl.BlockSpec((1, D), lambda i: (0, 0))
    return pl.pallas_call(
        body, name=name, grid=(T // tb,),
        in_specs=[row, vec, row],
        out_specs=[pl.BlockSpec((1, 1), lambda i: (0, 0)), row, vec],
        out_shape=[jax.ShapeDtypeStruct((1, 1), F32), jax.ShapeDtypeStruct((T, D), F32),
                   jax.ShapeDtypeStruct((1, D), F32)],
    )(x2, w, target)


def _adamw(w, g, m, v, name):
    R, C = w.shape
    tb = _row_tile(R, 256) if R % 8 == 0 else R

    def body(w_ref, g_ref, m_ref, v_ref, d_ref, nm_ref, nv_ref):
        gv = g_ref[...]
        nm = ADAM_B1 * m_ref[...] + (1.0 - ADAM_B1) * gv
        nv = ADAM_B2 * v_ref[...] + (1.0 - ADAM_B2) * (gv * gv)
        m_hat = nm / (1.0 - ADAM_B1 ** ADAM_STEP)
        v_hat = nv / (1.0 - ADAM_B2 ** ADAM_STEP)
        d_ref[...] = -ADAM_LR * (m_hat / (jnp.sqrt(v_hat) + ADAM_EPS) + ADAM_WD * w_ref[...])
        nm_ref[...] = nm
        nv_ref[...] = nv

    blk = pl.BlockSpec((tb, C), lambda i: (i, 0))
    sd = jax.ShapeDtypeStruct((R, C), F32)
    return pl.pallas_call(
        body, name=name, grid=(R // tb,), in_specs=[blk] * 4, out_specs=[blk] * 3, out_shape=[sd] * 3,
    )(w, g, m, v)


def _chunk_masks(rows, chunk):
    shift = chunk.bit_length() - 1
    i, j = _iota2((rows, rows), 0), _iota2((rows, rows), 1)
    same = jnp.right_shift(i, shift) == jnp.right_shift(j, shift)
    return same.astype(F32), (same & (j <= i)).astype(F32), (same & (j < i)).astype(F32)


def _head_lanes(h):
    return slice(h * LANES, (h + 1) * LANES)


def _chunk_rows(c, chunk):
    return pl.ds(pl.multiple_of(c * chunk, chunk), chunk)


def _hg_consts(rows):
    same, tril, _ = _chunk_masks(rows, HG_CHUNK)
    shift = HG_CHUNK.bit_length() - 1
    i, j = _iota2((rows, rows), 0), _iota2((rows, rows), 1)
    mid_row = jnp.left_shift(jnp.right_shift(i, shift), shift) + (HG_CHUNK // 2 - 1)
    upto_mid = (same > 0) & (j <= mid_row)
    return jnp.concatenate([tril, same, upto_mid.astype(F32)], axis=0), tril


def _hg_prep(consts, qr, fr, ir, l0, l1):
    sums, tril = consts
    rows = tril.shape[0]
    lb = _sigmoid(l0 - l1)
    f = lb + (1.0 - lb) * _sigmoid(fr)
    q = qr * _sigmoid(qr) * (HG_HEAD_DIM ** -0.5)
    k = 1.0 - f
    g = jnp.log(f)
    acc = _dot(sums, g, NN, HIGHEST)
    a = acc[:rows]
    tot = acc[rows:2 * rows]
    mid = acc[2 * rows:]
    att = _dot(q * jnp.exp(a - mid), k * jnp.exp(mid - a), NT) * tril
    return q * jnp.exp(a), _dot(att, ir), k * jnp.exp(tot - a), jnp.exp(tot)


def _hg_post(o, gr, nw):
    on = o * lax.rsqrt(jnp.mean(o * o, axis=-1, keepdims=True) + NORM_EPS)
    return on * nw * (gr * _sigmoid(gr))


def _hg_specs(T, tb, rev):
    nT = T // tb
    tix = (lambda t: nT - 1 - t) if rev else (lambda t: t)
    col = lambda blk: pl.BlockSpec((tb, HG_WIDTH), lambda t: (tix(t), blk))
    vec = pl.BlockSpec((1, HG_WIDTH), lambda t: (0, 0))
    st = pl.BlockSpec((HG_HEADS, tb // HG_CHUNK, HG_HEAD_DIM, HG_HEAD_DIM), lambda t: (0, tix(t), 0, 0))
    return nT, col, vec, st


def _hg_fwd(proj, l0, l1, nw, name):
    T = proj.shape[0]
    tb = _row_tile(T, SCAN_ROWS)
    nsub = tb // HG_CHUNK
    nT, col, vec, st = _hg_specs(T, tb, False)

    def body(q_ref, f_ref, i_ref, g_ref, l0_ref, l1_ref, nw_ref, o_ref, st_ref, s_ref, qe_ref, kd_ref, dec_ref):
        @pl.when(pl.program_id(0) == 0)
        def _():
            s_ref[...] = jnp.zeros_like(s_ref)

        consts = _hg_consts(tb)
        for h in range(HG_HEADS):
            ln = _head_lanes(h)
            qe, o_intra, kd, dec = _hg_prep(consts, q_ref[:, ln], f_ref[:, ln], i_ref[:, ln],
                                            l0_ref[:, ln], l1_ref[:, ln])
            qe_ref[h], kd_ref[h], dec_ref[h] = qe, kd, dec
            o_ref[:, ln] = o_intra

        def step(c, carry):
            rows = _chunk_rows(c, HG_CHUNK)
            for h in range(HG_HEADS):
                ln = _head_lanes(h)
                S = s_ref[h]
                st_ref[h, c] = S
                o_ref[rows, ln] += _dot(qe_ref[h, rows, :], S, NT)
                s_ref[h] = S * dec_ref[h, pl.ds(c * HG_CHUNK, 1), :] + _dot(i_ref[rows, ln], kd_ref[h, rows, :], TN)
            return carry

        lax.fori_loop(0, nsub, step, 0)
        for h in range(HG_HEADS):
            ln = _head_lanes(h)
            o_ref[:, ln] = _hg_post(o_ref[:, ln], g_ref[:, ln], nw_ref[:, ln])

    blk = pltpu.VMEM((HG_HEADS, tb, LANES), F32)
    return pl.pallas_call(
        body, name=name, grid=(nT,),
        in_specs=[col(0), col(1), col(2), col(3), vec, vec, vec],
        out_specs=[col(0), st],
        out_shape=[jax.ShapeDtypeStruct((T, HG_WIDTH), F32),
                   jax.ShapeDtypeStruct((HG_HEADS, T // HG_CHUNK, HG_HEAD_DIM, HG_HEAD_DIM), F32)],
        scratch_shapes=[pltpu.VMEM((HG_HEADS, HG_HEAD_DIM, HG_HEAD_DIM), F32), blk, blk, blk],
    )(proj, proj, proj, proj, l0, l1, nw)


def _hg_bwd(proj, states, do, do_blk, l0, l1, nw, name):
    T = proj.shape[0]
    tb = _row_tile(T, SCAN_ROWS)
    nsub = tb // HG_CHUNK
    nT, col, vec, st = _hg_specs(T, tb, True)

    def body(q_ref, f_ref, i_ref, g_ref, st_ref, do_ref, l0_ref, l1_ref, nw_ref,
             dq_ref, df_ref, di_ref, dg_ref, dl0_ref, dl1_ref, dnw_ref,
             ds_ref, qe_ref, kd_ref, dec_ref, o_ref, dqe_ref, dkd_ref, ddec_ref, dis_ref):
        @pl.when(pl.program_id(0) == 0)
        def _():
            ds_ref[...] = jnp.zeros_like(ds_ref)
            dl0_ref[...] = jnp.zeros_like(dl0_ref)
            dl1_ref[...] = jnp.zeros_like(dl1_ref)
            dnw_ref[...] = jnp.zeros_like(dnw_ref)

        consts = _hg_consts(tb)
        prep_vjps = []
        for h in range(HG_HEADS):
            ln = _head_lanes(h)
            (qe, o_intra, kd, dec), vjp = jax.vjp(
                functools.partial(_hg_prep, consts), q_ref[:, ln], f_ref[:, ln], i_ref[:, ln],
                l0_ref[:, ln], l1_ref[:, ln])
            prep_vjps.append(vjp)
            qe_ref[h], kd_ref[h], dec_ref[h], o_ref[h] = qe, kd, dec, o_intra

        def redo(c, carry):
            rows = _chunk_rows(c, HG_CHUNK)
            for h in range(HG_HEADS):
                o_ref[h, rows, :] += _dot(qe_ref[h, rows, :], st_ref[h, c], NT)
            return carry

        lax.fori_loop(0, nsub, redo, 0)
        for h in range(HG_HEADS):
            ln = _head_lanes(h)
            _, vjp = jax.vjp(_hg_post, o_ref[h], g_ref[:, ln], nw_ref[:, ln])
            d_o, dgr, dnw = vjp(do_ref[:, ln])
            o_ref[h] = d_o
            dg_ref[:, ln] = dgr.astype(dg_ref.dtype)
            dnw_ref[:, ln] += dnw
        ddec_ref[...] = jnp.zeros_like(ddec_ref)

        def step(i, carry):
            c = nsub - 1 - i
            rows = _chunk_rows(c, HG_CHUNK)
            row0 = pl.ds(c * HG_CHUNK, 1)
            for h in range(HG_HEADS):
                ln = _head_lanes(h)
                G = ds_ref[h]
                S = st_ref[h, c]
                d_o = o_ref[h, rows, :]
                dqe_ref[h, rows, :] = _dot(d_o, S)
                dkd_ref[h, rows, :] = _dot(i_ref[rows, ln], G)
                dis_ref[h, rows, :] = _dot(kd_ref[h, rows, :], G, NT)
                ddec_ref[h, row0, :] = jnp.sum(S * G, axis=0, keepdims=True)
                ds_ref[h] = G * dec_ref[h, row0, :] + _dot(d_o, qe_ref[h, rows, :], TN)
            return carry

        lax.fori_loop(0, nsub, step, 0)
        for h in range(HG_HEADS):
            ln = _head_lanes(h)
            dq, df, di, dl0, dl1 = prep_vjps[h]((dqe_ref[h], o_ref[h], dkd_ref[h], ddec_ref[h]))
            dq_ref[:, ln] = dq.astype(dq_ref.dtype)
            df_ref[:, ln] = df.astype(df_ref.dtype)
            di_ref[:, ln] = (di + dis_ref[h]).astype(di_ref.dtype)
            dl0_ref[:, ln] += dl0
            dl1_ref[:, ln] += dl1

    dcol = jax.ShapeDtypeStruct((T, HG_WIDTH), BF16)
    dvec = jax.ShapeDtypeStruct((1, HG_WIDTH), F32)
    blk = pltpu.VMEM((HG_HEADS, tb, LANES), F32)
    return pl.pallas_call(
        body, name=name, grid=(nT,),
        in_specs=[col(0), col(1), col(2), col(3), st, col(do_blk), vec, vec, vec],
        out_specs=[col(0)] * 4 + [vec] * 3,
        out_shape=[dcol] * 4 + [dvec] * 3,
        scratch_shapes=[pltpu.VMEM((HG_HEADS, HG_HEAD_DIM, HG_HEAD_DIM), F32)] + [blk] * 8,
    )(proj, proj, proj, proj, states, do, l0, l1, nw)


def _rw_consts(rows):
    same, tril, stril = _chunk_masks(rows, RW_CHUNK)
    br, bc = _iota2((LANES, LANES), 0), _iota2((LANES, LANES), 1)
    blockdiag = ((br < RW_HEAD_DIM) == (bc < RW_HEAD_DIM)).astype(F32)
    m0 = (_iota2((1, LANES), 1) < RW_HEAD_DIM).astype(F32)
    return same, tril, stril, blockdiag, m0, 1.0 - m0


def _unit_lower_inverse_impl(low):
    rows = low.shape[0]
    x = low + (_iota2(low.shape, 0) == _iota2(low.shape, 1)).astype(F32)
    p = _dot(low, low)
    n = 4
    while n < RW_CHUNK:
        z = _dot(jnp.concatenate([p, x], axis=0), p)
        p, x = z[:rows], x + z[rows:]
        n *= 2
    return x + _dot(x, p)


@jax.custom_vjp
def _unit_lower_inverse(low):
    return _unit_lower_inverse_impl(low)


def _unit_lower_inverse_fwd(low):
    x = _unit_lower_inverse_impl(low)
    return x, x


def _unit_lower_inverse_bwd(x, dx):
    return (_dot(_dot(x, dx, TN), x, NT),)


_unit_lower_inverse.defvjp(_unit_lower_inverse_fwd, _unit_lower_inverse_bwd)


N_PREP_OUT = 9


def _rw_prep(consts, r, kx, v, lw, gd, w0, a0, k_k, k_a, w2p, a2p, g2):
    same, tril, stril, blockdiag, m0, m1 = consts
    xw = w0 + _dot(jnp.tanh(lw), w2p)
    w = jnp.minimum(xw, 0.0) - jnp.log(1.0 + jnp.exp(-jnp.abs(xw))) - 0.5
    ld = -jnp.exp(w)
    a_s = _sigmoid(a0 + _dot(lw, a2p))
    g = _dot(_sigmoid(gd), g2)
    kk = kx * k_k
    kk = kk / jnp.maximum(jnp.sqrt(_dot(kk * kk, blockdiag)), L2_EPS)
    k2 = kx * (1.0 + (a_s - 1.0) * k_a)
    bv = kk * a_s
    rows = tril.shape[0]
    acc = _dot(jnp.concatenate([tril, same], axis=0), ld, NN, HIGHEST)
    cum, tot = acc[:rows], acc[rows:]
    ecn = jnp.exp(-cum)
    a_t = -kk * jnp.exp(cum - ld)
    b_h = bv * ecn
    k_h = k2 * ecn
    r_t = r * jnp.exp(cum)
    rem = jnp.exp(tot - cum)
    z = _dot(jnp.concatenate([a_t * m0, a_t * m1, r_t * m0, r_t * m1], axis=0),
             jnp.concatenate([b_h, k_h], axis=0), NT)
    W = U = Q = Y0 = 0.0
    for h, m in enumerate((m0, m1)):
        za, zr = z[h * rows:(h + 1) * rows], z[(2 + h) * rows:(3 + h) * rows]
        lak = za[:, rows:] * stril
        mrb = zr[:, :rows] * tril
        mrk = zr[:, rows:] * tril
        tinv = _unit_lower_inverse(za[:, :rows] * stril)
        lv = _dot(jnp.concatenate([lak, mrk], axis=0), v)
        wu = _dot(tinv, jnp.concatenate([a_t * m, lv[:rows]], axis=1))
        w_m, u_m = wu[:, :LANES], m * wu[:, LANES:]
        qy = _dot(mrb, jnp.concatenate([w_m, u_m], axis=1))
        W = W + w_m
        U = U + u_m
        Q = Q + r_t * m + qy[:, :LANES]
        Y0 = Y0 + qy[:, LANES:] + m * lv[rows:]
    return W, U, Q, Y0, bv * rem, k2 * rem, jnp.exp(tot), k2, g


def _rw_post(blockdiag, y, r, v, k2, g, r_k, ln_w, ln_b):
    inv_n = 1.0 / RW_HEAD_DIM
    yc = y - _dot(y, blockdiag) * inv_n
    var = _dot(yc * yc, blockdiag) * inv_n
    yn = yc * lax.rsqrt(var + RW_GN_EPS) * ln_w + ln_b
    bonus = _dot(r * k2 * r_k, blockdiag) * v
    return (yn + bonus) * g


N_RW_VEC = 7
N_RW_MAT = 3


def _rw_specs(T, tb, rev):
    nT = T // tb
    tix = (lambda t: nT - 1 - t) if rev else (lambda t: t)
    wide = lambda blk: pl.BlockSpec((tb, RW_WIDTH), lambda t: (tix(t), blk))
    narrow = lambda blk: pl.BlockSpec((tb, LANES), lambda t: (tix(t), blk))
    vec = pl.BlockSpec((1, RW_WIDTH), lambda t: (0, 0))
    mat = pl.BlockSpec((RW_PAIRS, LANES, LANES), lambda t: (0, 0, 0))
    st = pl.BlockSpec((RW_PAIRS, tb // RW_CHUNK, LANES, LANES), lambda t: (0, tix(t), 0, 0))
    lora0 = 3 * RW_WIDTH // LANES
    ins = [wide(0), wide(1), wide(2), narrow(lora0), narrow(lora0 + 1)]
    return nT, wide, vec, mat, st, ins


def _rw_prep_args(p, r_ref, k_ref, v_ref, lw_ref, gd_ref, vrefs, mrefs):
    ln = _head_lanes(p)
    w0, a0, k_k, k_a = [x[:, ln] for x in vrefs[:4]]
    return (r_ref[:, ln], k_ref[:, ln], v_ref[:, ln], lw_ref[...], gd_ref[...], w0, a0, k_k, k_a,
            *[x[p] for x in mrefs])


def _stack_chunks(ref, top, bottom):
    C = RW_CHUNK
    for c in range(ref.shape[0]):
        ref[c, 0:C, :] = top[c * C:(c + 1) * C]
        ref[c, C:2 * C, :] = bottom[c * C:(c + 1) * C]


def _rw_fwd(rws, vecs, mats, name):
    T = rws.shape[0]
    tb = _row_tile(T, SCAN_ROWS)
    nsub = tb // RW_CHUNK
    C = RW_CHUNK
    nT, wide, vec, mat, st, ins = _rw_specs(T, tb, False)

    def body(*refs):
        r_ref, k_ref, v_ref, lw_ref, gd_ref = refs[:5]
        vrefs = refs[5:5 + N_RW_VEC]
        mrefs = refs[5 + N_RW_VEC:5 + N_RW_VEC + N_RW_MAT]
        o_ref, st_ref, s_ref, wq_ref, uy_ref, bk_ref, misc_ref, y_ref = refs[-8:]

        @pl.when(pl.program_id(0) == 0)
        def _():
            s_ref[...] = jnp.zeros_like(s_ref)

        consts = _rw_consts(tb)
        blockdiag = consts[3]
        for p in range(RW_PAIRS):
            W, U, Q, Y0, Bg, Kg, dec, k2, g = _rw_prep(
                consts, *_rw_prep_args(p, r_ref, k_ref, v_ref, lw_ref, gd_ref, vrefs, mrefs))
            _stack_chunks(wq_ref.at[p], W, Q)
            _stack_chunks(uy_ref.at[p], U, Y0)
            _stack_chunks(bk_ref.at[p], Bg, Kg)
            misc_ref[0, p], misc_ref[1, p], misc_ref[2, p] = dec, k2, g

        def step(c, carry):
            rows = _chunk_rows(c, C)
            for p in range(RW_PAIRS):
                S = s_ref[p]
                st_ref[p, c] = S
                py = _dot(wq_ref[p, c], S, NT) + uy_ref[p, c]
                y_ref[p, rows, :] = py[C:]
                pv = jnp.concatenate([py[:C], v_ref[rows, _head_lanes(p)]], axis=0)
                s_ref[p] = (S * misc_ref[0, p, pl.ds(c * C, 1), :] + _dot(pv, bk_ref[p, c], TN)) * blockdiag
            return carry

        lax.fori_loop(0, nsub, step, 0)
        for p in range(RW_PAIRS):
            ln = _head_lanes(p)
            r_k, ln_w, ln_b = [x[:, ln] for x in vrefs[4:]]
            o_ref[:, ln] = _rw_post(blockdiag, y_ref[p], r_ref[:, ln], v_ref[:, ln], misc_ref[1, p], misc_ref[2, p],
                                    r_k, ln_w, ln_b)

    stacked = pltpu.VMEM((RW_PAIRS, nsub, 2 * C, LANES), F32)
    return pl.pallas_call(
        body, name=name, grid=(nT,),
        in_specs=ins + [vec] * N_RW_VEC + [mat] * N_RW_MAT,
        out_specs=[wide(0), st],
        out_shape=[jax.ShapeDtypeStruct((T, RW_WIDTH), F32),
                   jax.ShapeDtypeStruct((RW_PAIRS, T // RW_CHUNK, LANES, LANES), F32)],
        scratch_shapes=[pltpu.VMEM((RW_PAIRS, LANES, LANES), F32), stacked, stacked, stacked,
                        pltpu.VMEM((3, RW_PAIRS, tb, LANES), F32), pltpu.VMEM((RW_PAIRS, tb, LANES), F32)],
    )(rws, rws, rws, rws, rws, *vecs, *mats)


def _rw_bwd(rws, states, do, do_blk, vecs, mats, name):
    T = rws.shape[0]
    tb = _row_tile(T, SCAN_ROWS)
    nsub = tb // RW_CHUNK
    C = RW_CHUNK
    nT, wide, vec, mat, st, ins = _rw_specs(T, tb, True)
    nin = 5 + 1 + 1 + N_RW_VEC + N_RW_MAT

    def body(*refs):
        r_ref, k_ref, v_ref, lw_ref, gd_ref = refs[:5]
        st_ref, do_ref = refs[5], refs[6]
        vrefs = refs[7:7 + N_RW_VEC]
        mrefs = refs[7 + N_RW_VEC:nin]
        dr_ref, dk_ref, dv_ref, dlo_ref = refs[nin:nin + 4]
        dvec = refs[nin + 4:nin + 4 + N_RW_VEC]
        dmat = refs[nin + 4 + N_RW_VEC:nin + 4 + N_RW_VEC + N_RW_MAT]
        ds_ref, wq_ref, uy_ref, bk_ref, pv_ref, dec_ref, y_ref, dpre_ref, dvs_ref = refs[-9:]

        @pl.when(pl.program_id(0) == 0)
        def _():
            ds_ref[...] = jnp.zeros_like(ds_ref)
            for x in dvec + dmat:
                x[...] = jnp.zeros_like(x)

        consts = _rw_consts(tb)
        blockdiag = consts[3]
        dlw, dgd = 0.0, 0.0
        for p in range(RW_PAIRS):
            ln = _head_lanes(p)
            (W, U, Q, Y0, Bg, Kg, dec, k2, g), prep_vjp = jax.vjp(
                functools.partial(_rw_prep, consts),
                *_rw_prep_args(p, r_ref, k_ref, v_ref, lw_ref, gd_ref, vrefs, mrefs))
            _stack_chunks(wq_ref, W, Q)
            _stack_chunks(uy_ref, U, Y0)
            _stack_chunks(bk_ref, Bg, Kg)
            dec_ref[...] = dec

            def redo(c, carry, p=p, ln=ln):
                rows = _chunk_rows(c, C)
                py = _dot(wq_ref[c], st_ref[p, c], NT) + uy_ref[c]
                y_ref[rows, :] = py[C:]
                pv_ref[c, 0:C, :] = py[:C]
                pv_ref[c, C:2 * C, :] = v_ref[rows, ln]
                return carry

            lax.fori_loop(0, nsub, redo, 0)
            r_k, ln_w, ln_b = [x[:, ln] for x in vrefs[4:]]
            _, post_vjp = jax.vjp(functools.partial(_rw_post, blockdiag), y_ref[...], r_ref[:, ln], v_ref[:, ln],
                                  k2, g, r_k, ln_w, ln_b)
            dy, dr2, dv2, dk2, dg, dr_k, dln_w, dln_b = post_vjp(do_ref[:, ln])
            dpre_ref[3] = dy
            dvs_ref[...] = dv2
            for x, gx in zip(dvec[4:], (dr_k, dln_w, dln_b)):
                x[:, ln] += gx
            dpre_ref[6] = jnp.zeros_like(dpre_ref[6])

            def step(i, carry, p=p):
                c = nsub - 1 - i
                rows = _chunk_rows(c, C)
                row0 = pl.ds(c * C, 1)
                G = ds_ref[p] * blockdiag
                S = st_ref[p, c]
                t1 = _dot(bk_ref[c], G, NT)
                dpy = jnp.concatenate([t1[:C], dpre_ref[3, rows, :]], axis=0)
                t2 = _dot(dpy, S)
                t3 = _dot(pv_ref[c], G)
                dvs_ref[rows, :] += t1[C:]
                dpre_ref[0, rows, :] = t2[:C]
                dpre_ref[1, rows, :] = t1[:C]
                dpre_ref[2, rows, :] = t2[C:]
                dpre_ref[4, rows, :] = t3[:C]
                dpre_ref[5, rows, :] = t3[C:]
                dpre_ref[6, row0, :] = jnp.sum(S * G, axis=0, keepdims=True)
                ds_ref[p] = G * dec_ref[row0, :] + _dot(dpy, wq_ref[c], TN)
                return carry

            lax.fori_loop(0, nsub, step, 0)
            grads = prep_vjp(tuple(dpre_ref[i] for i in range(7)) + (dk2, dg))
            dr_ref[:, ln] = grads[0] + dr2
            dk_ref[:, ln] = grads[1]
            dv_ref[:, ln] = grads[2] + dvs_ref[...]
            dlw = dlw + grads[3]
            dgd = dgd + grads[4]
            for x, gx in zip(dvec[:4], grads[5:9]):
                x[:, ln] += gx
            for x, gx in zip(dmat, grads[9:]):
                x[p] += gx
        dlo_ref[:, 0:LANES] = dlw
        dlo_ref[:, LANES:2 * LANES] = dgd

    dcol = jax.ShapeDtypeStruct((T, RW_WIDTH), F32)
    dlo_spec = pl.BlockSpec((tb, 2 * LANES), lambda t: (nT - 1 - t, 0))
    blk = pltpu.VMEM((tb, LANES), F32)
    stacked = pltpu.VMEM((nsub, 2 * C, LANES), F32)
    return pl.pallas_call(
        body, name=name, grid=(nT,),
        in_specs=ins + [st, wide(do_blk)] + [vec] * N_RW_VEC + [mat] * N_RW_MAT,
        out_specs=[wide(0)] * 3 + [dlo_spec] + [vec] * N_RW_VEC + [mat] * N_RW_MAT,
        out_shape=[dcol] * 3 + [jax.ShapeDtypeStruct((T, 2 * LANES), F32)]
        + [jax.ShapeDtypeStruct((1, RW_WIDTH), F32)] * N_RW_VEC
        + [jax.ShapeDtypeStruct((RW_PAIRS, LANES, LANES), F32)] * N_RW_MAT,
        scratch_shapes=[pltpu.VMEM((RW_PAIRS, LANES, LANES), F32), stacked, stacked, stacked, stacked, blk, blk,
                        pltpu.VMEM((7, tb, LANES), F32), blk],
    )(rws, rws, rws, rws, rws, states, do, *vecs, *mats)


def _my_index():
    return 4 * lax.axis_index("x") + 2 * lax.axis_index("y") + lax.axis_index("c")


def _peer(bits):
    pos = []
    for name, flip in zip(("x", "y", "c"), bits):
        i = lax.axis_index(name)
        pos.append(1 - i if flip else i)
    return tuple(pos)


def _peer_index(bits):
    x, y, c = _peer(bits)
    return 4 * x + 2 * y + c


def _all_gather(shards, name):
    n = len(shards)
    chips = [(1, 0, 0), (0, 1, 0), (1, 1, 0)]
    sib = (0, 0, 1)

    def body(*refs):
        ins, outs = refs[:n], refs[n:2 * n]
        send_sems, recv_sems, local_sems = refs[2 * n:]

        def rows(k, dev):
            r = ins[k].shape[0]
            return outs[k].at[pl.ds(dev * r, r), :]

        def copy(k, slot, block_dev, to_bits, src=None):
            return pltpu.make_async_remote_copy(
                src_ref=rows(k, block_dev) if src is None else src, dst_ref=rows(k, block_dev),
                send_sem=send_sems.at[k, slot], recv_sem=recv_sems.at[k, slot],
                device_id=_peer(to_bits), device_id_type=MESH_ID)

        me = _my_index()
        started = []
        for k in range(n):
            mine = pltpu.make_async_copy(ins[k], rows(k, me), local_sems.at[k])
            mine.start()
            started.append(mine)
        sends = []
        for k in range(n):
            first = [copy(k, 0, me, sib, src=ins[k])]
            first += [copy(k, 1 + j, me, chip, src=ins[k]) for j, chip in enumerate(chips)]
            for cp in first:
                cp.start()
            sends += first
        for k in range(n):
            for j, chip in enumerate(chips):
                copy(k, 1 + j, _peer_index(chip), chip).wait_recv()
                fwd = copy(k, 4 + j, _peer_index(chip), sib)
                fwd.start()
                sends.append(fwd)
        for k in range(n):
            copy(k, 0, _peer_index(sib), sib).wait_recv()
            for j, chip in enumerate(chips):
                both = (chip[0], chip[1], 1)
                copy(k, 4 + j, _peer_index(both), sib).wait_recv()
        for cp in sends:
            cp.wait_send()
        for cp in started:
            cp.wait()

    any_spec = pl.BlockSpec(memory_space=pl.ANY)
    return pl.pallas_call(
        body, name=name,
        in_specs=[any_spec] * n, out_specs=[any_spec] * n,
        out_shape=[jax.ShapeDtypeStruct((N_DEV * s.shape[0], s.shape[1]), s.dtype) for s in shards],
        scratch_shapes=[pltpu.SemaphoreType.DMA((n, 7)), pltpu.SemaphoreType.DMA((n, 7)),
                        pltpu.SemaphoreType.DMA((n,))],
    )(*shards)


def _exchange(partials, name):
    n = len(partials)
    flips = [(dx, dy, dc) for dx in (0, 1) for dy in (0, 1) for dc in (0, 1)][1:]

    def body(*refs):
        ins, outs = refs[:n], refs[n:2 * n]
        send_sems, recv_sems, local_sems = refs[2 * n:]
        me = _my_index()
        local = []
        for k in range(n):
            cp = pltpu.make_async_copy(ins[k].at[me], outs[k].at[me], local_sems.at[k])
            cp.start()
            local.append(cp)
        copies = []
        for k in range(n):
            for d, bits in enumerate(flips):
                cp = pltpu.make_async_remote_copy(
                    src_ref=ins[k].at[_peer_index(bits)], dst_ref=outs[k].at[me],
                    send_sem=send_sems.at[k, d], recv_sem=recv_sems.at[k, d],
                    device_id=_peer(bits), device_id_type=MESH_ID)
                cp.start()
                copies.append(cp)
        for cp in copies:
            cp.wait_recv()
        for cp in copies:
            cp.wait_send()
        for cp in local:
            cp.wait()

    any_spec = pl.BlockSpec(memory_space=pl.ANY)
    return pl.pallas_call(
        body, name=name,
        in_specs=[any_spec] * n, out_specs=[any_spec] * n,
        out_shape=[jax.ShapeDtypeStruct(p.shape, p.dtype) for p in partials],
        scratch_shapes=[pltpu.SemaphoreType.DMA((n, 7)), pltpu.SemaphoreType.DMA((n, 7)),
                        pltpu.SemaphoreType.DMA((n,))],
    )(*partials)


HBM_SPEC = pl.BlockSpec(memory_space=pltpu.HBM)
SEM_SPEC = pl.BlockSpec(memory_space=pltpu.SEMAPHORE)
ALL_FLIPS = [(dx, dy, dc) for dx in (0, 1) for dy in (0, 1) for dc in (0, 1)][1:]


def _spread_copies(srcs, lands, send_sems, recv_sems, per_peer_source):
    me = _my_index()
    copies = []
    for k, land in enumerate(lands):
        for d, bits in enumerate(ALL_FLIPS):
            src = srcs[k].at[_peer_index(bits)] if per_peer_source else land.at[me]
            copies.append(pltpu.make_async_remote_copy(
                src_ref=src, dst_ref=land.at[me],
                send_sem=send_sems.at[k * 7 + d], recv_sem=recv_sems.at[k * 7 + d],
                device_id=_peer(bits), device_id_type=MESH_ID))
    return copies


def _spread_start(srcs, lands, name):
    ns, n = len(srcs), len(lands)

    def body(*refs):
        src_refs, land_refs = refs[:ns], refs[ns:ns + n]
        send_sems, recv_sems = refs[ns + n], refs[ns + n + 1]
        token = refs[-1]
        for cp in _spread_copies(src_refs, land_refs, send_sems, recv_sems, ns > 0):
            cp.start()
        token[...] = jnp.zeros_like(token)

    bufs = list(srcs) + list(lands)
    out = pl.pallas_call(
        body, name=name,
        out_shape=(pltpu.SemaphoreType.DMA((7 * n,)), pltpu.SemaphoreType.DMA((7 * n,)),
                   *[pltpu.HBM(b.shape, b.dtype) for b in bufs], jax.ShapeDtypeStruct((8, LANES), F32)),
        in_specs=[HBM_SPEC] * (ns + n),
        out_specs=(SEM_SPEC, SEM_SPEC, *[HBM_SPEC] * (ns + n), pl.BlockSpec(memory_space=pltpu.VMEM)),
        input_output_aliases={i: 2 + i for i in range(ns + n)},
        compiler_params=pltpu.CompilerParams(has_side_effects=pltpu.SideEffectType.DATAFLOW_SIDE_EFFECTING),
    )(*[pltpu.with_memory_space_constraint(b, pltpu.HBM) for b in bufs])
    return out[0], out[1], list(out[2:2 + ns]), list(out[2 + ns:2 + ns + n]), out[-1]


def _spread_wait(send_sems, recv_sems, srcs, lands, after, name):
    ns, n = len(srcs), len(lands)

    def body(*refs):
        src_refs, land_refs = refs[:ns], refs[ns:ns + n]
        send_sems, recv_sems = refs[ns + n], refs[ns + n + 1]
        for cp in _spread_copies(src_refs, land_refs, send_sems, recv_sems, ns > 0):
            cp.wait_send()
            cp.wait_recv()

    bufs = list(srcs) + list(lands)
    out = pl.pallas_call(
        body, name=name,
        out_shape=tuple(pltpu.HBM(b.shape, b.dtype) for b in bufs),
        in_specs=[HBM_SPEC] * (ns + n) + [SEM_SPEC, SEM_SPEC, pl.BlockSpec(memory_space=pl.ANY)],
        out_specs=tuple([HBM_SPEC] * (ns + n)),
        input_output_aliases={i: i for i in range(ns + n)},
        compiler_params=pltpu.CompilerParams(has_side_effects=pltpu.SideEffectType.DATAFLOW_SIDE_EFFECTING),
    )(*bufs, send_sems, recv_sems, after)
    return list(out[ns:])


def _own_slot_only(block, me):
    return lax.dynamic_update_slice(jnp.zeros((N_DEV,) + block.shape, block.dtype), block[None], (me, 0, 0))


def _sum_slots(landed, name):
    _, R, C = landed.shape
    tb = _row_tile(R, 128)

    def body(l_ref, o_ref):
        acc = l_ref[0].astype(F32)
        for s in range(1, N_DEV):
            acc = acc + l_ref[s].astype(F32)
        o_ref[...] = acc

    return pl.pallas_call(
        body, name=name, grid=(R // tb,),
        in_specs=[pl.BlockSpec((N_DEV, tb, C), lambda i: (0, i, 0))],
        out_specs=pl.BlockSpec((tb, C), lambda i: (i, 0)),
        out_shape=jax.ShapeDtypeStruct((R, C), F32),
    )(landed)


def _pack_rows(flat_list, width=LANES):
    flat = jnp.concatenate([a.reshape(-1) for a in flat_list])
    n = flat.shape[0]
    rows = -(-n // width)
    rows = -(-rows // 8) * 8
    return jnp.pad(flat, (0, rows * width - n)).reshape(rows, width)


def _unpack(packed, shapes):
    flat = packed.reshape(-1)
    out, off = [], 0
    for s in shapes:
        n = 1
        for d in s:
            n *= d
        out.append(flat[off:off + n].reshape(s))
        off += n
    return out


def kernel(x, norm1_w, w_in, hg_lb_logits, hg_norm_w, rw_shift_mu, rw_w0, rw_w2, rw_a0, rw_a2, rw_g2, rw_k_k, rw_k_a, rw_r_k, rw_ln_w, rw_ln_b, w_out, norm2_w, w_up, conv_w, conv_b, w_down, final_norm_w, loss_target, m_norm1_w, m_w_in, m_hg_lb_logits, m_hg_norm_w, m_rw_shift_mu, m_rw_w0, m_rw_w2, m_rw_a0, m_rw_a2, m_rw_g2, m_rw_k_k, m_rw_k_a, m_rw_r_k, m_rw_ln_w, m_rw_ln_b, m_w_out, m_norm2_w, m_w_up, m_conv_w, m_conv_b, m_w_down, m_final_norm_w, v_norm1_w, v_w_in, v_hg_lb_logits, v_hg_norm_w, v_rw_shift_mu, v_rw_w0, v_rw_w2, v_rw_a0, v_rw_a2, v_rw_g2, v_rw_k_k, v_rw_k_a, v_rw_r_k, v_rw_ln_w, v_rw_ln_b, v_w_out, v_norm2_w, v_w_up, v_conv_w, v_conv_b, v_w_down, v_final_norm_w):
    weights = dict(norm1_w=norm1_w, w_in=w_in, hg_lb_logits=hg_lb_logits, hg_norm_w=hg_norm_w,
                   rw_shift_mu=rw_shift_mu, rw_w0=rw_w0, rw_w2=rw_w2, rw_a0=rw_a0, rw_a2=rw_a2, rw_g2=rw_g2,
                   rw_k_k=rw_k_k, rw_k_a=rw_k_a, rw_r_k=rw_r_k, rw_ln_w=rw_ln_w, rw_ln_b=rw_ln_b, w_out=w_out,
                   norm2_w=norm2_w, w_up=w_up, conv_w=conv_w, conv_b=conv_b, w_down=w_down,
                   final_norm_w=final_norm_w)
    m_in = dict(norm1_w=m_norm1_w, w_in=m_w_in, hg_lb_logits=m_hg_lb_logits, hg_norm_w=m_hg_norm_w,
                rw_shift_mu=m_rw_shift_mu, rw_w0=m_rw_w0, rw_w2=m_rw_w2, rw_a0=m_rw_a0, rw_a2=m_rw_a2,
                rw_g2=m_rw_g2, rw_k_k=m_rw_k_k, rw_k_a=m_rw_k_a, rw_r_k=m_rw_r_k, rw_ln_w=m_rw_ln_w,
                rw_ln_b=m_rw_ln_b, w_out=m_w_out, norm2_w=m_norm2_w, w_up=m_w_up, conv_w=m_conv_w,
                conv_b=m_conv_b, w_down=m_w_down, final_norm_w=m_final_norm_w)
    v_in = dict(norm1_w=v_norm1_w, w_in=v_w_in, hg_lb_logits=v_hg_lb_logits, hg_norm_w=v_hg_norm_w,
                rw_shift_mu=v_rw_shift_mu, rw_w0=v_rw_w0, rw_w2=v_rw_w2, rw_a0=v_rw_a0, rw_a2=v_rw_a2,
                rw_g2=v_rw_g2, rw_k_k=v_rw_k_k, rw_k_a=v_rw_k_a, rw_r_k=v_rw_r_k, rw_ln_w=v_rw_ln_w,
                rw_ln_b=v_rw_ln_b, w_out=v_w_out, norm2_w=v_norm2_w, w_up=v_w_up, conv_w=v_conv_w,
                conv_b=v_conv_b, w_down=v_w_down, final_norm_w=v_final_norm_w)
    names = list(weights)
    sharded_small = ["rw_w2", "rw_a2", "rw_g2", "conv_w"]
    sharded_big = ["w_in", "w_out", "w_up", "w_down"]
    replicated = [n for n in names if n not in sharded_small + sharded_big]

    xs = x[0]
    tgt = loss_target[0]

    small_shard = _pack_rows([weights[n] for n in sharded_small])
    g_win_t, g_small = _all_gather([w_in[0].T.astype(BF16), small_shard], "gather_weights")
    me = _my_index()
    later = [_own_slot_only(z, me) for z in (w_up[0].T.astype(BF16), w_out[0].astype(BF16), w_down[0].astype(BF16))]
    g_send, g_recv, _, later, g_token = _spread_start([], later, "gather_later_start")
    small_shapes = [weights[n].shape for n in sharded_small]
    per_dev = [_unpack(g_small.reshape(N_DEV, -1)[j], small_shapes) for j in range(N_DEV)]
    w2_full, a2_full, g2_full, convw_full = [jnp.concatenate([per_dev[j][i][0] for j in range(N_DEV)], axis=-1)
                                             for i in range(4)]
    zeros64 = jnp.zeros((RW_PAIRS, 64, LANES), F32)
    by_pair = lambda z: z.reshape(z.shape[0], RW_PAIRS, LANES).transpose(1, 0, 2)
    w2p = jnp.concatenate([by_pair(w2_full), zeros64], axis=1)
    a2p = jnp.concatenate([zeros64, by_pair(a2_full)], axis=1)
    g2p = by_pair(g2_full)

    l0, l1 = hg_lb_logits[0:1], hg_lb_logits[1:2]
    h1 = _rms_fwd(xs, norm1_w + g_token[0:1, 0:1], "norm1")
    proj = _mm_nt(h1, g_win_t, "proj_in")
    o_hg, hg_states = _hg_fwd(proj, l0, l1, hg_norm_w, "hgrn2_fwd")
    rws = _shift_fwd(proj, rw_shift_mu, "token_shift")
    rw_vecs = [rw_w0, rw_a0, rw_k_k, rw_k_a, rw_r_k, rw_ln_w, rw_ln_b]
    rw_mats = [w2p, a2p, g2p]
    o_rw, rw_states = _rw_fwd(rws, rw_vecs, rw_mats, "rwkv7_fwd")
    o_mix = jnp.concatenate([o_hg, o_rw], axis=-1).astype(BF16)
    g_wup_t, g_wout, g_wdown = [z.reshape(-1, z.shape[-1])
                                for z in _spread_wait(g_send, g_recv, [], later, o_mix, "gather_later_wait")]
    x1 = _mm_nn(o_mix, g_wout, xs, "proj_out")
    h2 = _rms_fwd(x1, norm2_w, "norm2")
    u = _mm_nt(h2, g_wup_t, "ffn_up")
    act = _ffn_act_fwd(u, convw_full, conv_b, "ffn_act")
    x2 = _mm_nn(act, g_wdown, x1, "ffn_down")
    loss_part, dx2, d_final_w = _loss_head(x2, final_norm_w.reshape(1, -1), tgt, "loss_head")

    d_wdown = _mm_tn(act, dx2, 1408, "ffn_down_dw", BF16)
    dact = _mm_nt(dx2, g_wdown, "ffn_down_dx", BF16)
    du_g, du_v, dcw_g, dcw_v, dcb_g, dcb_v = _ffn_act_bwd(u, dact, convw_full, conv_b, "ffn_act_bwd")
    du = jnp.concatenate([du_g, du_v], axis=-1)
    d_convw = jnp.concatenate([dcw_g, dcw_v], axis=-1)
    d_convb = jnp.concatenate([dcb_g, dcb_v], axis=-1)
    d_wup_t = _mm_tn(du, h2, 1408, "ffn_up_dw", BF16)
    dh2 = _mm_nn(du, g_wup_t, None, "ffn_up_dx")
    dx1, d_norm2 = _rms_bwd(dh2, x1, norm2_w, dx2, "norm2_bwd")
    d_wout = _mm_tn(o_mix, dx1, 512, "proj_out_dw", BF16)
    do = _mm_nt(dx1, g_wout, "proj_out_dx")
    early = [z.reshape(N_DEV, z.shape[0] // N_DEV, z.shape[1]) for z in (d_wup_t, d_wout, d_wdown)]
    early_land = [_own_slot_only(lax.dynamic_index_in_dim(z, me, 0, keepdims=False), me) for z in early]
    e_send, e_recv, early, early_land, e_token = _spread_start(early, early_land, "exchange_early_start")
    hg_norm_w_t = hg_norm_w + e_token[0:1, 0:1]
    dq, df, di, dg, d_l0, d_l1, d_hg_nw = _hg_bwd(proj, hg_states, do, 0, l0, l1, hg_norm_w_t, "hgrn2_bwd")
    rw_out = _rw_bwd(rws, rw_states, do, 1, rw_vecs, rw_mats, "rwkv7_bwd")
    d_rw_vecs = rw_out[4:4 + N_RW_VEC]
    d_w2p, d_a2p, d_g2p = rw_out[4 + N_RW_VEC:]
    dp_parts, dmu_parts = [], []
    for i, z in enumerate(rw_out[:4]):
        dp, dmu = _shift_bwd(z, proj, rw_shift_mu, i * RW_WIDTH, "token_shift_bwd_%d" % i)
        dp_parts.append(dp)
        dmu_parts.append(dmu)
    d_mu = jnp.concatenate(dmu_parts, axis=-1)
    dproj = jnp.concatenate([dq, df, di, dg] + dp_parts, axis=-1)
    d_win_t = _mm_tn(dproj, h1, 768, "proj_in_dw", BF16)
    dh1 = _mm_nn(dproj, g_win_t, None, "proj_in_dx")
    grad_x, d_norm1 = _rms_bwd(dh1, xs, norm1_w, dx1, "norm1_bwd")

    from_pairs = lambda z: z.transpose(1, 0, 2).reshape(z.shape[1], RW_WIDTH)
    d_w2 = from_pairs(d_w2p[:, :64])
    d_a2 = from_pairs(d_a2p[:, 64:])
    d_g2 = from_pairs(d_g2p)
    col_blocks = lambda z: z.reshape(z.shape[0], N_DEV, -1).transpose(1, 0, 2)
    small_part = jnp.stack([
        _pack_rows([col_blocks(d_w2)[j], col_blocks(d_a2)[j], col_blocks(d_g2)[j], col_blocks(d_convw)[j]])
        for j in range(N_DEV)])
    rep_grads = dict(norm1_w=d_norm1, hg_lb_logits=jnp.concatenate([d_l0, d_l1], axis=0), hg_norm_w=d_hg_nw,
                     rw_shift_mu=d_mu, rw_w0=d_rw_vecs[0], rw_a0=d_rw_vecs[1], rw_k_k=d_rw_vecs[2],
                     rw_k_a=d_rw_vecs[3], rw_r_k=d_rw_vecs[4], rw_ln_w=d_rw_vecs[5], rw_ln_b=d_rw_vecs[6],
                     norm2_w=d_norm2, conv_b=d_convb, final_norm_w=d_final_w)
    rep_pack = _pack_rows([loss_part] + [rep_grads[n] for n in replicated])
    rep_part = jnp.broadcast_to(rep_pack[None], (N_DEV,) + rep_pack.shape)
    blocks = lambda z: z.reshape(N_DEV, z.shape[0] // N_DEV, z.shape[1])
    landed_early = _spread_wait(e_send, e_recv, early, early_land, grad_x, "exchange_early_wait")
    landed_late = _exchange([blocks(d_win_t), small_part, rep_part], "exchange_grads")
    landed = [landed_late[0]] + landed_early + list(landed_late[1:])
    sums = [_sum_slots(z, "sum_grads_%d" % i) for i, z in enumerate(landed)]
    g_small_sum = _unpack(sums[4], small_shapes)
    rep_sum = _unpack(sums[5], [(1, 1)] + [weights[n].shape for n in replicated])
    loss = rep_sum[0].reshape(())
    grads = dict(zip(replicated, rep_sum[1:]))
    grads.update(dict(zip(sharded_small, g_small_sum)))
    grads["w_in"] = sums[0].T[None]
    grads["w_up"] = sums[1].T[None]
    grads["w_out"] = sums[2][None]
    grads["w_down"] = sums[3][None]

    delta, new_m, new_v = {}, {}, {}
    for n in sharded_big:
        shp = weights[n].shape
        as2d = lambda z: z.reshape(shp[1], shp[2])
        d, nm, nv = _adamw(as2d(weights[n]), as2d(grads[n]), as2d(m_in[n]), as2d(v_in[n]), "adamw_" + n)
        delta[n], new_m[n], new_v[n] = d.reshape(shp), nm.reshape(shp), nv.reshape(shp)
    small_names = replicated + sharded_small
    packs = [_pack_rows([src[n] for n in small_names]) for src in (weights, grads, m_in, v_in)]
    outs = _adamw(*packs, "adamw_small")
    small_shapes_all = [weights[n].shape for n in small_names]
    for dst, packed in zip((delta, new_m, new_v), outs):
        dst.update(dict(zip(small_names, _unpack(packed, small_shapes_all))))

    return (loss, grad_x[None], *[grads[n] for n in names], *[delta[n] for n in names],
            *[new_m[n] for n in names], *[new_v[n] for n in names])
```

```python
import functools

import jax
import jax.numpy as jnp
from jax import lax
from jax.experimental import pallas as pl
from jax.experimental.pallas import tpu as pltpu

F32 = jnp.float32
BF16 = jnp.bfloat16
HIGHEST = lax.Precision.HIGHEST
SCAN_PRECISION = None
MESH_ID = pl.DeviceIdType.MESH

N_DEV = 8
D_MODEL = 1024
HG_WIDTH = 512
HG_HEAD_DIM = 128
HG_HEADS = 4
RW_WIDTH = 512
RW_PAIRS = 4
RW_HEAD_DIM = 64
HG_COLS = 2048
RW_COLS = 1792
D_FF = 2816
NORM_EPS = 1e-6
RW_GN_EPS = 64e-5
L2_EPS = 1e-12
ADAM_LR, ADAM_B1, ADAM_B2, ADAM_EPS, ADAM_WD, ADAM_STEP = 0.001, 0.9, 0.999, 1e-08, 0.01, 10

HG_CHUNK = 16
RW_CHUNK = 64
SCAN_ROWS = 256
LANES = 128

NN = ((1,), (0,))
NT = ((1,), (1,))
TN = ((0,), (0,))


def _dot(a, b, dims=NN, precision=SCAN_PRECISION):
    if precision is None:
        a, b = a.astype(BF16), b.astype(BF16)
    return lax.dot_general(a, b, (dims, ((), ())), precision=precision, preferred_element_type=F32)


def _iota2(shape, dim):
    return lax.broadcasted_iota(jnp.int32, shape, dim)


def _sigmoid(z):
    return 1.0 / (1.0 + jnp.exp(-z))


def _row_tile(n, want):
    t = min(n, want)
    while n % t:
        t //= 2
    return t


def _rms_fwd(x, w, name):
    T, D = x.shape
    tb = _row_tile(T, 512)

    def body(x_ref, w_ref, h_ref):
        xv = x_ref[...]
        r = lax.rsqrt(jnp.mean(xv * xv, axis=-1, keepdims=True) + NORM_EPS)
        h_ref[...] = (xv * r * w_ref[...]).astype(h_ref.dtype)

    return pl.pallas_call(
        body, name=name, grid=(T // tb,),
        in_specs=[pl.BlockSpec((tb, D), lambda i: (i, 0)), pl.BlockSpec((1, D), lambda i: (0, 0))],
        out_specs=pl.BlockSpec((tb, D), lambda i: (i, 0)),
        out_shape=jax.ShapeDtypeStruct((T, D), BF16),
    )(x, w)


def _rms_bwd(dh, x, w, dres, name):
    T, D = x.shape
    tb = _row_tile(T, 256)

    def body(dh_ref, x_ref, w_ref, dres_ref, dx_ref, dw_ref):
        @pl.when(pl.program_id(0) == 0)
        def _():
            dw_ref[...] = jnp.zeros_like(dw_ref)

        xv = x_ref[...]
        r = lax.rsqrt(jnp.mean(xv * xv, axis=-1, keepdims=True) + NORM_EPS)
        xn = xv * r
        dy = dh_ref[...].astype(F32)
        dxn = dy * w_ref[...]
        dx_ref[...] = dres_ref[...] + r * (dxn - xn * jnp.mean(dxn * xn, axis=-1, keepdims=True))
        dw_ref[...] += jnp.sum(dy * xn, axis=0, keepdims=True)

    row = pl.BlockSpec((tb, D), lambda i: (i, 0))
    vec = pl.BlockSpec((1, D), lambda i: (0, 0))
    return pl.pallas_call(
        body, name=name, grid=(T // tb,),
        in_specs=[row, row, vec, row], out_specs=[row, vec],
        out_shape=[jax.ShapeDtypeStruct((T, D), F32), jax.ShapeDtypeStruct((1, D), F32)],
    )(dh, x, w, dres)


def _mm_nt(a, bt, name, out_dtype=F32):
    T, K = a.shape
    N = bt.shape[0]
    tm = _row_tile(T, 256)

    def body(a_ref, b_ref, o_ref):
        o_ref[...] = _dot(a_ref[...].astype(BF16), b_ref[...].astype(BF16), NT, None).astype(o_ref.dtype)

    return pl.pallas_call(
        body, name=name, grid=(T // tm,),
        in_specs=[pl.BlockSpec((tm, K), lambda i: (i, 0)), pl.BlockSpec((N, K), lambda i: (0, 0))],
        out_specs=pl.BlockSpec((tm, N), lambda i: (i, 0)),
        out_shape=jax.ShapeDtypeStruct((T, N), out_dtype),
    )(a, bt)


def _mm_nn(a, b, res, name, out_dtype=F32):
    T, K = a.shape
    N = b.shape[1]
    tm = _row_tile(T, 256)

    def body(a_ref, b_ref, *rest):
        o_ref = rest[-1]
        acc = _dot(a_ref[...].astype(BF16), b_ref[...].astype(BF16), NN, None)
        if res is not None:
            acc = acc + rest[0][...]
        o_ref[...] = acc.astype(o_ref.dtype)

    in_specs = [pl.BlockSpec((tm, K), lambda i: (i, 0)), pl.BlockSpec((K, N), lambda i: (0, 0))]
    args = [a, b]
    if res is not None:
        in_specs.append(pl.BlockSpec((tm, N), lambda i: (i, 0)))
        args.append(res)
    return pl.pallas_call(
        body, name=name, grid=(T // tm,), in_specs=in_specs,
        out_specs=pl.BlockSpec((tm, N), lambda i: (i, 0)),
        out_shape=jax.ShapeDtypeStruct((T, N), out_dtype),
    )(*args)


def _mm_tn(a, b, tmm, name, out_dtype=F32):
    T, M = a.shape
    N = b.shape[1]
    tk = _row_tile(T, 512)
    nk = T // tk

    def body(a_ref, b_ref, o_ref, acc_ref):
        @pl.when(pl.program_id(1) == 0)
        def _():
            acc_ref[...] = jnp.zeros_like(acc_ref)

        acc_ref[...] += _dot(a_ref[...].astype(BF16), b_ref[...].astype(BF16), TN, None)

        @pl.when(pl.program_id(1) == nk - 1)
        def _():
            o_ref[...] = acc_ref[...].astype(o_ref.dtype)

    return pl.pallas_call(
        body, name=name, grid=(M // tmm, nk),
        in_specs=[pl.BlockSpec((tk, tmm), lambda m, k: (k, m)), pl.BlockSpec((tk, N), lambda m, k: (k, 0))],
        out_specs=pl.BlockSpec((tmm, N), lambda m, k: (m, 0)),
        out_shape=jax.ShapeDtypeStruct((M, N), out_dtype),
        scratch_shapes=[pltpu.VMEM((tmm, N), F32)],
    )(a, b)


def _shift_rows_down(z, n):
    rows = _iota2(z.shape, 0)
    return jnp.where(rows < n, 0.0, pltpu.roll(z, n, 0))


def _shift_rows_up(z, n):
    T = z.shape[0]
    rows = _iota2(z.shape, 0)
    return jnp.where(rows >= T - n, 0.0, pltpu.roll(z, T - n, 0))


def _shift_fwd(proj, mu, name):
    T = proj.shape[0]
    nblk = RW_COLS // LANES
    first = HG_COLS // LANES

    def body(p_ref, mu_ref, o_ref):
        p = p_ref[...]
        o_ref[...] = p + (_shift_rows_down(p, 1) - p) * mu_ref[...]

    return pl.pallas_call(
        body, name=name, grid=(nblk,),
        in_specs=[pl.BlockSpec((T, LANES), lambda j: (0, first + j)), pl.BlockSpec((1, LANES), lambda j: (0, j))],
        out_specs=pl.BlockSpec((T, LANES), lambda j: (0, j)),
        out_shape=jax.ShapeDtypeStruct((T, RW_COLS), F32),
    )(proj, mu)


def _shift_bwd(ds, proj, mu, col0, name):
    T, width = ds.shape
    nblk = width // LANES
    first = (HG_COLS + col0) // LANES
    mu0 = col0 // LANES

    def body(ds_ref, p_ref, mu_ref, dp_ref, dmu_ref):
        dsv = ds_ref[...]
        p = p_ref[...]
        m = mu_ref[...]
        dp_ref[...] = (dsv * (1.0 - m) + _shift_rows_up(dsv * m, 1)).astype(dp_ref.dtype)
        dmu_ref[...] = jnp.sum(dsv * (_shift_rows_down(p, 1) - p), axis=0, keepdims=True)

    return pl.pallas_call(
        body, name=name, grid=(nblk,),
        in_specs=[pl.BlockSpec((T, LANES), lambda j: (0, j)),
                  pl.BlockSpec((T, LANES), lambda j: (0, first + j)),
                  pl.BlockSpec((1, LANES), lambda j: (0, mu0 + j))],
        out_specs=[pl.BlockSpec((T, LANES), lambda j: (0, j)), pl.BlockSpec((1, LANES), lambda j: (0, j))],
        out_shape=[jax.ShapeDtypeStruct((T, width), BF16), jax.ShapeDtypeStruct((1, width), F32)],
    )(ds, proj, mu)


def _conv3(z, w_ref):
    return w_ref[0:1, :] * _shift_rows_down(z, 2) + w_ref[1:2, :] * _shift_rows_down(z, 1) + w_ref[2:3, :] * z


def _ffn_act_fwd(u, conv_w, conv_b, name):
    T = u.shape[0]
    nblk = D_FF // LANES

    def body(ug_ref, uv_ref, wg_ref, wv_ref, bg_ref, bv_ref, act_ref):
        gate = _conv3(ug_ref[...], wg_ref) + bg_ref[...]
        val = _conv3(uv_ref[...], wv_ref) + bv_ref[...]
        act_ref[...] = (gate * _sigmoid(gate) * val).astype(act_ref.dtype)

    col = lambda off: pl.BlockSpec((T, LANES), lambda j: (0, off + j))
    wsp = lambda off: pl.BlockSpec((3, LANES), lambda j: (0, off + j))
    bsp = lambda off: pl.BlockSpec((1, LANES), lambda j: (0, off + j))
    return pl.pallas_call(
        body, name=name, grid=(nblk,),
        in_specs=[col(0), col(nblk), wsp(0), wsp(nblk), bsp(0), bsp(nblk)],
        out_specs=pl.BlockSpec((T, LANES), lambda j: (0, j)),
        out_shape=jax.ShapeDtypeStruct((T, D_FF), BF16),
    )(u, u, conv_w, conv_w, conv_b, conv_b)


def _ffn_act_bwd(u, dact, conv_w, conv_b, name):
    T = u.shape[0]
    nblk = D_FF // LANES

    def conv_bwd(z, dzc, w_ref, du_ref, dw_ref, db_ref):
        du = w_ref[2:3, :] * dzc + w_ref[1:2, :] * _shift_rows_up(dzc, 1) + w_ref[0:1, :] * _shift_rows_up(dzc, 2)
        du_ref[...] = du.astype(du_ref.dtype)
        dw_ref[0:1, :] = jnp.sum(dzc * _shift_rows_down(z, 2), axis=0, keepdims=True)
        dw_ref[1:2, :] = jnp.sum(dzc * _shift_rows_down(z, 1), axis=0, keepdims=True)
        dw_ref[2:3, :] = jnp.sum(dzc * z, axis=0, keepdims=True)
        db_ref[...] = jnp.sum(dzc, axis=0, keepdims=True)

    def body(ug_ref, uv_ref, da_ref, wg_ref, wv_ref, bg_ref, bv_ref,
             dug_ref, duv_ref, dwg_ref, dwv_ref, dbg_ref, dbv_ref):
        ug, uv = ug_ref[...], uv_ref[...]
        gate = _conv3(ug, wg_ref) + bg_ref[...]
        val = _conv3(uv, wv_ref) + bv_ref[...]
        da = da_ref[...].astype(F32)
        sg = _sigmoid(gate)
        dgate = da * val * (sg * (1.0 + gate * (1.0 - sg)))
        dval = da * gate * sg
        conv_bwd(ug, dgate, wg_ref, dug_ref, dwg_ref, dbg_ref)
        conv_bwd(uv, dval, wv_ref, duv_ref, dwv_ref, dbv_ref)

    col = lambda off: pl.BlockSpec((T, LANES), lambda j: (0, off + j))
    wsp = lambda off: pl.BlockSpec((3, LANES), lambda j: (0, off + j))
    bsp = lambda off: pl.BlockSpec((1, LANES), lambda j: (0, off + j))
    half = lambda r, dt: jax.ShapeDtypeStruct((r, D_FF), dt)
    return pl.pallas_call(
        body, name=name, grid=(nblk,),
        in_specs=[col(0), col(nblk), col(0), wsp(0), wsp(nblk), bsp(0), bsp(nblk)],
        out_specs=[col(0), col(0), wsp(0), wsp(0), bsp(0), bsp(0)],
        out_shape=[half(T, BF16), half(T, BF16), half(3, F32), half(3, F32), half(1, F32), half(1, F32)],
    )(u, u, dact, conv_w, conv_w, conv_b, conv_b)


def _loss_head(x2, w, target, name):
    T, D = x2.shape
    tb = _row_tile(T, 256)

    def body(x_ref, w_ref, t_ref, loss_ref, dx_ref, dw_ref):
        @pl.when(pl.program_id(0) == 0)
        def _():
            loss_ref[...] = jnp.zeros_like(loss_ref)
            dw_ref[...] = jnp.zeros_like(dw_ref)

        xv = x_ref[...]
        r = lax.rsqrt(jnp.mean(xv * xv, axis=-1, keepdims=True) + NORM_EPS)
        xn = xv * r
        err = xn * w_ref[...] - t_ref[...]
        row_loss = jnp.sum(err * err, axis=-1, keepdims=True) * (0.5 / D)
        loss_ref[...] += jnp.sum(row_loss, axis=0, keepdims=True)
        dy = err * (1.0 / D)
        dxn = dy * w_ref[...]
        dx_ref[...] = r * (dxn - xn * jnp.mean(dxn * xn, axis=-1, keepdims=True))
        dw_ref[...] += jnp.sum(dy * xn, axis=0, keepdims=True)

    row = pl.BlockSpec((tb, D), lambda i: (i, 0))
    vec = pl.BlockSpec((1, D), lambda i: (0, 0))
    return pl.pallas_call(
        body, name=name, grid=(T // tb,),
        in_specs=[row, vec, row],
        out_specs=[pl.BlockSpec((1, 1), lambda i: (0, 0)), row, vec],
        out_shape=[jax.ShapeDtypeStruct((1, 1), F32), jax.ShapeDtypeStruct((T, D), F32),
                   jax.ShapeDtypeStruct((1, D), F32)],
    )(x2, w, target)


def _adamw(w, g, m, v, name):
    R, C = w.shape
    tb = _row_tile(R, 256) if R % 8 == 0 else R

    def body(w_ref, g_ref, m_ref, v_ref, d_ref, nm_ref, nv_ref):
        gv = g_ref[...]
        nm = ADAM_B1 * m_ref[...] + (1.0 - ADAM_B1) * gv
        nv = ADAM_B2 * v_ref[...] + (1.0 - ADAM_B2) * (gv * gv)
        m_hat = nm / (1.0 - ADAM_B1 ** ADAM_STEP)
        v_hat = nv / (1.0 - ADAM_B2 ** ADAM_STEP)
        d_ref[...] = -ADAM_LR * (m_hat / (jnp.sqrt(v_hat) + ADAM_EPS) + ADAM_WD * w_ref[...])
        nm_ref[...] = nm
        nv_ref[...] = nv

    blk = pl.BlockSpec((tb, C), lambda i: (i, 0))
    sd = jax.ShapeDtypeStruct((R, C), F32)
    return pl.pallas_call(
        body, name=name, grid=(R // tb,), in_specs=[blk] * 4, out_specs=[blk] * 3, out_shape=[sd] * 3,
    )(w, g, m, v)


def _chunk_masks(rows, chunk):
    shift = chunk.bit_length() - 1
    i, j = _iota2((rows, rows), 0), _iota2((rows, rows), 1)
    same = jnp.right_shift(i, shift) == jnp.right_shift(j, shift)
    return same.astype(F32), (same & (j <= i)).astype(F32), (same & (j < i)).astype(F32)


def _head_lanes(h):
    return slice(h * LANES, (h + 1) * LANES)


def _chunk_rows(c, chunk):
    return pl.ds(pl.multiple_of(c * chunk, chunk), chunk)


def _hg_consts(rows):
    same, tril, _ = _chunk_masks(rows, HG_CHUNK)
    shift = HG_CHUNK.bit_length() - 1
    i, j = _iota2((rows, rows), 0), _iota2((rows, rows), 1)
    mid_row = jnp.left_shift(jnp.right_shift(i, shift), shift) + (HG_CHUNK // 2 - 1)
    upto_mid = (same > 0) & (j <= mid_row)
    return jnp.concatenate([tril, same, upto_mid.astype(F32)], axis=0), tril


def _hg_prep(consts, qr, fr, ir, l0, l1):
    sums, tril = consts
    rows = tril.shape[0]
    lb = _sigmoid(l0 - l1)
    f = lb + (1.0 - lb) * _sigmoid(fr)
    q = qr * _sigmoid(qr) * (HG_HEAD_DIM ** -0.5)
    k = 1.0 - f
    g = jnp.log(f)
    acc = _dot(sums, g, NN, HIGHEST)
    a = acc[:rows]
    tot = acc[rows:2 * rows]
    mid = acc[2 * rows:]
    att = _dot(q * jnp.exp(a - mid), k * jnp.exp(mid - a), NT) * tril
    return q * jnp.exp(a), _dot(att, ir), k * jnp.exp(tot - a), jnp.exp(tot)


def _hg_post(o, gr, nw):
    on = o * lax.rsqrt(jnp.mean(o * o, axis=-1, keepdims=True) + NORM_EPS)
    return on * nw * (gr * _sigmoid(gr))


def _hg_specs(T, tb, rev):
    nT = T // tb
    tix = (lambda t: nT - 1 - t) if rev else (lambda t: t)
    col = lambda blk: pl.BlockSpec((tb, HG_WIDTH), lambda t: (tix(t), blk))
    vec = pl.BlockSpec((1, HG_WIDTH), lambda t: (0, 0))
    st = pl.BlockSpec((HG_HEADS, tb // HG_CHUNK, HG_HEAD_DIM, HG_HEAD_DIM), lambda t: (0, tix(t), 0, 0))
    return nT, col, vec, st


def _hg_fwd(proj, l0, l1, nw, name):
    T = proj.shape[0]
    tb = _row_tile(T, SCAN_ROWS)
    nsub = tb // HG_CHUNK
    nT, col, vec, st = _hg_specs(T, tb, False)

    def body(q_ref, f_ref, i_ref, g_ref, l0_ref, l1_ref, nw_ref, o_ref, st_ref, s_ref, qe_ref, kd_ref, dec_ref):
        @pl.when(pl.program_id(0) == 0)
        def _():
            s_ref[...] = jnp.zeros_like(s_ref)

        consts = _hg_consts(tb)
        for h in range(HG_HEADS):
            ln = _head_lanes(h)
            qe, o_intra, kd, dec = _hg_prep(consts, q_ref[:, ln], f_ref[:, ln], i_ref[:, ln],
                                            l0_ref[:, ln], l1_ref[:, ln])
            qe_ref[h], kd_ref[h], dec_ref[h] = qe, kd, dec
            o_ref[:, ln] = o_intra

        def step(c, carry):
            rows = _chunk_rows(c, HG_CHUNK)
            for h in range(HG_HEADS):
                ln = _head_lanes(h)
                S = s_ref[h]
                st_ref[h, c] = S
                o_ref[rows, ln] += _dot(qe_ref[h, rows, :], S, NT)
                s_ref[h] = S * dec_ref[h, pl.ds(c * HG_CHUNK, 1), :] + _dot(i_ref[rows, ln], kd_ref[h, rows, :], TN)
            return carry

        lax.fori_loop(0, nsub, step, 0)
        for h in range(HG_HEADS):
            ln = _head_lanes(h)
            o_ref[:, ln] = _hg_post(o_ref[:, ln], g_ref[:, ln], nw_ref[:, ln])

    blk = pltpu.VMEM((HG_HEADS, tb, LANES), F32)
    return pl.pallas_call(
        body, name=name, grid=(nT,),
        in_specs=[col(0), col(1), col(2), col(3), vec, vec, vec],
        out_specs=[col(0), st],
        out_shape=[jax.ShapeDtypeStruct((T, HG_WIDTH), F32),
                   jax.ShapeDtypeStruct((HG_HEADS, T // HG_CHUNK, HG_HEAD_DIM, HG_HEAD_DIM), F32)],
        scratch_shapes=[pltpu.VMEM((HG_HEADS, HG_HEAD_DIM, HG_HEAD_DIM), F32), blk, blk, blk],
    )(proj, proj, proj, proj, l0, l1, nw)


def _hg_bwd(proj, states, do, do_blk, l0, l1, nw, name):
    T = proj.shape[0]
    tb = _row_tile(T, SCAN_ROWS)
    nsub = tb // HG_CHUNK
    nT, col, vec, st = _hg_specs(T, tb, True)

    def body(q_ref, f_ref, i_ref, g_ref, st_ref, do_ref, l0_ref, l1_ref, nw_ref,
             dq_ref, df_ref, di_ref, dg_ref, dl0_ref, dl1_ref, dnw_ref,
             ds_ref, qe_ref, kd_ref, dec_ref, o_ref, dqe_ref, dkd_ref, ddec_ref, dis_ref):
        @pl.when(pl.program_id(0) == 0)
        def _():
            ds_ref[...] = jnp.zeros_like(ds_ref)
            dl0_ref[...] = jnp.zeros_like(dl0_ref)
            dl1_ref[...] = jnp.zeros_like(dl1_ref)
            dnw_ref[...] = jnp.zeros_like(dnw_ref)

        consts = _hg_consts(tb)
        prep_vjps = []
        for h in range(HG_HEADS):
            ln = _head_lanes(h)
            (qe, o_intra, kd, dec), vjp = jax.vjp(
                functools.partial(_hg_prep, consts), q_ref[:, ln], f_ref[:, ln], i_ref[:, ln],
                l0_ref[:, ln], l1_ref[:, ln])
            prep_vjps.append(vjp)
            qe_ref[h], kd_ref[h], dec_ref[h], o_ref[h] = qe, kd, dec, o_intra

        def redo(c, carry):
            rows = _chunk_rows(c, HG_CHUNK)
            for h in range(HG_HEADS):
                o_ref[h, rows, :] += _dot(qe_ref[h, rows, :], st_ref[h, c], NT)
            return carry

        lax.fori_loop(0, nsub, redo, 0)
        for h in range(HG_HEADS):
            ln = _head_lanes(h)
            _, vjp = jax.vjp(_hg_post, o_ref[h], g_ref[:, ln], nw_ref[:, ln])
            d_o, dgr, dnw = vjp(do_ref[:, ln])
            o_ref[h] = d_o
            dg_ref[:, ln] = dgr.astype(dg_ref.dtype)
            dnw_ref[:, ln] += dnw
        ddec_ref[...] = jnp.zeros_like(ddec_ref)

        def step(i, carry):
            c = nsub - 1 - i
            rows = _chunk_rows(c, HG_CHUNK)
            row0 = pl.ds(c * HG_CHUNK, 1)
            for h in range(HG_HEADS):
                ln = _head_lanes(h)
                G = ds_ref[h]
                S = st_ref[h, c]
                d_o = o_ref[h, rows, :]
                dqe_ref[h, rows, :] = _dot(d_o, S)
                dkd_ref[h, rows, :] = _dot(i_ref[rows, ln], G)
                dis_ref[h, rows, :] = _dot(kd_ref[h, rows, :], G, NT)
                ddec_ref[h, row0, :] = jnp.sum(S * G, axis=0, keepdims=True)
                ds_ref[h] = G * dec_ref[h, row0, :] + _dot(d_o, qe_ref[h, rows, :], TN)
            return carry

        lax.fori_loop(0, nsub, step, 0)
        for h in range(HG_HEADS):
            ln = _head_lanes(h)
            dq, df, di, dl0, dl1 = prep_vjps[h]((dqe_ref[h], o_ref[h], dkd_ref[h], ddec_ref[h]))
            dq_ref[:, ln] = dq.astype(dq_ref.dtype)
            df_ref[:, ln] = df.astype(df_ref.dtype)
            di_ref[:, ln] = (di + dis_ref[h]).astype(di_ref.dtype)
            dl0_ref[:, ln] += dl0
            dl1_ref[:, ln] += dl1

    dcol = jax.ShapeDtypeStruct((T, HG_WIDTH), BF16)
    dvec = jax.ShapeDtypeStruct((1, HG_WIDTH), F32)
    blk = pltpu.VMEM((HG_HEADS, tb, LANES), F32)
    return pl.pallas_call(
        body, name=name, grid=(nT,),
        in_specs=[col(0), col(1), col(2), col(3), st, col(do_blk), vec, vec, vec],
        out_specs=[col(0)] * 4 + [vec] * 3,
        out_shape=[dcol] * 4 + [dvec] * 3,
        scratch_shapes=[pltpu.VMEM((HG_HEADS, HG_HEAD_DIM, HG_HEAD_DIM), F32)] + [blk] * 8,
    )(proj, proj, proj, proj, states, do, l0, l1, nw)


def _rw_consts(rows):
    same, tril, stril = _chunk_masks(rows, RW_CHUNK)
    br, bc = _iota2((LANES, LANES), 0), _iota2((LANES, LANES), 1)
    blockdiag = ((br < RW_HEAD_DIM) == (bc < RW_HEAD_DIM)).astype(F32)
    m0 = (_iota2((1, LANES), 1) < RW_HEAD_DIM).astype(F32)
    return same, tril, stril, blockdiag, m0, 1.0 - m0


def _unit_lower_inverse_impl(low):
    rows = low.shape[0]
    x = low + (_iota2(low.shape, 0) == _iota2(low.shape, 1)).astype(F32)
    p = _dot(low, low)
    n = 4
    while n < RW_CHUNK:
        z = _dot(jnp.concatenate([p, x], axis=0), p)
        p, x = z[:rows], x + z[rows:]
        n *= 2
    return x + _dot(x, p)


@jax.custom_vjp
def _unit_lower_inverse(low):
    return _unit_lower_inverse_impl(low)


def _unit_lower_inverse_fwd(low):
    x = _unit_lower_inverse_impl(low)
    return x, x


def _unit_lower_inverse_bwd(x, dx):
    return (_dot(_dot(x, dx, TN), x, NT),)


_unit_lower_inverse.defvjp(_unit_lower_inverse_fwd, _unit_lower_inverse_bwd)


N_PREP_OUT = 9


def _rw_prep(consts, r, kx, v, lw, gd, w0, a0, k_k, k_a, w2p, a2p, g2):
    same, tril, stril, blockdiag, m0, m1 = consts
    xw = w0 + _dot(jnp.tanh(lw), w2p)
    w = jnp.minimum(xw, 0.0) - jnp.log(1.0 + jnp.exp(-jnp.abs(xw))) - 0.5
    ld = -jnp.exp(w)
    a_s = _sigmoid(a0 + _dot(lw, a2p))
    g = _dot(_sigmoid(gd), g2)
    kk = kx * k_k
    kk = kk / jnp.maximum(jnp.sqrt(_dot(kk * kk, blockdiag)), L2_EPS)
    k2 = kx * (1.0 + (a_s - 1.0) * k_a)
    bv = kk * a_s
    rows = tril.shape[0]
    acc = _dot(jnp.concatenate([tril, same], axis=0), ld, NN, HIGHEST)
    cum, tot = acc[:rows], acc[rows:]
    ecn = jnp.exp(-cum)
    a_t = -kk * jnp.exp(cum - ld)
    b_h = bv * ecn
    k_h = k2 * ecn
    r_t = r * jnp.exp(cum)
    rem = jnp.exp(tot - cum)
    z = _dot(jnp.concatenate([a_t * m0, a_t * m1, r_t * m0, r_t * m1], axis=0),
             jnp.concatenate([b_h, k_h], axis=0), NT)
    W = U = Q = Y0 = 0.0
    for h, m in enumerate((m0, m1)):
        za, zr = z[h * rows:(h + 1) * rows], z[(2 + h) * rows:(3 + h) * rows]
        lak = za[:, rows:] * stril
        mrb = zr[:, :rows] * tril
        mrk = zr[:, rows:] * tril
        tinv = _unit_lower_inverse(za[:, :rows] * stril)
        lv = _dot(jnp.concatenate([lak, mrk], axis=0), v)
        wu = _dot(tinv, jnp.concatenate([a_t * m, lv[:rows]], axis=1))
        w_m, u_m = wu[:, :LANES], m * wu[:, LANES:]
        qy = _dot(mrb, jnp.concatenate([w_m, u_m], axis=1))
        W = W + w_m
        U = U + u_m
        Q = Q + r_t * m + qy[:, :LANES]
        Y0 = Y0 + qy[:, LANES:] + m * lv[rows:]
    return W, U, Q, Y0, bv * rem, k2 * rem, jnp.exp(tot), k2, g


def _rw_post(blockdiag, y, r, v, k2, g, r_k, ln_w, ln_b):
    inv_n = 1.0 / RW_HEAD_DIM
    yc = y - _dot(y, blockdiag) * inv_n
    var = _dot(yc * yc, blockdiag) * inv_n
    yn = yc * lax.rsqrt(var + RW_GN_EPS) * ln_w + ln_b
    bonus = _dot(r * k2 * r_k, blockdiag) * v
    return (yn + bonus) * g


N_RW_VEC = 7
N_RW_MAT = 3


def _rw_specs(T, tb, rev):
    nT = T // tb
    tix = (lambda t: nT - 1 - t) if rev else (lambda t: t)
    wide = lambda blk: pl.BlockSpec((tb, RW_WIDTH), lambda t: (tix(t), blk))
    narrow = lambda blk: pl.BlockSpec((tb, LANES), lambda t: (tix(t), blk))
    vec = pl.BlockSpec((1, RW_WIDTH), lambda t: (0, 0))
    mat = pl.BlockSpec((RW_PAIRS, LANES, LANES), lambda t: (0, 0, 0))
    st = pl.BlockSpec((RW_PAIRS, tb // RW_CHUNK, LANES, LANES), lambda t: (0, tix(t), 0, 0))
    lora0 = 3 * RW_WIDTH // LANES
    ins = [wide(0), wide(1), wide(2), narrow(lora0), narrow(lora0 + 1)]
    return nT, wide, vec, mat, st, ins


def _rw_prep_args(p, r_ref, k_ref, v_ref, lw_ref, gd_ref, vrefs, mrefs):
    ln = _head_lanes(p)
    w0, a0, k_k, k_a = [x[:, ln] for x in vrefs[:4]]
    return (r_ref[:, ln], k_ref[:, ln], v_ref[:, ln], lw_ref[...], gd_ref[...], w0, a0, k_k, k_a,
            *[x[p] for x in mrefs])


def _stack_chunks(ref, top, bottom):
    C = RW_CHUNK
    for c in range(ref.shape[0]):
        ref[c, 0:C, :] = top[c * C:(c + 1) * C]
        ref[c, C:2 * C, :] = bottom[c * C:(c + 1) * C]


def _rw_fwd(rws, vecs, mats, name):
    T = rws.shape[0]
    tb = _row_tile(T, SCAN_ROWS)
    nsub = tb // RW_CHUNK
    C = RW_CHUNK
    nT, wide, vec, mat, st, ins = _rw_specs(T, tb, False)

    def body(*refs):
        r_ref, k_ref, v_ref, lw_ref, gd_ref = refs[:5]
        vrefs = refs[5:5 + N_RW_VEC]
        mrefs = refs[5 + N_RW_VEC:5 + N_RW_VEC + N_RW_MAT]
        o_ref, st_ref, s_ref, wq_ref, uy_ref, bk_ref, misc_ref, y_ref = refs[-8:]

        @pl.when(pl.program_id(0) == 0)
        def _():
            s_ref[...] = jnp.zeros_like(s_ref)

        consts = _rw_consts(tb)
        blockdiag = consts[3]
        for p in range(RW_PAIRS):
            W, U, Q, Y0, Bg, Kg, dec, k2, g = _rw_prep(
                consts, *_rw_prep_args(p, r_ref, k_ref, v_ref, lw_ref, gd_ref, vrefs, mrefs))
            _stack_chunks(wq_ref.at[p], W, Q)
            _stack_chunks(uy_ref.at[p], U, Y0)
            _stack_chunks(bk_ref.at[p], Bg, Kg)
            misc_ref[0, p], misc_ref[1, p], misc_ref[2, p] = dec, k2, g

        def step(c, carry):
            rows = _chunk_rows(c, C)
            for p in range(RW_PAIRS):
                S = s_ref[p]
                st_ref[p, c] = S
                py = _dot(wq_ref[p, c], S, NT) + uy_ref[p, c]
                y_ref[p, rows, :] = py[C:]
                pv = jnp.concatenate([py[:C], v_ref[rows, _head_lanes(p)]], axis=0)
                s_ref[p] = (S * misc_ref[0, p, pl.ds(c * C, 1), :] + _dot(pv, bk_ref[p, c], TN)) * blockdiag
            return carry

        lax.fori_loop(0, nsub, step, 0)
        for p in range(RW_PAIRS):
            ln = _head_lanes(p)
            r_k, ln_w, ln_b = [x[:, ln] for x in vrefs[4:]]
            o_ref[:, ln] = _rw_post(blockdiag, y_ref[p], r_ref[:, ln], v_ref[:, ln], misc_ref[1, p], misc_ref[2, p],
                                    r_k, ln_w, ln_b)

    stacked = pltpu.VMEM((RW_PAIRS, nsub, 2 * C, LANES), F32)
    return pl.pallas_call(
        body, name=name, grid=(nT,),
        in_specs=ins + [vec] * N_RW_VEC + [mat] * N_RW_MAT,
        out_specs=[wide(0), st],
        out_shape=[jax.ShapeDtypeStruct((T, RW_WIDTH), F32),
                   jax.ShapeDtypeStruct((RW_PAIRS, T // RW_CHUNK, LANES, LANES), F32)],
        scratch_shapes=[pltpu.VMEM((RW_PAIRS, LANES, LANES), F32), stacked, stacked, stacked,
                        pltpu.VMEM((3, RW_PAIRS, tb, LANES), F32), pltpu.VMEM((RW_PAIRS, tb, LANES), F32)],
    )(rws, rws, rws, rws, rws, *vecs, *mats)


def _rw_bwd(rws, states, do, do_blk, vecs, mats, name):
    T = rws.shape[0]
    tb = _row_tile(T, SCAN_ROWS)
    nsub = tb // RW_CHUNK
    C = RW_CHUNK
    nT, wide, vec, mat, st, ins = _rw_specs(T, tb, True)
    nin = 5 + 1 + 1 + N_RW_VEC + N_RW_MAT

    def body(*refs):
        r_ref, k_ref, v_ref, lw_ref, gd_ref = refs[:5]
        st_ref, do_ref = refs[5], refs[6]
        vrefs = refs[7:7 + N_RW_VEC]
        mrefs = refs[7 + N_RW_VEC:nin]
        dr_ref, dk_ref, dv_ref, dlo_ref = refs[nin:nin + 4]
        dvec = refs[nin + 4:nin + 4 + N_RW_VEC]
        dmat = refs[nin + 4 + N_RW_VEC:nin + 4 + N_RW_VEC + N_RW_MAT]
        ds_ref, wq_ref, uy_ref, bk_ref, pv_ref, dec_ref, y_ref, dpre_ref, dvs_ref = refs[-9:]

        @pl.when(pl.program_id(0) == 0)
        def _():
            ds_ref[...] = jnp.zeros_like(ds_ref)
            for x in dvec + dmat:
                x[...] = jnp.zeros_like(x)

        consts = _rw_consts(tb)
        blockdiag = consts[3]
        dlw, dgd = 0.0, 0.0
        for p in range(RW_PAIRS):
            ln = _head_lanes(p)
            (W, U, Q, Y0, Bg, Kg, dec, k2, g), prep_vjp = jax.vjp(
                functools.partial(_rw_prep, consts),
                *_rw_prep_args(p, r_ref, k_ref, v_ref, lw_ref, gd_ref, vrefs, mrefs))
            _stack_chunks(wq_ref, W, Q)
            _stack_chunks(uy_ref, U, Y0)
            _stack_chunks(bk_ref, Bg, Kg)
            dec_ref[...] = dec

            def redo(c, carry, p=p, ln=ln):
                rows = _chunk_rows(c, C)
                py = _dot(wq_ref[c], st_ref[p, c], NT) + uy_ref[c]
                y_ref[rows, :] = py[C:]
                pv_ref[c, 0:C, :] = py[:C]
                pv_ref[c, C:2 * C, :] = v_ref[rows, ln]
                return carry

            lax.fori_loop(0, nsub, redo, 0)
            r_k, ln_w, ln_b = [x[:, ln] for x in vrefs[4:]]
            _, post_vjp = jax.vjp(functools.partial(_rw_post, blockdiag), y_ref[...], r_ref[:, ln], v_ref[:, ln],
                                  k2, g, r_k, ln_w, ln_b)
            dy, dr2, dv2, dk2, dg, dr_k, dln_w, dln_b = post_vjp(do_ref[:, ln])
            dpre_ref[3] = dy
            dvs_ref[...] = dv2
            for x, gx in zip(dvec[4:], (dr_k, dln_w, dln_b)):
                x[:, ln] += gx
            dpre_ref[6] = jnp.zeros_like(dpre_ref[6])

            def step(i, carry, p=p):
                c = nsub - 1 - i
                rows = _chunk_rows(c, C)
                row0 = pl.ds(c * C, 1)
                G = ds_ref[p] * blockdiag
                S = st_ref[p, c]
                t1 = _dot(bk_ref[c], G, NT)
                dpy = jnp.concatenate([t1[:C], dpre_ref[3, rows, :]], axis=0)
                t2 = _dot(dpy, S)
                t3 = _dot(pv_ref[c], G)
                dvs_ref[rows, :] += t1[C:]
                dpre_ref[0, rows, :] = t2[:C]
                dpre_ref[1, rows, :] = t1[:C]
                dpre_ref[2, rows, :] = t2[C:]
                dpre_ref[4, rows, :] = t3[:C]
                dpre_ref[5, rows, :] = t3[C:]
                dpre_ref[6, row0, :] = jnp.sum(S * G, axis=0, keepdims=True)
                ds_ref[p] = G * dec_ref[row0, :] + _dot(dpy, wq_ref[c], TN)
                return carry

            lax.fori_loop(0, nsub, step, 0)
            grads = prep_vjp(tuple(dpre_ref[i] for i in range(7)) + (dk2, dg))
            dr_ref[:, ln] = grads[0] + dr2
            dk_ref[:, ln] = grads[1]
            dv_ref[:, ln] = grads[2] + dvs_ref[...]
            dlw = dlw + grads[3]
            dgd = dgd + grads[4]
            for x, gx in zip(dvec[:4], grads[5:9]):
                x[:, ln] += gx
            for x, gx in zip(dmat, grads[9:]):
                x[p] += gx
        dlo_ref[:, 0:LANES] = dlw
        dlo_ref[:, LANES:2 * LANES] = dgd

    dcol = jax.ShapeDtypeStruct((T, RW_WIDTH), F32)
    dlo_spec = pl.BlockSpec((tb, 2 * LANES), lambda t: (nT - 1 - t, 0))
    blk = pltpu.VMEM((tb, LANES), F32)
    stacked = pltpu.VMEM((nsub, 2 * C, LANES), F32)
    return pl.pallas_call(
        body, name=name, grid=(nT,),
        in_specs=ins + [st, wide(do_blk)] + [vec] * N_RW_VEC + [mat] * N_RW_MAT,
        out_specs=[wide(0)] * 3 + [dlo_spec] + [vec] * N_RW_VEC + [mat] * N_RW_MAT,
        out_shape=[dcol] * 3 + [jax.ShapeDtypeStruct((T, 2 * LANES), F32)]
        + [jax.ShapeDtypeStruct((1, RW_WIDTH), F32)] * N_RW_VEC
        + [jax.ShapeDtypeStruct((RW_PAIRS, LANES, LANES), F32)] * N_RW_MAT,
        scratch_shapes=[pltpu.VMEM((RW_PAIRS, LANES, LANES), F32), stacked, stacked, stacked, stacked, blk, blk,
                        pltpu.VMEM((7, tb, LANES), F32), blk],
    )(rws, rws, rws, rws, rws, states, do, *vecs, *mats)


def _my_index():
    return 4 * lax.axis_index("x") + 2 * lax.axis_index("y") + lax.axis_index("c")


def _peer(bits):
    pos = []
    for name, flip in zip(("x", "y", "c"), bits):
        i = lax.axis_index(name)
        pos.append(1 - i if flip else i)
    return tuple(pos)


def _peer_index(bits):
    x, y, c = _peer(bits)
    return 4 * x + 2 * y + c


def _all_gather(shards, name):
    n = len(shards)
    chips = [(1, 0, 0), (0, 1, 0), (1, 1, 0)]
    sib = (0, 0, 1)

    def body(*refs):
        ins, outs = refs[:n], refs[n:2 * n]
        send_sems, recv_sems, local_sems = refs[2 * n:]

        def rows(k, dev):
            r = ins[k].shape[0]
            return outs[k].at[pl.ds(dev * r, r), :]

        def copy(k, slot, block_dev, to_bits, src=None):
            return pltpu.make_async_remote_copy(
                src_ref=rows(k, block_dev) if src is None else src, dst_ref=rows(k, block_dev),
                send_sem=send_sems.at[k, slot], recv_sem=recv_sems.at[k, slot],
                device_id=_peer(to_bits), device_id_type=MESH_ID)

        me = _my_index()
        started = []
        for k in range(n):
            mine = pltpu.make_async_copy(ins[k], rows(k, me), local_sems.at[k])
            mine.start()
            started.append(mine)
        sends = []
        for k in range(n):
            first = [copy(k, 0, me, sib, src=ins[k])]
            first += [copy(k, 1 + j, me, chip, src=ins[k]) for j, chip in enumerate(chips)]
            for cp in first:
                cp.start()
            sends += first
        for k in range(n):
            for j, chip in enumerate(chips):
                copy(k, 1 + j, _peer_index(chip), chip).wait_recv()
                fwd = copy(k, 4 + j, _peer_index(chip), sib)
                fwd.start()
                sends.append(fwd)
        for k in range(n):
            copy(k, 0, _peer_index(sib), sib).wait_recv()
            for j, chip in enumerate(chips):
                both = (chip[0], chip[1], 1)
                copy(k, 4 + j, _peer_index(both), sib).wait_recv()
        for cp in sends:
            cp.wait_send()
        for cp in started:
            cp.wait()

    any_spec = pl.BlockSpec(memory_space=pl.ANY)
    return pl.pallas_call(
        body, name=name,
        in_specs=[any_spec] * n, out_specs=[any_spec] * n,
        out_shape=[jax.ShapeDtypeStruct((N_DEV * s.shape[0], s.shape[1]), s.dtype) for s in shards],
        scratch_shapes=[pltpu.SemaphoreType.DMA((n, 7)), pltpu.SemaphoreType.DMA((n, 7)),
                        pltpu.SemaphoreType.DMA((n,))],
    )(*shards)


def _exchange(partials, name):
    n = len(partials)
    flips = [(dx, dy, dc) for dx in (0, 1) for dy in (0, 1) for dc in (0, 1)][1:]

    def body(*refs):
        ins, outs = refs[:n], refs[n:2 * n]
        send_sems, recv_sems, local_sems = refs[2 * n:]
        me = _my_index()
        local = []
        for k in range(n):
            cp = pltpu.make_async_copy(ins[k].at[me], outs[k].at[me], local_sems.at[k])
            cp.start()
            local.append(cp)
        copies = []
        for k in range(n):
            for d, bits in enumerate(flips):
                cp = pltpu.make_async_remote_copy(
                    src_ref=ins[k].at[_peer_index(bits)], dst_ref=outs[k].at[me],
                    send_sem=send_sems.at[k, d], recv_sem=recv_sems.at[k, d],
                    device_id=_peer(bits), device_id_type=MESH_ID)
                cp.start()
                copies.append(cp)
        for cp in copies:
            cp.wait_recv()
        for cp in copies:
            cp.wait_send()
        for cp in local:
            cp.wait()

    any_spec = pl.BlockSpec(memory_space=pl.ANY)
    return pl.pallas_call(
        body, name=name,
        in_specs=[any_spec] * n, out_specs=[any_spec] * n,
        out_shape=[jax.ShapeDtypeStruct(p.shape, p.dtype) for p in partials],
        scratch_shapes=[pltpu.SemaphoreType.DMA((n, 7)), pltpu.SemaphoreType.DMA((n, 7)),
                        pltpu.SemaphoreType.DMA((n,))],
    )(*partials)


HBM_SPEC = pl.BlockSpec(memory_space=pltpu.HBM)
SEM_SPEC = pl.BlockSpec(memory_space=pltpu.SEMAPHORE)
ALL_FLIPS = [(dx, dy, dc) for dx in (0, 1) for dy in (0, 1) for dc in (0, 1)][1:]


def _spread_copies(srcs, lands, send_sems, recv_sems, per_peer_source):
    me = _my_index()
    copies = []
    for k, land in enumerate(lands):
        for d, bits in enumerate(ALL_FLIPS):
            src = srcs[k].at[_peer_index(bits)] if per_peer_source else land.at[me]
            copies.append(pltpu.make_async_remote_copy(
                src_ref=src, dst_ref=land.at[me],
                send_sem=send_sems.at[k * 7 + d], recv_sem=recv_sems.at[k * 7 + d],
                device_id=_peer(bits), device_id_type=MESH_ID))
    return copies


def _spread_start(srcs, lands, name):
    ns, n = len(srcs), len(lands)

    def body(*refs):
        src_refs, land_refs = refs[:ns], refs[ns:ns + n]
        send_sems, recv_sems = refs[ns + n], refs[ns + n + 1]
        token = refs[-1]
        for cp in _spread_copies(src_refs, land_refs, send_sems, recv_sems, ns > 0):
            cp.start()
        token[...] = jnp.zeros_like(token)

    bufs = list(srcs) + list(lands)
    out = pl.pallas_call(
        body, name=name,
        out_shape=(pltpu.SemaphoreType.DMA((7 * n,)), pltpu.SemaphoreType.DMA((7 * n,)),
                   *[pltpu.HBM(b.shape, b.dtype) for b in bufs], jax.ShapeDtypeStruct((8, LANES), F32)),
        in_specs=[HBM_SPEC] * (ns + n),
        out_specs=(SEM_SPEC, SEM_SPEC, *[HBM_SPEC] * (ns + n), pl.BlockSpec(memory_space=pltpu.VMEM)),
        input_output_aliases={i: 2 + i for i in range(ns + n)},
        compiler_params=pltpu.CompilerParams(has_side_effects=pltpu.SideEffectType.DATAFLOW_SIDE_EFFECTING),
    )(*[pltpu.with_memory_space_constraint(b, pltpu.HBM) for b in bufs])
    return out[0], out[1], list(out[2:2 + ns]), list(out[2 + ns:2 + ns + n]), out[-1]


def _spread_wait(send_sems, recv_sems, srcs, lands, after, name):
    ns, n = len(srcs), len(lands)

    def body(*refs):
        src_refs, land_refs = refs[:ns], refs[ns:ns + n]
        send_sems, recv_sems = refs[ns + n], refs[ns + n + 1]
        for cp in _spread_copies(src_refs, land_refs, send_sems, recv_sems, ns > 0):
            cp.wait_send()
            cp.wait_recv()

    bufs = list(srcs) + list(lands)
    out = pl.pallas_call(
        body, name=name,
        out_shape=tuple(pltpu.HBM(b.shape, b.dtype) for b in bufs),
        in_specs=[HBM_SPEC] * (ns + n) + [SEM_SPEC, SEM_SPEC, pl.BlockSpec(memory_space=pl.ANY)],
        out_specs=tuple([HBM_SPEC] * (ns + n)),
        input_output_aliases={i: i for i in range(ns + n)},
        compiler_params=pltpu.CompilerParams(has_side_effects=pltpu.SideEffectType.DATAFLOW_SIDE_EFFECTING),
    )(*bufs, send_sems, recv_sems, after)
    return list(out[ns:])


def _own_slot_only(block, me):
    return lax.dynamic_update_slice(jnp.zeros((N_DEV,) + block.shape, block.dtype), block[None], (me, 0, 0))


def _sum_slots(landed, name):
    _, R, C = landed.shape
    tb = _row_tile(R, 128)

    def body(l_ref, o_ref):
        acc = l_ref[0].astype(F32)
        for s in range(1, N_DEV):
            acc = acc + l_ref[s].astype(F32)
        o_ref[...] = acc

    return pl.pallas_call(
        body, name=name, grid=(R // tb,),
        in_specs=[pl.BlockSpec((N_DEV, tb, C), lambda i: (0, i, 0))],
        out_specs=pl.BlockSpec((tb, C), lambda i: (i, 0)),
        out_shape=jax.ShapeDtypeStruct((R, C), F32),
    )(landed)


def _pack_rows(flat_list, width=LANES):
    flat = jnp.concatenate([a.reshape(-1) for a in flat_list])
    n = flat.shape[0]
    rows = -(-n // width)
    rows = -(-rows // 8) * 8
    return jnp.pad(flat, (0, rows * width - n)).reshape(rows, width)


def _unpack(packed, shapes):
    flat = packed.reshape(-1)
    out, off = [], 0
    for s in shapes:
        n = 1
        for d in s:
            n *= d
        out.append(flat[off:off + n].reshape(s))
        off += n
    return out


def kernel(x, norm1_w, w_in, hg_lb_logits, hg_norm_w, rw_shift_mu, rw_w0, rw_w2, rw_a0, rw_a2, rw_g2, rw_k_k, rw_k_a, rw_r_k, rw_ln_w, rw_ln_b, w_out, norm2_w, w_up, conv_w, conv_b, w_down, final_norm_w, loss_target, m_norm1_w, m_w_in, m_hg_lb_logits, m_hg_norm_w, m_rw_shift_mu, m_rw_w0, m_rw_w2, m_rw_a0, m_rw_a2, m_rw_g2, m_rw_k_k, m_rw_k_a, m_rw_r_k, m_rw_ln_w, m_rw_ln_b, m_w_out, m_norm2_w, m_w_up, m_conv_w, m_conv_b, m_w_down, m_final_norm_w, v_norm1_w, v_w_in, v_hg_lb_logits, v_hg_norm_w, v_rw_shift_mu, v_rw_w0, v_rw_w2, v_rw_a0, v_rw_a2, v_rw_g2, v_rw_k_k, v_rw_k_a, v_rw_r_k, v_rw_ln_w, v_rw_ln_b, v_w_out, v_norm2_w, v_w_up, v_conv_w, v_conv_b, v_w_down, v_final_norm_w):
    weights = dict(norm1_w=norm1_w, w_in=w_in, hg_lb_logits=hg_lb_logits, hg_norm_w=hg_norm_w,
                   rw_shift_mu=rw_shift_mu, rw_w0=rw_w0, rw_w2=rw_w2, rw_a0=rw_a0, rw_a2=rw_a2, rw_g2=rw_g2,
                   rw_k_k=rw_k_k, rw_k_a=rw_k_a, rw_r_k=rw_r_k, rw_ln_w=rw_ln_w, rw_ln_b=rw_ln_b, w_out=w_out,
                   norm2_w=norm2_w, w_up=w_up, conv_w=conv_w, conv_b=conv_b, w_down=w_down,
                   final_norm_w=final_norm_w)
    m_in = dict(norm1_w=m_norm1_w, w_in=m_w_in, hg_lb_logits=m_hg_lb_logits, hg_norm_w=m_hg_norm_w,
                rw_shift_mu=m_rw_shift_mu, rw_w0=m_rw_w0, rw_w2=m_rw_w2, rw_a0=m_rw_a0, rw_a2=m_rw_a2,
                rw_g2=m_rw_g2, rw_k_k=m_rw_k_k, rw_k_a=m_rw_k_a, rw_r_k=m_rw_r_k, rw_ln_w=m_rw_ln_w,
                rw_ln_b=m_rw_ln_b, w_out=m_w_out, norm2_w=m_norm2_w, w_up=m_w_up, conv_w=m_conv_w,
                conv_b=m_conv_b, w_down=m_w_down, final_norm_w=m_final_norm_w)
    v_in = dict(norm1_w=v_norm1_w, w_in=v_w_in, hg_lb_logits=v_hg_lb_logits, hg_norm_w=v_hg_norm_w,
                rw_shift_mu=v_rw_shift_mu, rw_w0=v_rw_w0, rw_w2=v_rw_w2, rw_a0=v_rw_a0, rw_a2=v_rw_a2,
                rw_g2=v_rw_g2, rw_k_k=v_rw_k_k, rw_k_a=v_rw_k_a, rw_r_k=v_rw_r_k, rw_ln_w=v_rw_ln_w,
                rw_ln_b=v_rw_ln_b, w_out=v_w_out, norm2_w=v_norm2_w, w_up=v_w_up, conv_w=v_conv_w,
                conv_b=v_conv_b, w_down=v_w_down, final_norm_w=v_final_norm_w)
    names = list(weights)
    sharded_small = ["rw_w2", "rw_a2", "rw_g2", "conv_w"]
    sharded_big = ["w_in", "w_out", "w_up", "w_down"]
    replicated = [n for n in names if n not in sharded_small + sharded_big]

    xs = x[0]
    tgt = loss_target[0]

    small_shard = _pack_rows([weights[n] for n in sharded_small])
    g_win_t, g_small = _all_gather([w_in[0].T.astype(BF16), small_shard], "gather_weights")
    me = _my_index()
    later = (w_up[0].T.astype(BF16), w_out[0].astype(BF16), w_down[0].astype(BF16))
    later, _ = lax.optimization_barrier((later, g_small))
    later = [_own_slot_only(z, me) for z in later]
    g_send, g_recv, _, later, g_token = _spread_start([], later, "gather_later_start")
    small_shapes = [weights[n].shape for n in sharded_small]
    per_dev = [_unpack(g_small.reshape(N_DEV, -1)[j], small_shapes) for j in range(N_DEV)]
    w2_full, a2_full, g2_full, convw_full = [jnp.concatenate([per_dev[j][i][0] for j in range(N_DEV)], axis=-1)
                                             for i in range(4)]
    zeros64 = jnp.zeros((RW_PAIRS, 64, LANES), F32)
    by_pair = lambda z: z.reshape(z.shape[0], RW_PAIRS, LANES).transpose(1, 0, 2)
    w2p = jnp.concatenate([by_pair(w2_full), zeros64], axis=1)
    a2p = jnp.concatenate([zeros64, by_pair(a2_full)], axis=1)
    g2p = by_pair(g2_full)

    l0, l1 = hg_lb_logits[0:1], hg_lb_logits[1:2]
    h1 = _rms_fwd(xs, norm1_w + g_token[0:1, 0:1], "norm1")
    proj = _mm_nt(h1, g_win_t, "proj_in")
    o_hg, hg_states = _hg_fwd(proj, l0, l1, hg_norm_w, "hgrn2_fwd")
    rws = _shift_fwd(proj, rw_shift_mu, "token_shift")
    rw_vecs = [rw_w0, rw_a0, rw_k_k, rw_k_a, rw_r_k, rw_ln_w, rw_ln_b]
    rw_mats = [w2p, a2p, g2p]
    o_rw, rw_states = _rw_fwd(rws, rw_vecs, rw_mats, "rwkv7_fwd")
    o_mix = jnp.concatenate([o_hg, o_rw], axis=-1).astype(BF16)
    g_wup_t, g_wout, g_wdown = [z.reshape(-1, z.shape[-1])
                                for z in _spread_wait(g_send, g_recv, [], later, o_mix, "gather_later_wait")]
    x1 = _mm_nn(o_mix, g_wout, xs, "proj_out")
    h2 = _rms_fwd(x1, norm2_w, "norm2")
    u = _mm_nt(h2, g_wup_t, "ffn_up")
    act = _ffn_act_fwd(u, convw_full, conv_b, "ffn_act")
    x2 = _mm_nn(act, g_wdown, x1, "ffn_down")
    loss_part, dx2, d_final_w = _loss_head(x2, final_norm_w.reshape(1, -1), tgt, "loss_head")

    d_wdown = _mm_tn(act, dx2, 1408, "ffn_down_dw", BF16)
    dact = _mm_nt(dx2, g_wdown, "ffn_down_dx", BF16)
    du_g, du_v, dcw_g, dcw_v, dcb_g, dcb_v = _ffn_act_bwd(u, dact, convw_full, conv_b, "ffn_act_bwd")
    du = jnp.concatenate([du_g, du_v], axis=-1)
    d_convw = jnp.concatenate([dcw_g, dcw_v], axis=-1)
    d_convb = jnp.concatenate([dcb_g, dcb_v], axis=-1)
    d_wup_t = _mm_tn(du, h2, 1408, "ffn_up_dw", BF16)
    dh2 = _mm_nn(du, g_wup_t, None, "ffn_up_dx")
    dx1, d_norm2 = _rms_bwd(dh2, x1, norm2_w, dx2, "norm2_bwd")
    d_wout = _mm_tn(o_mix, dx1, 512, "proj_out_dw", BF16)
    do = _mm_nt(dx1, g_wout, "proj_out_dx")
    early = [z.reshape(N_DEV, z.shape[0] // N_DEV, z.shape[1]) for z in (d_wup_t, d_wout, d_wdown)]
    early_land = [_own_slot_only(lax.dynamic_index_in_dim(z, me, 0, keepdims=False), me) for z in early]
    e_send, e_recv, early, early_land, e_token = _spread_start(early, early_land, "exchange_early_start")
    hg_norm_w_t = hg_norm_w + e_token[0:1, 0:1]
    dq, df, di, dg, d_l0, d_l1, d_hg_nw = _hg_bwd(proj, hg_states, do, 0, l0, l1, hg_norm_w_t, "hgrn2_bwd")
    rw_out = _rw_bwd(rws, rw_states, do, 1, rw_vecs, rw_mats, "rwkv7_bwd")
    d_rw_vecs = rw_out[4:4 + N_RW_VEC]
    d_w2p, d_a2p, d_g2p = rw_out[4 + N_RW_VEC:]
    dp_parts, dmu_parts = [], []
    for i, z in enumerate(rw_out[:4]):
        dp, dmu = _shift_bwd(z, proj, rw_shift_mu, i * RW_WIDTH, "token_shift_bwd_%d" % i)
        dp_parts.append(dp)
        dmu_parts.append(dmu)
    d_mu = jnp.concatenate(dmu_parts, axis=-1)
    dproj = jnp.concatenate([dq, df, di, dg] + dp_parts, axis=-1)
    d_win_t = _mm_tn(dproj, h1, 768, "proj_in_dw", BF16)
    from_pairs = lambda z: z.transpose(1, 0, 2).reshape(z.shape[1], RW_WIDTH)
    d_w2 = from_pairs(d_w2p[:, :64])
    d_a2 = from_pairs(d_a2p[:, 64:])
    d_g2 = from_pairs(d_g2p)
    col_blocks = lambda z: z.reshape(z.shape[0], N_DEV, -1).transpose(1, 0, 2)
    small_part = jnp.stack([
        _pack_rows([col_blocks(d_w2)[j], col_blocks(d_a2)[j], col_blocks(d_g2)[j], col_blocks(d_convw)[j]])
        for j in range(N_DEV)])
    late = [d_win_t.reshape(N_DEV, d_win_t.shape[0] // N_DEV, d_win_t.shape[1]), small_part]
    late_land = [_own_slot_only(lax.dynamic_index_in_dim(z, me, 0, keepdims=False), me) for z in late]
    l_send, l_recv, late, late_land, l_token = _spread_start(late, late_land, "exchange_late_start")
    dh1 = _mm_nn(dproj, g_win_t, None, "proj_in_dx")
    grad_x, d_norm1 = _rms_bwd(dh1, xs, norm1_w + l_token[0:1, 0:1], dx1, "norm1_bwd")

    rep_grads = dict(norm1_w=d_norm1, hg_lb_logits=jnp.concatenate([d_l0, d_l1], axis=0), hg_norm_w=d_hg_nw,
                     rw_shift_mu=d_mu, rw_w0=d_rw_vecs[0], rw_a0=d_rw_vecs[1], rw_k_k=d_rw_vecs[2],
                     rw_k_a=d_rw_vecs[3], rw_r_k=d_rw_vecs[4], rw_ln_w=d_rw_vecs[5], rw_ln_b=d_rw_vecs[6],
                     norm2_w=d_norm2, conv_b=d_convb, final_norm_w=d_final_w)
    rep_pack = _pack_rows([loss_part] + [rep_grads[n] for n in replicated])
    rep_part = jnp.broadcast_to(rep_pack[None], (N_DEV,) + rep_pack.shape)
    landed_early = _spread_wait(e_send, e_recv, early, early_land, grad_x, "exchange_early_wait")
    landed_late = _spread_wait(l_send, l_recv, late, late_land, grad_x, "exchange_late_wait")
    (landed_rep,) = _exchange([rep_part], "exchange_grads")
    landed = [landed_late[0]] + landed_early + [landed_late[1], landed_rep]
    sums = [_sum_slots(z, "sum_grads_%d" % i) for i, z in enumerate(landed)]
    g_small_sum = _unpack(sums[4], small_shapes)
    rep_sum = _unpack(sums[5], [(1, 1)] + [weights[n].shape for n in replicated])
    loss = rep_sum[0].reshape(())
    grads = dict(zip(replicated, rep_sum[1:]))
    grads.update(dict(zip(sharded_small, g_small_sum)))
    grads["w_in"] = sums[0].T[None]
    grads["w_up"] = sums[1].T[None]
    grads["w_out"] = sums[2][None]
    grads["w_down"] = sums[3][None]

    delta, new_m, new_v = {}, {}, {}
    for n in sharded_big:
        shp = weights[n].shape
        as2d = lambda z: z.reshape(shp[1], shp[2])
        d, nm, nv = _adamw(as2d(weights[n]), as2d(grads[n]), as2d(m_in[n]), as2d(v_in[n]), "adamw_" + n)
        delta[n], new_m[n], new_v[n] = d.reshape(shp), nm.reshape(shp), nv.reshape(shp)
    small_names = replicated + sharded_small
    packs = [_pack_rows([src[n] for n in small_names]) for src in (weights, grads, m_in, v_in)]
    outs = _adamw(*packs, "adamw_small")
    small_shapes_all = [weights[n].shape for n in small_names]
    for dst, packed in zip((delta, new_m, new_v), outs):
        dst.update(dict(zip(small_names, _unpack(packed, small_shapes_all))))

    return (loss, grad_x[None], *[grads[n] for n in names], *[delta[n] for n in names],
            *[new_m[n] for n in names], *[new_v[n] for n in names])
```

```python
import functools

import jax
import jax.numpy as jnp
from jax import lax
from jax.experimental import pallas as pl
from jax.experimental.pallas import tpu as pltpu

F32 = jnp.float32
BF16 = jnp.bfloat16
HIGHEST = lax.Precision.HIGHEST
SCAN_PRECISION = None
MESH_ID = pl.DeviceIdType.MESH

N_DEV = 8
D_MODEL = 1024
HG_WIDTH = 512
HG_HEAD_DIM = 128
HG_HEADS = 4
RW_WIDTH = 512
RW_PAIRS = 4
RW_HEAD_DIM = 64
HG_COLS = 2048
RW_COLS = 1792
D_FF = 2816
NORM_EPS = 1e-6
RW_GN_EPS = 64e-5
L2_EPS = 1e-12
ADAM_LR, ADAM_B1, ADAM_B2, ADAM_EPS, ADAM_WD, ADAM_STEP = 0.001, 0.9, 0.999, 1e-08, 0.01, 10

HG_CHUNK = 16
RW_CHUNK = 64
SCAN_ROWS = 256
LANES = 128

NN = ((1,), (0,))
NT = ((1,), (1,))
TN = ((0,), (0,))


def _dot(a, b, dims=NN, precision=SCAN_PRECISION):
    if precision is None:
        a, b = a.astype(BF16), b.astype(BF16)
    return lax.dot_general(a, b, (dims, ((), ())), precision=precision, preferred_element_type=F32)


def _iota2(shape, dim):
    return lax.broadcasted_iota(jnp.int32, shape, dim)


def _sigmoid(z):
    return 1.0 / (1.0 + jnp.exp(-z))


def _row_tile(n, want):
    t = min(n, want)
    while n % t:
        t //= 2
    return t


def _rms_fwd(x, w, name):
    T, D = x.shape
    tb = _row_tile(T, 512)

    def body(x_ref, w_ref, h_ref):
        xv = x_ref[...]
        r = lax.rsqrt(jnp.mean(xv * xv, axis=-1, keepdims=True) + NORM_EPS)
        h_ref[...] = (xv * r * w_ref[...]).astype(h_ref.dtype)

    return pl.pallas_call(
        body, name=name, grid=(T // tb,),
        in_specs=[pl.BlockSpec((tb, D), lambda i: (i, 0)), pl.BlockSpec((1, D), lambda i: (0, 0))],
        out_specs=pl.BlockSpec((tb, D), lambda i: (i, 0)),
        out_shape=jax.ShapeDtypeStruct((T, D), BF16),
    )(x, w)


def _rms_bwd(dh, x, w, dres, name):
    T, D = x.shape
    tb = _row_tile(T, 256)

    def body(dh_ref, x_ref, w_ref, dres_ref, dx_ref, dw_ref):
        @pl.when(pl.program_id(0) == 0)
        def _():
            dw_ref[...] = jnp.zeros_like(dw_ref)

        xv = x_ref[...]
        r = lax.rsqrt(jnp.mean(xv * xv, axis=-1, keepdims=True) + NORM_EPS)
        xn = xv * r
        dy = dh_ref[...].astype(F32)
        dxn = dy * w_ref[...]
        dx_ref[...] = dres_ref[...] + r * (dxn - xn * jnp.mean(dxn * xn, axis=-1, keepdims=True))
        dw_ref[...] += jnp.sum(dy * xn, axis=0, keepdims=True)

    row = pl.BlockSpec((tb, D), lambda i: (i, 0))
    vec = pl.BlockSpec((1, D), lambda i: (0, 0))
    return pl.pallas_call(
        body, name=name, grid=(T // tb,),
        in_specs=[row, row, vec, row], out_specs=[row, vec],
        out_shape=[jax.ShapeDtypeStruct((T, D), F32), jax.ShapeDtypeStruct((1, D), F32)],
    )(dh, x, w, dres)


def _mm_nt(a, bt, name, out_dtype=F32):
    T, K = a.shape
    N = bt.shape[0]
    tm = _row_tile(T, 256)

    def body(a_ref, b_ref, o_ref):
        o_ref[...] = _dot(a_ref[...].astype(BF16), b_ref[...].astype(BF16), NT, None).astype(o_ref.dtype)

    return pl.pallas_call(
        body, name=name, grid=(T // tm,),
        in_specs=[pl.BlockSpec((tm, K), lambda i: (i, 0)), pl.BlockSpec((N, K), lambda i: (0, 0))],
        out_specs=pl.BlockSpec((tm, N), lambda i: (i, 0)),
        out_shape=jax.ShapeDtypeStruct((T, N), out_dtype),
    )(a, bt)


def _mm_nn(a, b, res, name, out_dtype=F32):
    T, K = a.shape
    N = b.shape[1]
    tm = _row_tile(T, 256)

    def body(a_ref, b_ref, *rest):
        o_ref = rest[-1]
        acc = _dot(a_ref[...].astype(BF16), b_ref[...].astype(BF16), NN, None)
        if res is not None:
            acc = acc + rest[0][...]
        o_ref[...] = acc.astype(o_ref.dtype)

    in_specs = [pl.BlockSpec((tm, K), lambda i: (i, 0)), pl.BlockSpec((K, N), lambda i: (0, 0))]
    args = [a, b]
    if res is not None:
        in_specs.append(pl.BlockSpec((tm, N), lambda i: (i, 0)))
        args.append(res)
    return pl.pallas_call(
        body, name=name, grid=(T // tm,), in_specs=in_specs,
        out_specs=pl.BlockSpec((tm, N), lambda i: (i, 0)),
        out_shape=jax.ShapeDtypeStruct((T, N), out_dtype),
    )(*args)


def _mm_tn(a, b, tmm, name, out_dtype=F32):
    T, M = a.shape
    N = b.shape[1]
    tk = _row_tile(T, 512)
    nk = T // tk

    def body(a_ref, b_ref, o_ref, acc_ref):
        @pl.when(pl.program_id(1) == 0)
        def _():
            acc_ref[...] = jnp.zeros_like(acc_ref)

        acc_ref[...] += _dot(a_ref[...].astype(BF16), b_ref[...].astype(BF16), TN, None)

        @pl.when(pl.program_id(1) == nk - 1)
        def _():
            o_ref[...] = acc_ref[...].astype(o_ref.dtype)

    return pl.pallas_call(
        body, name=name, grid=(M // tmm, nk),
        in_specs=[pl.BlockSpec((tk, tmm), lambda m, k: (k, m)), pl.BlockSpec((tk, N), lambda m, k: (k, 0))],
        out_specs=pl.BlockSpec((tmm, N), lambda m, k: (m, 0)),
        out_shape=jax.ShapeDtypeStruct((M, N), out_dtype),
        scratch_shapes=[pltpu.VMEM((tmm, N), F32)],
    )(a, b)


def _shift_rows_down(z, n):
    rows = _iota2(z.shape, 0)
    return jnp.where(rows < n, 0.0, pltpu.roll(z, n, 0))


def _shift_rows_up(z, n):
    T = z.shape[0]
    rows = _iota2(z.shape, 0)
    return jnp.where(rows >= T - n, 0.0, pltpu.roll(z, T - n, 0))


def _shift_fwd(proj, mu, name):
    T = proj.shape[0]
    nblk = RW_COLS // LANES
    first = HG_COLS // LANES

    def body(p_ref, mu_ref, o_ref):
        p = p_ref[...]
        o_ref[...] = p + (_shift_rows_down(p, 1) - p) * mu_ref[...]

    return pl.pallas_call(
        body, name=name, grid=(nblk,),
        in_specs=[pl.BlockSpec((T, LANES), lambda j: (0, first + j)), pl.BlockSpec((1, LANES), lambda j: (0, j))],
        out_specs=pl.BlockSpec((T, LANES), lambda j: (0, j)),
        out_shape=jax.ShapeDtypeStruct((T, RW_COLS), F32),
    )(proj, mu)


def _shift_bwd(ds, proj, mu, col0, name):
    T, width = ds.shape
    nblk = width // LANES
    first = (HG_COLS + col0) // LANES
    mu0 = col0 // LANES

    def body(ds_ref, p_ref, mu_ref, dp_ref, dmu_ref):
        dsv = ds_ref[...]
        p = p_ref[...]
        m = mu_ref[...]
        dp_ref[...] = (dsv * (1.0 - m) + _shift_rows_up(dsv * m, 1)).astype(dp_ref.dtype)
        dmu_ref[...] = jnp.sum(dsv * (_shift_rows_down(p, 1) - p), axis=0, keepdims=True)

    return pl.pallas_call(
        body, name=name, grid=(nblk,),
        in_specs=[pl.BlockSpec((T, LANES), lambda j: (0, j)),
                  pl.BlockSpec((T, LANES), lambda j: (0, first + j)),
                  pl.BlockSpec((1, LANES), lambda j: (0, mu0 + j))],
        out_specs=[pl.BlockSpec((T, LANES), lambda j: (0, j)), pl.BlockSpec((1, LANES), lambda j: (0, j))],
        out_shape=[jax.ShapeDtypeStruct((T, width), BF16), jax.ShapeDtypeStruct((1, width), F32)],
    )(ds, proj, mu)


def _conv3(z, w_ref):
    return w_ref[0:1, :] * _shift_rows_down(z, 2) + w_ref[1:2, :] * _shift_rows_down(z, 1) + w_ref[2:3, :] * z


def _ffn_act_fwd(u, conv_w, conv_b, name):
    T = u.shape[0]
    nblk = D_FF // LANES

    def body(ug_ref, uv_ref, wg_ref, wv_ref, bg_ref, bv_ref, act_ref):
        gate = _conv3(ug_ref[...], wg_ref) + bg_ref[...]
        val = _conv3(uv_ref[...], wv_ref) + bv_ref[...]
        act_ref[...] = (gate * _sigmoid(gate) * val).astype(act_ref.dtype)

    col = lambda off: pl.BlockSpec((T, LANES), lambda j: (0, off + j))
    wsp = lambda off: pl.BlockSpec((3, LANES), lambda j: (0, off + j))
    bsp = lambda off: pl.BlockSpec((1, LANES), lambda j: (0, off + j))
    return pl.pallas_call(
        body, name=name, grid=(nblk,),
        in_specs=[col(0), col(nblk), wsp(0), wsp(nblk), bsp(0), bsp(nblk)],
        out_specs=pl.BlockSpec((T, LANES), lambda j: (0, j)),
        out_shape=jax.ShapeDtypeStruct((T, D_FF), BF16),
    )(u, u, conv_w, conv_w, conv_b, conv_b)


def _ffn_act_bwd(u, dact, conv_w, conv_b, name):
    T = u.shape[0]
    nblk = D_FF // LANES

    def conv_bwd(z, dzc, w_ref, du_ref, dw_ref, db_ref):
        du = w_ref[2:3, :] * dzc + w_ref[1:2, :] * _shift_rows_up(dzc, 1) + w_ref[0:1, :] * _shift_rows_up(dzc, 2)
        du_ref[...] = du.astype(du_ref.dtype)
        dw_ref[0:1, :] = jnp.sum(dzc * _shift_rows_down(z, 2), axis=0, keepdims=True)
        dw_ref[1:2, :] = jnp.sum(dzc * _shift_rows_down(z, 1), axis=0, keepdims=True)
        dw_ref[2:3, :] = jnp.sum(dzc * z, axis=0, keepdims=True)
        db_ref[...] = jnp.sum(dzc, axis=0, keepdims=True)

    def body(ug_ref, uv_ref, da_ref, wg_ref, wv_ref, bg_ref, bv_ref,
             dug_ref, duv_ref, dwg_ref, dwv_ref, dbg_ref, dbv_ref):
        ug, uv = ug_ref[...], uv_ref[...]
        gate = _conv3(ug, wg_ref) + bg_ref[...]
        val = _conv3(uv, wv_ref) + bv_ref[...]
        da = da_ref[...].astype(F32)
        sg = _sigmoid(gate)
        dgate = da * val * (sg * (1.0 + gate * (1.0 - sg)))
        dval = da * gate * sg
        conv_bwd(ug, dgate, wg_ref, dug_ref, dwg_ref, dbg_ref)
        conv_bwd(uv, dval, wv_ref, duv_ref, dwv_ref, dbv_ref)

    col = lambda off: pl.BlockSpec((T, LANES), lambda j: (0, off + j))
    wsp = lambda off: pl.BlockSpec((3, LANES), lambda j: (0, off + j))
    bsp = lambda off: pl.BlockSpec((1, LANES), lambda j: (0, off + j))
    half = lambda r, dt: jax.ShapeDtypeStruct((r, D_FF), dt)
    return pl.pallas_call(
        body, name=name, grid=(nblk,),
        in_specs=[col(0), col(nblk), col(0), wsp(0), wsp(nblk), bsp(0), bsp(nblk)],
        out_specs=[col(0), col(0), wsp(0), wsp(0), bsp(0), bsp(0)],
        out_shape=[half(T, BF16), half(T, BF16), half(3, F32), half(3, F32), half(1, F32), half(1, F32)],
    )(u, u, dact, conv_w, conv_w, conv_b, conv_b)


def _loss_head(x2, w, target, name):
    T, D = x2.shape
    tb = _row_tile(T, 256)

    def body(x_ref, w_ref, t_ref, loss_ref, dx_ref, dw_ref):
        @pl.when(pl.program_id(0) == 0)
        def _():
            loss_ref[...] = jnp.zeros_like(loss_ref)
            dw_ref[...] = jnp.zeros_like(dw_ref)

        xv = x_ref[...]
        r = lax.rsqrt(jnp.mean(xv * xv, axis=-1, keepdims=True) + NORM_EPS)
        xn = xv * r
        err = xn * w_ref[...] - t_ref[...]
        row_loss = jnp.sum(err * err, axis=-1, keepdims=True) * (0.5 / D)
        loss_ref[...] += jnp.sum(row_loss, axis=0, keepdims=True)
        dy = err * (1.0 / D)
        dxn = dy * w_ref[...]
        dx_ref[...] = r * (dxn - xn * jnp.mean(dxn * xn, axis=-1, keepdims=True))
        dw_ref[...] += jnp.sum(dy * xn, axis=0, keepdims=True)

    row = pl.BlockSpec((tb, D), lambda i: (i, 0))
    vec = pl.BlockSpec((1, D), lambda i: (0, 0))
    return pl.pallas_call(
        body, name=name, grid=(T // tb,),
        in_specs=[row, vec, row],
        out_specs=[pl.BlockSpec((1, 1), lambda i: (0, 0)), row, vec],
        out_shape=[jax.ShapeDtypeStruct((1, 1), F32), jax.ShapeDtypeStruct((T, D), F32),
                   jax.ShapeDtypeStruct((1, D), F32)],
    )(x2, w, target)


def _adamw(w, g, m, v, name):
    R, C = w.shape
    tb = _row_tile(R, 256) if R % 8 == 0 else R

    def body(w_ref, g_ref, m_ref, v_ref, d_ref, nm_ref, nv_ref):
        gv = g_ref[...]
        nm = ADAM_B1 * m_ref[...] + (1.0 - ADAM_B1) * gv
        nv = ADAM_B2 * v_ref[...] + (1.0 - ADAM_B2) * (gv * gv)
        m_hat = nm / (1.0 - ADAM_B1 ** ADAM_STEP)
        v_hat = nv / (1.0 - ADAM_B2 ** ADAM_STEP)
        d_ref[...] = -ADAM_LR * (m_hat / (jnp.sqrt(v_hat) + ADAM_EPS) + ADAM_WD * w_ref[...])
        nm_ref[...] = nm
        nv_ref[...] = nv

    blk = pl.BlockSpec((tb, C), lambda i: (i, 0))
    sd = jax.ShapeDtypeStruct((R, C), F32)
    return pl.pallas_call(
        body, name=name, grid=(R // tb,), in_specs=[blk] * 4, out_specs=[blk] * 3, out_shape=[sd] * 3,
    )(w, g, m, v)


def _chunk_masks(rows, chunk):
    shift = chunk.bit_length() - 1
    i, j = _iota2((rows, rows), 0), _iota2((rows, rows), 1)
    same = jnp.right_shift(i, shift) == jnp.right_shift(j, shift)
    return same.astype(F32), (same & (j <= i)).astype(F32), (same & (j < i)).astype(F32)


def _head_lanes(h):
    return slice(h * LANES, (h + 1) * LANES)


def _chunk_rows(c, chunk):
    return pl.ds(pl.multiple_of(c * chunk, chunk), chunk)


def _hg_consts(rows):
    same, tril, _ = _chunk_masks(rows, HG_CHUNK)
    shift = HG_CHUNK.bit_length() - 1
    i, j = _iota2((rows, rows), 0), _iota2((rows, rows), 1)
    mid_row = jnp.left_shift(jnp.right_shift(i, shift), shift) + (HG_CHUNK // 2 - 1)
    upto_mid = (same > 0) & (j <= mid_row)
    return jnp.concatenate([tril, same, upto_mid.astype(F32)], axis=0), tril


N_HG_IN = 5


def _hg_prep(consts, *flat):
    sums, tril = consts
    rows = tril.shape[0]
    heads, logs = [], []
    for h in range(len(flat) // N_HG_IN):
        qr, fr, ir, l0, l1 = flat[N_HG_IN * h:N_HG_IN * (h + 1)]
        lb = _sigmoid(l0 - l1)
        f = lb + (1.0 - lb) * _sigmoid(fr)
        heads.append((qr * _sigmoid(qr) * (HG_HEAD_DIM ** -0.5), 1.0 - f, ir))
        logs.append(jnp.log(f))
    acc = _dot(sums, jnp.concatenate(logs, axis=1), NN, HIGHEST)
    sums_of = []
    for h in range(len(heads)):
        acc_h = acc[:, h * LANES:(h + 1) * LANES]
        sums_of.append((acc_h[:rows], acc_h[rows:2 * rows], acc_h[2 * rows:]))
    atts = [_dot(q * jnp.exp(a - mid), k * jnp.exp(mid - a), NT) * tril
            for (q, k, _), (a, _, mid) in zip(heads, sums_of)]
    intra = [_dot(att, ir) for att, (_, _, ir) in zip(atts, heads)]
    return tuple((q * jnp.exp(a), o_intra, k * jnp.exp(tot - a), jnp.exp(tot))
                 for (q, k, _), (a, tot, _), o_intra in zip(heads, sums_of, intra))


def _hg_prep_args(q_ref, f_ref, i_ref, l0_ref, l1_ref):
    flat = []
    for h in range(HG_HEADS):
        ln = _head_lanes(h)
        flat += [q_ref[:, ln], f_ref[:, ln], i_ref[:, ln], l0_ref[:, ln], l1_ref[:, ln]]
    return flat


def _hg_post(o, gr, nw):
    on = o * lax.rsqrt(jnp.mean(o * o, axis=-1, keepdims=True) + NORM_EPS)
    return on * nw * (gr * _sigmoid(gr))


def _hg_specs(T, tb, rev):
    nT = T // tb
    tix = (lambda t: nT - 1 - t) if rev else (lambda t: t)
    col = lambda blk: pl.BlockSpec((tb, HG_WIDTH), lambda t: (tix(t), blk))
    vec = pl.BlockSpec((1, HG_WIDTH), lambda t: (0, 0))
    st = pl.BlockSpec((HG_HEADS, tb // HG_CHUNK, HG_HEAD_DIM, HG_HEAD_DIM), lambda t: (0, tix(t), 0, 0))
    return nT, col, vec, st


def _hg_fwd(proj, l0, l1, nw, name):
    T = proj.shape[0]
    tb = _row_tile(T, SCAN_ROWS)
    nsub = tb // HG_CHUNK
    nT, col, vec, st = _hg_specs(T, tb, False)

    def body(q_ref, f_ref, i_ref, g_ref, l0_ref, l1_ref, nw_ref, o_ref, st_ref, s_ref, qe_ref, kd_ref, dec_ref):
        @pl.when(pl.program_id(0) == 0)
        def _():
            s_ref[...] = jnp.zeros_like(s_ref)

        consts = _hg_consts(tb)
        outs = _hg_prep(consts, *_hg_prep_args(q_ref, f_ref, i_ref, l0_ref, l1_ref))
        for h, (qe, o_intra, kd, dec) in enumerate(outs):
            qe_ref[h], kd_ref[h], dec_ref[h] = qe, kd, dec
            o_ref[:, _head_lanes(h)] = o_intra

        def step(c, carry):
            rows = _chunk_rows(c, HG_CHUNK)
            for h in range(HG_HEADS):
                ln = _head_lanes(h)
                S = s_ref[h]
                st_ref[h, c] = S
                o_ref[rows, ln] += _dot(qe_ref[h, rows, :], S, NT)
                s_ref[h] = S * dec_ref[h, pl.ds(c * HG_CHUNK, 1), :] + _dot(i_ref[rows, ln], kd_ref[h, rows, :], TN)
            return carry

        lax.fori_loop(0, nsub, step, 0)
        for h in range(HG_HEADS):
            ln = _head_lanes(h)
            o_ref[:, ln] = _hg_post(o_ref[:, ln], g_ref[:, ln], nw_ref[:, ln])

    blk = pltpu.VMEM((HG_HEADS, tb, LANES), F32)
    return pl.pallas_call(
        body, name=name, grid=(nT,),
        in_specs=[col(0), col(1), col(2), col(3), vec, vec, vec],
        out_specs=[col(0), st],
        out_shape=[jax.ShapeDtypeStruct((T, HG_WIDTH), F32),
                   jax.ShapeDtypeStruct((HG_HEADS, T // HG_CHUNK, HG_HEAD_DIM, HG_HEAD_DIM), F32)],
        scratch_shapes=[pltpu.VMEM((HG_HEADS, HG_HEAD_DIM, HG_HEAD_DIM), F32), blk, blk, blk],
    )(proj, proj, proj, proj, l0, l1, nw)


def _hg_bwd(proj, states, do, do_blk, l0, l1, nw, name):
    T = proj.shape[0]
    tb = _row_tile(T, SCAN_ROWS)
    nsub = tb // HG_CHUNK
    nT, col, vec, st = _hg_specs(T, tb, True)

    def body(q_ref, f_ref, i_ref, g_ref, st_ref, do_ref, l0_ref, l1_ref, nw_ref,
             dq_ref, df_ref, di_ref, dg_ref, dl0_ref, dl1_ref, dnw_ref,
             ds_ref, qe_ref, kd_ref, dec_ref, o_ref, dqe_ref, dkd_ref, ddec_ref, dis_ref):
        @pl.when(pl.program_id(0) == 0)
        def _():
            ds_ref[...] = jnp.zeros_like(ds_ref)
            dl0_ref[...] = jnp.zeros_like(dl0_ref)
            dl1_ref[...] = jnp.zeros_like(dl1_ref)
            dnw_ref[...] = jnp.zeros_like(dnw_ref)

        consts = _hg_consts(tb)
        outs, prep_vjp = jax.vjp(functools.partial(_hg_prep, consts),
                                 *_hg_prep_args(q_ref, f_ref, i_ref, l0_ref, l1_ref))
        for h, (qe, o_intra, kd, dec) in enumerate(outs):
            qe_ref[h], kd_ref[h], dec_ref[h], o_ref[h] = qe, kd, dec, o_intra

        def redo(c, carry):
            rows = _chunk_rows(c, HG_CHUNK)
            for h in range(HG_HEADS):
                o_ref[h, rows, :] += _dot(qe_ref[h, rows, :], st_ref[h, c], NT)
            return carry

        lax.fori_loop(0, nsub, redo, 0)
        for h in range(HG_HEADS):
            ln = _head_lanes(h)
            _, vjp = jax.vjp(_hg_post, o_ref[h], g_ref[:, ln], nw_ref[:, ln])
            d_o, dgr, dnw = vjp(do_ref[:, ln])
            o_ref[h] = d_o
            dg_ref[:, ln] = dgr.astype(dg_ref.dtype)
            dnw_ref[:, ln] += dnw
        ddec_ref[...] = jnp.zeros_like(ddec_ref)

        def step(i, carry):
            c = nsub - 1 - i
            rows = _chunk_rows(c, HG_CHUNK)
            row0 = pl.ds(c * HG_CHUNK, 1)
            for h in range(HG_HEADS):
                ln = _head_lanes(h)
                G = ds_ref[h]
                S = st_ref[h, c]
                d_o = o_ref[h, rows, :]
                dqe_ref[h, rows, :] = _dot(d_o, S)
                dkd_ref[h, rows, :] = _dot(i_ref[rows, ln], G)
                dis_ref[h, rows, :] = _dot(kd_ref[h, rows, :], G, NT)
                ddec_ref[h, row0, :] = jnp.sum(S * G, axis=0, keepdims=True)
                ds_ref[h] = G * dec_ref[h, row0, :] + _dot(d_o, qe_ref[h, rows, :], TN)
            return carry

        lax.fori_loop(0, nsub, step, 0)
        grads = prep_vjp(tuple((dqe_ref[h], o_ref[h], dkd_ref[h], ddec_ref[h]) for h in range(HG_HEADS)))
        for h in range(HG_HEADS):
            ln = _head_lanes(h)
            dq, df, di, dl0, dl1 = grads[N_HG_IN * h:N_HG_IN * (h + 1)]
            dq_ref[:, ln] = dq.astype(dq_ref.dtype)
            df_ref[:, ln] = df.astype(df_ref.dtype)
            di_ref[:, ln] = (di + dis_ref[h]).astype(di_ref.dtype)
            dl0_ref[:, ln] += dl0
            dl1_ref[:, ln] += dl1

    dcol = jax.ShapeDtypeStruct((T, HG_WIDTH), BF16)
    dvec = jax.ShapeDtypeStruct((1, HG_WIDTH), F32)
    blk = pltpu.VMEM((HG_HEADS, tb, LANES), F32)
    return pl.pallas_call(
        body, name=name, grid=(nT,),
        in_specs=[col(0), col(1), col(2), col(3), st, col(do_blk), vec, vec, vec],
        out_specs=[col(0)] * 4 + [vec] * 3,
        out_shape=[dcol] * 4 + [dvec] * 3,
        scratch_shapes=[pltpu.VMEM((HG_HEADS, HG_HEAD_DIM, HG_HEAD_DIM), F32)] + [blk] * 8,
    )(proj, proj, proj, proj, states, do, l0, l1, nw)


def _rw_consts(rows):
    same, tril, stril = _chunk_masks(rows, RW_CHUNK)
    br, bc = _iota2((LANES, LANES), 0), _iota2((LANES, LANES), 1)
    blockdiag = ((br < RW_HEAD_DIM) == (bc < RW_HEAD_DIM)).astype(F32)
    m0 = (_iota2((1, LANES), 1) < RW_HEAD_DIM).astype(F32)
    return same, tril, stril, blockdiag, m0, 1.0 - m0


def _unit_lower_inverses_impl(lows):
    rows = lows[0].shape[0]
    eye = (_iota2(lows[0].shape, 0) == _iota2(lows[0].shape, 1)).astype(F32)
    xs = [low + eye for low in lows]
    ps = [_dot(low, low) for low in lows]
    n = 4
    while n < RW_CHUNK:
        zs = [_dot(jnp.concatenate([p, x], axis=0), p) for p, x in zip(ps, xs)]
        ps = [z[:rows] for z in zs]
        xs = [x + z[rows:] for x, z in zip(xs, zs)]
        n *= 2
    return tuple(x + _dot(x, p) for x, p in zip(xs, ps))


@jax.custom_vjp
def _unit_lower_inverses(lows):
    return _unit_lower_inverses_impl(lows)


def _unit_lower_inverses_fwd(lows):
    xs = _unit_lower_inverses_impl(lows)
    return xs, xs


def _unit_lower_inverses_bwd(xs, dxs):
    ts = [_dot(x, dx, TN) for x, dx in zip(xs, dxs)]
    return (tuple(_dot(t, x, NT) for t, x in zip(ts, xs)),)


_unit_lower_inverses.defvjp(_unit_lower_inverses_fwd, _unit_lower_inverses_bwd)


N_PREP_IN = 12
N_PREP_OUT = 9
RW_GROUP = 2


def _rw_prep(consts, *flat):
    same, tril, stril, blockdiag, m0, m1 = consts
    rows = tril.shape[0]
    masks = (m0, m1)
    pre = []
    for i in range(len(flat) // N_PREP_IN):
        r, kx, v, lw, gd, w0, a0, k_k, k_a, w2p, a2p, g2 = flat[N_PREP_IN * i:N_PREP_IN * (i + 1)]
        xw = w0 + _dot(jnp.tanh(lw), w2p)
        w = jnp.minimum(xw, 0.0) - jnp.log(1.0 + jnp.exp(-jnp.abs(xw))) - 0.5
        ld = -jnp.exp(w)
        a_s = _sigmoid(a0 + _dot(lw, a2p))
        g = _dot(_sigmoid(gd), g2)
        kk = kx * k_k
        kk = kk / jnp.maximum(jnp.sqrt(_dot(kk * kk, blockdiag)), L2_EPS)
        k2 = kx * (1.0 + (a_s - 1.0) * k_a)
        bv = kk * a_s
        acc = _dot(jnp.concatenate([tril, same], axis=0), ld, NN, HIGHEST)
        cum, tot = acc[:rows], acc[rows:]
        ecn = jnp.exp(-cum)
        a_t = -kk * jnp.exp(cum - ld)
        r_t = r * jnp.exp(cum)
        rem = jnp.exp(tot - cum)
        z = _dot(jnp.concatenate([a_t * m0, a_t * m1, r_t * m0, r_t * m1], axis=0),
                 jnp.concatenate([bv * ecn, k2 * ecn], axis=0), NT)
        pre.append((v, a_t, r_t, z, (bv * rem, k2 * rem, jnp.exp(tot), k2, g)))
    heads = [(i, h) for i in range(len(pre)) for h in range(2)]
    za = {ih: pre[ih[0]][3][ih[1] * rows:(ih[1] + 1) * rows] for ih in heads}
    zr = {ih: pre[ih[0]][3][(2 + ih[1]) * rows:(3 + ih[1]) * rows] for ih in heads}
    tinv = dict(zip(heads, _unit_lower_inverses(tuple(za[ih][:, :rows] * stril for ih in heads))))
    lv = {ih: _dot(jnp.concatenate([za[ih][:, rows:] * stril, zr[ih][:, rows:] * tril], axis=0), pre[ih[0]][0])
          for ih in heads}
    wu = {ih: _dot(tinv[ih], jnp.concatenate([pre[ih[0]][1] * masks[ih[1]], lv[ih][:rows]], axis=1)) for ih in heads}
    w_m = {ih: wu[ih][:, :LANES] for ih in heads}
    u_m = {ih: masks[ih[1]] * wu[ih][:, LANES:] for ih in heads}
    qy = {ih: _dot(zr[ih][:, :rows] * tril, jnp.concatenate([w_m[ih], u_m[ih]], axis=1)) for ih in heads}
    outs = []
    for i in range(len(pre)):
        a, b = (i, 0), (i, 1)
        W = w_m[a] + w_m[b]
        U = u_m[a] + u_m[b]
        Q = pre[i][2] + qy[a][:, :LANES] + qy[b][:, :LANES]
        Y0 = qy[a][:, LANES:] + qy[b][:, LANES:] + m0 * lv[a][rows:] + m1 * lv[b][rows:]
        outs.append((W, U, Q, Y0) + pre[i][4])
    return tuple(outs)


def _rw_post(blockdiag, y, r, v, k2, g, r_k, ln_w, ln_b):
    inv_n = 1.0 / RW_HEAD_DIM
    yc = y - _dot(y, blockdiag) * inv_n
    var = _dot(yc * yc, blockdiag) * inv_n
    yn = yc * lax.rsqrt(var + RW_GN_EPS) * ln_w + ln_b
    bonus = _dot(r * k2 * r_k, blockdiag) * v
    return (yn + bonus) * g


N_RW_VEC = 7
N_RW_MAT = 3


def _rw_specs(T, tb, rev):
    nT = T // tb
    tix = (lambda t: nT - 1 - t) if rev else (lambda t: t)
    wide = lambda blk: pl.BlockSpec((tb, RW_WIDTH), lambda t: (tix(t), blk))
    narrow = lambda blk: pl.BlockSpec((tb, LANES), lambda t: (tix(t), blk))
    vec = pl.BlockSpec((1, RW_WIDTH), lambda t: (0, 0))
    mat = pl.BlockSpec((RW_PAIRS, LANES, LANES), lambda t: (0, 0, 0))
    st = pl.BlockSpec((RW_PAIRS, tb // RW_CHUNK, LANES, LANES), lambda t: (0, tix(t), 0, 0))
    lora0 = 3 * RW_WIDTH // LANES
    ins = [wide(0), wide(1), wide(2), narrow(lora0), narrow(lora0 + 1)]
    return nT, wide, vec, mat, st, ins


def _rw_prep_args(p, r_ref, k_ref, v_ref, lw_ref, gd_ref, vrefs, mrefs):
    ln = _head_lanes(p)
    w0, a0, k_k, k_a = [x[:, ln] for x in vrefs[:4]]
    return (r_ref[:, ln], k_ref[:, ln], v_ref[:, ln], lw_ref[...], gd_ref[...], w0, a0, k_k, k_a,
            *[x[p] for x in mrefs])


def _stack_chunks(ref, top, bottom):
    C = RW_CHUNK
    for c in range(ref.shape[0]):
        ref[c, 0:C, :] = top[c * C:(c + 1) * C]
        ref[c, C:2 * C, :] = bottom[c * C:(c + 1) * C]


def _group_args(p0, r_ref, k_ref, v_ref, lw_ref, gd_ref, vrefs, mrefs):
    flat = []
    for p in range(p0, p0 + RW_GROUP):
        flat += list(_rw_prep_args(p, r_ref, k_ref, v_ref, lw_ref, gd_ref, vrefs, mrefs))
    return flat


def _rw_fwd(rws, vecs, mats, name):
    T = rws.shape[0]
    tb = _row_tile(T, SCAN_ROWS)
    nsub = tb // RW_CHUNK
    C = RW_CHUNK
    nT, wide, vec, mat, st, ins = _rw_specs(T, tb, False)

    def body(*refs):
        r_ref, k_ref, v_ref, lw_ref, gd_ref = refs[:5]
        vrefs = refs[5:5 + N_RW_VEC]
        mrefs = refs[5 + N_RW_VEC:5 + N_RW_VEC + N_RW_MAT]
        o_ref, st_ref, s_ref, wq_ref, uy_ref, bk_ref, misc_ref, y_ref = refs[-8:]

        @pl.when(pl.program_id(0) == 0)
        def _():
            s_ref[...] = jnp.zeros_like(s_ref)

        consts = _rw_consts(tb)
        blockdiag = consts[3]
        for p0 in range(0, RW_PAIRS, RW_GROUP):
            outs = _rw_prep(consts, *_group_args(p0, r_ref, k_ref, v_ref, lw_ref, gd_ref, vrefs, mrefs))
            for p, (W, U, Q, Y0, Bg, Kg, dec, k2, g) in zip(range(p0, p0 + RW_GROUP), outs):
                _stack_chunks(wq_ref.at[p], W, Q)
                _stack_chunks(uy_ref.at[p], U, Y0)
                _stack_chunks(bk_ref.at[p], Bg, Kg)
                misc_ref[0, p], misc_ref[1, p], misc_ref[2, p] = dec, k2, g

        def step(c, carry):
            rows = _chunk_rows(c, C)
            for p in range(RW_PAIRS):
                S = s_ref[p]
                st_ref[p, c] = S
                py = _dot(wq_ref[p, c], S, NT) + uy_ref[p, c]
                y_ref[p, rows, :] = py[C:]
                pv = jnp.concatenate([py[:C], v_ref[rows, _head_lanes(p)]], axis=0)
                s_ref[p] = (S * misc_ref[0, p, pl.ds(c * C, 1), :] + _dot(pv, bk_ref[p, c], TN)) * blockdiag
            return carry

        lax.fori_loop(0, nsub, step, 0)
        for p in range(RW_PAIRS):
            ln = _head_lanes(p)
            r_k, ln_w, ln_b = [x[:, ln] for x in vrefs[4:]]
            o_ref[:, ln] = _rw_post(blockdiag, y_ref[p], r_ref[:, ln], v_ref[:, ln], misc_ref[1, p], misc_ref[2, p],
                                    r_k, ln_w, ln_b)

    stacked = pltpu.VMEM((RW_PAIRS, nsub, 2 * C, LANES), F32)
    return pl.pallas_call(
        body, name=name, grid=(nT,),
        in_specs=ins + [vec] * N_RW_VEC + [mat] * N_RW_MAT,
        out_specs=[wide(0), st],
        out_shape=[jax.ShapeDtypeStruct((T, RW_WIDTH), F32),
                   jax.ShapeDtypeStruct((RW_PAIRS, T // RW_CHUNK, LANES, LANES), F32)],
        scratch_shapes=[pltpu.VMEM((RW_PAIRS, LANES, LANES), F32), stacked, stacked, stacked,
                        pltpu.VMEM((3, RW_PAIRS, tb, LANES), F32), pltpu.VMEM((RW_PAIRS, tb, LANES), F32)],
    )(rws, rws, rws, rws, rws, *vecs, *mats)


def _rw_bwd(rws, states, do, do_blk, vecs, mats, name):
    T = rws.shape[0]
    tb = _row_tile(T, SCAN_ROWS)
    nsub = tb // RW_CHUNK
    C = RW_CHUNK
    G = RW_GROUP
    nT, wide, vec, mat, st, ins = _rw_specs(T, tb, True)
    nin = 5 + 1 + 1 + N_RW_VEC + N_RW_MAT

    def body(*refs):
        r_ref, k_ref, v_ref, lw_ref, gd_ref = refs[:5]
        st_ref, do_ref = refs[5], refs[6]
        vrefs = refs[7:7 + N_RW_VEC]
        mrefs = refs[7 + N_RW_VEC:nin]
        dr_ref, dk_ref, dv_ref, dlo_ref = refs[nin:nin + 4]
        dvec = refs[nin + 4:nin + 4 + N_RW_VEC]
        dmat = refs[nin + 4 + N_RW_VEC:nin + 4 + N_RW_VEC + N_RW_MAT]
        ds_ref, wq_ref, uy_ref, bk_ref, pv_ref, dec_ref, y_ref, dpre_ref, dvs_ref = refs[-9:]

        @pl.when(pl.program_id(0) == 0)
        def _():
            ds_ref[...] = jnp.zeros_like(ds_ref)
            for x in dvec + dmat:
                x[...] = jnp.zeros_like(x)

        consts = _rw_consts(tb)
        blockdiag = consts[3]
        dlw, dgd = 0.0, 0.0
        for p0 in range(0, RW_PAIRS, G):
            outs, prep_vjp = jax.vjp(functools.partial(_rw_prep, consts),
                                     *_group_args(p0, r_ref, k_ref, v_ref, lw_ref, gd_ref, vrefs, mrefs))
            for q, (W, U, Q, Y0, Bg, Kg, dec, _, _) in enumerate(outs):
                _stack_chunks(wq_ref.at[q], W, Q)
                _stack_chunks(uy_ref.at[q], U, Y0)
                _stack_chunks(bk_ref.at[q], Bg, Kg)
                dec_ref[q] = dec

            def redo(c, carry, p0=p0):
                rows = _chunk_rows(c, C)
                for q in range(G):
                    py = _dot(wq_ref[q, c], st_ref[p0 + q, c], NT) + uy_ref[q, c]
                    y_ref[q, rows, :] = py[C:]
                    pv_ref[q, c, 0:C, :] = py[:C]
                    pv_ref[q, c, C:2 * C, :] = v_ref[rows, _head_lanes(p0 + q)]
                return carry

            lax.fori_loop(0, nsub, redo, 0)
            post = []
            for q in range(G):
                ln = _head_lanes(p0 + q)
                r_k, ln_w, ln_b = [x[:, ln] for x in vrefs[4:]]
                _, post_vjp = jax.vjp(functools.partial(_rw_post, blockdiag), y_ref[q], r_ref[:, ln], v_ref[:, ln],
                                      outs[q][7], outs[q][8], r_k, ln_w, ln_b)
                dy, dr2, dv2, dk2, dg, dr_k, dln_w, dln_b = post_vjp(do_ref[:, ln])
                dpre_ref[q, 3] = dy
                dvs_ref[q] = dv2
                for x, gx in zip(dvec[4:], (dr_k, dln_w, dln_b)):
                    x[:, ln] += gx
                dpre_ref[q, 6] = jnp.zeros_like(dpre_ref[q, 6])
                post.append((dr2, dk2, dg))

            def step(i, carry, p0=p0):
                c = nsub - 1 - i
                rows = _chunk_rows(c, C)
                row0 = pl.ds(c * C, 1)
                for q in range(G):
                    Gs = ds_ref[p0 + q] * blockdiag
                    S = st_ref[p0 + q, c]
                    t1 = _dot(bk_ref[q, c], Gs, NT)
                    dpy = jnp.concatenate([t1[:C], dpre_ref[q, 3, rows, :]], axis=0)
                    t2 = _dot(dpy, S)
                    t3 = _dot(pv_ref[q, c], Gs)
                    dvs_ref[q, rows, :] += t1[C:]
                    dpre_ref[q, 0, rows, :] = t2[:C]
                    dpre_ref[q, 1, rows, :] = t1[:C]
                    dpre_ref[q, 2, rows, :] = t2[C:]
                    dpre_ref[q, 4, rows, :] = t3[:C]
                    dpre_ref[q, 5, rows, :] = t3[C:]
                    dpre_ref[q, 6, row0, :] = jnp.sum(S * Gs, axis=0, keepdims=True)
                    ds_ref[p0 + q] = Gs * dec_ref[q, row0, :] + _dot(dpy, wq_ref[q, c], TN)
                return carry

            lax.fori_loop(0, nsub, step, 0)
            grads = prep_vjp(tuple(tuple(dpre_ref[q, i] for i in range(7)) + post[q][1:] for q in range(G)))
            for q in range(G):
                ln = _head_lanes(p0 + q)
                gq = grads[N_PREP_IN * q:N_PREP_IN * (q + 1)]
                dr_ref[:, ln] = gq[0] + post[q][0]
                dk_ref[:, ln] = gq[1]
                dv_ref[:, ln] = gq[2] + dvs_ref[q]
                dlw = dlw + gq[3]
                dgd = dgd + gq[4]
                for x, gx in zip(dvec[:4], gq[5:9]):
                    x[:, ln] += gx
                for x, gx in zip(dmat, gq[9:]):
                    x[p0 + q] += gx
        dlo_ref[:, 0:LANES] = dlw
        dlo_ref[:, LANES:2 * LANES] = dgd

    dcol = jax.ShapeDtypeStruct((T, RW_WIDTH), F32)
    dlo_spec = pl.BlockSpec((tb, 2 * LANES), lambda t: (nT - 1 - t, 0))
    blk = pltpu.VMEM((G, tb, LANES), F32)
    stacked = pltpu.VMEM((G, nsub, 2 * C, LANES), F32)
    return pl.pallas_call(
        body, name=name, grid=(nT,),
        in_specs=ins + [st, wide(do_blk)] + [vec] * N_RW_VEC + [mat] * N_RW_MAT,
        out_specs=[wide(0)] * 3 + [dlo_spec] + [vec] * N_RW_VEC + [mat] * N_RW_MAT,
        out_shape=[dcol] * 3 + [jax.ShapeDtypeStruct((T, 2 * LANES), F32)]
        + [jax.ShapeDtypeStruct((1, RW_WIDTH), F32)] * N_RW_VEC
        + [jax.ShapeDtypeStruct((RW_PAIRS, LANES, LANES), F32)] * N_RW_MAT,
        scratch_shapes=[pltpu.VMEM((RW_PAIRS, LANES, LANES), F32), stacked, stacked, stacked, stacked, blk, blk,
                        pltpu.VMEM((G, 7, tb, LANES), F32), blk],
    )(rws, rws, rws, rws, rws, states, do, *vecs, *mats)


def _my_index():
    return 4 * lax.axis_index("x") + 2 * lax.axis_index("y") + lax.axis_index("c")


def _peer(bits):
    pos = []
    for name, flip in zip(("x", "y", "c"), bits):
        i = lax.axis_index(name)
        pos.append(1 - i if flip else i)
    return tuple(pos)


def _peer_index(bits):
    x, y, c = _peer(bits)
    return 4 * x + 2 * y + c


def _all_gather(shards, name):
    n = len(shards)
    chips = [(1, 0, 0), (0, 1, 0), (1, 1, 0)]
    sib = (0, 0, 1)

    def body(*refs):
        ins, outs = refs[:n], refs[n:2 * n]
        send_sems, recv_sems, local_sems = refs[2 * n:]

        def rows(k, dev):
            r = ins[k].shape[0]
            return outs[k].at[pl.ds(dev * r, r), :]

        def copy(k, slot, block_dev, to_bits, src=None):
            return pltpu.make_async_remote_copy(
                src_ref=rows(k, block_dev) if src is None else src, dst_ref=rows(k, block_dev),
                send_sem=send_sems.at[k, slot], recv_sem=recv_sems.at[k, slot],
                device_id=_peer(to_bits), device_id_type=MESH_ID)

        me = _my_index()
        started = []
        for k in range(n):
            mine = pltpu.make_async_copy(ins[k], rows(k, me), local_sems.at[k])
            mine.start()
            started.append(mine)
        sends = []
        for k in range(n):
            first = [copy(k, 0, me, sib, src=ins[k])]
            first += [copy(k, 1 + j, me, chip, src=ins[k]) for j, chip in enumerate(chips)]
            for cp in first:
                cp.start()
            sends += first
        for k in range(n):
            for j, chip in enumerate(chips):
                copy(k, 1 + j, _peer_index(chip), chip).wait_recv()
                fwd = copy(k, 4 + j, _peer_index(chip), sib)
                fwd.start()
                sends.append(fwd)
        for k in range(n):
            copy(k, 0, _peer_index(sib), sib).wait_recv()
            for j, chip in enumerate(chips):
                both = (chip[0], chip[1], 1)
                copy(k, 4 + j, _peer_index(both), sib).wait_recv()
        for cp in sends:
            cp.wait_send()
        for cp in started:
            cp.wait()

    any_spec = pl.BlockSpec(memory_space=pl.ANY)
    return pl.pallas_call(
        body, name=name,
        in_specs=[any_spec] * n, out_specs=[any_spec] * n,
        out_shape=[jax.ShapeDtypeStruct((N_DEV * s.shape[0], s.shape[1]), s.dtype) for s in shards],
        scratch_shapes=[pltpu.SemaphoreType.DMA((n, 7)), pltpu.SemaphoreType.DMA((n, 7)),
                        pltpu.SemaphoreType.DMA((n,))],
    )(*shards)


def _exchange(partials, name):
    n = len(partials)
    flips = [(dx, dy, dc) for dx in (0, 1) for dy in (0, 1) for dc in (0, 1)][1:]

    def body(*refs):
        ins, outs = refs[:n], refs[n:2 * n]
        send_sems, recv_sems, local_sems = refs[2 * n:]
        me = _my_index()
        local = []
        for k in range(n):
            cp = pltpu.make_async_copy(ins[k].at[me], outs[k].at[me], local_sems.at[k])
            cp.start()
            local.append(cp)
        copies = []
        for k in range(n):
            for d, bits in enumerate(flips):
                cp = pltpu.make_async_remote_copy(
                    src_ref=ins[k].at[_peer_index(bits)], dst_ref=outs[k].at[me],
                    send_sem=send_sems.at[k, d], recv_sem=recv_sems.at[k, d],
                    device_id=_peer(bits), device_id_type=MESH_ID)
                cp.start()
                copies.append(cp)
        for cp in copies:
            cp.wait_recv()
        for cp in copies:
            cp.wait_send()
        for cp in local:
            cp.wait()

    any_spec = pl.BlockSpec(memory_space=pl.ANY)
    return pl.pallas_call(
        body, name=name,
        in_specs=[any_spec] * n, out_specs=[any_spec] * n,
        out_shape=[jax.ShapeDtypeStruct(p.shape, p.dtype) for p in partials],
        scratch_shapes=[pltpu.SemaphoreType.DMA((n, 7)), pltpu.SemaphoreType.DMA((n, 7)),
                        pltpu.SemaphoreType.DMA((n,))],
    )(*partials)


HBM_SPEC = pl.BlockSpec(memory_space=pltpu.HBM)
SEM_SPEC = pl.BlockSpec(memory_space=pltpu.SEMAPHORE)
ALL_FLIPS = [(dx, dy, dc) for dx in (0, 1) for dy in (0, 1) for dc in (0, 1)][1:]


def _spread_copies(srcs, lands, send_sems, recv_sems, per_peer_source):
    me = _my_index()
    copies = []
    for k, land in enumerate(lands):
        for d, bits in enumerate(ALL_FLIPS):
            src = srcs[k].at[_peer_index(bits)] if per_peer_source else land.at[me]
            copies.append(pltpu.make_async_remote_copy(
                src_ref=src, dst_ref=land.at[me],
                send_sem=send_sems.at[k * 7 + d], recv_sem=recv_sems.at[k * 7 + d],
                device_id=_peer(bits), device_id_type=MESH_ID))
    return copies


def _spread_start(srcs, lands, name):
    ns, n = len(srcs), len(lands)

    def body(*refs):
        src_refs, land_refs = refs[:ns], refs[ns:ns + n]
        send_sems, recv_sems = refs[ns + n], refs[ns + n + 1]
        token = refs[-1]
        for cp in _spread_copies(src_refs, land_refs, send_sems, recv_sems, ns > 0):
            cp.start()
        token[...] = jnp.zeros_like(token)

    bufs = list(srcs) + list(lands)
    out = pl.pallas_call(
        body, name=name,
        out_shape=(pltpu.SemaphoreType.DMA((7 * n,)), pltpu.SemaphoreType.DMA((7 * n,)),
                   *[pltpu.HBM(b.shape, b.dtype) for b in bufs], jax.ShapeDtypeStruct((8, LANES), F32)),
        in_specs=[HBM_SPEC] * (ns + n),
        out_specs=(SEM_SPEC, SEM_SPEC, *[HBM_SPEC] * (ns + n), pl.BlockSpec(memory_space=pltpu.VMEM)),
        input_output_aliases={i: 2 + i for i in range(ns + n)},
        compiler_params=pltpu.CompilerParams(has_side_effects=pltpu.SideEffectType.DATAFLOW_SIDE_EFFECTING),
    )(*[pltpu.with_memory_space_constraint(b, pltpu.HBM) for b in bufs])
    return out[0], out[1], list(out[2:2 + ns]), list(out[2 + ns:2 + ns + n]), out[-1]


def _spread_wait(send_sems, recv_sems, srcs, lands, after, name):
    ns, n = len(srcs), len(lands)

    def body(*refs):
        src_refs, land_refs = refs[:ns], refs[ns:ns + n]
        send_sems, recv_sems = refs[ns + n], refs[ns + n + 1]
        for cp in _spread_copies(src_refs, land_refs, send_sems, recv_sems, ns > 0):
            cp.wait_send()
            cp.wait_recv()

    bufs = list(srcs) + list(lands)
    out = pl.pallas_call(
        body, name=name,
        out_shape=tuple(pltpu.HBM(b.shape, b.dtype) for b in bufs),
        in_specs=[HBM_SPEC] * (ns + n) + [SEM_SPEC, SEM_SPEC, pl.BlockSpec(memory_space=pl.ANY)],
        out_specs=tuple([HBM_SPEC] * (ns + n)),
        input_output_aliases={i: i for i in range(ns + n)},
        compiler_params=pltpu.CompilerParams(has_side_effects=pltpu.SideEffectType.DATAFLOW_SIDE_EFFECTING),
    )(*bufs, send_sems, recv_sems, after)
    return list(out[ns:])


def _own_slot_only(block, me):
    return lax.dynamic_update_slice(jnp.zeros((N_DEV,) + block.shape, block.dtype), block[None], (me, 0, 0))


def _sum_slots(landed, name):
    _, R, C = landed.shape
    tb = _row_tile(R, 128)

    def body(l_ref, o_ref):
        acc = l_ref[0].astype(F32)
        for s in range(1, N_DEV):
            acc = acc + l_ref[s].astype(F32)
        o_ref[...] = acc

    return pl.pallas_call(
        body, name=name, grid=(R // tb,),
        in_specs=[pl.BlockSpec((N_DEV, tb, C), lambda i: (0, i, 0))],
        out_specs=pl.BlockSpec((tb, C), lambda i: (i, 0)),
        out_shape=jax.ShapeDtypeStruct((R, C), F32),
    )(landed)


def _pack_rows(flat_list, width=LANES):
    flat = jnp.concatenate([a.reshape(-1) for a in flat_list])
    n = flat.shape[0]
    rows = -(-n // width)
    rows = -(-rows // 8) * 8
    return jnp.pad(flat, (0, rows * width - n)).reshape(rows, width)


def _unpack(packed, shapes):
    flat = packed.reshape(-1)
    out, off = [], 0
    for s in shapes:
        n = 1
        for d in s:
            n *= d
        out.append(flat[off:off + n].reshape(s))
        off += n
    return out


def kernel(x, norm1_w, w_in, hg_lb_logits, hg_norm_w, rw_shift_mu, rw_w0, rw_w2, rw_a0, rw_a2, rw_g2, rw_k_k, rw_k_a, rw_r_k, rw_ln_w, rw_ln_b, w_out, norm2_w, w_up, conv_w, conv_b, w_down, final_norm_w, loss_target, m_norm1_w, m_w_in, m_hg_lb_logits, m_hg_norm_w, m_rw_shift_mu, m_rw_w0, m_rw_w2, m_rw_a0, m_rw_a2, m_rw_g2, m_rw_k_k, m_rw_k_a, m_rw_r_k, m_rw_ln_w, m_rw_ln_b, m_w_out, m_norm2_w, m_w_up, m_conv_w, m_conv_b, m_w_down, m_final_norm_w, v_norm1_w, v_w_in, v_hg_lb_logits, v_hg_norm_w, v_rw_shift_mu, v_rw_w0, v_rw_w2, v_rw_a0, v_rw_a2, v_rw_g2, v_rw_k_k, v_rw_k_a, v_rw_r_k, v_rw_ln_w, v_rw_ln_b, v_w_out, v_norm2_w, v_w_up, v_conv_w, v_conv_b, v_w_down, v_final_norm_w):
    weights = dict(norm1_w=norm1_w, w_in=w_in, hg_lb_logits=hg_lb_logits, hg_norm_w=hg_norm_w,
                   rw_shift_mu=rw_shift_mu, rw_w0=rw_w0, rw_w2=rw_w2, rw_a0=rw_a0, rw_a2=rw_a2, rw_g2=rw_g2,
                   rw_k_k=rw_k_k, rw_k_a=rw_k_a, rw_r_k=rw_r_k, rw_ln_w=rw_ln_w, rw_ln_b=rw_ln_b, w_out=w_out,
                   norm2_w=norm2_w, w_up=w_up, conv_w=conv_w, conv_b=conv_b, w_down=w_down,
                   final_norm_w=final_norm_w)
    m_in = dict(norm1_w=m_norm1_w, w_in=m_w_in, hg_lb_logits=m_hg_lb_logits, hg_norm_w=m_hg_norm_w,
                rw_shift_mu=m_rw_shift_mu, rw_w0=m_rw_w0, rw_w2=m_rw_w2, rw_a0=m_rw_a0, rw_a2=m_rw_a2,
                rw_g2=m_rw_g2, rw_k_k=m_rw_k_k, rw_k_a=m_rw_k_a, rw_r_k=m_rw_r_k, rw_ln_w=m_rw_ln_w,
                rw_ln_b=m_rw_ln_b, w_out=m_w_out, norm2_w=m_norm2_w, w_up=m_w_up, conv_w=m_conv_w,
                conv_b=m_conv_b, w_down=m_w_down, final_norm_w=m_final_norm_w)
    v_in = dict(norm1_w=v_norm1_w, w_in=v_w_in, hg_lb_logits=v_hg_lb_logits, hg_norm_w=v_hg_norm_w,
                rw_shift_mu=v_rw_shift_mu, rw_w0=v_rw_w0, rw_w2=v_rw_w2, rw_a0=v_rw_a0, rw_a2=v_rw_a2,
                rw_g2=v_rw_g2, rw_k_k=v_rw_k_k, rw_k_a=v_rw_k_a, rw_r_k=v_rw_r_k, rw_ln_w=v_rw_ln_w,
                rw_ln_b=v_rw_ln_b, w_out=v_w_out, norm2_w=v_norm2_w, w_up=v_w_up, conv_w=v_conv_w,
                conv_b=v_conv_b, w_down=v_w_down, final_norm_w=v_final_norm_w)
    names = list(weights)
    sharded_small = ["rw_w2", "rw_a2", "rw_g2", "conv_w"]
    sharded_big = ["w_in", "w_out", "w_up", "w_down"]
    replicated = [n for n in names if n not in sharded_small + sharded_big]

    xs = x[0]
    tgt = loss_target[0]

    small_shard = _pack_rows([weights[n] for n in sharded_small])
    g_win_t, g_small = _all_gather([w_in[0].T.astype(BF16), small_shard], "gather_weights")
    me = _my_index()
    later = (w_up[0].T.astype(BF16), w_out[0].astype(BF16), w_down[0].astype(BF16))
    later, _ = lax.optimization_barrier((later, g_small))
    later = [_own_slot_only(z, me) for z in later]
    g_send, g_recv, _, later, g_token = _spread_start([], later, "gather_later_start")
    small_shapes = [weights[n].shape for n in sharded_small]
    per_dev = [_unpack(g_small.reshape(N_DEV, -1)[j], small_shapes) for j in range(N_DEV)]
    w2_full, a2_full, g2_full, convw_full = [jnp.concatenate([per_dev[j][i][0] for j in range(N_DEV)], axis=-1)
                                             for i in range(4)]
    zeros64 = jnp.zeros((RW_PAIRS, 64, LANES), F32)
    by_pair = lambda z: z.reshape(z.shape[0], RW_PAIRS, LANES).transpose(1, 0, 2)
    w2p = jnp.concatenate([by_pair(w2_full), zeros64], axis=1)
    a2p = jnp.concatenate([zeros64, by_pair(a2_full)], axis=1)
    g2p = by_pair(g2_full)

    l0, l1 = hg_lb_logits[0:1], hg_lb_logits[1:2]
    h1 = _rms_fwd(xs, norm1_w + g_token[0:1, 0:1], "norm1")
    proj = _mm_nt(h1, g_win_t, "proj_in")
    o_hg, hg_states = _hg_fwd(proj, l0, l1, hg_norm_w, "hgrn2_fwd")
    rws = _shift_fwd(proj, rw_shift_mu, "token_shift")
    rw_vecs = [rw_w0, rw_a0, rw_k_k, rw_k_a, rw_r_k, rw_ln_w, rw_ln_b]
    rw_mats = [w2p, a2p, g2p]
    o_rw, rw_states = _rw_fwd(rws, rw_vecs, rw_mats, "rwkv7_fwd")
    o_mix = jnp.concatenate([o_hg, o_rw], axis=-1).astype(BF16)
    g_wup_t, g_wout, g_wdown = [z.reshape(-1, z.shape[-1])
                                for z in _spread_wait(g_send, g_recv, [], later, o_mix, "gather_later_wait")]
    x1 = _mm_nn(o_mix, g_wout, xs, "proj_out")
    h2 = _rms_fwd(x1, norm2_w, "norm2")
    u = _mm_nt(h2, g_wup_t, "ffn_up")
    act = _ffn_act_fwd(u, convw_full, conv_b, "ffn_act")
    x2 = _mm_nn(act, g_wdown, x1, "ffn_down")
    loss_part, dx2, d_final_w = _loss_head(x2, final_norm_w.reshape(1, -1), tgt, "loss_head")

    d_wdown = _mm_tn(act, dx2, 1408, "ffn_down_dw", BF16)
    dact = _mm_nt(dx2, g_wdown, "ffn_down_dx", BF16)
    du_g, du_v, dcw_g, dcw_v, dcb_g, dcb_v = _ffn_act_bwd(u, dact, convw_full, conv_b, "ffn_act_bwd")
    du = jnp.concatenate([du_g, du_v], axis=-1)
    d_convw = jnp.concatenate([dcw_g, dcw_v], axis=-1)
    d_convb = jnp.concatenate([dcb_g, dcb_v], axis=-1)
    d_wup_t = _mm_tn(du, h2, 1408, "ffn_up_dw", BF16)
    dh2 = _mm_nn(du, g_wup_t, None, "ffn_up_dx")
    dx1, d_norm2 = _rms_bwd(dh2, x1, norm2_w, dx2, "norm2_bwd")
    d_wout = _mm_tn(o_mix, dx1, 512, "proj_out_dw", BF16)
    do = _mm_nt(dx1, g_wout, "proj_out_dx")
    early = [z.reshape(N_DEV, z.shape[0] // N_DEV, z.shape[1]) for z in (d_wup_t, d_wout, d_wdown)]
    early_land = [_own_slot_only(lax.dynamic_index_in_dim(z, me, 0, keepdims=False), me) for z in early]
    e_send, e_recv, early, early_land, e_token = _spread_start(early, early_land, "exchange_early_start")
    hg_norm_w_t = hg_norm_w + e_token[0:1, 0:1]
    dq, df, di, dg, d_l0, d_l1, d_hg_nw = _hg_bwd(proj, hg_states, do, 0, l0, l1, hg_norm_w_t, "hgrn2_bwd")
    rw_out = _rw_bwd(rws, rw_states, do, 1, rw_vecs, rw_mats, "rwkv7_bwd")
    d_rw_vecs = rw_out[4:4 + N_RW_VEC]
    d_w2p, d_a2p, d_g2p = rw_out[4 + N_RW_VEC:]
    dp_parts, dmu_parts = [], []
    for i, z in enumerate(rw_out[:4]):
        dp, dmu = _shift_bwd(z, proj, rw_shift_mu, i * RW_WIDTH, "token_shift_bwd_%d" % i)
        dp_parts.append(dp)
        dmu_parts.append(dmu)
    d_mu = jnp.concatenate(dmu_parts, axis=-1)
    dproj = jnp.concatenate([dq, df, di, dg] + dp_parts, axis=-1)
    d_win_t = _mm_tn(dproj, h1, 768, "proj_in_dw", BF16)
    from_pairs = lambda z: z.transpose(1, 0, 2).reshape(z.shape[1], RW_WIDTH)
    d_w2 = from_pairs(d_w2p[:, :64])
    d_a2 = from_pairs(d_a2p[:, 64:])
    d_g2 = from_pairs(d_g2p)
    col_blocks = lambda z: z.reshape(z.shape[0], N_DEV, -1).transpose(1, 0, 2)
    small_part = jnp.stack([
        _pack_rows([col_blocks(d_w2)[j], col_blocks(d_a2)[j], col_blocks(d_g2)[j], col_blocks(d_convw)[j]])
        for j in range(N_DEV)])
    late = [d_win_t.reshape(N_DEV, d_win_t.shape[0] // N_DEV, d_win_t.shape[1]), small_part]
    late_land = [_own_slot_only(lax.dynamic_index_in_dim(z, me, 0, keepdims=False), me) for z in late]
    l_send, l_recv, late, late_land, l_token = _spread_start(late, late_land, "exchange_late_start")
    dh1 = _mm_nn(dproj, g_win_t, None, "proj_in_dx")
    grad_x, d_norm1 = _rms_bwd(dh1, xs, norm1_w + l_token[0:1, 0:1], dx1, "norm1_bwd")

    rep_grads = dict(norm1_w=d_norm1, hg_lb_logits=jnp.concatenate([d_l0, d_l1], axis=0), hg_norm_w=d_hg_nw,
                     rw_shift_mu=d_mu, rw_w0=d_rw_vecs[0], rw_a0=d_rw_vecs[1], rw_k_k=d_rw_vecs[2],
                     rw_k_a=d_rw_vecs[3], rw_r_k=d_rw_vecs[4], rw_ln_w=d_rw_vecs[5], rw_ln_b=d_rw_vecs[6],
                     norm2_w=d_norm2, conv_b=d_convb, final_norm_w=d_final_w)
    rep_pack = _pack_rows([loss_part] + [rep_grads[n] for n in replicated])
    rep_part = jnp.broadcast_to(rep_pack[None], (N_DEV,) + rep_pack.shape)
    landed_early = _spread_wait(e_send, e_recv, early, early_land, grad_x, "exchange_early_wait")
    landed_late = _spread_wait(l_send, l_recv, late, late_land, grad_x, "exchange_late_wait")
    (landed_rep,) = _exchange([rep_part], "exchange_grads")
    landed = [landed_late[0]] + landed_early + [landed_late[1], landed_rep]
    sums = [_sum_slots(z, "sum_grads_%d" % i) for i, z in enumerate(landed)]
    g_small_sum = _unpack(sums[4], small_shapes)
    rep_sum = _unpack(sums[5], [(1, 1)] + [weights[n].shape for n in replicated])
    loss = rep_sum[0].reshape(())
    grads = dict(zip(replicated, rep_sum[1:]))
    grads.update(dict(zip(sharded_small, g_small_sum)))
    grads["w_in"] = sums[0].T[None]
    grads["w_up"] = sums[1].T[None]
    grads["w_out"] = sums[2][None]
    grads["w_down"] = sums[3][None]

    delta, new_m, new_v = {}, {}, {}
    for n in sharded_big:
        shp = weights[n].shape
        as2d = lambda z: z.reshape(shp[1], shp[2])
        d, nm, nv = _adamw(as2d(weights[n]), as2d(grads[n]), as2d(m_in[n]), as2d(v_in[n]), "adamw_" + n)
        delta[n], new_m[n], new_v[n] = d.reshape(shp), nm.reshape(shp), nv.reshape(shp)
    small_names = replicated + sharded_small
    packs = [_pack_rows([src[n] for n in small_names]) for src in (weights, grads, m_in, v_in)]
    outs = _adamw(*packs, "adamw_small")
    small_shapes_all = [weights[n].shape for n in small_names]
    for dst, packed in zip((delta, new_m, new_v), outs):
        dst.update(dict(zip(small_names, _unpack(packed, small_shapes_all))))

    return (loss, grad_x[None], *[grads[n] for n in names], *[delta[n] for n in names],
            *[new_m[n] for n in names], *[new_v[n] for n in names])
```

```python
import functools

import jax
import jax.numpy as jnp
from jax import lax
from jax.experimental import pallas as pl
from jax.experimental.pallas import tpu as pltpu

F32 = jnp.float32
BF16 = jnp.bfloat16
HIGHEST = lax.Precision.HIGHEST
SCAN_PRECISION = None
MESH_ID = pl.DeviceIdType.MESH

N_DEV = 8
D_MODEL = 1024
HG_WIDTH = 512
HG_HEAD_DIM = 128
HG_HEADS = 4
RW_WIDTH = 512
RW_PAIRS = 4
RW_HEAD_DIM = 64
HG_COLS = 2048
RW_COLS = 1792
D_FF = 2816
NORM_EPS = 1e-6
RW_GN_EPS = 64e-5
L2_EPS = 1e-12
ADAM_LR, ADAM_B1, ADAM_B2, ADAM_EPS, ADAM_WD, ADAM_STEP = 0.001, 0.9, 0.999, 1e-08, 0.01, 10

HG_CHUNK = 16
RW_CHUNK = 64
SCAN_ROWS = 256
LANES = 128

NN = ((1,), (0,))
NT = ((1,), (1,))
TN = ((0,), (0,))


def _dot(a, b, dims=NN, precision=SCAN_PRECISION):
    if precision is None:
        a, b = a.astype(BF16), b.astype(BF16)
    return lax.dot_general(a, b, (dims, ((), ())), precision=precision, preferred_element_type=F32)


def _iota2(shape, dim):
    return lax.broadcasted_iota(jnp.int32, shape, dim)


def _sigmoid(z):
    return 0.5 * jnp.tanh(0.5 * z) + 0.5


def _row_tile(n, want):
    t = min(n, want)
    while n % t:
        t //= 2
    return t


def _rms_fwd(x, w, name):
    T, D = x.shape
    tb = _row_tile(T, 512)

    def body(x_ref, w_ref, h_ref):
        xv = x_ref[...]
        r = lax.rsqrt(jnp.mean(xv * xv, axis=-1, keepdims=True) + NORM_EPS)
        h_ref[...] = (xv * r * w_ref[...]).astype(h_ref.dtype)

    return pl.pallas_call(
        body, name=name, grid=(T // tb,),
        in_specs=[pl.BlockSpec((tb, D), lambda i: (i, 0)), pl.BlockSpec((1, D), lambda i: (0, 0))],
        out_specs=pl.BlockSpec((tb, D), lambda i: (i, 0)),
        out_shape=jax.ShapeDtypeStruct((T, D), BF16),
    )(x, w)


def _rms_bwd(dh, x, w, dres, name):
    T, D = x.shape
    tb = _row_tile(T, 256)

    def body(dh_ref, x_ref, w_ref, dres_ref, dx_ref, dw_ref):
        @pl.when(pl.program_id(0) == 0)
        def _():
            dw_ref[...] = jnp.zeros_like(dw_ref)

        xv = x_ref[...]
        r = lax.rsqrt(jnp.mean(xv * xv, axis=-1, keepdims=True) + NORM_EPS)
        xn = xv * r
        dy = dh_ref[...].astype(F32)
        dxn = dy * w_ref[...]
        dx_ref[...] = dres_ref[...] + r * (dxn - xn * jnp.mean(dxn * xn, axis=-1, keepdims=True))
        dw_ref[...] += jnp.sum(dy * xn, axis=0, keepdims=True)

    row = pl.BlockSpec((tb, D), lambda i: (i, 0))
    vec = pl.BlockSpec((1, D), lambda i: (0, 0))
    return pl.pallas_call(
        body, name=name, grid=(T // tb,),
        in_specs=[row, row, vec, row], out_specs=[row, vec],
        out_shape=[jax.ShapeDtypeStruct((T, D), F32), jax.ShapeDtypeStruct((1, D), F32)],
    )(dh, x, w, dres)


def _mm_nt(a, bt, name, out_dtype=F32):
    T, K = a.shape
    N = bt.shape[0]
    tm = _row_tile(T, 256)

    def body(a_ref, b_ref, o_ref):
        o_ref[...] = _dot(a_ref[...].astype(BF16), b_ref[...].astype(BF16), NT, None).astype(o_ref.dtype)

    return pl.pallas_call(
        body, name=name, grid=(T // tm,),
        in_specs=[pl.BlockSpec((tm, K), lambda i: (i, 0)), pl.BlockSpec((N, K), lambda i: (0, 0))],
        out_specs=pl.BlockSpec((tm, N), lambda i: (i, 0)),
        out_shape=jax.ShapeDtypeStruct((T, N), out_dtype),
    )(a, bt)


def _mm_nn(a, b, res, name, out_dtype=F32):
    parts = list(a) if isinstance(a, (list, tuple)) else [a]
    T = parts[0].shape[0]
    K, N = b.shape
    tm = _row_tile(T, 256)
    widths = [p.shape[1] for p in parts]
    n = len(parts)

    def body(*refs):
        b_ref, o_ref = refs[n], refs[-1]
        acc, off = None, 0
        for a_ref, w in zip(refs[:n], widths):
            d = _dot(a_ref[...].astype(BF16), b_ref[off:off + w, :].astype(BF16), NN, None)
            acc = d if acc is None else acc + d
            off += w
        if res is not None:
            acc = acc + refs[n + 1][...]
        o_ref[...] = acc.astype(o_ref.dtype)

    in_specs = [pl.BlockSpec((tm, w), lambda i: (i, 0)) for w in widths] + [pl.BlockSpec((K, N), lambda i: (0, 0))]
    args = parts + [b]
    if res is not None:
        in_specs.append(pl.BlockSpec((tm, N), lambda i: (i, 0)))
        args.append(res)
    return pl.pallas_call(
        body, name=name, grid=(T // tm,), in_specs=in_specs,
        out_specs=pl.BlockSpec((tm, N), lambda i: (i, 0)),
        out_shape=jax.ShapeDtypeStruct((T, N), out_dtype),
    )(*args)


def _mm_tn(a, b, tmm, name, out_dtype=F32):
    T, M = a.shape
    N = b.shape[1]
    tk = _row_tile(T, 512)
    nk = T // tk

    def body(a_ref, b_ref, o_ref, acc_ref):
        @pl.when(pl.program_id(1) == 0)
        def _():
            acc_ref[...] = jnp.zeros_like(acc_ref)

        acc_ref[...] += _dot(a_ref[...].astype(BF16), b_ref[...].astype(BF16), TN, None)

        @pl.when(pl.program_id(1) == nk - 1)
        def _():
            o_ref[...] = acc_ref[...].astype(o_ref.dtype)

    return pl.pallas_call(
        body, name=name, grid=(M // tmm, nk),
        in_specs=[pl.BlockSpec((tk, tmm), lambda m, k: (k, m)), pl.BlockSpec((tk, N), lambda m, k: (k, 0))],
        out_specs=pl.BlockSpec((tmm, N), lambda m, k: (m, 0)),
        out_shape=jax.ShapeDtypeStruct((M, N), out_dtype),
        scratch_shapes=[pltpu.VMEM((tmm, N), F32)],
    )(a, b)


def _shift_rows_down(z, n):
    rows = _iota2(z.shape, 0)
    return jnp.where(rows < n, 0.0, pltpu.roll(z, n, 0))


def _shift_rows_up(z, n):
    T = z.shape[0]
    rows = _iota2(z.shape, 0)
    return jnp.where(rows >= T - n, 0.0, pltpu.roll(z, T - n, 0))


def _shift_fwd(proj, mu, name):
    T = proj.shape[0]
    nblk = RW_COLS // LANES
    first = HG_COLS // LANES

    def body(p_ref, mu_ref, o_ref):
        p = p_ref[...]
        o_ref[...] = p + (_shift_rows_down(p, 1) - p) * mu_ref[...]

    return pl.pallas_call(
        body, name=name, grid=(nblk,),
        in_specs=[pl.BlockSpec((T, LANES), lambda j: (0, first + j)), pl.BlockSpec((1, LANES), lambda j: (0, j))],
        out_specs=pl.BlockSpec((T, LANES), lambda j: (0, j)),
        out_shape=jax.ShapeDtypeStruct((T, RW_COLS), F32),
    )(proj, mu)


def _shift_bwd(ds, proj, mu, col0, name):
    T, width = ds.shape
    nblk = width // LANES
    first = (HG_COLS + col0) // LANES
    mu0 = col0 // LANES

    def body(ds_ref, p_ref, mu_ref, dp_ref, dmu_ref):
        dsv = ds_ref[...]
        p = p_ref[...]
        m = mu_ref[...]
        dp_ref[...] = (dsv * (1.0 - m) + _shift_rows_up(dsv * m, 1)).astype(dp_ref.dtype)
        dmu_ref[...] = jnp.sum(dsv * (_shift_rows_down(p, 1) - p), axis=0, keepdims=True)

    return pl.pallas_call(
        body, name=name, grid=(nblk,),
        in_specs=[pl.BlockSpec((T, LANES), lambda j: (0, j)),
                  pl.BlockSpec((T, LANES), lambda j: (0, first + j)),
                  pl.BlockSpec((1, LANES), lambda j: (0, mu0 + j))],
        out_specs=[pl.BlockSpec((T, LANES), lambda j: (0, j)), pl.BlockSpec((1, LANES), lambda j: (0, j))],
        out_shape=[jax.ShapeDtypeStruct((T, width), BF16), jax.ShapeDtypeStruct((1, width), F32)],
    )(ds, proj, mu)


def _conv3(z, w_ref):
    return w_ref[0:1, :] * _shift_rows_down(z, 2) + w_ref[1:2, :] * _shift_rows_down(z, 1) + w_ref[2:3, :] * z


def _ffn_act_fwd(u, conv_w, conv_b, name):
    T = u.shape[0]
    nblk = D_FF // LANES

    def body(ug_ref, uv_ref, wg_ref, wv_ref, bg_ref, bv_ref, act_ref):
        gate = _conv3(ug_ref[...], wg_ref) + bg_ref[...]
        val = _conv3(uv_ref[...], wv_ref) + bv_ref[...]
        act_ref[...] = (gate * _sigmoid(gate) * val).astype(act_ref.dtype)

    col = lambda off: pl.BlockSpec((T, LANES), lambda j: (0, off + j))
    wsp = lambda off: pl.BlockSpec((3, LANES), lambda j: (0, off + j))
    bsp = lambda off: pl.BlockSpec((1, LANES), lambda j: (0, off + j))
    return pl.pallas_call(
        body, name=name, grid=(nblk,),
        in_specs=[col(0), col(nblk), wsp(0), wsp(nblk), bsp(0), bsp(nblk)],
        out_specs=pl.BlockSpec((T, LANES), lambda j: (0, j)),
        out_shape=jax.ShapeDtypeStruct((T, D_FF), BF16),
    )(u, u, conv_w, conv_w, conv_b, conv_b)


def _ffn_act_bwd(u, dact, conv_w, conv_b, name):
    T = u.shape[0]
    nblk = D_FF // LANES

    def conv_bwd(z, dzc, w_ref, du_ref, dw_ref, db_ref):
        up1, up2 = _shift_rows_up(dzc, 1), _shift_rows_up(dzc, 2)
        du = w_ref[2:3, :] * dzc + w_ref[1:2, :] * up1 + w_ref[0:1, :] * up2
        du_ref[...] = du.astype(du_ref.dtype)
        dw_ref[0:1, :] = jnp.sum(up2 * z, axis=0, keepdims=True)
        dw_ref[1:2, :] = jnp.sum(up1 * z, axis=0, keepdims=True)
        dw_ref[2:3, :] = jnp.sum(dzc * z, axis=0, keepdims=True)
        db_ref[...] = jnp.sum(dzc, axis=0, keepdims=True)

    def body(ug_ref, uv_ref, da_ref, wg_ref, wv_ref, bg_ref, bv_ref,
             dug_ref, duv_ref, dwg_ref, dwv_ref, dbg_ref, dbv_ref):
        ug, uv = ug_ref[...], uv_ref[...]
        gate = _conv3(ug, wg_ref) + bg_ref[...]
        val = _conv3(uv, wv_ref) + bv_ref[...]
        da = da_ref[...].astype(F32)
        sg = _sigmoid(gate)
        dgate = da * val * (sg * (1.0 + gate * (1.0 - sg)))
        dval = da * gate * sg
        conv_bwd(ug, dgate, wg_ref, dug_ref, dwg_ref, dbg_ref)
        conv_bwd(uv, dval, wv_ref, duv_ref, dwv_ref, dbv_ref)

    col = lambda off: pl.BlockSpec((T, LANES), lambda j: (0, off + j))
    wsp = lambda off: pl.BlockSpec((3, LANES), lambda j: (0, off + j))
    bsp = lambda off: pl.BlockSpec((1, LANES), lambda j: (0, off + j))
    half = lambda r, dt: jax.ShapeDtypeStruct((r, D_FF), dt)
    return pl.pallas_call(
        body, name=name, grid=(nblk,),
        in_specs=[col(0), col(nblk), col(0), wsp(0), wsp(nblk), bsp(0), bsp(nblk)],
        out_specs=[col(0), col(0), wsp(0), wsp(0), bsp(0), bsp(0)],
        out_shape=[half(T, BF16), half(T, BF16), half(3, F32), half(3, F32), half(1, F32), half(1, F32)],
    )(u, u, dact, conv_w, conv_w, conv_b, conv_b)


def _loss_head(x2, w, target, name):
    T, D = x2.shape
    tb = _row_tile(T, 256)

    def body(x_ref, w_ref, t_ref, loss_ref, dx_ref, dw_ref):
        @pl.when(pl.program_id(0) == 0)
        def _():
            loss_ref[...] = jnp.zeros_like(loss_ref)
            dw_ref[...] = jnp.zeros_like(dw_ref)

        xv = x_ref[...]
        r = lax.rsqrt(jnp.mean(xv * xv, axis=-1, keepdims=True) + NORM_EPS)
        xn = xv * r
        err = xn * w_ref[...] - t_ref[...]
        row_loss = jnp.sum(err * err, axis=-1, keepdims=True) * (0.5 / D)
        loss_ref[...] += jnp.sum(row_loss, axis=0, keepdims=True)
        dy = err * (1.0 / D)
        dxn = dy * w_ref[...]
        dx_ref[...] = r * (dxn - xn * jnp.mean(dxn * xn, axis=-1, keepdims=True))
        dw_ref[...] += jnp.sum(dy * xn, axis=0, keepdims=True)

    row = pl.BlockSpec((tb, D), lambda i: (i, 0))
    vec = pl.BlockSpec((1, D), lambda i: (0, 0))
    return pl.pallas_call(
        body, name=name, grid=(T // tb,),
        in_specs=[row, vec, row],
        out_specs=[pl.BlockSpec((1, 1), lambda i: (0, 0)), row, vec],
        out_shape=[jax.ShapeDtypeStruct((1, 1), F32), jax.ShapeDtypeStruct((T, D), F32),
                   jax.ShapeDtypeStruct((1, D), F32)],
    )(x2, w, target)


def _adamw(w, g, m, v, name):
    R, C = w.shape
    tb = _row_tile(R, 256) if R % 8 == 0 else R

    def body(w_ref, g_ref, m_ref, v_ref, d_ref, nm_ref, nv_ref):
        gv = g_ref[...]
        nm = ADAM_B1 * m_ref[...] + (1.0 - ADAM_B1) * gv
        nv = ADAM_B2 * v_ref[...] + (1.0 - ADAM_B2) * (gv * gv)
        m_hat = nm / (1.0 - ADAM_B1 ** ADAM_STEP)
        v_hat = nv / (1.0 - ADAM_B2 ** ADAM_STEP)
        d_ref[...] = -ADAM_LR * (m_hat / (jnp.sqrt(v_hat) + ADAM_EPS) + ADAM_WD * w_ref[...])
        nm_ref[...] = nm
        nv_ref[...] = nv

    blk = pl.BlockSpec((tb, C), lambda i: (i, 0))
    sd = jax.ShapeDtypeStruct((R, C), F32)
    return pl.pallas_call(
        body, name=name, grid=(R // tb,), in_specs=[blk] * 4, out_specs=[blk] * 3, out_shape=[sd] * 3,
    )(w, g, m, v)


def _chunk_masks(rows, chunk):
    shift = chunk.bit_length() - 1
    i, j = _iota2((rows, rows), 0), _iota2((rows, rows), 1)
    same = jnp.right_shift(i, shift) == jnp.right_shift(j, shift)
    return same.astype(F32), (same & (j <= i)).astype(F32), (same & (j < i)).astype(F32)


def _head_lanes(h):
    return slice(h * LANES, (h + 1) * LANES)


def _chunk_rows(c, chunk):
    return pl.ds(pl.multiple_of(c * chunk, chunk), chunk)


def _hg_consts(rows):
    same, tril, _ = _chunk_masks(rows, HG_CHUNK)
    shift = HG_CHUNK.bit_length() - 1
    i, j = _iota2((rows, rows), 0), _iota2((rows, rows), 1)
    mid_row = jnp.left_shift(jnp.right_shift(i, shift), shift) + (HG_CHUNK // 2 - 1)
    upto_mid = (same > 0) & (j <= mid_row)
    return jnp.concatenate([tril, same, upto_mid.astype(F32)], axis=0), tril


N_HG_IN = 5


def _hg_prep(consts, *flat):
    sums, tril = consts
    rows = tril.shape[0]
    heads, logs = [], []
    for h in range(len(flat) // N_HG_IN):
        qr, fr, ir, l0, l1 = flat[N_HG_IN * h:N_HG_IN * (h + 1)]
        lb = _sigmoid(l0 - l1)
        f = lb + (1.0 - lb) * _sigmoid(fr)
        heads.append((qr * _sigmoid(qr) * (HG_HEAD_DIM ** -0.5), 1.0 - f, ir))
        logs.append(jnp.log(f))
    acc = _dot(sums, jnp.concatenate(logs, axis=1), NN, HIGHEST)
    sums_of = []
    for h in range(len(heads)):
        acc_h = acc[:, h * LANES:(h + 1) * LANES]
        sums_of.append((acc_h[:rows], acc_h[rows:2 * rows], acc_h[2 * rows:]))
    atts = [_dot(q * jnp.exp(a - mid), k * jnp.exp(mid - a), NT) * tril
            for (q, k, _), (a, _, mid) in zip(heads, sums_of)]
    intra = [_dot(att, ir) for att, (_, _, ir) in zip(atts, heads)]
    return tuple((q * jnp.exp(a), o_intra, k * jnp.exp(tot - a), jnp.exp(tot))
                 for (q, k, _), (a, tot, _), o_intra in zip(heads, sums_of, intra))


def _hg_prep_args(q_ref, f_ref, i_ref, l0_ref, l1_ref):
    flat = []
    for h in range(HG_HEADS):
        ln = _head_lanes(h)
        flat += [q_ref[:, ln], f_ref[:, ln], i_ref[:, ln], l0_ref[:, ln], l1_ref[:, ln]]
    return flat


def _hg_post(o, gr, nw):
    on = o * lax.rsqrt(jnp.mean(o * o, axis=-1, keepdims=True) + NORM_EPS)
    return on * nw * (gr * _sigmoid(gr))


def _hg_specs(T, tb, rev):
    nT = T // tb
    tix = (lambda t: nT - 1 - t) if rev else (lambda t: t)
    col = lambda blk: pl.BlockSpec((tb, HG_WIDTH), lambda t: (tix(t), blk))
    vec = pl.BlockSpec((1, HG_WIDTH), lambda t: (0, 0))
    st = pl.BlockSpec((HG_HEADS, tb // HG_CHUNK, HG_HEAD_DIM, HG_HEAD_DIM), lambda t: (0, tix(t), 0, 0))
    return nT, col, vec, st


def _hg_fwd(proj, l0, l1, nw, name):
    T = proj.shape[0]
    tb = _row_tile(T, SCAN_ROWS)
    nsub = tb // HG_CHUNK
    nT, col, vec, st = _hg_specs(T, tb, False)

    def body(q_ref, f_ref, i_ref, g_ref, l0_ref, l1_ref, nw_ref, o_ref, st_ref, s_ref, qe_ref, kd_ref, dec_ref):
        @pl.when(pl.program_id(0) == 0)
        def _():
            s_ref[...] = jnp.zeros_like(s_ref)

        consts = _hg_consts(tb)
        outs = _hg_prep(consts, *_hg_prep_args(q_ref, f_ref, i_ref, l0_ref, l1_ref))
        for h, (qe, o_intra, kd, dec) in enumerate(outs):
            qe_ref[h], kd_ref[h], dec_ref[h] = qe, kd, dec
            o_ref[:, _head_lanes(h)] = o_intra

        def step(c, carry):
            rows = _chunk_rows(c, HG_CHUNK)
            for h in range(HG_HEADS):
                ln = _head_lanes(h)
                S = s_ref[h]
                st_ref[h, c] = S
                o_ref[rows, ln] += _dot(qe_ref[h, rows, :], S, NT)
                s_ref[h] = S * dec_ref[h, pl.ds(c * HG_CHUNK, 1), :] + _dot(i_ref[rows, ln], kd_ref[h, rows, :], TN)
            return carry

        lax.fori_loop(0, nsub, step, 0)
        for h in range(HG_HEADS):
            ln = _head_lanes(h)
            o_ref[:, ln] = _hg_post(o_ref[:, ln], g_ref[:, ln], nw_ref[:, ln])

    blk = pltpu.VMEM((HG_HEADS, tb, LANES), F32)
    return pl.pallas_call(
        body, name=name, grid=(nT,),
        in_specs=[col(0), col(1), col(2), col(3), vec, vec, vec],
        out_specs=[col(0), st],
        out_shape=[jax.ShapeDtypeStruct((T, HG_WIDTH), F32),
                   jax.ShapeDtypeStruct((HG_HEADS, T // HG_CHUNK, HG_HEAD_DIM, HG_HEAD_DIM), F32)],
        scratch_shapes=[pltpu.VMEM((HG_HEADS, HG_HEAD_DIM, HG_HEAD_DIM), F32), blk, blk, blk],
    )(proj, proj, proj, proj, l0, l1, nw)


def _hg_bwd(proj, states, do, do_blk, l0, l1, nw, name):
    T = proj.shape[0]
    tb = _row_tile(T, SCAN_ROWS)
    nsub = tb // HG_CHUNK
    nT, col, vec, st = _hg_specs(T, tb, True)

    def body(q_ref, f_ref, i_ref, g_ref, st_ref, do_ref, l0_ref, l1_ref, nw_ref,
             dq_ref, df_ref, di_ref, dg_ref, dl0_ref, dl1_ref, dnw_ref,
             ds_ref, qe_ref, kd_ref, dec_ref, o_ref, dqe_ref, dkd_ref, ddec_ref, dis_ref):
        @pl.when(pl.program_id(0) == 0)
        def _():
            ds_ref[...] = jnp.zeros_like(ds_ref)
            dl0_ref[...] = jnp.zeros_like(dl0_ref)
            dl1_ref[...] = jnp.zeros_like(dl1_ref)
            dnw_ref[...] = jnp.zeros_like(dnw_ref)

        consts = _hg_consts(tb)
        outs, prep_vjp = jax.vjp(functools.partial(_hg_prep, consts),
                                 *_hg_prep_args(q_ref, f_ref, i_ref, l0_ref, l1_ref))
        for h, (qe, o_intra, kd, dec) in enumerate(outs):
            qe_ref[h], kd_ref[h], dec_ref[h], o_ref[h] = qe, kd, dec, o_intra

        def redo(c, carry):
            rows = _chunk_rows(c, HG_CHUNK)
            for h in range(HG_HEADS):
                o_ref[h, rows, :] += _dot(qe_ref[h, rows, :], st_ref[h, c], NT)
            return carry

        lax.fori_loop(0, nsub, redo, 0)
        for h in range(HG_HEADS):
            ln = _head_lanes(h)
            _, vjp = jax.vjp(_hg_post, o_ref[h], g_ref[:, ln], nw_ref[:, ln])
            d_o, dgr, dnw = vjp(do_ref[:, ln])
            o_ref[h] = d_o
            dg_ref[:, ln] = dgr.astype(dg_ref.dtype)
            dnw_ref[:, ln] += dnw
        ddec_ref[...] = jnp.zeros_like(ddec_ref)

        def step(i, carry):
            c = nsub - 1 - i
            rows = _chunk_rows(c, HG_CHUNK)
            row0 = pl.ds(c * HG_CHUNK, 1)
            for h in range(HG_HEADS):
                ln = _head_lanes(h)
                G = ds_ref[h]
                S = st_ref[h, c]
                d_o = o_ref[h, rows, :]
                dqe_ref[h, rows, :] = _dot(d_o, S)
                dkd_ref[h, rows, :] = _dot(i_ref[rows, ln], G)
                dis_ref[h, rows, :] = _dot(kd_ref[h, rows, :], G, NT)
                ddec_ref[h, row0, :] = jnp.sum(S * G, axis=0, keepdims=True)
                ds_ref[h] = G * dec_ref[h, row0, :] + _dot(d_o, qe_ref[h, rows, :], TN)
            return carry

        lax.fori_loop(0, nsub, step, 0)
        grads = prep_vjp(tuple((dqe_ref[h], o_ref[h], dkd_ref[h], ddec_ref[h]) for h in range(HG_HEADS)))
        for h in range(HG_HEADS):
            ln = _head_lanes(h)
            dq, df, di, dl0, dl1 = grads[N_HG_IN * h:N_HG_IN * (h + 1)]
            dq_ref[:, ln] = dq.astype(dq_ref.dtype)
            df_ref[:, ln] = df.astype(df_ref.dtype)
            di_ref[:, ln] = (di + dis_ref[h]).astype(di_ref.dtype)
            dl0_ref[:, ln] += dl0
            dl1_ref[:, ln] += dl1

    dcol = jax.ShapeDtypeStruct((T, HG_WIDTH), BF16)
    dvec = jax.ShapeDtypeStruct((1, HG_WIDTH), F32)
    blk = pltpu.VMEM((HG_HEADS, tb, LANES), F32)
    return pl.pallas_call(
        body, name=name, grid=(nT,),
        in_specs=[col(0), col(1), col(2), col(3), st, col(do_blk), vec, vec, vec],
        out_specs=[col(0)] * 4 + [vec] * 3,
        out_shape=[dcol] * 4 + [dvec] * 3,
        scratch_shapes=[pltpu.VMEM((HG_HEADS, HG_HEAD_DIM, HG_HEAD_DIM), F32)] + [blk] * 8,
    )(proj, proj, proj, proj, states, do, l0, l1, nw)


def _rw_consts(rows):
    same, tril, stril = _chunk_masks(rows, RW_CHUNK)
    br, bc = _iota2((LANES, LANES), 0), _iota2((LANES, LANES), 1)
    blockdiag = ((br < RW_HEAD_DIM) == (bc < RW_HEAD_DIM)).astype(F32)
    m0 = (_iota2((1, LANES), 1) < RW_HEAD_DIM).astype(F32)
    return same, tril, stril, blockdiag, m0, 1.0 - m0


def _unit_lower_inverses_impl(lows):
    rows = lows[0].shape[0]
    eye = (_iota2(lows[0].shape, 0) == _iota2(lows[0].shape, 1)).astype(F32)
    xs = [low + eye for low in lows]
    ps = [_dot(low, low) for low in lows]
    n = 4
    while n < RW_CHUNK:
        zs = [_dot(jnp.concatenate([p, x], axis=0), p) for p, x in zip(ps, xs)]
        ps = [z[:rows] for z in zs]
        xs = [x + z[rows:] for x, z in zip(xs, zs)]
        n *= 2
    return tuple(x + _dot(x, p) for x, p in zip(xs, ps))


@jax.custom_vjp
def _unit_lower_inverses(lows):
    return _unit_lower_inverses_impl(lows)


def _unit_lower_inverses_fwd(lows):
    xs = _unit_lower_inverses_impl(lows)
    return xs, xs


def _unit_lower_inverses_bwd(xs, dxs):
    ts = [_dot(x, dx, TN) for x, dx in zip(xs, dxs)]
    return (tuple(_dot(t, x, NT) for t, x in zip(ts, xs)),)


_unit_lower_inverses.defvjp(_unit_lower_inverses_fwd, _unit_lower_inverses_bwd)


N_PREP_IN = 12
N_PREP_OUT = 9
RW_GROUP = 2


def _rw_prep(consts, *flat):
    same, tril, stril, blockdiag, m0, m1 = consts
    rows = tril.shape[0]
    masks = (m0, m1)
    pre = []
    for i in range(len(flat) // N_PREP_IN):
        r, kx, v, lw, gd, w0, a0, k_k, k_a, w2p, a2p, g2 = flat[N_PREP_IN * i:N_PREP_IN * (i + 1)]
        xw = w0 + _dot(jnp.tanh(lw), w2p)
        w = jnp.minimum(xw, 0.0) - jnp.log(1.0 + jnp.exp(-jnp.abs(xw))) - 0.5
        ld = -jnp.exp(w)
        a_s = _sigmoid(a0 + _dot(lw, a2p))
        g = _dot(_sigmoid(gd), g2)
        kk = kx * k_k
        kk = kk / jnp.maximum(jnp.sqrt(_dot(kk * kk, blockdiag)), L2_EPS)
        k2 = kx * (1.0 + (a_s - 1.0) * k_a)
        bv = kk * a_s
        acc = _dot(jnp.concatenate([tril, same], axis=0), ld, NN, HIGHEST)
        cum, tot = acc[:rows], acc[rows:]
        ecn = jnp.exp(-cum)
        a_t = -kk * jnp.exp(cum - ld)
        r_t = r * jnp.exp(cum)
        rem = jnp.exp(tot - cum)
        z = _dot(jnp.concatenate([a_t * m0, a_t * m1, r_t * m0, r_t * m1], axis=0),
                 jnp.concatenate([bv * ecn, k2 * ecn], axis=0), NT)
        pre.append((v, a_t, r_t, z, (bv * rem, k2 * rem, jnp.exp(tot), k2, g)))
    heads = [(i, h) for i in range(len(pre)) for h in range(2)]
    za = {ih: pre[ih[0]][3][ih[1] * rows:(ih[1] + 1) * rows] for ih in heads}
    zr = {ih: pre[ih[0]][3][(2 + ih[1]) * rows:(3 + ih[1]) * rows] for ih in heads}
    tinv = dict(zip(heads, _unit_lower_inverses(tuple(za[ih][:, :rows] * stril for ih in heads))))
    lv = {ih: _dot(jnp.concatenate([za[ih][:, rows:] * stril, zr[ih][:, rows:] * tril], axis=0), pre[ih[0]][0])
          for ih in heads}
    wu = {ih: _dot(tinv[ih], jnp.concatenate([pre[ih[0]][1] * masks[ih[1]], lv[ih][:rows]], axis=1)) for ih in heads}
    w_m = {ih: wu[ih][:, :LANES] for ih in heads}
    u_m = {ih: masks[ih[1]] * wu[ih][:, LANES:] for ih in heads}
    qy = {ih: _dot(zr[ih][:, :rows] * tril, jnp.concatenate([w_m[ih], u_m[ih]], axis=1)) for ih in heads}
    outs = []
    for i in range(len(pre)):
        a, b = (i, 0), (i, 1)
        W = w_m[a] + w_m[b]
        U = u_m[a] + u_m[b]
        Q = pre[i][2] + qy[a][:, :LANES] + qy[b][:, :LANES]
        Y0 = qy[a][:, LANES:] + qy[b][:, LANES:] + m0 * lv[a][rows:] + m1 * lv[b][rows:]
        outs.append((W, U, Q, Y0) + pre[i][4])
    return tuple(outs)


def _rw_post(blockdiag, y, r, v, k2, g, r_k, ln_w, ln_b):
    inv_n = 1.0 / RW_HEAD_DIM
    yc = y - _dot(y, blockdiag) * inv_n
    var = _dot(yc * yc, blockdiag) * inv_n
    yn = yc * lax.rsqrt(var + RW_GN_EPS) * ln_w + ln_b
    bonus = _dot(r * k2 * r_k, blockdiag) * v
    return (yn + bonus) * g


N_RW_VEC = 7
N_RW_MAT = 3


def _rw_specs(T, tb, rev):
    nT = T // tb
    tix = (lambda t: nT - 1 - t) if rev else (lambda t: t)
    wide = lambda blk: pl.BlockSpec((tb, RW_WIDTH), lambda t: (tix(t), blk))
    narrow = lambda blk: pl.BlockSpec((tb, LANES), lambda t: (tix(t), blk))
    vec = pl.BlockSpec((1, RW_WIDTH), lambda t: (0, 0))
    mat = pl.BlockSpec((RW_PAIRS, LANES, LANES), lambda t: (0, 0, 0))
    st = pl.BlockSpec((RW_PAIRS, tb // RW_CHUNK, LANES, LANES), lambda t: (0, tix(t), 0, 0))
    lora0 = 3 * RW_WIDTH // LANES
    ins = [wide(0), wide(1), wide(2), narrow(lora0), narrow(lora0 + 1)]
    return nT, wide, vec, mat, st, ins


def _rw_prep_args(p, r_ref, k_ref, v_ref, lw_ref, gd_ref, vrefs, mrefs):
    ln = _head_lanes(p)
    w0, a0, k_k, k_a = [x[:, ln] for x in vrefs[:4]]
    return (r_ref[:, ln], k_ref[:, ln], v_ref[:, ln], lw_ref[...], gd_ref[...], w0, a0, k_k, k_a,
            *[x[p] for x in mrefs])


def _stack_chunks(ref, top, bottom):
    C = RW_CHUNK
    for c in range(ref.shape[0]):
        ref[c, 0:C, :] = top[c * C:(c + 1) * C]
        ref[c, C:2 * C, :] = bottom[c * C:(c + 1) * C]


def _group_args(p0, r_ref, k_ref, v_ref, lw_ref, gd_ref, vrefs, mrefs):
    flat = []
    for p in range(p0, p0 + RW_GROUP):
        flat += list(_rw_prep_args(p, r_ref, k_ref, v_ref, lw_ref, gd_ref, vrefs, mrefs))
    return flat


def _rw_fwd(rws, vecs, mats, name):
    T = rws.shape[0]
    tb = _row_tile(T, SCAN_ROWS)
    nsub = tb // RW_CHUNK
    C = RW_CHUNK
    nT, wide, vec, mat, st, ins = _rw_specs(T, tb, False)

    def body(*refs):
        r_ref, k_ref, v_ref, lw_ref, gd_ref = refs[:5]
        vrefs = refs[5:5 + N_RW_VEC]
        mrefs = refs[5 + N_RW_VEC:5 + N_RW_VEC + N_RW_MAT]
        o_ref, st_ref, s_ref, wq_ref, uy_ref, bk_ref, misc_ref, y_ref = refs[-8:]

        @pl.when(pl.program_id(0) == 0)
        def _():
            s_ref[...] = jnp.zeros_like(s_ref)

        consts = _rw_consts(tb)
        blockdiag = consts[3]
        for p0 in range(0, RW_PAIRS, RW_GROUP):
            outs = _rw_prep(consts, *_group_args(p0, r_ref, k_ref, v_ref, lw_ref, gd_ref, vrefs, mrefs))
            for p, (W, U, Q, Y0, Bg, Kg, dec, k2, g) in zip(range(p0, p0 + RW_GROUP), outs):
                _stack_chunks(wq_ref.at[p], W, Q)
                _stack_chunks(uy_ref.at[p], U, Y0)
                _stack_chunks(bk_ref.at[p], Bg, Kg)
                misc_ref[0, p], misc_ref[1, p], misc_ref[2, p] = dec, k2, g

        def step(c, carry):
            rows = _chunk_rows(c, C)
            for p in range(RW_PAIRS):
                S = s_ref[p]
                st_ref[p, c] = S
                py = _dot(wq_ref[p, c], S, NT) + uy_ref[p, c]
                y_ref[p, rows, :] = py[C:]
                pv = jnp.concatenate([py[:C], v_ref[rows, _head_lanes(p)]], axis=0)
                s_ref[p] = (S * misc_ref[0, p, pl.ds(c * C, 1), :] + _dot(pv, bk_ref[p, c], TN)) * blockdiag
            return carry

        lax.fori_loop(0, nsub, step, 0)
        for p in range(RW_PAIRS):
            ln = _head_lanes(p)
            r_k, ln_w, ln_b = [x[:, ln] for x in vrefs[4:]]
            o_ref[:, ln] = _rw_post(blockdiag, y_ref[p], r_ref[:, ln], v_ref[:, ln], misc_ref[1, p], misc_ref[2, p],
                                    r_k, ln_w, ln_b)

    stacked = pltpu.VMEM((RW_PAIRS, nsub, 2 * C, LANES), F32)
    return pl.pallas_call(
        body, name=name, grid=(nT,),
        in_specs=ins + [vec] * N_RW_VEC + [mat] * N_RW_MAT,
        out_specs=[wide(0), st],
        out_shape=[jax.ShapeDtypeStruct((T, RW_WIDTH), F32),
                   jax.ShapeDtypeStruct((RW_PAIRS, T // RW_CHUNK, LANES, LANES), F32)],
        scratch_shapes=[pltpu.VMEM((RW_PAIRS, LANES, LANES), F32), stacked, stacked, stacked,
                        pltpu.VMEM((3, RW_PAIRS, tb, LANES), F32), pltpu.VMEM((RW_PAIRS, tb, LANES), F32)],
    )(rws, rws, rws, rws, rws, *vecs, *mats)


def _rw_bwd(rws, states, do, do_blk, vecs, mats, name):
    T = rws.shape[0]
    tb = _row_tile(T, SCAN_ROWS)
    nsub = tb // RW_CHUNK
    C = RW_CHUNK
    G = RW_GROUP
    nT, wide, vec, mat, st, ins = _rw_specs(T, tb, True)
    nin = 5 + 1 + 1 + N_RW_VEC + N_RW_MAT

    def body(*refs):
        r_ref, k_ref, v_ref, lw_ref, gd_ref = refs[:5]
        st_ref, do_ref = refs[5], refs[6]
        vrefs = refs[7:7 + N_RW_VEC]
        mrefs = refs[7 + N_RW_VEC:nin]
        dr_ref, dk_ref, dv_ref, dlo_ref = refs[nin:nin + 4]
        dvec = refs[nin + 4:nin + 4 + N_RW_VEC]
        dmat = refs[nin + 4 + N_RW_VEC:nin + 4 + N_RW_VEC + N_RW_MAT]
        ds_ref, wq_ref, uy_ref, bk_ref, pv_ref, dec_ref, y_ref, dpre_ref, dvs_ref = refs[-9:]

        @pl.when(pl.program_id(0) == 0)
        def _():
            ds_ref[...] = jnp.zeros_like(ds_ref)
            for x in dvec + dmat:
                x[...] = jnp.zeros_like(x)

        consts = _rw_consts(tb)
        blockdiag = consts[3]
        dlw, dgd = 0.0, 0.0
        for p0 in range(0, RW_PAIRS, G):
            outs, prep_vjp = jax.vjp(functools.partial(_rw_prep, consts),
                                     *_group_args(p0, r_ref, k_ref, v_ref, lw_ref, gd_ref, vrefs, mrefs))
            for q, (W, U, Q, Y0, Bg, Kg, dec, _, _) in enumerate(outs):
                _stack_chunks(wq_ref.at[q], W, Q)
                _stack_chunks(uy_ref.at[q], U, Y0)
                _stack_chunks(bk_ref.at[q], Bg, Kg)
                dec_ref[q] = dec

            def redo(c, carry, p0=p0):
                rows = _chunk_rows(c, C)
                for q in range(G):
                    py = _dot(wq_ref[q, c], st_ref[p0 + q, c], NT) + uy_ref[q, c]
                    y_ref[q, rows, :] = py[C:]
                    pv_ref[q, c, 0:C, :] = py[:C]
                    pv_ref[q, c, C:2 * C, :] = v_ref[rows, _head_lanes(p0 + q)]
                return carry

            lax.fori_loop(0, nsub, redo, 0)
            post = []
            for q in range(G):
                ln = _head_lanes(p0 + q)
                r_k, ln_w, ln_b = [x[:, ln] for x in vrefs[4:]]
                _, post_vjp = jax.vjp(functools.partial(_rw_post, blockdiag), y_ref[q], r_ref[:, ln], v_ref[:, ln],
                                      outs[q][7], outs[q][8], r_k, ln_w, ln_b)
                dy, dr2, dv2, dk2, dg, dr_k, dln_w, dln_b = post_vjp(do_ref[:, ln])
                dpre_ref[q, 3] = dy
                dvs_ref[q] = dv2
                for x, gx in zip(dvec[4:], (dr_k, dln_w, dln_b)):
                    x[:, ln] += gx
                dpre_ref[q, 6] = jnp.zeros_like(dpre_ref[q, 6])
                post.append((dr2, dk2, dg))

            def step(i, carry, p0=p0):
                c = nsub - 1 - i
                rows = _chunk_rows(c, C)
                row0 = pl.ds(c * C, 1)
                for q in range(G):
                    Gs = ds_ref[p0 + q] * blockdiag
                    S = st_ref[p0 + q, c]
                    t1 = _dot(bk_ref[q, c], Gs, NT)
                    dpy = jnp.concatenate([t1[:C], dpre_ref[q, 3, rows, :]], axis=0)
                    t2 = _dot(dpy, S)
                    t3 = _dot(pv_ref[q, c], Gs)
                    dvs_ref[q, rows, :] += t1[C:]
                    dpre_ref[q, 0, rows, :] = t2[:C]
                    dpre_ref[q, 1, rows, :] = t1[:C]
                    dpre_ref[q, 2, rows, :] = t2[C:]
                    dpre_ref[q, 4, rows, :] = t3[:C]
                    dpre_ref[q, 5, rows, :] = t3[C:]
                    dpre_ref[q, 6, row0, :] = jnp.sum(S * Gs, axis=0, keepdims=True)
                    ds_ref[p0 + q] = Gs * dec_ref[q, row0, :] + _dot(dpy, wq_ref[q, c], TN)
                return carry

            lax.fori_loop(0, nsub, step, 0)
            grads = prep_vjp(tuple(tuple(dpre_ref[q, i] for i in range(7)) + post[q][1:] for q in range(G)))
            for q in range(G):
                ln = _head_lanes(p0 + q)
                gq = grads[N_PREP_IN * q:N_PREP_IN * (q + 1)]
                dr_ref[:, ln] = gq[0] + post[q][0]
                dk_ref[:, ln] = gq[1]
                dv_ref[:, ln] = gq[2] + dvs_ref[q]
                dlw = dlw + gq[3]
                dgd = dgd + gq[4]
                for x, gx in zip(dvec[:4], gq[5:9]):
                    x[:, ln] += gx
                for x, gx in zip(dmat, gq[9:]):
                    x[p0 + q] += gx
        dlo_ref[:, 0:LANES] = dlw
        dlo_ref[:, LANES:2 * LANES] = dgd

    dcol = jax.ShapeDtypeStruct((T, RW_WIDTH), F32)
    dlo_spec = pl.BlockSpec((tb, 2 * LANES), lambda t: (nT - 1 - t, 0))
    blk = pltpu.VMEM((G, tb, LANES), F32)
    stacked = pltpu.VMEM((G, nsub, 2 * C, LANES), F32)
    return pl.pallas_call(
        body, name=name, grid=(nT,),
        in_specs=ins + [st, wide(do_blk)] + [vec] * N_RW_VEC + [mat] * N_RW_MAT,
        out_specs=[wide(0)] * 3 + [dlo_spec] + [vec] * N_RW_VEC + [mat] * N_RW_MAT,
        out_shape=[dcol] * 3 + [jax.ShapeDtypeStruct((T, 2 * LANES), F32)]
        + [jax.ShapeDtypeStruct((1, RW_WIDTH), F32)] * N_RW_VEC
        + [jax.ShapeDtypeStruct((RW_PAIRS, LANES, LANES), F32)] * N_RW_MAT,
        scratch_shapes=[pltpu.VMEM((RW_PAIRS, LANES, LANES), F32), stacked, stacked, stacked, stacked, blk, blk,
                        pltpu.VMEM((G, 7, tb, LANES), F32), blk],
    )(rws, rws, rws, rws, rws, states, do, *vecs, *mats)


def _my_index():
    return 4 * lax.axis_index("x") + 2 * lax.axis_index("y") + lax.axis_index("c")


def _peer(bits):
    pos = []
    for name, flip in zip(("x", "y", "c"), bits):
        i = lax.axis_index(name)
        pos.append(1 - i if flip else i)
    return tuple(pos)


def _peer_index(bits):
    x, y, c = _peer(bits)
    return 4 * x + 2 * y + c


def _all_gather(shards, name):
    n = len(shards)
    chips = [(1, 0, 0), (0, 1, 0), (1, 1, 0)]
    sib = (0, 0, 1)

    def body(*refs):
        ins, outs = refs[:n], refs[n:2 * n]
        send_sems, recv_sems, local_sems = refs[2 * n:]

        def rows(k, dev):
            r = ins[k].shape[0]
            return outs[k].at[pl.ds(dev * r, r), :]

        def copy(k, slot, block_dev, to_bits, src=None):
            return pltpu.make_async_remote_copy(
                src_ref=rows(k, block_dev) if src is None else src, dst_ref=rows(k, block_dev),
                send_sem=send_sems.at[k, slot], recv_sem=recv_sems.at[k, slot],
                device_id=_peer(to_bits), device_id_type=MESH_ID)

        me = _my_index()
        started = []
        for k in range(n):
            mine = pltpu.make_async_copy(ins[k], rows(k, me), local_sems.at[k])
            mine.start()
            started.append(mine)
        sends = []
        for k in range(n):
            first = [copy(k, 0, me, sib, src=ins[k])]
            first += [copy(k, 1 + j, me, chip, src=ins[k]) for j, chip in enumerate(chips)]
            for cp in first:
                cp.start()
            sends += first
        for k in range(n):
            for j, chip in enumerate(chips):
                copy(k, 1 + j, _peer_index(chip), chip).wait_recv()
                fwd = copy(k, 4 + j, _peer_index(chip), sib)
                fwd.start()
                sends.append(fwd)
        for k in range(n):
            copy(k, 0, _peer_index(sib), sib).wait_recv()
            for j, chip in enumerate(chips):
                both = (chip[0], chip[1], 1)
                copy(k, 4 + j, _peer_index(both), sib).wait_recv()
        for cp in sends:
            cp.wait_send()
        for cp in started:
            cp.wait()

    any_spec = pl.BlockSpec(memory_space=pl.ANY)
    return pl.pallas_call(
        body, name=name,
        in_specs=[any_spec] * n, out_specs=[any_spec] * n,
        out_shape=[jax.ShapeDtypeStruct((N_DEV * s.shape[0], s.shape[1]), s.dtype) for s in shards],
        scratch_shapes=[pltpu.SemaphoreType.DMA((n, 7)), pltpu.SemaphoreType.DMA((n, 7)),
                        pltpu.SemaphoreType.DMA((n,))],
    )(*shards)


def _exchange(partials, name):
    n = len(partials)
    flips = [(dx, dy, dc) for dx in (0, 1) for dy in (0, 1) for dc in (0, 1)][1:]

    def body(*refs):
        ins, outs = refs[:n], refs[n:2 * n]
        send_sems, recv_sems, local_sems = refs[2 * n:]
        me = _my_index()
        local = []
        for k in range(n):
            cp = pltpu.make_async_copy(ins[k].at[me], outs[k].at[me], local_sems.at[k])
            cp.start()
            local.append(cp)
        copies = []
        for k in range(n):
            for d, bits in enumerate(flips):
                cp = pltpu.make_async_remote_copy(
                    src_ref=ins[k].at[_peer_index(bits)], dst_ref=outs[k].at[me],
                    send_sem=send_sems.at[k, d], recv_sem=recv_sems.at[k, d],
                    device_id=_peer(bits), device_id_type=MESH_ID)
                cp.start()
                copies.append(cp)
        for cp in copies:
            cp.wait_recv()
        for cp in copies:
            cp.wait_send()
        for cp in local:
            cp.wait()

    any_spec = pl.BlockSpec(memory_space=pl.ANY)
    return pl.pallas_call(
        body, name=name,
        in_specs=[any_spec] * n, out_specs=[any_spec] * n,
        out_shape=[jax.ShapeDtypeStruct(p.shape, p.dtype) for p in partials],
        scratch_shapes=[pltpu.SemaphoreType.DMA((n, 7)), pltpu.SemaphoreType.DMA((n, 7)),
                        pltpu.SemaphoreType.DMA((n,))],
    )(*partials)


HBM_SPEC = pl.BlockSpec(memory_space=pltpu.HBM)
SEM_SPEC = pl.BlockSpec(memory_space=pltpu.SEMAPHORE)
ALL_FLIPS = [(dx, dy, dc) for dx in (0, 1) for dy in (0, 1) for dc in (0, 1)][1:]


def _spread_copies(srcs, lands, send_sems, recv_sems, per_peer_source):
    me = _my_index()
    copies = []
    for k, land in enumerate(lands):
        for d, bits in enumerate(ALL_FLIPS):
            src = srcs[k].at[_peer_index(bits)] if per_peer_source else land.at[me]
            copies.append(pltpu.make_async_remote_copy(
                src_ref=src, dst_ref=land.at[me],
                send_sem=send_sems.at[k * 7 + d], recv_sem=recv_sems.at[k * 7 + d],
                device_id=_peer(bits), device_id_type=MESH_ID))
    return copies


def _spread_start(srcs, lands, name):
    ns, n = len(srcs), len(lands)

    def body(*refs):
        src_refs, land_refs = refs[:ns], refs[ns:ns + n]
        send_sems, recv_sems = refs[ns + n], refs[ns + n + 1]
        token = refs[-1]
        for cp in _spread_copies(src_refs, land_refs, send_sems, recv_sems, ns > 0):
            cp.start()
        token[...] = jnp.zeros_like(token)

    bufs = list(srcs) + list(lands)
    out = pl.pallas_call(
        body, name=name,
        out_shape=(pltpu.SemaphoreType.DMA((7 * n,)), pltpu.SemaphoreType.DMA((7 * n,)),
                   *[pltpu.HBM(b.shape, b.dtype) for b in bufs], jax.ShapeDtypeStruct((8, LANES), F32)),
        in_specs=[HBM_SPEC] * (ns + n),
        out_specs=(SEM_SPEC, SEM_SPEC, *[HBM_SPEC] * (ns + n), pl.BlockSpec(memory_space=pltpu.VMEM)),
        input_output_aliases={i: 2 + i for i in range(ns + n)},
        compiler_params=pltpu.CompilerParams(has_side_effects=pltpu.SideEffectType.DATAFLOW_SIDE_EFFECTING),
    )(*[pltpu.with_memory_space_constraint(b, pltpu.HBM) for b in bufs])
    return out[0], out[1], list(out[2:2 + ns]), list(out[2 + ns:2 + ns + n]), out[-1]


def _spread_wait(send_sems, recv_sems, srcs, lands, after, name):
    ns, n = len(srcs), len(lands)

    def body(*refs):
        src_refs, land_refs = refs[:ns], refs[ns:ns + n]
        send_sems, recv_sems = refs[ns + n], refs[ns + n + 1]
        for cp in _spread_copies(src_refs, land_refs, send_sems, recv_sems, ns > 0):
            cp.wait_send()
            cp.wait_recv()

    bufs = list(srcs) + list(lands)
    out = pl.pallas_call(
        body, name=name,
        out_shape=tuple(pltpu.HBM(b.shape, b.dtype) for b in bufs),
        in_specs=[HBM_SPEC] * (ns + n) + [SEM_SPEC, SEM_SPEC, pl.BlockSpec(memory_space=pl.ANY)],
        out_specs=tuple([HBM_SPEC] * (ns + n)),
        input_output_aliases={i: i for i in range(ns + n)},
        compiler_params=pltpu.CompilerParams(has_side_effects=pltpu.SideEffectType.DATAFLOW_SIDE_EFFECTING),
    )(*bufs, send_sems, recv_sems, after)
    return list(out[ns:])


def _own_slot_only(block, me):
    return lax.dynamic_update_slice(lax.empty((N_DEV,) + block.shape, block.dtype), block[None], (me, 0, 0))


def _sum_slots(landed, name):
    _, R, C = landed.shape
    tb = _row_tile(R, 128)

    def body(l_ref, o_ref):
        acc = l_ref[0].astype(F32)
        for s in range(1, N_DEV):
            acc = acc + l_ref[s].astype(F32)
        o_ref[...] = acc

    return pl.pallas_call(
        body, name=name, grid=(R // tb,),
        in_specs=[pl.BlockSpec((N_DEV, tb, C), lambda i: (0, i, 0))],
        out_specs=pl.BlockSpec((tb, C), lambda i: (i, 0)),
        out_shape=jax.ShapeDtypeStruct((R, C), F32),
    )(landed)


def _pack_rows(flat_list, width=LANES):
    flat = jnp.concatenate([a.reshape(-1) for a in flat_list])
    n = flat.shape[0]
    rows = -(-n // width)
    rows = -(-rows // 8) * 8
    return jnp.pad(flat, (0, rows * width - n)).reshape(rows, width)


def _unpack(packed, shapes):
    flat = packed.reshape(-1)
    out, off = [], 0
    for s in shapes:
        n = 1
        for d in s:
            n *= d
        out.append(flat[off:off + n].reshape(s))
        off += n
    return out


def kernel(x, norm1_w, w_in, hg_lb_logits, hg_norm_w, rw_shift_mu, rw_w0, rw_w2, rw_a0, rw_a2, rw_g2, rw_k_k, rw_k_a, rw_r_k, rw_ln_w, rw_ln_b, w_out, norm2_w, w_up, conv_w, conv_b, w_down, final_norm_w, loss_target, m_norm1_w, m_w_in, m_hg_lb_logits, m_hg_norm_w, m_rw_shift_mu, m_rw_w0, m_rw_w2, m_rw_a0, m_rw_a2, m_rw_g2, m_rw_k_k, m_rw_k_a, m_rw_r_k, m_rw_ln_w, m_rw_ln_b, m_w_out, m_norm2_w, m_w_up, m_conv_w, m_conv_b, m_w_down, m_final_norm_w, v_norm1_w, v_w_in, v_hg_lb_logits, v_hg_norm_w, v_rw_shift_mu, v_rw_w0, v_rw_w2, v_rw_a0, v_rw_a2, v_rw_g2, v_rw_k_k, v_rw_k_a, v_rw_r_k, v_rw_ln_w, v_rw_ln_b, v_w_out, v_norm2_w, v_w_up, v_conv_w, v_conv_b, v_w_down, v_final_norm_w):
    weights = dict(norm1_w=norm1_w, w_in=w_in, hg_lb_logits=hg_lb_logits, hg_norm_w=hg_norm_w,
                   rw_shift_mu=rw_shift_mu, rw_w0=rw_w0, rw_w2=rw_w2, rw_a0=rw_a0, rw_a2=rw_a2, rw_g2=rw_g2,
                   rw_k_k=rw_k_k, rw_k_a=rw_k_a, rw_r_k=rw_r_k, rw_ln_w=rw_ln_w, rw_ln_b=rw_ln_b, w_out=w_out,
                   norm2_w=norm2_w, w_up=w_up, conv_w=conv_w, conv_b=conv_b, w_down=w_down,
                   final_norm_w=final_norm_w)
    m_in = dict(norm1_w=m_norm1_w, w_in=m_w_in, hg_lb_logits=m_hg_lb_logits, hg_norm_w=m_hg_norm_w,
                rw_shift_mu=m_rw_shift_mu, rw_w0=m_rw_w0, rw_w2=m_rw_w2, rw_a0=m_rw_a0, rw_a2=m_rw_a2,
                rw_g2=m_rw_g2, rw_k_k=m_rw_k_k, rw_k_a=m_rw_k_a, rw_r_k=m_rw_r_k, rw_ln_w=m_rw_ln_w,
                rw_ln_b=m_rw_ln_b, w_out=m_w_out, norm2_w=m_norm2_w, w_up=m_w_up, conv_w=m_conv_w,
                conv_b=m_conv_b, w_down=m_w_down, final_norm_w=m_final_norm_w)
    v_in = dict(norm1_w=v_norm1_w, w_in=v_w_in, hg_lb_logits=v_hg_lb_logits, hg_norm_w=v_hg_norm_w,
                rw_shift_mu=v_rw_shift_mu, rw_w0=v_rw_w0, rw_w2=v_rw_w2, rw_a0=v_rw_a0, rw_a2=v_rw_a2,
                rw_g2=v_rw_g2, rw_k_k=v_rw_k_k, rw_k_a=v_rw_k_a, rw_r_k=v_rw_r_k, rw_ln_w=v_rw_ln_w,
                rw_ln_b=v_rw_ln_b, w_out=v_w_out, norm2_w=v_norm2_w, w_up=v_w_up, conv_w=v_conv_w,
                conv_b=v_conv_b, w_down=v_w_down, final_norm_w=v_final_norm_w)
    names = list(weights)
    sharded_small = ["rw_w2", "rw_a2", "rw_g2", "conv_w"]
    sharded_big = ["w_in", "w_out", "w_up", "w_down"]
    replicated = [n for n in names if n not in sharded_small + sharded_big]

    xs = x[0]
    tgt = loss_target[0]

    small_shard = _pack_rows([weights[n] for n in sharded_small])
    g_win_t, g_small = _all_gather([w_in[0].T.astype(BF16), small_shard], "gather_weights")
    me = _my_index()
    later = (w_up[0].T.astype(BF16), w_out[0].astype(BF16), w_down[0].astype(BF16))
    later, _ = lax.optimization_barrier((later, g_small))
    later = [_own_slot_only(z, me) for z in later]
    g_send, g_recv, _, later, g_token = _spread_start([], later, "gather_later_start")
    small_shapes = [weights[n].shape for n in sharded_small]
    per_dev = [_unpack(g_small.reshape(N_DEV, -1)[j], small_shapes) for j in range(N_DEV)]
    w2_full, a2_full, g2_full, convw_full = [jnp.concatenate([per_dev[j][i][0] for j in range(N_DEV)], axis=-1)
                                             for i in range(4)]
    zeros64 = jnp.zeros((RW_PAIRS, 64, LANES), F32)
    by_pair = lambda z: z.reshape(z.shape[0], RW_PAIRS, LANES).transpose(1, 0, 2)
    w2p = jnp.concatenate([by_pair(w2_full), zeros64], axis=1)
    a2p = jnp.concatenate([zeros64, by_pair(a2_full)], axis=1)
    g2p = by_pair(g2_full)

    l0, l1 = hg_lb_logits[0:1], hg_lb_logits[1:2]
    h1 = _rms_fwd(xs, norm1_w + g_token[0:1, 0:1], "norm1")
    proj = _mm_nt(h1, g_win_t, "proj_in")
    o_hg, hg_states = _hg_fwd(proj, l0, l1, hg_norm_w, "hgrn2_fwd")
    rws = _shift_fwd(proj, rw_shift_mu, "token_shift")
    rw_vecs = [rw_w0, rw_a0, rw_k_k, rw_k_a, rw_r_k, rw_ln_w, rw_ln_b]
    rw_mats = [w2p, a2p, g2p]
    o_rw, rw_states = _rw_fwd(rws, rw_vecs, rw_mats, "rwkv7_fwd")
    o_mix = jnp.concatenate([o_hg, o_rw], axis=-1).astype(BF16)
    g_wup_t, g_wout, g_wdown = [z.reshape(-1, z.shape[-1])
                                for z in _spread_wait(g_send, g_recv, [], later, o_mix, "gather_later_wait")]
    x1 = _mm_nn(o_mix, g_wout, xs, "proj_out")
    h2 = _rms_fwd(x1, norm2_w, "norm2")
    u = _mm_nt(h2, g_wup_t, "ffn_up")
    act = _ffn_act_fwd(u, convw_full, conv_b, "ffn_act")
    x2 = _mm_nn(act, g_wdown, x1, "ffn_down")
    loss_part, dx2, d_final_w = _loss_head(x2, final_norm_w.reshape(1, -1), tgt, "loss_head")

    d_wdown = _mm_tn(act, dx2, 1408, "ffn_down_dw", BF16)
    dact = _mm_nt(dx2, g_wdown, "ffn_down_dx", BF16)
    du_g, du_v, dcw_g, dcw_v, dcb_g, dcb_v = _ffn_act_bwd(u, dact, convw_full, conv_b, "ffn_act_bwd")
    d_convw = jnp.concatenate([dcw_g, dcw_v], axis=-1)
    d_convb = jnp.concatenate([dcb_g, dcb_v], axis=-1)
    d_wup_t = jnp.concatenate([_mm_tn(du_g, h2, 1408, "ffn_up_dw_gate", BF16),
                               _mm_tn(du_v, h2, 1408, "ffn_up_dw_value", BF16)], axis=0)
    dh2 = _mm_nn([du_g, du_v], g_wup_t, None, "ffn_up_dx")
    dx1, d_norm2 = _rms_bwd(dh2, x1, norm2_w, dx2, "norm2_bwd")
    d_wout = _mm_tn(o_mix, dx1, 512, "proj_out_dw", BF16)
    do = _mm_nt(dx1, g_wout, "proj_out_dx")
    early = [z.reshape(N_DEV, z.shape[0] // N_DEV, z.shape[1]) for z in (d_wup_t, d_wout, d_wdown)]
    early_land = [_own_slot_only(lax.dynamic_index_in_dim(z, me, 0, keepdims=False), me) for z in early]
    e_send, e_recv, early, early_land, e_token = _spread_start(early, early_land, "exchange_early_start")
    hg_norm_w_t = hg_norm_w + e_token[0:1, 0:1]
    dq, df, di, dg, d_l0, d_l1, d_hg_nw = _hg_bwd(proj, hg_states, do, 0, l0, l1, hg_norm_w_t, "hgrn2_bwd")
    rw_out = _rw_bwd(rws, rw_states, do, 1, rw_vecs, rw_mats, "rwkv7_bwd")
    d_rw_vecs = rw_out[4:4 + N_RW_VEC]
    d_w2p, d_a2p, d_g2p = rw_out[4 + N_RW_VEC:]
    dp_parts, dmu_parts = [], []
    for i, z in enumerate(rw_out[:4]):
        dp, dmu = _shift_bwd(z, proj, rw_shift_mu, i * RW_WIDTH, "token_shift_bwd_%d" % i)
        dp_parts.append(dp)
        dmu_parts.append(dmu)
    d_mu = jnp.concatenate(dmu_parts, axis=-1)
    dproj = jnp.concatenate([dq, df, di, dg] + dp_parts, axis=-1)
    d_win_t = _mm_tn(dproj, h1, 768, "proj_in_dw", BF16)
    from_pairs = lambda z: z.transpose(1, 0, 2).reshape(z.shape[1], RW_WIDTH)
    d_w2 = from_pairs(d_w2p[:, :64])
    d_a2 = from_pairs(d_a2p[:, 64:])
    d_g2 = from_pairs(d_g2p)
    col_blocks = lambda z: z.reshape(z.shape[0], N_DEV, -1).transpose(1, 0, 2)
    small_part = jnp.stack([
        _pack_rows([col_blocks(d_w2)[j], col_blocks(d_a2)[j], col_blocks(d_g2)[j], col_blocks(d_convw)[j]])
        for j in range(N_DEV)])
    late = [d_win_t.reshape(N_DEV, d_win_t.shape[0] // N_DEV, d_win_t.shape[1]), small_part]
    late_land = [_own_slot_only(lax.dynamic_index_in_dim(z, me, 0, keepdims=False), me) for z in late]
    l_send, l_recv, late, late_land, l_token = _spread_start(late, late_land, "exchange_late_start")
    dh1 = _mm_nn(dproj, g_win_t, None, "proj_in_dx")
    grad_x, d_norm1 = _rms_bwd(dh1, xs, norm1_w + l_token[0:1, 0:1], dx1, "norm1_bwd")

    rep_grads = dict(norm1_w=d_norm1, hg_lb_logits=jnp.concatenate([d_l0, d_l1], axis=0), hg_norm_w=d_hg_nw,
                     rw_shift_mu=d_mu, rw_w0=d_rw_vecs[0], rw_a0=d_rw_vecs[1], rw_k_k=d_rw_vecs[2],
                     rw_k_a=d_rw_vecs[3], rw_r_k=d_rw_vecs[4], rw_ln_w=d_rw_vecs[5], rw_ln_b=d_rw_vecs[6],
                     norm2_w=d_norm2, conv_b=d_convb, final_norm_w=d_final_w)
    rep_pack = _pack_rows([loss_part] + [rep_grads[n] for n in replicated])
    rep_part = jnp.broadcast_to(rep_pack[None], (N_DEV,) + rep_pack.shape)
    landed_early = _spread_wait(e_send, e_recv, early, early_land, grad_x, "exchange_early_wait")
    landed_late = _spread_wait(l_send, l_recv, late, late_land, grad_x, "exchange_late_wait")
    (landed_rep,) = _exchange([rep_part], "exchange_grads")
    landed = [landed_late[0]] + landed_early + [landed_late[1], landed_rep]
    sums = [_sum_slots(z, "sum_grads_%d" % i) for i, z in enumerate(landed)]
    g_small_sum = _unpack(sums[4], small_shapes)
    rep_sum = _unpack(sums[5], [(1, 1)] + [weights[n].shape for n in replicated])
    loss = rep_sum[0].reshape(())
    grads = dict(zip(replicated, rep_sum[1:]))
    grads.update(dict(zip(sharded_small, g_small_sum)))
    grads["w_in"] = sums[0].T[None]
    grads["w_up"] = sums[1].T[None]
    grads["w_out"] = sums[2][None]
    grads["w_down"] = sums[3][None]

    delta, new_m, new_v = {}, {}, {}
    for n in sharded_big:
        shp = weights[n].shape
        as2d = lambda z: z.reshape(shp[1], shp[2])
        d, nm, nv = _adamw(as2d(weights[n]), as2d(grads[n]), as2d(m_in[n]), as2d(v_in[n]), "adamw_" + n)
        delta[n], new_m[n], new_v[n] = d.reshape(shp), nm.reshape(shp), nv.reshape(shp)
    small_names = replicated + sharded_small
    packs = [_pack_rows([src[n] for n in small_names]) for src in (weights, grads, m_in, v_in)]
    outs = _adamw(*packs, "adamw_small")
    small_shapes_all = [weights[n].shape for n in small_names]
    for dst, packed in zip((delta, new_m, new_v), outs):
        dst.update(dict(zip(small_names, _unpack(packed, small_shapes_all))))

    return (loss, grad_x[None], *[grads[n] for n in names], *[delta[n] for n in names],
            *[new_m[n] for n in names], *[new_v[n] for n in names])
```

```python
import functools

import jax
import jax.numpy as jnp
from jax import lax
from jax.experimental import pallas as pl
from jax.experimental.pallas import tpu as pltpu

F32 = jnp.float32
BF16 = jnp.bfloat16
HIGHEST = lax.Precision.HIGHEST
SCAN_PRECISION = None
MESH_ID = pl.DeviceIdType.MESH

N_DEV = 8
D_MODEL = 1024
HG_WIDTH = 512
HG_HEAD_DIM = 128
HG_HEADS = 4
RW_WIDTH = 512
RW_PAIRS = 4
RW_HEAD_DIM = 64
HG_COLS = 2048
RW_COLS = 1792
D_FF = 2816
NORM_EPS = 1e-6
RW_GN_EPS = 64e-5
L2_EPS = 1e-12
ADAM_LR, ADAM_B1, ADAM_B2, ADAM_EPS, ADAM_WD, ADAM_STEP = 0.001, 0.9, 0.999, 1e-08, 0.01, 10

HG_CHUNK = 16
RW_CHUNK = 64
SCAN_ROWS = 256
LANES = 128

NN = ((1,), (0,))
NT = ((1,), (1,))
TN = ((0,), (0,))


def _dot(a, b, dims=NN, precision=SCAN_PRECISION):
    if precision is None:
        a, b = a.astype(BF16), b.astype(BF16)
    return lax.dot_general(a, b, (dims, ((), ())), precision=precision, preferred_element_type=F32)


def _iota2(shape, dim):
    return lax.broadcasted_iota(jnp.int32, shape, dim)


def _sigmoid(z):
    return 0.5 * jnp.tanh(0.5 * z) + 0.5


def _row_tile(n, want):
    t = min(n, want)
    while n % t:
        t //= 2
    return t


def _rms_fwd(x, w, name):
    T, D = x.shape
    tb = _row_tile(T, 512)

    def body(x_ref, w_ref, h_ref):
        xv = x_ref[...]
        r = lax.rsqrt(jnp.mean(xv * xv, axis=-1, keepdims=True) + NORM_EPS)
        h_ref[...] = (xv * r * w_ref[...]).astype(h_ref.dtype)

    return pl.pallas_call(
        body, name=name, grid=(T // tb,),
        in_specs=[pl.BlockSpec((tb, D), lambda i: (i, 0)), pl.BlockSpec((1, D), lambda i: (0, 0))],
        out_specs=pl.BlockSpec((tb, D), lambda i: (i, 0)),
        out_shape=jax.ShapeDtypeStruct((T, D), BF16),
    )(x, w)


def _rms_bwd(dh, x, w, dres, name):
    T, D = x.shape
    tb = _row_tile(T, 256)

    def body(dh_ref, x_ref, w_ref, dres_ref, dx_ref, dw_ref):
        @pl.when(pl.program_id(0) == 0)
        def _():
            dw_ref[...] = jnp.zeros_like(dw_ref)

        xv = x_ref[...]
        r = lax.rsqrt(jnp.mean(xv * xv, axis=-1, keepdims=True) + NORM_EPS)
        xn = xv * r
        dy = dh_ref[...].astype(F32)
        dxn = dy * w_ref[...]
        dx_ref[...] = dres_ref[...] + r * (dxn - xn * jnp.mean(dxn * xn, axis=-1, keepdims=True))
        dw_ref[...] += jnp.sum(dy * xn, axis=0, keepdims=True)

    row = pl.BlockSpec((tb, D), lambda i: (i, 0))
    vec = pl.BlockSpec((1, D), lambda i: (0, 0))
    return pl.pallas_call(
        body, name=name, grid=(T // tb,),
        in_specs=[row, row, vec, row], out_specs=[row, vec],
        out_shape=[jax.ShapeDtypeStruct((T, D), F32), jax.ShapeDtypeStruct((1, D), F32)],
    )(dh, x, w, dres)


def _mm_nt(a, bt, name, out_dtype=F32):
    T, K = a.shape
    N = bt.shape[0]
    tm = _row_tile(T, 256)

    def body(a_ref, b_ref, o_ref):
        o_ref[...] = _dot(a_ref[...].astype(BF16), b_ref[...].astype(BF16), NT, None).astype(o_ref.dtype)

    return pl.pallas_call(
        body, name=name, grid=(T // tm,),
        in_specs=[pl.BlockSpec((tm, K), lambda i: (i, 0)), pl.BlockSpec((N, K), lambda i: (0, 0))],
        out_specs=pl.BlockSpec((tm, N), lambda i: (i, 0)),
        out_shape=jax.ShapeDtypeStruct((T, N), out_dtype),
    )(a, bt)


def _mm_nn(a, b, res, name, out_dtype=F32):
    parts = list(a) if isinstance(a, (list, tuple)) else [a]
    T = parts[0].shape[0]
    K, N = b.shape
    tm = _row_tile(T, 256)
    widths = [p.shape[1] for p in parts]
    n = len(parts)

    def body(*refs):
        b_ref, o_ref = refs[n], refs[-1]
        acc, off = None, 0
        for a_ref, w in zip(refs[:n], widths):
            d = _dot(a_ref[...].astype(BF16), b_ref[off:off + w, :].astype(BF16), NN, None)
            acc = d if acc is None else acc + d
            off += w
        if res is not None:
            acc = acc + refs[n + 1][...]
        o_ref[...] = acc.astype(o_ref.dtype)

    in_specs = [pl.BlockSpec((tm, w), lambda i: (i, 0)) for w in widths] + [pl.BlockSpec((K, N), lambda i: (0, 0))]
    args = parts + [b]
    if res is not None:
        in_specs.append(pl.BlockSpec((tm, N), lambda i: (i, 0)))
        args.append(res)
    return pl.pallas_call(
        body, name=name, grid=(T // tm,), in_specs=in_specs,
        out_specs=pl.BlockSpec((tm, N), lambda i: (i, 0)),
        out_shape=jax.ShapeDtypeStruct((T, N), out_dtype),
    )(*args)


def _mm_tn(a, b, tmm, name, out_dtype=F32):
    T, M = a.shape
    N = b.shape[1]
    tk = _row_tile(T, 512)
    nk = T // tk

    def body(a_ref, b_ref, o_ref, acc_ref):
        @pl.when(pl.program_id(1) == 0)
        def _():
            acc_ref[...] = jnp.zeros_like(acc_ref)

        acc_ref[...] += _dot(a_ref[...].astype(BF16), b_ref[...].astype(BF16), TN, None)

        @pl.when(pl.program_id(1) == nk - 1)
        def _():
            o_ref[...] = acc_ref[...].astype(o_ref.dtype)

    return pl.pallas_call(
        body, name=name, grid=(M // tmm, nk),
        in_specs=[pl.BlockSpec((tk, tmm), lambda m, k: (k, m)), pl.BlockSpec((tk, N), lambda m, k: (k, 0))],
        out_specs=pl.BlockSpec((tmm, N), lambda m, k: (m, 0)),
        out_shape=jax.ShapeDtypeStruct((M, N), out_dtype),
        scratch_shapes=[pltpu.VMEM((tmm, N), F32)],
    )(a, b)


class _RowShifts:
    def __init__(self, shape):
        index = _iota2(shape, 0)
        self.rows = shape[0]
        self.first = {n: index < n for n in (1, 2)}
        self.last = {n: index >= shape[0] - n for n in (1, 2)}

    def down(self, z, n):
        return jnp.where(self.first[n], 0.0, pltpu.roll(z, n, 0))

    def up(self, z, n):
        return jnp.where(self.last[n], 0.0, pltpu.roll(z, self.rows - n, 0))


def _shift_fwd(proj, mu, name):
    T = proj.shape[0]
    nblk = RW_COLS // LANES
    first = HG_COLS // LANES

    def body(p_ref, mu_ref, o_ref):
        p = p_ref[...]
        o_ref[...] = p + (_RowShifts(p.shape).down(p, 1) - p) * mu_ref[...]

    return pl.pallas_call(
        body, name=name, grid=(nblk,),
        in_specs=[pl.BlockSpec((T, LANES), lambda j: (0, first + j)), pl.BlockSpec((1, LANES), lambda j: (0, j))],
        out_specs=pl.BlockSpec((T, LANES), lambda j: (0, j)),
        out_shape=jax.ShapeDtypeStruct((T, RW_COLS), F32),
    )(proj, mu)


def _shift_bwd(ds, proj, mu, col0, name):
    T, width = ds.shape
    nblk = width // LANES
    first = (HG_COLS + col0) // LANES
    mu0 = col0 // LANES

    def body(ds_ref, p_ref, mu_ref, dp_ref, dmu_ref):
        dsv = ds_ref[...]
        p = p_ref[...]
        m = mu_ref[...]
        shifts = _RowShifts(p.shape)
        dp_ref[...] = (dsv * (1.0 - m) + shifts.up(dsv * m, 1)).astype(dp_ref.dtype)
        dmu_ref[...] = jnp.sum(dsv * (shifts.down(p, 1) - p), axis=0, keepdims=True)

    return pl.pallas_call(
        body, name=name, grid=(nblk,),
        in_specs=[pl.BlockSpec((T, LANES), lambda j: (0, j)),
                  pl.BlockSpec((T, LANES), lambda j: (0, first + j)),
                  pl.BlockSpec((1, LANES), lambda j: (0, mu0 + j))],
        out_specs=[pl.BlockSpec((T, LANES), lambda j: (0, j)), pl.BlockSpec((1, LANES), lambda j: (0, j))],
        out_shape=[jax.ShapeDtypeStruct((T, width), BF16), jax.ShapeDtypeStruct((1, width), F32)],
    )(ds, proj, mu)


def _conv3(z, w_ref, shifts):
    return w_ref[0:1, :] * shifts.down(z, 2) + w_ref[1:2, :] * shifts.down(z, 1) + w_ref[2:3, :] * z


def _ffn_act_fwd(u, conv_w, conv_b, name):
    T = u.shape[0]
    nblk = D_FF // LANES

    def body(ug_ref, uv_ref, wg_ref, wv_ref, bg_ref, bv_ref, act_ref):
        shifts = _RowShifts((T, LANES))
        gate = _conv3(ug_ref[...], wg_ref, shifts) + bg_ref[...]
        val = _conv3(uv_ref[...], wv_ref, shifts) + bv_ref[...]
        act_ref[...] = (gate * _sigmoid(gate) * val).astype(act_ref.dtype)

    col = lambda off: pl.BlockSpec((T, LANES), lambda j: (0, off + j))
    wsp = lambda off: pl.BlockSpec((3, LANES), lambda j: (0, off + j))
    bsp = lambda off: pl.BlockSpec((1, LANES), lambda j: (0, off + j))
    return pl.pallas_call(
        body, name=name, grid=(nblk,),
        in_specs=[col(0), col(nblk), wsp(0), wsp(nblk), bsp(0), bsp(nblk)],
        out_specs=pl.BlockSpec((T, LANES), lambda j: (0, j)),
        out_shape=jax.ShapeDtypeStruct((T, D_FF), BF16),
    )(u, u, conv_w, conv_w, conv_b, conv_b)


def _ffn_act_bwd(u, dact, conv_w, conv_b, name):
    T = u.shape[0]
    nblk = D_FF // LANES

    def conv_bwd(z, dzc, w_ref, du_ref, dw_ref, db_ref, shifts):
        up1, up2 = shifts.up(dzc, 1), shifts.up(dzc, 2)
        du = w_ref[2:3, :] * dzc + w_ref[1:2, :] * up1 + w_ref[0:1, :] * up2
        du_ref[...] = du.astype(du_ref.dtype)
        dw_ref[0:1, :] = jnp.sum(up2 * z, axis=0, keepdims=True)
        dw_ref[1:2, :] = jnp.sum(up1 * z, axis=0, keepdims=True)
        dw_ref[2:3, :] = jnp.sum(dzc * z, axis=0, keepdims=True)
        db_ref[...] = jnp.sum(dzc, axis=0, keepdims=True)

    def body(ug_ref, uv_ref, da_ref, wg_ref, wv_ref, bg_ref, bv_ref,
             dug_ref, duv_ref, dwg_ref, dwv_ref, dbg_ref, dbv_ref):
        ug, uv = ug_ref[...], uv_ref[...]
        shifts = _RowShifts((T, LANES))
        gate = _conv3(ug, wg_ref, shifts) + bg_ref[...]
        val = _conv3(uv, wv_ref, shifts) + bv_ref[...]
        da = da_ref[...].astype(F32)
        sg = _sigmoid(gate)
        dgate = da * val * (sg * (1.0 + gate * (1.0 - sg)))
        dval = da * gate * sg
        conv_bwd(ug, dgate, wg_ref, dug_ref, dwg_ref, dbg_ref, shifts)
        conv_bwd(uv, dval, wv_ref, duv_ref, dwv_ref, dbv_ref, shifts)

    col = lambda off: pl.BlockSpec((T, LANES), lambda j: (0, off + j))
    wsp = lambda off: pl.BlockSpec((3, LANES), lambda j: (0, off + j))
    bsp = lambda off: pl.BlockSpec((1, LANES), lambda j: (0, off + j))
    half = lambda r, dt: jax.ShapeDtypeStruct((r, D_FF), dt)
    return pl.pallas_call(
        body, name=name, grid=(nblk,),
        in_specs=[col(0), col(nblk), col(0), wsp(0), wsp(nblk), bsp(0), bsp(nblk)],
        out_specs=[col(0), col(0), wsp(0), wsp(0), bsp(0), bsp(0)],
        out_shape=[half(T, BF16), half(T, BF16), half(3, F32), half(3, F32), half(1, F32), half(1, F32)],
    )(u, u, dact, conv_w, conv_w, conv_b, conv_b)


def _loss_head(x2, w, target, name):
    T, D = x2.shape
    tb = _row_tile(T, 256)

    def body(x_ref, w_ref, t_ref, loss_ref, dx_ref, dw_ref):
        @pl.when(pl.program_id(0) == 0)
        def _():
            loss_ref[...] = jnp.zeros_like(loss_ref)
            dw_ref[...] = jnp.zeros_like(dw_ref)

        xv = x_ref[...]
        r = lax.rsqrt(jnp.mean(xv * xv, axis=-1, keepdims=True) + NORM_EPS)
        xn = xv * r
        err = xn * w_ref[...] - t_ref[...]
        row_loss = jnp.sum(err * err, axis=-1, keepdims=True) * (0.5 / D)
        loss_ref[...] += jnp.sum(row_loss, axis=0, keepdims=True)
        dy = err * (1.0 / D)
        dxn = dy * w_ref[...]
        dx_ref[...] = r * (dxn - xn * jnp.mean(dxn * xn, axis=-1, keepdims=True))
        dw_ref[...] += jnp.sum(dy * xn, axis=0, keepdims=True)

    row = pl.BlockSpec((tb, D), lambda i: (i, 0))
    vec = pl.BlockSpec((1, D), lambda i: (0, 0))
    return pl.pallas_call(
        body, name=name, grid=(T // tb,),
        in_specs=[row, vec, row],
        out_specs=[pl.BlockSpec((1, 1), lambda i: (0, 0)), row, vec],
        out_shape=[jax.ShapeDtypeStruct((1, 1), F32), jax.ShapeDtypeStruct((T, D), F32),
                   jax.ShapeDtypeStruct((1, D), F32)],
    )(x2, w, target)


def _adamw(w, g, m, v, name):
    R, C = w.shape
    tb = _row_tile(R, 256) if R % 8 == 0 else R

    def body(w_ref, g_ref, m_ref, v_ref, d_ref, nm_ref, nv_ref):
        gv = g_ref[...]
        nm = ADAM_B1 * m_ref[...] + (1.0 - ADAM_B1) * gv
        nv = ADAM_B2 * v_ref[...] + (1.0 - ADAM_B2) * (gv * gv)
        m_hat = nm / (1.0 - ADAM_B1 ** ADAM_STEP)
        v_hat = nv / (1.0 - ADAM_B2 ** ADAM_STEP)
        d_ref[...] = -ADAM_LR * (m_hat / (jnp.sqrt(v_hat) + ADAM_EPS) + ADAM_WD * w_ref[...])
        nm_ref[...] = nm
        nv_ref[...] = nv

    blk = pl.BlockSpec((tb, C), lambda i: (i, 0))
    sd = jax.ShapeDtypeStruct((R, C), F32)
    return pl.pallas_call(
        body, name=name, grid=(R // tb,), in_specs=[blk] * 4, out_specs=[blk] * 3, out_shape=[sd] * 3,
    )(w, g, m, v)


def _chunk_masks(rows, chunk):
    shift = chunk.bit_length() - 1
    i, j = _iota2((rows, rows), 0), _iota2((rows, rows), 1)
    same = jnp.right_shift(i, shift) == jnp.right_shift(j, shift)
    return same.astype(F32), (same & (j <= i)).astype(F32), (same & (j < i)).astype(F32)


def _head_lanes(h):
    return slice(h * LANES, (h + 1) * LANES)


def _chunk_rows(c, chunk):
    return pl.ds(pl.multiple_of(c * chunk, chunk), chunk)


def _hg_consts(rows):
    same, tril, _ = _chunk_masks(rows, HG_CHUNK)
    shift = HG_CHUNK.bit_length() - 1
    i, j = _iota2((rows, rows), 0), _iota2((rows, rows), 1)
    mid_row = jnp.left_shift(jnp.right_shift(i, shift), shift) + (HG_CHUNK // 2 - 1)
    upto_mid = (same > 0) & (j <= mid_row)
    return jnp.concatenate([tril, same, upto_mid.astype(F32)], axis=0), tril


N_HG_IN = 5


def _hg_prep(consts, *flat):
    sums, tril = consts
    rows = tril.shape[0]
    heads, logs = [], []
    for h in range(len(flat) // N_HG_IN):
        qr, fr, ir, l0, l1 = flat[N_HG_IN * h:N_HG_IN * (h + 1)]
        lb = _sigmoid(l0 - l1)
        f = lb + (1.0 - lb) * _sigmoid(fr)
        heads.append((qr * _sigmoid(qr) * (HG_HEAD_DIM ** -0.5), 1.0 - f, ir))
        logs.append(jnp.log(f))
    acc = _dot(sums, jnp.concatenate(logs, axis=1), NN, HIGHEST)
    sums_of = []
    for h in range(len(heads)):
        acc_h = acc[:, h * LANES:(h + 1) * LANES]
        sums_of.append((acc_h[:rows], acc_h[rows:2 * rows], acc_h[2 * rows:]))
    atts = [_dot(q * jnp.exp(a - mid), k * jnp.exp(mid - a), NT) * tril
            for (q, k, _), (a, _, mid) in zip(heads, sums_of)]
    intra = [_dot(att, ir) for att, (_, _, ir) in zip(atts, heads)]
    return tuple((q * jnp.exp(a), o_intra, k * jnp.exp(tot - a), jnp.exp(tot))
                 for (q, k, _), (a, tot, _), o_intra in zip(heads, sums_of, intra))


def _hg_prep_args(q_ref, f_ref, i_ref, l0_ref, l1_ref):
    flat = []
    for h in range(HG_HEADS):
        ln = _head_lanes(h)
        flat += [q_ref[:, ln], f_ref[:, ln], i_ref[:, ln], l0_ref[:, ln], l1_ref[:, ln]]
    return flat


def _hg_post(o, gr, nw):
    on = o * lax.rsqrt(jnp.mean(o * o, axis=-1, keepdims=True) + NORM_EPS)
    return on * nw * (gr * _sigmoid(gr))


def _hg_specs(T, tb, rev):
    nT = T // tb
    tix = (lambda t: nT - 1 - t) if rev else (lambda t: t)
    col = lambda blk: pl.BlockSpec((tb, HG_WIDTH), lambda t: (tix(t), blk))
    vec = pl.BlockSpec((1, HG_WIDTH), lambda t: (0, 0))
    st = pl.BlockSpec((HG_HEADS, tb // HG_CHUNK, HG_HEAD_DIM, HG_HEAD_DIM), lambda t: (0, tix(t), 0, 0))
    return nT, col, vec, st


def _hg_fwd(proj, l0, l1, nw, name):
    T = proj.shape[0]
    tb = _row_tile(T, SCAN_ROWS)
    nsub = tb // HG_CHUNK
    nT, col, vec, st = _hg_specs(T, tb, False)

    def body(q_ref, f_ref, i_ref, g_ref, l0_ref, l1_ref, nw_ref, o_ref, st_ref, s_ref, qe_ref, kd_ref, dec_ref):
        @pl.when(pl.program_id(0) == 0)
        def _():
            s_ref[...] = jnp.zeros_like(s_ref)

        consts = _hg_consts(tb)
        outs = _hg_prep(consts, *_hg_prep_args(q_ref, f_ref, i_ref, l0_ref, l1_ref))
        for h, (qe, o_intra, kd, dec) in enumerate(outs):
            qe_ref[h], kd_ref[h], dec_ref[h] = qe, kd, dec
            o_ref[:, _head_lanes(h)] = o_intra

        def step(c, carry):
            rows = _chunk_rows(c, HG_CHUNK)
            for h in range(HG_HEADS):
                ln = _head_lanes(h)
                S = s_ref[h]
                st_ref[h, c] = S
                o_ref[rows, ln] += _dot(qe_ref[h, rows, :], S, NT)
                s_ref[h] = S * dec_ref[h, pl.ds(c * HG_CHUNK, 1), :] + _dot(i_ref[rows, ln], kd_ref[h, rows, :], TN)
            return carry

        lax.fori_loop(0, nsub, step, 0)
        for h in range(HG_HEADS):
            ln = _head_lanes(h)
            o_ref[:, ln] = _hg_post(o_ref[:, ln], g_ref[:, ln], nw_ref[:, ln])

    blk = pltpu.VMEM((HG_HEADS, tb, LANES), F32)
    return pl.pallas_call(
        body, name=name, grid=(nT,),
        in_specs=[col(0), col(1), col(2), col(3), vec, vec, vec],
        out_specs=[col(0), st],
        out_shape=[jax.ShapeDtypeStruct((T, HG_WIDTH), F32),
                   jax.ShapeDtypeStruct((HG_HEADS, T // HG_CHUNK, HG_HEAD_DIM, HG_HEAD_DIM), F32)],
        scratch_shapes=[pltpu.VMEM((HG_HEADS, HG_HEAD_DIM, HG_HEAD_DIM), F32), blk, blk, blk],
    )(proj, proj, proj, proj, l0, l1, nw)


def _hg_bwd(proj, states, do, do_blk, l0, l1, nw, name):
    T = proj.shape[0]
    tb = _row_tile(T, SCAN_ROWS)
    nsub = tb // HG_CHUNK
    nT, col, vec, st = _hg_specs(T, tb, True)

    def body(q_ref, f_ref, i_ref, g_ref, st_ref, do_ref, l0_ref, l1_ref, nw_ref,
             dq_ref, df_ref, di_ref, dg_ref, dl0_ref, dl1_ref, dnw_ref,
             ds_ref, qe_ref, kd_ref, dec_ref, o_ref, dqe_ref, dkd_ref, ddec_ref, dis_ref):
        @pl.when(pl.program_id(0) == 0)
        def _():
            ds_ref[...] = jnp.zeros_like(ds_ref)
            dl0_ref[...] = jnp.zeros_like(dl0_ref)
            dl1_ref[...] = jnp.zeros_like(dl1_ref)
            dnw_ref[...] = jnp.zeros_like(dnw_ref)

        consts = _hg_consts(tb)
        outs, prep_vjp = jax.vjp(functools.partial(_hg_prep, consts),
                                 *_hg_prep_args(q_ref, f_ref, i_ref, l0_ref, l1_ref))
        for h, (qe, o_intra, kd, dec) in enumerate(outs):
            qe_ref[h], kd_ref[h], dec_ref[h], o_ref[h] = qe, kd, dec, o_intra

        def redo(c, carry):
            rows = _chunk_rows(c, HG_CHUNK)
            for h in range(HG_HEADS):
                o_ref[h, rows, :] += _dot(qe_ref[h, rows, :], st_ref[h, c], NT)
            return carry

        lax.fori_loop(0, nsub, redo, 0)
        for h in range(HG_HEADS):
            ln = _head_lanes(h)
            _, vjp = jax.vjp(_hg_post, o_ref[h], g_ref[:, ln], nw_ref[:, ln])
            d_o, dgr, dnw = vjp(do_ref[:, ln])
            o_ref[h] = d_o
            dg_ref[:, ln] = dgr.astype(dg_ref.dtype)
            dnw_ref[:, ln] += dnw
        ddec_ref[...] = jnp.zeros_like(ddec_ref)

        def step(i, carry):
            c = nsub - 1 - i
            rows = _chunk_rows(c, HG_CHUNK)
            row0 = pl.ds(c * HG_CHUNK, 1)
            for h in range(HG_HEADS):
                ln = _head_lanes(h)
                G = ds_ref[h]
                S = st_ref[h, c]
                d_o = o_ref[h, rows, :]
                dqe_ref[h, rows, :] = _dot(d_o, S)
                dkd_ref[h, rows, :] = _dot(i_ref[rows, ln], G)
                dis_ref[h, rows, :] = _dot(kd_ref[h, rows, :], G, NT)
                ddec_ref[h, row0, :] = jnp.sum(S * G, axis=0, keepdims=True)
                ds_ref[h] = G * dec_ref[h, row0, :] + _dot(d_o, qe_ref[h, rows, :], TN)
            return carry

        lax.fori_loop(0, nsub, step, 0)
        grads = prep_vjp(tuple((dqe_ref[h], o_ref[h], dkd_ref[h], ddec_ref[h]) for h in range(HG_HEADS)))
        for h in range(HG_HEADS):
            ln = _head_lanes(h)
            dq, df, di, dl0, dl1 = grads[N_HG_IN * h:N_HG_IN * (h + 1)]
            dq_ref[:, ln] = dq.astype(dq_ref.dtype)
            df_ref[:, ln] = df.astype(df_ref.dtype)
            di_ref[:, ln] = (di + dis_ref[h]).astype(di_ref.dtype)
            dl0_ref[:, ln] += dl0
            dl1_ref[:, ln] += dl1

    dcol = jax.ShapeDtypeStruct((T, HG_WIDTH), BF16)
    dvec = jax.ShapeDtypeStruct((1, HG_WIDTH), F32)
    blk = pltpu.VMEM((HG_HEADS, tb, LANES), F32)
    return pl.pallas_call(
        body, name=name, grid=(nT,),
        in_specs=[col(0), col(1), col(2), col(3), st, col(do_blk), vec, vec, vec],
        out_specs=[col(0)] * 4 + [vec] * 3,
        out_shape=[dcol] * 4 + [dvec] * 3,
        scratch_shapes=[pltpu.VMEM((HG_HEADS, HG_HEAD_DIM, HG_HEAD_DIM), F32)] + [blk] * 8,
    )(proj, proj, proj, proj, states, do, l0, l1, nw)


def _rw_consts(rows):
    same, tril, stril = _chunk_masks(rows, RW_CHUNK)
    br, bc = _iota2((LANES, LANES), 0), _iota2((LANES, LANES), 1)
    blockdiag = ((br < RW_HEAD_DIM) == (bc < RW_HEAD_DIM)).astype(F32)
    m0 = (_iota2((1, LANES), 1) < RW_HEAD_DIM).astype(F32)
    return same, tril, stril, blockdiag, m0, 1.0 - m0


def _unit_lower_inverses_impl(lows):
    rows = lows[0].shape[0]
    eye = (_iota2(lows[0].shape, 0) == _iota2(lows[0].shape, 1)).astype(F32)
    xs = [low + eye for low in lows]
    ps = [_dot(low, low) for low in lows]
    n = 4
    while n < RW_CHUNK:
        zs = [_dot(jnp.concatenate([p, x], axis=0), p) for p, x in zip(ps, xs)]
        ps = [z[:rows] for z in zs]
        xs = [x + z[rows:] for x, z in zip(xs, zs)]
        n *= 2
    return tuple(x + _dot(x, p) for x, p in zip(xs, ps))


@jax.custom_vjp
def _unit_lower_inverses(lows):
    return _unit_lower_inverses_impl(lows)


def _unit_lower_inverses_fwd(lows):
    xs = _unit_lower_inverses_impl(lows)
    return xs, xs


def _unit_lower_inverses_bwd(xs, dxs):
    ts = [_dot(x, dx, TN) for x, dx in zip(xs, dxs)]
    return (tuple(_dot(t, x, NT) for t, x in zip(ts, xs)),)


_unit_lower_inverses.defvjp(_unit_lower_inverses_fwd, _unit_lower_inverses_bwd)


N_PREP_IN = 12
N_PREP_OUT = 9
RW_GROUP = 2


def _rw_prep(consts, *flat):
    same, tril, stril, blockdiag, m0, m1 = consts
    rows = tril.shape[0]
    masks = (m0, m1)
    pre = []
    for i in range(len(flat) // N_PREP_IN):
        r, kx, v, lw, gd, w0, a0, k_k, k_a, w2p, a2p, g2 = flat[N_PREP_IN * i:N_PREP_IN * (i + 1)]
        xw = w0 + _dot(jnp.tanh(lw), w2p)
        w = jnp.minimum(xw, 0.0) - jnp.log(1.0 + jnp.exp(-jnp.abs(xw))) - 0.5
        ld = -jnp.exp(w)
        a_s = _sigmoid(a0 + _dot(lw, a2p))
        g = _dot(_sigmoid(gd), g2)
        kk = kx * k_k
        kk = kk / jnp.maximum(jnp.sqrt(_dot(kk * kk, blockdiag)), L2_EPS)
        k2 = kx * (1.0 + (a_s - 1.0) * k_a)
        bv = kk * a_s
        acc = _dot(jnp.concatenate([tril, same], axis=0), ld, NN, HIGHEST)
        cum, tot = acc[:rows], acc[rows:]
        ecn = jnp.exp(-cum)
        a_t = -kk * jnp.exp(cum - ld)
        r_t = r * jnp.exp(cum)
        rem = jnp.exp(tot - cum)
        z = _dot(jnp.concatenate([a_t * m0, a_t * m1, r_t * m0, r_t * m1], axis=0),
                 jnp.concatenate([bv * ecn, k2 * ecn], axis=0), NT)
        pre.append((v, a_t, r_t, z, (bv * rem, k2 * rem, jnp.exp(tot), k2, g)))
    heads = [(i, h) for i in range(len(pre)) for h in range(2)]
    za = {ih: pre[ih[0]][3][ih[1] * rows:(ih[1] + 1) * rows] for ih in heads}
    zr = {ih: pre[ih[0]][3][(2 + ih[1]) * rows:(3 + ih[1]) * rows] for ih in heads}
    tinv = dict(zip(heads, _unit_lower_inverses(tuple(za[ih][:, :rows] * stril for ih in heads))))
    lv = {ih: _dot(jnp.concatenate([za[ih][:, rows:] * stril, zr[ih][:, rows:] * tril], axis=0), pre[ih[0]][0])
          for ih in heads}
    wu = {ih: _dot(tinv[ih], jnp.concatenate([pre[ih[0]][1] * masks[ih[1]], lv[ih][:rows]], axis=1)) for ih in heads}
    w_m = {ih: wu[ih][:, :LANES] for ih in heads}
    u_m = {ih: masks[ih[1]] * wu[ih][:, LANES:] for ih in heads}
    qy = {ih: _dot(zr[ih][:, :rows] * tril, jnp.concatenate([w_m[ih], u_m[ih]], axis=1)) for ih in heads}
    outs = []
    for i in range(len(pre)):
        a, b = (i, 0), (i, 1)
        W = w_m[a] + w_m[b]
        U = u_m[a] + u_m[b]
        Q = pre[i][2] + qy[a][:, :LANES] + qy[b][:, :LANES]
        Y0 = qy[a][:, LANES:] + qy[b][:, LANES:] + m0 * lv[a][rows:] + m1 * lv[b][rows:]
        outs.append((W, U, Q, Y0) + pre[i][4])
    return tuple(outs)


def _rw_post(blockdiag, y, r, v, k2, g, r_k, ln_w, ln_b):
    inv_n = 1.0 / RW_HEAD_DIM
    yc = y - _dot(y, blockdiag) * inv_n
    var = _dot(yc * yc, blockdiag) * inv_n
    yn = yc * lax.rsqrt(var + RW_GN_EPS) * ln_w + ln_b
    bonus = _dot(r * k2 * r_k, blockdiag) * v
    return (yn + bonus) * g


N_RW_VEC = 7
N_RW_MAT = 3


def _rw_specs(T, tb, rev):
    nT = T // tb
    tix = (lambda t: nT - 1 - t) if rev else (lambda t: t)
    wide = lambda blk: pl.BlockSpec((tb, RW_WIDTH), lambda t: (tix(t), blk))
    narrow = lambda blk: pl.BlockSpec((tb, LANES), lambda t: (tix(t), blk))
    vec = pl.BlockSpec((1, RW_WIDTH), lambda t: (0, 0))
    mat = pl.BlockSpec((RW_PAIRS, LANES, LANES), lambda t: (0, 0, 0))
    st = pl.BlockSpec((RW_PAIRS, tb // RW_CHUNK, LANES, LANES), lambda t: (0, tix(t), 0, 0))
    lora0 = 3 * RW_WIDTH // LANES
    ins = [wide(0), wide(1), wide(2), narrow(lora0), narrow(lora0 + 1)]
    return nT, wide, vec, mat, st, ins


def _rw_prep_args(p, r_ref, k_ref, v_ref, lw_ref, gd_ref, vrefs, mrefs):
    ln = _head_lanes(p)
    w0, a0, k_k, k_a = [x[:, ln] for x in vrefs[:4]]
    return (r_ref[:, ln], k_ref[:, ln], v_ref[:, ln], lw_ref[...], gd_ref[...], w0, a0, k_k, k_a,
            *[x[p] for x in mrefs])


def _stack_chunks(ref, top, bottom):
    C = RW_CHUNK
    for c in range(ref.shape[0]):
        ref[c, 0:C, :] = top[c * C:(c + 1) * C]
        ref[c, C:2 * C, :] = bottom[c * C:(c + 1) * C]


def _group_args(p0, r_ref, k_ref, v_ref, lw_ref, gd_ref, vrefs, mrefs):
    flat = []
    for p in range(p0, p0 + RW_GROUP):
        flat += list(_rw_prep_args(p, r_ref, k_ref, v_ref, lw_ref, gd_ref, vrefs, mrefs))
    return flat


def _rw_fwd(rws, vecs, mats, name):
    T = rws.shape[0]
    tb = _row_tile(T, SCAN_ROWS)
    nsub = tb // RW_CHUNK
    C = RW_CHUNK
    nT, wide, vec, mat, st, ins = _rw_specs(T, tb, False)

    def body(*refs):
        r_ref, k_ref, v_ref, lw_ref, gd_ref = refs[:5]
        vrefs = refs[5:5 + N_RW_VEC]
        mrefs = refs[5 + N_RW_VEC:5 + N_RW_VEC + N_RW_MAT]
        o_ref, st_ref, s_ref, wq_ref, uy_ref, bk_ref, misc_ref, y_ref = refs[-8:]

        @pl.when(pl.program_id(0) == 0)
        def _():
            s_ref[...] = jnp.zeros_like(s_ref)

        consts = _rw_consts(tb)
        blockdiag = consts[3]
        for p0 in range(0, RW_PAIRS, RW_GROUP):
            outs = _rw_prep(consts, *_group_args(p0, r_ref, k_ref, v_ref, lw_ref, gd_ref, vrefs, mrefs))
            for p, (W, U, Q, Y0, Bg, Kg, dec, k2, g) in zip(range(p0, p0 + RW_GROUP), outs):
                _stack_chunks(wq_ref.at[p], W, Q)
                _stack_chunks(uy_ref.at[p], U, Y0)
                _stack_chunks(bk_ref.at[p], Bg, Kg)
                misc_ref[0, p], misc_ref[1, p], misc_ref[2, p] = dec, k2, g

        def step(c, carry):
            rows = _chunk_rows(c, C)
            for p in range(RW_PAIRS):
                S = s_ref[p]
                st_ref[p, c] = S
                py = _dot(wq_ref[p, c], S, NT) + uy_ref[p, c]
                y_ref[p, rows, :] = py[C:]
                pv = jnp.concatenate([py[:C], v_ref[rows, _head_lanes(p)]], axis=0)
                s_ref[p] = (S * misc_ref[0, p, pl.ds(c * C, 1), :] + _dot(pv, bk_ref[p, c], TN)) * blockdiag
            return carry

        lax.fori_loop(0, nsub, step, 0)
        for p in range(RW_PAIRS):
            ln = _head_lanes(p)
            r_k, ln_w, ln_b = [x[:, ln] for x in vrefs[4:]]
            o_ref[:, ln] = _rw_post(blockdiag, y_ref[p], r_ref[:, ln], v_ref[:, ln], misc_ref[1, p], misc_ref[2, p],
                                    r_k, ln_w, ln_b)

    stacked = pltpu.VMEM((RW_PAIRS, nsub, 2 * C, LANES), F32)
    return pl.pallas_call(
        body, name=name, grid=(nT,),
        in_specs=ins + [vec] * N_RW_VEC + [mat] * N_RW_MAT,
        out_specs=[wide(0), st],
        out_shape=[jax.ShapeDtypeStruct((T, RW_WIDTH), F32),
                   jax.ShapeDtypeStruct((RW_PAIRS, T // RW_CHUNK, LANES, LANES), F32)],
        scratch_shapes=[pltpu.VMEM((RW_PAIRS, LANES, LANES), F32), stacked, stacked, stacked,
                        pltpu.VMEM((3, RW_PAIRS, tb, LANES), F32), pltpu.VMEM((RW_PAIRS, tb, LANES), F32)],
    )(rws, rws, rws, rws, rws, *vecs, *mats)


def _rw_bwd(rws, states, do, do_blk, vecs, mats, name):
    T = rws.shape[0]
    tb = _row_tile(T, SCAN_ROWS)
    nsub = tb // RW_CHUNK
    C = RW_CHUNK
    G = RW_GROUP
    nT, wide, vec, mat, st, ins = _rw_specs(T, tb, True)
    nin = 5 + 1 + 1 + N_RW_VEC + N_RW_MAT

    def body(*refs):
        r_ref, k_ref, v_ref, lw_ref, gd_ref = refs[:5]
        st_ref, do_ref = refs[5], refs[6]
        vrefs = refs[7:7 + N_RW_VEC]
        mrefs = refs[7 + N_RW_VEC:nin]
        dr_ref, dk_ref, dv_ref, dlo_ref = refs[nin:nin + 4]
        dvec = refs[nin + 4:nin + 4 + N_RW_VEC]
        dmat = refs[nin + 4 + N_RW_VEC:nin + 4 + N_RW_VEC + N_RW_MAT]
        ds_ref, wq_ref, uy_ref, bk_ref, pv_ref, dec_ref, y_ref, dpre_ref, dvs_ref = refs[-9:]

        @pl.when(pl.program_id(0) == 0)
        def _():
            ds_ref[...] = jnp.zeros_like(ds_ref)
            for x in dvec + dmat:
                x[...] = jnp.zeros_like(x)

        consts = _rw_consts(tb)
        blockdiag = consts[3]
        dlw, dgd = 0.0, 0.0
        for p0 in range(0, RW_PAIRS, G):
            outs, prep_vjp = jax.vjp(functools.partial(_rw_prep, consts),
                                     *_group_args(p0, r_ref, k_ref, v_ref, lw_ref, gd_ref, vrefs, mrefs))
            for q, (W, U, Q, Y0, Bg, Kg, dec, _, _) in enumerate(outs):
                _stack_chunks(wq_ref.at[q], W, Q)
                _stack_chunks(uy_ref.at[q], U, Y0)
                _stack_chunks(bk_ref.at[q], Bg, Kg)
                dec_ref[q] = dec

            def redo(c, carry, p0=p0):
                rows = _chunk_rows(c, C)
                for q in range(G):
                    py = _dot(wq_ref[q, c], st_ref[p0 + q, c], NT) + uy_ref[q, c]
                    y_ref[q, rows, :] = py[C:]
                    pv_ref[q, c, 0:C, :] = py[:C]
                    pv_ref[q, c, C:2 * C, :] = v_ref[rows, _head_lanes(p0 + q)]
                return carry

            lax.fori_loop(0, nsub, redo, 0)
            post = []
            for q in range(G):
                ln = _head_lanes(p0 + q)
                r_k, ln_w, ln_b = [x[:, ln] for x in vrefs[4:]]
                _, post_vjp = jax.vjp(functools.partial(_rw_post, blockdiag), y_ref[q], r_ref[:, ln], v_ref[:, ln],
                                      outs[q][7], outs[q][8], r_k, ln_w, ln_b)
                dy, dr2, dv2, dk2, dg, dr_k, dln_w, dln_b = post_vjp(do_ref[:, ln])
                dpre_ref[q, 3] = dy
                dvs_ref[q] = dv2
                for x, gx in zip(dvec[4:], (dr_k, dln_w, dln_b)):
                    x[:, ln] += gx
                dpre_ref[q, 6] = jnp.zeros_like(dpre_ref[q, 6])
                post.append((dr2, dk2, dg))

            def step(i, carry, p0=p0):
                c = nsub - 1 - i
                rows = _chunk_rows(c, C)
                row0 = pl.ds(c * C, 1)
                for q in range(G):
                    Gs = ds_ref[p0 + q] * blockdiag
                    S = st_ref[p0 + q, c]
                    t1 = _dot(bk_ref[q, c], Gs, NT)
                    dpy = jnp.concatenate([t1[:C], dpre_ref[q, 3, rows, :]], axis=0)
                    t2 = _dot(dpy, S)
                    t3 = _dot(pv_ref[q, c], Gs)
                    dvs_ref[q, rows, :] += t1[C:]
                    dpre_ref[q, 0, rows, :] = t2[:C]
                    dpre_ref[q, 1, rows, :] = t1[:C]
                    dpre_ref[q, 2, rows, :] = t2[C:]
                    dpre_ref[q, 4, rows, :] = t3[:C]
                    dpre_ref[q, 5, rows, :] = t3[C:]
                    dpre_ref[q, 6, row0, :] = jnp.sum(S * Gs, axis=0, keepdims=True)
                    ds_ref[p0 + q] = Gs * dec_ref[q, row0, :] + _dot(dpy, wq_ref[q, c], TN)
                return carry

            lax.fori_loop(0, nsub, step, 0)
            grads = prep_vjp(tuple(tuple(dpre_ref[q, i] for i in range(7)) + post[q][1:] for q in range(G)))
            for q in range(G):
                ln = _head_lanes(p0 + q)
                gq = grads[N_PREP_IN * q:N_PREP_IN * (q + 1)]
                dr_ref[:, ln] = gq[0] + post[q][0]
                dk_ref[:, ln] = gq[1]
                dv_ref[:, ln] = gq[2] + dvs_ref[q]
                dlw = dlw + gq[3]
                dgd = dgd + gq[4]
                for x, gx in zip(dvec[:4], gq[5:9]):
                    x[:, ln] += gx
                for x, gx in zip(dmat, gq[9:]):
                    x[p0 + q] += gx
        dlo_ref[:, 0:LANES] = dlw
        dlo_ref[:, LANES:2 * LANES] = dgd

    dcol = jax.ShapeDtypeStruct((T, RW_WIDTH), F32)
    dlo_spec = pl.BlockSpec((tb, 2 * LANES), lambda t: (nT - 1 - t, 0))
    blk = pltpu.VMEM((G, tb, LANES), F32)
    stacked = pltpu.VMEM((G, nsub, 2 * C, LANES), F32)
    return pl.pallas_call(
        body, name=name, grid=(nT,),
        in_specs=ins + [st, wide(do_blk)] + [vec] * N_RW_VEC + [mat] * N_RW_MAT,
        out_specs=[wide(0)] * 3 + [dlo_spec] + [vec] * N_RW_VEC + [mat] * N_RW_MAT,
        out_shape=[dcol] * 3 + [jax.ShapeDtypeStruct((T, 2 * LANES), F32)]
        + [jax.ShapeDtypeStruct((1, RW_WIDTH), F32)] * N_RW_VEC
        + [jax.ShapeDtypeStruct((RW_PAIRS, LANES, LANES), F32)] * N_RW_MAT,
        scratch_shapes=[pltpu.VMEM((RW_PAIRS, LANES, LANES), F32), stacked, stacked, stacked, stacked, blk, blk,
                        pltpu.VMEM((G, 7, tb, LANES), F32), blk],
    )(rws, rws, rws, rws, rws, states, do, *vecs, *mats)


def _my_index():
    return 4 * lax.axis_index("x") + 2 * lax.axis_index("y") + lax.axis_index("c")


def _peer(bits):
    pos = []
    for name, flip in zip(("x", "y", "c"), bits):
        i = lax.axis_index(name)
        pos.append(1 - i if flip else i)
    return tuple(pos)


def _peer_index(bits):
    x, y, c = _peer(bits)
    return 4 * x + 2 * y + c


def _all_gather(shards, name):
    n = len(shards)
    chips = [(1, 0, 0), (0, 1, 0), (1, 1, 0)]
    sib = (0, 0, 1)

    def body(*refs):
        ins, outs = refs[:n], refs[n:2 * n]
        send_sems, recv_sems, local_sems = refs[2 * n:]

        def rows(k, dev):
            r = ins[k].shape[0]
            return outs[k].at[pl.ds(dev * r, r), :]

        def copy(k, slot, block_dev, to_bits, src=None):
            return pltpu.make_async_remote_copy(
                src_ref=rows(k, block_dev) if src is None else src, dst_ref=rows(k, block_dev),
                send_sem=send_sems.at[k, slot], recv_sem=recv_sems.at[k, slot],
                device_id=_peer(to_bits), device_id_type=MESH_ID)

        me = _my_index()
        started = []
        for k in range(n):
            mine = pltpu.make_async_copy(ins[k], rows(k, me), local_sems.at[k])
            mine.start()
            started.append(mine)
        sends = []
        for k in range(n):
            first = [copy(k, 0, me, sib, src=ins[k])]
            first += [copy(k, 1 + j, me, chip, src=ins[k]) for j, chip in enumerate(chips)]
            for cp in first:
                cp.start()
            sends += first
        for k in range(n):
            for j, chip in enumerate(chips):
                copy(k, 1 + j, _peer_index(chip), chip).wait_recv()
                fwd = copy(k, 4 + j, _peer_index(chip), sib)
                fwd.start()
                sends.append(fwd)
        for k in range(n):
            copy(k, 0, _peer_index(sib), sib).wait_recv()
            for j, chip in enumerate(chips):
                both = (chip[0], chip[1], 1)
                copy(k, 4 + j, _peer_index(both), sib).wait_recv()
        for cp in sends:
            cp.wait_send()
        for cp in started:
            cp.wait()

    any_spec = pl.BlockSpec(memory_space=pl.ANY)
    return pl.pallas_call(
        body, name=name,
        in_specs=[any_spec] * n, out_specs=[any_spec] * n,
        out_shape=[jax.ShapeDtypeStruct((N_DEV * s.shape[0], s.shape[1]), s.dtype) for s in shards],
        scratch_shapes=[pltpu.SemaphoreType.DMA((n, 7)), pltpu.SemaphoreType.DMA((n, 7)),
                        pltpu.SemaphoreType.DMA((n,))],
    )(*shards)


def _exchange(partials, name):
    n = len(partials)
    flips = [(dx, dy, dc) for dx in (0, 1) for dy in (0, 1) for dc in (0, 1)][1:]

    def body(*refs):
        ins, outs = refs[:n], refs[n:2 * n]
        send_sems, recv_sems, local_sems = refs[2 * n:]
        me = _my_index()
        local = []
        for k in range(n):
            cp = pltpu.make_async_copy(ins[k].at[me], outs[k].at[me], local_sems.at[k])
            cp.start()
            local.append(cp)
        copies = []
        for k in range(n):
            for d, bits in enumerate(flips):
                cp = pltpu.make_async_remote_copy(
                    src_ref=ins[k].at[_peer_index(bits)], dst_ref=outs[k].at[me],
                    send_sem=send_sems.at[k, d], recv_sem=recv_sems.at[k, d],
                    device_id=_peer(bits), device_id_type=MESH_ID)
                cp.start()
                copies.append(cp)
        for cp in copies:
            cp.wait_recv()
        for cp in copies:
            cp.wait_send()
        for cp in local:
            cp.wait()

    any_spec = pl.BlockSpec(memory_space=pl.ANY)
    return pl.pallas_call(
        body, name=name,
        in_specs=[any_spec] * n, out_specs=[any_spec] * n,
        out_shape=[jax.ShapeDtypeStruct(p.shape, p.dtype) for p in partials],
        scratch_shapes=[pltpu.SemaphoreType.DMA((n, 7)), pltpu.SemaphoreType.DMA((n, 7)),
                        pltpu.SemaphoreType.DMA((n,))],
    )(*partials)


HBM_SPEC = pl.BlockSpec(memory_space=pltpu.HBM)
SEM_SPEC = pl.BlockSpec(memory_space=pltpu.SEMAPHORE)
ALL_FLIPS = [(dx, dy, dc) for dx in (0, 1) for dy in (0, 1) for dc in (0, 1)][1:]


def _spread_copies(srcs, lands, send_sems, recv_sems, per_peer_source):
    me = _my_index()
    copies = []
    for k, land in enumerate(lands):
        for d, bits in enumerate(ALL_FLIPS):
            src = srcs[k].at[_peer_index(bits)] if per_peer_source else land.at[me]
            copies.append(pltpu.make_async_remote_copy(
                src_ref=src, dst_ref=land.at[me],
                send_sem=send_sems.at[k * 7 + d], recv_sem=recv_sems.at[k * 7 + d],
                device_id=_peer(bits), device_id_type=MESH_ID))
    return copies


def _spread_start(srcs, lands, name):
    ns, n = len(srcs), len(lands)

    def body(*refs):
        src_refs, land_refs = refs[:ns], refs[ns:ns + n]
        send_sems, recv_sems = refs[ns + n], refs[ns + n + 1]
        token = refs[-1]
        for cp in _spread_copies(src_refs, land_refs, send_sems, recv_sems, ns > 0):
            cp.start()
        token[...] = jnp.zeros_like(token)

    bufs = list(srcs) + list(lands)
    out = pl.pallas_call(
        body, name=name,
        out_shape=(pltpu.SemaphoreType.DMA((7 * n,)), pltpu.SemaphoreType.DMA((7 * n,)),
                   *[pltpu.HBM(b.shape, b.dtype) for b in bufs], jax.ShapeDtypeStruct((8, LANES), F32)),
        in_specs=[HBM_SPEC] * (ns + n),
        out_specs=(SEM_SPEC, SEM_SPEC, *[HBM_SPEC] * (ns + n), pl.BlockSpec(memory_space=pltpu.VMEM)),
        input_output_aliases={i: 2 + i for i in range(ns + n)},
        compiler_params=pltpu.CompilerParams(has_side_effects=pltpu.SideEffectType.DATAFLOW_SIDE_EFFECTING),
    )(*[pltpu.with_memory_space_constraint(b, pltpu.HBM) for b in bufs])
    return out[0], out[1], list(out[2:2 + ns]), list(out[2 + ns:2 + ns + n]), out[-1]


def _spread_wait(send_sems, recv_sems, srcs, lands, after, name):
    ns, n = len(srcs), len(lands)

    def body(*refs):
        src_refs, land_refs = refs[:ns], refs[ns:ns + n]
        send_sems, recv_sems = refs[ns + n], refs[ns + n + 1]
        for cp in _spread_copies(src_refs, land_refs, send_sems, recv_sems, ns > 0):
            cp.wait_send()
            cp.wait_recv()

    bufs = list(srcs) + list(lands)
    out = pl.pallas_call(
        body, name=name,
        out_shape=tuple(pltpu.HBM(b.shape, b.dtype) for b in bufs),
        in_specs=[HBM_SPEC] * (ns + n) + [SEM_SPEC, SEM_SPEC, pl.BlockSpec(memory_space=pl.ANY)],
        out_specs=tuple([HBM_SPEC] * (ns + n)),
        input_output_aliases={i: i for i in range(ns + n)},
        compiler_params=pltpu.CompilerParams(has_side_effects=pltpu.SideEffectType.DATAFLOW_SIDE_EFFECTING),
    )(*bufs, send_sems, recv_sems, after)
    return list(out[ns:])


def _own_slot_only(block, me):
    return lax.dynamic_update_slice(lax.empty((N_DEV,) + block.shape, block.dtype), block[None], (me, 0, 0))


def _sum_slots(landed, name):
    _, R, C = landed.shape
    tb = _row_tile(R, 128)

    def body(l_ref, o_ref):
        acc = l_ref[0].astype(F32)
        for s in range(1, N_DEV):
            acc = acc + l_ref[s].astype(F32)
        o_ref[...] = acc

    return pl.pallas_call(
        body, name=name, grid=(R // tb,),
        in_specs=[pl.BlockSpec((N_DEV, tb, C), lambda i: (0, i, 0))],
        out_specs=pl.BlockSpec((tb, C), lambda i: (i, 0)),
        out_shape=jax.ShapeDtypeStruct((R, C), F32),
    )(landed)


def _pack_rows(flat_list, width=LANES):
    flat = jnp.concatenate([a.reshape(-1) for a in flat_list])
    n = flat.shape[0]
    rows = -(-n // width)
    rows = -(-rows // 8) * 8
    return jnp.pad(flat, (0, rows * width - n)).reshape(rows, width)


def _unpack(packed, shapes):
    flat = packed.reshape(-1)
    out, off = [], 0
    for s in shapes:
        n = 1
        for d in s:
            n *= d
        out.append(flat[off:off + n].reshape(s))
        off += n
    return out


def kernel(x, norm1_w, w_in, hg_lb_logits, hg_norm_w, rw_shift_mu, rw_w0, rw_w2, rw_a0, rw_a2, rw_g2, rw_k_k, rw_k_a, rw_r_k, rw_ln_w, rw_ln_b, w_out, norm2_w, w_up, conv_w, conv_b, w_down, final_norm_w, loss_target, m_norm1_w, m_w_in, m_hg_lb_logits, m_hg_norm_w, m_rw_shift_mu, m_rw_w0, m_rw_w2, m_rw_a0, m_rw_a2, m_rw_g2, m_rw_k_k, m_rw_k_a, m_rw_r_k, m_rw_ln_w, m_rw_ln_b, m_w_out, m_norm2_w, m_w_up, m_conv_w, m_conv_b, m_w_down, m_final_norm_w, v_norm1_w, v_w_in, v_hg_lb_logits, v_hg_norm_w, v_rw_shift_mu, v_rw_w0, v_rw_w2, v_rw_a0, v_rw_a2, v_rw_g2, v_rw_k_k, v_rw_k_a, v_rw_r_k, v_rw_ln_w, v_rw_ln_b, v_w_out, v_norm2_w, v_w_up, v_conv_w, v_conv_b, v_w_down, v_final_norm_w):
    weights = dict(norm1_w=norm1_w, w_in=w_in, hg_lb_logits=hg_lb_logits, hg_norm_w=hg_norm_w,
                   rw_shift_mu=rw_shift_mu, rw_w0=rw_w0, rw_w2=rw_w2, rw_a0=rw_a0, rw_a2=rw_a2, rw_g2=rw_g2,
                   rw_k_k=rw_k_k, rw_k_a=rw_k_a, rw_r_k=rw_r_k, rw_ln_w=rw_ln_w, rw_ln_b=rw_ln_b, w_out=w_out,
                   norm2_w=norm2_w, w_up=w_up, conv_w=conv_w, conv_b=conv_b, w_down=w_down,
                   final_norm_w=final_norm_w)
    m_in = dict(norm1_w=m_norm1_w, w_in=m_w_in, hg_lb_logits=m_hg_lb_logits, hg_norm_w=m_hg_norm_w,
                rw_shift_mu=m_rw_shift_mu, rw_w0=m_rw_w0, rw_w2=m_rw_w2, rw_a0=m_rw_a0, rw_a2=m_rw_a2,
                rw_g2=m_rw_g2, rw_k_k=m_rw_k_k, rw_k_a=m_rw_k_a, rw_r_k=m_rw_r_k, rw_ln_w=m_rw_ln_w,
                rw_ln_b=m_rw_ln_b, w_out=m_w_out, norm2_w=m_norm2_w, w_up=m_w_up, conv_w=m_conv_w,
                conv_b=m_conv_b, w_down=m_w_down, final_norm_w=m_final_norm_w)
    v_in = dict(norm1_w=v_norm1_w, w_in=v_w_in, hg_lb_logits=v_hg_lb_logits, hg_norm_w=v_hg_norm_w,
                rw_shift_mu=v_rw_shift_mu, rw_w0=v_rw_w0, rw_w2=v_rw_w2, rw_a0=v_rw_a0, rw_a2=v_rw_a2,
                rw_g2=v_rw_g2, rw_k_k=v_rw_k_k, rw_k_a=v_rw_k_a, rw_r_k=v_rw_r_k, rw_ln_w=v_rw_ln_w,
                rw_ln_b=v_rw_ln_b, w_out=v_w_out, norm2_w=v_norm2_w, w_up=v_w_up, conv_w=v_conv_w,
                conv_b=v_conv_b, w_down=v_w_down, final_norm_w=v_final_norm_w)
    names = list(weights)
    sharded_small = ["rw_w2", "rw_a2", "rw_g2", "conv_w"]
    sharded_big = ["w_in", "w_out", "w_up", "w_down"]
    replicated = [n for n in names if n not in sharded_small + sharded_big]

    xs = x[0]
    tgt = loss_target[0]

    small_shard = _pack_rows([weights[n] for n in sharded_small])
    g_win_t, g_small = _all_gather([w_in[0].T.astype(BF16), small_shard], "gather_weights")
    me = _my_index()
    later = (w_up[0].T.astype(BF16), w_out[0].astype(BF16), w_down[0].astype(BF16))
    later, _ = lax.optimization_barrier((later, g_small))
    later = [_own_slot_only(z, me) for z in later]
    g_send, g_recv, _, later, g_token = _spread_start([], later, "gather_later_start")
    small_shapes = [weights[n].shape for n in sharded_small]
    per_dev = [_unpack(g_small.reshape(N_DEV, -1)[j], small_shapes) for j in range(N_DEV)]
    w2_full, a2_full, g2_full, convw_full = [jnp.concatenate([per_dev[j][i][0] for j in range(N_DEV)], axis=-1)
                                             for i in range(4)]
    zeros64 = jnp.zeros((RW_PAIRS, 64, LANES), F32)
    by_pair = lambda z: z.reshape(z.shape[0], RW_PAIRS, LANES).transpose(1, 0, 2)
    w2p = jnp.concatenate([by_pair(w2_full), zeros64], axis=1)
    a2p = jnp.concatenate([zeros64, by_pair(a2_full)], axis=1)
    g2p = by_pair(g2_full)

    l0, l1 = hg_lb_logits[0:1], hg_lb_logits[1:2]
    h1 = _rms_fwd(xs, norm1_w + g_token[0:1, 0:1], "norm1")
    proj = _mm_nt(h1, g_win_t, "proj_in")
    o_hg, hg_states = _hg_fwd(proj, l0, l1, hg_norm_w, "hgrn2_fwd")
    rws = _shift_fwd(proj, rw_shift_mu, "token_shift")
    rw_vecs = [rw_w0, rw_a0, rw_k_k, rw_k_a, rw_r_k, rw_ln_w, rw_ln_b]
    rw_mats = [w2p, a2p, g2p]
    o_rw, rw_states = _rw_fwd(rws, rw_vecs, rw_mats, "rwkv7_fwd")
    o_mix = jnp.concatenate([o_hg, o_rw], axis=-1).astype(BF16)
    g_wup_t, g_wout, g_wdown = [z.reshape(-1, z.shape[-1])
                                for z in _spread_wait(g_send, g_recv, [], later, o_mix, "gather_later_wait")]
    x1 = _mm_nn(o_mix, g_wout, xs, "proj_out")
    h2 = _rms_fwd(x1, norm2_w, "norm2")
    u = _mm_nt(h2, g_wup_t, "ffn_up")
    act = _ffn_act_fwd(u, convw_full, conv_b, "ffn_act")
    x2 = _mm_nn(act, g_wdown, x1, "ffn_down")
    loss_part, dx2, d_final_w = _loss_head(x2, final_norm_w.reshape(1, -1), tgt, "loss_head")

    d_wdown = _mm_tn(act, dx2, 1408, "ffn_down_dw", BF16)
    dact = _mm_nt(dx2, g_wdown, "ffn_down_dx", BF16)
    du_g, du_v, dcw_g, dcw_v, dcb_g, dcb_v = _ffn_act_bwd(u, dact, convw_full, conv_b, "ffn_act_bwd")
    d_convw = jnp.concatenate([dcw_g, dcw_v], axis=-1)
    d_convb = jnp.concatenate([dcb_g, dcb_v], axis=-1)
    d_wup_t = jnp.concatenate([_mm_tn(du_g, h2, 1408, "ffn_up_dw_gate", BF16),
                               _mm_tn(du_v, h2, 1408, "ffn_up_dw_value", BF16)], axis=0)
    dh2 = _mm_nn([du_g, du_v], g_wup_t, None, "ffn_up_dx")
    dx1, d_norm2 = _rms_bwd(dh2, x1, norm2_w, dx2, "norm2_bwd")
    d_wout = _mm_tn(o_mix, dx1, 512, "proj_out_dw", BF16)
    do = _mm_nt(dx1, g_wout, "proj_out_dx")
    early = [z.reshape(N_DEV, z.shape[0] // N_DEV, z.shape[1]) for z in (d_wup_t, d_wout, d_wdown)]
    early_land = [_own_slot_only(lax.dynamic_index_in_dim(z, me, 0, keepdims=False), me) for z in early]
    e_send, e_recv, early, early_land, e_token = _spread_start(early, early_land, "exchange_early_start")
    hg_norm_w_t = hg_norm_w + e_token[0:1, 0:1]
    dq, df, di, dg, d_l0, d_l1, d_hg_nw = _hg_bwd(proj, hg_states, do, 0, l0, l1, hg_norm_w_t, "hgrn2_bwd")
    rw_out = _rw_bwd(rws, rw_states, do, 1, rw_vecs, rw_mats, "rwkv7_bwd")
    d_rw_vecs = rw_out[4:4 + N_RW_VEC]
    d_w2p, d_a2p, d_g2p = rw_out[4 + N_RW_VEC:]
    dp_parts, dmu_parts = [], []
    for i, z in enumerate(rw_out[:4]):
        dp, dmu = _shift_bwd(z, proj, rw_shift_mu, i * RW_WIDTH, "token_shift_bwd_%d" % i)
        dp_parts.append(dp)
        dmu_parts.append(dmu)
    d_mu = jnp.concatenate(dmu_parts, axis=-1)
    dproj = jnp.concatenate([dq, df, di, dg] + dp_parts, axis=-1)
    d_win_t = _mm_tn(dproj, h1, 768, "proj_in_dw", BF16)
    from_pairs = lambda z: z.transpose(1, 0, 2).reshape(z.shape[1], RW_WIDTH)
    d_w2 = from_pairs(d_w2p[:, :64])
    d_a2 = from_pairs(d_a2p[:, 64:])
    d_g2 = from_pairs(d_g2p)
    col_blocks = lambda z: z.reshape(z.shape[0], N_DEV, -1).transpose(1, 0, 2)
    small_part = jnp.stack([
        _pack_rows([col_blocks(d_w2)[j], col_blocks(d_a2)[j], col_blocks(d_g2)[j], col_blocks(d_convw)[j]])
        for j in range(N_DEV)])
    late = [d_win_t.reshape(N_DEV, d_win_t.shape[0] // N_DEV, d_win_t.shape[1]), small_part]
    late_land = [_own_slot_only(lax.dynamic_index_in_dim(z, me, 0, keepdims=False), me) for z in late]
    l_send, l_recv, late, late_land, l_token = _spread_start(late, late_land, "exchange_late_start")
    dh1 = _mm_nn(dproj, g_win_t, None, "proj_in_dx")
    grad_x, d_norm1 = _rms_bwd(dh1, xs, norm1_w + l_token[0:1, 0:1], dx1, "norm1_bwd")

    rep_grads = dict(norm1_w=d_norm1, hg_lb_logits=jnp.concatenate([d_l0, d_l1], axis=0), hg_norm_w=d_hg_nw,
                     rw_shift_mu=d_mu, rw_w0=d_rw_vecs[0], rw_a0=d_rw_vecs[1], rw_k_k=d_rw_vecs[2],
                     rw_k_a=d_rw_vecs[3], rw_r_k=d_rw_vecs[4], rw_ln_w=d_rw_vecs[5], rw_ln_b=d_rw_vecs[6],
                     norm2_w=d_norm2, conv_b=d_convb, final_norm_w=d_final_w)
    rep_pack = _pack_rows([loss_part] + [rep_grads[n] for n in replicated])
    rep_part = jnp.broadcast_to(rep_pack[None], (N_DEV,) + rep_pack.shape)
    grads, delta, new_m, new_v = {}, {}, {}, {}

    def adamw_big(n, g):
        shp = weights[n].shape
        as2d = lambda z: z.reshape(shp[1], shp[2])
        grads[n] = g[None]
        d, nm, nv = _adamw(as2d(weights[n]), g, as2d(m_in[n]), as2d(v_in[n]), "adamw_" + n)
        delta[n], new_m[n], new_v[n] = d.reshape(shp), nm.reshape(shp), nv.reshape(shp)

    landed_early = _spread_wait(e_send, e_recv, early, early_land, grad_x, "exchange_early_wait")
    adamw_big("w_up", _sum_slots(landed_early[0], "sum_grads_w_up").T)
    adamw_big("w_out", _sum_slots(landed_early[1], "sum_grads_w_out"))
    adamw_big("w_down", _sum_slots(landed_early[2], "sum_grads_w_down"))
    landed_late = _spread_wait(l_send, l_recv, late, late_land, delta["w_down"], "exchange_late_wait")
    adamw_big("w_in", _sum_slots(landed_late[0], "sum_grads_w_in").T)
    (landed_rep,) = _exchange([rep_part], "exchange_grads")
    g_small_sum = _unpack(_sum_slots(landed_late[1], "sum_grads_small"), small_shapes)
    rep_sum = _unpack(_sum_slots(landed_rep, "sum_grads_replicated"), [(1, 1)] + [weights[n].shape for n in replicated])
    loss = rep_sum[0].reshape(())
    grads.update(dict(zip(replicated, rep_sum[1:])))
    grads.update(dict(zip(sharded_small, g_small_sum)))

    small_names = replicated + sharded_small
    packs = [_pack_rows([src[n] for n in small_names]) for src in (weights, grads, m_in, v_in)]
    outs = _adamw(*packs, "adamw_small")
    small_shapes_all = [weights[n].shape for n in small_names]
    for dst, packed in zip((delta, new_m, new_v), outs):
        dst.update(dict(zip(small_names, _unpack(packed, small_shapes_all))))

    return (loss, grad_x[None], *[grads[n] for n in names], *[delta[n] for n in names],
            *[new_m[n] for n in names], *[new_v[n] for n in names])
```

```python
import functools

import jax
import jax.numpy as jnp
from jax import lax
from jax.experimental import pallas as pl
from jax.experimental.pallas import tpu as pltpu

F32 = jnp.float32
BF16 = jnp.bfloat16
HIGHEST = lax.Precision.HIGHEST
SCAN_PRECISION = None
MESH_ID = pl.DeviceIdType.MESH

N_DEV = 8
D_MODEL = 1024
HG_WIDTH = 512
HG_HEAD_DIM = 128
HG_HEADS = 4
RW_WIDTH = 512
RW_PAIRS = 4
RW_HEAD_DIM = 64
HG_COLS = 2048
RW_COLS = 1792
D_FF = 2816
NORM_EPS = 1e-6
RW_GN_EPS = 64e-5
L2_EPS = 1e-12
ADAM_LR, ADAM_B1, ADAM_B2, ADAM_EPS, ADAM_WD, ADAM_STEP = 0.001, 0.9, 0.999, 1e-08, 0.01, 10

HG_CHUNK = 32
HG_HALF = 16
RW_CHUNK = 64
SCAN_ROWS = 256
LANES = 128

NN = ((1,), (0,))
NT = ((1,), (1,))
TN = ((0,), (0,))


def _dot(a, b, dims=NN, precision=SCAN_PRECISION):
    if precision is None:
        a, b = a.astype(BF16), b.astype(BF16)
    return lax.dot_general(a, b, (dims, ((), ())), precision=precision, preferred_element_type=F32)


def _iota2(shape, dim):
    return lax.broadcasted_iota(jnp.int32, shape, dim)


def _sigmoid(z):
    return 0.5 * jnp.tanh(0.5 * z) + 0.5


def _row_tile(n, want):
    t = min(n, want)
    while n % t:
        t //= 2
    return t


def _rms_fwd(x, w, name):
    T, D = x.shape
    tb = _row_tile(T, 512)

    def body(x_ref, w_ref, h_ref):
        xv = x_ref[...]
        r = lax.rsqrt(jnp.mean(xv * xv, axis=-1, keepdims=True) + NORM_EPS)
        h_ref[...] = (xv * r * w_ref[...]).astype(h_ref.dtype)

    return pl.pallas_call(
        body, name=name, grid=(T // tb,),
        in_specs=[pl.BlockSpec((tb, D), lambda i: (i, 0)), pl.BlockSpec((1, D), lambda i: (0, 0))],
        out_specs=pl.BlockSpec((tb, D), lambda i: (i, 0)),
        out_shape=jax.ShapeDtypeStruct((T, D), BF16),
    )(x, w)


def _rms_bwd(dh, x, w, dres, name):
    T, D = x.shape
    tb = _row_tile(T, 256)

    def body(dh_ref, x_ref, w_ref, dres_ref, dx_ref, dw_ref):
        @pl.when(pl.program_id(0) == 0)
        def _():
            dw_ref[...] = jnp.zeros_like(dw_ref)

        xv = x_ref[...]
        r = lax.rsqrt(jnp.mean(xv * xv, axis=-1, keepdims=True) + NORM_EPS)
        xn = xv * r
        dy = dh_ref[...].astype(F32)
        dxn = dy * w_ref[...]
        dx_ref[...] = dres_ref[...] + r * (dxn - xn * jnp.mean(dxn * xn, axis=-1, keepdims=True))
        dw_ref[...] += jnp.sum(dy * xn, axis=0, keepdims=True)

    row = pl.BlockSpec((tb, D), lambda i: (i, 0))
    vec = pl.BlockSpec((1, D), lambda i: (0, 0))
    return pl.pallas_call(
        body, name=name, grid=(T // tb,),
        in_specs=[row, row, vec, row], out_specs=[row, vec],
        out_shape=[jax.ShapeDtypeStruct((T, D), F32), jax.ShapeDtypeStruct((1, D), F32)],
    )(dh, x, w, dres)


def _mm_nt(a, bt, name, out_dtype=F32):
    T, K = a.shape
    N = bt.shape[0]
    tm = _row_tile(T, 256)

    def body(a_ref, b_ref, o_ref):
        o_ref[...] = _dot(a_ref[...].astype(BF16), b_ref[...].astype(BF16), NT, None).astype(o_ref.dtype)

    return pl.pallas_call(
        body, name=name, grid=(T // tm,),
        in_specs=[pl.BlockSpec((tm, K), lambda i: (i, 0)), pl.BlockSpec((N, K), lambda i: (0, 0))],
        out_specs=pl.BlockSpec((tm, N), lambda i: (i, 0)),
        out_shape=jax.ShapeDtypeStruct((T, N), out_dtype),
    )(a, bt)


def _mm_nn(a, b, res, name, out_dtype=F32):
    parts = list(a) if isinstance(a, (list, tuple)) else [a]
    T = parts[0].shape[0]
    K, N = b.shape
    tm = _row_tile(T, 256)
    widths = [p.shape[1] for p in parts]
    n = len(parts)

    def body(*refs):
        b_ref, o_ref = refs[n], refs[-1]
        acc, off = None, 0
        for a_ref, w in zip(refs[:n], widths):
            d = _dot(a_ref[...].astype(BF16), b_ref[off:off + w, :].astype(BF16), NN, None)
            acc = d if acc is None else acc + d
            off += w
        if res is not None:
            acc = acc + refs[n + 1][...]
        o_ref[...] = acc.astype(o_ref.dtype)

    in_specs = [pl.BlockSpec((tm, w), lambda i: (i, 0)) for w in widths] + [pl.BlockSpec((K, N), lambda i: (0, 0))]
    args = parts + [b]
    if res is not None:
        in_specs.append(pl.BlockSpec((tm, N), lambda i: (i, 0)))
        args.append(res)
    return pl.pallas_call(
        body, name=name, grid=(T // tm,), in_specs=in_specs,
        out_specs=pl.BlockSpec((tm, N), lambda i: (i, 0)),
        out_shape=jax.ShapeDtypeStruct((T, N), out_dtype),
    )(*args)


def _mm_tn(a, b, tmm, name, out_dtype=F32):
    T, M = a.shape
    N = b.shape[1]
    tk = _row_tile(T, 512)
    nk = T // tk

    def body(a_ref, b_ref, o_ref, acc_ref):
        @pl.when(pl.program_id(1) == 0)
        def _():
            acc_ref[...] = jnp.zeros_like(acc_ref)

        acc_ref[...] += _dot(a_ref[...].astype(BF16), b_ref[...].astype(BF16), TN, None)

        @pl.when(pl.program_id(1) == nk - 1)
        def _():
            o_ref[...] = acc_ref[...].astype(o_ref.dtype)

    return pl.pallas_call(
        body, name=name, grid=(M // tmm, nk),
        in_specs=[pl.BlockSpec((tk, tmm), lambda m, k: (k, m)), pl.BlockSpec((tk, N), lambda m, k: (k, 0))],
        out_specs=pl.BlockSpec((tmm, N), lambda m, k: (m, 0)),
        out_shape=jax.ShapeDtypeStruct((M, N), out_dtype),
        scratch_shapes=[pltpu.VMEM((tmm, N), F32)],
    )(a, b)


class _RowShifts:
    def __init__(self, shape):
        index = _iota2(shape, 0)
        self.rows = shape[0]
        self.first = {n: index < n for n in (1, 2)}
        self.last = {n: index >= shape[0] - n for n in (1, 2)}

    def down(self, z, n):
        return jnp.where(self.first[n], 0.0, pltpu.roll(z, n, 0))

    def up(self, z, n):
        return jnp.where(self.last[n], 0.0, pltpu.roll(z, self.rows - n, 0))


def _shift_fwd(proj, mu, name):
    T = proj.shape[0]
    nblk = RW_COLS // LANES
    first = HG_COLS // LANES

    def body(p_ref, mu_ref, o_ref):
        p = p_ref[...]
        o_ref[...] = p + (_RowShifts(p.shape).down(p, 1) - p) * mu_ref[...]

    return pl.pallas_call(
        body, name=name, grid=(nblk,),
        in_specs=[pl.BlockSpec((T, LANES), lambda j: (0, first + j)), pl.BlockSpec((1, LANES), lambda j: (0, j))],
        out_specs=pl.BlockSpec((T, LANES), lambda j: (0, j)),
        out_shape=jax.ShapeDtypeStruct((T, RW_COLS), F32),
    )(proj, mu)


def _shift_bwd(ds, proj, mu, col0, name):
    T, width = ds.shape
    nblk = width // LANES
    first = (HG_COLS + col0) // LANES
    mu0 = col0 // LANES

    def body(ds_ref, p_ref, mu_ref, dp_ref, dmu_ref):
        dsv = ds_ref[...]
        p = p_ref[...]
        m = mu_ref[...]
        shifts = _RowShifts(p.shape)
        dp_ref[...] = (dsv * (1.0 - m) + shifts.up(dsv * m, 1)).astype(dp_ref.dtype)
        dmu_ref[...] = jnp.sum(dsv * (shifts.down(p, 1) - p), axis=0, keepdims=True)

    return pl.pallas_call(
        body, name=name, grid=(nblk,),
        in_specs=[pl.BlockSpec((T, LANES), lambda j: (0, j)),
                  pl.BlockSpec((T, LANES), lambda j: (0, first + j)),
                  pl.BlockSpec((1, LANES), lambda j: (0, mu0 + j))],
        out_specs=[pl.BlockSpec((T, LANES), lambda j: (0, j)), pl.BlockSpec((1, LANES), lambda j: (0, j))],
        out_shape=[jax.ShapeDtypeStruct((T, width), BF16), jax.ShapeDtypeStruct((1, width), F32)],
    )(ds, proj, mu)


def _conv3(z, w_ref, shifts):
    return w_ref[0:1, :] * shifts.down(z, 2) + w_ref[1:2, :] * shifts.down(z, 1) + w_ref[2:3, :] * z


def _ffn_act_fwd(u, conv_w, conv_b, name):
    T = u.shape[0]
    nblk = D_FF // LANES

    def body(ug_ref, uv_ref, wg_ref, wv_ref, bg_ref, bv_ref, act_ref):
        shifts = _RowShifts((T, LANES))
        gate = _conv3(ug_ref[...], wg_ref, shifts) + bg_ref[...]
        val = _conv3(uv_ref[...], wv_ref, shifts) + bv_ref[...]
        act_ref[...] = (gate * _sigmoid(gate) * val).astype(act_ref.dtype)

    col = lambda off: pl.BlockSpec((T, LANES), lambda j: (0, off + j))
    wsp = lambda off: pl.BlockSpec((3, LANES), lambda j: (0, off + j))
    bsp = lambda off: pl.BlockSpec((1, LANES), lambda j: (0, off + j))
    return pl.pallas_call(
        body, name=name, grid=(nblk,),
        in_specs=[col(0), col(nblk), wsp(0), wsp(nblk), bsp(0), bsp(nblk)],
        out_specs=pl.BlockSpec((T, LANES), lambda j: (0, j)),
        out_shape=jax.ShapeDtypeStruct((T, D_FF), BF16),
    )(u, u, conv_w, conv_w, conv_b, conv_b)


def _ffn_act_bwd(u, dact, conv_w, conv_b, name):
    T = u.shape[0]
    nblk = D_FF // LANES

    def conv_bwd(z, dzc, w_ref, du_ref, dw_ref, db_ref, shifts):
        up1, up2 = shifts.up(dzc, 1), shifts.up(dzc, 2)
        du = w_ref[2:3, :] * dzc + w_ref[1:2, :] * up1 + w_ref[0:1, :] * up2
        du_ref[...] = du.astype(du_ref.dtype)
        dw_ref[0:1, :] = jnp.sum(up2 * z, axis=0, keepdims=True)
        dw_ref[1:2, :] = jnp.sum(up1 * z, axis=0, keepdims=True)
        dw_ref[2:3, :] = jnp.sum(dzc * z, axis=0, keepdims=True)
        db_ref[...] = jnp.sum(dzc, axis=0, keepdims=True)

    def body(ug_ref, uv_ref, da_ref, wg_ref, wv_ref, bg_ref, bv_ref,
             dug_ref, duv_ref, dwg_ref, dwv_ref, dbg_ref, dbv_ref):
        ug, uv = ug_ref[...], uv_ref[...]
        shifts = _RowShifts((T, LANES))
        gate = _conv3(ug, wg_ref, shifts) + bg_ref[...]
        val = _conv3(uv, wv_ref, shifts) + bv_ref[...]
        da = da_ref[...].astype(F32)
        sg = _sigmoid(gate)
        dgate = da * val * (sg * (1.0 + gate * (1.0 - sg)))
        dval = da * gate * sg
        conv_bwd(ug, dgate, wg_ref, dug_ref, dwg_ref, dbg_ref, shifts)
        conv_bwd(uv, dval, wv_ref, duv_ref, dwv_ref, dbv_ref, shifts)

    col = lambda off: pl.BlockSpec((T, LANES), lambda j: (0, off + j))
    wsp = lambda off: pl.BlockSpec((3, LANES), lambda j: (0, off + j))
    bsp = lambda off: pl.BlockSpec((1, LANES), lambda j: (0, off + j))
    half = lambda r, dt: jax.ShapeDtypeStruct((r, D_FF), dt)
    return pl.pallas_call(
        body, name=name, grid=(nblk,),
        in_specs=[col(0), col(nblk), col(0), wsp(0), wsp(nblk), bsp(0), bsp(nblk)],
        out_specs=[col(0), col(0), wsp(0), wsp(0), bsp(0), bsp(0)],
        out_shape=[half(T, BF16), half(T, BF16), half(3, F32), half(3, F32), half(1, F32), half(1, F32)],
    )(u, u, dact, conv_w, conv_w, conv_b, conv_b)


def _loss_head(x2, w, target, name):
    T, D = x2.shape
    tb = _row_tile(T, 256)

    def body(x_ref, w_ref, t_ref, loss_ref, dx_ref, dw_ref):
        @pl.when(pl.program_id(0) == 0)
        def _():
            loss_ref[...] = jnp.zeros_like(loss_ref)
            dw_ref[...] = jnp.zeros_like(dw_ref)

        xv = x_ref[...]
        r = lax.rsqrt(jnp.mean(xv * xv, axis=-1, keepdims=True) + NORM_EPS)
        xn = xv * r
        err = xn * w_ref[...] - t_ref[...]
        row_loss = jnp.sum(err * err, axis=-1, keepdims=True) * (0.5 / D)
        loss_ref[...] += jnp.sum(row_loss, axis=0, keepdims=True)
        dy = err * (1.0 / D)
        dxn = dy * w_ref[...]
        dx_ref[...] = r * (dxn - xn * jnp.mean(dxn * xn, axis=-1, keepdims=True))
        dw_ref[...] += jnp.sum(dy * xn, axis=0, keepdims=True)

    row = pl.BlockSpec((tb, D), lambda i: (i, 0))
    vec = pl.BlockSpec((1, D), lambda i: (0, 0))
    return pl.pallas_call(
        body, name=name, grid=(T // tb,),
        in_specs=[row, vec, row],
        out_specs=[pl.BlockSpec((1, 1), lambda i: (0, 0)), row, vec],
        out_shape=[jax.ShapeDtypeStruct((1, 1), F32), jax.ShapeDtypeStruct((T, D), F32),
                   jax.ShapeDtypeStruct((1, D), F32)],
    )(x2, w, target)


def _adamw(w, g, m, v, name):
    R, C = w.shape
    tb = _row_tile(R, 256) if R % 8 == 0 else R

    def body(w_ref, g_ref, m_ref, v_ref, d_ref, nm_ref, nv_ref):
        gv = g_ref[...]
        nm = ADAM_B1 * m_ref[...] + (1.0 - ADAM_B1) * gv
        nv = ADAM_B2 * v_ref[...] + (1.0 - ADAM_B2) * (gv * gv)
        m_hat = nm / (1.0 - ADAM_B1 ** ADAM_STEP)
        v_hat = nv / (1.0 - ADAM_B2 ** ADAM_STEP)
        d_ref[...] = -ADAM_LR * (m_hat / (jnp.sqrt(v_hat) + ADAM_EPS) + ADAM_WD * w_ref[...])
        nm_ref[...] = nm
        nv_ref[...] = nv

    blk = pl.BlockSpec((tb, C), lambda i: (i, 0))
    sd = jax.ShapeDtypeStruct((R, C), F32)
    return pl.pallas_call(
        body, name=name, grid=(R // tb,), in_specs=[blk] * 4, out_specs=[blk] * 3, out_shape=[sd] * 3,
    )(w, g, m, v)


def _chunk_masks(rows, chunk):
    shift = chunk.bit_length() - 1
    i, j = _iota2((rows, rows), 0), _iota2((rows, rows), 1)
    same = jnp.right_shift(i, shift) == jnp.right_shift(j, shift)
    return same.astype(F32), (same & (j <= i)).astype(F32), (same & (j < i)).astype(F32)


def _head_lanes(h):
    return slice(h * LANES, (h + 1) * LANES)


def _chunk_rows(c, chunk):
    return pl.ds(pl.multiple_of(c * chunk, chunk), chunk)


def _hg_consts(rows):
    same, tril, _ = _chunk_masks(rows, HG_CHUNK)
    half_same, half_tril, _ = _chunk_masks(rows, HG_HALF)
    i, j = _iota2((rows, rows), 0), _iota2((rows, rows), 1)
    half_shift, shift = HG_HALF.bit_length() - 1, HG_CHUNK.bit_length() - 1
    mid_row = jnp.left_shift(jnp.right_shift(i, half_shift), half_shift) + (HG_HALF // 2 - 1)
    bound_row = jnp.left_shift(jnp.right_shift(i, shift), shift) + (HG_HALF - 1)
    upto_mid = ((same > 0) & (j <= mid_row)).astype(F32)
    upto_bound = ((same > 0) & (j <= bound_row)).astype(F32)
    lower_left = tril * (1.0 - half_same)
    return jnp.concatenate([tril, same, upto_mid, upto_bound], axis=0), half_tril, lower_left


N_HG_IN = 5


def _hg_prep(consts, *flat):
    sums, half_tril, lower_left = consts
    rows = half_tril.shape[0]
    heads, logs = [], []
    for h in range(len(flat) // N_HG_IN):
        qr, fr, ir, l0, l1 = flat[N_HG_IN * h:N_HG_IN * (h + 1)]
        lb = _sigmoid(l0 - l1)
        f = lb + (1.0 - lb) * _sigmoid(fr)
        heads.append((qr * _sigmoid(qr) * (HG_HEAD_DIM ** -0.5), 1.0 - f, ir))
        logs.append(jnp.log(f))
    acc = _dot(sums, jnp.concatenate(logs, axis=1), NN, HIGHEST)
    sums_of = []
    for h in range(len(heads)):
        acc_h = acc[:, h * LANES:(h + 1) * LANES]
        sums_of.append(tuple(acc_h[n * rows:(n + 1) * rows] for n in range(4)))
    near = [_dot(q * jnp.exp(a - mid), k * jnp.exp(mid - a), NT) * half_tril
            for (q, k, _), (a, _, mid, _) in zip(heads, sums_of)]
    far = [_dot(q * jnp.exp(jnp.minimum(a - bound, 0.0)), k * jnp.exp(jnp.minimum(bound - a, 0.0)), NT) * lower_left
           for (q, k, _), (a, _, _, bound) in zip(heads, sums_of)]
    intra = [_dot(n + f, ir) for n, f, (_, _, ir) in zip(near, far, heads)]
    return tuple((q * jnp.exp(a), o_intra, k * jnp.exp(tot - a), jnp.exp(tot))
                 for (q, k, _), (a, tot, _, _), o_intra in zip(heads, sums_of, intra))


def _hg_prep_args(q_ref, f_ref, i_ref, l0_ref, l1_ref):
    flat = []
    for h in range(HG_HEADS):
        ln = _head_lanes(h)
        flat += [q_ref[:, ln], f_ref[:, ln], i_ref[:, ln], l0_ref[:, ln], l1_ref[:, ln]]
    return flat


def _hg_post(o, gr, nw):
    on = o * lax.rsqrt(jnp.mean(o * o, axis=-1, keepdims=True) + NORM_EPS)
    return on * nw * (gr * _sigmoid(gr))


def _hg_specs(T, tb, rev):
    nT = T // tb
    tix = (lambda t: nT - 1 - t) if rev else (lambda t: t)
    col = lambda blk: pl.BlockSpec((tb, HG_WIDTH), lambda t: (tix(t), blk))
    vec = pl.BlockSpec((1, HG_WIDTH), lambda t: (0, 0))
    st = pl.BlockSpec((HG_HEADS, tb // HG_CHUNK, HG_HEAD_DIM, HG_HEAD_DIM), lambda t: (0, tix(t), 0, 0))
    return nT, col, vec, st


def _hg_fwd(proj, l0, l1, nw, name):
    T = proj.shape[0]
    tb = _row_tile(T, SCAN_ROWS)
    nsub = tb // HG_CHUNK
    nT, col, vec, st = _hg_specs(T, tb, False)

    def body(q_ref, f_ref, i_ref, g_ref, l0_ref, l1_ref, nw_ref, o_ref, st_ref, s_ref, qe_ref, kd_ref, dec_ref):
        @pl.when(pl.program_id(0) == 0)
        def _():
            s_ref[...] = jnp.zeros_like(s_ref)

        consts = _hg_consts(tb)
        outs = _hg_prep(consts, *_hg_prep_args(q_ref, f_ref, i_ref, l0_ref, l1_ref))
        for h, (qe, o_intra, kd, dec) in enumerate(outs):
            qe_ref[h], kd_ref[h], dec_ref[h] = qe, kd, dec
            o_ref[:, _head_lanes(h)] = o_intra

        def step(c, carry):
            rows = _chunk_rows(c, HG_CHUNK)
            for h in range(HG_HEADS):
                ln = _head_lanes(h)
                S = s_ref[h]
                st_ref[h, c] = S
                o_ref[rows, ln] += _dot(qe_ref[h, rows, :], S, NT)
                s_ref[h] = S * dec_ref[h, pl.ds(c * HG_CHUNK, 1), :] + _dot(i_ref[rows, ln], kd_ref[h, rows, :], TN)
            return carry

        lax.fori_loop(0, nsub, step, 0)
        for h in range(HG_HEADS):
            ln = _head_lanes(h)
            o_ref[:, ln] = _hg_post(o_ref[:, ln], g_ref[:, ln], nw_ref[:, ln])

    blk = pltpu.VMEM((HG_HEADS, tb, LANES), F32)
    return pl.pallas_call(
        body, name=name, grid=(nT,),
        in_specs=[col(0), col(1), col(2), col(3), vec, vec, vec],
        out_specs=[col(0), st],
        out_shape=[jax.ShapeDtypeStruct((T, HG_WIDTH), F32),
                   jax.ShapeDtypeStruct((HG_HEADS, T // HG_CHUNK, HG_HEAD_DIM, HG_HEAD_DIM), F32)],
        scratch_shapes=[pltpu.VMEM((HG_HEADS, HG_HEAD_DIM, HG_HEAD_DIM), F32), blk, blk, blk],
    )(proj, proj, proj, proj, l0, l1, nw)


def _hg_bwd(proj, states, do, do_blk, l0, l1, nw, name):
    T = proj.shape[0]
    tb = _row_tile(T, SCAN_ROWS)
    nsub = tb // HG_CHUNK
    nT, col, vec, st = _hg_specs(T, tb, True)

    def body(q_ref, f_ref, i_ref, g_ref, st_ref, do_ref, l0_ref, l1_ref, nw_ref,
             dq_ref, df_ref, di_ref, dg_ref, dl0_ref, dl1_ref, dnw_ref,
             ds_ref, qe_ref, kd_ref, dec_ref, o_ref, dqe_ref, dkd_ref, ddec_ref, dis_ref):
        @pl.when(pl.program_id(0) == 0)
        def _():
            ds_ref[...] = jnp.zeros_like(ds_ref)
            dl0_ref[...] = jnp.zeros_like(dl0_ref)
            dl1_ref[...] = jnp.zeros_like(dl1_ref)
            dnw_ref[...] = jnp.zeros_like(dnw_ref)

        consts = _hg_consts(tb)
        outs, prep_vjp = jax.vjp(functools.partial(_hg_prep, consts),
                                 *_hg_prep_args(q_ref, f_ref, i_ref, l0_ref, l1_ref))
        for h, (qe, o_intra, kd, dec) in enumerate(outs):
            qe_ref[h], kd_ref[h], dec_ref[h], o_ref[h] = qe, kd, dec, o_intra

        def redo(c, carry):
            rows = _chunk_rows(c, HG_CHUNK)
            for h in range(HG_HEADS):
                o_ref[h, rows, :] += _dot(qe_ref[h, rows, :], st_ref[h, c], NT)
            return carry

        lax.fori_loop(0, nsub, redo, 0)
        for h in range(HG_HEADS):
            ln = _head_lanes(h)
            _, vjp = jax.vjp(_hg_post, o_ref[h], g_ref[:, ln], nw_ref[:, ln])
            d_o, dgr, dnw = vjp(do_ref[:, ln])
            o_ref[h] = d_o
            dg_ref[:, ln] = dgr.astype(dg_ref.dtype)
            dnw_ref[:, ln] += dnw
        ddec_ref[...] = jnp.zeros_like(ddec_ref)

        def step(i, carry):
            c = nsub - 1 - i
            rows = _chunk_rows(c, HG_CHUNK)
            row0 = pl.ds(c * HG_CHUNK, 1)
            for h in range(HG_HEADS):
                ln = _head_lanes(h)
                G = ds_ref[h]
                S = st_ref[h, c]
                d_o = o_ref[h, rows, :]
                dqe_ref[h, rows, :] = _dot(d_o, S)
                dkd_ref[h, rows, :] = _dot(i_ref[rows, ln], G)
                dis_ref[h, rows, :] = _dot(kd_ref[h, rows, :], G, NT)
                ddec_ref[h, row0, :] = jnp.sum(S * G, axis=0, keepdims=True)
                ds_ref[h] = G * dec_ref[h, row0, :] + _dot(d_o, qe_ref[h, rows, :], TN)
            return carry

        lax.fori_loop(0, nsub, step, 0)
        grads = prep_vjp(tuple((dqe_ref[h], o_ref[h], dkd_ref[h], ddec_ref[h]) for h in range(HG_HEADS)))
        for h in range(HG_HEADS):
            ln = _head_lanes(h)
            dq, df, di, dl0, dl1 = grads[N_HG_IN * h:N_HG_IN * (h + 1)]
            dq_ref[:, ln] = dq.astype(dq_ref.dtype)
            df_ref[:, ln] = df.astype(df_ref.dtype)
            di_ref[:, ln] = (di + dis_ref[h]).astype(di_ref.dtype)
            dl0_ref[:, ln] += dl0
            dl1_ref[:, ln] += dl1

    dcol = jax.ShapeDtypeStruct((T, HG_WIDTH), BF16)
    dvec = jax.ShapeDtypeStruct((1, HG_WIDTH), F32)
    blk = pltpu.VMEM((HG_HEADS, tb, LANES), F32)
    return pl.pallas_call(
        body, name=name, grid=(nT,),
        in_specs=[col(0), col(1), col(2), col(3), st, col(do_blk), vec, vec, vec],
        out_specs=[col(0)] * 4 + [vec] * 3,
        out_shape=[dcol] * 4 + [dvec] * 3,
        scratch_shapes=[pltpu.VMEM((HG_HEADS, HG_HEAD_DIM, HG_HEAD_DIM), F32)] + [blk] * 8,
    )(proj, proj, proj, proj, states, do, l0, l1, nw)


def _rw_consts(rows):
    same, tril, stril = _chunk_masks(rows, RW_CHUNK)
    br, bc = _iota2((LANES, LANES), 0), _iota2((LANES, LANES), 1)
    blockdiag = ((br < RW_HEAD_DIM) == (bc < RW_HEAD_DIM)).astype(F32)
    m0 = (_iota2((1, LANES), 1) < RW_HEAD_DIM).astype(F32)
    return same, tril, stril, blockdiag, m0, 1.0 - m0


def _unit_lower_inverses_impl(lows):
    rows = lows[0].shape[0]
    eye = (_iota2(lows[0].shape, 0) == _iota2(lows[0].shape, 1)).astype(F32)
    xs = [low + eye for low in lows]
    ps = [_dot(low, low) for low in lows]
    n = 4
    while n < RW_CHUNK:
        zs = [_dot(jnp.concatenate([p, x], axis=0), p) for p, x in zip(ps, xs)]
        ps = [z[:rows] for z in zs]
        xs = [x + z[rows:] for x, z in zip(xs, zs)]
        n *= 2
    return tuple(x + _dot(x, p) for x, p in zip(xs, ps))


@jax.custom_vjp
def _unit_lower_inverses(lows):
    return _unit_lower_inverses_impl(lows)


def _unit_lower_inverses_fwd(lows):
    xs = _unit_lower_inverses_impl(lows)
    return xs, xs


def _unit_lower_inverses_bwd(xs, dxs):
    ts = [_dot(x, dx, TN) for x, dx in zip(xs, dxs)]
    return (tuple(_dot(t, x, NT) for t, x in zip(ts, xs)),)


_unit_lower_inverses.defvjp(_unit_lower_inverses_fwd, _unit_lower_inverses_bwd)


N_PREP_IN = 12
N_PREP_OUT = 9
RW_GROUP = 2


def _rw_prep(consts, *flat):
    same, tril, stril, blockdiag, m0, m1 = consts
    rows = tril.shape[0]
    masks = (m0, m1)
    pre = []
    for i in range(len(flat) // N_PREP_IN):
        r, kx, v, lw, gd, w0, a0, k_k, k_a, w2p, a2p, g2 = flat[N_PREP_IN * i:N_PREP_IN * (i + 1)]
        xw = w0 + _dot(jnp.tanh(lw), w2p)
        w = jnp.minimum(xw, 0.0) - jnp.log(1.0 + jnp.exp(-jnp.abs(xw))) - 0.5
        ld = -jnp.exp(w)
        a_s = _sigmoid(a0 + _dot(lw, a2p))
        g = _dot(_sigmoid(gd), g2)
        kk = kx * k_k
        kk = kk / jnp.maximum(jnp.sqrt(_dot(kk * kk, blockdiag)), L2_EPS)
        k2 = kx * (1.0 + (a_s - 1.0) * k_a)
        bv = kk * a_s
        acc = _dot(jnp.concatenate([tril, same], axis=0), ld, NN, HIGHEST)
        cum, tot = acc[:rows], acc[rows:]
        ecn = jnp.exp(-cum)
        a_t = -kk * jnp.exp(cum - ld)
        r_t = r * jnp.exp(cum)
        rem = jnp.exp(tot - cum)
        z = _dot(jnp.concatenate([a_t * m0, a_t * m1, r_t * m0, r_t * m1], axis=0),
                 jnp.concatenate([bv * ecn, k2 * ecn], axis=0), NT)
        pre.append((v, a_t, r_t, z, (bv * rem, k2 * rem, jnp.exp(tot), k2, g)))
    heads = [(i, h) for i in range(len(pre)) for h in range(2)]
    za = {ih: pre[ih[0]][3][ih[1] * rows:(ih[1] + 1) * rows] for ih in heads}
    zr = {ih: pre[ih[0]][3][(2 + ih[1]) * rows:(3 + ih[1]) * rows] for ih in heads}
    tinv = dict(zip(heads, _unit_lower_inverses(tuple(za[ih][:, :rows] * stril for ih in heads))))
    lv = {ih: _dot(jnp.concatenate([za[ih][:, rows:] * stril, zr[ih][:, rows:] * tril], axis=0), pre[ih[0]][0])
          for ih in heads}
    wu = {ih: _dot(tinv[ih], jnp.concatenate([pre[ih[0]][1] * masks[ih[1]], lv[ih][:rows]], axis=1)) for ih in heads}
    w_m = {ih: wu[ih][:, :LANES] for ih in heads}
    u_m = {ih: masks[ih[1]] * wu[ih][:, LANES:] for ih in heads}
    qy = {ih: _dot(zr[ih][:, :rows] * tril, jnp.concatenate([w_m[ih], u_m[ih]], axis=1)) for ih in heads}
    outs = []
    for i in range(len(pre)):
        a, b = (i, 0), (i, 1)
        W = w_m[a] + w_m[b]
        U = u_m[a] + u_m[b]
        Q = pre[i][2] + qy[a][:, :LANES] + qy[b][:, :LANES]
        Y0 = qy[a][:, LANES:] + qy[b][:, LANES:] + m0 * lv[a][rows:] + m1 * lv[b][rows:]
        outs.append((W, U, Q, Y0) + pre[i][4])
    return tuple(outs)


def _rw_post(blockdiag, y, r, v, k2, g, r_k, ln_w, ln_b):
    inv_n = 1.0 / RW_HEAD_DIM
    yc = y - _dot(y, blockdiag) * inv_n
    var = _dot(yc * yc, blockdiag) * inv_n
    yn = yc * lax.rsqrt(var + RW_GN_EPS) * ln_w + ln_b
    bonus = _dot(r * k2 * r_k, blockdiag) * v
    return (yn + bonus) * g


N_RW_VEC = 7
N_RW_MAT = 3


def _rw_specs(T, tb, rev):
    nT = T // tb
    tix = (lambda t: nT - 1 - t) if rev else (lambda t: t)
    wide = lambda blk: pl.BlockSpec((tb, RW_WIDTH), lambda t: (tix(t), blk))
    narrow = lambda blk: pl.BlockSpec((tb, LANES), lambda t: (tix(t), blk))
    vec = pl.BlockSpec((1, RW_WIDTH), lambda t: (0, 0))
    mat = pl.BlockSpec((RW_PAIRS, LANES, LANES), lambda t: (0, 0, 0))
    st = pl.BlockSpec((RW_PAIRS, tb // RW_CHUNK, LANES, LANES), lambda t: (0, tix(t), 0, 0))
    lora0 = 3 * RW_WIDTH // LANES
    ins = [wide(0), wide(1), wide(2), narrow(lora0), narrow(lora0 + 1)]
    return nT, wide, vec, mat, st, ins


def _rw_prep_args(p, r_ref, k_ref, v_ref, lw_ref, gd_ref, vrefs, mrefs):
    ln = _head_lanes(p)
    w0, a0, k_k, k_a = [x[:, ln] for x in vrefs[:4]]
    return (r_ref[:, ln], k_ref[:, ln], v_ref[:, ln], lw_ref[...], gd_ref[...], w0, a0, k_k, k_a,
            *[x[p] for x in mrefs])


def _stack_chunks(ref, top, bottom):
    C = RW_CHUNK
    for c in range(ref.shape[0]):
        ref[c, 0:C, :] = top[c * C:(c + 1) * C]
        ref[c, C:2 * C, :] = bottom[c * C:(c + 1) * C]


def _group_args(p0, r_ref, k_ref, v_ref, lw_ref, gd_ref, vrefs, mrefs):
    flat = []
    for p in range(p0, p0 + RW_GROUP):
        flat += list(_rw_prep_args(p, r_ref, k_ref, v_ref, lw_ref, gd_ref, vrefs, mrefs))
    return flat


def _rw_fwd(rws, vecs, mats, name):
    T = rws.shape[0]
    tb = _row_tile(T, SCAN_ROWS)
    nsub = tb // RW_CHUNK
    C = RW_CHUNK
    nT, wide, vec, mat, st, ins = _rw_specs(T, tb, False)

    def body(*refs):
        r_ref, k_ref, v_ref, lw_ref, gd_ref = refs[:5]
        vrefs = refs[5:5 + N_RW_VEC]
        mrefs = refs[5 + N_RW_VEC:5 + N_RW_VEC + N_RW_MAT]
        o_ref, st_ref, s_ref, wq_ref, uy_ref, bk_ref, misc_ref, y_ref = refs[-8:]

        @pl.when(pl.program_id(0) == 0)
        def _():
            s_ref[...] = jnp.zeros_like(s_ref)

        consts = _rw_consts(tb)
        blockdiag = consts[3]
        for p0 in range(0, RW_PAIRS, RW_GROUP):
            outs = _rw_prep(consts, *_group_args(p0, r_ref, k_ref, v_ref, lw_ref, gd_ref, vrefs, mrefs))
            for p, (W, U, Q, Y0, Bg, Kg, dec, k2, g) in zip(range(p0, p0 + RW_GROUP), outs):
                _stack_chunks(wq_ref.at[p], W, Q)
                _stack_chunks(uy_ref.at[p], U, Y0)
                _stack_chunks(bk_ref.at[p], Bg, Kg)
                misc_ref[0, p], misc_ref[1, p], misc_ref[2, p] = dec, k2, g

        def step(c, carry):
            rows = _chunk_rows(c, C)
            for p in range(RW_PAIRS):
                S = s_ref[p]
                st_ref[p, c] = S
                py = _dot(wq_ref[p, c], S, NT) + uy_ref[p, c]
                y_ref[p, rows, :] = py[C:]
                pv = jnp.concatenate([py[:C], v_ref[rows, _head_lanes(p)]], axis=0)
                s_ref[p] = (S * misc_ref[0, p, pl.ds(c * C, 1), :] + _dot(pv, bk_ref[p, c], TN)) * blockdiag
            return carry

        lax.fori_loop(0, nsub, step, 0)
        for p in range(RW_PAIRS):
            ln = _head_lanes(p)
            r_k, ln_w, ln_b = [x[:, ln] for x in vrefs[4:]]
            o_ref[:, ln] = _rw_post(blockdiag, y_ref[p], r_ref[:, ln], v_ref[:, ln], misc_ref[1, p], misc_ref[2, p],
                                    r_k, ln_w, ln_b)

    stacked = pltpu.VMEM((RW_PAIRS, nsub, 2 * C, LANES), F32)
    return pl.pallas_call(
        body, name=name, grid=(nT,),
        in_specs=ins + [vec] * N_RW_VEC + [mat] * N_RW_MAT,
        out_specs=[wide(0), st],
        out_shape=[jax.ShapeDtypeStruct((T, RW_WIDTH), F32),
                   jax.ShapeDtypeStruct((RW_PAIRS, T // RW_CHUNK, LANES, LANES), F32)],
        scratch_shapes=[pltpu.VMEM((RW_PAIRS, LANES, LANES), F32), stacked, stacked, stacked,
                        pltpu.VMEM((3, RW_PAIRS, tb, LANES), F32), pltpu.VMEM((RW_PAIRS, tb, LANES), F32)],
    )(rws, rws, rws, rws, rws, *vecs, *mats)


def _rw_bwd(rws, states, do, do_blk, vecs, mats, name):
    T = rws.shape[0]
    tb = _row_tile(T, SCAN_ROWS)
    nsub = tb // RW_CHUNK
    C = RW_CHUNK
    G = RW_GROUP
    nT, wide, vec, mat, st, ins = _rw_specs(T, tb, True)
    nin = 5 + 1 + 1 + N_RW_VEC + N_RW_MAT

    def body(*refs):
        r_ref, k_ref, v_ref, lw_ref, gd_ref = refs[:5]
        st_ref, do_ref = refs[5], refs[6]
        vrefs = refs[7:7 + N_RW_VEC]
        mrefs = refs[7 + N_RW_VEC:nin]
        dr_ref, dk_ref, dv_ref, dlo_ref = refs[nin:nin + 4]
        dvec = refs[nin + 4:nin + 4 + N_RW_VEC]
        dmat = refs[nin + 4 + N_RW_VEC:nin + 4 + N_RW_VEC + N_RW_MAT]
        ds_ref, wq_ref, uy_ref, bk_ref, pv_ref, dec_ref, y_ref, dpre_ref, dvs_ref = refs[-9:]

        @pl.when(pl.program_id(0) == 0)
        def _():
            ds_ref[...] = jnp.zeros_like(ds_ref)
            for x in dvec + dmat:
                x[...] = jnp.zeros_like(x)

        consts = _rw_consts(tb)
        blockdiag = consts[3]
        dlw, dgd = 0.0, 0.0
        for p0 in range(0, RW_PAIRS, G):
            outs, prep_vjp = jax.vjp(functools.partial(_rw_prep, consts),
                                     *_group_args(p0, r_ref, k_ref, v_ref, lw_ref, gd_ref, vrefs, mrefs))
            for q, (W, U, Q, Y0, Bg, Kg, dec, _, _) in enumerate(outs):
                _stack_chunks(wq_ref.at[q], W, Q)
                _stack_chunks(uy_ref.at[q], U, Y0)
                _stack_chunks(bk_ref.at[q], Bg, Kg)
                dec_ref[q] = dec

            def redo(c, carry, p0=p0):
                rows = _chunk_rows(c, C)
                for q in range(G):
                    py = _dot(wq_ref[q, c], st_ref[p0 + q, c], NT) + uy_ref[q, c]
                    y_ref[q, rows, :] = py[C:]
                    pv_ref[q, c, 0:C, :] = py[:C]
                    pv_ref[q, c, C:2 * C, :] = v_ref[rows, _head_lanes(p0 + q)]
                return carry

            lax.fori_loop(0, nsub, redo, 0)
            post = []
            for q in range(G):
                ln = _head_lanes(p0 + q)
                r_k, ln_w, ln_b = [x[:, ln] for x in vrefs[4:]]
                _, post_vjp = jax.vjp(functools.partial(_rw_post, blockdiag), y_ref[q], r_ref[:, ln], v_ref[:, ln],
                                      outs[q][7], outs[q][8], r_k, ln_w, ln_b)
                dy, dr2, dv2, dk2, dg, dr_k, dln_w, dln_b = post_vjp(do_ref[:, ln])
                dpre_ref[q, 3] = dy
                dvs_ref[q] = dv2
                for x, gx in zip(dvec[4:], (dr_k, dln_w, dln_b)):
                    x[:, ln] += gx
                dpre_ref[q, 6] = jnp.zeros_like(dpre_ref[q, 6])
                post.append((dr2, dk2, dg))

            def step(i, carry, p0=p0):
                c = nsub - 1 - i
                rows = _chunk_rows(c, C)
                row0 = pl.ds(c * C, 1)
                for q in range(G):
                    Gs = ds_ref[p0 + q] * blockdiag
                    S = st_ref[p0 + q, c]
                    t1 = _dot(bk_ref[q, c], Gs, NT)
                    dpy = jnp.concatenate([t1[:C], dpre_ref[q, 3, rows, :]], axis=0)
                    t2 = _dot(dpy, S)
                    t3 = _dot(pv_ref[q, c], Gs)
                    dvs_ref[q, rows, :] += t1[C:]
                    dpre_ref[q, 0, rows, :] = t2[:C]
                    dpre_ref[q, 1, rows, :] = t1[:C]
                    dpre_ref[q, 2, rows, :] = t2[C:]
                    dpre_ref[q, 4, rows, :] = t3[:C]
                    dpre_ref[q, 5, rows, :] = t3[C:]
                    dpre_ref[q, 6, row0, :] = jnp.sum(S * Gs, axis=0, keepdims=True)
                    ds_ref[p0 + q] = Gs * dec_ref[q, row0, :] + _dot(dpy, wq_ref[q, c], TN)
                return carry

            lax.fori_loop(0, nsub, step, 0)
            grads = prep_vjp(tuple(tuple(dpre_ref[q, i] for i in range(7)) + post[q][1:] for q in range(G)))
            for q in range(G):
                ln = _head_lanes(p0 + q)
                gq = grads[N_PREP_IN * q:N_PREP_IN * (q + 1)]
                dr_ref[:, ln] = gq[0] + post[q][0]
                dk_ref[:, ln] = gq[1]
                dv_ref[:, ln] = gq[2] + dvs_ref[q]
                dlw = dlw + gq[3]
                dgd = dgd + gq[4]
                for x, gx in zip(dvec[:4], gq[5:9]):
                    x[:, ln] += gx
                for x, gx in zip(dmat, gq[9:]):
                    x[p0 + q] += gx
        dlo_ref[:, 0:LANES] = dlw
        dlo_ref[:, LANES:2 * LANES] = dgd

    dcol = jax.ShapeDtypeStruct((T, RW_WIDTH), F32)
    dlo_spec = pl.BlockSpec((tb, 2 * LANES), lambda t: (nT - 1 - t, 0))
    blk = pltpu.VMEM((G, tb, LANES), F32)
    stacked = pltpu.VMEM((G, nsub, 2 * C, LANES), F32)
    return pl.pallas_call(
        body, name=name, grid=(nT,),
        in_specs=ins + [st, wide(do_blk)] + [vec] * N_RW_VEC + [mat] * N_RW_MAT,
        out_specs=[wide(0)] * 3 + [dlo_spec] + [vec] * N_RW_VEC + [mat] * N_RW_MAT,
        out_shape=[dcol] * 3 + [jax.ShapeDtypeStruct((T, 2 * LANES), F32)]
        + [jax.ShapeDtypeStruct((1, RW_WIDTH), F32)] * N_RW_VEC
        + [jax.ShapeDtypeStruct((RW_PAIRS, LANES, LANES), F32)] * N_RW_MAT,
        scratch_shapes=[pltpu.VMEM((RW_PAIRS, LANES, LANES), F32), stacked, stacked, stacked, stacked, blk, blk,
                        pltpu.VMEM((G, 7, tb, LANES), F32), blk],
    )(rws, rws, rws, rws, rws, states, do, *vecs, *mats)


def _my_index():
    return 4 * lax.axis_index("x") + 2 * lax.axis_index("y") + lax.axis_index("c")


def _peer(bits):
    pos = []
    for name, flip in zip(("x", "y", "c"), bits):
        i = lax.axis_index(name)
        pos.append(1 - i if flip else i)
    return tuple(pos)


def _peer_index(bits):
    x, y, c = _peer(bits)
    return 4 * x + 2 * y + c


def _all_gather(shards, name):
    n = len(shards)
    chips = [(1, 0, 0), (0, 1, 0), (1, 1, 0)]
    sib = (0, 0, 1)

    def body(*refs):
        ins, outs = refs[:n], refs[n:2 * n]
        send_sems, recv_sems, local_sems = refs[2 * n:]

        def rows(k, dev):
            r = ins[k].shape[0]
            return outs[k].at[pl.ds(dev * r, r), :]

        def copy(k, slot, block_dev, to_bits, src=None):
            return pltpu.make_async_remote_copy(
                src_ref=rows(k, block_dev) if src is None else src, dst_ref=rows(k, block_dev),
                send_sem=send_sems.at[k, slot], recv_sem=recv_sems.at[k, slot],
                device_id=_peer(to_bits), device_id_type=MESH_ID)

        me = _my_index()
        started = []
        for k in range(n):
            mine = pltpu.make_async_copy(ins[k], rows(k, me), local_sems.at[k])
            mine.start()
            started.append(mine)
        sends = []
        for k in range(n):
            first = [copy(k, 0, me, sib, src=ins[k])]
            first += [copy(k, 1 + j, me, chip, src=ins[k]) for j, chip in enumerate(chips)]
            for cp in first:
                cp.start()
            sends += first
        for k in range(n):
            for j, chip in enumerate(chips):
                copy(k, 1 + j, _peer_index(chip), chip).wait_recv()
                fwd = copy(k, 4 + j, _peer_index(chip), sib)
                fwd.start()
                sends.append(fwd)
        for k in range(n):
            copy(k, 0, _peer_index(sib), sib).wait_recv()
            for j, chip in enumerate(chips):
                both = (chip[0], chip[1], 1)
                copy(k, 4 + j, _peer_index(both), sib).wait_recv()
        for cp in sends:
            cp.wait_send()
        for cp in started:
            cp.wait()

    any_spec = pl.BlockSpec(memory_space=pl.ANY)
    return pl.pallas_call(
        body, name=name,
        in_specs=[any_spec] * n, out_specs=[any_spec] * n,
        out_shape=[jax.ShapeDtypeStruct((N_DEV * s.shape[0], s.shape[1]), s.dtype) for s in shards],
        scratch_shapes=[pltpu.SemaphoreType.DMA((n, 7)), pltpu.SemaphoreType.DMA((n, 7)),
                        pltpu.SemaphoreType.DMA((n,))],
    )(*shards)


def _exchange(partials, name):
    n = len(partials)
    flips = [(dx, dy, dc) for dx in (0, 1) for dy in (0, 1) for dc in (0, 1)][1:]

    def body(*refs):
        ins, outs = refs[:n], refs[n:2 * n]
        send_sems, recv_sems, local_sems = refs[2 * n:]
        me = _my_index()
        local = []
        for k in range(n):
            cp = pltpu.make_async_copy(ins[k].at[me], outs[k].at[me], local_sems.at[k])
            cp.start()
            local.append(cp)
        copies = []
        for k in range(n):
            for d, bits in enumerate(flips):
                cp = pltpu.make_async_remote_copy(
                    src_ref=ins[k].at[_peer_index(bits)], dst_ref=outs[k].at[me],
                    send_sem=send_sems.at[k, d], recv_sem=recv_sems.at[k, d],
                    device_id=_peer(bits), device_id_type=MESH_ID)
                cp.start()
                copies.append(cp)
        for cp in copies:
            cp.wait_recv()
        for cp in copies:
            cp.wait_send()
        for cp in local:
            cp.wait()

    any_spec = pl.BlockSpec(memory_space=pl.ANY)
    return pl.pallas_call(
        body, name=name,
        in_specs=[any_spec] * n, out_specs=[any_spec] * n,
        out_shape=[jax.ShapeDtypeStruct(p.shape, p.dtype) for p in partials],
        scratch_shapes=[pltpu.SemaphoreType.DMA((n, 7)), pltpu.SemaphoreType.DMA((n, 7)),
                        pltpu.SemaphoreType.DMA((n,))],
    )(*partials)


HBM_SPEC = pl.BlockSpec(memory_space=pltpu.HBM)
SEM_SPEC = pl.BlockSpec(memory_space=pltpu.SEMAPHORE)
ALL_FLIPS = [(dx, dy, dc) for dx in (0, 1) for dy in (0, 1) for dc in (0, 1)][1:]


def _spread_copies(srcs, lands, send_sems, recv_sems, per_peer_source):
    me = _my_index()
    copies = []
    for k, land in enumerate(lands):
        for d, bits in enumerate(ALL_FLIPS):
            src = srcs[k].at[_peer_index(bits)] if per_peer_source else land.at[me]
            copies.append(pltpu.make_async_remote_copy(
                src_ref=src, dst_ref=land.at[me],
                send_sem=send_sems.at[k * 7 + d], recv_sem=recv_sems.at[k * 7 + d],
                device_id=_peer(bits), device_id_type=MESH_ID))
    return copies


def _spread_start(srcs, lands, name):
    ns, n = len(srcs), len(lands)

    def body(*refs):
        src_refs, land_refs = refs[:ns], refs[ns:ns + n]
        send_sems, recv_sems = refs[ns + n], refs[ns + n + 1]
        token = refs[-1]
        for cp in _spread_copies(src_refs, land_refs, send_sems, recv_sems, ns > 0):
            cp.start()
        token[...] = jnp.zeros_like(token)

    bufs = list(srcs) + list(lands)
    out = pl.pallas_call(
        body, name=name,
        out_shape=(pltpu.SemaphoreType.DMA((7 * n,)), pltpu.SemaphoreType.DMA((7 * n,)),
                   *[pltpu.HBM(b.shape, b.dtype) for b in bufs], jax.ShapeDtypeStruct((8, LANES), F32)),
        in_specs=[HBM_SPEC] * (ns + n),
        out_specs=(SEM_SPEC, SEM_SPEC, *[HBM_SPEC] * (ns + n), pl.BlockSpec(memory_space=pltpu.VMEM)),
        input_output_aliases={i: 2 + i for i in range(ns + n)},
        compiler_params=pltpu.CompilerParams(has_side_effects=pltpu.SideEffectType.DATAFLOW_SIDE_EFFECTING),
    )(*[pltpu.with_memory_space_constraint(b, pltpu.HBM) for b in bufs])
    return out[0], out[1], list(out[2:2 + ns]), list(out[2 + ns:2 + ns + n]), out[-1]


def _spread_wait(send_sems, recv_sems, srcs, lands, after, name):
    ns, n = len(srcs), len(lands)

    def body(*refs):
        src_refs, land_refs = refs[:ns], refs[ns:ns + n]
        send_sems, recv_sems = refs[ns + n], refs[ns + n + 1]
        for cp in _spread_copies(src_refs, land_refs, send_sems, recv_sems, ns > 0):
            cp.wait_send()
            cp.wait_recv()

    bufs = list(srcs) + list(lands)
    out = pl.pallas_call(
        body, name=name,
        out_shape=tuple(pltpu.HBM(b.shape, b.dtype) for b in bufs),
        in_specs=[HBM_SPEC] * (ns + n) + [SEM_SPEC, SEM_SPEC, pl.BlockSpec(memory_space=pl.ANY)],
        out_specs=tuple([HBM_SPEC] * (ns + n)),
        input_output_aliases={i: i for i in range(ns + n)},
        compiler_params=pltpu.CompilerParams(has_side_effects=pltpu.SideEffectType.DATAFLOW_SIDE_EFFECTING),
    )(*bufs, send_sems, recv_sems, after)
    return list(out[ns:])


def _own_slot_only(block, me):
    return lax.dynamic_update_slice(lax.empty((N_DEV,) + block.shape, block.dtype), block[None], (me, 0, 0))


def _sum_slots(landed, name):
    _, R, C = landed.shape
    tb = _row_tile(R, 128)

    def body(l_ref, o_ref):
        acc = l_ref[0].astype(F32)
        for s in range(1, N_DEV):
            acc = acc + l_ref[s].astype(F32)
        o_ref[...] = acc

    return pl.pallas_call(
        body, name=name, grid=(R // tb,),
        in_specs=[pl.BlockSpec((N_DEV, tb, C), lambda i: (0, i, 0))],
        out_specs=pl.BlockSpec((tb, C), lambda i: (i, 0)),
        out_shape=jax.ShapeDtypeStruct((R, C), F32),
    )(landed)


def _pack_rows(flat_list, width=LANES):
    flat = jnp.concatenate([a.reshape(-1) for a in flat_list])
    n = flat.shape[0]
    rows = -(-n // width)
    rows = -(-rows // 8) * 8
    return jnp.pad(flat, (0, rows * width - n)).reshape(rows, width)


def _unpack(packed, shapes):
    flat = packed.reshape(-1)
    out, off = [], 0
    for s in shapes:
        n = 1
        for d in s:
            n *= d
        out.append(flat[off:off + n].reshape(s))
        off += n
    return out


def kernel(x, norm1_w, w_in, hg_lb_logits, hg_norm_w, rw_shift_mu, rw_w0, rw_w2, rw_a0, rw_a2, rw_g2, rw_k_k, rw_k_a, rw_r_k, rw_ln_w, rw_ln_b, w_out, norm2_w, w_up, conv_w, conv_b, w_down, final_norm_w, loss_target, m_norm1_w, m_w_in, m_hg_lb_logits, m_hg_norm_w, m_rw_shift_mu, m_rw_w0, m_rw_w2, m_rw_a0, m_rw_a2, m_rw_g2, m_rw_k_k, m_rw_k_a, m_rw_r_k, m_rw_ln_w, m_rw_ln_b, m_w_out, m_norm2_w, m_w_up, m_conv_w, m_conv_b, m_w_down, m_final_norm_w, v_norm1_w, v_w_in, v_hg_lb_logits, v_hg_norm_w, v_rw_shift_mu, v_rw_w0, v_rw_w2, v_rw_a0, v_rw_a2, v_rw_g2, v_rw_k_k, v_rw_k_a, v_rw_r_k, v_rw_ln_w, v_rw_ln_b, v_w_out, v_norm2_w, v_w_up, v_conv_w, v_conv_b, v_w_down, v_final_norm_w):
    weights = dict(norm1_w=norm1_w, w_in=w_in, hg_lb_logits=hg_lb_logits, hg_norm_w=hg_norm_w,
                   rw_shift_mu=rw_shift_mu, rw_w0=rw_w0, rw_w2=rw_w2, rw_a0=rw_a0, rw_a2=rw_a2, rw_g2=rw_g2,
                   rw_k_k=rw_k_k, rw_k_a=rw_k_a, rw_r_k=rw_r_k, rw_ln_w=rw_ln_w, rw_ln_b=rw_ln_b, w_out=w_out,
                   norm2_w=norm2_w, w_up=w_up, conv_w=conv_w, conv_b=conv_b, w_down=w_down,
                   final_norm_w=final_norm_w)
    m_in = dict(norm1_w=m_norm1_w, w_in=m_w_in, hg_lb_logits=m_hg_lb_logits, hg_norm_w=m_hg_norm_w,
                rw_shift_mu=m_rw_shift_mu, rw_w0=m_rw_w0, rw_w2=m_rw_w2, rw_a0=m_rw_a0, rw_a2=m_rw_a2,
                rw_g2=m_rw_g2, rw_k_k=m_rw_k_k, rw_k_a=m_rw_k_a, rw_r_k=m_rw_r_k, rw_ln_w=m_rw_ln_w,
                rw_ln_b=m_rw_ln_b, w_out=m_w_out, norm2_w=m_norm2_w, w_up=m_w_up, conv_w=m_conv_w,
                conv_b=m_conv_b, w_down=m_w_down, final_norm_w=m_final_norm_w)
    v_in = dict(norm1_w=v_norm1_w, w_in=v_w_in, hg_lb_logits=v_hg_lb_logits, hg_norm_w=v_hg_norm_w,
                rw_shift_mu=v_rw_shift_mu, rw_w0=v_rw_w0, rw_w2=v_rw_w2, rw_a0=v_rw_a0, rw_a2=v_rw_a2,
                rw_g2=v_rw_g2, rw_k_k=v_rw_k_k, rw_k_a=v_rw_k_a, rw_r_k=v_rw_r_k, rw_ln_w=v_rw_ln_w,
                rw_ln_b=v_rw_ln_b, w_out=v_w_out, norm2_w=v_norm2_w, w_up=v_w_up, conv_w=v_conv_w,
                conv_b=v_conv_b, w_down=v_w_down, final_norm_w=v_final_norm_w)
    names = list(weights)
    sharded_small = ["rw_w2", "rw_a2", "rw_g2", "conv_w"]
    sharded_big = ["w_in", "w_out", "w_up", "w_down"]
    replicated = [n for n in names if n not in sharded_small + sharded_big]

    xs = x[0]
    tgt = loss_target[0]

    small_shard = _pack_rows([weights[n] for n in sharded_small])
    g_win_t, g_small = _all_gather([w_in[0].T.astype(BF16), small_shard], "gather_weights")
    me = _my_index()
    later = (w_up[0].T.astype(BF16), w_out[0].astype(BF16), w_down[0].astype(BF16))
    later, _ = lax.optimization_barrier((later, g_small))
    later = [_own_slot_only(z, me) for z in later]
    g_send, g_recv, _, later, g_token = _spread_start([], later, "gather_later_start")
    small_shapes = [weights[n].shape for n in sharded_small]
    per_dev = [_unpack(g_small.reshape(N_DEV, -1)[j], small_shapes) for j in range(N_DEV)]
    w2_full, a2_full, g2_full, convw_full = [jnp.concatenate([per_dev[j][i][0] for j in range(N_DEV)], axis=-1)
                                             for i in range(4)]
    zeros64 = jnp.zeros((RW_PAIRS, 64, LANES), F32)
    by_pair = lambda z: z.reshape(z.shape[0], RW_PAIRS, LANES).transpose(1, 0, 2)
    w2p = jnp.concatenate([by_pair(w2_full), zeros64], axis=1)
    a2p = jnp.concatenate([zeros64, by_pair(a2_full)], axis=1)
    g2p = by_pair(g2_full)

    l0, l1 = hg_lb_logits[0:1], hg_lb_logits[1:2]
    h1 = _rms_fwd(xs, norm1_w + g_token[0:1, 0:1], "norm1")
    proj = _mm_nt(h1, g_win_t, "proj_in")
    o_hg, hg_states = _hg_fwd(proj, l0, l1, hg_norm_w, "hgrn2_fwd")
    rws = _shift_fwd(proj, rw_shift_mu, "token_shift")
    rw_vecs = [rw_w0, rw_a0, rw_k_k, rw_k_a, rw_r_k, rw_ln_w, rw_ln_b]
    rw_mats = [w2p, a2p, g2p]
    o_rw, rw_states = _rw_fwd(rws, rw_vecs, rw_mats, "rwkv7_fwd")
    o_mix = jnp.concatenate([o_hg, o_rw], axis=-1).astype(BF16)
    g_wup_t, g_wout, g_wdown = [z.reshape(-1, z.shape[-1])
                                for z in _spread_wait(g_send, g_recv, [], later, o_mix, "gather_later_wait")]
    x1 = _mm_nn(o_mix, g_wout, xs, "proj_out")
    h2 = _rms_fwd(x1, norm2_w, "norm2")
    u = _mm_nt(h2, g_wup_t, "ffn_up")
    act = _ffn_act_fwd(u, convw_full, conv_b, "ffn_act")
    x2 = _mm_nn(act, g_wdown, x1, "ffn_down")
    loss_part, dx2, d_final_w = _loss_head(x2, final_norm_w.reshape(1, -1), tgt, "loss_head")

    d_wdown = _mm_tn(act, dx2, 1408, "ffn_down_dw", BF16)
    dact = _mm_nt(dx2, g_wdown, "ffn_down_dx", BF16)
    du_g, du_v, dcw_g, dcw_v, dcb_g, dcb_v = _ffn_act_bwd(u, dact, convw_full, conv_b, "ffn_act_bwd")
    d_convw = jnp.concatenate([dcw_g, dcw_v], axis=-1)
    d_convb = jnp.concatenate([dcb_g, dcb_v], axis=-1)
    d_wup_t = jnp.concatenate([_mm_tn(du_g, h2, 1408, "ffn_up_dw_gate", BF16),
                               _mm_tn(du_v, h2, 1408, "ffn_up_dw_value", BF16)], axis=0)
    dh2 = _mm_nn([du_g, du_v], g_wup_t, None, "ffn_up_dx")
    dx1, d_norm2 = _rms_bwd(dh2, x1, norm2_w, dx2, "norm2_bwd")
    d_wout = _mm_tn(o_mix, dx1, 512, "proj_out_dw", BF16)
    do = _mm_nt(dx1, g_wout, "proj_out_dx")
    early = [z.reshape(N_DEV, z.shape[0] // N_DEV, z.shape[1]) for z in (d_wup_t, d_wout, d_wdown)]
    early_land = [_own_slot_only(lax.dynamic_index_in_dim(z, me, 0, keepdims=False), me) for z in early]
    e_send, e_recv, early, early_land, e_token = _spread_start(early, early_land, "exchange_early_start")
    hg_norm_w_t = hg_norm_w + e_token[0:1, 0:1]
    dq, df, di, dg, d_l0, d_l1, d_hg_nw = _hg_bwd(proj, hg_states, do, 0, l0, l1, hg_norm_w_t, "hgrn2_bwd")
    rw_out = _rw_bwd(rws, rw_states, do, 1, rw_vecs, rw_mats, "rwkv7_bwd")
    d_rw_vecs = rw_out[4:4 + N_RW_VEC]
    d_w2p, d_a2p, d_g2p = rw_out[4 + N_RW_VEC:]
    dp_parts, dmu_parts = [], []
    for i, z in enumerate(rw_out[:4]):
        dp, dmu = _shift_bwd(z, proj, rw_shift_mu, i * RW_WIDTH, "token_shift_bwd_%d" % i)
        dp_parts.append(dp)
        dmu_parts.append(dmu)
    d_mu = jnp.concatenate(dmu_parts, axis=-1)
    dproj = jnp.concatenate([dq, df, di, dg] + dp_parts, axis=-1)
    d_win_t = _mm_tn(dproj, h1, 768, "proj_in_dw", BF16)
    from_pairs = lambda z: z.transpose(1, 0, 2).reshape(z.shape[1], RW_WIDTH)
    d_w2 = from_pairs(d_w2p[:, :64])
    d_a2 = from_pairs(d_a2p[:, 64:])
    d_g2 = from_pairs(d_g2p)
    col_blocks = lambda z: z.reshape(z.shape[0], N_DEV, -1).transpose(1, 0, 2)
    small_part = jnp.stack([
        _pack_rows([col_blocks(d_w2)[j], col_blocks(d_a2)[j], col_blocks(d_g2)[j], col_blocks(d_convw)[j]])
        for j in range(N_DEV)])
    late = [d_win_t.reshape(N_DEV, d_win_t.shape[0] // N_DEV, d_win_t.shape[1]), small_part]
    late_land = [_own_slot_only(lax.dynamic_index_in_dim(z, me, 0, keepdims=False), me) for z in late]
    l_send, l_recv, late, late_land, l_token = _spread_start(late, late_land, "exchange_late_start")
    dh1 = _mm_nn(dproj, g_win_t, None, "proj_in_dx")
    grad_x, d_norm1 = _rms_bwd(dh1, xs, norm1_w + l_token[0:1, 0:1], dx1, "norm1_bwd")

    rep_grads = dict(norm1_w=d_norm1, hg_lb_logits=jnp.concatenate([d_l0, d_l1], axis=0), hg_norm_w=d_hg_nw,
                     rw_shift_mu=d_mu, rw_w0=d_rw_vecs[0], rw_a0=d_rw_vecs[1], rw_k_k=d_rw_vecs[2],
                     rw_k_a=d_rw_vecs[3], rw_r_k=d_rw_vecs[4], rw_ln_w=d_rw_vecs[5], rw_ln_b=d_rw_vecs[6],
                     norm2_w=d_norm2, conv_b=d_convb, final_norm_w=d_final_w)
    rep_pack = _pack_rows([loss_part] + [rep_grads[n] for n in replicated])
    rep_part = jnp.broadcast_to(rep_pack[None], (N_DEV,) + rep_pack.shape)
    grads, delta, new_m, new_v = {}, {}, {}, {}

    def adamw_big(n, g):
        shp = weights[n].shape
        as2d = lambda z: z.reshape(shp[1], shp[2])
        grads[n] = g[None]
        d, nm, nv = _adamw(as2d(weights[n]), g, as2d(m_in[n]), as2d(v_in[n]), "adamw_" + n)
        delta[n], new_m[n], new_v[n] = d.reshape(shp), nm.reshape(shp), nv.reshape(shp)

    landed_early = _spread_wait(e_send, e_recv, early, early_land, grad_x, "exchange_early_wait")
    adamw_big("w_up", _sum_slots(landed_early[0], "sum_grads_w_up").T)
    adamw_big("w_out", _sum_slots(landed_early[1], "sum_grads_w_out"))
    adamw_big("w_down", _sum_slots(landed_early[2], "sum_grads_w_down"))
    landed_late = _spread_wait(l_send, l_recv, late, late_land, delta["w_down"], "exchange_late_wait")
    adamw_big("w_in", _sum_slots(landed_late[0], "sum_grads_w_in").T)
    (landed_rep,) = _exchange([rep_part], "exchange_grads")
    g_small_sum = _unpack(_sum_slots(landed_late[1], "sum_grads_small"), small_shapes)
    rep_sum = _unpack(_sum_slots(landed_rep, "sum_grads_replicated"), [(1, 1)] + [weights[n].shape for n in replicated])
    loss = rep_sum[0].reshape(())
    grads.update(dict(zip(replicated, rep_sum[1:])))
    grads.update(dict(zip(sharded_small, g_small_sum)))

    small_names = replicated + sharded_small
    packs = [_pack_rows([src[n] for n in small_names]) for src in (weights, grads, m_in, v_in)]
    outs = _adamw(*packs, "adamw_small")
    small_shapes_all = [weights[n].shape for n in small_names]
    for dst, packed in zip((delta, new_m, new_v), outs):
        dst.update(dict(zip(small_names, _unpack(packed, small_shapes_all))))

    return (loss, grad_x[None], *[grads[n] for n in names], *[delta[n] for n in names],
            *[new_m[n] for n in names], *[new_v[n] for n in names])
```

```python
import functools

import jax
import jax.numpy as jnp
from jax import lax
from jax.experimental import pallas as pl
from jax.experimental.pallas import tpu as pltpu

F32 = jnp.float32
BF16 = jnp.bfloat16
HIGHEST = lax.Precision.HIGHEST
SCAN_PRECISION = None
MESH_ID = pl.DeviceIdType.MESH

N_DEV = 8
D_MODEL = 1024
HG_WIDTH = 512
HG_HEAD_DIM = 128
HG_HEADS = 4
RW_WIDTH = 512
RW_PAIRS = 4
RW_HEAD_DIM = 64
HG_COLS = 2048
RW_COLS = 1792
D_FF = 2816
NORM_EPS = 1e-6
RW_GN_EPS = 64e-5
L2_EPS = 1e-12
ADAM_LR, ADAM_B1, ADAM_B2, ADAM_EPS, ADAM_WD, ADAM_STEP = 0.001, 0.9, 0.999, 1e-08, 0.01, 10

HG_CHUNK = 32
HG_HALF = 16
RW_CHUNK = 64
SCAN_ROWS = 256
LANES = 128

NN = ((1,), (0,))
NT = ((1,), (1,))
TN = ((0,), (0,))


def _dot(a, b, dims=NN, precision=SCAN_PRECISION):
    if precision is None:
        a, b = a.astype(BF16), b.astype(BF16)
    return lax.dot_general(a, b, (dims, ((), ())), precision=precision, preferred_element_type=F32)


def _iota2(shape, dim):
    return lax.broadcasted_iota(jnp.int32, shape, dim)


def _sigmoid(z):
    return 0.5 * jnp.tanh(0.5 * z) + 0.5


def _row_tile(n, want):
    t = min(n, want)
    while n % t:
        t //= 2
    return t


def _rms_fwd(x, w, name):
    T, D = x.shape
    tb = _row_tile(T, 512)

    def body(x_ref, w_ref, h_ref):
        xv = x_ref[...]
        r = lax.rsqrt(jnp.mean(xv * xv, axis=-1, keepdims=True) + NORM_EPS)
        h_ref[...] = (xv * r * w_ref[...]).astype(h_ref.dtype)

    return pl.pallas_call(
        body, name=name, grid=(T // tb,),
        in_specs=[pl.BlockSpec((tb, D), lambda i: (i, 0)), pl.BlockSpec((1, D), lambda i: (0, 0))],
        out_specs=pl.BlockSpec((tb, D), lambda i: (i, 0)),
        out_shape=jax.ShapeDtypeStruct((T, D), BF16),
    )(x, w)


def _rms_bwd(dh, x, w, dres, name):
    T, D = x.shape
    tb = _row_tile(T, 256)

    def body(dh_ref, x_ref, w_ref, dres_ref, dx_ref, dw_ref):
        @pl.when(pl.program_id(0) == 0)
        def _():
            dw_ref[...] = jnp.zeros_like(dw_ref)

        xv = x_ref[...]
        r = lax.rsqrt(jnp.mean(xv * xv, axis=-1, keepdims=True) + NORM_EPS)
        xn = xv * r
        dy = dh_ref[...].astype(F32)
        dxn = dy * w_ref[...]
        dx_ref[...] = dres_ref[...] + r * (dxn - xn * jnp.mean(dxn * xn, axis=-1, keepdims=True))
        dw_ref[...] += jnp.sum(dy * xn, axis=0, keepdims=True)

    row = pl.BlockSpec((tb, D), lambda i: (i, 0))
    vec = pl.BlockSpec((1, D), lambda i: (0, 0))
    return pl.pallas_call(
        body, name=name, grid=(T // tb,),
        in_specs=[row, row, vec, row], out_specs=[row, vec],
        out_shape=[jax.ShapeDtypeStruct((T, D), F32), jax.ShapeDtypeStruct((1, D), F32)],
    )(dh, x, w, dres)


def _mm_nt(a, bt, name, out_dtype=F32):
    T, K = a.shape
    N = bt.shape[0]
    tm = _row_tile(T, 256)

    def body(a_ref, b_ref, o_ref):
        o_ref[...] = _dot(a_ref[...].astype(BF16), b_ref[...].astype(BF16), NT, None).astype(o_ref.dtype)

    return pl.pallas_call(
        body, name=name, grid=(T // tm,),
        in_specs=[pl.BlockSpec((tm, K), lambda i: (i, 0)), pl.BlockSpec((N, K), lambda i: (0, 0))],
        out_specs=pl.BlockSpec((tm, N), lambda i: (i, 0)),
        out_shape=jax.ShapeDtypeStruct((T, N), out_dtype),
    )(a, bt)


def _mm_nn(a, b, res, name, out_dtype=F32):
    parts = list(a) if isinstance(a, (list, tuple)) else [a]
    T = parts[0].shape[0]
    K, N = b.shape
    tm = _row_tile(T, 256)
    widths = [p.shape[1] for p in parts]
    n = len(parts)

    def body(*refs):
        b_ref, o_ref = refs[n], refs[-1]
        acc, off = None, 0
        for a_ref, w in zip(refs[:n], widths):
            d = _dot(a_ref[...].astype(BF16), b_ref[off:off + w, :].astype(BF16), NN, None)
            acc = d if acc is None else acc + d
            off += w
        if res is not None:
            acc = acc + refs[n + 1][...]
        o_ref[...] = acc.astype(o_ref.dtype)

    in_specs = [pl.BlockSpec((tm, w), lambda i: (i, 0)) for w in widths] + [pl.BlockSpec((K, N), lambda i: (0, 0))]
    args = parts + [b]
    if res is not None:
        in_specs.append(pl.BlockSpec((tm, N), lambda i: (i, 0)))
        args.append(res)
    return pl.pallas_call(
        body, name=name, grid=(T // tm,), in_specs=in_specs,
        out_specs=pl.BlockSpec((tm, N), lambda i: (i, 0)),
        out_shape=jax.ShapeDtypeStruct((T, N), out_dtype),
    )(*args)


def _mm_tn(a, b, tmm, name, out_dtype=F32):
    T, M = a.shape
    N = b.shape[1]
    tk = _row_tile(T, 512)
    nk = T // tk

    def body(a_ref, b_ref, o_ref, acc_ref):
        @pl.when(pl.program_id(1) == 0)
        def _():
            acc_ref[...] = jnp.zeros_like(acc_ref)

        acc_ref[...] += _dot(a_ref[...].astype(BF16), b_ref[...].astype(BF16), TN, None)

        @pl.when(pl.program_id(1) == nk - 1)
        def _():
            o_ref[...] = acc_ref[...].astype(o_ref.dtype)

    return pl.pallas_call(
        body, name=name, grid=(M // tmm, nk),
        in_specs=[pl.BlockSpec((tk, tmm), lambda m, k: (k, m)), pl.BlockSpec((tk, N), lambda m, k: (k, 0))],
        out_specs=pl.BlockSpec((tmm, N), lambda m, k: (m, 0)),
        out_shape=jax.ShapeDtypeStruct((M, N), out_dtype),
        scratch_shapes=[pltpu.VMEM((tmm, N), F32)],
    )(a, b)


class _RowShifts:
    def __init__(self, shape):
        index = _iota2(shape, 0)
        self.rows = shape[0]
        self.first = {n: index < n for n in (1, 2)}
        self.last = {n: index >= shape[0] - n for n in (1, 2)}

    def down(self, z, n):
        return jnp.where(self.first[n], 0.0, pltpu.roll(z, n, 0))

    def up(self, z, n):
        return jnp.where(self.last[n], 0.0, pltpu.roll(z, self.rows - n, 0))


def _shift_fwd(proj, mu, name):
    T = proj.shape[0]
    nblk = RW_COLS // LANES
    first = HG_COLS // LANES

    def body(p_ref, mu_ref, o_ref):
        p = p_ref[...]
        o_ref[...] = p + (_RowShifts(p.shape).down(p, 1) - p) * mu_ref[...]

    return pl.pallas_call(
        body, name=name, grid=(nblk,),
        in_specs=[pl.BlockSpec((T, LANES), lambda j: (0, first + j)), pl.BlockSpec((1, LANES), lambda j: (0, j))],
        out_specs=pl.BlockSpec((T, LANES), lambda j: (0, j)),
        out_shape=jax.ShapeDtypeStruct((T, RW_COLS), F32),
    )(proj, mu)


def _shift_bwd(ds, proj, mu, col0, name):
    T, width = ds.shape
    nblk = width // LANES
    first = (HG_COLS + col0) // LANES
    mu0 = col0 // LANES

    def body(ds_ref, p_ref, mu_ref, dp_ref, dmu_ref):
        dsv = ds_ref[...]
        p = p_ref[...]
        m = mu_ref[...]
        shifts = _RowShifts(p.shape)
        dp_ref[...] = (dsv * (1.0 - m) + shifts.up(dsv * m, 1)).astype(dp_ref.dtype)
        dmu_ref[...] = jnp.sum(dsv * (shifts.down(p, 1) - p), axis=0, keepdims=True)

    return pl.pallas_call(
        body, name=name, grid=(nblk,),
        in_specs=[pl.BlockSpec((T, LANES), lambda j: (0, j)),
                  pl.BlockSpec((T, LANES), lambda j: (0, first + j)),
                  pl.BlockSpec((1, LANES), lambda j: (0, mu0 + j))],
        out_specs=[pl.BlockSpec((T, LANES), lambda j: (0, j)), pl.BlockSpec((1, LANES), lambda j: (0, j))],
        out_shape=[jax.ShapeDtypeStruct((T, width), BF16), jax.ShapeDtypeStruct((1, width), F32)],
    )(ds, proj, mu)


def _conv3(z, w_ref, shifts):
    return w_ref[0:1, :] * shifts.down(z, 2) + w_ref[1:2, :] * shifts.down(z, 1) + w_ref[2:3, :] * z


def _ffn_act_fwd(u, conv_w, conv_b, name):
    T = u.shape[0]
    nblk = D_FF // LANES

    def body(ug_ref, uv_ref, wg_ref, wv_ref, bg_ref, bv_ref, act_ref):
        shifts = _RowShifts((T, LANES))
        gate = _conv3(ug_ref[...], wg_ref, shifts) + bg_ref[...]
        val = _conv3(uv_ref[...], wv_ref, shifts) + bv_ref[...]
        act_ref[...] = (gate * _sigmoid(gate) * val).astype(act_ref.dtype)

    col = lambda off: pl.BlockSpec((T, LANES), lambda j: (0, off + j))
    wsp = lambda off: pl.BlockSpec((3, LANES), lambda j: (0, off + j))
    bsp = lambda off: pl.BlockSpec((1, LANES), lambda j: (0, off + j))
    return pl.pallas_call(
        body, name=name, grid=(nblk,),
        in_specs=[col(0), col(nblk), wsp(0), wsp(nblk), bsp(0), bsp(nblk)],
        out_specs=pl.BlockSpec((T, LANES), lambda j: (0, j)),
        out_shape=jax.ShapeDtypeStruct((T, D_FF), BF16),
    )(u, u, conv_w, conv_w, conv_b, conv_b)


def _ffn_act_bwd(u, dact, conv_w, conv_b, name):
    T = u.shape[0]
    nblk = D_FF // LANES

    def conv_bwd(z, dzc, w_ref, du_ref, dw_ref, db_ref, shifts):
        up1, up2 = shifts.up(dzc, 1), shifts.up(dzc, 2)
        du = w_ref[2:3, :] * dzc + w_ref[1:2, :] * up1 + w_ref[0:1, :] * up2
        du_ref[...] = du.astype(du_ref.dtype)
        dw_ref[0:1, :] = jnp.sum(up2 * z, axis=0, keepdims=True)
        dw_ref[1:2, :] = jnp.sum(up1 * z, axis=0, keepdims=True)
        dw_ref[2:3, :] = jnp.sum(dzc * z, axis=0, keepdims=True)
        db_ref[...] = jnp.sum(dzc, axis=0, keepdims=True)

    def body(ug_ref, uv_ref, da_ref, wg_ref, wv_ref, bg_ref, bv_ref,
             dug_ref, duv_ref, dwg_ref, dwv_ref, dbg_ref, dbv_ref):
        ug, uv = ug_ref[...], uv_ref[...]
        shifts = _RowShifts((T, LANES))
        gate = _conv3(ug, wg_ref, shifts) + bg_ref[...]
        val = _conv3(uv, wv_ref, shifts) + bv_ref[...]
        da = da_ref[...].astype(F32)
        sg = _sigmoid(gate)
        dgate = da * val * (sg * (1.0 + gate * (1.0 - sg)))
        dval = da * gate * sg
        conv_bwd(ug, dgate, wg_ref, dug_ref, dwg_ref, dbg_ref, shifts)
        conv_bwd(uv, dval, wv_ref, duv_ref, dwv_ref, dbv_ref, shifts)

    col = lambda off: pl.BlockSpec((T, LANES), lambda j: (0, off + j))
    wsp = lambda off: pl.BlockSpec((3, LANES), lambda j: (0, off + j))
    bsp = lambda off: pl.BlockSpec((1, LANES), lambda j: (0, off + j))
    half = lambda r, dt: jax.ShapeDtypeStruct((r, D_FF), dt)
    return pl.pallas_call(
        body, name=name, grid=(nblk,),
        in_specs=[col(0), col(nblk), col(0), wsp(0), wsp(nblk), bsp(0), bsp(nblk)],
        out_specs=[col(0), col(0), wsp(0), wsp(0), bsp(0), bsp(0)],
        out_shape=[half(T, BF16), half(T, BF16), half(3, F32), half(3, F32), half(1, F32), half(1, F32)],
    )(u, u, dact, conv_w, conv_w, conv_b, conv_b)


def _loss_head(x2, w, target, name):
    T, D = x2.shape
    tb = _row_tile(T, 256)

    def body(x_ref, w_ref, t_ref, loss_ref, dx_ref, dw_ref):
        @pl.when(pl.program_id(0) == 0)
        def _():
            loss_ref[...] = jnp.zeros_like(loss_ref)
            dw_ref[...] = jnp.zeros_like(dw_ref)

        xv = x_ref[...]
        r = lax.rsqrt(jnp.mean(xv * xv, axis=-1, keepdims=True) + NORM_EPS)
        xn = xv * r
        err = xn * w_ref[...] - t_ref[...]
        row_loss = jnp.sum(err * err, axis=-1, keepdims=True) * (0.5 / D)
        loss_ref[...] += jnp.sum(row_loss, axis=0, keepdims=True)
        dy = err * (1.0 / D)
        dxn = dy * w_ref[...]
        dx_ref[...] = r * (dxn - xn * jnp.mean(dxn * xn, axis=-1, keepdims=True))
        dw_ref[...] += jnp.sum(dy * xn, axis=0, keepdims=True)

    row = pl.BlockSpec((tb, D), lambda i: (i, 0))
    vec = pl.BlockSpec((1, D), lambda i: (0, 0))
    return pl.pallas_call(
        body, name=name, grid=(T // tb,),
        in_specs=[row, vec, row],
        out_specs=[pl.BlockSpec((1, 1), lambda i: (0, 0)), row, vec],
        out_shape=[jax.ShapeDtypeStruct((1, 1), F32), jax.ShapeDtypeStruct((T, D), F32),
                   jax.ShapeDtypeStruct((1, D), F32)],
    )(x2, w, target)


def _adamw(w, g, m, v, name):
    R, C = w.shape
    tb = _row_tile(R, 256) if R % 8 == 0 else R

    def body(w_ref, g_ref, m_ref, v_ref, d_ref, nm_ref, nv_ref):
        gv = g_ref[...]
        nm = ADAM_B1 * m_ref[...] + (1.0 - ADAM_B1) * gv
        nv = ADAM_B2 * v_ref[...] + (1.0 - ADAM_B2) * (gv * gv)
        m_hat = nm / (1.0 - ADAM_B1 ** ADAM_STEP)
        v_hat = nv / (1.0 - ADAM_B2 ** ADAM_STEP)
        d_ref[...] = -ADAM_LR * (m_hat / (jnp.sqrt(v_hat) + ADAM_EPS) + ADAM_WD * w_ref[...])
        nm_ref[...] = nm
        nv_ref[...] = nv

    blk = pl.BlockSpec((tb, C), lambda i: (i, 0))
    sd = jax.ShapeDtypeStruct((R, C), F32)
    return pl.pallas_call(
        body, name=name, grid=(R // tb,), in_specs=[blk] * 4, out_specs=[blk] * 3, out_shape=[sd] * 3,
    )(w, g, m, v)


def _chunk_masks(rows, chunk):
    shift = chunk.bit_length() - 1
    i, j = _iota2((rows, rows), 0), _iota2((rows, rows), 1)
    same = jnp.right_shift(i, shift) == jnp.right_shift(j, shift)
    return same.astype(F32), (same & (j <= i)).astype(F32), (same & (j < i)).astype(F32)


def _head_lanes(h):
    return slice(h * LANES, (h + 1) * LANES)


def _chunk_rows(c, chunk):
    return pl.ds(pl.multiple_of(c * chunk, chunk), chunk)


def _hg_consts(rows):
    same, tril, _ = _chunk_masks(rows, HG_CHUNK)
    half_same, half_tril, _ = _chunk_masks(rows, HG_HALF)
    i, j = _iota2((rows, rows), 0), _iota2((rows, rows), 1)
    half_shift, shift = HG_HALF.bit_length() - 1, HG_CHUNK.bit_length() - 1
    mid_row = jnp.left_shift(jnp.right_shift(i, half_shift), half_shift) + (HG_HALF // 2 - 1)
    bound_row = jnp.left_shift(jnp.right_shift(i, shift), shift) + (HG_HALF - 1)
    upto_mid = ((same > 0) & (j <= mid_row)).astype(F32)
    upto_bound = ((same > 0) & (j <= bound_row)).astype(F32)
    lower_left = tril * (1.0 - half_same)
    return jnp.concatenate([tril, same, upto_mid, upto_bound], axis=0), half_tril, lower_left


N_HG_IN = 5


def _hg_prep(consts, *flat):
    sums, half_tril, lower_left = consts
    rows = half_tril.shape[0]
    heads, logs = [], []
    for h in range(len(flat) // N_HG_IN):
        qr, fr, ir, l0, l1 = flat[N_HG_IN * h:N_HG_IN * (h + 1)]
        lb = _sigmoid(l0 - l1)
        f = lb + (1.0 - lb) * _sigmoid(fr)
        heads.append((qr * _sigmoid(qr) * (HG_HEAD_DIM ** -0.5), 1.0 - f, ir))
        logs.append(jnp.log(f))
    acc = _dot(sums, jnp.concatenate(logs, axis=1), NN, HIGHEST)
    sums_of = []
    for h in range(len(heads)):
        acc_h = acc[:, h * LANES:(h + 1) * LANES]
        sums_of.append(tuple(acc_h[n * rows:(n + 1) * rows] for n in range(4)))
    near = [_dot(q * jnp.exp(a - mid), k * jnp.exp(mid - a), NT) * half_tril
            for (q, k, _), (a, _, mid, _) in zip(heads, sums_of)]
    far = [_dot(q * jnp.exp(jnp.minimum(a - bound, 0.0)), k * jnp.exp(jnp.minimum(bound - a, 0.0)), NT) * lower_left
           for (q, k, _), (a, _, _, bound) in zip(heads, sums_of)]
    intra = [_dot(n + f, ir) for n, f, (_, _, ir) in zip(near, far, heads)]
    return tuple((q * jnp.exp(a), o_intra, k * jnp.exp(tot - a), jnp.exp(tot))
                 for (q, k, _), (a, tot, _, _), o_intra in zip(heads, sums_of, intra))


def _hg_prep_args(q_ref, f_ref, i_ref, l0_ref, l1_ref):
    flat = []
    for h in range(HG_HEADS):
        ln = _head_lanes(h)
        flat += [q_ref[:, ln], f_ref[:, ln], i_ref[:, ln], l0_ref[:, ln], l1_ref[:, ln]]
    return flat


def _hg_post(o, gr, nw):
    on = o * lax.rsqrt(jnp.mean(o * o, axis=-1, keepdims=True) + NORM_EPS)
    return on * nw * (gr * _sigmoid(gr))


def _hg_specs(T, tb, rev):
    nT = T // tb
    tix = (lambda t: nT - 1 - t) if rev else (lambda t: t)
    col = lambda blk: pl.BlockSpec((tb, HG_WIDTH), lambda t: (tix(t), blk))
    vec = pl.BlockSpec((1, HG_WIDTH), lambda t: (0, 0))
    st = pl.BlockSpec((HG_HEADS, tb // HG_CHUNK, HG_HEAD_DIM, HG_HEAD_DIM), lambda t: (0, tix(t), 0, 0))
    return nT, col, vec, st


def _hg_fwd(proj, l0, l1, nw, name):
    T = proj.shape[0]
    tb = _row_tile(T, SCAN_ROWS)
    nsub = tb // HG_CHUNK
    nT, col, vec, st = _hg_specs(T, tb, False)

    def body(q_ref, f_ref, i_ref, g_ref, l0_ref, l1_ref, nw_ref, o_ref, st_ref, s_ref, qe_ref, kd_ref, dec_ref):
        @pl.when(pl.program_id(0) == 0)
        def _():
            s_ref[...] = jnp.zeros_like(s_ref)

        consts = _hg_consts(tb)
        outs = _hg_prep(consts, *_hg_prep_args(q_ref, f_ref, i_ref, l0_ref, l1_ref))
        for h, (qe, o_intra, kd, dec) in enumerate(outs):
            qe_ref[h], kd_ref[h], dec_ref[h] = qe, kd, dec
            o_ref[:, _head_lanes(h)] = o_intra

        def step(c, carry):
            rows = _chunk_rows(c, HG_CHUNK)
            for h in range(HG_HEADS):
                ln = _head_lanes(h)
                S = s_ref[h]
                st_ref[h, c] = S
                o_ref[rows, ln] += _dot(qe_ref[h, rows, :], S, NT)
                s_ref[h] = S * dec_ref[h, pl.ds(c * HG_CHUNK, 1), :] + _dot(i_ref[rows, ln], kd_ref[h, rows, :], TN)
            return carry

        lax.fori_loop(0, nsub, step, 0)
        for h in range(HG_HEADS):
            ln = _head_lanes(h)
            o_ref[:, ln] = _hg_post(o_ref[:, ln], g_ref[:, ln], nw_ref[:, ln])

    blk = pltpu.VMEM((HG_HEADS, tb, LANES), F32)
    return pl.pallas_call(
        body, name=name, grid=(nT,),
        in_specs=[col(0), col(1), col(2), col(3), vec, vec, vec],
        out_specs=[col(0), st],
        out_shape=[jax.ShapeDtypeStruct((T, HG_WIDTH), F32),
                   jax.ShapeDtypeStruct((HG_HEADS, T // HG_CHUNK, HG_HEAD_DIM, HG_HEAD_DIM), F32)],
        scratch_shapes=[pltpu.VMEM((HG_HEADS, HG_HEAD_DIM, HG_HEAD_DIM), F32), blk, blk, blk],
    )(proj, proj, proj, proj, l0, l1, nw)


def _hg_bwd(proj, states, do, do_blk, l0, l1, nw, name):
    T = proj.shape[0]
    tb = _row_tile(T, SCAN_ROWS)
    nsub = tb // HG_CHUNK
    nT, col, vec, st = _hg_specs(T, tb, True)

    def body(q_ref, f_ref, i_ref, g_ref, st_ref, do_ref, l0_ref, l1_ref, nw_ref,
             dq_ref, df_ref, di_ref, dg_ref, dl0_ref, dl1_ref, dnw_ref,
             ds_ref, qe_ref, kd_ref, dec_ref, o_ref, dqe_ref, dkd_ref, ddec_ref, dis_ref):
        @pl.when(pl.program_id(0) == 0)
        def _():
            ds_ref[...] = jnp.zeros_like(ds_ref)
            dl0_ref[...] = jnp.zeros_like(dl0_ref)
            dl1_ref[...] = jnp.zeros_like(dl1_ref)
            dnw_ref[...] = jnp.zeros_like(dnw_ref)

        consts = _hg_consts(tb)
        outs, prep_vjp = jax.vjp(functools.partial(_hg_prep, consts),
                                 *_hg_prep_args(q_ref, f_ref, i_ref, l0_ref, l1_ref))
        for h, (qe, o_intra, kd, dec) in enumerate(outs):
            qe_ref[h], kd_ref[h], dec_ref[h], o_ref[h] = qe, kd, dec, o_intra

        def redo(c, carry):
            rows = _chunk_rows(c, HG_CHUNK)
            for h in range(HG_HEADS):
                o_ref[h, rows, :] += _dot(qe_ref[h, rows, :], st_ref[h, c], NT)
            return carry

        lax.fori_loop(0, nsub, redo, 0)
        for h in range(HG_HEADS):
            ln = _head_lanes(h)
            _, vjp = jax.vjp(_hg_post, o_ref[h], g_ref[:, ln], nw_ref[:, ln])
            d_o, dgr, dnw = vjp(do_ref[:, ln])
            o_ref[h] = d_o
            dg_ref[:, ln] = dgr.astype(dg_ref.dtype)
            dnw_ref[:, ln] += dnw
        ddec_ref[...] = jnp.zeros_like(ddec_ref)

        def step(i, carry):
            c = nsub - 1 - i
            rows = _chunk_rows(c, HG_CHUNK)
            row0 = pl.ds(c * HG_CHUNK, 1)
            for h in range(HG_HEADS):
                ln = _head_lanes(h)
                G = ds_ref[h]
                S = st_ref[h, c]
                d_o = o_ref[h, rows, :]
                dqe_ref[h, rows, :] = _dot(d_o, S)
                dkd_ref[h, rows, :] = _dot(i_ref[rows, ln], G)
                dis_ref[h, rows, :] = _dot(kd_ref[h, rows, :], G, NT)
                ddec_ref[h, row0, :] = jnp.sum(S * G, axis=0, keepdims=True)
                ds_ref[h] = G * dec_ref[h, row0, :] + _dot(d_o, qe_ref[h, rows, :], TN)
            return carry

        lax.fori_loop(0, nsub, step, 0)
        grads = prep_vjp(tuple((dqe_ref[h], o_ref[h], dkd_ref[h], ddec_ref[h]) for h in range(HG_HEADS)))
        for h in range(HG_HEADS):
            ln = _head_lanes(h)
            dq, df, di, dl0, dl1 = grads[N_HG_IN * h:N_HG_IN * (h + 1)]
            dq_ref[:, ln] = dq.astype(dq_ref.dtype)
            df_ref[:, ln] = df.astype(df_ref.dtype)
            di_ref[:, ln] = (di + dis_ref[h]).astype(di_ref.dtype)
            dl0_ref[:, ln] += dl0
            dl1_ref[:, ln] += dl1

    dcol = jax.ShapeDtypeStruct((T, HG_WIDTH), BF16)
    dvec = jax.ShapeDtypeStruct((1, HG_WIDTH), F32)
    blk = pltpu.VMEM((HG_HEADS, tb, LANES), F32)
    return pl.pallas_call(
        body, name=name, grid=(nT,),
        in_specs=[col(0), col(1), col(2), col(3), st, col(do_blk), vec, vec, vec],
        out_specs=[col(0)] * 4 + [vec] * 3,
        out_shape=[dcol] * 4 + [dvec] * 3,
        scratch_shapes=[pltpu.VMEM((HG_HEADS, HG_HEAD_DIM, HG_HEAD_DIM), F32)] + [blk] * 8,
    )(proj, proj, proj, proj, states, do, l0, l1, nw)


def _rw_consts(rows):
    same, tril, stril = _chunk_masks(rows, RW_CHUNK)
    br, bc = _iota2((LANES, LANES), 0), _iota2((LANES, LANES), 1)
    blockdiag = ((br < RW_HEAD_DIM) == (bc < RW_HEAD_DIM)).astype(F32)
    m0 = (_iota2((1, LANES), 1) < RW_HEAD_DIM).astype(F32)
    return same, tril, stril, blockdiag, m0, 1.0 - m0


def _unit_lower_inverses_impl(lows):
    rows = lows[0].shape[0]
    eye = (_iota2(lows[0].shape, 0) == _iota2(lows[0].shape, 1)).astype(F32)
    xs = [low + eye for low in lows]
    ps = [_dot(low, low) for low in lows]
    n = 4
    while n < RW_CHUNK:
        zs = [_dot(jnp.concatenate([p, x], axis=0), p) for p, x in zip(ps, xs)]
        ps = [z[:rows] for z in zs]
        xs = [x + z[rows:] for x, z in zip(xs, zs)]
        n *= 2
    return tuple(x + _dot(x, p) for x, p in zip(xs, ps))


@jax.custom_vjp
def _unit_lower_inverses(lows):
    return _unit_lower_inverses_impl(lows)


def _unit_lower_inverses_fwd(lows):
    xs = _unit_lower_inverses_impl(lows)
    return xs, xs


def _unit_lower_inverses_bwd(xs, dxs):
    ts = [_dot(x, dx, TN) for x, dx in zip(xs, dxs)]
    return (tuple(_dot(t, x, NT) for t, x in zip(ts, xs)),)


_unit_lower_inverses.defvjp(_unit_lower_inverses_fwd, _unit_lower_inverses_bwd)


N_PREP_IN = 12
N_PREP_OUT = 9
RW_GROUP = 2


def _rw_prep(consts, *flat):
    same, tril, stril, blockdiag, m0, m1 = consts
    rows = tril.shape[0]
    masks = (m0, m1)
    pre = []
    for i in range(len(flat) // N_PREP_IN):
        r, kx, v, lw, gd, w0, a0, k_k, k_a, w2p, a2p, g2 = flat[N_PREP_IN * i:N_PREP_IN * (i + 1)]
        xw = w0 + _dot(jnp.tanh(lw), w2p)
        w = jnp.minimum(xw, 0.0) - jnp.log(1.0 + jnp.exp(-jnp.abs(xw))) - 0.5
        ld = -jnp.exp(w)
        a_s = _sigmoid(a0 + _dot(lw, a2p))
        g = _dot(_sigmoid(gd), g2)
        kk = kx * k_k
        kk = kk / jnp.maximum(jnp.sqrt(_dot(kk * kk, blockdiag)), L2_EPS)
        k2 = kx * (1.0 + (a_s - 1.0) * k_a)
        bv = kk * a_s
        acc = _dot(jnp.concatenate([tril, same], axis=0), ld, NN, HIGHEST)
        cum, tot = acc[:rows], acc[rows:]
        ecn = jnp.exp(-cum)
        a_t = -kk * jnp.exp(cum - ld)
        r_t = r * jnp.exp(cum)
        rem = jnp.exp(tot - cum)
        z = _dot(jnp.concatenate([a_t * m0, a_t * m1, r_t * m0, r_t * m1], axis=0),
                 jnp.concatenate([bv * ecn, k2 * ecn], axis=0), NT)
        pre.append((v, a_t, r_t, z, (bv * rem, k2 * rem, jnp.exp(tot), k2, g)))
    heads = [(i, h) for i in range(len(pre)) for h in range(2)]
    za = {ih: pre[ih[0]][3][ih[1] * rows:(ih[1] + 1) * rows] for ih in heads}
    zr = {ih: pre[ih[0]][3][(2 + ih[1]) * rows:(3 + ih[1]) * rows] for ih in heads}
    tinv = dict(zip(heads, _unit_lower_inverses(tuple(za[ih][:, :rows] * stril for ih in heads))))
    lv = {ih: _dot(jnp.concatenate([za[ih][:, rows:] * stril, zr[ih][:, rows:] * tril], axis=0), pre[ih[0]][0])
          for ih in heads}
    wu = {ih: _dot(tinv[ih], jnp.concatenate([pre[ih[0]][1] * masks[ih[1]], lv[ih][:rows]], axis=1)) for ih in heads}
    w_m = {ih: wu[ih][:, :LANES] for ih in heads}
    u_m = {ih: masks[ih[1]] * wu[ih][:, LANES:] for ih in heads}
    qy = {ih: _dot(zr[ih][:, :rows] * tril, jnp.concatenate([w_m[ih], u_m[ih]], axis=1)) for ih in heads}
    outs = []
    for i in range(len(pre)):
        a, b = (i, 0), (i, 1)
        W = w_m[a] + w_m[b]
        U = u_m[a] + u_m[b]
        Q = pre[i][2] + qy[a][:, :LANES] + qy[b][:, :LANES]
        Y0 = qy[a][:, LANES:] + qy[b][:, LANES:] + m0 * lv[a][rows:] + m1 * lv[b][rows:]
        outs.append((W, U, Q, Y0) + pre[i][4])
    return tuple(outs)


def _rw_post(blockdiag, y, r, v, k2, g, r_k, ln_w, ln_b):
    inv_n = 1.0 / RW_HEAD_DIM
    yc = y - _dot(y, blockdiag) * inv_n
    var = _dot(yc * yc, blockdiag) * inv_n
    yn = yc * lax.rsqrt(var + RW_GN_EPS) * ln_w + ln_b
    bonus = _dot(r * k2 * r_k, blockdiag) * v
    return (yn + bonus) * g


N_RW_VEC = 7
N_RW_MAT = 3


def _rw_specs(T, tb, rev):
    nT = T // tb
    tix = (lambda t: nT - 1 - t) if rev else (lambda t: t)
    wide = lambda blk: pl.BlockSpec((tb, RW_WIDTH), lambda t: (tix(t), blk))
    narrow = lambda blk: pl.BlockSpec((tb, LANES), lambda t: (tix(t), blk))
    vec = pl.BlockSpec((1, RW_WIDTH), lambda t: (0, 0))
    mat = pl.BlockSpec((RW_PAIRS, LANES, LANES), lambda t: (0, 0, 0))
    st = pl.BlockSpec((RW_PAIRS, tb // RW_CHUNK, LANES, LANES), lambda t: (0, tix(t), 0, 0))
    lora0 = 3 * RW_WIDTH // LANES
    ins = [wide(0), wide(1), wide(2), narrow(lora0), narrow(lora0 + 1)]
    return nT, wide, vec, mat, st, ins


def _rw_prep_args(p, r_ref, k_ref, v_ref, lw_ref, gd_ref, vrefs, mrefs):
    ln = _head_lanes(p)
    w0, a0, k_k, k_a = [x[:, ln] for x in vrefs[:4]]
    return (r_ref[:, ln], k_ref[:, ln], v_ref[:, ln], lw_ref[...], gd_ref[...], w0, a0, k_k, k_a,
            *[x[p] for x in mrefs])


def _stack_chunks(ref, top, bottom):
    C = RW_CHUNK
    for c in range(ref.shape[0]):
        ref[c, 0:C, :] = top[c * C:(c + 1) * C]
        ref[c, C:2 * C, :] = bottom[c * C:(c + 1) * C]


def _group_args(p0, r_ref, k_ref, v_ref, lw_ref, gd_ref, vrefs, mrefs):
    flat = []
    for p in range(p0, p0 + RW_GROUP):
        flat += list(_rw_prep_args(p, r_ref, k_ref, v_ref, lw_ref, gd_ref, vrefs, mrefs))
    return flat


def _rw_fwd(rws, vecs, mats, name):
    T = rws.shape[0]
    tb = _row_tile(T, SCAN_ROWS)
    nsub = tb // RW_CHUNK
    C = RW_CHUNK
    nT, wide, vec, mat, st, ins = _rw_specs(T, tb, False)

    def body(*refs):
        r_ref, k_ref, v_ref, lw_ref, gd_ref = refs[:5]
        vrefs = refs[5:5 + N_RW_VEC]
        mrefs = refs[5 + N_RW_VEC:5 + N_RW_VEC + N_RW_MAT]
        o_ref, st_ref, s_ref, wq_ref, uy_ref, bk_ref, misc_ref, y_ref = refs[-8:]

        @pl.when(pl.program_id(0) == 0)
        def _():
            s_ref[...] = jnp.zeros_like(s_ref)

        consts = _rw_consts(tb)
        blockdiag = consts[3]
        for p0 in range(0, RW_PAIRS, RW_GROUP):
            outs = _rw_prep(consts, *_group_args(p0, r_ref, k_ref, v_ref, lw_ref, gd_ref, vrefs, mrefs))
            for p, (W, U, Q, Y0, Bg, Kg, dec, k2, g) in zip(range(p0, p0 + RW_GROUP), outs):
                _stack_chunks(wq_ref.at[p], W, Q)
                _stack_chunks(uy_ref.at[p], U, Y0)
                _stack_chunks(bk_ref.at[p], Bg, Kg)
                misc_ref[0, p], misc_ref[1, p], misc_ref[2, p] = dec, k2, g

        def step(c, carry):
            rows = _chunk_rows(c, C)
            for p in range(RW_PAIRS):
                S = s_ref[p]
                st_ref[p, c] = S
                py = _dot(wq_ref[p, c], S, NT) + uy_ref[p, c]
                y_ref[p, rows, :] = py[C:]
                pv = jnp.concatenate([py[:C], v_ref[rows, _head_lanes(p)]], axis=0)
                s_ref[p] = (S * misc_ref[0, p, pl.ds(c * C, 1), :] + _dot(pv, bk_ref[p, c], TN)) * blockdiag
            return carry

        lax.fori_loop(0, nsub, step, 0)
        for p in range(RW_PAIRS):
            ln = _head_lanes(p)
            r_k, ln_w, ln_b = [x[:, ln] for x in vrefs[4:]]
            o_ref[:, ln] = _rw_post(blockdiag, y_ref[p], r_ref[:, ln], v_ref[:, ln], misc_ref[1, p], misc_ref[2, p],
                                    r_k, ln_w, ln_b)

    stacked = pltpu.VMEM((RW_PAIRS, nsub, 2 * C, LANES), F32)
    return pl.pallas_call(
        body, name=name, grid=(nT,),
        in_specs=ins + [vec] * N_RW_VEC + [mat] * N_RW_MAT,
        out_specs=[wide(0), st],
        out_shape=[jax.ShapeDtypeStruct((T, RW_WIDTH), F32),
                   jax.ShapeDtypeStruct((RW_PAIRS, T // RW_CHUNK, LANES, LANES), F32)],
        scratch_shapes=[pltpu.VMEM((RW_PAIRS, LANES, LANES), F32), stacked, stacked, stacked,
                        pltpu.VMEM((3, RW_PAIRS, tb, LANES), F32), pltpu.VMEM((RW_PAIRS, tb, LANES), F32)],
    )(rws, rws, rws, rws, rws, *vecs, *mats)


def _rw_bwd(rws, states, do, do_blk, vecs, mats, name):
    T = rws.shape[0]
    tb = _row_tile(T, SCAN_ROWS)
    nsub = tb // RW_CHUNK
    C = RW_CHUNK
    G = RW_GROUP
    nT, wide, vec, mat, st, ins = _rw_specs(T, tb, True)
    nin = 5 + 1 + 1 + N_RW_VEC + N_RW_MAT

    def body(*refs):
        r_ref, k_ref, v_ref, lw_ref, gd_ref = refs[:5]
        st_ref, do_ref = refs[5], refs[6]
        vrefs = refs[7:7 + N_RW_VEC]
        mrefs = refs[7 + N_RW_VEC:nin]
        dr_ref, dk_ref, dv_ref, dlo_ref = refs[nin:nin + 4]
        dvec = refs[nin + 4:nin + 4 + N_RW_VEC]
        dmat = refs[nin + 4 + N_RW_VEC:nin + 4 + N_RW_VEC + N_RW_MAT]
        ds_ref, wq_ref, uy_ref, bk_ref, pv_ref, dec_ref, y_ref, dpre_ref, dvs_ref = refs[-9:]

        @pl.when(pl.program_id(0) == 0)
        def _():
            ds_ref[...] = jnp.zeros_like(ds_ref)
            for x in dvec + dmat:
                x[...] = jnp.zeros_like(x)

        consts = _rw_consts(tb)
        blockdiag = consts[3]
        dlw, dgd = 0.0, 0.0
        for p0 in range(0, RW_PAIRS, G):
            outs, prep_vjp = jax.vjp(functools.partial(_rw_prep, consts),
                                     *_group_args(p0, r_ref, k_ref, v_ref, lw_ref, gd_ref, vrefs, mrefs))
            for q, (W, U, Q, Y0, Bg, Kg, dec, _, _) in enumerate(outs):
                _stack_chunks(wq_ref.at[q], W, Q)
                _stack_chunks(uy_ref.at[q], U, Y0)
                _stack_chunks(bk_ref.at[q], Bg, Kg)
                dec_ref[q] = dec

            def redo(c, carry, p0=p0):
                rows = _chunk_rows(c, C)
                for q in range(G):
                    py = _dot(wq_ref[q, c], st_ref[p0 + q, c], NT) + uy_ref[q, c]
                    y_ref[q, rows, :] = py[C:]
                    pv_ref[q, c, 0:C, :] = py[:C]
                    pv_ref[q, c, C:2 * C, :] = v_ref[rows, _head_lanes(p0 + q)]
                return carry

            lax.fori_loop(0, nsub, redo, 0)
            post = []
            for q in range(G):
                ln = _head_lanes(p0 + q)
                r_k, ln_w, ln_b = [x[:, ln] for x in vrefs[4:]]
                _, post_vjp = jax.vjp(functools.partial(_rw_post, blockdiag), y_ref[q], r_ref[:, ln], v_ref[:, ln],
                                      outs[q][7], outs[q][8], r_k, ln_w, ln_b)
                dy, dr2, dv2, dk2, dg, dr_k, dln_w, dln_b = post_vjp(do_ref[:, ln])
                dpre_ref[q, 3] = dy
                dvs_ref[q] = dv2
                for x, gx in zip(dvec[4:], (dr_k, dln_w, dln_b)):
                    x[:, ln] += gx
                dpre_ref[q, 6] = jnp.zeros_like(dpre_ref[q, 6])
                post.append((dr2, dk2, dg))

            def step(i, carry, p0=p0):
                c = nsub - 1 - i
                rows = _chunk_rows(c, C)
                row0 = pl.ds(c * C, 1)
                for q in range(G):
                    Gs = ds_ref[p0 + q] * blockdiag
                    S = st_ref[p0 + q, c]
                    t1 = _dot(bk_ref[q, c], Gs, NT)
                    dpy = jnp.concatenate([t1[:C], dpre_ref[q, 3, rows, :]], axis=0)
                    t2 = _dot(dpy, S)
                    t3 = _dot(pv_ref[q, c], Gs)
                    dvs_ref[q, rows, :] += t1[C:]
                    dpre_ref[q, 0, rows, :] = t2[:C]
                    dpre_ref[q, 1, rows, :] = t1[:C]
                    dpre_ref[q, 2, rows, :] = t2[C:]
                    dpre_ref[q, 4, rows, :] = t3[:C]
                    dpre_ref[q, 5, rows, :] = t3[C:]
                    dpre_ref[q, 6, row0, :] = jnp.sum(S * Gs, axis=0, keepdims=True)
                    ds_ref[p0 + q] = Gs * dec_ref[q, row0, :] + _dot(dpy, wq_ref[q, c], TN)
                return carry

            lax.fori_loop(0, nsub, step, 0)
            grads = prep_vjp(tuple(tuple(dpre_ref[q, i] for i in range(7)) + post[q][1:] for q in range(G)))
            for q in range(G):
                ln = _head_lanes(p0 + q)
                gq = grads[N_PREP_IN * q:N_PREP_IN * (q + 1)]
                dr_ref[:, ln] = gq[0] + post[q][0]
                dk_ref[:, ln] = gq[1]
                dv_ref[:, ln] = gq[2] + dvs_ref[q]
                dlw = dlw + gq[3]
                dgd = dgd + gq[4]
                for x, gx in zip(dvec[:4], gq[5:9]):
                    x[:, ln] += gx
                for x, gx in zip(dmat, gq[9:]):
                    x[p0 + q] += gx
        dlo_ref[:, 0:LANES] = dlw
        dlo_ref[:, LANES:2 * LANES] = dgd

    dcol = jax.ShapeDtypeStruct((T, RW_WIDTH), F32)
    dlo_spec = pl.BlockSpec((tb, 2 * LANES), lambda t: (nT - 1 - t, 0))
    blk = pltpu.VMEM((G, tb, LANES), F32)
    stacked = pltpu.VMEM((G, nsub, 2 * C, LANES), F32)
    return pl.pallas_call(
        body, name=name, grid=(nT,),
        in_specs=ins + [st, wide(do_blk)] + [vec] * N_RW_VEC + [mat] * N_RW_MAT,
        out_specs=[wide(0)] * 3 + [dlo_spec] + [vec] * N_RW_VEC + [mat] * N_RW_MAT,
        out_shape=[dcol] * 3 + [jax.ShapeDtypeStruct((T, 2 * LANES), F32)]
        + [jax.ShapeDtypeStruct((1, RW_WIDTH), F32)] * N_RW_VEC
        + [jax.ShapeDtypeStruct((RW_PAIRS, LANES, LANES), F32)] * N_RW_MAT,
        scratch_shapes=[pltpu.VMEM((RW_PAIRS, LANES, LANES), F32), stacked, stacked, stacked, stacked, blk, blk,
                        pltpu.VMEM((G, 7, tb, LANES), F32), blk],
    )(rws, rws, rws, rws, rws, states, do, *vecs, *mats)


def _my_index():
    return 4 * lax.axis_index("x") + 2 * lax.axis_index("y") + lax.axis_index("c")


def _peer(bits):
    pos = []
    for name, flip in zip(("x", "y", "c"), bits):
        i = lax.axis_index(name)
        pos.append(1 - i if flip else i)
    return tuple(pos)


def _peer_index(bits):
    x, y, c = _peer(bits)
    return 4 * x + 2 * y + c


def _all_gather(shards, name):
    n = len(shards)
    chips = [(1, 0, 0), (0, 1, 0), (1, 1, 0)]
    sib = (0, 0, 1)

    def body(*refs):
        ins, outs = refs[:n], refs[n:2 * n]
        send_sems, recv_sems, local_sems = refs[2 * n:]

        def rows(k, dev):
            r = ins[k].shape[0]
            return outs[k].at[pl.ds(dev * r, r), :]

        def copy(k, slot, block_dev, to_bits, src=None):
            return pltpu.make_async_remote_copy(
                src_ref=rows(k, block_dev) if src is None else src, dst_ref=rows(k, block_dev),
                send_sem=send_sems.at[k, slot], recv_sem=recv_sems.at[k, slot],
                device_id=_peer(to_bits), device_id_type=MESH_ID)

        me = _my_index()
        started = []
        for k in range(n):
            mine = pltpu.make_async_copy(ins[k], rows(k, me), local_sems.at[k])
            mine.start()
            started.append(mine)
        sends = []
        for k in range(n):
            first = [copy(k, 0, me, sib, src=ins[k])]
            first += [copy(k, 1 + j, me, chip, src=ins[k]) for j, chip in enumerate(chips)]
            for cp in first:
                cp.start()
            sends += first
        for k in range(n):
            for j, chip in enumerate(chips):
                copy(k, 1 + j, _peer_index(chip), chip).wait_recv()
                fwd = copy(k, 4 + j, _peer_index(chip), sib)
                fwd.start()
                sends.append(fwd)
        for k in range(n):
            copy(k, 0, _peer_index(sib), sib).wait_recv()
            for j, chip in enumerate(chips):
                both = (chip[0], chip[1], 1)
                copy(k, 4 + j, _peer_index(both), sib).wait_recv()
        for cp in sends:
            cp.wait_send()
        for cp in started:
            cp.wait()

    any_spec = pl.BlockSpec(memory_space=pl.ANY)
    return pl.pallas_call(
        body, name=name,
        in_specs=[any_spec] * n, out_specs=[any_spec] * n,
        out_shape=[jax.ShapeDtypeStruct((N_DEV * s.shape[0], s.shape[1]), s.dtype) for s in shards],
        scratch_shapes=[pltpu.SemaphoreType.DMA((n, 7)), pltpu.SemaphoreType.DMA((n, 7)),
                        pltpu.SemaphoreType.DMA((n,))],
    )(*shards)


def _exchange(partials, name):
    n = len(partials)
    flips = [(dx, dy, dc) for dx in (0, 1) for dy in (0, 1) for dc in (0, 1)][1:]

    def body(*refs):
        ins, outs = refs[:n], refs[n:2 * n]
        send_sems, recv_sems, local_sems = refs[2 * n:]
        me = _my_index()
        local = []
        for k in range(n):
            cp = pltpu.make_async_copy(ins[k].at[me], outs[k].at[me], local_sems.at[k])
            cp.start()
            local.append(cp)
        copies = []
        for k in range(n):
            for d, bits in enumerate(flips):
                cp = pltpu.make_async_remote_copy(
                    src_ref=ins[k].at[_peer_index(bits)], dst_ref=outs[k].at[me],
                    send_sem=send_sems.at[k, d], recv_sem=recv_sems.at[k, d],
                    device_id=_peer(bits), device_id_type=MESH_ID)
                cp.start()
                copies.append(cp)
        for cp in copies:
            cp.wait_recv()
        for cp in copies:
            cp.wait_send()
        for cp in local:
            cp.wait()

    any_spec = pl.BlockSpec(memory_space=pl.ANY)
    return pl.pallas_call(
        body, name=name,
        in_specs=[any_spec] * n, out_specs=[any_spec] * n,
        out_shape=[jax.ShapeDtypeStruct(p.shape, p.dtype) for p in partials],
        scratch_shapes=[pltpu.SemaphoreType.DMA((n, 7)), pltpu.SemaphoreType.DMA((n, 7)),
                        pltpu.SemaphoreType.DMA((n,))],
    )(*partials)


HBM_SPEC = pl.BlockSpec(memory_space=pltpu.HBM)
SEM_SPEC = pl.BlockSpec(memory_space=pltpu.SEMAPHORE)
ALL_FLIPS = [(dx, dy, dc) for dx in (0, 1) for dy in (0, 1) for dc in (0, 1)][1:]


def _spread_copies(srcs, lands, send_sems, recv_sems, to_x):
    me = _my_index()
    my_x = lax.axis_index("x")
    copies = []
    for k, land in enumerate(lands):
        for d, bits in enumerate(ALL_FLIPS):
            if not srcs:
                src = land.at[me]
            elif to_x is None:
                src = srcs[k].at[_peer_index(bits)]
            else:
                _, py, pc = _peer(bits)
                src = srcs[k].at[2 * py + pc]
            cp = pltpu.make_async_remote_copy(
                src_ref=src, dst_ref=land.at[me],
                send_sem=send_sems.at[k * 7 + d], recv_sem=recv_sems.at[k * 7 + d],
                device_id=_peer(bits), device_id_type=MESH_ID)
            sends = True if to_x is None else my_x == (to_x ^ bits[0])
            receives = True if to_x is None else my_x == to_x
            copies.append((cp, sends, receives))
    return copies


def _when(cond, fn):
    if cond is True:
        fn()
    else:
        pl.when(cond)(fn)


def _spread_start(srcs, lands, name, to_x=None):
    ns, n = len(srcs), len(lands)

    def body(*refs):
        src_refs, land_refs = refs[:ns], refs[ns:ns + n]
        send_sems, recv_sems = refs[ns + n], refs[ns + n + 1]
        token = refs[-1]
        for cp, sends, _ in _spread_copies(src_refs, land_refs, send_sems, recv_sems, to_x):
            _when(sends, cp.start)
        token[...] = jnp.zeros_like(token)

    bufs = list(srcs) + list(lands)
    out = pl.pallas_call(
        body, name=name,
        out_shape=(pltpu.SemaphoreType.DMA((7 * n,)), pltpu.SemaphoreType.DMA((7 * n,)),
                   *[pltpu.HBM(b.shape, b.dtype) for b in bufs], jax.ShapeDtypeStruct((8, LANES), F32)),
        in_specs=[HBM_SPEC] * (ns + n),
        out_specs=(SEM_SPEC, SEM_SPEC, *[HBM_SPEC] * (ns + n), pl.BlockSpec(memory_space=pltpu.VMEM)),
        input_output_aliases={i: 2 + i for i in range(ns + n)},
        compiler_params=pltpu.CompilerParams(has_side_effects=pltpu.SideEffectType.DATAFLOW_SIDE_EFFECTING),
    )(*[pltpu.with_memory_space_constraint(b, pltpu.HBM) for b in bufs])
    return out[0], out[1], list(out[2:2 + ns]), list(out[2 + ns:2 + ns + n]), out[-1]


def _spread_wait(send_sems, recv_sems, srcs, lands, after, name, to_x=None):
    ns, n = len(srcs), len(lands)

    def body(*refs):
        src_refs, land_refs = refs[:ns], refs[ns:ns + n]
        send_sems, recv_sems = refs[ns + n], refs[ns + n + 1]
        for cp, sends, receives in _spread_copies(src_refs, land_refs, send_sems, recv_sems, to_x):
            _when(sends, cp.wait_send)
            _when(receives, cp.wait_recv)

    bufs = list(srcs) + list(lands)
    out = pl.pallas_call(
        body, name=name,
        out_shape=tuple(pltpu.HBM(b.shape, b.dtype) for b in bufs),
        in_specs=[HBM_SPEC] * (ns + n) + [SEM_SPEC, SEM_SPEC, pl.BlockSpec(memory_space=pl.ANY)],
        out_specs=tuple([HBM_SPEC] * (ns + n)),
        input_output_aliases={i: i for i in range(ns + n)},
        compiler_params=pltpu.CompilerParams(has_side_effects=pltpu.SideEffectType.DATAFLOW_SIDE_EFFECTING),
    )(*bufs, send_sems, recv_sems, after)
    return list(out[ns:])


def _own_slot_only(block, me):
    return lax.dynamic_update_slice(lax.empty((N_DEV,) + block.shape, block.dtype), block[None], (me, 0, 0))


def _sum_slots(landed, name):
    _, R, C = landed.shape
    tb = _row_tile(R, 128)

    def body(l_ref, o_ref):
        acc = l_ref[0].astype(F32)
        for s in range(1, N_DEV):
            acc = acc + l_ref[s].astype(F32)
        o_ref[...] = acc

    return pl.pallas_call(
        body, name=name, grid=(R // tb,),
        in_specs=[pl.BlockSpec((N_DEV, tb, C), lambda i: (0, i, 0))],
        out_specs=pl.BlockSpec((tb, C), lambda i: (i, 0)),
        out_shape=jax.ShapeDtypeStruct((R, C), F32),
    )(landed)


def _pack_rows(flat_list, width=LANES):
    flat = jnp.concatenate([a.reshape(-1) for a in flat_list])
    n = flat.shape[0]
    rows = -(-n // width)
    rows = -(-rows // 8) * 8
    return jnp.pad(flat, (0, rows * width - n)).reshape(rows, width)


def _unpack(packed, shapes):
    flat = packed.reshape(-1)
    out, off = [], 0
    for s in shapes:
        n = 1
        for d in s:
            n *= d
        out.append(flat[off:off + n].reshape(s))
        off += n
    return out


def kernel(x, norm1_w, w_in, hg_lb_logits, hg_norm_w, rw_shift_mu, rw_w0, rw_w2, rw_a0, rw_a2, rw_g2, rw_k_k, rw_k_a, rw_r_k, rw_ln_w, rw_ln_b, w_out, norm2_w, w_up, conv_w, conv_b, w_down, final_norm_w, loss_target, m_norm1_w, m_w_in, m_hg_lb_logits, m_hg_norm_w, m_rw_shift_mu, m_rw_w0, m_rw_w2, m_rw_a0, m_rw_a2, m_rw_g2, m_rw_k_k, m_rw_k_a, m_rw_r_k, m_rw_ln_w, m_rw_ln_b, m_w_out, m_norm2_w, m_w_up, m_conv_w, m_conv_b, m_w_down, m_final_norm_w, v_norm1_w, v_w_in, v_hg_lb_logits, v_hg_norm_w, v_rw_shift_mu, v_rw_w0, v_rw_w2, v_rw_a0, v_rw_a2, v_rw_g2, v_rw_k_k, v_rw_k_a, v_rw_r_k, v_rw_ln_w, v_rw_ln_b, v_w_out, v_norm2_w, v_w_up, v_conv_w, v_conv_b, v_w_down, v_final_norm_w):
    weights = dict(norm1_w=norm1_w, w_in=w_in, hg_lb_logits=hg_lb_logits, hg_norm_w=hg_norm_w,
                   rw_shift_mu=rw_shift_mu, rw_w0=rw_w0, rw_w2=rw_w2, rw_a0=rw_a0, rw_a2=rw_a2, rw_g2=rw_g2,
                   rw_k_k=rw_k_k, rw_k_a=rw_k_a, rw_r_k=rw_r_k, rw_ln_w=rw_ln_w, rw_ln_b=rw_ln_b, w_out=w_out,
                   norm2_w=norm2_w, w_up=w_up, conv_w=conv_w, conv_b=conv_b, w_down=w_down,
                   final_norm_w=final_norm_w)
    m_in = dict(norm1_w=m_norm1_w, w_in=m_w_in, hg_lb_logits=m_hg_lb_logits, hg_norm_w=m_hg_norm_w,
                rw_shift_mu=m_rw_shift_mu, rw_w0=m_rw_w0, rw_w2=m_rw_w2, rw_a0=m_rw_a0, rw_a2=m_rw_a2,
                rw_g2=m_rw_g2, rw_k_k=m_rw_k_k, rw_k_a=m_rw_k_a, rw_r_k=m_rw_r_k, rw_ln_w=m_rw_ln_w,
                rw_ln_b=m_rw_ln_b, w_out=m_w_out, norm2_w=m_norm2_w, w_up=m_w_up, conv_w=m_conv_w,
                conv_b=m_conv_b, w_down=m_w_down, final_norm_w=m_final_norm_w)
    v_in = dict(norm1_w=v_norm1_w, w_in=v_w_in, hg_lb_logits=v_hg_lb_logits, hg_norm_w=v_hg_norm_w,
                rw_shift_mu=v_rw_shift_mu, rw_w0=v_rw_w0, rw_w2=v_rw_w2, rw_a0=v_rw_a0, rw_a2=v_rw_a2,
                rw_g2=v_rw_g2, rw_k_k=v_rw_k_k, rw_k_a=v_rw_k_a, rw_r_k=v_rw_r_k, rw_ln_w=v_rw_ln_w,
                rw_ln_b=v_rw_ln_b, w_out=v_w_out, norm2_w=v_norm2_w, w_up=v_w_up, conv_w=v_conv_w,
                conv_b=v_conv_b, w_down=v_w_down, final_norm_w=v_final_norm_w)
    names = list(weights)
    sharded_small = ["rw_w2", "rw_a2", "rw_g2", "conv_w"]
    sharded_big = ["w_in", "w_out", "w_up", "w_down"]
    replicated = [n for n in names if n not in sharded_small + sharded_big]

    xs = x[0]
    tgt = loss_target[0]

    small_shard = _pack_rows([weights[n] for n in sharded_small])
    g_win_t, g_small = _all_gather([w_in[0].T.astype(BF16), small_shard], "gather_weights")
    me = _my_index()
    later = (w_up[0].T.astype(BF16), w_out[0].astype(BF16), w_down[0].astype(BF16))
    later, _ = lax.optimization_barrier((later, g_small))
    later = [_own_slot_only(z, me) for z in later]
    g_send, g_recv, _, later, g_token = _spread_start([], later, "gather_later_start")
    small_shapes = [weights[n].shape for n in sharded_small]
    per_dev = [_unpack(g_small.reshape(N_DEV, -1)[j], small_shapes) for j in range(N_DEV)]
    w2_full, a2_full, g2_full, convw_full = [jnp.concatenate([per_dev[j][i][0] for j in range(N_DEV)], axis=-1)
                                             for i in range(4)]
    zeros64 = jnp.zeros((RW_PAIRS, 64, LANES), F32)
    by_pair = lambda z: z.reshape(z.shape[0], RW_PAIRS, LANES).transpose(1, 0, 2)
    w2p = jnp.concatenate([by_pair(w2_full), zeros64], axis=1)
    a2p = jnp.concatenate([zeros64, by_pair(a2_full)], axis=1)
    g2p = by_pair(g2_full)

    l0, l1 = hg_lb_logits[0:1], hg_lb_logits[1:2]
    h1 = _rms_fwd(xs, norm1_w + g_token[0:1, 0:1], "norm1")
    proj = _mm_nt(h1, g_win_t, "proj_in")
    o_hg, hg_states = _hg_fwd(proj, l0, l1, hg_norm_w, "hgrn2_fwd")
    rws = _shift_fwd(proj, rw_shift_mu, "token_shift")
    rw_vecs = [rw_w0, rw_a0, rw_k_k, rw_k_a, rw_r_k, rw_ln_w, rw_ln_b]
    rw_mats = [w2p, a2p, g2p]
    o_rw, rw_states = _rw_fwd(rws, rw_vecs, rw_mats, "rwkv7_fwd")
    o_mix = jnp.concatenate([o_hg, o_rw], axis=-1).astype(BF16)
    g_wup_t, g_wout, g_wdown = [z.reshape(-1, z.shape[-1])
                                for z in _spread_wait(g_send, g_recv, [], later, o_mix, "gather_later_wait")]
    x1 = _mm_nn(o_mix, g_wout, xs, "proj_out")
    h2 = _rms_fwd(x1, norm2_w, "norm2")
    u = _mm_nt(h2, g_wup_t, "ffn_up")
    act = _ffn_act_fwd(u, convw_full, conv_b, "ffn_act")
    x2 = _mm_nn(act, g_wdown, x1, "ffn_down")
    loss_part, dx2, d_final_w = _loss_head(x2, final_norm_w.reshape(1, -1), tgt, "loss_head")

    d_wdown = _mm_tn(act, dx2, 1408, "ffn_down_dw", BF16)
    dact = _mm_nt(dx2, g_wdown, "ffn_down_dx", BF16)
    du_g, du_v, dcw_g, dcw_v, dcb_g, dcb_v = _ffn_act_bwd(u, dact, convw_full, conv_b, "ffn_act_bwd")
    d_convw = jnp.concatenate([dcw_g, dcw_v], axis=-1)
    d_convb = jnp.concatenate([dcb_g, dcb_v], axis=-1)
    d_wup_t = jnp.concatenate([_mm_tn(du_g, h2, 1408, "ffn_up_dw_gate", BF16),
                               _mm_tn(du_v, h2, 1408, "ffn_up_dw_value", BF16)], axis=0)
    dh2 = _mm_nn([du_g, du_v], g_wup_t, None, "ffn_up_dx")
    dx1, d_norm2 = _rms_bwd(dh2, x1, norm2_w, dx2, "norm2_bwd")
    d_wout = _mm_tn(o_mix, dx1, 512, "proj_out_dw", BF16)
    do = _mm_nt(dx1, g_wout, "proj_out_dx")
    early = [z.reshape(N_DEV, z.shape[0] // N_DEV, z.shape[1]) for z in (d_wup_t, d_wout, d_wdown)]
    early_land = [_own_slot_only(lax.dynamic_index_in_dim(z, me, 0, keepdims=False), me) for z in early]
    e_send, e_recv, early, early_land, e_token = _spread_start(early, early_land, "exchange_early_start")
    hg_norm_w_t = hg_norm_w + e_token[0:1, 0:1]
    dq, df, di, dg, d_l0, d_l1, d_hg_nw = _hg_bwd(proj, hg_states, do, 0, l0, l1, hg_norm_w_t, "hgrn2_bwd")
    half = N_DEV // 2
    own_half_block = lambda z: _own_slot_only(lax.dynamic_index_in_dim(z, me % half, 0, keepdims=False), me)
    n_lo = half * w_in.shape[2]
    d_win_lo = _mm_tn(jnp.concatenate([dq, df, di, dg[:, :n_lo - 3 * HG_WIDTH]], axis=-1), h1, 640,
                      "proj_in_dw_low", BF16).reshape(half, -1, D_MODEL)
    m_send, m_recv, mid, mid_land, m_token = _spread_start([d_win_lo], [own_half_block(d_win_lo)],
                                                            "exchange_mid_start", to_x=0)
    rw_vecs_t = [rw_vecs[0] + m_token[0:1, 0:1]] + rw_vecs[1:]
    rw_out = _rw_bwd(rws, rw_states, do, 1, rw_vecs_t, rw_mats, "rwkv7_bwd")
    d_rw_vecs = rw_out[4:4 + N_RW_VEC]
    d_w2p, d_a2p, d_g2p = rw_out[4 + N_RW_VEC:]
    dp_parts, dmu_parts = [], []
    for i, z in enumerate(rw_out[:4]):
        dp, dmu = _shift_bwd(z, proj, rw_shift_mu, i * RW_WIDTH, "token_shift_bwd_%d" % i)
        dp_parts.append(dp)
        dmu_parts.append(dmu)
    d_mu = jnp.concatenate(dmu_parts, axis=-1)
    d_win_hi = _mm_tn(jnp.concatenate([dg[:, n_lo - 3 * HG_WIDTH:]] + dp_parts, axis=-1), h1, 640,
                      "proj_in_dw_high", BF16).reshape(half, -1, D_MODEL)
    from_pairs = lambda z: z.transpose(1, 0, 2).reshape(z.shape[1], RW_WIDTH)
    d_w2 = from_pairs(d_w2p[:, :64])
    d_a2 = from_pairs(d_a2p[:, 64:])
    d_g2 = from_pairs(d_g2p)
    col_blocks = lambda z: z.reshape(z.shape[0], N_DEV, -1).transpose(1, 0, 2)
    small_part = jnp.stack([
        _pack_rows([col_blocks(d_w2)[j], col_blocks(d_a2)[j], col_blocks(d_g2)[j], col_blocks(d_convw)[j]])
        for j in range(N_DEV)])
    l_send, l_recv, late, late_land, l_token = _spread_start([d_win_hi], [own_half_block(d_win_hi)],
                                                             "exchange_late_start", to_x=1)
    dh1 = _mm_nn([dq, df, di, dg] + dp_parts, g_win_t, None, "proj_in_dx")
    grad_x, d_norm1 = _rms_bwd(dh1, xs, norm1_w + l_token[0:1, 0:1], dx1, "norm1_bwd")

    rep_grads = dict(norm1_w=d_norm1, hg_lb_logits=jnp.concatenate([d_l0, d_l1], axis=0), hg_norm_w=d_hg_nw,
                     rw_shift_mu=d_mu, rw_w0=d_rw_vecs[0], rw_a0=d_rw_vecs[1], rw_k_k=d_rw_vecs[2],
                     rw_k_a=d_rw_vecs[3], rw_r_k=d_rw_vecs[4], rw_ln_w=d_rw_vecs[5], rw_ln_b=d_rw_vecs[6],
                     norm2_w=d_norm2, conv_b=d_convb, final_norm_w=d_final_w)
    rep_pack = _pack_rows([loss_part] + [rep_grads[n] for n in replicated])
    rep_part = jnp.broadcast_to(rep_pack[None], (N_DEV,) + rep_pack.shape)
    grads, delta, new_m, new_v = {}, {}, {}, {}

    def adamw_big(n, g):
        shp = weights[n].shape
        as2d = lambda z: z.reshape(shp[1], shp[2])
        grads[n] = g[None]
        d, nm, nv = _adamw(as2d(weights[n]), g, as2d(m_in[n]), as2d(v_in[n]), "adamw_" + n)
        delta[n], new_m[n], new_v[n] = d.reshape(shp), nm.reshape(shp), nv.reshape(shp)

    landed_early = _spread_wait(e_send, e_recv, early, early_land, grad_x, "exchange_early_wait")
    adamw_big("w_up", _sum_slots(landed_early[0], "sum_grads_w_up").T)
    adamw_big("w_out", _sum_slots(landed_early[1], "sum_grads_w_out"))
    adamw_big("w_down", _sum_slots(landed_early[2], "sum_grads_w_down"))
    (landed_mid,) = _spread_wait(m_send, m_recv, mid, mid_land, grad_x, "exchange_mid_wait", to_x=0)
    (landed_late,) = _spread_wait(l_send, l_recv, late, late_land, delta["w_down"], "exchange_late_wait", to_x=1)
    g_win = jnp.where(lax.axis_index("x") == 0, _sum_slots(landed_mid, "sum_grads_w_in_low"),
                      _sum_slots(landed_late, "sum_grads_w_in_high"))
    adamw_big("w_in", g_win.T)
    landed_rep, landed_small = _exchange([rep_part, small_part], "exchange_grads")
    g_small_sum = _unpack(_sum_slots(landed_small, "sum_grads_small"), small_shapes)
    rep_sum = _unpack(_sum_slots(landed_rep, "sum_grads_replicated"), [(1, 1)] + [weights[n].shape for n in replicated])
    loss = rep_sum[0].reshape(())
    grads.update(dict(zip(replicated, rep_sum[1:])))
    grads.update(dict(zip(sharded_small, g_small_sum)))

    small_names = replicated + sharded_small
    packs = [_pack_rows([src[n] for n in small_names]) for src in (weights, grads, m_in, v_in)]
    outs = _adamw(*packs, "adamw_small")
    small_shapes_all = [weights[n].shape for n in small_names]
    for dst, packed in zip((delta, new_m, new_v), outs):
        dst.update(dict(zip(small_names, _unpack(packed, small_shapes_all))))

    return (loss, grad_x[None], *[grads[n] for n in names], *[delta[n] for n in names],
            *[new_m[n] for n in names], *[new_v[n] for n in names])
```

```python
import functools

import jax
import jax.numpy as jnp
from jax import lax
from jax.experimental import pallas as pl
from jax.experimental.pallas import tpu as pltpu

F32 = jnp.float32
BF16 = jnp.bfloat16
HIGHEST = lax.Precision.HIGHEST
SCAN_PRECISION = None
MESH_ID = pl.DeviceIdType.MESH

N_DEV = 8
D_MODEL = 1024
HG_WIDTH = 512
HG_HEAD_DIM = 128
HG_HEADS = 4
RW_WIDTH = 512
RW_PAIRS = 4
RW_HEAD_DIM = 64
HG_COLS = 2048
RW_COLS = 1792
D_FF = 2816
NORM_EPS = 1e-6
RW_GN_EPS = 64e-5
L2_EPS = 1e-12
ADAM_LR, ADAM_B1, ADAM_B2, ADAM_EPS, ADAM_WD, ADAM_STEP = 0.001, 0.9, 0.999, 1e-08, 0.01, 10

HG_CHUNK = 32
HG_HALF = 16
RW_CHUNK = 64
SCAN_ROWS = 256
LANES = 128

NN = ((1,), (0,))
NT = ((1,), (1,))
TN = ((0,), (0,))


def _dot(a, b, dims=NN, precision=SCAN_PRECISION):
    if precision is None:
        a, b = a.astype(BF16), b.astype(BF16)
    return lax.dot_general(a, b, (dims, ((), ())), precision=precision, preferred_element_type=F32)


def _iota2(shape, dim):
    return lax.broadcasted_iota(jnp.int32, shape, dim)


def _sigmoid(z):
    return 0.5 * jnp.tanh(0.5 * z) + 0.5


def _row_tile(n, want):
    t = min(n, want)
    while n % t:
        t //= 2
    return t


def _rms_fwd(x, w, name):
    T, D = x.shape
    tb = _row_tile(T, 512)

    def body(x_ref, w_ref, h_ref):
        xv = x_ref[...]
        r = lax.rsqrt(jnp.mean(xv * xv, axis=-1, keepdims=True) + NORM_EPS)
        h_ref[...] = (xv * r * w_ref[...]).astype(h_ref.dtype)

    return pl.pallas_call(
        body, name=name, grid=(T // tb,),
        in_specs=[pl.BlockSpec((tb, D), lambda i: (i, 0)), pl.BlockSpec((1, D), lambda i: (0, 0))],
        out_specs=pl.BlockSpec((tb, D), lambda i: (i, 0)),
        out_shape=jax.ShapeDtypeStruct((T, D), BF16),
    )(x, w)


def _rms_bwd(dh, x, w, dres, name):
    T, D = x.shape
    tb = _row_tile(T, 256)

    def body(dh_ref, x_ref, w_ref, dres_ref, dx_ref, dw_ref):
        @pl.when(pl.program_id(0) == 0)
        def _():
            dw_ref[...] = jnp.zeros_like(dw_ref)

        xv = x_ref[...]
        r = lax.rsqrt(jnp.mean(xv * xv, axis=-1, keepdims=True) + NORM_EPS)
        xn = xv * r
        dy = dh_ref[...].astype(F32)
        dxn = dy * w_ref[...]
        dx_ref[...] = dres_ref[...] + r * (dxn - xn * jnp.mean(dxn * xn, axis=-1, keepdims=True))
        dw_ref[...] += jnp.sum(dy * xn, axis=0, keepdims=True)

    row = pl.BlockSpec((tb, D), lambda i: (i, 0))
    vec = pl.BlockSpec((1, D), lambda i: (0, 0))
    return pl.pallas_call(
        body, name=name, grid=(T // tb,),
        in_specs=[row, row, vec, row], out_specs=[row, vec],
        out_shape=[jax.ShapeDtypeStruct((T, D), F32), jax.ShapeDtypeStruct((1, D), F32)],
    )(dh, x, w, dres)


def _mm_nt(a, bt, name, out_dtype=F32):
    T, K = a.shape
    N = bt.shape[0]
    tm = _row_tile(T, 256)

    def body(a_ref, b_ref, o_ref):
        o_ref[...] = _dot(a_ref[...].astype(BF16), b_ref[...].astype(BF16), NT, None).astype(o_ref.dtype)

    return pl.pallas_call(
        body, name=name, grid=(T // tm,),
        in_specs=[pl.BlockSpec((tm, K), lambda i: (i, 0)), pl.BlockSpec((N, K), lambda i: (0, 0))],
        out_specs=pl.BlockSpec((tm, N), lambda i: (i, 0)),
        out_shape=jax.ShapeDtypeStruct((T, N), out_dtype),
    )(a, bt)


def _mm_nn(a, b, res, name, out_dtype=F32):
    parts = list(a) if isinstance(a, (list, tuple)) else [a]
    T = parts[0].shape[0]
    K, N = b.shape
    tm = _row_tile(T, 256)
    widths = [p.shape[1] for p in parts]
    n = len(parts)

    def body(*refs):
        b_ref, o_ref = refs[n], refs[-1]
        acc, off = None, 0
        for a_ref, w in zip(refs[:n], widths):
            d = _dot(a_ref[...].astype(BF16), b_ref[off:off + w, :].astype(BF16), NN, None)
            acc = d if acc is None else acc + d
            off += w
        if res is not None:
            acc = acc + refs[n + 1][...]
        o_ref[...] = acc.astype(o_ref.dtype)

    in_specs = [pl.BlockSpec((tm, w), lambda i: (i, 0)) for w in widths] + [pl.BlockSpec((K, N), lambda i: (0, 0))]
    args = parts + [b]
    if res is not None:
        in_specs.append(pl.BlockSpec((tm, N), lambda i: (i, 0)))
        args.append(res)
    return pl.pallas_call(
        body, name=name, grid=(T // tm,), in_specs=in_specs,
        out_specs=pl.BlockSpec((tm, N), lambda i: (i, 0)),
        out_shape=jax.ShapeDtypeStruct((T, N), out_dtype),
    )(*args)


def _mm_tn(a, b, tmm, name, out_dtype=F32):
    T, M = a.shape
    N = b.shape[1]
    tk = _row_tile(T, 512)
    nk = T // tk

    def body(a_ref, b_ref, o_ref, acc_ref):
        @pl.when(pl.program_id(1) == 0)
        def _():
            acc_ref[...] = jnp.zeros_like(acc_ref)

        acc_ref[...] += _dot(a_ref[...].astype(BF16), b_ref[...].astype(BF16), TN, None)

        @pl.when(pl.program_id(1) == nk - 1)
        def _():
            o_ref[...] = acc_ref[...].astype(o_ref.dtype)

    return pl.pallas_call(
        body, name=name, grid=(M // tmm, nk),
        in_specs=[pl.BlockSpec((tk, tmm), lambda m, k: (k, m)), pl.BlockSpec((tk, N), lambda m, k: (k, 0))],
        out_specs=pl.BlockSpec((tmm, N), lambda m, k: (m, 0)),
        out_shape=jax.ShapeDtypeStruct((M, N), out_dtype),
        scratch_shapes=[pltpu.VMEM((tmm, N), F32)],
    )(a, b)


class _RowShifts:
    def __init__(self, shape):
        index = _iota2(shape, 0)
        self.rows = shape[0]
        self.first = {n: index < n for n in (1, 2)}
        self.last = {n: index >= shape[0] - n for n in (1, 2)}

    def down(self, z, n):
        return jnp.where(self.first[n], 0.0, pltpu.roll(z, n, 0))

    def up(self, z, n):
        return jnp.where(self.last[n], 0.0, pltpu.roll(z, self.rows - n, 0))


def _shift_fwd(proj, mu, name):
    T = proj.shape[0]
    nblk = RW_COLS // LANES
    first = HG_COLS // LANES

    def body(p_ref, mu_ref, o_ref):
        p = p_ref[...]
        o_ref[...] = p + (_RowShifts(p.shape).down(p, 1) - p) * mu_ref[...]

    return pl.pallas_call(
        body, name=name, grid=(nblk,),
        in_specs=[pl.BlockSpec((T, LANES), lambda j: (0, first + j)), pl.BlockSpec((1, LANES), lambda j: (0, j))],
        out_specs=pl.BlockSpec((T, LANES), lambda j: (0, j)),
        out_shape=jax.ShapeDtypeStruct((T, RW_COLS), F32),
    )(proj, mu)


def _shift_bwd(ds, proj, mu, col0, name):
    T, width = ds.shape
    nblk = width // LANES
    first = (HG_COLS + col0) // LANES
    mu0 = col0 // LANES

    def body(ds_ref, p_ref, mu_ref, dp_ref, dmu_ref):
        dsv = ds_ref[...]
        p = p_ref[...]
        m = mu_ref[...]
        shifts = _RowShifts(p.shape)
        dp_ref[...] = (dsv * (1.0 - m) + shifts.up(dsv * m, 1)).astype(dp_ref.dtype)
        dmu_ref[...] = jnp.sum(dsv * (shifts.down(p, 1) - p), axis=0, keepdims=True)

    return pl.pallas_call(
        body, name=name, grid=(nblk,),
        in_specs=[pl.BlockSpec((T, LANES), lambda j: (0, j)),
                  pl.BlockSpec((T, LANES), lambda j: (0, first + j)),
                  pl.BlockSpec((1, LANES), lambda j: (0, mu0 + j))],
        out_specs=[pl.BlockSpec((T, LANES), lambda j: (0, j)), pl.BlockSpec((1, LANES), lambda j: (0, j))],
        out_shape=[jax.ShapeDtypeStruct((T, width), BF16), jax.ShapeDtypeStruct((1, width), F32)],
    )(ds, proj, mu)


def _conv3(z, w_ref, shifts):
    return w_ref[0:1, :] * shifts.down(z, 2) + w_ref[1:2, :] * shifts.down(z, 1) + w_ref[2:3, :] * z


def _ffn_act_fwd(u, conv_w, conv_b, name):
    T = u.shape[0]
    nblk = D_FF // LANES

    def body(ug_ref, uv_ref, wg_ref, wv_ref, bg_ref, bv_ref, act_ref):
        shifts = _RowShifts((T, LANES))
        gate = _conv3(ug_ref[...], wg_ref, shifts) + bg_ref[...]
        val = _conv3(uv_ref[...], wv_ref, shifts) + bv_ref[...]
        act_ref[...] = (gate * _sigmoid(gate) * val).astype(act_ref.dtype)

    col = lambda off: pl.BlockSpec((T, LANES), lambda j: (0, off + j))
    wsp = lambda off: pl.BlockSpec((3, LANES), lambda j: (0, off + j))
    bsp = lambda off: pl.BlockSpec((1, LANES), lambda j: (0, off + j))
    return pl.pallas_call(
        body, name=name, grid=(nblk,),
        in_specs=[col(0), col(nblk), wsp(0), wsp(nblk), bsp(0), bsp(nblk)],
        out_specs=pl.BlockSpec((T, LANES), lambda j: (0, j)),
        out_shape=jax.ShapeDtypeStruct((T, D_FF), BF16),
    )(u, u, conv_w, conv_w, conv_b, conv_b)


def _ffn_act_bwd(u, dact, conv_w, conv_b, name):
    T = u.shape[0]
    nblk = D_FF // LANES

    def conv_bwd(z, dzc, w_ref, du_ref, dw_ref, db_ref, shifts):
        up1, up2 = shifts.up(dzc, 1), shifts.up(dzc, 2)
        du = w_ref[2:3, :] * dzc + w_ref[1:2, :] * up1 + w_ref[0:1, :] * up2
        du_ref[...] = du.astype(du_ref.dtype)
        dw_ref[0:1, :] = jnp.sum(up2 * z, axis=0, keepdims=True)
        dw_ref[1:2, :] = jnp.sum(up1 * z, axis=0, keepdims=True)
        dw_ref[2:3, :] = jnp.sum(dzc * z, axis=0, keepdims=True)
        db_ref[...] = jnp.sum(dzc, axis=0, keepdims=True)

    def body(ug_ref, uv_ref, da_ref, wg_ref, wv_ref, bg_ref, bv_ref,
             dug_ref, duv_ref, dwg_ref, dwv_ref, dbg_ref, dbv_ref):
        ug, uv = ug_ref[...], uv_ref[...]
        shifts = _RowShifts((T, LANES))
        gate = _conv3(ug, wg_ref, shifts) + bg_ref[...]
        val = _conv3(uv, wv_ref, shifts) + bv_ref[...]
        da = da_ref[...].astype(F32)
        sg = _sigmoid(gate)
        dgate = da * val * (sg * (1.0 + gate * (1.0 - sg)))
        dval = da * gate * sg
        conv_bwd(ug, dgate, wg_ref, dug_ref, dwg_ref, dbg_ref, shifts)
        conv_bwd(uv, dval, wv_ref, duv_ref, dwv_ref, dbv_ref, shifts)

    col = lambda off: pl.BlockSpec((T, LANES), lambda j: (0, off + j))
    wsp = lambda off: pl.BlockSpec((3, LANES), lambda j: (0, off + j))
    bsp = lambda off: pl.BlockSpec((1, LANES), lambda j: (0, off + j))
    half = lambda r, dt: jax.ShapeDtypeStruct((r, D_FF), dt)
    return pl.pallas_call(
        body, name=name, grid=(nblk,),
        in_specs=[col(0), col(nblk), col(0), wsp(0), wsp(nblk), bsp(0), bsp(nblk)],
        out_specs=[col(0), col(0), wsp(0), wsp(0), bsp(0), bsp(0)],
        out_shape=[half(T, BF16), half(T, BF16), half(3, F32), half(3, F32), half(1, F32), half(1, F32)],
    )(u, u, dact, conv_w, conv_w, conv_b, conv_b)


def _loss_head(x2, w, target, name):
    T, D = x2.shape
    tb = _row_tile(T, 256)

    def body(x_ref, w_ref, t_ref, loss_ref, dx_ref, dw_ref):
        @pl.when(pl.program_id(0) == 0)
        def _():
            loss_ref[...] = jnp.zeros_like(loss_ref)
            dw_ref[...] = jnp.zeros_like(dw_ref)

        xv = x_ref[...]
        r = lax.rsqrt(jnp.mean(xv * xv, axis=-1, keepdims=True) + NORM_EPS)
        xn = xv * r
        err = xn * w_ref[...] - t_ref[...]
        row_loss = jnp.sum(err * err, axis=-1, keepdims=True) * (0.5 / D)
        loss_ref[...] += jnp.sum(row_loss, axis=0, keepdims=True)
        dy = err * (1.0 / D)
        dxn = dy * w_ref[...]
        dx_ref[...] = r * (dxn - xn * jnp.mean(dxn * xn, axis=-1, keepdims=True))
        dw_ref[...] += jnp.sum(dy * xn, axis=0, keepdims=True)

    row = pl.BlockSpec((tb, D), lambda i: (i, 0))
    vec = pl.BlockSpec((1, D), lambda i: (0, 0))
    return pl.pallas_call(
        body, name=name, grid=(T // tb,),
        in_specs=[row, vec, row],
        out_specs=[pl.BlockSpec((1, 1), lambda i: (0, 0)), row, vec],
        out_shape=[jax.ShapeDtypeStruct((1, 1), F32), jax.ShapeDtypeStruct((T, D), F32),
                   jax.ShapeDtypeStruct((1, D), F32)],
    )(x2, w, target)


def _adamw(w, g, m, v, name):
    R, C = w.shape
    tb = _row_tile(R, 256) if R % 8 == 0 else R

    def body(w_ref, g_ref, m_ref, v_ref, d_ref, nm_ref, nv_ref):
        gv = g_ref[...]
        nm = ADAM_B1 * m_ref[...] + (1.0 - ADAM_B1) * gv
        nv = ADAM_B2 * v_ref[...] + (1.0 - ADAM_B2) * (gv * gv)
        m_hat = nm / (1.0 - ADAM_B1 ** ADAM_STEP)
        v_hat = nv / (1.0 - ADAM_B2 ** ADAM_STEP)
        d_ref[...] = -ADAM_LR * (m_hat / (jnp.sqrt(v_hat) + ADAM_EPS) + ADAM_WD * w_ref[...])
        nm_ref[...] = nm
        nv_ref[...] = nv

    blk = pl.BlockSpec((tb, C), lambda i: (i, 0))
    sd = jax.ShapeDtypeStruct((R, C), F32)
    return pl.pallas_call(
        body, name=name, grid=(R // tb,), in_specs=[blk] * 4, out_specs=[blk] * 3, out_shape=[sd] * 3,
    )(w, g, m, v)


def _chunk_masks(rows, chunk):
    shift = chunk.bit_length() - 1
    i, j = _iota2((rows, rows), 0), _iota2((rows, rows), 1)
    same = jnp.right_shift(i, shift) == jnp.right_shift(j, shift)
    return same.astype(F32), (same & (j <= i)).astype(F32), (same & (j < i)).astype(F32)


def _head_lanes(h):
    return slice(h * LANES, (h + 1) * LANES)


def _chunk_rows(c, chunk):
    return pl.ds(pl.multiple_of(c * chunk, chunk), chunk)


def _hg_consts(rows):
    same, tril, _ = _chunk_masks(rows, HG_CHUNK)
    half_same, half_tril, _ = _chunk_masks(rows, HG_HALF)
    i, j = _iota2((rows, rows), 0), _iota2((rows, rows), 1)
    half_shift, shift = HG_HALF.bit_length() - 1, HG_CHUNK.bit_length() - 1
    mid_row = jnp.left_shift(jnp.right_shift(i, half_shift), half_shift) + (HG_HALF // 2 - 1)
    bound_row = jnp.left_shift(jnp.right_shift(i, shift), shift) + (HG_HALF - 1)
    upto_mid = ((same > 0) & (j <= mid_row)).astype(F32)
    upto_bound = ((same > 0) & (j <= bound_row)).astype(F32)
    lower_left = tril * (1.0 - half_same)
    return jnp.concatenate([tril, same, upto_mid, upto_bound], axis=0), half_tril, lower_left


N_HG_IN = 5


def _hg_prep(consts, *flat):
    sums, half_tril, lower_left = consts
    rows = half_tril.shape[0]
    heads, logs = [], []
    for h in range(len(flat) // N_HG_IN):
        qr, fr, ir, l0, l1 = flat[N_HG_IN * h:N_HG_IN * (h + 1)]
        lb = _sigmoid(l0 - l1)
        f = lb + (1.0 - lb) * _sigmoid(fr)
        heads.append((qr * _sigmoid(qr) * (HG_HEAD_DIM ** -0.5), 1.0 - f, ir))
        logs.append(jnp.log(f))
    acc = _dot(sums, jnp.concatenate(logs, axis=1), NN, HIGHEST)
    sums_of = []
    for h in range(len(heads)):
        acc_h = acc[:, h * LANES:(h + 1) * LANES]
        sums_of.append(tuple(acc_h[n * rows:(n + 1) * rows] for n in range(4)))
    near = [_dot(q * jnp.exp(a - mid), k * jnp.exp(mid - a), NT) * half_tril
            for (q, k, _), (a, _, mid, _) in zip(heads, sums_of)]
    far = [_dot(q * jnp.exp(jnp.minimum(a - bound, 0.0)), k * jnp.exp(jnp.minimum(bound - a, 0.0)), NT) * lower_left
           for (q, k, _), (a, _, _, bound) in zip(heads, sums_of)]
    intra = [_dot(n + f, ir) for n, f, (_, _, ir) in zip(near, far, heads)]
    return tuple((q * jnp.exp(a), o_intra, k * jnp.exp(tot - a), jnp.exp(tot))
                 for (q, k, _), (a, tot, _, _), o_intra in zip(heads, sums_of, intra))


def _hg_prep_args(q_ref, f_ref, i_ref, l0_ref, l1_ref):
    flat = []
    for h in range(HG_HEADS):
        ln = _head_lanes(h)
        flat += [q_ref[:, ln], f_ref[:, ln], i_ref[:, ln], l0_ref[:, ln], l1_ref[:, ln]]
    return flat


def _hg_post(o, gr, nw):
    on = o * lax.rsqrt(jnp.mean(o * o, axis=-1, keepdims=True) + NORM_EPS)
    return on * nw * (gr * _sigmoid(gr))


def _hg_specs(T, tb, rev):
    nT = T // tb
    tix = (lambda t: nT - 1 - t) if rev else (lambda t: t)
    col = lambda blk: pl.BlockSpec((tb, HG_WIDTH), lambda t: (tix(t), blk))
    vec = pl.BlockSpec((1, HG_WIDTH), lambda t: (0, 0))
    st = pl.BlockSpec((HG_HEADS, tb // HG_CHUNK, HG_HEAD_DIM, HG_HEAD_DIM), lambda t: (0, tix(t), 0, 0))
    return nT, col, vec, st


def _hg_fwd(proj, l0, l1, nw, name):
    T = proj.shape[0]
    tb = _row_tile(T, SCAN_ROWS)
    nsub = tb // HG_CHUNK
    nT, col, vec, st = _hg_specs(T, tb, False)

    def body(q_ref, f_ref, i_ref, g_ref, l0_ref, l1_ref, nw_ref, o_ref, st_ref, s_ref, qe_ref, kd_ref, dec_ref):
        @pl.when(pl.program_id(0) == 0)
        def _():
            s_ref[...] = jnp.zeros_like(s_ref)

        consts = _hg_consts(tb)
        outs = _hg_prep(consts, *_hg_prep_args(q_ref, f_ref, i_ref, l0_ref, l1_ref))
        for h, (qe, o_intra, kd, dec) in enumerate(outs):
            qe_ref[h], kd_ref[h], dec_ref[h] = qe, kd, dec
            o_ref[:, _head_lanes(h)] = o_intra

        def step(c, carry):
            rows = _chunk_rows(c, HG_CHUNK)
            for h in range(HG_HEADS):
                ln = _head_lanes(h)
                S = s_ref[h]
                st_ref[h, c] = S
                o_ref[rows, ln] += _dot(qe_ref[h, rows, :], S, NT)
                s_ref[h] = S * dec_ref[h, pl.ds(c * HG_CHUNK, 1), :] + _dot(i_ref[rows, ln], kd_ref[h, rows, :], TN)
            return carry

        lax.fori_loop(0, nsub, step, 0)
        for h in range(HG_HEADS):
            ln = _head_lanes(h)
            o_ref[:, ln] = _hg_post(o_ref[:, ln], g_ref[:, ln], nw_ref[:, ln])

    blk = pltpu.VMEM((HG_HEADS, tb, LANES), F32)
    return pl.pallas_call(
        body, name=name, grid=(nT,),
        in_specs=[col(0), col(1), col(2), col(3), vec, vec, vec],
        out_specs=[col(0), st],
        out_shape=[jax.ShapeDtypeStruct((T, HG_WIDTH), F32),
                   jax.ShapeDtypeStruct((HG_HEADS, T // HG_CHUNK, HG_HEAD_DIM, HG_HEAD_DIM), F32)],
        scratch_shapes=[pltpu.VMEM((HG_HEADS, HG_HEAD_DIM, HG_HEAD_DIM), F32), blk, blk, blk],
    )(proj, proj, proj, proj, l0, l1, nw)


def _hg_bwd(proj, states, do, do_blk, l0, l1, nw, name):
    T = proj.shape[0]
    tb = _row_tile(T, SCAN_ROWS)
    nsub = tb // HG_CHUNK
    nT, col, vec, st = _hg_specs(T, tb, True)

    def body(q_ref, f_ref, i_ref, g_ref, st_ref, do_ref, l0_ref, l1_ref, nw_ref,
             dq_ref, df_ref, di_ref, dg_ref, dl0_ref, dl1_ref, dnw_ref,
             ds_ref, qe_ref, kd_ref, dec_ref, o_ref, dqe_ref, dkd_ref, ddec_ref, dis_ref):
        @pl.when(pl.program_id(0) == 0)
        def _():
            ds_ref[...] = jnp.zeros_like(ds_ref)
            dl0_ref[...] = jnp.zeros_like(dl0_ref)
            dl1_ref[...] = jnp.zeros_like(dl1_ref)
            dnw_ref[...] = jnp.zeros_like(dnw_ref)

        consts = _hg_consts(tb)
        outs, prep_vjp = jax.vjp(functools.partial(_hg_prep, consts),
                                 *_hg_prep_args(q_ref, f_ref, i_ref, l0_ref, l1_ref))
        for h, (qe, o_intra, kd, dec) in enumerate(outs):
            qe_ref[h], kd_ref[h], dec_ref[h], o_ref[h] = qe, kd, dec, o_intra

        def redo(c, carry):
            rows = _chunk_rows(c, HG_CHUNK)
            for h in range(HG_HEADS):
                o_ref[h, rows, :] += _dot(qe_ref[h, rows, :], st_ref[h, c], NT)
            return carry

        lax.fori_loop(0, nsub, redo, 0)
        for h in range(HG_HEADS):
            ln = _head_lanes(h)
            _, vjp = jax.vjp(_hg_post, o_ref[h], g_ref[:, ln], nw_ref[:, ln])
            d_o, dgr, dnw = vjp(do_ref[:, ln])
            o_ref[h] = d_o
            dg_ref[:, ln] = dgr.astype(dg_ref.dtype)
            dnw_ref[:, ln] += dnw
        ddec_ref[...] = jnp.zeros_like(ddec_ref)

        def step(i, carry):
            c = nsub - 1 - i
            rows = _chunk_rows(c, HG_CHUNK)
            row0 = pl.ds(c * HG_CHUNK, 1)
            for h in range(HG_HEADS):
                ln = _head_lanes(h)
                G = ds_ref[h]
                S = st_ref[h, c]
                d_o = o_ref[h, rows, :]
                dqe_ref[h, rows, :] = _dot(d_o, S)
                dkd_ref[h, rows, :] = _dot(i_ref[rows, ln], G)
                dis_ref[h, rows, :] = _dot(kd_ref[h, rows, :], G, NT)
                ddec_ref[h, row0, :] = jnp.sum(S * G, axis=0, keepdims=True)
                ds_ref[h] = G * dec_ref[h, row0, :] + _dot(d_o, qe_ref[h, rows, :], TN)
            return carry

        lax.fori_loop(0, nsub, step, 0)
        grads = prep_vjp(tuple((dqe_ref[h], o_ref[h], dkd_ref[h], ddec_ref[h]) for h in range(HG_HEADS)))
        for h in range(HG_HEADS):
            ln = _head_lanes(h)
            dq, df, di, dl0, dl1 = grads[N_HG_IN * h:N_HG_IN * (h + 1)]
            dq_ref[:, ln] = dq.astype(dq_ref.dtype)
            df_ref[:, ln] = df.astype(df_ref.dtype)
            di_ref[:, ln] = (di + dis_ref[h]).astype(di_ref.dtype)
            dl0_ref[:, ln] += dl0
            dl1_ref[:, ln] += dl1

    dcol = jax.ShapeDtypeStruct((T, HG_WIDTH), BF16)
    dvec = jax.ShapeDtypeStruct((1, HG_WIDTH), F32)
    blk = pltpu.VMEM((HG_HEADS, tb, LANES), F32)
    return pl.pallas_call(
        body, name=name, grid=(nT,),
        in_specs=[col(0), col(1), col(2), col(3), st, col(do_blk), vec, vec, vec],
        out_specs=[col(0)] * 4 + [vec] * 3,
        out_shape=[dcol] * 4 + [dvec] * 3,
        scratch_shapes=[pltpu.VMEM((HG_HEADS, HG_HEAD_DIM, HG_HEAD_DIM), F32)] + [blk] * 8,
    )(proj, proj, proj, proj, states, do, l0, l1, nw)


def _rw_consts(rows):
    same, tril, stril = _chunk_masks(rows, RW_CHUNK)
    br, bc = _iota2((LANES, LANES), 0), _iota2((LANES, LANES), 1)
    blockdiag = ((br < RW_HEAD_DIM) == (bc < RW_HEAD_DIM)).astype(F32)
    m0 = (_iota2((1, LANES), 1) < RW_HEAD_DIM).astype(F32)
    return same, tril, stril, blockdiag, m0, 1.0 - m0


def _unit_lower_inverses_impl(lows):
    rows = lows[0].shape[0]
    eye = (_iota2(lows[0].shape, 0) == _iota2(lows[0].shape, 1)).astype(F32)
    xs = [low + eye for low in lows]
    ps = [_dot(low, low) for low in lows]
    n = 4
    while n < RW_CHUNK:
        zs = [_dot(jnp.concatenate([p, x], axis=0), p) for p, x in zip(ps, xs)]
        ps = [z[:rows] for z in zs]
        xs = [x + z[rows:] for x, z in zip(xs, zs)]
        n *= 2
    return tuple(x + _dot(x, p) for x, p in zip(xs, ps))


@jax.custom_vjp
def _unit_lower_inverses(lows):
    return _unit_lower_inverses_impl(lows)


def _unit_lower_inverses_fwd(lows):
    xs = _unit_lower_inverses_impl(lows)
    return xs, xs


def _unit_lower_inverses_bwd(xs, dxs):
    ts = [_dot(x, dx, TN) for x, dx in zip(xs, dxs)]
    return (tuple(_dot(t, x, NT) for t, x in zip(ts, xs)),)


_unit_lower_inverses.defvjp(_unit_lower_inverses_fwd, _unit_lower_inverses_bwd)


N_PREP_IN = 12
N_PREP_OUT = 9
RW_GROUP = 2


def _rw_prep(consts, *flat):
    same, tril, stril, blockdiag, m0, m1 = consts
    rows = tril.shape[0]
    masks = (m0, m1)
    pre = []
    for i in range(len(flat) // N_PREP_IN):
        r, kx, v, lw, gd, w0, a0, k_k, k_a, w2p, a2p, g2 = flat[N_PREP_IN * i:N_PREP_IN * (i + 1)]
        xw = w0 + _dot(jnp.tanh(lw), w2p)
        w = jnp.minimum(xw, 0.0) - jnp.log(1.0 + jnp.exp(-jnp.abs(xw))) - 0.5
        ld = -jnp.exp(w)
        a_s = _sigmoid(a0 + _dot(lw, a2p))
        g = _dot(_sigmoid(gd), g2)
        kk = kx * k_k
        kk = kk / jnp.maximum(jnp.sqrt(_dot(kk * kk, blockdiag)), L2_EPS)
        k2 = kx * (1.0 + (a_s - 1.0) * k_a)
        bv = kk * a_s
        acc = _dot(jnp.concatenate([tril, same], axis=0), ld, NN, HIGHEST)
        cum, tot = acc[:rows], acc[rows:]
        ecn = jnp.exp(-cum)
        a_t = -kk * jnp.exp(cum - ld)
        r_t = r * jnp.exp(cum)
        rem = jnp.exp(tot - cum)
        z = _dot(jnp.concatenate([a_t * m0, a_t * m1, r_t * m0, r_t * m1], axis=0),
                 jnp.concatenate([bv * ecn, k2 * ecn], axis=0), NT)
        pre.append((v, a_t, r_t, z, (bv * rem, k2 * rem, jnp.exp(tot), k2, g)))
    heads = [(i, h) for i in range(len(pre)) for h in range(2)]
    za = {ih: pre[ih[0]][3][ih[1] * rows:(ih[1] + 1) * rows] for ih in heads}
    zr = {ih: pre[ih[0]][3][(2 + ih[1]) * rows:(3 + ih[1]) * rows] for ih in heads}
    tinv = dict(zip(heads, _unit_lower_inverses(tuple(za[ih][:, :rows] * stril for ih in heads))))
    lv = {ih: _dot(jnp.concatenate([za[ih][:, rows:] * stril, zr[ih][:, rows:] * tril], axis=0), pre[ih[0]][0])
          for ih in heads}
    wu = {ih: _dot(tinv[ih], jnp.concatenate([pre[ih[0]][1] * masks[ih[1]], lv[ih][:rows]], axis=1)) for ih in heads}
    w_m = {ih: wu[ih][:, :LANES] for ih in heads}
    u_m = {ih: masks[ih[1]] * wu[ih][:, LANES:] for ih in heads}
    qy = {ih: _dot(zr[ih][:, :rows] * tril, jnp.concatenate([w_m[ih], u_m[ih]], axis=1)) for ih in heads}
    outs = []
    for i in range(len(pre)):
        a, b = (i, 0), (i, 1)
        W = w_m[a] + w_m[b]
        U = u_m[a] + u_m[b]
        Q = pre[i][2] + qy[a][:, :LANES] + qy[b][:, :LANES]
        Y0 = qy[a][:, LANES:] + qy[b][:, LANES:] + m0 * lv[a][rows:] + m1 * lv[b][rows:]
        outs.append((W, U, Q, Y0) + pre[i][4])
    return tuple(outs)


def _rw_post(blockdiag, y, r, v, k2, g, r_k, ln_w, ln_b):
    inv_n = 1.0 / RW_HEAD_DIM
    yc = y - _dot(y, blockdiag) * inv_n
    var = _dot(yc * yc, blockdiag) * inv_n
    yn = yc * lax.rsqrt(var + RW_GN_EPS) * ln_w + ln_b
    bonus = _dot(r * k2 * r_k, blockdiag) * v
    return (yn + bonus) * g


N_RW_VEC = 7
N_RW_MAT = 3


def _rw_specs(T, tb, rev):
    nT = T // tb
    tix = (lambda t: nT - 1 - t) if rev else (lambda t: t)
    wide = lambda blk: pl.BlockSpec((tb, RW_WIDTH), lambda t: (tix(t), blk))
    narrow = lambda blk: pl.BlockSpec((tb, LANES), lambda t: (tix(t), blk))
    vec = pl.BlockSpec((1, RW_WIDTH), lambda t: (0, 0))
    mat = pl.BlockSpec((RW_PAIRS, LANES, LANES), lambda t: (0, 0, 0))
    st = pl.BlockSpec((RW_PAIRS, tb // RW_CHUNK, LANES, LANES), lambda t: (0, tix(t), 0, 0))
    lora0 = 3 * RW_WIDTH // LANES
    ins = [wide(0), wide(1), wide(2), narrow(lora0), narrow(lora0 + 1)]
    return nT, wide, vec, mat, st, ins


def _rw_prep_args(p, r_ref, k_ref, v_ref, lw_ref, gd_ref, vrefs, mrefs):
    ln = _head_lanes(p)
    w0, a0, k_k, k_a = [x[:, ln] for x in vrefs[:4]]
    return (r_ref[:, ln], k_ref[:, ln], v_ref[:, ln], lw_ref[...], gd_ref[...], w0, a0, k_k, k_a,
            *[x[p] for x in mrefs])


def _stack_chunks(ref, top, bottom):
    C = RW_CHUNK
    for c in range(ref.shape[0]):
        ref[c, 0:C, :] = top[c * C:(c + 1) * C]
        ref[c, C:2 * C, :] = bottom[c * C:(c + 1) * C]


def _group_args(p0, r_ref, k_ref, v_ref, lw_ref, gd_ref, vrefs, mrefs):
    flat = []
    for p in range(p0, p0 + RW_GROUP):
        flat += list(_rw_prep_args(p, r_ref, k_ref, v_ref, lw_ref, gd_ref, vrefs, mrefs))
    return flat


def _rw_fwd(rws, vecs, mats, name):
    T = rws.shape[0]
    tb = _row_tile(T, SCAN_ROWS)
    nsub = tb // RW_CHUNK
    C = RW_CHUNK
    nT, wide, vec, mat, st, ins = _rw_specs(T, tb, False)

    def body(*refs):
        r_ref, k_ref, v_ref, lw_ref, gd_ref = refs[:5]
        vrefs = refs[5:5 + N_RW_VEC]
        mrefs = refs[5 + N_RW_VEC:5 + N_RW_VEC + N_RW_MAT]
        o_ref, st_ref, s_ref, wq_ref, uy_ref, bk_ref, misc_ref, y_ref = refs[-8:]

        @pl.when(pl.program_id(0) == 0)
        def _():
            s_ref[...] = jnp.zeros_like(s_ref)

        consts = _rw_consts(tb)
        blockdiag = consts[3]
        for p0 in range(0, RW_PAIRS, RW_GROUP):
            outs = _rw_prep(consts, *_group_args(p0, r_ref, k_ref, v_ref, lw_ref, gd_ref, vrefs, mrefs))
            for p, (W, U, Q, Y0, Bg, Kg, dec, k2, g) in zip(range(p0, p0 + RW_GROUP), outs):
                _stack_chunks(wq_ref.at[p], W, Q)
                _stack_chunks(uy_ref.at[p], U, Y0)
                _stack_chunks(bk_ref.at[p], Bg, Kg)
                misc_ref[0, p], misc_ref[1, p], misc_ref[2, p] = dec, k2, g

        def step(c, carry):
            rows = _chunk_rows(c, C)
            states = [s_ref[p] for p in range(RW_PAIRS)]
            for p, S in enumerate(states):
                st_ref[p, c] = S
            pys = [_dot(wq_ref[p, c], S, NT) + uy_ref[p, c] for p, S in enumerate(states)]
            pvs = [jnp.concatenate([py[:C], v_ref[rows, _head_lanes(p)]], axis=0) for p, py in enumerate(pys)]
            updates = [_dot(pv, bk_ref[p, c], TN) for p, pv in enumerate(pvs)]
            for p, S in enumerate(states):
                y_ref[p, rows, :] = pys[p][C:]
                s_ref[p] = (S * misc_ref[0, p, pl.ds(c * C, 1), :] + updates[p]) * blockdiag
            return carry

        lax.fori_loop(0, nsub, step, 0)
        for p in range(RW_PAIRS):
            ln = _head_lanes(p)
            r_k, ln_w, ln_b = [x[:, ln] for x in vrefs[4:]]
            o_ref[:, ln] = _rw_post(blockdiag, y_ref[p], r_ref[:, ln], v_ref[:, ln], misc_ref[1, p], misc_ref[2, p],
                                    r_k, ln_w, ln_b)

    stacked = pltpu.VMEM((RW_PAIRS, nsub, 2 * C, LANES), F32)
    return pl.pallas_call(
        body, name=name, grid=(nT,),
        in_specs=ins + [vec] * N_RW_VEC + [mat] * N_RW_MAT,
        out_specs=[wide(0), st],
        out_shape=[jax.ShapeDtypeStruct((T, RW_WIDTH), F32),
                   jax.ShapeDtypeStruct((RW_PAIRS, T // RW_CHUNK, LANES, LANES), F32)],
        scratch_shapes=[pltpu.VMEM((RW_PAIRS, LANES, LANES), F32), stacked, stacked, stacked,
                        pltpu.VMEM((3, RW_PAIRS, tb, LANES), F32), pltpu.VMEM((RW_PAIRS, tb, LANES), F32)],
    )(rws, rws, rws, rws, rws, *vecs, *mats)


def _rw_bwd(rws, states, do, do_blk, vecs, mats, name):
    T = rws.shape[0]
    tb = _row_tile(T, SCAN_ROWS)
    nsub = tb // RW_CHUNK
    C = RW_CHUNK
    G = RW_GROUP
    nT, wide, vec, mat, st, ins = _rw_specs(T, tb, True)
    nin = 5 + 1 + 1 + N_RW_VEC + N_RW_MAT

    def body(*refs):
        r_ref, k_ref, v_ref, lw_ref, gd_ref = refs[:5]
        st_ref, do_ref = refs[5], refs[6]
        vrefs = refs[7:7 + N_RW_VEC]
        mrefs = refs[7 + N_RW_VEC:nin]
        dr_ref, dk_ref, dv_ref, dlo_ref = refs[nin:nin + 4]
        dvec = refs[nin + 4:nin + 4 + N_RW_VEC]
        dmat = refs[nin + 4 + N_RW_VEC:nin + 4 + N_RW_VEC + N_RW_MAT]
        ds_ref, wq_ref, uy_ref, bk_ref, pv_ref, dec_ref, y_ref, dpre_ref, dvs_ref = refs[-9:]

        @pl.when(pl.program_id(0) == 0)
        def _():
            ds_ref[...] = jnp.zeros_like(ds_ref)
            for x in dvec + dmat:
                x[...] = jnp.zeros_like(x)

        consts = _rw_consts(tb)
        blockdiag = consts[3]
        dlw, dgd = 0.0, 0.0
        for p0 in range(0, RW_PAIRS, G):
            outs, prep_vjp = jax.vjp(functools.partial(_rw_prep, consts),
                                     *_group_args(p0, r_ref, k_ref, v_ref, lw_ref, gd_ref, vrefs, mrefs))
            for q, (W, U, Q, Y0, Bg, Kg, dec, _, _) in enumerate(outs):
                _stack_chunks(wq_ref.at[q], W, Q)
                _stack_chunks(uy_ref.at[q], U, Y0)
                _stack_chunks(bk_ref.at[q], Bg, Kg)
                dec_ref[q] = dec

            def redo(c, carry, p0=p0):
                rows = _chunk_rows(c, C)
                for q in range(G):
                    py = _dot(wq_ref[q, c], st_ref[p0 + q, c], NT) + uy_ref[q, c]
                    y_ref[q, rows, :] = py[C:]
                    pv_ref[q, c, 0:C, :] = py[:C]
                    pv_ref[q, c, C:2 * C, :] = v_ref[rows, _head_lanes(p0 + q)]
                return carry

            lax.fori_loop(0, nsub, redo, 0)
            post = []
            for q in range(G):
                ln = _head_lanes(p0 + q)
                r_k, ln_w, ln_b = [x[:, ln] for x in vrefs[4:]]
                _, post_vjp = jax.vjp(functools.partial(_rw_post, blockdiag), y_ref[q], r_ref[:, ln], v_ref[:, ln],
                                      outs[q][7], outs[q][8], r_k, ln_w, ln_b)
                dy, dr2, dv2, dk2, dg, dr_k, dln_w, dln_b = post_vjp(do_ref[:, ln])
                dpre_ref[q, 3] = dy
                dvs_ref[q] = dv2
                for x, gx in zip(dvec[4:], (dr_k, dln_w, dln_b)):
                    x[:, ln] += gx
                dpre_ref[q, 6] = jnp.zeros_like(dpre_ref[q, 6])
                post.append((dr2, dk2, dg))

            def step(i, carry, p0=p0):
                c = nsub - 1 - i
                rows = _chunk_rows(c, C)
                row0 = pl.ds(c * C, 1)
                qs = range(G)
                Gs = [ds_ref[p0 + q] * blockdiag for q in qs]
                Ss = [st_ref[p0 + q, c] for q in qs]
                t1 = [_dot(bk_ref[q, c], Gs[q], NT) for q in qs]
                t3 = [_dot(pv_ref[q, c], Gs[q]) for q in qs]
                dpy = [jnp.concatenate([t1[q][:C], dpre_ref[q, 3, rows, :]], axis=0) for q in qs]
                t2 = [_dot(dpy[q], Ss[q]) for q in qs]
                back = [_dot(dpy[q], wq_ref[q, c], TN) for q in qs]
                for q in qs:
                    dvs_ref[q, rows, :] += t1[q][C:]
                    dpre_ref[q, 0, rows, :] = t2[q][:C]
                    dpre_ref[q, 1, rows, :] = t1[q][:C]
                    dpre_ref[q, 2, rows, :] = t2[q][C:]
                    dpre_ref[q, 4, rows, :] = t3[q][:C]
                    dpre_ref[q, 5, rows, :] = t3[q][C:]
                    dpre_ref[q, 6, row0, :] = jnp.sum(Ss[q] * Gs[q], axis=0, keepdims=True)
                    ds_ref[p0 + q] = Gs[q] * dec_ref[q, row0, :] + back[q]
                return carry

            lax.fori_loop(0, nsub, step, 0)
            grads = prep_vjp(tuple(tuple(dpre_ref[q, i] for i in range(7)) + post[q][1:] for q in range(G)))
            for q in range(G):
                ln = _head_lanes(p0 + q)
                gq = grads[N_PREP_IN * q:N_PREP_IN * (q + 1)]
                dr_ref[:, ln] = gq[0] + post[q][0]
                dk_ref[:, ln] = gq[1]
                dv_ref[:, ln] = gq[2] + dvs_ref[q]
                dlw = dlw + gq[3]
                dgd = dgd + gq[4]
                for x, gx in zip(dvec[:4], gq[5:9]):
                    x[:, ln] += gx
                for x, gx in zip(dmat, gq[9:]):
                    x[p0 + q] += gx
        dlo_ref[:, 0:LANES] = dlw
        dlo_ref[:, LANES:2 * LANES] = dgd

    dcol = jax.ShapeDtypeStruct((T, RW_WIDTH), F32)
    dlo_spec = pl.BlockSpec((tb, 2 * LANES), lambda t: (nT - 1 - t, 0))
    blk = pltpu.VMEM((G, tb, LANES), F32)
    stacked = pltpu.VMEM((G, nsub, 2 * C, LANES), F32)
    return pl.pallas_call(
        body, name=name, grid=(nT,),
        in_specs=ins + [st, wide(do_blk)] + [vec] * N_RW_VEC + [mat] * N_RW_MAT,
        out_specs=[wide(0)] * 3 + [dlo_spec] + [vec] * N_RW_VEC + [mat] * N_RW_MAT,
        out_shape=[dcol] * 3 + [jax.ShapeDtypeStruct((T, 2 * LANES), F32)]
        + [jax.ShapeDtypeStruct((1, RW_WIDTH), F32)] * N_RW_VEC
        + [jax.ShapeDtypeStruct((RW_PAIRS, LANES, LANES), F32)] * N_RW_MAT,
        scratch_shapes=[pltpu.VMEM((RW_PAIRS, LANES, LANES), F32), stacked, stacked, stacked, stacked, blk, blk,
                        pltpu.VMEM((G, 7, tb, LANES), F32), blk],
    )(rws, rws, rws, rws, rws, states, do, *vecs, *mats)


def _my_index():
    return 4 * lax.axis_index("x") + 2 * lax.axis_index("y") + lax.axis_index("c")


def _peer(bits):
    pos = []
    for name, flip in zip(("x", "y", "c"), bits):
        i = lax.axis_index(name)
        pos.append(1 - i if flip else i)
    return tuple(pos)


def _peer_index(bits):
    x, y, c = _peer(bits)
    return 4 * x + 2 * y + c


def _all_gather(shards, name):
    n = len(shards)
    chips = [(1, 0, 0), (0, 1, 0), (1, 1, 0)]
    sib = (0, 0, 1)

    def body(*refs):
        ins, outs = refs[:n], refs[n:2 * n]
        send_sems, recv_sems, local_sems = refs[2 * n:]

        def rows(k, dev):
            r = ins[k].shape[0]
            return outs[k].at[pl.ds(dev * r, r), :]

        def copy(k, slot, block_dev, to_bits, src=None):
            return pltpu.make_async_remote_copy(
                src_ref=rows(k, block_dev) if src is None else src, dst_ref=rows(k, block_dev),
                send_sem=send_sems.at[k, slot], recv_sem=recv_sems.at[k, slot],
                device_id=_peer(to_bits), device_id_type=MESH_ID)

        me = _my_index()
        started = []
        for k in range(n):
            mine = pltpu.make_async_copy(ins[k], rows(k, me), local_sems.at[k])
            mine.start()
            started.append(mine)
        sends = []
        for k in range(n):
            first = [copy(k, 0, me, sib, src=ins[k])]
            first += [copy(k, 1 + j, me, chip, src=ins[k]) for j, chip in enumerate(chips)]
            for cp in first:
                cp.start()
            sends += first
        for k in range(n):
            for j, chip in enumerate(chips):
                copy(k, 1 + j, _peer_index(chip), chip).wait_recv()
                fwd = copy(k, 4 + j, _peer_index(chip), sib)
                fwd.start()
                sends.append(fwd)
        for k in range(n):
            copy(k, 0, _peer_index(sib), sib).wait_recv()
            for j, chip in enumerate(chips):
                both = (chip[0], chip[1], 1)
                copy(k, 4 + j, _peer_index(both), sib).wait_recv()
        for cp in sends:
            cp.wait_send()
        for cp in started:
            cp.wait()

    any_spec = pl.BlockSpec(memory_space=pl.ANY)
    return pl.pallas_call(
        body, name=name,
        in_specs=[any_spec] * n, out_specs=[any_spec] * n,
        out_shape=[jax.ShapeDtypeStruct((N_DEV * s.shape[0], s.shape[1]), s.dtype) for s in shards],
        scratch_shapes=[pltpu.SemaphoreType.DMA((n, 7)), pltpu.SemaphoreType.DMA((n, 7)),
                        pltpu.SemaphoreType.DMA((n,))],
    )(*shards)


def _exchange(partials, name):
    n = len(partials)
    flips = [(dx, dy, dc) for dx in (0, 1) for dy in (0, 1) for dc in (0, 1)][1:]

    def body(*refs):
        ins, outs = refs[:n], refs[n:2 * n]
        send_sems, recv_sems, local_sems = refs[2 * n:]
        me = _my_index()
        local = []
        for k in range(n):
            cp = pltpu.make_async_copy(ins[k].at[me], outs[k].at[me], local_sems.at[k])
            cp.start()
            local.append(cp)
        copies = []
        for k in range(n):
            for d, bits in enumerate(flips):
                cp = pltpu.make_async_remote_copy(
                    src_ref=ins[k].at[_peer_index(bits)], dst_ref=outs[k].at[me],
                    send_sem=send_sems.at[k, d], recv_sem=recv_sems.at[k, d],
                    device_id=_peer(bits), device_id_type=MESH_ID)
                cp.start()
                copies.append(cp)
        for cp in copies:
            cp.wait_recv()
        for cp in copies:
            cp.wait_send()
        for cp in local:
            cp.wait()

    any_spec = pl.BlockSpec(memory_space=pl.ANY)
    return pl.pallas_call(
        body, name=name,
        in_specs=[any_spec] * n, out_specs=[any_spec] * n,
        out_shape=[jax.ShapeDtypeStruct(p.shape, p.dtype) for p in partials],
        scratch_shapes=[pltpu.SemaphoreType.DMA((n, 7)), pltpu.SemaphoreType.DMA((n, 7)),
                        pltpu.SemaphoreType.DMA((n,))],
    )(*partials)


HBM_SPEC = pl.BlockSpec(memory_space=pltpu.HBM)
SEM_SPEC = pl.BlockSpec(memory_space=pltpu.SEMAPHORE)
ALL_FLIPS = [(dx, dy, dc) for dx in (0, 1) for dy in (0, 1) for dc in (0, 1)][1:]


def _spread_copies(srcs, lands, send_sems, recv_sems, to_x):
    me = _my_index()
    my_x = lax.axis_index("x")
    copies = []
    for k, land in enumerate(lands):
        for d, bits in enumerate(ALL_FLIPS):
            if not srcs:
                src = land.at[me]
            elif to_x is None:
                src = srcs[k].at[_peer_index(bits)]
            else:
                _, py, pc = _peer(bits)
                src = srcs[k].at[2 * py + pc]
            cp = pltpu.make_async_remote_copy(
                src_ref=src, dst_ref=land.at[me],
                send_sem=send_sems.at[k * 7 + d], recv_sem=recv_sems.at[k * 7 + d],
                device_id=_peer(bits), device_id_type=MESH_ID)
            sends = True if to_x is None else my_x == (to_x ^ bits[0])
            receives = True if to_x is None else my_x == to_x
            copies.append((cp, sends, receives))
    return copies


def _when(cond, fn):
    if cond is True:
        fn()
    else:
        pl.when(cond)(fn)


def _spread_start(srcs, lands, name, to_x=None):
    ns, n = len(srcs), len(lands)

    def body(*refs):
        src_refs, land_refs = refs[:ns], refs[ns:ns + n]
        send_sems, recv_sems = refs[ns + n], refs[ns + n + 1]
        token = refs[-1]
        for cp, sends, _ in _spread_copies(src_refs, land_refs, send_sems, recv_sems, to_x):
            _when(sends, cp.start)
        token[...] = jnp.zeros_like(token)

    bufs = list(srcs) + list(lands)
    out = pl.pallas_call(
        body, name=name,
        out_shape=(pltpu.SemaphoreType.DMA((7 * n,)), pltpu.SemaphoreType.DMA((7 * n,)),
                   *[pltpu.HBM(b.shape, b.dtype) for b in bufs], jax.ShapeDtypeStruct((8, LANES), F32)),
        in_specs=[HBM_SPEC] * (ns + n),
        out_specs=(SEM_SPEC, SEM_SPEC, *[HBM_SPEC] * (ns + n), pl.BlockSpec(memory_space=pltpu.VMEM)),
        input_output_aliases={i: 2 + i for i in range(ns + n)},
        compiler_params=pltpu.CompilerParams(has_side_effects=pltpu.SideEffectType.DATAFLOW_SIDE_EFFECTING),
    )(*[pltpu.with_memory_space_constraint(b, pltpu.HBM) for b in bufs])
    return out[0], out[1], list(out[2:2 + ns]), list(out[2 + ns:2 + ns + n]), out[-1]


def _spread_wait(send_sems, recv_sems, srcs, lands, after, name, to_x=None):
    ns, n = len(srcs), len(lands)

    def body(*refs):
        src_refs, land_refs = refs[:ns], refs[ns:ns + n]
        send_sems, recv_sems = refs[ns + n], refs[ns + n + 1]
        for cp, sends, receives in _spread_copies(src_refs, land_refs, send_sems, recv_sems, to_x):
            _when(sends, cp.wait_send)
            _when(receives, cp.wait_recv)

    bufs = list(srcs) + list(lands)
    out = pl.pallas_call(
        body, name=name,
        out_shape=tuple(pltpu.HBM(b.shape, b.dtype) for b in bufs),
        in_specs=[HBM_SPEC] * (ns + n) + [SEM_SPEC, SEM_SPEC, pl.BlockSpec(memory_space=pl.ANY)],
        out_specs=tuple([HBM_SPEC] * (ns + n)),
        input_output_aliases={i: i for i in range(ns + n)},
        compiler_params=pltpu.CompilerParams(has_side_effects=pltpu.SideEffectType.DATAFLOW_SIDE_EFFECTING),
    )(*bufs, send_sems, recv_sems, after)
    return list(out[ns:])


def _own_slot_only(block, me):
    return lax.dynamic_update_slice(lax.empty((N_DEV,) + block.shape, block.dtype), block[None], (me, 0, 0))


def _sum_slots(landed, name):
    _, R, C = landed.shape
    tb = _row_tile(R, 128)

    def body(l_ref, o_ref):
        acc = l_ref[0].astype(F32)
        for s in range(1, N_DEV):
            acc = acc + l_ref[s].astype(F32)
        o_ref[...] = acc

    return pl.pallas_call(
        body, name=name, grid=(R // tb,),
        in_specs=[pl.BlockSpec((N_DEV, tb, C), lambda i: (0, i, 0))],
        out_specs=pl.BlockSpec((tb, C), lambda i: (i, 0)),
        out_shape=jax.ShapeDtypeStruct((R, C), F32),
    )(landed)


def _pack_rows(flat_list, width=LANES):
    flat = jnp.concatenate([a.reshape(-1) for a in flat_list])
    n = flat.shape[0]
    rows = -(-n // width)
    rows = -(-rows // 8) * 8
    return jnp.pad(flat, (0, rows * width - n)).reshape(rows, width)


def _unpack(packed, shapes):
    flat = packed.reshape(-1)
    out, off = [], 0
    for s in shapes:
        n = 1
        for d in s:
            n *= d
        out.append(flat[off:off + n].reshape(s))
        off += n
    return out


def kernel(x, norm1_w, w_in, hg_lb_logits, hg_norm_w, rw_shift_mu, rw_w0, rw_w2, rw_a0, rw_a2, rw_g2, rw_k_k, rw_k_a, rw_r_k, rw_ln_w, rw_ln_b, w_out, norm2_w, w_up, conv_w, conv_b, w_down, final_norm_w, loss_target, m_norm1_w, m_w_in, m_hg_lb_logits, m_hg_norm_w, m_rw_shift_mu, m_rw_w0, m_rw_w2, m_rw_a0, m_rw_a2, m_rw_g2, m_rw_k_k, m_rw_k_a, m_rw_r_k, m_rw_ln_w, m_rw_ln_b, m_w_out, m_norm2_w, m_w_up, m_conv_w, m_conv_b, m_w_down, m_final_norm_w, v_norm1_w, v_w_in, v_hg_lb_logits, v_hg_norm_w, v_rw_shift_mu, v_rw_w0, v_rw_w2, v_rw_a0, v_rw_a2, v_rw_g2, v_rw_k_k, v_rw_k_a, v_rw_r_k, v_rw_ln_w, v_rw_ln_b, v_w_out, v_norm2_w, v_w_up, v_conv_w, v_conv_b, v_w_down, v_final_norm_w):
    weights = dict(norm1_w=norm1_w, w_in=w_in, hg_lb_logits=hg_lb_logits, hg_norm_w=hg_norm_w,
                   rw_shift_mu=rw_shift_mu, rw_w0=rw_w0, rw_w2=rw_w2, rw_a0=rw_a0, rw_a2=rw_a2, rw_g2=rw_g2,
                   rw_k_k=rw_k_k, rw_k_a=rw_k_a, rw_r_k=rw_r_k, rw_ln_w=rw_ln_w, rw_ln_b=rw_ln_b, w_out=w_out,
                   norm2_w=norm2_w, w_up=w_up, conv_w=conv_w, conv_b=conv_b, w_down=w_down,
                   final_norm_w=final_norm_w)
    m_in = dict(norm1_w=m_norm1_w, w_in=m_w_in, hg_lb_logits=m_hg_lb_logits, hg_norm_w=m_hg_norm_w,
                rw_shift_mu=m_rw_shift_mu, rw_w0=m_rw_w0, rw_w2=m_rw_w2, rw_a0=m_rw_a0, rw_a2=m_rw_a2,
                rw_g2=m_rw_g2, rw_k_k=m_rw_k_k, rw_k_a=m_rw_k_a, rw_r_k=m_rw_r_k, rw_ln_w=m_rw_ln_w,
                rw_ln_b=m_rw_ln_b, w_out=m_w_out, norm2_w=m_norm2_w, w_up=m_w_up, conv_w=m_conv_w,
                conv_b=m_conv_b, w_down=m_w_down, final_norm_w=m_final_norm_w)
    v_in = dict(norm1_w=v_norm1_w, w_in=v_w_in, hg_lb_logits=v_hg_lb_logits, hg_norm_w=v_hg_norm_w,
                rw_shift_mu=v_rw_shift_mu, rw_w0=v_rw_w0, rw_w2=v_rw_w2, rw_a0=v_rw_a0, rw_a2=v_rw_a2,
                rw_g2=v_rw_g2, rw_k_k=v_rw_k_k, rw_k_a=v_rw_k_a, rw_r_k=v_rw_r_k, rw_ln_w=v_rw_ln_w,
                rw_ln_b=v_rw_ln_b, w_out=v_w_out, norm2_w=v_norm2_w, w_up=v_w_up, conv_w=v_conv_w,
                conv_b=v_conv_b, w_down=v_w_down, final_norm_w=v_final_norm_w)
    names = list(weights)
    sharded_small = ["rw_w2", "rw_a2", "rw_g2", "conv_w"]
    sharded_big = ["w_in", "w_out", "w_up", "w_down"]
    replicated = [n for n in names if n not in sharded_small + sharded_big]

    xs = x[0]
    tgt = loss_target[0]

    small_shard = _pack_rows([weights[n] for n in sharded_small])
    g_win_t, g_small = _all_gather([w_in[0].T.astype(BF16), small_shard], "gather_weights")
    me = _my_index()
    later = (w_up[0].T.astype(BF16), w_out[0].astype(BF16), w_down[0].astype(BF16))
    later, _ = lax.optimization_barrier((later, g_small))
    later = [_own_slot_only(z, me) for z in later]
    g_send, g_recv, _, later, g_token = _spread_start([], later, "gather_later_start")
    small_shapes = [weights[n].shape for n in sharded_small]
    per_dev = [_unpack(g_small.reshape(N_DEV, -1)[j], small_shapes) for j in range(N_DEV)]
    w2_full, a2_full, g2_full, convw_full = [jnp.concatenate([per_dev[j][i][0] for j in range(N_DEV)], axis=-1)
                                             for i in range(4)]
    zeros64 = jnp.zeros((RW_PAIRS, 64, LANES), F32)
    by_pair = lambda z: z.reshape(z.shape[0], RW_PAIRS, LANES).transpose(1, 0, 2)
    w2p = jnp.concatenate([by_pair(w2_full), zeros64], axis=1)
    a2p = jnp.concatenate([zeros64, by_pair(a2_full)], axis=1)
    g2p = by_pair(g2_full)

    l0, l1 = hg_lb_logits[0:1], hg_lb_logits[1:2]
    h1 = _rms_fwd(xs, norm1_w + g_token[0:1, 0:1], "norm1")
    proj = _mm_nt(h1, g_win_t, "proj_in")
    o_hg, hg_states = _hg_fwd(proj, l0, l1, hg_norm_w, "hgrn2_fwd")
    rws = _shift_fwd(proj, rw_shift_mu, "token_shift")
    rw_vecs = [rw_w0, rw_a0, rw_k_k, rw_k_a, rw_r_k, rw_ln_w, rw_ln_b]
    rw_mats = [w2p, a2p, g2p]
    o_rw, rw_states = _rw_fwd(rws, rw_vecs, rw_mats, "rwkv7_fwd")
    o_mix = jnp.concatenate([o_hg, o_rw], axis=-1).astype(BF16)
    g_wup_t, g_wout, g_wdown = [z.reshape(-1, z.shape[-1])
                                for z in _spread_wait(g_send, g_recv, [], later, o_mix, "gather_later_wait")]
    x1 = _mm_nn(o_mix, g_wout, xs, "proj_out")
    h2 = _rms_fwd(x1, norm2_w, "norm2")
    u = _mm_nt(h2, g_wup_t, "ffn_up")
    act = _ffn_act_fwd(u, convw_full, conv_b, "ffn_act")
    x2 = _mm_nn(act, g_wdown, x1, "ffn_down")
    loss_part, dx2, d_final_w = _loss_head(x2, final_norm_w.reshape(1, -1), tgt, "loss_head")

    d_wdown = _mm_tn(act, dx2, 1408, "ffn_down_dw", BF16)
    dact = _mm_nt(dx2, g_wdown, "ffn_down_dx", BF16)
    du_g, du_v, dcw_g, dcw_v, dcb_g, dcb_v = _ffn_act_bwd(u, dact, convw_full, conv_b, "ffn_act_bwd")
    d_convw = jnp.concatenate([dcw_g, dcw_v], axis=-1)
    d_convb = jnp.concatenate([dcb_g, dcb_v], axis=-1)
    d_wup_t = jnp.concatenate([_mm_tn(du_g, h2, 1408, "ffn_up_dw_gate", BF16),
                               _mm_tn(du_v, h2, 1408, "ffn_up_dw_value", BF16)], axis=0)
    dh2 = _mm_nn([du_g, du_v], g_wup_t, None, "ffn_up_dx")
    dx1, d_norm2 = _rms_bwd(dh2, x1, norm2_w, dx2, "norm2_bwd")
    d_wout = _mm_tn(o_mix, dx1, 512, "proj_out_dw", BF16)
    do = _mm_nt(dx1, g_wout, "proj_out_dx")
    early = [z.reshape(N_DEV, z.shape[0] // N_DEV, z.shape[1]) for z in (d_wup_t, d_wout, d_wdown)]
    early_land = [_own_slot_only(lax.dynamic_index_in_dim(z, me, 0, keepdims=False), me) for z in early]
    e_send, e_recv, early, early_land, e_token = _spread_start(early, early_land, "exchange_early_start")
    hg_norm_w_t = hg_norm_w + e_token[0:1, 0:1]
    dq, df, di, dg, d_l0, d_l1, d_hg_nw = _hg_bwd(proj, hg_states, do, 0, l0, l1, hg_norm_w_t, "hgrn2_bwd")
    half = N_DEV // 2
    own_half_block = lambda z: _own_slot_only(lax.dynamic_index_in_dim(z, me % half, 0, keepdims=False), me)
    n_lo = half * w_in.shape[2]
    d_win_lo = _mm_tn(jnp.concatenate([dq, df, di, dg[:, :n_lo - 3 * HG_WIDTH]], axis=-1), h1, 640,
                      "proj_in_dw_low", BF16).reshape(half, -1, D_MODEL)
    m_send, m_recv, mid, mid_land, m_token = _spread_start([d_win_lo], [own_half_block(d_win_lo)],
                                                            "exchange_mid_start", to_x=0)
    rw_vecs_t = [rw_vecs[0] + m_token[0:1, 0:1]] + rw_vecs[1:]
    rw_out = _rw_bwd(rws, rw_states, do, 1, rw_vecs_t, rw_mats, "rwkv7_bwd")
    d_rw_vecs = rw_out[4:4 + N_RW_VEC]
    d_w2p, d_a2p, d_g2p = rw_out[4 + N_RW_VEC:]
    dp_parts, dmu_parts = [], []
    for i, z in enumerate(rw_out[:4]):
        dp, dmu = _shift_bwd(z, proj, rw_shift_mu, i * RW_WIDTH, "token_shift_bwd_%d" % i)
        dp_parts.append(dp)
        dmu_parts.append(dmu)
    d_mu = jnp.concatenate(dmu_parts, axis=-1)
    d_win_hi = _mm_tn(jnp.concatenate([dg[:, n_lo - 3 * HG_WIDTH:]] + dp_parts, axis=-1), h1, 640,
                      "proj_in_dw_high", BF16).reshape(half, -1, D_MODEL)
    from_pairs = lambda z: z.transpose(1, 0, 2).reshape(z.shape[1], RW_WIDTH)
    d_w2 = from_pairs(d_w2p[:, :64])
    d_a2 = from_pairs(d_a2p[:, 64:])
    d_g2 = from_pairs(d_g2p)
    col_blocks = lambda z: z.reshape(z.shape[0], N_DEV, -1).transpose(1, 0, 2)
    small_part = jnp.stack([
        _pack_rows([col_blocks(d_w2)[j], col_blocks(d_a2)[j], col_blocks(d_g2)[j], col_blocks(d_convw)[j]])
        for j in range(N_DEV)])
    l_send, l_recv, late, late_land, l_token = _spread_start([d_win_hi], [own_half_block(d_win_hi)],
                                                             "exchange_late_start", to_x=1)
    dh1 = _mm_nn([dq, df, di, dg] + dp_parts, g_win_t, None, "proj_in_dx")
    grad_x, d_norm1 = _rms_bwd(dh1, xs, norm1_w + l_token[0:1, 0:1], dx1, "norm1_bwd")

    rep_grads = dict(norm1_w=d_norm1, hg_lb_logits=jnp.concatenate([d_l0, d_l1], axis=0), hg_norm_w=d_hg_nw,
                     rw_shift_mu=d_mu, rw_w0=d_rw_vecs[0], rw_a0=d_rw_vecs[1], rw_k_k=d_rw_vecs[2],
                     rw_k_a=d_rw_vecs[3], rw_r_k=d_rw_vecs[4], rw_ln_w=d_rw_vecs[5], rw_ln_b=d_rw_vecs[6],
                     norm2_w=d_norm2, conv_b=d_convb, final_norm_w=d_final_w)
    rep_pack = _pack_rows([loss_part] + [rep_grads[n] for n in replicated])
    rep_part = jnp.broadcast_to(rep_pack[None], (N_DEV,) + rep_pack.shape)
    grads, delta, new_m, new_v = {}, {}, {}, {}

    def adamw_big(n, g):
        shp = weights[n].shape
        as2d = lambda z: z.reshape(shp[1], shp[2])
        grads[n] = g[None]
        d, nm, nv = _adamw(as2d(weights[n]), g, as2d(m_in[n]), as2d(v_in[n]), "adamw_" + n)
        delta[n], new_m[n], new_v[n] = d.reshape(shp), nm.reshape(shp), nv.reshape(shp)

    landed_early = _spread_wait(e_send, e_recv, early, early_land, grad_x, "exchange_early_wait")
    adamw_big("w_up", _sum_slots(landed_early[0], "sum_grads_w_up").T)
    adamw_big("w_out", _sum_slots(landed_early[1], "sum_grads_w_out"))
    adamw_big("w_down", _sum_slots(landed_early[2], "sum_grads_w_down"))
    (landed_mid,) = _spread_wait(m_send, m_recv, mid, mid_land, grad_x, "exchange_mid_wait", to_x=0)
    (landed_late,) = _spread_wait(l_send, l_recv, late, late_land, delta["w_down"], "exchange_late_wait", to_x=1)
    g_win = jnp.where(lax.axis_index("x") == 0, _sum_slots(landed_mid, "sum_grads_w_in_low"),
                      _sum_slots(landed_late, "sum_grads_w_in_high"))
    adamw_big("w_in", g_win.T)
    landed_rep, landed_small = _exchange([rep_part, small_part], "exchange_grads")
    g_small_sum = _unpack(_sum_slots(landed_small, "sum_grads_small"), small_shapes)
    rep_sum = _unpack(_sum_slots(landed_rep, "sum_grads_replicated"), [(1, 1)] + [weights[n].shape for n in replicated])
    loss = rep_sum[0].reshape(())
    grads.update(dict(zip(replicated, rep_sum[1:])))
    grads.update(dict(zip(sharded_small, g_small_sum)))

    small_names = replicated + sharded_small
    packs = [_pack_rows([src[n] for n in small_names]) for src in (weights, grads, m_in, v_in)]
    outs = _adamw(*packs, "adamw_small")
    small_shapes_all = [weights[n].shape for n in small_names]
    for dst, packed in zip((delta, new_m, new_v), outs):
        dst.update(dict(zip(small_names, _unpack(packed, small_shapes_all))))

    return (loss, grad_x[None], *[grads[n] for n in names], *[delta[n] for n in names],
            *[new_m[n] for n in names], *[new_v[n] for n in names])
```

```python
import functools

import jax
import jax.numpy as jnp
from jax import lax
from jax.experimental import pallas as pl
from jax.experimental.pallas import tpu as pltpu

F32 = jnp.float32
BF16 = jnp.bfloat16
HIGHEST = lax.Precision.HIGHEST
SCAN_PRECISION = None
MESH_ID = pl.DeviceIdType.MESH

N_DEV = 8
D_MODEL = 1024
HG_WIDTH = 512
HG_HEAD_DIM = 128
HG_HEADS = 4
RW_WIDTH = 512
RW_PAIRS = 4
RW_HEAD_DIM = 64
HG_COLS = 2048
RW_COLS = 1792
D_FF = 2816
NORM_EPS = 1e-6
RW_GN_EPS = 64e-5
L2_EPS = 1e-12
ADAM_LR, ADAM_B1, ADAM_B2, ADAM_EPS, ADAM_WD, ADAM_STEP = 0.001, 0.9, 0.999, 1e-08, 0.01, 10

HG_CHUNK = 32
HG_HALF = 16
RW_CHUNK = 64
SCAN_ROWS = 256
LANES = 128

NN = ((1,), (0,))
NT = ((1,), (1,))
TN = ((0,), (0,))


def _dot(a, b, dims=NN, precision=SCAN_PRECISION):
    if precision is None:
        a, b = a.astype(BF16), b.astype(BF16)
    return lax.dot_general(a, b, (dims, ((), ())), precision=precision, preferred_element_type=F32)


def _iota2(shape, dim):
    return lax.broadcasted_iota(jnp.int32, shape, dim)


def _sigmoid(z):
    return 0.5 * jnp.tanh(0.5 * z) + 0.5


def _row_tile(n, want):
    t = min(n, want)
    while n % t:
        t //= 2
    return t


def _rms_fwd(x, w, name):
    T, D = x.shape
    tb = _row_tile(T, 512)

    def body(x_ref, w_ref, h_ref):
        xv = x_ref[...]
        r = lax.rsqrt(jnp.mean(xv * xv, axis=-1, keepdims=True) + NORM_EPS)
        h_ref[...] = (xv * r * w_ref[...]).astype(h_ref.dtype)

    return pl.pallas_call(
        body, name=name, grid=(T // tb,),
        in_specs=[pl.BlockSpec((tb, D), lambda i: (i, 0)), pl.BlockSpec((1, D), lambda i: (0, 0))],
        out_specs=pl.BlockSpec((tb, D), lambda i: (i, 0)),
        out_shape=jax.ShapeDtypeStruct((T, D), BF16),
    )(x, w)


def _rms_bwd(dh, x, w, dres, name):
    T, D = x.shape
    tb = _row_tile(T, 256)

    def body(dh_ref, x_ref, w_ref, dres_ref, dx_ref, dw_ref):
        @pl.when(pl.program_id(0) == 0)
        def _():
            dw_ref[...] = jnp.zeros_like(dw_ref)

        xv = x_ref[...]
        r = lax.rsqrt(jnp.mean(xv * xv, axis=-1, keepdims=True) + NORM_EPS)
        xn = xv * r
        dy = dh_ref[...].astype(F32)
        dxn = dy * w_ref[...]
        dx_ref[...] = dres_ref[...] + r * (dxn - xn * jnp.mean(dxn * xn, axis=-1, keepdims=True))
        dw_ref[...] += jnp.sum(dy * xn, axis=0, keepdims=True)

    row = pl.BlockSpec((tb, D), lambda i: (i, 0))
    vec = pl.BlockSpec((1, D), lambda i: (0, 0))
    return pl.pallas_call(
        body, name=name, grid=(T // tb,),
        in_specs=[row, row, vec, row], out_specs=[row, vec],
        out_shape=[jax.ShapeDtypeStruct((T, D), F32), jax.ShapeDtypeStruct((1, D), F32)],
    )(dh, x, w, dres)


def _mm_nt(a, bt, name, out_dtype=F32):
    T, K = a.shape
    N = bt.shape[0]
    tm = _row_tile(T, 256)

    def body(a_ref, b_ref, o_ref):
        o_ref[...] = _dot(a_ref[...].astype(BF16), b_ref[...].astype(BF16), NT, None).astype(o_ref.dtype)

    return pl.pallas_call(
        body, name=name, grid=(T // tm,),
        in_specs=[pl.BlockSpec((tm, K), lambda i: (i, 0)), pl.BlockSpec((N, K), lambda i: (0, 0))],
        out_specs=pl.BlockSpec((tm, N), lambda i: (i, 0)),
        out_shape=jax.ShapeDtypeStruct((T, N), out_dtype),
    )(a, bt)


def _mm_nn(a, b, res, name, out_dtype=F32):
    parts = list(a) if isinstance(a, (list, tuple)) else [a]
    T = parts[0].shape[0]
    K, N = b.shape
    tm = _row_tile(T, 256)
    widths = [p.shape[1] for p in parts]
    n = len(parts)

    def body(*refs):
        b_ref, o_ref = refs[n], refs[-1]
        acc, off = None, 0
        for a_ref, w in zip(refs[:n], widths):
            d = _dot(a_ref[...].astype(BF16), b_ref[off:off + w, :].astype(BF16), NN, None)
            acc = d if acc is None else acc + d
            off += w
        if res is not None:
            acc = acc + refs[n + 1][...]
        o_ref[...] = acc.astype(o_ref.dtype)

    in_specs = [pl.BlockSpec((tm, w), lambda i: (i, 0)) for w in widths] + [pl.BlockSpec((K, N), lambda i: (0, 0))]
    args = parts + [b]
    if res is not None:
        in_specs.append(pl.BlockSpec((tm, N), lambda i: (i, 0)))
        args.append(res)
    return pl.pallas_call(
        body, name=name, grid=(T // tm,), in_specs=in_specs,
        out_specs=pl.BlockSpec((tm, N), lambda i: (i, 0)),
        out_shape=jax.ShapeDtypeStruct((T, N), out_dtype),
    )(*args)


def _mm_tn(a, b, tmm, name, out_dtype=F32):
    T, M = a.shape
    N = b.shape[1]
    tk = _row_tile(T, 512)
    nk = T // tk

    def body(a_ref, b_ref, o_ref, acc_ref):
        @pl.when(pl.program_id(1) == 0)
        def _():
            acc_ref[...] = jnp.zeros_like(acc_ref)

        acc_ref[...] += _dot(a_ref[...].astype(BF16), b_ref[...].astype(BF16), TN, None)

        @pl.when(pl.program_id(1) == nk - 1)
        def _():
            o_ref[...] = acc_ref[...].astype(o_ref.dtype)

    return pl.pallas_call(
        body, name=name, grid=(M // tmm, nk),
        in_specs=[pl.BlockSpec((tk, tmm), lambda m, k: (k, m)), pl.BlockSpec((tk, N), lambda m, k: (k, 0))],
        out_specs=pl.BlockSpec((tmm, N), lambda m, k: (m, 0)),
        out_shape=jax.ShapeDtypeStruct((M, N), out_dtype),
        scratch_shapes=[pltpu.VMEM((tmm, N), F32)],
    )(a, b)


class _RowShifts:
    def __init__(self, shape):
        index = _iota2(shape, 0)
        self.rows = shape[0]
        self.first = {n: index < n for n in (1, 2)}
        self.last = {n: index >= shape[0] - n for n in (1, 2)}

    def down(self, z, n):
        return jnp.where(self.first[n], 0.0, pltpu.roll(z, n, 0))

    def up(self, z, n):
        return jnp.where(self.last[n], 0.0, pltpu.roll(z, self.rows - n, 0))


def _shift_fwd(proj, mu, name):
    T = proj.shape[0]
    nblk = RW_COLS // LANES
    first = HG_COLS // LANES

    def body(p_ref, mu_ref, o_ref):
        p = p_ref[...]
        o_ref[...] = p + (_RowShifts(p.shape).down(p, 1) - p) * mu_ref[...]

    return pl.pallas_call(
        body, name=name, grid=(nblk,),
        in_specs=[pl.BlockSpec((T, LANES), lambda j: (0, first + j)), pl.BlockSpec((1, LANES), lambda j: (0, j))],
        out_specs=pl.BlockSpec((T, LANES), lambda j: (0, j)),
        out_shape=jax.ShapeDtypeStruct((T, RW_COLS), F32),
    )(proj, mu)


def _shift_bwd(ds, proj, mu, col0, name):
    T, width = ds.shape
    nblk = width // LANES
    first = (HG_COLS + col0) // LANES
    mu0 = col0 // LANES

    def body(ds_ref, p_ref, mu_ref, dp_ref, dmu_ref):
        dsv = ds_ref[...]
        p = p_ref[...]
        m = mu_ref[...]
        shifts = _RowShifts(p.shape)
        dp_ref[...] = (dsv * (1.0 - m) + shifts.up(dsv * m, 1)).astype(dp_ref.dtype)
        dmu_ref[...] = jnp.sum(dsv * (shifts.down(p, 1) - p), axis=0, keepdims=True)

    return pl.pallas_call(
        body, name=name, grid=(nblk,),
        in_specs=[pl.BlockSpec((T, LANES), lambda j: (0, j)),
                  pl.BlockSpec((T, LANES), lambda j: (0, first + j)),
                  pl.BlockSpec((1, LANES), lambda j: (0, mu0 + j))],
        out_specs=[pl.BlockSpec((T, LANES), lambda j: (0, j)), pl.BlockSpec((1, LANES), lambda j: (0, j))],
        out_shape=[jax.ShapeDtypeStruct((T, width), BF16), jax.ShapeDtypeStruct((1, width), F32)],
    )(ds, proj, mu)


def _conv3(z, w_ref, shifts):
    return w_ref[0:1, :] * shifts.down(z, 2) + w_ref[1:2, :] * shifts.down(z, 1) + w_ref[2:3, :] * z


def _ffn_act_fwd(u, conv_w, conv_b, name):
    T = u.shape[0]
    nblk = D_FF // LANES

    def body(ug_ref, uv_ref, wg_ref, wv_ref, bg_ref, bv_ref, act_ref):
        shifts = _RowShifts((T, LANES))
        gate = _conv3(ug_ref[...], wg_ref, shifts) + bg_ref[...]
        val = _conv3(uv_ref[...], wv_ref, shifts) + bv_ref[...]
        act_ref[...] = (gate * _sigmoid(gate) * val).astype(act_ref.dtype)

    col = lambda off: pl.BlockSpec((T, LANES), lambda j: (0, off + j))
    wsp = lambda off: pl.BlockSpec((3, LANES), lambda j: (0, off + j))
    bsp = lambda off: pl.BlockSpec((1, LANES), lambda j: (0, off + j))
    return pl.pallas_call(
        body, name=name, grid=(nblk,),
        in_specs=[col(0), col(nblk), wsp(0), wsp(nblk), bsp(0), bsp(nblk)],
        out_specs=pl.BlockSpec((T, LANES), lambda j: (0, j)),
        out_shape=jax.ShapeDtypeStruct((T, D_FF), BF16),
    )(u, u, conv_w, conv_w, conv_b, conv_b)


def _ffn_act_bwd(u, dact, conv_w, conv_b, name):
    T = u.shape[0]
    nblk = D_FF // LANES

    def conv_bwd(z, dzc, w_ref, du_ref, dw_ref, db_ref, shifts):
        up1, up2 = shifts.up(dzc, 1), shifts.up(dzc, 2)
        du = w_ref[2:3, :] * dzc + w_ref[1:2, :] * up1 + w_ref[0:1, :] * up2
        du_ref[...] = du.astype(du_ref.dtype)
        dw_ref[0:1, :] = jnp.sum(up2 * z, axis=0, keepdims=True)
        dw_ref[1:2, :] = jnp.sum(up1 * z, axis=0, keepdims=True)
        dw_ref[2:3, :] = jnp.sum(dzc * z, axis=0, keepdims=True)
        db_ref[...] = jnp.sum(dzc, axis=0, keepdims=True)

    def body(ug_ref, uv_ref, da_ref, wg_ref, wv_ref, bg_ref, bv_ref,
             dug_ref, duv_ref, dwg_ref, dwv_ref, dbg_ref, dbv_ref):
        ug, uv = ug_ref[...], uv_ref[...]
        shifts = _RowShifts((T, LANES))
        gate = _conv3(ug, wg_ref, shifts) + bg_ref[...]
        val = _conv3(uv, wv_ref, shifts) + bv_ref[...]
        da = da_ref[...].astype(F32)
        sg = _sigmoid(gate)
        dgate = da * val * (sg * (1.0 + gate * (1.0 - sg)))
        dval = da * gate * sg
        conv_bwd(ug, dgate, wg_ref, dug_ref, dwg_ref, dbg_ref, shifts)
        conv_bwd(uv, dval, wv_ref, duv_ref, dwv_ref, dbv_ref, shifts)

    col = lambda off: pl.BlockSpec((T, LANES), lambda j: (0, off + j))
    wsp = lambda off: pl.BlockSpec((3, LANES), lambda j: (0, off + j))
    bsp = lambda off: pl.BlockSpec((1, LANES), lambda j: (0, off + j))
    half = lambda r, dt: jax.ShapeDtypeStruct((r, D_FF), dt)
    return pl.pallas_call(
        body, name=name, grid=(nblk,),
        in_specs=[col(0), col(nblk), col(0), wsp(0), wsp(nblk), bsp(0), bsp(nblk)],
        out_specs=[col(0), col(0), wsp(0), wsp(0), bsp(0), bsp(0)],
        out_shape=[half(T, BF16), half(T, BF16), half(3, F32), half(3, F32), half(1, F32), half(1, F32)],
    )(u, u, dact, conv_w, conv_w, conv_b, conv_b)


def _loss_head(x2, w, target, name):
    T, D = x2.shape
    tb = _row_tile(T, 256)

    def body(x_ref, w_ref, t_ref, loss_ref, dx_ref, dw_ref):
        @pl.when(pl.program_id(0) == 0)
        def _():
            loss_ref[...] = jnp.zeros_like(loss_ref)
            dw_ref[...] = jnp.zeros_like(dw_ref)

        xv = x_ref[...]
        r = lax.rsqrt(jnp.mean(xv * xv, axis=-1, keepdims=True) + NORM_EPS)
        xn = xv * r
        err = xn * w_ref[...] - t_ref[...]
        row_loss = jnp.sum(err * err, axis=-1, keepdims=True) * (0.5 / D)
        loss_ref[...] += jnp.sum(row_loss, axis=0, keepdims=True)
        dy = err * (1.0 / D)
        dxn = dy * w_ref[...]
        dx_ref[...] = r * (dxn - xn * jnp.mean(dxn * xn, axis=-1, keepdims=True))
        dw_ref[...] += jnp.sum(dy * xn, axis=0, keepdims=True)

    row = pl.BlockSpec((tb, D), lambda i: (i, 0))
    vec = pl.BlockSpec((1, D), lambda i: (0, 0))
    return pl.pallas_call(
        body, name=name, grid=(T // tb,),
        in_specs=[row, vec, row],
        out_specs=[pl.BlockSpec((1, 1), lambda i: (0, 0)), row, vec],
        out_shape=[jax.ShapeDtypeStruct((1, 1), F32), jax.ShapeDtypeStruct((T, D), F32),
                   jax.ShapeDtypeStruct((1, D), F32)],
    )(x2, w, target)


def _adamw(w, g, m, v, name):
    R, C = w.shape
    tb = _row_tile(R, 256) if R % 8 == 0 else R

    def body(w_ref, g_ref, m_ref, v_ref, d_ref, nm_ref, nv_ref):
        gv = g_ref[...]
        nm = ADAM_B1 * m_ref[...] + (1.0 - ADAM_B1) * gv
        nv = ADAM_B2 * v_ref[...] + (1.0 - ADAM_B2) * (gv * gv)
        m_hat = nm / (1.0 - ADAM_B1 ** ADAM_STEP)
        v_hat = nv / (1.0 - ADAM_B2 ** ADAM_STEP)
        d_ref[...] = -ADAM_LR * (m_hat / (jnp.sqrt(v_hat) + ADAM_EPS) + ADAM_WD * w_ref[...])
        nm_ref[...] = nm
        nv_ref[...] = nv

    blk = pl.BlockSpec((tb, C), lambda i: (i, 0))
    sd = jax.ShapeDtypeStruct((R, C), F32)
    return pl.pallas_call(
        body, name=name, grid=(R // tb,), in_specs=[blk] * 4, out_specs=[blk] * 3, out_shape=[sd] * 3,
    )(w, g, m, v)


def _chunk_masks(rows, chunk):
    shift = chunk.bit_length() - 1
    i, j = _iota2((rows, rows), 0), _iota2((rows, rows), 1)
    same = jnp.right_shift(i, shift) == jnp.right_shift(j, shift)
    return same.astype(F32), (same & (j <= i)).astype(F32), (same & (j < i)).astype(F32)


def _head_lanes(h):
    return slice(h * LANES, (h + 1) * LANES)


def _chunk_rows(c, chunk):
    return pl.ds(pl.multiple_of(c * chunk, chunk), chunk)


def _hg_consts(rows):
    same, tril, _ = _chunk_masks(rows, HG_CHUNK)
    half_same, half_tril, _ = _chunk_masks(rows, HG_HALF)
    i, j = _iota2((rows, rows), 0), _iota2((rows, rows), 1)
    half_shift, shift = HG_HALF.bit_length() - 1, HG_CHUNK.bit_length() - 1
    mid_row = jnp.left_shift(jnp.right_shift(i, half_shift), half_shift) + (HG_HALF // 2 - 1)
    bound_row = jnp.left_shift(jnp.right_shift(i, shift), shift) + (HG_HALF - 1)
    upto_mid = ((same > 0) & (j <= mid_row)).astype(F32)
    upto_bound = ((same > 0) & (j <= bound_row)).astype(F32)
    lower_left = tril * (1.0 - half_same)
    return jnp.concatenate([tril, same, upto_mid, upto_bound], axis=0), half_tril, lower_left


N_HG_IN = 5


def _hg_prep(consts, *flat):
    sums, half_tril, lower_left = consts
    rows = half_tril.shape[0]
    heads, logs = [], []
    for h in range(len(flat) // N_HG_IN):
        qr, fr, ir, l0, l1 = flat[N_HG_IN * h:N_HG_IN * (h + 1)]
        lb = _sigmoid(l0 - l1)
        f = lb + (1.0 - lb) * _sigmoid(fr)
        heads.append((qr * _sigmoid(qr) * (HG_HEAD_DIM ** -0.5), 1.0 - f, ir))
        logs.append(jnp.log(f))
    acc = _dot(sums, jnp.concatenate(logs, axis=1), NN, HIGHEST)
    sums_of = []
    for h in range(len(heads)):
        acc_h = acc[:, h * LANES:(h + 1) * LANES]
        sums_of.append(tuple(acc_h[n * rows:(n + 1) * rows] for n in range(4)))
    near = [_dot(q * jnp.exp(a - mid), k * jnp.exp(mid - a), NT) * half_tril
            for (q, k, _), (a, _, mid, _) in zip(heads, sums_of)]
    far = [_dot(q * jnp.exp(jnp.minimum(a - bound, 0.0)), k * jnp.exp(jnp.minimum(bound - a, 0.0)), NT) * lower_left
           for (q, k, _), (a, _, _, bound) in zip(heads, sums_of)]
    intra = [_dot(n + f, ir) for n, f, (_, _, ir) in zip(near, far, heads)]
    return tuple((q * jnp.exp(a), o_intra, k * jnp.exp(tot - a), jnp.exp(tot))
                 for (q, k, _), (a, tot, _, _), o_intra in zip(heads, sums_of, intra))


def _hg_prep_args(q_ref, f_ref, i_ref, l0_ref, l1_ref):
    flat = []
    for h in range(HG_HEADS):
        ln = _head_lanes(h)
        flat += [q_ref[:, ln], f_ref[:, ln], i_ref[:, ln], l0_ref[:, ln], l1_ref[:, ln]]
    return flat


def _hg_post(o, gr, nw):
    on = o * lax.rsqrt(jnp.mean(o * o, axis=-1, keepdims=True) + NORM_EPS)
    return on * nw * (gr * _sigmoid(gr))


def _hg_specs(T, tb, rev):
    nT = T // tb
    tix = (lambda t: nT - 1 - t) if rev else (lambda t: t)
    col = lambda blk: pl.BlockSpec((tb, HG_WIDTH), lambda t: (tix(t), blk))
    vec = pl.BlockSpec((1, HG_WIDTH), lambda t: (0, 0))
    st = pl.BlockSpec((HG_HEADS, tb // HG_CHUNK, HG_HEAD_DIM, HG_HEAD_DIM), lambda t: (0, tix(t), 0, 0))
    return nT, col, vec, st


def _hg_fwd(proj, l0, l1, nw, name):
    T = proj.shape[0]
    tb = _row_tile(T, SCAN_ROWS)
    nsub = tb // HG_CHUNK
    nT, col, vec, st = _hg_specs(T, tb, False)

    def body(q_ref, f_ref, i_ref, g_ref, l0_ref, l1_ref, nw_ref, o_ref, st_ref, s_ref, qe_ref, kd_ref, dec_ref):
        @pl.when(pl.program_id(0) == 0)
        def _():
            s_ref[...] = jnp.zeros_like(s_ref)

        consts = _hg_consts(tb)
        outs = _hg_prep(consts, *_hg_prep_args(q_ref, f_ref, i_ref, l0_ref, l1_ref))
        for h, (qe, o_intra, kd, dec) in enumerate(outs):
            qe_ref[h], kd_ref[h], dec_ref[h] = qe, kd, dec
            o_ref[:, _head_lanes(h)] = o_intra

        def step(c, carry):
            rows = _chunk_rows(c, HG_CHUNK)
            for h in range(HG_HEADS):
                ln = _head_lanes(h)
                S = s_ref[h]
                st_ref[h, c] = S
                o_ref[rows, ln] += _dot(qe_ref[h, rows, :], S, NT)
                s_ref[h] = S * dec_ref[h, pl.ds(c * HG_CHUNK, 1), :] + _dot(i_ref[rows, ln], kd_ref[h, rows, :], TN)
            return carry

        lax.fori_loop(0, nsub, step, 0)
        for h in range(HG_HEADS):
            ln = _head_lanes(h)
            o_ref[:, ln] = _hg_post(o_ref[:, ln], g_ref[:, ln], nw_ref[:, ln])

    blk = pltpu.VMEM((HG_HEADS, tb, LANES), F32)
    return pl.pallas_call(
        body, name=name, grid=(nT,),
        in_specs=[col(0), col(1), col(2), col(3), vec, vec, vec],
        out_specs=[col(0), st],
        out_shape=[jax.ShapeDtypeStruct((T, HG_WIDTH), F32),
                   jax.ShapeDtypeStruct((HG_HEADS, T // HG_CHUNK, HG_HEAD_DIM, HG_HEAD_DIM), F32)],
        scratch_shapes=[pltpu.VMEM((HG_HEADS, HG_HEAD_DIM, HG_HEAD_DIM), F32), blk, blk, blk],
    )(proj, proj, proj, proj, l0, l1, nw)


def _hg_bwd(proj, states, do, do_blk, l0, l1, nw, name):
    T = proj.shape[0]
    tb = _row_tile(T, SCAN_ROWS)
    nsub = tb // HG_CHUNK
    nT, col, vec, st = _hg_specs(T, tb, True)

    def body(q_ref, f_ref, i_ref, g_ref, st_ref, do_ref, l0_ref, l1_ref, nw_ref,
             dq_ref, df_ref, di_ref, dg_ref, dl0_ref, dl1_ref, dnw_ref,
             ds_ref, qe_ref, kd_ref, dec_ref, o_ref, dqe_ref, dkd_ref, ddec_ref, dis_ref):
        @pl.when(pl.program_id(0) == 0)
        def _():
            ds_ref[...] = jnp.zeros_like(ds_ref)
            dl0_ref[...] = jnp.zeros_like(dl0_ref)
            dl1_ref[...] = jnp.zeros_like(dl1_ref)
            dnw_ref[...] = jnp.zeros_like(dnw_ref)

        consts = _hg_consts(tb)
        outs, prep_vjp = jax.vjp(functools.partial(_hg_prep, consts),
                                 *_hg_prep_args(q_ref, f_ref, i_ref, l0_ref, l1_ref))
        for h, (qe, o_intra, kd, dec) in enumerate(outs):
            qe_ref[h], kd_ref[h], dec_ref[h], o_ref[h] = qe, kd, dec, o_intra

        def redo(c, carry):
            rows = _chunk_rows(c, HG_CHUNK)
            for h in range(HG_HEADS):
                o_ref[h, rows, :] += _dot(qe_ref[h, rows, :], st_ref[h, c], NT)
            return carry

        lax.fori_loop(0, nsub, redo, 0)
        for h in range(HG_HEADS):
            ln = _head_lanes(h)
            _, vjp = jax.vjp(_hg_post, o_ref[h], g_ref[:, ln], nw_ref[:, ln])
            d_o, dgr, dnw = vjp(do_ref[:, ln])
            o_ref[h] = d_o
            dg_ref[:, ln] = dgr.astype(dg_ref.dtype)
            dnw_ref[:, ln] += dnw
        ddec_ref[...] = jnp.zeros_like(ddec_ref)

        def step(i, carry):
            c = nsub - 1 - i
            rows = _chunk_rows(c, HG_CHUNK)
            row0 = pl.ds(c * HG_CHUNK, 1)
            for h in range(HG_HEADS):
                ln = _head_lanes(h)
                G = ds_ref[h]
                S = st_ref[h, c]
                d_o = o_ref[h, rows, :]
                dqe_ref[h, rows, :] = _dot(d_o, S)
                dkd_ref[h, rows, :] = _dot(i_ref[rows, ln], G)
                dis_ref[h, rows, :] = _dot(kd_ref[h, rows, :], G, NT)
                ddec_ref[h, row0, :] = jnp.sum(S * G, axis=0, keepdims=True)
                ds_ref[h] = G * dec_ref[h, row0, :] + _dot(d_o, qe_ref[h, rows, :], TN)
            return carry

        lax.fori_loop(0, nsub, step, 0)
        grads = prep_vjp(tuple((dqe_ref[h], o_ref[h], dkd_ref[h], ddec_ref[h]) for h in range(HG_HEADS)))
        for h in range(HG_HEADS):
            ln = _head_lanes(h)
            dq, df, di, dl0, dl1 = grads[N_HG_IN * h:N_HG_IN * (h + 1)]
            dq_ref[:, ln] = dq.astype(dq_ref.dtype)
            df_ref[:, ln] = df.astype(df_ref.dtype)
            di_ref[:, ln] = (di + dis_ref[h]).astype(di_ref.dtype)
            dl0_ref[:, ln] += dl0
            dl1_ref[:, ln] += dl1

    dcol = jax.ShapeDtypeStruct((T, HG_WIDTH), BF16)
    dvec = jax.ShapeDtypeStruct((1, HG_WIDTH), F32)
    blk = pltpu.VMEM((HG_HEADS, tb, LANES), F32)
    return pl.pallas_call(
        body, name=name, grid=(nT,),
        in_specs=[col(0), col(1), col(2), col(3), st, col(do_blk), vec, vec, vec],
        out_specs=[col(0)] * 4 + [vec] * 3,
        out_shape=[dcol] * 4 + [dvec] * 3,
        scratch_shapes=[pltpu.VMEM((HG_HEADS, HG_HEAD_DIM, HG_HEAD_DIM), F32)] + [blk] * 8,
    )(proj, proj, proj, proj, states, do, l0, l1, nw)


def _rw_consts(rows):
    same, tril, stril = _chunk_masks(rows, RW_CHUNK)
    br, bc = _iota2((LANES, LANES), 0), _iota2((LANES, LANES), 1)
    blockdiag = ((br < RW_HEAD_DIM) == (bc < RW_HEAD_DIM)).astype(F32)
    m0 = (_iota2((1, LANES), 1) < RW_HEAD_DIM).astype(F32)
    return same, tril, stril, blockdiag, m0, 1.0 - m0


def _unit_lower_inverses_impl(lows):
    rows = lows[0].shape[0]
    eye = (_iota2(lows[0].shape, 0) == _iota2(lows[0].shape, 1)).astype(F32)
    xs = [low + eye for low in lows]
    ps = [_dot(low, low) for low in lows]
    n = 4
    while n < RW_CHUNK:
        zs = [_dot(jnp.concatenate([p, x], axis=0), p) for p, x in zip(ps, xs)]
        ps = [z[:rows] for z in zs]
        xs = [x + z[rows:] for x, z in zip(xs, zs)]
        n *= 2
    return tuple(x + _dot(x, p) for x, p in zip(xs, ps))


@jax.custom_vjp
def _unit_lower_inverses(lows):
    return _unit_lower_inverses_impl(lows)


def _unit_lower_inverses_fwd(lows):
    xs = _unit_lower_inverses_impl(lows)
    return xs, xs


def _unit_lower_inverses_bwd(xs, dxs):
    ts = [_dot(x, dx, TN) for x, dx in zip(xs, dxs)]
    return (tuple(_dot(t, x, NT) for t, x in zip(ts, xs)),)


_unit_lower_inverses.defvjp(_unit_lower_inverses_fwd, _unit_lower_inverses_bwd)


N_PREP_IN = 12
N_PREP_OUT = 9
RW_GROUP = 2


def _rw_prep(consts, *flat):
    same, tril, stril, blockdiag, m0, m1 = consts
    rows = tril.shape[0]
    masks = (m0, m1)
    pairs = [flat[N_PREP_IN * i:N_PREP_IN * (i + 1)] for i in range(len(flat) // N_PREP_IN)]
    lora = [(_dot(jnp.tanh(lw), w2p), _dot(lw, a2p), _dot(_sigmoid(gd), g2), _dot(jnp.square(kx * k_k), blockdiag))
            for _, kx, _, lw, gd, _, _, k_k, _, w2p, a2p, g2 in pairs]
    mid = []
    for (r, kx, v, lw, gd, w0, a0, k_k, k_a, w2p, a2p, g2), (xw, xa, g, kk_sq) in zip(pairs, lora):
        xw = w0 + xw
        w = jnp.minimum(xw, 0.0) - jnp.log(1.0 + jnp.exp(-jnp.abs(xw))) - 0.5
        a_s = _sigmoid(a0 + xa)
        kk = kx * k_k / jnp.maximum(jnp.sqrt(kk_sq), L2_EPS)
        mid.append((-jnp.exp(w), a_s, kk, kx * (1.0 + (a_s - 1.0) * k_a), g))
    accs = [_dot(jnp.concatenate([tril, same], axis=0), ld, NN, HIGHEST) for ld, _, _, _, _ in mid]
    pre = []
    for (r, _, v, *_), (ld, a_s, kk, k2, g), acc in zip(pairs, mid, accs):
        bv = kk * a_s
        cum, tot = acc[:rows], acc[rows:]
        ecn = jnp.exp(-cum)
        a_t = -kk * jnp.exp(cum - ld)
        r_t = r * jnp.exp(cum)
        rem = jnp.exp(tot - cum)
        pre.append((v, a_t, r_t, (bv * ecn, k2 * ecn), (bv * rem, k2 * rem, jnp.exp(tot), k2, g)))
    zs = [_dot(jnp.concatenate([a_t * m0, a_t * m1, r_t * m0, r_t * m1], axis=0), jnp.concatenate(bk, axis=0), NT)
          for _, a_t, r_t, bk, _ in pre]
    pre = [(v, a_t, r_t, z, out) for (v, a_t, r_t, _, out), z in zip(pre, zs)]
    heads = [(i, h) for i in range(len(pre)) for h in range(2)]
    za = {ih: pre[ih[0]][3][ih[1] * rows:(ih[1] + 1) * rows] for ih in heads}
    zr = {ih: pre[ih[0]][3][(2 + ih[1]) * rows:(3 + ih[1]) * rows] for ih in heads}
    tinv = dict(zip(heads, _unit_lower_inverses(tuple(za[ih][:, :rows] * stril for ih in heads))))
    lv = {ih: _dot(jnp.concatenate([za[ih][:, rows:] * stril, zr[ih][:, rows:] * tril], axis=0), pre[ih[0]][0])
          for ih in heads}
    wu = {ih: _dot(tinv[ih], jnp.concatenate([pre[ih[0]][1] * masks[ih[1]], lv[ih][:rows]], axis=1)) for ih in heads}
    w_m = {ih: wu[ih][:, :LANES] for ih in heads}
    u_m = {ih: masks[ih[1]] * wu[ih][:, LANES:] for ih in heads}
    qy = {ih: _dot(zr[ih][:, :rows] * tril, jnp.concatenate([w_m[ih], u_m[ih]], axis=1)) for ih in heads}
    outs = []
    for i in range(len(pre)):
        a, b = (i, 0), (i, 1)
        W = w_m[a] + w_m[b]
        U = u_m[a] + u_m[b]
        Q = pre[i][2] + qy[a][:, :LANES] + qy[b][:, :LANES]
        Y0 = qy[a][:, LANES:] + qy[b][:, LANES:] + m0 * lv[a][rows:] + m1 * lv[b][rows:]
        outs.append((W, U, Q, Y0) + pre[i][4])
    return tuple(outs)


N_POST_IN = 8


def _rw_post(blockdiag, *flat):
    inv_n = 1.0 / RW_HEAD_DIM
    pairs = [flat[N_POST_IN * i:N_POST_IN * (i + 1)] for i in range(len(flat) // N_POST_IN)]
    sums = [(_dot(y, blockdiag), _dot(r * k2 * r_k, blockdiag)) for y, r, _, k2, _, r_k, _, _ in pairs]
    centred = [p[0] - s[0] * inv_n for p, s in zip(pairs, sums)]
    variances = [_dot(yc * yc, blockdiag) * inv_n for yc in centred]
    return tuple((yc * lax.rsqrt(var + RW_GN_EPS) * ln_w + ln_b + s[1] * v) * g
                 for (_, _, v, _, g, _, ln_w, ln_b), s, yc, var in zip(pairs, sums, centred, variances))


N_RW_VEC = 7
N_RW_MAT = 3


def _rw_specs(T, tb, rev):
    nT = T // tb
    tix = (lambda t: nT - 1 - t) if rev else (lambda t: t)
    wide = lambda blk: pl.BlockSpec((tb, RW_WIDTH), lambda t: (tix(t), blk))
    narrow = lambda blk: pl.BlockSpec((tb, LANES), lambda t: (tix(t), blk))
    vec = pl.BlockSpec((1, RW_WIDTH), lambda t: (0, 0))
    mat = pl.BlockSpec((RW_PAIRS, LANES, LANES), lambda t: (0, 0, 0))
    st = pl.BlockSpec((RW_PAIRS, tb // RW_CHUNK, LANES, LANES), lambda t: (0, tix(t), 0, 0))
    lora0 = 3 * RW_WIDTH // LANES
    ins = [wide(0), wide(1), wide(2), narrow(lora0), narrow(lora0 + 1)]
    return nT, wide, vec, mat, st, ins


def _rw_prep_args(p, r_ref, k_ref, v_ref, lw_ref, gd_ref, vrefs, mrefs):
    ln = _head_lanes(p)
    w0, a0, k_k, k_a = [x[:, ln] for x in vrefs[:4]]
    return (r_ref[:, ln], k_ref[:, ln], v_ref[:, ln], lw_ref[...], gd_ref[...], w0, a0, k_k, k_a,
            *[x[p] for x in mrefs])


def _stack_chunks(ref, top, bottom):
    C = RW_CHUNK
    for c in range(ref.shape[0]):
        ref[c, 0:C, :] = top[c * C:(c + 1) * C]
        ref[c, C:2 * C, :] = bottom[c * C:(c + 1) * C]


def _group_args(p0, r_ref, k_ref, v_ref, lw_ref, gd_ref, vrefs, mrefs):
    flat = []
    for p in range(p0, p0 + RW_GROUP):
        flat += list(_rw_prep_args(p, r_ref, k_ref, v_ref, lw_ref, gd_ref, vrefs, mrefs))
    return flat


def _rw_fwd(rws, vecs, mats, name):
    T = rws.shape[0]
    tb = _row_tile(T, SCAN_ROWS)
    nsub = tb // RW_CHUNK
    C = RW_CHUNK
    nT, wide, vec, mat, st, ins = _rw_specs(T, tb, False)

    def body(*refs):
        r_ref, k_ref, v_ref, lw_ref, gd_ref = refs[:5]
        vrefs = refs[5:5 + N_RW_VEC]
        mrefs = refs[5 + N_RW_VEC:5 + N_RW_VEC + N_RW_MAT]
        o_ref, st_ref, s_ref, wq_ref, uy_ref, bk_ref, misc_ref, y_ref = refs[-8:]

        @pl.when(pl.program_id(0) == 0)
        def _():
            s_ref[...] = jnp.zeros_like(s_ref)

        consts = _rw_consts(tb)
        blockdiag = consts[3]
        for p0 in range(0, RW_PAIRS, RW_GROUP):
            outs = _rw_prep(consts, *_group_args(p0, r_ref, k_ref, v_ref, lw_ref, gd_ref, vrefs, mrefs))
            for p, (W, U, Q, Y0, Bg, Kg, dec, k2, g) in zip(range(p0, p0 + RW_GROUP), outs):
                _stack_chunks(wq_ref.at[p], W, Q)
                _stack_chunks(uy_ref.at[p], U, Y0)
                _stack_chunks(bk_ref.at[p], Bg, Kg)
                misc_ref[0, p], misc_ref[1, p], misc_ref[2, p] = dec, k2, g

        def step(c, carry):
            rows = _chunk_rows(c, C)
            states = [s_ref[p] for p in range(RW_PAIRS)]
            for p, S in enumerate(states):
                st_ref[p, c] = S
            pys = [_dot(wq_ref[p, c], S, NT) + uy_ref[p, c] for p, S in enumerate(states)]
            pvs = [jnp.concatenate([py[:C], v_ref[rows, _head_lanes(p)]], axis=0) for p, py in enumerate(pys)]
            updates = [_dot(pv, bk_ref[p, c], TN) for p, pv in enumerate(pvs)]
            for p, S in enumerate(states):
                y_ref[p, rows, :] = pys[p][C:]
                s_ref[p] = (S * misc_ref[0, p, pl.ds(c * C, 1), :] + updates[p]) * blockdiag
            return carry

        lax.fori_loop(0, nsub, step, 0)
        flat = []
        for p in range(RW_PAIRS):
            ln = _head_lanes(p)
            flat += [y_ref[p], r_ref[:, ln], v_ref[:, ln], misc_ref[1, p], misc_ref[2, p]] + [x[:, ln] for x in vrefs[4:]]
        for p, out in enumerate(_rw_post(blockdiag, *flat)):
            o_ref[:, _head_lanes(p)] = out

    stacked = pltpu.VMEM((RW_PAIRS, nsub, 2 * C, LANES), F32)
    return pl.pallas_call(
        body, name=name, grid=(nT,),
        in_specs=ins + [vec] * N_RW_VEC + [mat] * N_RW_MAT,
        out_specs=[wide(0), st],
        out_shape=[jax.ShapeDtypeStruct((T, RW_WIDTH), F32),
                   jax.ShapeDtypeStruct((RW_PAIRS, T // RW_CHUNK, LANES, LANES), F32)],
        scratch_shapes=[pltpu.VMEM((RW_PAIRS, LANES, LANES), F32), stacked, stacked, stacked,
                        pltpu.VMEM((3, RW_PAIRS, tb, LANES), F32), pltpu.VMEM((RW_PAIRS, tb, LANES), F32)],
    )(rws, rws, rws, rws, rws, *vecs, *mats)


def _rw_bwd(rws, states, do, do_blk, vecs, mats, name):
    T = rws.shape[0]
    tb = _row_tile(T, SCAN_ROWS)
    nsub = tb // RW_CHUNK
    C = RW_CHUNK
    G = RW_GROUP
    nT, wide, vec, mat, st, ins = _rw_specs(T, tb, True)
    nin = 5 + 1 + 1 + N_RW_VEC + N_RW_MAT

    def body(*refs):
        r_ref, k_ref, v_ref, lw_ref, gd_ref = refs[:5]
        st_ref, do_ref = refs[5], refs[6]
        vrefs = refs[7:7 + N_RW_VEC]
        mrefs = refs[7 + N_RW_VEC:nin]
        dr_ref, dk_ref, dv_ref, dlo_ref = refs[nin:nin + 4]
        dvec = refs[nin + 4:nin + 4 + N_RW_VEC]
        dmat = refs[nin + 4 + N_RW_VEC:nin + 4 + N_RW_VEC + N_RW_MAT]
        ds_ref, wq_ref, uy_ref, bk_ref, pv_ref, dec_ref, y_ref, dpre_ref, dvs_ref = refs[-9:]

        @pl.when(pl.program_id(0) == 0)
        def _():
            ds_ref[...] = jnp.zeros_like(ds_ref)
            for x in dvec + dmat:
                x[...] = jnp.zeros_like(x)

        consts = _rw_consts(tb)
        blockdiag = consts[3]
        dlw, dgd = 0.0, 0.0
        for p0 in range(0, RW_PAIRS, G):
            outs, prep_vjp = jax.vjp(functools.partial(_rw_prep, consts),
                                     *_group_args(p0, r_ref, k_ref, v_ref, lw_ref, gd_ref, vrefs, mrefs))
            for q, (W, U, Q, Y0, Bg, Kg, dec, _, _) in enumerate(outs):
                _stack_chunks(wq_ref.at[q], W, Q)
                _stack_chunks(uy_ref.at[q], U, Y0)
                _stack_chunks(bk_ref.at[q], Bg, Kg)
                dec_ref[q] = dec

            def redo(c, carry, p0=p0):
                rows = _chunk_rows(c, C)
                for q in range(G):
                    py = _dot(wq_ref[q, c], st_ref[p0 + q, c], NT) + uy_ref[q, c]
                    y_ref[q, rows, :] = py[C:]
                    pv_ref[q, c, 0:C, :] = py[:C]
                    pv_ref[q, c, C:2 * C, :] = v_ref[rows, _head_lanes(p0 + q)]
                return carry

            lax.fori_loop(0, nsub, redo, 0)
            flat = []
            for q in range(G):
                ln = _head_lanes(p0 + q)
                flat += [y_ref[q], r_ref[:, ln], v_ref[:, ln], outs[q][7], outs[q][8]] + [x[:, ln] for x in vrefs[4:]]
            _, post_vjp = jax.vjp(functools.partial(_rw_post, blockdiag), *flat)
            post_grads = post_vjp(tuple(do_ref[:, _head_lanes(p0 + q)] for q in range(G)))
            post = []
            for q in range(G):
                ln = _head_lanes(p0 + q)
                dy, dr2, dv2, dk2, dg, dr_k, dln_w, dln_b = post_grads[N_POST_IN * q:N_POST_IN * (q + 1)]
                dpre_ref[q, 3] = dy
                dvs_ref[q] = dv2
                for x, gx in zip(dvec[4:], (dr_k, dln_w, dln_b)):
                    x[:, ln] += gx
                dpre_ref[q, 6] = jnp.zeros_like(dpre_ref[q, 6])
                post.append((dr2, dk2, dg))

            def step(i, carry, p0=p0):
                c = nsub - 1 - i
                rows = _chunk_rows(c, C)
                row0 = pl.ds(c * C, 1)
                qs = range(G)
                Gs = [ds_ref[p0 + q] * blockdiag for q in qs]
                Ss = [st_ref[p0 + q, c] for q in qs]
                t1 = [_dot(bk_ref[q, c], Gs[q], NT) for q in qs]
                t3 = [_dot(pv_ref[q, c], Gs[q]) for q in qs]
                dpy = [jnp.concatenate([t1[q][:C], dpre_ref[q, 3, rows, :]], axis=0) for q in qs]
                t2 = [_dot(dpy[q], Ss[q]) for q in qs]
                back = [_dot(dpy[q], wq_ref[q, c], TN) for q in qs]
                for q in qs:
                    dvs_ref[q, rows, :] += t1[q][C:]
                    dpre_ref[q, 0, rows, :] = t2[q][:C]
                    dpre_ref[q, 1, rows, :] = t1[q][:C]
                    dpre_ref[q, 2, rows, :] = t2[q][C:]
                    dpre_ref[q, 4, rows, :] = t3[q][:C]
                    dpre_ref[q, 5, rows, :] = t3[q][C:]
                    dpre_ref[q, 6, row0, :] = jnp.sum(Ss[q] * Gs[q], axis=0, keepdims=True)
                    ds_ref[p0 + q] = Gs[q] * dec_ref[q, row0, :] + back[q]
                return carry

            lax.fori_loop(0, nsub, step, 0)
            grads = prep_vjp(tuple(tuple(dpre_ref[q, i] for i in range(7)) + post[q][1:] for q in range(G)))
            for q in range(G):
                ln = _head_lanes(p0 + q)
                gq = grads[N_PREP_IN * q:N_PREP_IN * (q + 1)]
                dr_ref[:, ln] = gq[0] + post[q][0]
                dk_ref[:, ln] = gq[1]
                dv_ref[:, ln] = gq[2] + dvs_ref[q]
                dlw = dlw + gq[3]
                dgd = dgd + gq[4]
                for x, gx in zip(dvec[:4], gq[5:9]):
                    x[:, ln] += gx
                for x, gx in zip(dmat, gq[9:]):
                    x[p0 + q] += gx
        dlo_ref[:, 0:LANES] = dlw
        dlo_ref[:, LANES:2 * LANES] = dgd

    dcol = jax.ShapeDtypeStruct((T, RW_WIDTH), F32)
    dlo_spec = pl.BlockSpec((tb, 2 * LANES), lambda t: (nT - 1 - t, 0))
    blk = pltpu.VMEM((G, tb, LANES), F32)
    stacked = pltpu.VMEM((G, nsub, 2 * C, LANES), F32)
    return pl.pallas_call(
        body, name=name, grid=(nT,),
        in_specs=ins + [st, wide(do_blk)] + [vec] * N_RW_VEC + [mat] * N_RW_MAT,
        out_specs=[wide(0)] * 3 + [dlo_spec] + [vec] * N_RW_VEC + [mat] * N_RW_MAT,
        out_shape=[dcol] * 3 + [jax.ShapeDtypeStruct((T, 2 * LANES), F32)]
        + [jax.ShapeDtypeStruct((1, RW_WIDTH), F32)] * N_RW_VEC
        + [jax.ShapeDtypeStruct((RW_PAIRS, LANES, LANES), F32)] * N_RW_MAT,
        scratch_shapes=[pltpu.VMEM((RW_PAIRS, LANES, LANES), F32), stacked, stacked, stacked, stacked, blk, blk,
                        pltpu.VMEM((G, 7, tb, LANES), F32), blk],
    )(rws, rws, rws, rws, rws, states, do, *vecs, *mats)


def _my_index():
    return 4 * lax.axis_index("x") + 2 * lax.axis_index("y") + lax.axis_index("c")


def _peer(bits):
    pos = []
    for name, flip in zip(("x", "y", "c"), bits):
        i = lax.axis_index(name)
        pos.append(1 - i if flip else i)
    return tuple(pos)


def _peer_index(bits):
    x, y, c = _peer(bits)
    return 4 * x + 2 * y + c


def _all_gather(shards, name):
    n = len(shards)
    chips = [(1, 0, 0), (0, 1, 0), (1, 1, 0)]
    sib = (0, 0, 1)

    def body(*refs):
        ins, outs = refs[:n], refs[n:2 * n]
        send_sems, recv_sems, local_sems = refs[2 * n:]

        def rows(k, dev):
            r = ins[k].shape[0]
            return outs[k].at[pl.ds(dev * r, r), :]

        def copy(k, slot, block_dev, to_bits, src=None):
            return pltpu.make_async_remote_copy(
                src_ref=rows(k, block_dev) if src is None else src, dst_ref=rows(k, block_dev),
                send_sem=send_sems.at[k, slot], recv_sem=recv_sems.at[k, slot],
                device_id=_peer(to_bits), device_id_type=MESH_ID)

        me = _my_index()
        started = []
        for k in range(n):
            mine = pltpu.make_async_copy(ins[k], rows(k, me), local_sems.at[k])
            mine.start()
            started.append(mine)
        sends = []
        for k in range(n):
            first = [copy(k, 0, me, sib, src=ins[k])]
            first += [copy(k, 1 + j, me, chip, src=ins[k]) for j, chip in enumerate(chips)]
            for cp in first:
                cp.start()
            sends += first
        for k in range(n):
            for j, chip in enumerate(chips):
                copy(k, 1 + j, _peer_index(chip), chip).wait_recv()
                fwd = copy(k, 4 + j, _peer_index(chip), sib)
                fwd.start()
                sends.append(fwd)
        for k in range(n):
            copy(k, 0, _peer_index(sib), sib).wait_recv()
            for j, chip in enumerate(chips):
                both = (chip[0], chip[1], 1)
                copy(k, 4 + j, _peer_index(both), sib).wait_recv()
        for cp in sends:
            cp.wait_send()
        for cp in started:
            cp.wait()

    any_spec = pl.BlockSpec(memory_space=pl.ANY)
    return pl.pallas_call(
        body, name=name,
        in_specs=[any_spec] * n, out_specs=[any_spec] * n,
        out_shape=[jax.ShapeDtypeStruct((N_DEV * s.shape[0], s.shape[1]), s.dtype) for s in shards],
        scratch_shapes=[pltpu.SemaphoreType.DMA((n, 7)), pltpu.SemaphoreType.DMA((n, 7)),
                        pltpu.SemaphoreType.DMA((n,))],
    )(*shards)


def _exchange(partials, name):
    n = len(partials)
    flips = [(dx, dy, dc) for dx in (0, 1) for dy in (0, 1) for dc in (0, 1)][1:]

    def body(*refs):
        ins, outs = refs[:n], refs[n:2 * n]
        send_sems, recv_sems, local_sems = refs[2 * n:]
        me = _my_index()
        local = []
        for k in range(n):
            cp = pltpu.make_async_copy(ins[k].at[me], outs[k].at[me], local_sems.at[k])
            cp.start()
            local.append(cp)
        copies = []
        for k in range(n):
            for d, bits in enumerate(flips):
                cp = pltpu.make_async_remote_copy(
                    src_ref=ins[k].at[_peer_index(bits)], dst_ref=outs[k].at[me],
                    send_sem=send_sems.at[k, d], recv_sem=recv_sems.at[k, d],
                    device_id=_peer(bits), device_id_type=MESH_ID)
                cp.start()
                copies.append(cp)
        for cp in copies:
            cp.wait_recv()
        for cp in copies:
            cp.wait_send()
        for cp in local:
            cp.wait()

    any_spec = pl.BlockSpec(memory_space=pl.ANY)
    return pl.pallas_call(
        body, name=name,
        in_specs=[any_spec] * n, out_specs=[any_spec] * n,
        out_shape=[jax.ShapeDtypeStruct(p.shape, p.dtype) for p in partials],
        scratch_shapes=[pltpu.SemaphoreType.DMA((n, 7)), pltpu.SemaphoreType.DMA((n, 7)),
                        pltpu.SemaphoreType.DMA((n,))],
    )(*partials)


HBM_SPEC = pl.BlockSpec(memory_space=pltpu.HBM)
SEM_SPEC = pl.BlockSpec(memory_space=pltpu.SEMAPHORE)
ALL_FLIPS = [(dx, dy, dc) for dx in (0, 1) for dy in (0, 1) for dc in (0, 1)][1:]


def _spread_copies(srcs, lands, send_sems, recv_sems, to_x):
    me = _my_index()
    my_x = lax.axis_index("x")
    copies = []
    for k, land in enumerate(lands):
        for d, bits in enumerate(ALL_FLIPS):
            if not srcs:
                src = land.at[me]
            elif to_x is None:
                src = srcs[k].at[_peer_index(bits)]
            else:
                _, py, pc = _peer(bits)
                src = srcs[k].at[2 * py + pc]
            cp = pltpu.make_async_remote_copy(
                src_ref=src, dst_ref=land.at[me],
                send_sem=send_sems.at[k * 7 + d], recv_sem=recv_sems.at[k * 7 + d],
                device_id=_peer(bits), device_id_type=MESH_ID)
            sends = True if to_x is None else my_x == (to_x ^ bits[0])
            receives = True if to_x is None else my_x == to_x
            copies.append((cp, sends, receives))
    return copies


def _when(cond, fn):
    if cond is True:
        fn()
    else:
        pl.when(cond)(fn)


def _spread_start(srcs, lands, name, to_x=None):
    ns, n = len(srcs), len(lands)

    def body(*refs):
        src_refs, land_refs = refs[:ns], refs[ns:ns + n]
        send_sems, recv_sems = refs[ns + n], refs[ns + n + 1]
        token = refs[-1]
        for cp, sends, _ in _spread_copies(src_refs, land_refs, send_sems, recv_sems, to_x):
            _when(sends, cp.start)
        token[...] = jnp.zeros_like(token)

    bufs = list(srcs) + list(lands)
    out = pl.pallas_call(
        body, name=name,
        out_shape=(pltpu.SemaphoreType.DMA((7 * n,)), pltpu.SemaphoreType.DMA((7 * n,)),
                   *[pltpu.HBM(b.shape, b.dtype) for b in bufs], jax.ShapeDtypeStruct((8, LANES), F32)),
        in_specs=[HBM_SPEC] * (ns + n),
        out_specs=(SEM_SPEC, SEM_SPEC, *[HBM_SPEC] * (ns + n), pl.BlockSpec(memory_space=pltpu.VMEM)),
        input_output_aliases={i: 2 + i for i in range(ns + n)},
        compiler_params=pltpu.CompilerParams(has_side_effects=pltpu.SideEffectType.DATAFLOW_SIDE_EFFECTING),
    )(*[pltpu.with_memory_space_constraint(b, pltpu.HBM) for b in bufs])
    return out[0], out[1], list(out[2:2 + ns]), list(out[2 + ns:2 + ns + n]), out[-1]


def _spread_wait(send_sems, recv_sems, srcs, lands, after, name, to_x=None):
    ns, n = len(srcs), len(lands)

    def body(*refs):
        src_refs, land_refs = refs[:ns], refs[ns:ns + n]
        send_sems, recv_sems = refs[ns + n], refs[ns + n + 1]
        for cp, sends, receives in _spread_copies(src_refs, land_refs, send_sems, recv_sems, to_x):
            _when(sends, cp.wait_send)
            _when(receives, cp.wait_recv)

    bufs = list(srcs) + list(lands)
    out = pl.pallas_call(
        body, name=name,
        out_shape=tuple(pltpu.HBM(b.shape, b.dtype) for b in bufs),
        in_specs=[HBM_SPEC] * (ns + n) + [SEM_SPEC, SEM_SPEC, pl.BlockSpec(memory_space=pl.ANY)],
        out_specs=tuple([HBM_SPEC] * (ns + n)),
        input_output_aliases={i: i for i in range(ns + n)},
        compiler_params=pltpu.CompilerParams(has_side_effects=pltpu.SideEffectType.DATAFLOW_SIDE_EFFECTING),
    )(*bufs, send_sems, recv_sems, after)
    return list(out[ns:])


def _own_slot_only(block, me):
    return lax.dynamic_update_slice(lax.empty((N_DEV,) + block.shape, block.dtype), block[None], (me, 0, 0))


def _sum_slots(landed, name):
    _, R, C = landed.shape
    tb = _row_tile(R, 128)

    def body(l_ref, o_ref):
        acc = l_ref[0].astype(F32)
        for s in range(1, N_DEV):
            acc = acc + l_ref[s].astype(F32)
        o_ref[...] = acc

    return pl.pallas_call(
        body, name=name, grid=(R // tb,),
        in_specs=[pl.BlockSpec((N_DEV, tb, C), lambda i: (0, i, 0))],
        out_specs=pl.BlockSpec((tb, C), lambda i: (i, 0)),
        out_shape=jax.ShapeDtypeStruct((R, C), F32),
    )(landed)


def _pack_rows(flat_list, width=LANES):
    flat = jnp.concatenate([a.reshape(-1) for a in flat_list])
    n = flat.shape[0]
    rows = -(-n // width)
    rows = -(-rows // 8) * 8
    return jnp.pad(flat, (0, rows * width - n)).reshape(rows, width)


def _unpack(packed, shapes):
    flat = packed.reshape(-1)
    out, off = [], 0
    for s in shapes:
        n = 1
        for d in s:
            n *= d
        out.append(flat[off:off + n].reshape(s))
        off += n
    return out


def kernel(x, norm1_w, w_in, hg_lb_logits, hg_norm_w, rw_shift_mu, rw_w0, rw_w2, rw_a0, rw_a2, rw_g2, rw_k_k, rw_k_a, rw_r_k, rw_ln_w, rw_ln_b, w_out, norm2_w, w_up, conv_w, conv_b, w_down, final_norm_w, loss_target, m_norm1_w, m_w_in, m_hg_lb_logits, m_hg_norm_w, m_rw_shift_mu, m_rw_w0, m_rw_w2, m_rw_a0, m_rw_a2, m_rw_g2, m_rw_k_k, m_rw_k_a, m_rw_r_k, m_rw_ln_w, m_rw_ln_b, m_w_out, m_norm2_w, m_w_up, m_conv_w, m_conv_b, m_w_down, m_final_norm_w, v_norm1_w, v_w_in, v_hg_lb_logits, v_hg_norm_w, v_rw_shift_mu, v_rw_w0, v_rw_w2, v_rw_a0, v_rw_a2, v_rw_g2, v_rw_k_k, v_rw_k_a, v_rw_r_k, v_rw_ln_w, v_rw_ln_b, v_w_out, v_norm2_w, v_w_up, v_conv_w, v_conv_b, v_w_down, v_final_norm_w):
    weights = dict(norm1_w=norm1_w, w_in=w_in, hg_lb_logits=hg_lb_logits, hg_norm_w=hg_norm_w,
                   rw_shift_mu=rw_shift_mu, rw_w0=rw_w0, rw_w2=rw_w2, rw_a0=rw_a0, rw_a2=rw_a2, rw_g2=rw_g2,
                   rw_k_k=rw_k_k, rw_k_a=rw_k_a, rw_r_k=rw_r_k, rw_ln_w=rw_ln_w, rw_ln_b=rw_ln_b, w_out=w_out,
                   norm2_w=norm2_w, w_up=w_up, conv_w=conv_w, conv_b=conv_b, w_down=w_down,
                   final_norm_w=final_norm_w)
    m_in = dict(norm1_w=m_norm1_w, w_in=m_w_in, hg_lb_logits=m_hg_lb_logits, hg_norm_w=m_hg_norm_w,
                rw_shift_mu=m_rw_shift_mu, rw_w0=m_rw_w0, rw_w2=m_rw_w2, rw_a0=m_rw_a0, rw_a2=m_rw_a2,
                rw_g2=m_rw_g2, rw_k_k=m_rw_k_k, rw_k_a=m_rw_k_a, rw_r_k=m_rw_r_k, rw_ln_w=m_rw_ln_w,
                rw_ln_b=m_rw_ln_b, w_out=m_w_out, norm2_w=m_norm2_w, w_up=m_w_up, conv_w=m_conv_w,
                conv_b=m_conv_b, w_down=m_w_down, final_norm_w=m_final_norm_w)
    v_in = dict(norm1_w=v_norm1_w, w_in=v_w_in, hg_lb_logits=v_hg_lb_logits, hg_norm_w=v_hg_norm_w,
                rw_shift_mu=v_rw_shift_mu, rw_w0=v_rw_w0, rw_w2=v_rw_w2, rw_a0=v_rw_a0, rw_a2=v_rw_a2,
                rw_g2=v_rw_g2, rw_k_k=v_rw_k_k, rw_k_a=v_rw_k_a, rw_r_k=v_rw_r_k, rw_ln_w=v_rw_ln_w,
                rw_ln_b=v_rw_ln_b, w_out=v_w_out, norm2_w=v_norm2_w, w_up=v_w_up, conv_w=v_conv_w,
                conv_b=v_conv_b, w_down=v_w_down, final_norm_w=v_final_norm_w)
    names = list(weights)
    sharded_small = ["rw_w2", "rw_a2", "rw_g2", "conv_w"]
    sharded_big = ["w_in", "w_out", "w_up", "w_down"]
    replicated = [n for n in names if n not in sharded_small + sharded_big]

    xs = x[0]
    tgt = loss_target[0]

    small_shard = _pack_rows([weights[n] for n in sharded_small])
    g_win_t, g_small = _all_gather([w_in[0].T.astype(BF16), small_shard], "gather_weights")
    me = _my_index()
    later = (w_up[0].T.astype(BF16), w_out[0].astype(BF16), w_down[0].astype(BF16))
    later, _ = lax.optimization_barrier((later, g_small))
    later = [_own_slot_only(z, me) for z in later]
    g_send, g_recv, _, later, g_token = _spread_start([], later, "gather_later_start")
    small_shapes = [weights[n].shape for n in sharded_small]
    per_dev = [_unpack(g_small.reshape(N_DEV, -1)[j], small_shapes) for j in range(N_DEV)]
    w2_full, a2_full, g2_full, convw_full = [jnp.concatenate([per_dev[j][i][0] for j in range(N_DEV)], axis=-1)
                                             for i in range(4)]
    zeros64 = jnp.zeros((RW_PAIRS, 64, LANES), F32)
    by_pair = lambda z: z.reshape(z.shape[0], RW_PAIRS, LANES).transpose(1, 0, 2)
    w2p = jnp.concatenate([by_pair(w2_full), zeros64], axis=1)
    a2p = jnp.concatenate([zeros64, by_pair(a2_full)], axis=1)
    g2p = by_pair(g2_full)

    l0, l1 = hg_lb_logits[0:1], hg_lb_logits[1:2]
    h1 = _rms_fwd(xs, norm1_w + g_token[0:1, 0:1], "norm1")
    proj = _mm_nt(h1, g_win_t, "proj_in")
    o_hg, hg_states = _hg_fwd(proj, l0, l1, hg_norm_w, "hgrn2_fwd")
    rws = _shift_fwd(proj, rw_shift_mu, "token_shift")
    rw_vecs = [rw_w0, rw_a0, rw_k_k, rw_k_a, rw_r_k, rw_ln_w, rw_ln_b]
    rw_mats = [w2p, a2p, g2p]
    o_rw, rw_states = _rw_fwd(rws, rw_vecs, rw_mats, "rwkv7_fwd")
    o_mix = jnp.concatenate([o_hg, o_rw], axis=-1).astype(BF16)
    g_wup_t, g_wout, g_wdown = [z.reshape(-1, z.shape[-1])
                                for z in _spread_wait(g_send, g_recv, [], later, o_mix, "gather_later_wait")]
    x1 = _mm_nn(o_mix, g_wout, xs, "proj_out")
    h2 = _rms_fwd(x1, norm2_w, "norm2")
    u = _mm_nt(h2, g_wup_t, "ffn_up")
    act = _ffn_act_fwd(u, convw_full, conv_b, "ffn_act")
    x2 = _mm_nn(act, g_wdown, x1, "ffn_down")
    loss_part, dx2, d_final_w = _loss_head(x2, final_norm_w.reshape(1, -1), tgt, "loss_head")

    d_wdown = _mm_tn(act, dx2, 1408, "ffn_down_dw", BF16)
    dact = _mm_nt(dx2, g_wdown, "ffn_down_dx", BF16)
    du_g, du_v, dcw_g, dcw_v, dcb_g, dcb_v = _ffn_act_bwd(u, dact, convw_full, conv_b, "ffn_act_bwd")
    d_convw = jnp.concatenate([dcw_g, dcw_v], axis=-1)
    d_convb = jnp.concatenate([dcb_g, dcb_v], axis=-1)
    d_wup_t = jnp.concatenate([_mm_tn(du_g, h2, 1408, "ffn_up_dw_gate", BF16),
                               _mm_tn(du_v, h2, 1408, "ffn_up_dw_value", BF16)], axis=0)
    dh2 = _mm_nn([du_g, du_v], g_wup_t, None, "ffn_up_dx")
    dx1, d_norm2 = _rms_bwd(dh2, x1, norm2_w, dx2, "norm2_bwd")
    d_wout = _mm_tn(o_mix, dx1, 512, "proj_out_dw", BF16)
    do = _mm_nt(dx1, g_wout, "proj_out_dx")
    early = [z.reshape(N_DEV, z.shape[0] // N_DEV, z.shape[1]) for z in (d_wup_t, d_wout, d_wdown)]
    early_land = [_own_slot_only(lax.dynamic_index_in_dim(z, me, 0, keepdims=False), me) for z in early]
    e_send, e_recv, early, early_land, e_token = _spread_start(early, early_land, "exchange_early_start")
    hg_norm_w_t = hg_norm_w + e_token[0:1, 0:1]
    dq, df, di, dg, d_l0, d_l1, d_hg_nw = _hg_bwd(proj, hg_states, do, 0, l0, l1, hg_norm_w_t, "hgrn2_bwd")
    half = N_DEV // 2
    own_half_block = lambda z: _own_slot_only(lax.dynamic_index_in_dim(z, me % half, 0, keepdims=False), me)
    n_lo = half * w_in.shape[2]
    d_win_lo = _mm_tn(jnp.concatenate([dq, df, di, dg[:, :n_lo - 3 * HG_WIDTH]], axis=-1), h1, 640,
                      "proj_in_dw_low", BF16).reshape(half, -1, D_MODEL)
    m_send, m_recv, mid, mid_land, m_token = _spread_start([d_win_lo], [own_half_block(d_win_lo)],
                                                            "exchange_mid_start", to_x=0)
    rw_vecs_t = [rw_vecs[0] + m_token[0:1, 0:1]] + rw_vecs[1:]
    rw_out = _rw_bwd(rws, rw_states, do, 1, rw_vecs_t, rw_mats, "rwkv7_bwd")
    d_rw_vecs = rw_out[4:4 + N_RW_VEC]
    d_w2p, d_a2p, d_g2p = rw_out[4 + N_RW_VEC:]
    dp_parts, dmu_parts = [], []
    for i, z in enumerate(rw_out[:4]):
        dp, dmu = _shift_bwd(z, proj, rw_shift_mu, i * RW_WIDTH, "token_shift_bwd_%d" % i)
        dp_parts.append(dp)
        dmu_parts.append(dmu)
    d_mu = jnp.concatenate(dmu_parts, axis=-1)
    d_win_hi = _mm_tn(jnp.concatenate([dg[:, n_lo - 3 * HG_WIDTH:]] + dp_parts, axis=-1), h1, 640,
                      "proj_in_dw_high", BF16).reshape(half, -1, D_MODEL)
    from_pairs = lambda z: z.transpose(1, 0, 2).reshape(z.shape[1], RW_WIDTH)
    d_w2 = from_pairs(d_w2p[:, :64])
    d_a2 = from_pairs(d_a2p[:, 64:])
    d_g2 = from_pairs(d_g2p)
    col_blocks = lambda z: z.reshape(z.shape[0], N_DEV, -1).transpose(1, 0, 2)
    small_part = jnp.stack([
        _pack_rows([col_blocks(d_w2)[j], col_blocks(d_a2)[j], col_blocks(d_g2)[j], col_blocks(d_convw)[j]])
        for j in range(N_DEV)])
    l_send, l_recv, late, late_land, l_token = _spread_start([d_win_hi], [own_half_block(d_win_hi)],
                                                             "exchange_late_start", to_x=1)
    dh1 = _mm_nn([dq, df, di, dg] + dp_parts, g_win_t, None, "proj_in_dx")
    grad_x, d_norm1 = _rms_bwd(dh1, xs, norm1_w + l_token[0:1, 0:1], dx1, "norm1_bwd")

    rep_grads = dict(norm1_w=d_norm1, hg_lb_logits=jnp.concatenate([d_l0, d_l1], axis=0), hg_norm_w=d_hg_nw,
                     rw_shift_mu=d_mu, rw_w0=d_rw_vecs[0], rw_a0=d_rw_vecs[1], rw_k_k=d_rw_vecs[2],
                     rw_k_a=d_rw_vecs[3], rw_r_k=d_rw_vecs[4], rw_ln_w=d_rw_vecs[5], rw_ln_b=d_rw_vecs[6],
                     norm2_w=d_norm2, conv_b=d_convb, final_norm_w=d_final_w)
    rep_pack = _pack_rows([loss_part] + [rep_grads[n] for n in replicated])
    rep_part = jnp.broadcast_to(rep_pack[None], (N_DEV,) + rep_pack.shape)
    grads, delta, new_m, new_v = {}, {}, {}, {}

    def adamw_big(n, g):
        shp = weights[n].shape
        as2d = lambda z: z.reshape(shp[1], shp[2])
        grads[n] = g[None]
        d, nm, nv = _adamw(as2d(weights[n]), g, as2d(m_in[n]), as2d(v_in[n]), "adamw_" + n)
        delta[n], new_m[n], new_v[n] = d.reshape(shp), nm.reshape(shp), nv.reshape(shp)

    landed_early = _spread_wait(e_send, e_recv, early, early_land, grad_x, "exchange_early_wait")
    adamw_big("w_up", _sum_slots(landed_early[0], "sum_grads_w_up").T)
    adamw_big("w_out", _sum_slots(landed_early[1], "sum_grads_w_out"))
    adamw_big("w_down", _sum_slots(landed_early[2], "sum_grads_w_down"))
    (landed_mid,) = _spread_wait(m_send, m_recv, mid, mid_land, grad_x, "exchange_mid_wait", to_x=0)
    (landed_late,) = _spread_wait(l_send, l_recv, late, late_land, delta["w_down"], "exchange_late_wait", to_x=1)
    g_win = jnp.where(lax.axis_index("x") == 0, _sum_slots(landed_mid, "sum_grads_w_in_low"),
                      _sum_slots(landed_late, "sum_grads_w_in_high"))
    adamw_big("w_in", g_win.T)
    landed_rep, landed_small = _exchange([rep_part, small_part], "exchange_grads")
    g_small_sum = _unpack(_sum_slots(landed_small, "sum_grads_small"), small_shapes)
    rep_sum = _unpack(_sum_slots(landed_rep, "sum_grads_replicated"), [(1, 1)] + [weights[n].shape for n in replicated])
    loss = rep_sum[0].reshape(())
    grads.update(dict(zip(replicated, rep_sum[1:])))
    grads.update(dict(zip(sharded_small, g_small_sum)))

    small_names = replicated + sharded_small
    packs = [_pack_rows([src[n] for n in small_names]) for src in (weights, grads, m_in, v_in)]
    outs = _adamw(*packs, "adamw_small")
    small_shapes_all = [weights[n].shape for n in small_names]
    for dst, packed in zip((delta, new_m, new_v), outs):
        dst.update(dict(zip(small_names, _unpack(packed, small_shapes_all))))

    return (loss, grad_x[None], *[grads[n] for n in names], *[delta[n] for n in names],
            *[new_m[n] for n in names], *[new_v[n] for n in names])
```

```python
import functools

import jax
import jax.numpy as jnp
from jax import lax
from jax.experimental import pallas as pl
from jax.experimental.pallas import tpu as pltpu

F32 = jnp.float32
BF16 = jnp.bfloat16
HIGHEST = lax.Precision.HIGHEST
SCAN_PRECISION = None
MESH_ID = pl.DeviceIdType.MESH

N_DEV = 8
D_MODEL = 1024
HG_WIDTH = 512
HG_HEAD_DIM = 128
HG_HEADS = 4
RW_WIDTH = 512
RW_PAIRS = 4
RW_HEAD_DIM = 64
HG_COLS = 2048
RW_COLS = 1792
D_FF = 2816
NORM_EPS = 1e-6
RW_GN_EPS = 64e-5
L2_EPS = 1e-12
ADAM_LR, ADAM_B1, ADAM_B2, ADAM_EPS, ADAM_WD, ADAM_STEP = 0.001, 0.9, 0.999, 1e-08, 0.01, 10

HG_CHUNK = 32
HG_HALF = 16
RW_CHUNK = 64
SCAN_ROWS = 256
LANES = 128

NN = ((1,), (0,))
NT = ((1,), (1,))
TN = ((0,), (0,))


def _dot(a, b, dims=NN, precision=SCAN_PRECISION):
    if precision is None:
        a, b = a.astype(BF16), b.astype(BF16)
    return lax.dot_general(a, b, (dims, ((), ())), precision=precision, preferred_element_type=F32)


def _iota2(shape, dim):
    return lax.broadcasted_iota(jnp.int32, shape, dim)


def _sigmoid(z):
    return 0.5 * jnp.tanh(0.5 * z) + 0.5


def _row_tile(n, want):
    t = min(n, want)
    while n % t:
        t //= 2
    return t


def _rms_fwd(x, w, name):
    T, D = x.shape
    tb = _row_tile(T, 512)

    def body(x_ref, w_ref, h_ref):
        xv = x_ref[...]
        r = lax.rsqrt(jnp.mean(xv * xv, axis=-1, keepdims=True) + NORM_EPS)
        h_ref[...] = (xv * r * w_ref[...]).astype(h_ref.dtype)

    return pl.pallas_call(
        body, name=name, grid=(T // tb,),
        in_specs=[pl.BlockSpec((tb, D), lambda i: (i, 0)), pl.BlockSpec((1, D), lambda i: (0, 0))],
        out_specs=pl.BlockSpec((tb, D), lambda i: (i, 0)),
        out_shape=jax.ShapeDtypeStruct((T, D), BF16),
    )(x, w)


def _rms_bwd(dh, x, w, dres, name):
    T, D = x.shape
    tb = _row_tile(T, 256)

    def body(dh_ref, x_ref, w_ref, dres_ref, dx_ref, dw_ref):
        @pl.when(pl.program_id(0) == 0)
        def _():
            dw_ref[...] = jnp.zeros_like(dw_ref)

        xv = x_ref[...]
        r = lax.rsqrt(jnp.mean(xv * xv, axis=-1, keepdims=True) + NORM_EPS)
        xn = xv * r
        dy = dh_ref[...].astype(F32)
        dxn = dy * w_ref[...]
        dx_ref[...] = dres_ref[...] + r * (dxn - xn * jnp.mean(dxn * xn, axis=-1, keepdims=True))
        dw_ref[...] += jnp.sum(dy * xn, axis=0, keepdims=True)

    row = pl.BlockSpec((tb, D), lambda i: (i, 0))
    vec = pl.BlockSpec((1, D), lambda i: (0, 0))
    return pl.pallas_call(
        body, name=name, grid=(T // tb,),
        in_specs=[row, row, vec, row], out_specs=[row, vec],
        out_shape=[jax.ShapeDtypeStruct((T, D), F32), jax.ShapeDtypeStruct((1, D), F32)],
    )(dh, x, w, dres)


def _mm_nt(a, bt, name, out_dtype=F32):
    T, K = a.shape
    N = bt.shape[0]
    tm = _row_tile(T, 256)

    def body(a_ref, b_ref, o_ref):
        o_ref[...] = _dot(a_ref[...].astype(BF16), b_ref[...].astype(BF16), NT, None).astype(o_ref.dtype)

    return pl.pallas_call(
        body, name=name, grid=(T // tm,),
        in_specs=[pl.BlockSpec((tm, K), lambda i: (i, 0)), pl.BlockSpec((N, K), lambda i: (0, 0))],
        out_specs=pl.BlockSpec((tm, N), lambda i: (i, 0)),
        out_shape=jax.ShapeDtypeStruct((T, N), out_dtype),
    )(a, bt)


def _mm_nn(a, b, res, name, out_dtype=F32):
    parts = list(a) if isinstance(a, (list, tuple)) else [a]
    T = parts[0].shape[0]
    K, N = b.shape
    tm = _row_tile(T, 256)
    widths = [p.shape[1] for p in parts]
    n = len(parts)

    def body(*refs):
        b_ref, o_ref = refs[n], refs[-1]
        acc, off = None, 0
        for a_ref, w in zip(refs[:n], widths):
            d = _dot(a_ref[...].astype(BF16), b_ref[off:off + w, :].astype(BF16), NN, None)
            acc = d if acc is None else acc + d
            off += w
        if res is not None:
            acc = acc + refs[n + 1][...]
        o_ref[...] = acc.astype(o_ref.dtype)

    in_specs = [pl.BlockSpec((tm, w), lambda i: (i, 0)) for w in widths] + [pl.BlockSpec((K, N), lambda i: (0, 0))]
    args = parts + [b]
    if res is not None:
        in_specs.append(pl.BlockSpec((tm, N), lambda i: (i, 0)))
        args.append(res)
    return pl.pallas_call(
        body, name=name, grid=(T // tm,), in_specs=in_specs,
        out_specs=pl.BlockSpec((tm, N), lambda i: (i, 0)),
        out_shape=jax.ShapeDtypeStruct((T, N), out_dtype),
    )(*args)


def _mm_tn(a, b, tmm, name, out_dtype=F32):
    T, M = a.shape
    N = b.shape[1]
    tk = _row_tile(T, 512)
    nk = T // tk

    def body(a_ref, b_ref, o_ref, acc_ref):
        @pl.when(pl.program_id(1) == 0)
        def _():
            acc_ref[...] = jnp.zeros_like(acc_ref)

        acc_ref[...] += _dot(a_ref[...].astype(BF16), b_ref[...].astype(BF16), TN, None)

        @pl.when(pl.program_id(1) == nk - 1)
        def _():
            o_ref[...] = acc_ref[...].astype(o_ref.dtype)

    return pl.pallas_call(
        body, name=name, grid=(M // tmm, nk),
        in_specs=[pl.BlockSpec((tk, tmm), lambda m, k: (k, m)), pl.BlockSpec((tk, N), lambda m, k: (k, 0))],
        out_specs=pl.BlockSpec((tmm, N), lambda m, k: (m, 0)),
        out_shape=jax.ShapeDtypeStruct((M, N), out_dtype),
        scratch_shapes=[pltpu.VMEM((tmm, N), F32)],
    )(a, b)


class _RowShifts:
    def __init__(self, shape):
        index = _iota2(shape, 0)
        self.rows = shape[0]
        self.first = {n: index < n for n in (1, 2)}
        self.last = {n: index >= shape[0] - n for n in (1, 2)}

    def down(self, z, n):
        return jnp.where(self.first[n], 0.0, pltpu.roll(z, n, 0))

    def up(self, z, n):
        return jnp.where(self.last[n], 0.0, pltpu.roll(z, self.rows - n, 0))


def _shift_fwd(proj, mu, name):
    T = proj.shape[0]
    nblk = RW_COLS // LANES
    first = HG_COLS // LANES

    def body(p_ref, mu_ref, o_ref):
        p = p_ref[...]
        o_ref[...] = p + (_RowShifts(p.shape).down(p, 1) - p) * mu_ref[...]

    return pl.pallas_call(
        body, name=name, grid=(nblk,),
        in_specs=[pl.BlockSpec((T, LANES), lambda j: (0, first + j)), pl.BlockSpec((1, LANES), lambda j: (0, j))],
        out_specs=pl.BlockSpec((T, LANES), lambda j: (0, j)),
        out_shape=jax.ShapeDtypeStruct((T, RW_COLS), F32),
    )(proj, mu)


def _shift_bwd(ds, proj, mu, col0, name):
    T, width = ds.shape
    nblk = width // LANES
    first = (HG_COLS + col0) // LANES
    mu0 = col0 // LANES

    def body(ds_ref, p_ref, mu_ref, dp_ref, dmu_ref):
        dsv = ds_ref[...]
        p = p_ref[...]
        m = mu_ref[...]
        shifts = _RowShifts(p.shape)
        dp_ref[...] = (dsv * (1.0 - m) + shifts.up(dsv * m, 1)).astype(dp_ref.dtype)
        dmu_ref[...] = jnp.sum(dsv * (shifts.down(p, 1) - p), axis=0, keepdims=True)

    return pl.pallas_call(
        body, name=name, grid=(nblk,),
        in_specs=[pl.BlockSpec((T, LANES), lambda j: (0, j)),
                  pl.BlockSpec((T, LANES), lambda j: (0, first + j)),
                  pl.BlockSpec((1, LANES), lambda j: (0, mu0 + j))],
        out_specs=[pl.BlockSpec((T, LANES), lambda j: (0, j)), pl.BlockSpec((1, LANES), lambda j: (0, j))],
        out_shape=[jax.ShapeDtypeStruct((T, width), BF16), jax.ShapeDtypeStruct((1, width), F32)],
    )(ds, proj, mu)


def _conv3(z, w_ref, shifts):
    return w_ref[0:1, :] * shifts.down(z, 2) + w_ref[1:2, :] * shifts.down(z, 1) + w_ref[2:3, :] * z


def _ffn_act_fwd(u, conv_w, conv_b, name):
    T = u.shape[0]
    nblk = D_FF // LANES

    def body(ug_ref, uv_ref, wg_ref, wv_ref, bg_ref, bv_ref, act_ref):
        shifts = _RowShifts((T, LANES))
        gate = _conv3(ug_ref[...], wg_ref, shifts) + bg_ref[...]
        val = _conv3(uv_ref[...], wv_ref, shifts) + bv_ref[...]
        act_ref[...] = (gate * _sigmoid(gate) * val).astype(act_ref.dtype)

    col = lambda off: pl.BlockSpec((T, LANES), lambda j: (0, off + j))
    wsp = lambda off: pl.BlockSpec((3, LANES), lambda j: (0, off + j))
    bsp = lambda off: pl.BlockSpec((1, LANES), lambda j: (0, off + j))
    return pl.pallas_call(
        body, name=name, grid=(nblk,),
        in_specs=[col(0), col(nblk), wsp(0), wsp(nblk), bsp(0), bsp(nblk)],
        out_specs=pl.BlockSpec((T, LANES), lambda j: (0, j)),
        out_shape=jax.ShapeDtypeStruct((T, D_FF), BF16),
    )(u, u, conv_w, conv_w, conv_b, conv_b)


def _ffn_act_bwd(u, dact, conv_w, conv_b, name):
    T = u.shape[0]
    nblk = D_FF // LANES

    def conv_bwd(z, dzc, w_ref, du_ref, dw_ref, db_ref, shifts):
        up1, up2 = shifts.up(dzc, 1), shifts.up(dzc, 2)
        du = w_ref[2:3, :] * dzc + w_ref[1:2, :] * up1 + w_ref[0:1, :] * up2
        du_ref[...] = du.astype(du_ref.dtype)
        dw_ref[0:1, :] = jnp.sum(up2 * z, axis=0, keepdims=True)
        dw_ref[1:2, :] = jnp.sum(up1 * z, axis=0, keepdims=True)
        dw_ref[2:3, :] = jnp.sum(dzc * z, axis=0, keepdims=True)
        db_ref[...] = jnp.sum(dzc, axis=0, keepdims=True)

    def body(ug_ref, uv_ref, da_ref, wg_ref, wv_ref, bg_ref, bv_ref,
             dug_ref, duv_ref, dwg_ref, dwv_ref, dbg_ref, dbv_ref):
        ug, uv = ug_ref[...], uv_ref[...]
        shifts = _RowShifts((T, LANES))
        gate = _conv3(ug, wg_ref, shifts) + bg_ref[...]
        val = _conv3(uv, wv_ref, shifts) + bv_ref[...]
        da = da_ref[...].astype(F32)
        sg = _sigmoid(gate)
        dgate = da * val * (sg * (1.0 + gate * (1.0 - sg)))
        dval = da * gate * sg
        conv_bwd(ug, dgate, wg_ref, dug_ref, dwg_ref, dbg_ref, shifts)
        conv_bwd(uv, dval, wv_ref, duv_ref, dwv_ref, dbv_ref, shifts)

    col = lambda off: pl.BlockSpec((T, LANES), lambda j: (0, off + j))
    wsp = lambda off: pl.BlockSpec((3, LANES), lambda j: (0, off + j))
    bsp = lambda off: pl.BlockSpec((1, LANES), lambda j: (0, off + j))
    half = lambda r, dt: jax.ShapeDtypeStruct((r, D_FF), dt)
    return pl.pallas_call(
        body, name=name, grid=(nblk,),
        in_specs=[col(0), col(nblk), col(0), wsp(0), wsp(nblk), bsp(0), bsp(nblk)],
        out_specs=[col(0), col(0), wsp(0), wsp(0), bsp(0), bsp(0)],
        out_shape=[half(T, BF16), half(T, BF16), half(3, F32), half(3, F32), half(1, F32), half(1, F32)],
    )(u, u, dact, conv_w, conv_w, conv_b, conv_b)


def _loss_head(x2, w, target, name):
    T, D = x2.shape
    tb = _row_tile(T, 256)

    def body(x_ref, w_ref, t_ref, loss_ref, dx_ref, dw_ref):
        @pl.when(pl.program_id(0) == 0)
        def _():
            loss_ref[...] = jnp.zeros_like(loss_ref)
            dw_ref[...] = jnp.zeros_like(dw_ref)

        xv = x_ref[...]
        r = lax.rsqrt(jnp.mean(xv * xv, axis=-1, keepdims=True) + NORM_EPS)
        xn = xv * r
        err = xn * w_ref[...] - t_ref[...]
        row_loss = jnp.sum(err * err, axis=-1, keepdims=True) * (0.5 / D)
        loss_ref[...] += jnp.sum(row_loss, axis=0, keepdims=True)
        dy = err * (1.0 / D)
        dxn = dy * w_ref[...]
        dx_ref[...] = r * (dxn - xn * jnp.mean(dxn * xn, axis=-1, keepdims=True))
        dw_ref[...] += jnp.sum(dy * xn, axis=0, keepdims=True)

    row = pl.BlockSpec((tb, D), lambda i: (i, 0))
    vec = pl.BlockSpec((1, D), lambda i: (0, 0))
    return pl.pallas_call(
        body, name=name, grid=(T // tb,),
        in_specs=[row, vec, row],
        out_specs=[pl.BlockSpec((1, 1), lambda i: (0, 0)), row, vec],
        out_shape=[jax.ShapeDtypeStruct((1, 1), F32), jax.ShapeDtypeStruct((T, D), F32),
                   jax.ShapeDtypeStruct((1, D), F32)],
    )(x2, w, target)


def _adamw(w, g, m, v, name):
    R, C = w.shape
    tb = _row_tile(R, 256) if R % 8 == 0 else R

    def body(w_ref, g_ref, m_ref, v_ref, d_ref, nm_ref, nv_ref):
        gv = g_ref[...]
        nm = ADAM_B1 * m_ref[...] + (1.0 - ADAM_B1) * gv
        nv = ADAM_B2 * v_ref[...] + (1.0 - ADAM_B2) * (gv * gv)
        m_hat = nm / (1.0 - ADAM_B1 ** ADAM_STEP)
        v_hat = nv / (1.0 - ADAM_B2 ** ADAM_STEP)
        d_ref[...] = -ADAM_LR * (m_hat / (jnp.sqrt(v_hat) + ADAM_EPS) + ADAM_WD * w_ref[...])
        nm_ref[...] = nm
        nv_ref[...] = nv

    blk = pl.BlockSpec((tb, C), lambda i: (i, 0))
    sd = jax.ShapeDtypeStruct((R, C), F32)
    return pl.pallas_call(
        body, name=name, grid=(R // tb,), in_specs=[blk] * 4, out_specs=[blk] * 3, out_shape=[sd] * 3,
    )(w, g, m, v)


def _chunk_masks(rows, chunk):
    shift = chunk.bit_length() - 1
    i, j = _iota2((rows, rows), 0), _iota2((rows, rows), 1)
    same = jnp.right_shift(i, shift) == jnp.right_shift(j, shift)
    return same.astype(F32), (same & (j <= i)).astype(F32), (same & (j < i)).astype(F32)


def _head_lanes(h):
    return slice(h * LANES, (h + 1) * LANES)


def _chunk_rows(c, chunk):
    return pl.ds(pl.multiple_of(c * chunk, chunk), chunk)


def _hg_consts(rows):
    same, tril, _ = _chunk_masks(rows, HG_CHUNK)
    half_same, half_tril, _ = _chunk_masks(rows, HG_HALF)
    i, j = _iota2((rows, rows), 0), _iota2((rows, rows), 1)
    half_shift, shift = HG_HALF.bit_length() - 1, HG_CHUNK.bit_length() - 1
    mid_row = jnp.left_shift(jnp.right_shift(i, half_shift), half_shift) + (HG_HALF // 2 - 1)
    bound_row = jnp.left_shift(jnp.right_shift(i, shift), shift) + (HG_HALF - 1)
    upto_mid = ((same > 0) & (j <= mid_row)).astype(F32)
    upto_bound = ((same > 0) & (j <= bound_row)).astype(F32)
    lower_left = tril * (1.0 - half_same)
    return jnp.concatenate([tril, same, upto_mid, upto_bound], axis=0), half_tril, lower_left


N_HG_IN = 5


def _hg_prep(consts, *flat):
    sums, half_tril, lower_left = consts
    rows = half_tril.shape[0]
    heads, logs = [], []
    for h in range(len(flat) // N_HG_IN):
        qr, fr, ir, l0, l1 = flat[N_HG_IN * h:N_HG_IN * (h + 1)]
        lb = _sigmoid(l0 - l1)
        f = lb + (1.0 - lb) * _sigmoid(fr)
        heads.append((qr * _sigmoid(qr) * (HG_HEAD_DIM ** -0.5), 1.0 - f, ir))
        logs.append(jnp.log(f))
    acc = _dot(sums, jnp.concatenate(logs, axis=1), NN, HIGHEST)
    sums_of = []
    for h in range(len(heads)):
        acc_h = acc[:, h * LANES:(h + 1) * LANES]
        sums_of.append(tuple(acc_h[n * rows:(n + 1) * rows] for n in range(4)))
    near = [_dot(q * jnp.exp(a - mid), k * jnp.exp(mid - a), NT) * half_tril
            for (q, k, _), (a, _, mid, _) in zip(heads, sums_of)]
    far = [_dot(q * jnp.exp(jnp.minimum(a - bound, 0.0)), k * jnp.exp(jnp.minimum(bound - a, 0.0)), NT) * lower_left
           for (q, k, _), (a, _, _, bound) in zip(heads, sums_of)]
    intra = [_dot(n + f, ir) for n, f, (_, _, ir) in zip(near, far, heads)]
    return tuple((q * jnp.exp(a), o_intra, k * jnp.exp(tot - a), jnp.exp(tot))
                 for (q, k, _), (a, tot, _, _), o_intra in zip(heads, sums_of, intra))


def _hg_prep_args(q_ref, f_ref, i_ref, l0_ref, l1_ref):
    flat = []
    for h in range(HG_HEADS):
        ln = _head_lanes(h)
        flat += [q_ref[:, ln], f_ref[:, ln], i_ref[:, ln], l0_ref[:, ln], l1_ref[:, ln]]
    return flat


def _hg_post(o, gr, nw):
    on = o * lax.rsqrt(jnp.mean(o * o, axis=-1, keepdims=True) + NORM_EPS)
    return on * nw * (gr * _sigmoid(gr))


def _hg_specs(T, tb, rev):
    nT = T // tb
    tix = (lambda t: nT - 1 - t) if rev else (lambda t: t)
    col = lambda blk: pl.BlockSpec((tb, HG_WIDTH), lambda t: (tix(t), blk))
    vec = pl.BlockSpec((1, HG_WIDTH), lambda t: (0, 0))
    st = pl.BlockSpec((HG_HEADS, tb // HG_CHUNK, HG_HEAD_DIM, HG_HEAD_DIM), lambda t: (0, tix(t), 0, 0))
    return nT, col, vec, st


def _hg_fwd(proj, l0, l1, nw, name):
    T = proj.shape[0]
    tb = _row_tile(T, SCAN_ROWS)
    nsub = tb // HG_CHUNK
    nT, col, vec, st = _hg_specs(T, tb, False)

    def body(q_ref, f_ref, i_ref, g_ref, l0_ref, l1_ref, nw_ref, o_ref, st_ref, s_ref, qe_ref, kd_ref, dec_ref):
        @pl.when(pl.program_id(0) == 0)
        def _():
            s_ref[...] = jnp.zeros_like(s_ref)

        consts = _hg_consts(tb)
        outs = _hg_prep(consts, *_hg_prep_args(q_ref, f_ref, i_ref, l0_ref, l1_ref))
        for h, (qe, o_intra, kd, dec) in enumerate(outs):
            qe_ref[h], kd_ref[h], dec_ref[h] = qe, kd, dec
            o_ref[:, _head_lanes(h)] = o_intra

        def step(c, carry):
            rows = _chunk_rows(c, HG_CHUNK)
            heads = range(HG_HEADS)
            states = [s_ref[h] for h in heads]
            inter = [_dot(qe_ref[h, rows, :], states[h], NT) for h in heads]
            updates = [_dot(i_ref[rows, _head_lanes(h)], kd_ref[h, rows, :], TN) for h in heads]
            for h in heads:
                st_ref[h, c] = states[h]
                o_ref[rows, _head_lanes(h)] += inter[h]
                s_ref[h] = states[h] * dec_ref[h, pl.ds(c * HG_CHUNK, 1), :] + updates[h]
            return carry

        lax.fori_loop(0, nsub, step, 0)
        for h in range(HG_HEADS):
            ln = _head_lanes(h)
            o_ref[:, ln] = _hg_post(o_ref[:, ln], g_ref[:, ln], nw_ref[:, ln])

    blk = pltpu.VMEM((HG_HEADS, tb, LANES), F32)
    return pl.pallas_call(
        body, name=name, grid=(nT,),
        in_specs=[col(0), col(1), col(2), col(3), vec, vec, vec],
        out_specs=[col(0), st],
        out_shape=[jax.ShapeDtypeStruct((T, HG_WIDTH), F32),
                   jax.ShapeDtypeStruct((HG_HEADS, T // HG_CHUNK, HG_HEAD_DIM, HG_HEAD_DIM), F32)],
        scratch_shapes=[pltpu.VMEM((HG_HEADS, HG_HEAD_DIM, HG_HEAD_DIM), F32), blk, blk, blk],
    )(proj, proj, proj, proj, l0, l1, nw)


def _hg_bwd(proj, states, do, do_blk, l0, l1, nw, name):
    T = proj.shape[0]
    tb = _row_tile(T, SCAN_ROWS)
    nsub = tb // HG_CHUNK
    nT, col, vec, st = _hg_specs(T, tb, True)

    def body(q_ref, f_ref, i_ref, g_ref, st_ref, do_ref, l0_ref, l1_ref, nw_ref,
             dq_ref, df_ref, di_ref, dg_ref, dl0_ref, dl1_ref, dnw_ref,
             ds_ref, qe_ref, kd_ref, dec_ref, o_ref, dqe_ref, dkd_ref, ddec_ref, dis_ref):
        @pl.when(pl.program_id(0) == 0)
        def _():
            ds_ref[...] = jnp.zeros_like(ds_ref)
            dl0_ref[...] = jnp.zeros_like(dl0_ref)
            dl1_ref[...] = jnp.zeros_like(dl1_ref)
            dnw_ref[...] = jnp.zeros_like(dnw_ref)

        consts = _hg_consts(tb)
        outs, prep_vjp = jax.vjp(functools.partial(_hg_prep, consts),
                                 *_hg_prep_args(q_ref, f_ref, i_ref, l0_ref, l1_ref))
        for h, (qe, o_intra, kd, dec) in enumerate(outs):
            qe_ref[h], kd_ref[h], dec_ref[h], o_ref[h] = qe, kd, dec, o_intra

        def redo(c, carry):
            rows = _chunk_rows(c, HG_CHUNK)
            inter = [_dot(qe_ref[h, rows, :], st_ref[h, c], NT) for h in range(HG_HEADS)]
            for h in range(HG_HEADS):
                o_ref[h, rows, :] += inter[h]
            return carry

        lax.fori_loop(0, nsub, redo, 0)
        for h in range(HG_HEADS):
            ln = _head_lanes(h)
            _, vjp = jax.vjp(_hg_post, o_ref[h], g_ref[:, ln], nw_ref[:, ln])
            d_o, dgr, dnw = vjp(do_ref[:, ln])
            o_ref[h] = d_o
            dg_ref[:, ln] = dgr.astype(dg_ref.dtype)
            dnw_ref[:, ln] += dnw
        ddec_ref[...] = jnp.zeros_like(ddec_ref)

        def step(i, carry):
            c = nsub - 1 - i
            rows = _chunk_rows(c, HG_CHUNK)
            row0 = pl.ds(c * HG_CHUNK, 1)
            heads = range(HG_HEADS)
            Gs = [ds_ref[h] for h in heads]
            Ss = [st_ref[h, c] for h in heads]
            d_os = [o_ref[h, rows, :] for h in heads]
            dqe = [_dot(d_os[h], Ss[h]) for h in heads]
            dkd = [_dot(i_ref[rows, _head_lanes(h)], Gs[h]) for h in heads]
            dis = [_dot(kd_ref[h, rows, :], Gs[h], NT) for h in heads]
            back = [_dot(d_os[h], qe_ref[h, rows, :], TN) for h in heads]
            for h in heads:
                dqe_ref[h, rows, :] = dqe[h]
                dkd_ref[h, rows, :] = dkd[h]
                dis_ref[h, rows, :] = dis[h]
                ddec_ref[h, row0, :] = jnp.sum(Ss[h] * Gs[h], axis=0, keepdims=True)
                ds_ref[h] = Gs[h] * dec_ref[h, row0, :] + back[h]
            return carry

        lax.fori_loop(0, nsub, step, 0)
        grads = prep_vjp(tuple((dqe_ref[h], o_ref[h], dkd_ref[h], ddec_ref[h]) for h in range(HG_HEADS)))
        for h in range(HG_HEADS):
            ln = _head_lanes(h)
            dq, df, di, dl0, dl1 = grads[N_HG_IN * h:N_HG_IN * (h + 1)]
            dq_ref[:, ln] = dq.astype(dq_ref.dtype)
            df_ref[:, ln] = df.astype(df_ref.dtype)
            di_ref[:, ln] = (di + dis_ref[h]).astype(di_ref.dtype)
            dl0_ref[:, ln] += dl0
            dl1_ref[:, ln] += dl1

    dcol = jax.ShapeDtypeStruct((T, HG_WIDTH), BF16)
    dvec = jax.ShapeDtypeStruct((1, HG_WIDTH), F32)
    blk = pltpu.VMEM((HG_HEADS, tb, LANES), F32)
    return pl.pallas_call(
        body, name=name, grid=(nT,),
        in_specs=[col(0), col(1), col(2), col(3), st, col(do_blk), vec, vec, vec],
        out_specs=[col(0)] * 4 + [vec] * 3,
        out_shape=[dcol] * 4 + [dvec] * 3,
        scratch_shapes=[pltpu.VMEM((HG_HEADS, HG_HEAD_DIM, HG_HEAD_DIM), F32)] + [blk] * 8,
    )(proj, proj, proj, proj, states, do, l0, l1, nw)


def _rw_consts(rows):
    same, tril, stril = _chunk_masks(rows, RW_CHUNK)
    br, bc = _iota2((LANES, LANES), 0), _iota2((LANES, LANES), 1)
    blockdiag = ((br < RW_HEAD_DIM) == (bc < RW_HEAD_DIM)).astype(F32)
    m0 = (_iota2((1, LANES), 1) < RW_HEAD_DIM).astype(F32)
    return same, tril, stril, blockdiag, m0, 1.0 - m0


def _unit_lower_inverses_impl(lows):
    rows = lows[0].shape[0]
    eye = (_iota2(lows[0].shape, 0) == _iota2(lows[0].shape, 1)).astype(F32)
    xs = [low + eye for low in lows]
    ps = [_dot(low, low) for low in lows]
    n = 4
    while n < RW_CHUNK:
        zs = [_dot(jnp.concatenate([p, x], axis=0), p) for p, x in zip(ps, xs)]
        ps = [z[:rows] for z in zs]
        xs = [x + z[rows:] for x, z in zip(xs, zs)]
        n *= 2
    return tuple(x + _dot(x, p) for x, p in zip(xs, ps))


@jax.custom_vjp
def _unit_lower_inverses(lows):
    return _unit_lower_inverses_impl(lows)


def _unit_lower_inverses_fwd(lows):
    xs = _unit_lower_inverses_impl(lows)
    return xs, xs


def _unit_lower_inverses_bwd(xs, dxs):
    ts = [_dot(x, dx, TN) for x, dx in zip(xs, dxs)]
    return (tuple(_dot(t, x, NT) for t, x in zip(ts, xs)),)


_unit_lower_inverses.defvjp(_unit_lower_inverses_fwd, _unit_lower_inverses_bwd)


N_PREP_IN = 12
RW_GROUP = 2


def _rw_prep(consts, *flat):
    same, tril, stril, blockdiag, m0, m1 = consts
    rows = tril.shape[0]
    masks = (m0, m1)
    pairs = [flat[N_PREP_IN * i:N_PREP_IN * (i + 1)] for i in range(len(flat) // N_PREP_IN)]
    lora = [(_dot(jnp.tanh(lw), w2p), _dot(lw, a2p), _dot(_sigmoid(gd), g2), _dot(jnp.square(kx * k_k), blockdiag))
            for _, kx, _, lw, gd, _, _, k_k, _, w2p, a2p, g2 in pairs]
    mid = []
    for (r, kx, v, lw, gd, w0, a0, k_k, k_a, w2p, a2p, g2), (xw, xa, g, kk_sq) in zip(pairs, lora):
        xw = w0 + xw
        w = jnp.minimum(xw, 0.0) - jnp.log(1.0 + jnp.exp(-jnp.abs(xw))) - 0.5
        a_s = _sigmoid(a0 + xa)
        kk = kx * k_k / jnp.maximum(jnp.sqrt(kk_sq), L2_EPS)
        mid.append((-jnp.exp(w), a_s, kk, kx * (1.0 + (a_s - 1.0) * k_a), g))
    accs = [_dot(jnp.concatenate([tril, same], axis=0), ld, NN, HIGHEST) for ld, _, _, _, _ in mid]
    pre = []
    for (r, _, v, *_), (ld, a_s, kk, k2, g), acc in zip(pairs, mid, accs):
        bv = kk * a_s
        cum, tot = acc[:rows], acc[rows:]
        ecn = jnp.exp(-cum)
        a_t = -kk * jnp.exp(cum - ld)
        r_t = r * jnp.exp(cum)
        rem = jnp.exp(tot - cum)
        pre.append((v, a_t, r_t, (bv * ecn, k2 * ecn), (bv * rem, k2 * rem, jnp.exp(tot), k2, g)))
    zs = [_dot(jnp.concatenate([a_t * m0, a_t * m1, r_t * m0, r_t * m1], axis=0), jnp.concatenate(bk, axis=0), NT)
          for _, a_t, r_t, bk, _ in pre]
    pre = [(v, a_t, r_t, z, out) for (v, a_t, r_t, _, out), z in zip(pre, zs)]
    heads = [(i, h) for i in range(len(pre)) for h in range(2)]
    za = {ih: pre[ih[0]][3][ih[1] * rows:(ih[1] + 1) * rows] for ih in heads}
    zr = {ih: pre[ih[0]][3][(2 + ih[1]) * rows:(3 + ih[1]) * rows] for ih in heads}
    tinv = dict(zip(heads, _unit_lower_inverses(tuple(za[ih][:, :rows] * stril for ih in heads))))
    lv = {ih: _dot(jnp.concatenate([za[ih][:, rows:] * stril, zr[ih][:, rows:] * tril], axis=0), pre[ih[0]][0])
          for ih in heads}
    wu = {ih: _dot(tinv[ih], jnp.concatenate([pre[ih[0]][1] * masks[ih[1]], lv[ih][:rows]], axis=1)) for ih in heads}
    w_m = {ih: wu[ih][:, :LANES] for ih in heads}
    u_m = {ih: masks[ih[1]] * wu[ih][:, LANES:] for ih in heads}
    qy = {ih: _dot(zr[ih][:, :rows] * tril, jnp.concatenate([w_m[ih], u_m[ih]], axis=1)) for ih in heads}
    outs = []
    for i in range(len(pre)):
        a, b = (i, 0), (i, 1)
        W = w_m[a] + w_m[b]
        U = u_m[a] + u_m[b]
        Q = pre[i][2] + qy[a][:, :LANES] + qy[b][:, :LANES]
        Y0 = qy[a][:, LANES:] + qy[b][:, LANES:] + m0 * lv[a][rows:] + m1 * lv[b][rows:]
        outs.append((W, U, Q, Y0) + pre[i][4])
    return tuple(outs)


N_POST_IN = 8


def _rw_post(blockdiag, *flat):
    inv_n = 1.0 / RW_HEAD_DIM
    pairs = [flat[N_POST_IN * i:N_POST_IN * (i + 1)] for i in range(len(flat) // N_POST_IN)]
    sums = [(_dot(y, blockdiag), _dot(r * k2 * r_k, blockdiag)) for y, r, _, k2, _, r_k, _, _ in pairs]
    centred = [p[0] - s[0] * inv_n for p, s in zip(pairs, sums)]
    variances = [_dot(yc * yc, blockdiag) * inv_n for yc in centred]
    return tuple((yc * lax.rsqrt(var + RW_GN_EPS) * ln_w + ln_b + s[1] * v) * g
                 for (_, _, v, _, g, _, ln_w, ln_b), s, yc, var in zip(pairs, sums, centred, variances))


N_RW_VEC = 7
N_RW_MAT = 3


def _rw_specs(T, tb, rev):
    nT = T // tb
    tix = (lambda t: nT - 1 - t) if rev else (lambda t: t)
    wide = lambda blk: pl.BlockSpec((tb, RW_WIDTH), lambda t: (tix(t), blk))
    narrow = lambda blk: pl.BlockSpec((tb, LANES), lambda t: (tix(t), blk))
    vec = pl.BlockSpec((1, RW_WIDTH), lambda t: (0, 0))
    mat = pl.BlockSpec((RW_PAIRS, LANES, LANES), lambda t: (0, 0, 0))
    st = pl.BlockSpec((RW_PAIRS, tb // RW_CHUNK, LANES, LANES), lambda t: (0, tix(t), 0, 0))
    lora0 = 3 * RW_WIDTH // LANES
    ins = [wide(0), wide(1), wide(2), narrow(lora0), narrow(lora0 + 1)]
    return nT, wide, vec, mat, st, ins


def _rw_prep_args(p, r_ref, k_ref, v_ref, lw_ref, gd_ref, vrefs, mrefs):
    ln = _head_lanes(p)
    w0, a0, k_k, k_a = [x[:, ln] for x in vrefs[:4]]
    return (r_ref[:, ln], k_ref[:, ln], v_ref[:, ln], lw_ref[...], gd_ref[...], w0, a0, k_k, k_a,
            *[x[p] for x in mrefs])


def _stack_chunks(ref, top, bottom):
    C = RW_CHUNK
    for c in range(ref.shape[0]):
        ref[c, 0:C, :] = top[c * C:(c + 1) * C]
        ref[c, C:2 * C, :] = bottom[c * C:(c + 1) * C]


def _group_args(p0, r_ref, k_ref, v_ref, lw_ref, gd_ref, vrefs, mrefs):
    flat = []
    for p in range(p0, p0 + RW_GROUP):
        flat += list(_rw_prep_args(p, r_ref, k_ref, v_ref, lw_ref, gd_ref, vrefs, mrefs))
    return flat


def _rw_fwd(rws, vecs, mats, name):
    T = rws.shape[0]
    tb = _row_tile(T, SCAN_ROWS)
    nsub = tb // RW_CHUNK
    C = RW_CHUNK
    nT, wide, vec, mat, st, ins = _rw_specs(T, tb, False)

    def body(*refs):
        r_ref, k_ref, v_ref, lw_ref, gd_ref = refs[:5]
        vrefs = refs[5:5 + N_RW_VEC]
        mrefs = refs[5 + N_RW_VEC:5 + N_RW_VEC + N_RW_MAT]
        o_ref, st_ref, s_ref, wq_ref, uy_ref, bk_ref, misc_ref, y_ref = refs[-8:]

        @pl.when(pl.program_id(0) == 0)
        def _():
            s_ref[...] = jnp.zeros_like(s_ref)

        consts = _rw_consts(tb)
        blockdiag = consts[3]
        for p0 in range(0, RW_PAIRS, RW_GROUP):
            outs = _rw_prep(consts, *_group_args(p0, r_ref, k_ref, v_ref, lw_ref, gd_ref, vrefs, mrefs))
            for p, (W, U, Q, Y0, Bg, Kg, dec, k2, g) in zip(range(p0, p0 + RW_GROUP), outs):
                _stack_chunks(wq_ref.at[p], W, Q)
                _stack_chunks(uy_ref.at[p], U, Y0)
                _stack_chunks(bk_ref.at[p], Bg, Kg)
                misc_ref[0, p], misc_ref[1, p], misc_ref[2, p] = dec, k2, g

        def step(c, carry):
            rows = _chunk_rows(c, C)
            states = [s_ref[p] for p in range(RW_PAIRS)]
            for p, S in enumerate(states):
                st_ref[p, c] = S
            pys = [_dot(wq_ref[p, c], S, NT) + uy_ref[p, c] for p, S in enumerate(states)]
            pvs = [jnp.concatenate([py[:C], v_ref[rows, _head_lanes(p)]], axis=0) for p, py in enumerate(pys)]
            updates = [_dot(pv, bk_ref[p, c], TN) for p, pv in enumerate(pvs)]
            for p, S in enumerate(states):
                y_ref[p, rows, :] = pys[p][C:]
                s_ref[p] = (S * misc_ref[0, p, pl.ds(c * C, 1), :] + updates[p]) * blockdiag
            return carry

        lax.fori_loop(0, nsub, step, 0)
        flat = []
        for p in range(RW_PAIRS):
            ln = _head_lanes(p)
            flat += [y_ref[p], r_ref[:, ln], v_ref[:, ln], misc_ref[1, p], misc_ref[2, p]] + [x[:, ln] for x in vrefs[4:]]
        for p, out in enumerate(_rw_post(blockdiag, *flat)):
            o_ref[:, _head_lanes(p)] = out

    stacked = pltpu.VMEM((RW_PAIRS, nsub, 2 * C, LANES), F32)
    return pl.pallas_call(
        body, name=name, grid=(nT,),
        in_specs=ins + [vec] * N_RW_VEC + [mat] * N_RW_MAT,
        out_specs=[wide(0), st],
        out_shape=[jax.ShapeDtypeStruct((T, RW_WIDTH), F32),
                   jax.ShapeDtypeStruct((RW_PAIRS, T // RW_CHUNK, LANES, LANES), F32)],
        scratch_shapes=[pltpu.VMEM((RW_PAIRS, LANES, LANES), F32), stacked, stacked, stacked,
                        pltpu.VMEM((3, RW_PAIRS, tb, LANES), F32), pltpu.VMEM((RW_PAIRS, tb, LANES), F32)],
    )(rws, rws, rws, rws, rws, *vecs, *mats)


def _rw_bwd(rws, states, do, do_blk, vecs, mats, name):
    T = rws.shape[0]
    tb = _row_tile(T, SCAN_ROWS)
    nsub = tb // RW_CHUNK
    C = RW_CHUNK
    G = RW_GROUP
    nT, wide, vec, mat, st, ins = _rw_specs(T, tb, True)
    nin = 5 + 1 + 1 + N_RW_VEC + N_RW_MAT

    def body(*refs):
        r_ref, k_ref, v_ref, lw_ref, gd_ref = refs[:5]
        st_ref, do_ref = refs[5], refs[6]
        vrefs = refs[7:7 + N_RW_VEC]
        mrefs = refs[7 + N_RW_VEC:nin]
        dr_ref, dk_ref, dv_ref, dlo_ref = refs[nin:nin + 4]
        dvec = refs[nin + 4:nin + 4 + N_RW_VEC]
        dmat = refs[nin + 4 + N_RW_VEC:nin + 4 + N_RW_VEC + N_RW_MAT]
        ds_ref, wq_ref, uy_ref, bk_ref, pv_ref, dec_ref, y_ref, dpre_ref, dvs_ref = refs[-9:]

        @pl.when(pl.program_id(0) == 0)
        def _():
            ds_ref[...] = jnp.zeros_like(ds_ref)
            for x in dvec + dmat:
                x[...] = jnp.zeros_like(x)

        consts = _rw_consts(tb)
        blockdiag = consts[3]
        dlw, dgd = 0.0, 0.0
        for p0 in range(0, RW_PAIRS, G):
            outs, prep_vjp = jax.vjp(functools.partial(_rw_prep, consts),
                                     *_group_args(p0, r_ref, k_ref, v_ref, lw_ref, gd_ref, vrefs, mrefs))
            for q, (W, U, Q, Y0, Bg, Kg, dec, _, _) in enumerate(outs):
                _stack_chunks(wq_ref.at[q], W, Q)
                _stack_chunks(uy_ref.at[q], U, Y0)
                _stack_chunks(bk_ref.at[q], Bg, Kg)
                dec_ref[q] = dec

            def redo(c, carry, p0=p0):
                rows = _chunk_rows(c, C)
                pys = [_dot(wq_ref[q, c], st_ref[p0 + q, c], NT) + uy_ref[q, c] for q in range(G)]
                for q, py in enumerate(pys):
                    y_ref[q, rows, :] = py[C:]
                    pv_ref[q, c, 0:C, :] = py[:C]
                    pv_ref[q, c, C:2 * C, :] = v_ref[rows, _head_lanes(p0 + q)]
                return carry

            lax.fori_loop(0, nsub, redo, 0)
            flat = []
            for q in range(G):
                ln = _head_lanes(p0 + q)
                flat += [y_ref[q], r_ref[:, ln], v_ref[:, ln], outs[q][7], outs[q][8]] + [x[:, ln] for x in vrefs[4:]]
            _, post_vjp = jax.vjp(functools.partial(_rw_post, blockdiag), *flat)
            post_grads = post_vjp(tuple(do_ref[:, _head_lanes(p0 + q)] for q in range(G)))
            post = []
            for q in range(G):
                ln = _head_lanes(p0 + q)
                dy, dr2, dv2, dk2, dg, dr_k, dln_w, dln_b = post_grads[N_POST_IN * q:N_POST_IN * (q + 1)]
                dpre_ref[q, 3] = dy
                dvs_ref[q] = dv2
                for x, gx in zip(dvec[4:], (dr_k, dln_w, dln_b)):
                    x[:, ln] += gx
                dpre_ref[q, 6] = jnp.zeros_like(dpre_ref[q, 6])
                post.append((dr2, dk2, dg))

            def step(i, carry, p0=p0):
                c = nsub - 1 - i
                rows = _chunk_rows(c, C)
                row0 = pl.ds(c * C, 1)
                qs = range(G)
                Gs = [ds_ref[p0 + q] * blockdiag for q in qs]
                Ss = [st_ref[p0 + q, c] for q in qs]
                t1 = [_dot(bk_ref[q, c], Gs[q], NT) for q in qs]
                t3 = [_dot(pv_ref[q, c], Gs[q]) for q in qs]
                dpy = [jnp.concatenate([t1[q][:C], dpre_ref[q, 3, rows, :]], axis=0) for q in qs]
                t2 = [_dot(dpy[q], Ss[q]) for q in qs]
                back = [_dot(dpy[q], wq_ref[q, c], TN) for q in qs]
                for q in qs:
                    dvs_ref[q, rows, :] += t1[q][C:]
                    dpre_ref[q, 0, rows, :] = t2[q][:C]
                    dpre_ref[q, 1, rows, :] = t1[q][:C]
                    dpre_ref[q, 2, rows, :] = t2[q][C:]
                    dpre_ref[q, 4, rows, :] = t3[q][:C]
                    dpre_ref[q, 5, rows, :] = t3[q][C:]
                    dpre_ref[q, 6, row0, :] = jnp.sum(Ss[q] * Gs[q], axis=0, keepdims=True)
                    ds_ref[p0 + q] = Gs[q] * dec_ref[q, row0, :] + back[q]
                return carry

            lax.fori_loop(0, nsub, step, 0)
            grads = prep_vjp(tuple(tuple(dpre_ref[q, i] for i in range(7)) + post[q][1:] for q in range(G)))
            for q in range(G):
                ln = _head_lanes(p0 + q)
                gq = grads[N_PREP_IN * q:N_PREP_IN * (q + 1)]
                dr_ref[:, ln] = gq[0] + post[q][0]
                dk_ref[:, ln] = gq[1]
                dv_ref[:, ln] = gq[2] + dvs_ref[q]
                dlw = dlw + gq[3]
                dgd = dgd + gq[4]
                for x, gx in zip(dvec[:4], gq[5:9]):
                    x[:, ln] += gx
                for x, gx in zip(dmat, gq[9:]):
                    x[p0 + q] += gx
        dlo_ref[:, 0:LANES] = dlw
        dlo_ref[:, LANES:2 * LANES] = dgd

    dcol = jax.ShapeDtypeStruct((T, RW_WIDTH), F32)
    dlo_spec = pl.BlockSpec((tb, 2 * LANES), lambda t: (nT - 1 - t, 0))
    blk = pltpu.VMEM((G, tb, LANES), F32)
    stacked = pltpu.VMEM((G, nsub, 2 * C, LANES), F32)
    return pl.pallas_call(
        body, name=name, grid=(nT,),
        in_specs=ins + [st, wide(do_blk)] + [vec] * N_RW_VEC + [mat] * N_RW_MAT,
        out_specs=[wide(0)] * 3 + [dlo_spec] + [vec] * N_RW_VEC + [mat] * N_RW_MAT,
        out_shape=[dcol] * 3 + [jax.ShapeDtypeStruct((T, 2 * LANES), F32)]
        + [jax.ShapeDtypeStruct((1, RW_WIDTH), F32)] * N_RW_VEC
        + [jax.ShapeDtypeStruct((RW_PAIRS, LANES, LANES), F32)] * N_RW_MAT,
        scratch_shapes=[pltpu.VMEM((RW_PAIRS, LANES, LANES), F32), stacked, stacked, stacked, stacked, blk, blk,
                        pltpu.VMEM((G, 7, tb, LANES), F32), blk],
    )(rws, rws, rws, rws, rws, states, do, *vecs, *mats)


def _my_index():
    return 4 * lax.axis_index("x") + 2 * lax.axis_index("y") + lax.axis_index("c")


def _peer(bits):
    pos = []
    for name, flip in zip(("x", "y", "c"), bits):
        i = lax.axis_index(name)
        pos.append(1 - i if flip else i)
    return tuple(pos)


def _peer_index(bits):
    x, y, c = _peer(bits)
    return 4 * x + 2 * y + c


def _all_gather(shards, name):
    n = len(shards)
    chips = [(1, 0, 0), (0, 1, 0), (1, 1, 0)]
    sib = (0, 0, 1)

    def body(*refs):
        ins, outs = refs[:n], refs[n:2 * n]
        send_sems, recv_sems, local_sems = refs[2 * n:]

        def rows(k, dev):
            r = ins[k].shape[0]
            return outs[k].at[pl.ds(dev * r, r), :]

        def copy(k, slot, block_dev, to_bits, src=None):
            return pltpu.make_async_remote_copy(
                src_ref=rows(k, block_dev) if src is None else src, dst_ref=rows(k, block_dev),
                send_sem=send_sems.at[k, slot], recv_sem=recv_sems.at[k, slot],
                device_id=_peer(to_bits), device_id_type=MESH_ID)

        me = _my_index()
        started = []
        for k in range(n):
            mine = pltpu.make_async_copy(ins[k], rows(k, me), local_sems.at[k])
            mine.start()
            started.append(mine)
        sends = []
        for k in range(n):
            first = [copy(k, 0, me, sib, src=ins[k])]
            first += [copy(k, 1 + j, me, chip, src=ins[k]) for j, chip in enumerate(chips)]
            for cp in first:
                cp.start()
            sends += first
        for k in range(n):
            for j, chip in enumerate(chips):
                copy(k, 1 + j, _peer_index(chip), chip).wait_recv()
                fwd = copy(k, 4 + j, _peer_index(chip), sib)
                fwd.start()
                sends.append(fwd)
        for k in range(n):
            copy(k, 0, _peer_index(sib), sib).wait_recv()
            for j, chip in enumerate(chips):
                both = (chip[0], chip[1], 1)
                copy(k, 4 + j, _peer_index(both), sib).wait_recv()
        for cp in sends:
            cp.wait_send()
        for cp in started:
            cp.wait()

    any_spec = pl.BlockSpec(memory_space=pl.ANY)
    return pl.pallas_call(
        body, name=name,
        in_specs=[any_spec] * n, out_specs=[any_spec] * n,
        out_shape=[jax.ShapeDtypeStruct((N_DEV * s.shape[0], s.shape[1]), s.dtype) for s in shards],
        scratch_shapes=[pltpu.SemaphoreType.DMA((n, 7)), pltpu.SemaphoreType.DMA((n, 7)),
                        pltpu.SemaphoreType.DMA((n,))],
    )(*shards)


def _exchange(partials, name):
    n = len(partials)
    flips = [(dx, dy, dc) for dx in (0, 1) for dy in (0, 1) for dc in (0, 1)][1:]

    def body(*refs):
        ins, outs = refs[:n], refs[n:2 * n]
        send_sems, recv_sems, local_sems = refs[2 * n:]
        me = _my_index()
        local = []
        for k in range(n):
            cp = pltpu.make_async_copy(ins[k].at[me], outs[k].at[me], local_sems.at[k])
            cp.start()
            local.append(cp)
        copies = []
        for k in range(n):
            for d, bits in enumerate(flips):
                cp = pltpu.make_async_remote_copy(
                    src_ref=ins[k].at[_peer_index(bits)], dst_ref=outs[k].at[me],
                    send_sem=send_sems.at[k, d], recv_sem=recv_sems.at[k, d],
                    device_id=_peer(bits), device_id_type=MESH_ID)
                cp.start()
                copies.append(cp)
        for cp in copies:
            cp.wait_recv()
        for cp in copies:
            cp.wait_send()
        for cp in local:
            cp.wait()

    any_spec = pl.BlockSpec(memory_space=pl.ANY)
    return pl.pallas_call(
        body, name=name,
        in_specs=[any_spec] * n, out_specs=[any_spec] * n,
        out_shape=[jax.ShapeDtypeStruct(p.shape, p.dtype) for p in partials],
        scratch_shapes=[pltpu.SemaphoreType.DMA((n, 7)), pltpu.SemaphoreType.DMA((n, 7)),
                        pltpu.SemaphoreType.DMA((n,))],
    )(*partials)


HBM_SPEC = pl.BlockSpec(memory_space=pltpu.HBM)
SEM_SPEC = pl.BlockSpec(memory_space=pltpu.SEMAPHORE)
ALL_FLIPS = [(dx, dy, dc) for dx in (0, 1) for dy in (0, 1) for dc in (0, 1)][1:]


def _spread_copies(srcs, lands, send_sems, recv_sems, to_x):
    me = _my_index()
    my_x = lax.axis_index("x")
    copies = []
    for k, land in enumerate(lands):
        for d, bits in enumerate(ALL_FLIPS):
            if not srcs:
                src = land.at[me]
            elif to_x is None:
                src = srcs[k].at[_peer_index(bits)]
            else:
                _, py, pc = _peer(bits)
                src = srcs[k].at[2 * py + pc]
            cp = pltpu.make_async_remote_copy(
                src_ref=src, dst_ref=land.at[me],
                send_sem=send_sems.at[k * 7 + d], recv_sem=recv_sems.at[k * 7 + d],
                device_id=_peer(bits), device_id_type=MESH_ID)
            sends = True if to_x is None else my_x == (to_x ^ bits[0])
            receives = True if to_x is None else my_x == to_x
            copies.append((cp, sends, receives))
    return copies


def _when(cond, fn):
    if cond is True:
        fn()
    else:
        pl.when(cond)(fn)


def _spread_start(srcs, lands, name, to_x=None):
    ns, n = len(srcs), len(lands)

    def body(*refs):
        src_refs, land_refs = refs[:ns], refs[ns:ns + n]
        send_sems, recv_sems = refs[ns + n], refs[ns + n + 1]
        token = refs[-1]
        for cp, sends, _ in _spread_copies(src_refs, land_refs, send_sems, recv_sems, to_x):
            _when(sends, cp.start)
        token[...] = jnp.zeros_like(token)

    bufs = list(srcs) + list(lands)
    out = pl.pallas_call(
        body, name=name,
        out_shape=(pltpu.SemaphoreType.DMA((7 * n,)), pltpu.SemaphoreType.DMA((7 * n,)),
                   *[pltpu.HBM(b.shape, b.dtype) for b in bufs], jax.ShapeDtypeStruct((8, LANES), F32)),
        in_specs=[HBM_SPEC] * (ns + n),
        out_specs=(SEM_SPEC, SEM_SPEC, *[HBM_SPEC] * (ns + n), pl.BlockSpec(memory_space=pltpu.VMEM)),
        input_output_aliases={i: 2 + i for i in range(ns + n)},
        compiler_params=pltpu.CompilerParams(has_side_effects=pltpu.SideEffectType.DATAFLOW_SIDE_EFFECTING),
    )(*[pltpu.with_memory_space_constraint(b, pltpu.HBM) for b in bufs])
    return out[0], out[1], list(out[2:2 + ns]), list(out[2 + ns:2 + ns + n]), out[-1]


def _spread_wait(send_sems, recv_sems, srcs, lands, after, name, to_x=None):
    ns, n = len(srcs), len(lands)

    def body(*refs):
        src_refs, land_refs = refs[:ns], refs[ns:ns + n]
        send_sems, recv_sems = refs[ns + n], refs[ns + n + 1]
        for cp, sends, receives in _spread_copies(src_refs, land_refs, send_sems, recv_sems, to_x):
            _when(sends, cp.wait_send)
            _when(receives, cp.wait_recv)

    bufs = list(srcs) + list(lands)
    out = pl.pallas_call(
        body, name=name,
        out_shape=tuple(pltpu.HBM(b.shape, b.dtype) for b in bufs),
        in_specs=[HBM_SPEC] * (ns + n) + [SEM_SPEC, SEM_SPEC, pl.BlockSpec(memory_space=pl.ANY)],
        out_specs=tuple([HBM_SPEC] * (ns + n)),
        input_output_aliases={i: i for i in range(ns + n)},
        compiler_params=pltpu.CompilerParams(has_side_effects=pltpu.SideEffectType.DATAFLOW_SIDE_EFFECTING),
    )(*bufs, send_sems, recv_sems, after)
    return list(out[ns:])


def _own_slot_only(block, me):
    return lax.dynamic_update_slice(lax.empty((N_DEV,) + block.shape, block.dtype), block[None], (me, 0, 0))


def _sum_slots(landed, name):
    _, R, C = landed.shape
    tb = _row_tile(R, 128)

    def body(l_ref, o_ref):
        acc = l_ref[0].astype(F32)
        for s in range(1, N_DEV):
            acc = acc + l_ref[s].astype(F32)
        o_ref[...] = acc

    return pl.pallas_call(
        body, name=name, grid=(R // tb,),
        in_specs=[pl.BlockSpec((N_DEV, tb, C), lambda i: (0, i, 0))],
        out_specs=pl.BlockSpec((tb, C), lambda i: (i, 0)),
        out_shape=jax.ShapeDtypeStruct((R, C), F32),
    )(landed)


def _pack_rows(flat_list, width=LANES):
    flat = jnp.concatenate([a.reshape(-1) for a in flat_list])
    n = flat.shape[0]
    rows = -(-n // width)
    rows = -(-rows // 8) * 8
    return jnp.pad(flat, (0, rows * width - n)).reshape(rows, width)


def _unpack(packed, shapes):
    flat = packed.reshape(-1)
    out, off = [], 0
    for s in shapes:
        n = 1
        for d in s:
            n *= d
        out.append(flat[off:off + n].reshape(s))
        off += n
    return out


def kernel(x, norm1_w, w_in, hg_lb_logits, hg_norm_w, rw_shift_mu, rw_w0, rw_w2, rw_a0, rw_a2, rw_g2, rw_k_k, rw_k_a, rw_r_k, rw_ln_w, rw_ln_b, w_out, norm2_w, w_up, conv_w, conv_b, w_down, final_norm_w, loss_target, m_norm1_w, m_w_in, m_hg_lb_logits, m_hg_norm_w, m_rw_shift_mu, m_rw_w0, m_rw_w2, m_rw_a0, m_rw_a2, m_rw_g2, m_rw_k_k, m_rw_k_a, m_rw_r_k, m_rw_ln_w, m_rw_ln_b, m_w_out, m_norm2_w, m_w_up, m_conv_w, m_conv_b, m_w_down, m_final_norm_w, v_norm1_w, v_w_in, v_hg_lb_logits, v_hg_norm_w, v_rw_shift_mu, v_rw_w0, v_rw_w2, v_rw_a0, v_rw_a2, v_rw_g2, v_rw_k_k, v_rw_k_a, v_rw_r_k, v_rw_ln_w, v_rw_ln_b, v_w_out, v_norm2_w, v_w_up, v_conv_w, v_conv_b, v_w_down, v_final_norm_w):
    weights = dict(norm1_w=norm1_w, w_in=w_in, hg_lb_logits=hg_lb_logits, hg_norm_w=hg_norm_w,
                   rw_shift_mu=rw_shift_mu, rw_w0=rw_w0, rw_w2=rw_w2, rw_a0=rw_a0, rw_a2=rw_a2, rw_g2=rw_g2,
                   rw_k_k=rw_k_k, rw_k_a=rw_k_a, rw_r_k=rw_r_k, rw_ln_w=rw_ln_w, rw_ln_b=rw_ln_b, w_out=w_out,
                   norm2_w=norm2_w, w_up=w_up, conv_w=conv_w, conv_b=conv_b, w_down=w_down,
                   final_norm_w=final_norm_w)
    m_in = dict(norm1_w=m_norm1_w, w_in=m_w_in, hg_lb_logits=m_hg_lb_logits, hg_norm_w=m_hg_norm_w,
                rw_shift_mu=m_rw_shift_mu, rw_w0=m_rw_w0, rw_w2=m_rw_w2, rw_a0=m_rw_a0, rw_a2=m_rw_a2,
                rw_g2=m_rw_g2, rw_k_k=m_rw_k_k, rw_k_a=m_rw_k_a, rw_r_k=m_rw_r_k, rw_ln_w=m_rw_ln_w,
                rw_ln_b=m_rw_ln_b, w_out=m_w_out, norm2_w=m_norm2_w, w_up=m_w_up, conv_w=m_conv_w,
                conv_b=m_conv_b, w_down=m_w_down, final_norm_w=m_final_norm_w)
    v_in = dict(norm1_w=v_norm1_w, w_in=v_w_in, hg_lb_logits=v_hg_lb_logits, hg_norm_w=v_hg_norm_w,
                rw_shift_mu=v_rw_shift_mu, rw_w0=v_rw_w0, rw_w2=v_rw_w2, rw_a0=v_rw_a0, rw_a2=v_rw_a2,
                rw_g2=v_rw_g2, rw_k_k=v_rw_k_k, rw_k_a=v_rw_k_a, rw_r_k=v_rw_r_k, rw_ln_w=v_rw_ln_w,
                rw_ln_b=v_rw_ln_b, w_out=v_w_out, norm2_w=v_norm2_w, w_up=v_w_up, conv_w=v_conv_w,
                conv_b=v_conv_b, w_down=v_w_down, final_norm_w=v_final_norm_w)
    names = list(weights)
    sharded_small = ["rw_w2", "rw_a2", "rw_g2", "conv_w"]
    replicated = [n for n in names if n not in sharded_small + ["w_in", "w_out", "w_up", "w_down"]]

    xs = x[0]
    tgt = loss_target[0]

    small_shard = _pack_rows([weights[n] for n in sharded_small])
    g_win_t, g_small = _all_gather([w_in[0].T.astype(BF16), small_shard], "gather_weights")
    me = _my_index()
    later = (w_up[0].T.astype(BF16), w_out[0].astype(BF16), w_down[0].astype(BF16))
    later, _ = lax.optimization_barrier((later, g_small))
    later = [_own_slot_only(z, me) for z in later]
    g_send, g_recv, _, later, g_token = _spread_start([], later, "gather_later_start")
    small_shapes = [weights[n].shape for n in sharded_small]
    per_dev = [_unpack(g_small.reshape(N_DEV, -1)[j], small_shapes) for j in range(N_DEV)]
    w2_full, a2_full, g2_full, convw_full = [jnp.concatenate([per_dev[j][i][0] for j in range(N_DEV)], axis=-1)
                                             for i in range(4)]
    zeros64 = jnp.zeros((RW_PAIRS, 64, LANES), F32)
    by_pair = lambda z: z.reshape(z.shape[0], RW_PAIRS, LANES).transpose(1, 0, 2)
    w2p = jnp.concatenate([by_pair(w2_full), zeros64], axis=1)
    a2p = jnp.concatenate([zeros64, by_pair(a2_full)], axis=1)
    g2p = by_pair(g2_full)

    l0, l1 = hg_lb_logits[0:1], hg_lb_logits[1:2]
    h1 = _rms_fwd(xs, norm1_w + g_token[0:1, 0:1], "norm1")
    proj = _mm_nt(h1, g_win_t, "proj_in")
    o_hg, hg_states = _hg_fwd(proj, l0, l1, hg_norm_w, "hgrn2_fwd")
    rws = _shift_fwd(proj, rw_shift_mu, "token_shift")
    rw_vecs = [rw_w0, rw_a0, rw_k_k, rw_k_a, rw_r_k, rw_ln_w, rw_ln_b]
    rw_mats = [w2p, a2p, g2p]
    o_rw, rw_states = _rw_fwd(rws, rw_vecs, rw_mats, "rwkv7_fwd")
    o_mix = jnp.concatenate([o_hg, o_rw], axis=-1).astype(BF16)
    g_wup_t, g_wout, g_wdown = [z.reshape(-1, z.shape[-1])
                                for z in _spread_wait(g_send, g_recv, [], later, o_mix, "gather_later_wait")]
    x1 = _mm_nn(o_mix, g_wout, xs, "proj_out")
    h2 = _rms_fwd(x1, norm2_w, "norm2")
    u = _mm_nt(h2, g_wup_t, "ffn_up")
    act = _ffn_act_fwd(u, convw_full, conv_b, "ffn_act")
    x2 = _mm_nn(act, g_wdown, x1, "ffn_down")
    loss_part, dx2, d_final_w = _loss_head(x2, final_norm_w.reshape(1, -1), tgt, "loss_head")

    d_wdown = _mm_tn(act, dx2, 1408, "ffn_down_dw", BF16)
    dact = _mm_nt(dx2, g_wdown, "ffn_down_dx", BF16)
    du_g, du_v, dcw_g, dcw_v, dcb_g, dcb_v = _ffn_act_bwd(u, dact, convw_full, conv_b, "ffn_act_bwd")
    d_convw = jnp.concatenate([dcw_g, dcw_v], axis=-1)
    d_convb = jnp.concatenate([dcb_g, dcb_v], axis=-1)
    d_wup_t = jnp.concatenate([_mm_tn(du_g, h2, 1408, "ffn_up_dw_gate", BF16),
                               _mm_tn(du_v, h2, 1408, "ffn_up_dw_value", BF16)], axis=0)
    dh2 = _mm_nn([du_g, du_v], g_wup_t, None, "ffn_up_dx")
    dx1, d_norm2 = _rms_bwd(dh2, x1, norm2_w, dx2, "norm2_bwd")
    d_wout = _mm_tn(o_mix, dx1, 512, "proj_out_dw", BF16)
    do = _mm_nt(dx1, g_wout, "proj_out_dx")
    early = [z.reshape(N_DEV, z.shape[0] // N_DEV, z.shape[1]) for z in (d_wup_t, d_wout, d_wdown)]
    early_land = [_own_slot_only(lax.dynamic_index_in_dim(z, me, 0, keepdims=False), me) for z in early]
    e_send, e_recv, early, early_land, e_token = _spread_start(early, early_land, "exchange_early_start")
    hg_norm_w_t = hg_norm_w + e_token[0:1, 0:1]
    dq, df, di, dg, d_l0, d_l1, d_hg_nw = _hg_bwd(proj, hg_states, do, 0, l0, l1, hg_norm_w_t, "hgrn2_bwd")
    half = N_DEV // 2
    own_half_block = lambda z: _own_slot_only(lax.dynamic_index_in_dim(z, me % half, 0, keepdims=False), me)
    n_lo = half * w_in.shape[2]
    d_win_lo = _mm_tn(jnp.concatenate([dq, df, di, dg[:, :n_lo - 3 * HG_WIDTH]], axis=-1), h1, 640,
                      "proj_in_dw_low", BF16).reshape(half, -1, D_MODEL)
    m_send, m_recv, mid, mid_land, m_token = _spread_start([d_win_lo], [own_half_block(d_win_lo)],
                                                            "exchange_mid_start", to_x=0)
    rw_vecs_t = [rw_vecs[0] + m_token[0:1, 0:1]] + rw_vecs[1:]
    rw_out = _rw_bwd(rws, rw_states, do, 1, rw_vecs_t, rw_mats, "rwkv7_bwd")
    d_rw_vecs = rw_out[4:4 + N_RW_VEC]
    d_w2p, d_a2p, d_g2p = rw_out[4 + N_RW_VEC:]
    dp_parts, dmu_parts = [], []
    for i, z in enumerate(rw_out[:4]):
        dp, dmu = _shift_bwd(z, proj, rw_shift_mu, i * RW_WIDTH, "token_shift_bwd_%d" % i)
        dp_parts.append(dp)
        dmu_parts.append(dmu)
    d_mu = jnp.concatenate(dmu_parts, axis=-1)
    d_win_hi = _mm_tn(jnp.concatenate([dg[:, n_lo - 3 * HG_WIDTH:]] + dp_parts, axis=-1), h1, 640,
                      "proj_in_dw_high", BF16).reshape(half, -1, D_MODEL)
    from_pairs = lambda z: z.transpose(1, 0, 2).reshape(z.shape[1], RW_WIDTH)
    d_w2 = from_pairs(d_w2p[:, :64])
    d_a2 = from_pairs(d_a2p[:, 64:])
    d_g2 = from_pairs(d_g2p)
    col_blocks = lambda z: z.reshape(z.shape[0], N_DEV, -1).transpose(1, 0, 2)
    small_part = jnp.stack([
        _pack_rows([col_blocks(d_w2)[j], col_blocks(d_a2)[j], col_blocks(d_g2)[j], col_blocks(d_convw)[j]])
        for j in range(N_DEV)])
    l_send, l_recv, late, late_land, l_token = _spread_start([d_win_hi], [own_half_block(d_win_hi)],
                                                             "exchange_late_start", to_x=1)
    dh1 = _mm_nn([dq, df, di, dg] + dp_parts, g_win_t, None, "proj_in_dx")
    grad_x, d_norm1 = _rms_bwd(dh1, xs, norm1_w + l_token[0:1, 0:1], dx1, "norm1_bwd")

    rep_grads = dict(norm1_w=d_norm1, hg_lb_logits=jnp.concatenate([d_l0, d_l1], axis=0), hg_norm_w=d_hg_nw,
                     rw_shift_mu=d_mu, rw_w0=d_rw_vecs[0], rw_a0=d_rw_vecs[1], rw_k_k=d_rw_vecs[2],
                     rw_k_a=d_rw_vecs[3], rw_r_k=d_rw_vecs[4], rw_ln_w=d_rw_vecs[5], rw_ln_b=d_rw_vecs[6],
                     norm2_w=d_norm2, conv_b=d_convb, final_norm_w=d_final_w)
    rep_pack = _pack_rows([loss_part] + [rep_grads[n] for n in replicated])
    rep_part = jnp.broadcast_to(rep_pack[None], (N_DEV,) + rep_pack.shape)
    grads, delta, new_m, new_v = {}, {}, {}, {}

    def adamw_big(n, g):
        shp = weights[n].shape
        as2d = lambda z: z.reshape(shp[1], shp[2])
        grads[n] = g[None]
        d, nm, nv = _adamw(as2d(weights[n]), g, as2d(m_in[n]), as2d(v_in[n]), "adamw_" + n)
        delta[n], new_m[n], new_v[n] = d.reshape(shp), nm.reshape(shp), nv.reshape(shp)

    landed_early = _spread_wait(e_send, e_recv, early, early_land, grad_x, "exchange_early_wait")
    adamw_big("w_up", _sum_slots(landed_early[0], "sum_grads_w_up").T)
    adamw_big("w_out", _sum_slots(landed_early[1], "sum_grads_w_out"))
    adamw_big("w_down", _sum_slots(landed_early[2], "sum_grads_w_down"))
    (landed_mid,) = _spread_wait(m_send, m_recv, mid, mid_land, grad_x, "exchange_mid_wait", to_x=0)
    (landed_late,) = _spread_wait(l_send, l_recv, late, late_land, delta["w_down"], "exchange_late_wait", to_x=1)
    g_win = jnp.where(lax.axis_index("x") == 0, _sum_slots(landed_mid, "sum_grads_w_in_low"),
                      _sum_slots(landed_late, "sum_grads_w_in_high"))
    adamw_big("w_in", g_win.T)
    landed_rep, landed_small = _exchange([rep_part, small_part], "exchange_grads")
    g_small_sum = _unpack(_sum_slots(landed_small, "sum_grads_small"), small_shapes)
    rep_sum = _unpack(_sum_slots(landed_rep, "sum_grads_replicated"), [(1, 1)] + [weights[n].shape for n in replicated])
    loss = rep_sum[0].reshape(())
    grads.update(dict(zip(replicated, rep_sum[1:])))
    grads.update(dict(zip(sharded_small, g_small_sum)))

    small_names = replicated + sharded_small
    packs = [_pack_rows([src[n] for n in small_names]) for src in (weights, grads, m_in, v_in)]
    outs = _adamw(*packs, "adamw_small")
    small_shapes_all = [weights[n].shape for n in small_names]
    for dst, packed in zip((delta, new_m, new_v), outs):
        dst.update(dict(zip(small_names, _unpack(packed, small_shapes_all))))

    return (loss, grad_x[None], *[grads[n] for n in names], *[delta[n] for n in names],
            *[new_m[n] for n in names], *[new_v[n] for n in names])
```

```python
import functools

import jax
import jax.numpy as jnp
from jax import lax
from jax.experimental import pallas as pl
from jax.experimental.pallas import tpu as pltpu

F32 = jnp.float32
BF16 = jnp.bfloat16
SUM_PRECISION = lax.Precision.HIGH
SCAN_PRECISION = None
MESH_ID = pl.DeviceIdType.MESH

N_DEV = 8
D_MODEL = 1024
HG_WIDTH = 512
HG_HEAD_DIM = 128
HG_HEADS = 4
RW_WIDTH = 512
RW_PAIRS = 4
RW_HEAD_DIM = 64
HG_COLS = 2048
RW_COLS = 1792
D_FF = 2816
NORM_EPS = 1e-6
RW_GN_EPS = 64e-5
L2_EPS = 1e-12
ADAM_LR, ADAM_B1, ADAM_B2, ADAM_EPS, ADAM_WD, ADAM_STEP = 0.001, 0.9, 0.999, 1e-08, 0.01, 10

HG_CHUNK = 32
HG_HALF = 16
RW_CHUNK = 64
SCAN_ROWS = 256
LANES = 128

NN = ((1,), (0,))
NT = ((1,), (1,))
TN = ((0,), (0,))


def _dot(a, b, dims=NN, precision=SCAN_PRECISION):
    if precision is None:
        a, b = a.astype(BF16), b.astype(BF16)
    return lax.dot_general(a, b, (dims, ((), ())), precision=precision, preferred_element_type=F32)


def _iota2(shape, dim):
    return lax.broadcasted_iota(jnp.int32, shape, dim)


def _sigmoid(z):
    return 0.5 * jnp.tanh(0.5 * z) + 0.5


def _row_tile(n, want):
    t = min(n, want)
    while n % t:
        t //= 2
    return t


def _rms_fwd(x, w, name):
    T, D = x.shape
    tb = _row_tile(T, 512)

    def body(x_ref, w_ref, h_ref):
        xv = x_ref[...]
        r = lax.rsqrt(jnp.mean(xv * xv, axis=-1, keepdims=True) + NORM_EPS)
        h_ref[...] = (xv * r * w_ref[...]).astype(h_ref.dtype)

    return pl.pallas_call(
        body, name=name, grid=(T // tb,),
        in_specs=[pl.BlockSpec((tb, D), lambda i: (i, 0)), pl.BlockSpec((1, D), lambda i: (0, 0))],
        out_specs=pl.BlockSpec((tb, D), lambda i: (i, 0)),
        out_shape=jax.ShapeDtypeStruct((T, D), BF16),
    )(x, w)


def _mm_nn_rms_bwd(a, b, x, w, dres, name):
    parts = list(a)
    T, D = x.shape
    K = b.shape[0]
    tm = _row_tile(T, 256)
    widths = [p.shape[1] for p in parts]
    n = len(parts)

    def body(*refs):
        b_ref, x_ref, w_ref, dres_ref, dx_ref, dw_ref = refs[n:]

        @pl.when(pl.program_id(0) == 0)
        def _():
            dw_ref[...] = jnp.zeros_like(dw_ref)

        dy, off = None, 0
        for a_ref, wd in zip(refs[:n], widths):
            d = _dot(a_ref[...].astype(BF16), b_ref[off:off + wd, :].astype(BF16), NN, None)
            dy = d if dy is None else dy + d
            off += wd
        xv = x_ref[...]
        r = lax.rsqrt(jnp.mean(xv * xv, axis=-1, keepdims=True) + NORM_EPS)
        xn = xv * r
        dxn = dy * w_ref[...]
        dx_ref[...] = dres_ref[...] + r * (dxn - xn * jnp.mean(dxn * xn, axis=-1, keepdims=True))
        dw_ref[...] += jnp.sum(dy * xn, axis=0, keepdims=True)

    row = pl.BlockSpec((tm, D), lambda i: (i, 0))
    vec = pl.BlockSpec((1, D), lambda i: (0, 0))
    return pl.pallas_call(
        body, name=name, grid=(T // tm,),
        in_specs=[pl.BlockSpec((tm, wd), lambda i: (i, 0)) for wd in widths]
        + [pl.BlockSpec((K, D), lambda i: (0, 0)), row, vec, row],
        out_specs=[row, vec],
        out_shape=[jax.ShapeDtypeStruct((T, D), F32), jax.ShapeDtypeStruct((1, D), F32)],
    )(*parts, b, x, w, dres)


def _mm_nt(a, bt, name, out_dtype=F32):
    T, K = a.shape
    N = bt.shape[0]
    tm = _row_tile(T, 256)

    def body(a_ref, b_ref, o_ref):
        o_ref[...] = _dot(a_ref[...].astype(BF16), b_ref[...].astype(BF16), NT, None).astype(o_ref.dtype)

    return pl.pallas_call(
        body, name=name, grid=(T // tm,),
        in_specs=[pl.BlockSpec((tm, K), lambda i: (i, 0)), pl.BlockSpec((N, K), lambda i: (0, 0))],
        out_specs=pl.BlockSpec((tm, N), lambda i: (i, 0)),
        out_shape=jax.ShapeDtypeStruct((T, N), out_dtype),
    )(a, bt)


def _mm_nn(a, b, res, name, out_dtype=F32):
    parts = list(a) if isinstance(a, (list, tuple)) else [a]
    T = parts[0].shape[0]
    K, N = b.shape
    tm = _row_tile(T, 256)
    widths = [p.shape[1] for p in parts]
    n = len(parts)

    def body(*refs):
        b_ref, o_ref = refs[n], refs[-1]
        acc, off = None, 0
        for a_ref, w in zip(refs[:n], widths):
            d = _dot(a_ref[...].astype(BF16), b_ref[off:off + w, :].astype(BF16), NN, None)
            acc = d if acc is None else acc + d
            off += w
        if res is not None:
            acc = acc + refs[n + 1][...]
        o_ref[...] = acc.astype(o_ref.dtype)

    in_specs = [pl.BlockSpec((tm, w), lambda i: (i, 0)) for w in widths] + [pl.BlockSpec((K, N), lambda i: (0, 0))]
    args = parts + [b]
    if res is not None:
        in_specs.append(pl.BlockSpec((tm, N), lambda i: (i, 0)))
        args.append(res)
    return pl.pallas_call(
        body, name=name, grid=(T // tm,), in_specs=in_specs,
        out_specs=pl.BlockSpec((tm, N), lambda i: (i, 0)),
        out_shape=jax.ShapeDtypeStruct((T, N), out_dtype),
    )(*args)


def _mm_tn(a, b, tmm, name, out_dtype=F32):
    T, M = a.shape
    N = b.shape[1]
    tk = _row_tile(T, 512)
    nk = T // tk

    def body(a_ref, b_ref, o_ref, acc_ref):
        @pl.when(pl.program_id(1) == 0)
        def _():
            acc_ref[...] = jnp.zeros_like(acc_ref)

        acc_ref[...] += _dot(a_ref[...].astype(BF16), b_ref[...].astype(BF16), TN, None)

        @pl.when(pl.program_id(1) == nk - 1)
        def _():
            o_ref[...] = acc_ref[...].astype(o_ref.dtype)

    return pl.pallas_call(
        body, name=name, grid=(M // tmm, nk),
        in_specs=[pl.BlockSpec((tk, tmm), lambda m, k: (k, m)), pl.BlockSpec((tk, N), lambda m, k: (k, 0))],
        out_specs=pl.BlockSpec((tmm, N), lambda m, k: (m, 0)),
        out_shape=jax.ShapeDtypeStruct((M, N), out_dtype),
        scratch_shapes=[pltpu.VMEM((tmm, N), F32)],
    )(a, b)


class _RowShifts:
    def __init__(self, shape):
        index = _iota2(shape, 0)
        self.rows = shape[0]
        self.first = {n: index < n for n in (1, 2)}
        self.last = {n: index >= shape[0] - n for n in (1, 2)}

    def down(self, z, n):
        return jnp.where(self.first[n], 0.0, pltpu.roll(z, n, 0))

    def up(self, z, n):
        return jnp.where(self.last[n], 0.0, pltpu.roll(z, self.rows - n, 0))


def _shift_fwd(proj, mu, name):
    T = proj.shape[0]
    nblk = RW_COLS // LANES
    first = HG_COLS // LANES

    def body(p_ref, mu_ref, o_ref):
        p = p_ref[...]
        o_ref[...] = p + (_RowShifts(p.shape).down(p, 1) - p) * mu_ref[...]

    return pl.pallas_call(
        body, name=name, grid=(nblk,),
        in_specs=[pl.BlockSpec((T, LANES), lambda j: (0, first + j)), pl.BlockSpec((1, LANES), lambda j: (0, j))],
        out_specs=pl.BlockSpec((T, LANES), lambda j: (0, j)),
        out_shape=jax.ShapeDtypeStruct((T, RW_COLS), F32),
    )(proj, mu)


def _shift_bwd(ds, proj, mu, col0, name):
    T, width = ds.shape
    nblk = width // LANES
    first = (HG_COLS + col0) // LANES
    mu0 = col0 // LANES

    def body(ds_ref, p_ref, mu_ref, dp_ref, dmu_ref):
        dsv = ds_ref[...]
        p = p_ref[...]
        m = mu_ref[...]
        shifts = _RowShifts(p.shape)
        dp_ref[...] = (dsv * (1.0 - m) + shifts.up(dsv * m, 1)).astype(dp_ref.dtype)
        dmu_ref[...] = jnp.sum(dsv * (shifts.down(p, 1) - p), axis=0, keepdims=True)

    return pl.pallas_call(
        body, name=name, grid=(nblk,),
        in_specs=[pl.BlockSpec((T, LANES), lambda j: (0, j)),
                  pl.BlockSpec((T, LANES), lambda j: (0, first + j)),
                  pl.BlockSpec((1, LANES), lambda j: (0, mu0 + j))],
        out_specs=[pl.BlockSpec((T, LANES), lambda j: (0, j)), pl.BlockSpec((1, LANES), lambda j: (0, j))],
        out_shape=[jax.ShapeDtypeStruct((T, width), BF16), jax.ShapeDtypeStruct((1, width), F32)],
    )(ds, proj, mu)


def _conv3(z, w_ref, shifts):
    return w_ref[0:1, :] * shifts.down(z, 2) + w_ref[1:2, :] * shifts.down(z, 1) + w_ref[2:3, :] * z


def _ffn_act_fwd(u, conv_w, conv_b, name):
    T = u.shape[0]
    nblk = D_FF // LANES

    def body(ug_ref, uv_ref, wg_ref, wv_ref, bg_ref, bv_ref, act_ref):
        shifts = _RowShifts((T, LANES))
        gate = _conv3(ug_ref[...], wg_ref, shifts) + bg_ref[...]
        val = _conv3(uv_ref[...], wv_ref, shifts) + bv_ref[...]
        act_ref[...] = (gate * _sigmoid(gate) * val).astype(act_ref.dtype)

    col = lambda off: pl.BlockSpec((T, LANES), lambda j: (0, off + j))
    wsp = lambda off: pl.BlockSpec((3, LANES), lambda j: (0, off + j))
    bsp = lambda off: pl.BlockSpec((1, LANES), lambda j: (0, off + j))
    return pl.pallas_call(
        body, name=name, grid=(nblk,),
        in_specs=[col(0), col(nblk), wsp(0), wsp(nblk), bsp(0), bsp(nblk)],
        out_specs=pl.BlockSpec((T, LANES), lambda j: (0, j)),
        out_shape=jax.ShapeDtypeStruct((T, D_FF), BF16),
    )(u, u, conv_w, conv_w, conv_b, conv_b)


def _ffn_act_bwd(u, dact, conv_w, conv_b, name):
    T = u.shape[0]
    nblk = D_FF // LANES

    def conv_bwd(z, dzc, w_ref, du_ref, dw_ref, db_ref, shifts):
        up1, up2 = shifts.up(dzc, 1), shifts.up(dzc, 2)
        du = w_ref[2:3, :] * dzc + w_ref[1:2, :] * up1 + w_ref[0:1, :] * up2
        du_ref[...] = du.astype(du_ref.dtype)
        dw_ref[0:1, :] = jnp.sum(up2 * z, axis=0, keepdims=True)
        dw_ref[1:2, :] = jnp.sum(up1 * z, axis=0, keepdims=True)
        dw_ref[2:3, :] = jnp.sum(dzc * z, axis=0, keepdims=True)
        db_ref[...] = jnp.sum(dzc, axis=0, keepdims=True)

    def body(ug_ref, uv_ref, da_ref, wg_ref, wv_ref, bg_ref, bv_ref,
             dug_ref, duv_ref, dwg_ref, dwv_ref, dbg_ref, dbv_ref):
        ug, uv = ug_ref[...], uv_ref[...]
        shifts = _RowShifts((T, LANES))
        gate = _conv3(ug, wg_ref, shifts) + bg_ref[...]
        val = _conv3(uv, wv_ref, shifts) + bv_ref[...]
        da = da_ref[...].astype(F32)
        sg = _sigmoid(gate)
        dgate = da * val * (sg * (1.0 + gate * (1.0 - sg)))
        dval = da * gate * sg
        conv_bwd(ug, dgate, wg_ref, dug_ref, dwg_ref, dbg_ref, shifts)
        conv_bwd(uv, dval, wv_ref, duv_ref, dwv_ref, dbv_ref, shifts)

    col = lambda off: pl.BlockSpec((T, LANES), lambda j: (0, off + j))
    wsp = lambda off: pl.BlockSpec((3, LANES), lambda j: (0, off + j))
    bsp = lambda off: pl.BlockSpec((1, LANES), lambda j: (0, off + j))
    half = lambda r, dt: jax.ShapeDtypeStruct((r, D_FF), dt)
    return pl.pallas_call(
        body, name=name, grid=(nblk,),
        in_specs=[col(0), col(nblk), col(0), wsp(0), wsp(nblk), bsp(0), bsp(nblk)],
        out_specs=[col(0), col(0), wsp(0), wsp(0), bsp(0), bsp(0)],
        out_shape=[half(T, BF16), half(T, BF16), half(3, F32), half(3, F32), half(1, F32), half(1, F32)],
    )(u, u, dact, conv_w, conv_w, conv_b, conv_b)


def _loss_head(x2, w, target, name):
    T, D = x2.shape
    tb = _row_tile(T, 256)

    def body(x_ref, w_ref, t_ref, loss_ref, dx_ref, dw_ref):
        @pl.when(pl.program_id(0) == 0)
        def _():
            loss_ref[...] = jnp.zeros_like(loss_ref)
            dw_ref[...] = jnp.zeros_like(dw_ref)

        xv = x_ref[...]
        r = lax.rsqrt(jnp.mean(xv * xv, axis=-1, keepdims=True) + NORM_EPS)
        xn = xv * r
        err = xn * w_ref[...] - t_ref[...]
        row_loss = jnp.sum(err * err, axis=-1, keepdims=True) * (0.5 / D)
        loss_ref[...] += jnp.sum(row_loss, axis=0, keepdims=True)
        dy = err * (1.0 / D)
        dxn = dy * w_ref[...]
        dx_ref[...] = r * (dxn - xn * jnp.mean(dxn * xn, axis=-1, keepdims=True))
        dw_ref[...] += jnp.sum(dy * xn, axis=0, keepdims=True)

    row = pl.BlockSpec((tb, D), lambda i: (i, 0))
    vec = pl.BlockSpec((1, D), lambda i: (0, 0))
    return pl.pallas_call(
        body, name=name, grid=(T // tb,),
        in_specs=[row, vec, row],
        out_specs=[pl.BlockSpec((1, 1), lambda i: (0, 0)), row, vec],
        out_shape=[jax.ShapeDtypeStruct((1, 1), F32), jax.ShapeDtypeStruct((T, D), F32),
                   jax.ShapeDtypeStruct((1, D), F32)],
    )(x2, w, target)


def _adamw(w, g, m, v, name):
    R, C = w.shape
    tb = _row_tile(R, 256) if R % 8 == 0 else R

    def body(w_ref, g_ref, m_ref, v_ref, d_ref, nm_ref, nv_ref):
        gv = g_ref[...]
        nm = ADAM_B1 * m_ref[...] + (1.0 - ADAM_B1) * gv
        nv = ADAM_B2 * v_ref[...] + (1.0 - ADAM_B2) * (gv * gv)
        m_hat = nm / (1.0 - ADAM_B1 ** ADAM_STEP)
        v_hat = nv / (1.0 - ADAM_B2 ** ADAM_STEP)
        d_ref[...] = -ADAM_LR * (m_hat / (jnp.sqrt(v_hat) + ADAM_EPS) + ADAM_WD * w_ref[...])
        nm_ref[...] = nm
        nv_ref[...] = nv

    blk = pl.BlockSpec((tb, C), lambda i: (i, 0))
    sd = jax.ShapeDtypeStruct((R, C), F32)
    return pl.pallas_call(
        body, name=name, grid=(R // tb,), in_specs=[blk] * 4, out_specs=[blk] * 3, out_shape=[sd] * 3,
    )(w, g, m, v)


def _chunk_masks(rows, chunk):
    shift = chunk.bit_length() - 1
    i, j = _iota2((rows, rows), 0), _iota2((rows, rows), 1)
    same = jnp.right_shift(i, shift) == jnp.right_shift(j, shift)
    return same.astype(F32), (same & (j <= i)).astype(F32), (same & (j < i)).astype(F32)


def _head_lanes(h):
    return slice(h * LANES, (h + 1) * LANES)


def _chunk_rows(c, chunk):
    return pl.ds(pl.multiple_of(c * chunk, chunk), chunk)


def _hg_consts(rows):
    same, tril, _ = _chunk_masks(rows, HG_CHUNK)
    half_same, half_tril, _ = _chunk_masks(rows, HG_HALF)
    i, j = _iota2((rows, rows), 0), _iota2((rows, rows), 1)
    half_shift, shift = HG_HALF.bit_length() - 1, HG_CHUNK.bit_length() - 1
    mid_row = jnp.left_shift(jnp.right_shift(i, half_shift), half_shift) + (HG_HALF // 2 - 1)
    bound_row = jnp.left_shift(jnp.right_shift(i, shift), shift) + (HG_HALF - 1)
    upto_mid = ((same > 0) & (j <= mid_row)).astype(F32)
    upto_bound = ((same > 0) & (j <= bound_row)).astype(F32)
    lower_left = tril * (1.0 - half_same)
    return jnp.concatenate([tril, same, upto_mid, upto_bound], axis=0), half_tril, lower_left


N_HG_IN = 5


def _hg_prep(consts, *flat):
    sums, half_tril, lower_left = consts
    rows = half_tril.shape[0]
    heads, logs = [], []
    for h in range(len(flat) // N_HG_IN):
        qr, fr, ir, l0, l1 = flat[N_HG_IN * h:N_HG_IN * (h + 1)]
        lb = _sigmoid(l0 - l1)
        f = lb + (1.0 - lb) * _sigmoid(fr)
        heads.append((qr * _sigmoid(qr) * (HG_HEAD_DIM ** -0.5), 1.0 - f, ir))
        logs.append(jnp.log(f))
    acc = _dot(sums, jnp.concatenate(logs, axis=1), NN, SUM_PRECISION)
    sums_of = []
    for h in range(len(heads)):
        acc_h = acc[:, h * LANES:(h + 1) * LANES]
        sums_of.append(tuple(acc_h[n * rows:(n + 1) * rows] for n in range(4)))
    near = [_dot(q * jnp.exp(a - mid), k * jnp.exp(mid - a), NT) * half_tril
            for (q, k, _), (a, _, mid, _) in zip(heads, sums_of)]
    far = [_dot(q * jnp.exp(jnp.minimum(a - bound, 0.0)), k * jnp.exp(jnp.minimum(bound - a, 0.0)), NT) * lower_left
           for (q, k, _), (a, _, _, bound) in zip(heads, sums_of)]
    intra = [_dot(n + f, ir) for n, f, (_, _, ir) in zip(near, far, heads)]
    return tuple((q * jnp.exp(a), o_intra, k * jnp.exp(tot - a), jnp.exp(tot))
                 for (q, k, _), (a, tot, _, _), o_intra in zip(heads, sums_of, intra))


def _hg_prep_args(q_ref, f_ref, i_ref, l0_ref, l1_ref):
    flat = []
    for h in range(HG_HEADS):
        ln = _head_lanes(h)
        flat += [q_ref[:, ln], f_ref[:, ln], i_ref[:, ln], l0_ref[:, ln], l1_ref[:, ln]]
    return flat


def _hg_post(o, gr, nw):
    on = o * lax.rsqrt(jnp.mean(o * o, axis=-1, keepdims=True) + NORM_EPS)
    return on * nw * (gr * _sigmoid(gr))


def _hg_specs(T, tb, rev):
    nT = T // tb
    tix = (lambda t: nT - 1 - t) if rev else (lambda t: t)
    col = lambda blk: pl.BlockSpec((tb, HG_WIDTH), lambda t: (tix(t), blk))
    vec = pl.BlockSpec((1, HG_WIDTH), lambda t: (0, 0))
    st = pl.BlockSpec((HG_HEADS, tb // HG_CHUNK, HG_HEAD_DIM, HG_HEAD_DIM), lambda t: (0, tix(t), 0, 0))
    return nT, col, vec, st


def _hg_fwd(proj, l0, l1, nw, name):
    T = proj.shape[0]
    tb = _row_tile(T, SCAN_ROWS)
    nsub = tb // HG_CHUNK
    nT, col, vec, st = _hg_specs(T, tb, False)

    def body(q_ref, f_ref, i_ref, g_ref, l0_ref, l1_ref, nw_ref, o_ref, st_ref, s_ref, qe_ref, kd_ref, dec_ref):
        @pl.when(pl.program_id(0) == 0)
        def _():
            s_ref[...] = jnp.zeros_like(s_ref)

        consts = _hg_consts(tb)
        outs = _hg_prep(consts, *_hg_prep_args(q_ref, f_ref, i_ref, l0_ref, l1_ref))
        for h, (qe, o_intra, kd, dec) in enumerate(outs):
            qe_ref[h], kd_ref[h], dec_ref[h] = qe, kd, dec
            o_ref[:, _head_lanes(h)] = o_intra

        def step(c, carry):
            rows = _chunk_rows(c, HG_CHUNK)
            heads = range(HG_HEADS)
            states = [s_ref[h] for h in heads]
            inter = [_dot(qe_ref[h, rows, :], states[h], NT) for h in heads]
            updates = [_dot(i_ref[rows, _head_lanes(h)], kd_ref[h, rows, :], TN) for h in heads]
            for h in heads:
                st_ref[h, c] = states[h]
                o_ref[rows, _head_lanes(h)] += inter[h]
                s_ref[h] = states[h] * dec_ref[h, pl.ds(c * HG_CHUNK, 1), :] + updates[h]
            return carry

        lax.fori_loop(0, nsub, step, 0)
        for h in range(HG_HEADS):
            ln = _head_lanes(h)
            o_ref[:, ln] = _hg_post(o_ref[:, ln], g_ref[:, ln], nw_ref[:, ln])

    blk = pltpu.VMEM((HG_HEADS, tb, LANES), F32)
    return pl.pallas_call(
        body, name=name, grid=(nT,),
        in_specs=[col(0), col(1), col(2), col(3), vec, vec, vec],
        out_specs=[col(0), st],
        out_shape=[jax.ShapeDtypeStruct((T, HG_WIDTH), F32),
                   jax.ShapeDtypeStruct((HG_HEADS, T // HG_CHUNK, HG_HEAD_DIM, HG_HEAD_DIM), F32)],
        scratch_shapes=[pltpu.VMEM((HG_HEADS, HG_HEAD_DIM, HG_HEAD_DIM), F32), blk, blk, blk],
    )(proj, proj, proj, proj, l0, l1, nw)


def _hg_bwd(proj, states, do, do_blk, l0, l1, nw, name):
    T = proj.shape[0]
    tb = _row_tile(T, SCAN_ROWS)
    nsub = tb // HG_CHUNK
    nT, col, vec, st = _hg_specs(T, tb, True)

    def body(q_ref, f_ref, i_ref, g_ref, st_ref, do_ref, l0_ref, l1_ref, nw_ref,
             dq_ref, df_ref, di_ref, dg_ref, dl0_ref, dl1_ref, dnw_ref,
             ds_ref, qe_ref, kd_ref, dec_ref, o_ref, dqe_ref, dkd_ref, ddec_ref, dis_ref):
        @pl.when(pl.program_id(0) == 0)
        def _():
            ds_ref[...] = jnp.zeros_like(ds_ref)
            dl0_ref[...] = jnp.zeros_like(dl0_ref)
            dl1_ref[...] = jnp.zeros_like(dl1_ref)
            dnw_ref[...] = jnp.zeros_like(dnw_ref)

        consts = _hg_consts(tb)
        outs, prep_vjp = jax.vjp(functools.partial(_hg_prep, consts),
                                 *_hg_prep_args(q_ref, f_ref, i_ref, l0_ref, l1_ref))
        for h, (qe, o_intra, kd, dec) in enumerate(outs):
            qe_ref[h], kd_ref[h], dec_ref[h], o_ref[h] = qe, kd, dec, o_intra

        def redo(c, carry):
            rows = _chunk_rows(c, HG_CHUNK)
            inter = [_dot(qe_ref[h, rows, :], st_ref[h, c], NT) for h in range(HG_HEADS)]
            for h in range(HG_HEADS):
                o_ref[h, rows, :] += inter[h]
            return carry

        lax.fori_loop(0, nsub, redo, 0)
        for h in range(HG_HEADS):
            ln = _head_lanes(h)
            _, vjp = jax.vjp(_hg_post, o_ref[h], g_ref[:, ln], nw_ref[:, ln])
            d_o, dgr, dnw = vjp(do_ref[:, ln])
            o_ref[h] = d_o
            dg_ref[:, ln] = dgr.astype(dg_ref.dtype)
            dnw_ref[:, ln] += dnw
        ddec_ref[...] = jnp.zeros_like(ddec_ref)

        def step(i, carry):
            c = nsub - 1 - i
            rows = _chunk_rows(c, HG_CHUNK)
            row0 = pl.ds(c * HG_CHUNK, 1)
            heads = range(HG_HEADS)
            Gs = [ds_ref[h] for h in heads]
            Ss = [st_ref[h, c] for h in heads]
            d_os = [o_ref[h, rows, :] for h in heads]
            dqe = [_dot(d_os[h], Ss[h]) for h in heads]
            dkd = [_dot(i_ref[rows, _head_lanes(h)], Gs[h]) for h in heads]
            dis = [_dot(kd_ref[h, rows, :], Gs[h], NT) for h in heads]
            back = [_dot(d_os[h], qe_ref[h, rows, :], TN) for h in heads]
            for h in heads:
                dqe_ref[h, rows, :] = dqe[h]
                dkd_ref[h, rows, :] = dkd[h]
                dis_ref[h, rows, :] = dis[h]
                ddec_ref[h, row0, :] = jnp.sum(Ss[h] * Gs[h], axis=0, keepdims=True)
                ds_ref[h] = Gs[h] * dec_ref[h, row0, :] + back[h]
            return carry

        lax.fori_loop(0, nsub, step, 0)
        grads = prep_vjp(tuple((dqe_ref[h], o_ref[h], dkd_ref[h], ddec_ref[h]) for h in range(HG_HEADS)))
        for h in range(HG_HEADS):
            ln = _head_lanes(h)
            dq, df, di, dl0, dl1 = grads[N_HG_IN * h:N_HG_IN * (h + 1)]
            dq_ref[:, ln] = dq.astype(dq_ref.dtype)
            df_ref[:, ln] = df.astype(df_ref.dtype)
            di_ref[:, ln] = (di + dis_ref[h]).astype(di_ref.dtype)
            dl0_ref[:, ln] += dl0
            dl1_ref[:, ln] += dl1

    dcol = jax.ShapeDtypeStruct((T, HG_WIDTH), BF16)
    dvec = jax.ShapeDtypeStruct((1, HG_WIDTH), F32)
    blk = pltpu.VMEM((HG_HEADS, tb, LANES), F32)
    return pl.pallas_call(
        body, name=name, grid=(nT,),
        in_specs=[col(0), col(1), col(2), col(3), st, col(do_blk), vec, vec, vec],
        out_specs=[col(0)] * 4 + [vec] * 3,
        out_shape=[dcol] * 4 + [dvec] * 3,
        scratch_shapes=[pltpu.VMEM((HG_HEADS, HG_HEAD_DIM, HG_HEAD_DIM), F32)] + [blk] * 8,
    )(proj, proj, proj, proj, states, do, l0, l1, nw)


def _rw_consts(rows):
    same, tril, stril = _chunk_masks(rows, RW_CHUNK)
    br, bc = _iota2((LANES, LANES), 0), _iota2((LANES, LANES), 1)
    blockdiag = ((br < RW_HEAD_DIM) == (bc < RW_HEAD_DIM)).astype(F32)
    m0 = (_iota2((1, LANES), 1) < RW_HEAD_DIM).astype(F32)
    return same, tril, stril, blockdiag, m0, 1.0 - m0


def _unit_lower_inverses_impl(lows):
    rows = lows[0].shape[0]
    eye = (_iota2(lows[0].shape, 0) == _iota2(lows[0].shape, 1)).astype(F32)
    xs = [low + eye for low in lows]
    ps = [_dot(low, low) for low in lows]
    n = 4
    while n < RW_CHUNK:
        zs = [_dot(jnp.concatenate([p, x], axis=0), p) for p, x in zip(ps, xs)]
        ps = [z[:rows] for z in zs]
        xs = [x + z[rows:] for x, z in zip(xs, zs)]
        n *= 2
    return tuple(x + _dot(x, p) for x, p in zip(xs, ps))


@jax.custom_vjp
def _unit_lower_inverses(lows):
    return _unit_lower_inverses_impl(lows)


def _unit_lower_inverses_fwd(lows):
    xs = _unit_lower_inverses_impl(lows)
    return xs, xs


def _unit_lower_inverses_bwd(xs, dxs):
    ts = [_dot(x, dx, TN) for x, dx in zip(xs, dxs)]
    return (tuple(_dot(t, x, NT) for t, x in zip(ts, xs)),)


_unit_lower_inverses.defvjp(_unit_lower_inverses_fwd, _unit_lower_inverses_bwd)


N_PREP_IN = 12
RW_GROUP = 2


def _rw_prep(consts, *flat):
    same, tril, stril, blockdiag, m0, m1 = consts
    rows = tril.shape[0]
    masks = (m0, m1)
    pairs = [flat[N_PREP_IN * i:N_PREP_IN * (i + 1)] for i in range(len(flat) // N_PREP_IN)]
    lora = [(_dot(jnp.tanh(lw), w2p), _dot(lw, a2p), _dot(_sigmoid(gd), g2), _dot(jnp.square(kx * k_k), blockdiag))
            for _, kx, _, lw, gd, _, _, k_k, _, w2p, a2p, g2 in pairs]
    mid = []
    for (r, kx, v, lw, gd, w0, a0, k_k, k_a, w2p, a2p, g2), (xw, xa, g, kk_sq) in zip(pairs, lora):
        xw = w0 + xw
        w = jnp.minimum(xw, 0.0) - jnp.log(1.0 + jnp.exp(-jnp.abs(xw))) - 0.5
        a_s = _sigmoid(a0 + xa)
        kk = kx * k_k / jnp.maximum(jnp.sqrt(kk_sq), L2_EPS)
        mid.append((-jnp.exp(w), a_s, kk, kx * (1.0 + (a_s - 1.0) * k_a), g))
    accs = [_dot(jnp.concatenate([tril, same], axis=0), ld, NN, SUM_PRECISION) for ld, _, _, _, _ in mid]
    pre = []
    for (r, _, v, *_), (ld, a_s, kk, k2, g), acc in zip(pairs, mid, accs):
        bv = kk * a_s
        cum, tot = acc[:rows], acc[rows:]
        ecn = jnp.exp(-cum)
        a_t = -kk * jnp.exp(cum - ld)
        r_t = r * jnp.exp(cum)
        rem = jnp.exp(tot - cum)
        pre.append((v, a_t, r_t, (bv * ecn, k2 * ecn), (bv * rem, k2 * rem, jnp.exp(tot), k2, g)))
    zs = [_dot(jnp.concatenate([a_t * m0, a_t * m1, r_t * m0, r_t * m1], axis=0), jnp.concatenate(bk, axis=0), NT)
          for _, a_t, r_t, bk, _ in pre]
    pre = [(v, a_t, r_t, z, out) for (v, a_t, r_t, _, out), z in zip(pre, zs)]
    heads = [(i, h) for i in range(len(pre)) for h in range(2)]
    za = {ih: pre[ih[0]][3][ih[1] * rows:(ih[1] + 1) * rows] for ih in heads}
    zr = {ih: pre[ih[0]][3][(2 + ih[1]) * rows:(3 + ih[1]) * rows] for ih in heads}
    tinv = dict(zip(heads, _unit_lower_inverses(tuple(za[ih][:, :rows] * stril for ih in heads))))
    lv = {ih: _dot(jnp.concatenate([za[ih][:, rows:] * stril, zr[ih][:, rows:] * tril], axis=0), pre[ih[0]][0])
          for ih in heads}
    wu = {ih: _dot(tinv[ih], jnp.concatenate([pre[ih[0]][1] * masks[ih[1]], lv[ih][:rows]], axis=1)) for ih in heads}
    w_m = {ih: wu[ih][:, :LANES] for ih in heads}
    u_m = {ih: masks[ih[1]] * wu[ih][:, LANES:] for ih in heads}
    qy = {ih: _dot(zr[ih][:, :rows] * tril, jnp.concatenate([w_m[ih], u_m[ih]], axis=1)) for ih in heads}
    outs = []
    for i in range(len(pre)):
        a, b = (i, 0), (i, 1)
        W = w_m[a] + w_m[b]
        U = u_m[a] + u_m[b]
        Q = pre[i][2] + qy[a][:, :LANES] + qy[b][:, :LANES]
        Y0 = qy[a][:, LANES:] + qy[b][:, LANES:] + m0 * lv[a][rows:] + m1 * lv[b][rows:]
        outs.append((W, U, Q, Y0) + pre[i][4])
    return tuple(outs)


N_POST_IN = 8


def _rw_post(blockdiag, *flat):
    inv_n = 1.0 / RW_HEAD_DIM
    pairs = [flat[N_POST_IN * i:N_POST_IN * (i + 1)] for i in range(len(flat) // N_POST_IN)]
    sums = [(_dot(y, blockdiag), _dot(r * k2 * r_k, blockdiag)) for y, r, _, k2, _, r_k, _, _ in pairs]
    centred = [p[0] - s[0] * inv_n for p, s in zip(pairs, sums)]
    variances = [_dot(yc * yc, blockdiag) * inv_n for yc in centred]
    return tuple((yc * lax.rsqrt(var + RW_GN_EPS) * ln_w + ln_b + s[1] * v) * g
                 for (_, _, v, _, g, _, ln_w, ln_b), s, yc, var in zip(pairs, sums, centred, variances))


N_RW_VEC = 7
N_RW_MAT = 3


def _rw_specs(T, tb, rev):
    nT = T // tb
    tix = (lambda t: nT - 1 - t) if rev else (lambda t: t)
    wide = lambda blk: pl.BlockSpec((tb, RW_WIDTH), lambda t: (tix(t), blk))
    narrow = lambda blk: pl.BlockSpec((tb, LANES), lambda t: (tix(t), blk))
    vec = pl.BlockSpec((1, RW_WIDTH), lambda t: (0, 0))
    mat = pl.BlockSpec((RW_PAIRS, LANES, LANES), lambda t: (0, 0, 0))
    st = pl.BlockSpec((RW_PAIRS, tb // RW_CHUNK, LANES, LANES), lambda t: (0, tix(t), 0, 0))
    lora0 = 3 * RW_WIDTH // LANES
    ins = [wide(0), wide(1), wide(2), narrow(lora0), narrow(lora0 + 1)]
    return nT, wide, vec, mat, st, ins


def _rw_prep_args(p, r_ref, k_ref, v_ref, lw_ref, gd_ref, vrefs, mrefs):
    ln = _head_lanes(p)
    w0, a0, k_k, k_a = [x[:, ln] for x in vrefs[:4]]
    return (r_ref[:, ln], k_ref[:, ln], v_ref[:, ln], lw_ref[...], gd_ref[...], w0, a0, k_k, k_a,
            *[x[p] for x in mrefs])


def _stack_chunks(ref, top, bottom):
    C = RW_CHUNK
    for c in range(ref.shape[0]):
        ref[c, 0:C, :] = top[c * C:(c + 1) * C]
        ref[c, C:2 * C, :] = bottom[c * C:(c + 1) * C]


def _group_args(p0, r_ref, k_ref, v_ref, lw_ref, gd_ref, vrefs, mrefs):
    flat = []
    for p in range(p0, p0 + RW_GROUP):
        flat += list(_rw_prep_args(p, r_ref, k_ref, v_ref, lw_ref, gd_ref, vrefs, mrefs))
    return flat


def _rw_fwd(rws, vecs, mats, name):
    T = rws.shape[0]
    tb = _row_tile(T, SCAN_ROWS)
    nsub = tb // RW_CHUNK
    C = RW_CHUNK
    nT, wide, vec, mat, st, ins = _rw_specs(T, tb, False)

    def body(*refs):
        r_ref, k_ref, v_ref, lw_ref, gd_ref = refs[:5]
        vrefs = refs[5:5 + N_RW_VEC]
        mrefs = refs[5 + N_RW_VEC:5 + N_RW_VEC + N_RW_MAT]
        o_ref, st_ref, s_ref, wq_ref, uy_ref, bk_ref, misc_ref, y_ref = refs[-8:]

        @pl.when(pl.program_id(0) == 0)
        def _():
            s_ref[...] = jnp.zeros_like(s_ref)

        consts = _rw_consts(tb)
        blockdiag = consts[3]
        for p0 in range(0, RW_PAIRS, RW_GROUP):
            outs = _rw_prep(consts, *_group_args(p0, r_ref, k_ref, v_ref, lw_ref, gd_ref, vrefs, mrefs))
            for p, (W, U, Q, Y0, Bg, Kg, dec, k2, g) in zip(range(p0, p0 + RW_GROUP), outs):
                _stack_chunks(wq_ref.at[p], W, Q)
                _stack_chunks(uy_ref.at[p], U, Y0)
                _stack_chunks(bk_ref.at[p], Bg, Kg)
                misc_ref[0, p], misc_ref[1, p], misc_ref[2, p] = dec, k2, g

        def step(c, carry):
            rows = _chunk_rows(c, C)
            states = [s_ref[p] for p in range(RW_PAIRS)]
            for p, S in enumerate(states):
                st_ref[p, c] = S
            pys = [_dot(wq_ref[p, c], S, NT) + uy_ref[p, c] for p, S in enumerate(states)]
            pvs = [jnp.concatenate([py[:C], v_ref[rows, _head_lanes(p)]], axis=0) for p, py in enumerate(pys)]
            updates = [_dot(pv, bk_ref[p, c], TN) for p, pv in enumerate(pvs)]
            for p, S in enumerate(states):
                y_ref[p, rows, :] = pys[p][C:]
                s_ref[p] = (S * misc_ref[0, p, pl.ds(c * C, 1), :] + updates[p]) * blockdiag
            return carry

        lax.fori_loop(0, nsub, step, 0)
        flat = []
        for p in range(RW_PAIRS):
            ln = _head_lanes(p)
            flat += [y_ref[p], r_ref[:, ln], v_ref[:, ln], misc_ref[1, p], misc_ref[2, p]] + [x[:, ln] for x in vrefs[4:]]
        for p, out in enumerate(_rw_post(blockdiag, *flat)):
            o_ref[:, _head_lanes(p)] = out

    stacked = pltpu.VMEM((RW_PAIRS, nsub, 2 * C, LANES), F32)
    return pl.pallas_call(
        body, name=name, grid=(nT,),
        in_specs=ins + [vec] * N_RW_VEC + [mat] * N_RW_MAT,
        out_specs=[wide(0), st],
        out_shape=[jax.ShapeDtypeStruct((T, RW_WIDTH), F32),
                   jax.ShapeDtypeStruct((RW_PAIRS, T // RW_CHUNK, LANES, LANES), F32)],
        scratch_shapes=[pltpu.VMEM((RW_PAIRS, LANES, LANES), F32), stacked, stacked, stacked,
                        pltpu.VMEM((3, RW_PAIRS, tb, LANES), F32), pltpu.VMEM((RW_PAIRS, tb, LANES), F32)],
    )(rws, rws, rws, rws, rws, *vecs, *mats)


def _rw_bwd(rws, states, do, do_blk, vecs, mats, name):
    T = rws.shape[0]
    tb = _row_tile(T, SCAN_ROWS)
    nsub = tb // RW_CHUNK
    C = RW_CHUNK
    G = RW_GROUP
    nT, wide, vec, mat, st, ins = _rw_specs(T, tb, True)
    nin = 5 + 1 + 1 + N_RW_VEC + N_RW_MAT

    def body(*refs):
        r_ref, k_ref, v_ref, lw_ref, gd_ref = refs[:5]
        st_ref, do_ref = refs[5], refs[6]
        vrefs = refs[7:7 + N_RW_VEC]
        mrefs = refs[7 + N_RW_VEC:nin]
        dr_ref, dk_ref, dv_ref, dlo_ref = refs[nin:nin + 4]
        dvec = refs[nin + 4:nin + 4 + N_RW_VEC]
        dmat = refs[nin + 4 + N_RW_VEC:nin + 4 + N_RW_VEC + N_RW_MAT]
        ds_ref, wq_ref, uy_ref, bk_ref, pv_ref, dec_ref, y_ref, dpre_ref, dvs_ref = refs[-9:]

        @pl.when(pl.program_id(0) == 0)
        def _():
            ds_ref[...] = jnp.zeros_like(ds_ref)
            for x in dvec + dmat:
                x[...] = jnp.zeros_like(x)

        consts = _rw_consts(tb)
        blockdiag = consts[3]
        dlw, dgd = 0.0, 0.0
        for p0 in range(0, RW_PAIRS, G):
            outs, prep_vjp = jax.vjp(functools.partial(_rw_prep, consts),
                                     *_group_args(p0, r_ref, k_ref, v_ref, lw_ref, gd_ref, vrefs, mrefs))
            for q, (W, U, Q, Y0, Bg, Kg, dec, _, _) in enumerate(outs):
                _stack_chunks(wq_ref.at[q], W, Q)
                _stack_chunks(uy_ref.at[q], U, Y0)
                _stack_chunks(bk_ref.at[q], Bg, Kg)
                dec_ref[q] = dec

            def redo(c, carry, p0=p0):
                rows = _chunk_rows(c, C)
                pys = [_dot(wq_ref[q, c], st_ref[p0 + q, c], NT) + uy_ref[q, c] for q in range(G)]
                for q, py in enumerate(pys):
                    y_ref[q, rows, :] = py[C:]
                    pv_ref[q, c, 0:C, :] = py[:C]
                    pv_ref[q, c, C:2 * C, :] = v_ref[rows, _head_lanes(p0 + q)]
                return carry

            lax.fori_loop(0, nsub, redo, 0)
            flat = []
            for q in range(G):
                ln = _head_lanes(p0 + q)
                flat += [y_ref[q], r_ref[:, ln], v_ref[:, ln], outs[q][7], outs[q][8]] + [x[:, ln] for x in vrefs[4:]]
            _, post_vjp = jax.vjp(functools.partial(_rw_post, blockdiag), *flat)
            post_grads = post_vjp(tuple(do_ref[:, _head_lanes(p0 + q)] for q in range(G)))
            post = []
            for q in range(G):
                ln = _head_lanes(p0 + q)
                dy, dr2, dv2, dk2, dg, dr_k, dln_w, dln_b = post_grads[N_POST_IN * q:N_POST_IN * (q + 1)]
                dpre_ref[q, 3] = dy
                dvs_ref[q] = dv2
                for x, gx in zip(dvec[4:], (dr_k, dln_w, dln_b)):
                    x[:, ln] += gx
                dpre_ref[q, 6] = jnp.zeros_like(dpre_ref[q, 6])
                post.append((dr2, dk2, dg))

            def step(i, carry, p0=p0):
                c = nsub - 1 - i
                rows = _chunk_rows(c, C)
                row0 = pl.ds(c * C, 1)
                qs = range(G)
                Gs = [ds_ref[p0 + q] * blockdiag for q in qs]
                Ss = [st_ref[p0 + q, c] for q in qs]
                t1 = [_dot(bk_ref[q, c], Gs[q], NT) for q in qs]
                t3 = [_dot(pv_ref[q, c], Gs[q]) for q in qs]
                dpy = [jnp.concatenate([t1[q][:C], dpre_ref[q, 3, rows, :]], axis=0) for q in qs]
                t2 = [_dot(dpy[q], Ss[q]) for q in qs]
                back = [_dot(dpy[q], wq_ref[q, c], TN) for q in qs]
                for q in qs:
                    dvs_ref[q, rows, :] += t1[q][C:]
                    dpre_ref[q, 0, rows, :] = t2[q][:C]
                    dpre_ref[q, 1, rows, :] = t1[q][:C]
                    dpre_ref[q, 2, rows, :] = t2[q][C:]
                    dpre_ref[q, 4, rows, :] = t3[q][:C]
                    dpre_ref[q, 5, rows, :] = t3[q][C:]
                    dpre_ref[q, 6, row0, :] = jnp.sum(Ss[q] * Gs[q], axis=0, keepdims=True)
                    ds_ref[p0 + q] = Gs[q] * dec_ref[q, row0, :] + back[q]
                return carry

            lax.fori_loop(0, nsub, step, 0)
            grads = prep_vjp(tuple(tuple(dpre_ref[q, i] for i in range(7)) + post[q][1:] for q in range(G)))
            for q in range(G):
                ln = _head_lanes(p0 + q)
                gq = grads[N_PREP_IN * q:N_PREP_IN * (q + 1)]
                dr_ref[:, ln] = gq[0] + post[q][0]
                dk_ref[:, ln] = gq[1]
                dv_ref[:, ln] = gq[2] + dvs_ref[q]
                dlw = dlw + gq[3]
                dgd = dgd + gq[4]
                for x, gx in zip(dvec[:4], gq[5:9]):
                    x[:, ln] += gx
                for x, gx in zip(dmat, gq[9:]):
                    x[p0 + q] += gx
        dlo_ref[:, 0:LANES] = dlw
        dlo_ref[:, LANES:2 * LANES] = dgd

    dcol = jax.ShapeDtypeStruct((T, RW_WIDTH), F32)
    dlo_spec = pl.BlockSpec((tb, 2 * LANES), lambda t: (nT - 1 - t, 0))
    blk = pltpu.VMEM((G, tb, LANES), F32)
    stacked = pltpu.VMEM((G, nsub, 2 * C, LANES), F32)
    return pl.pallas_call(
        body, name=name, grid=(nT,),
        in_specs=ins + [st, wide(do_blk)] + [vec] * N_RW_VEC + [mat] * N_RW_MAT,
        out_specs=[wide(0)] * 3 + [dlo_spec] + [vec] * N_RW_VEC + [mat] * N_RW_MAT,
        out_shape=[dcol] * 3 + [jax.ShapeDtypeStruct((T, 2 * LANES), F32)]
        + [jax.ShapeDtypeStruct((1, RW_WIDTH), F32)] * N_RW_VEC
        + [jax.ShapeDtypeStruct((RW_PAIRS, LANES, LANES), F32)] * N_RW_MAT,
        scratch_shapes=[pltpu.VMEM((RW_PAIRS, LANES, LANES), F32), stacked, stacked, stacked, stacked, blk, blk,
                        pltpu.VMEM((G, 7, tb, LANES), F32), blk],
    )(rws, rws, rws, rws, rws, states, do, *vecs, *mats)


def _my_index():
    return 4 * lax.axis_index("x") + 2 * lax.axis_index("y") + lax.axis_index("c")


def _peer(bits):
    pos = []
    for name, flip in zip(("x", "y", "c"), bits):
        i = lax.axis_index(name)
        pos.append(1 - i if flip else i)
    return tuple(pos)


def _peer_index(bits):
    x, y, c = _peer(bits)
    return 4 * x + 2 * y + c


def _all_gather(shards, name):
    n = len(shards)
    chips = [(1, 0, 0), (0, 1, 0), (1, 1, 0)]
    sib = (0, 0, 1)

    def body(*refs):
        ins, outs = refs[:n], refs[n:2 * n]
        send_sems, recv_sems, local_sems = refs[2 * n:]

        def rows(k, dev):
            r = ins[k].shape[0]
            return outs[k].at[pl.ds(dev * r, r), :]

        def copy(k, slot, block_dev, to_bits, src=None):
            return pltpu.make_async_remote_copy(
                src_ref=rows(k, block_dev) if src is None else src, dst_ref=rows(k, block_dev),
                send_sem=send_sems.at[k, slot], recv_sem=recv_sems.at[k, slot],
                device_id=_peer(to_bits), device_id_type=MESH_ID)

        me = _my_index()
        started = []
        for k in range(n):
            mine = pltpu.make_async_copy(ins[k], rows(k, me), local_sems.at[k])
            mine.start()
            started.append(mine)
        sends = []
        for k in range(n):
            first = [copy(k, 0, me, sib, src=ins[k])]
            first += [copy(k, 1 + j, me, chip, src=ins[k]) for j, chip in enumerate(chips)]
            for cp in first:
                cp.start()
            sends += first
        for k in range(n):
            for j, chip in enumerate(chips):
                copy(k, 1 + j, _peer_index(chip), chip).wait_recv()
                fwd = copy(k, 4 + j, _peer_index(chip), sib)
                fwd.start()
                sends.append(fwd)
        for k in range(n):
            copy(k, 0, _peer_index(sib), sib).wait_recv()
            for j, chip in enumerate(chips):
                both = (chip[0], chip[1], 1)
                copy(k, 4 + j, _peer_index(both), sib).wait_recv()
        for cp in sends:
            cp.wait_send()
        for cp in started:
            cp.wait()

    any_spec = pl.BlockSpec(memory_space=pl.ANY)
    return pl.pallas_call(
        body, name=name,
        in_specs=[any_spec] * n, out_specs=[any_spec] * n,
        out_shape=[jax.ShapeDtypeStruct((N_DEV * s.shape[0], s.shape[1]), s.dtype) for s in shards],
        scratch_shapes=[pltpu.SemaphoreType.DMA((n, 7)), pltpu.SemaphoreType.DMA((n, 7)),
                        pltpu.SemaphoreType.DMA((n,))],
    )(*shards)


def _exchange(partials, name):
    n = len(partials)
    flips = [(dx, dy, dc) for dx in (0, 1) for dy in (0, 1) for dc in (0, 1)][1:]

    def body(*refs):
        ins, outs = refs[:n], refs[n:2 * n]
        send_sems, recv_sems, local_sems = refs[2 * n:]
        me = _my_index()
        local = []
        for k in range(n):
            cp = pltpu.make_async_copy(ins[k].at[me], outs[k].at[me], local_sems.at[k])
            cp.start()
            local.append(cp)
        copies = []
        for k in range(n):
            for d, bits in enumerate(flips):
                cp = pltpu.make_async_remote_copy(
                    src_ref=ins[k].at[_peer_index(bits)], dst_ref=outs[k].at[me],
                    send_sem=send_sems.at[k, d], recv_sem=recv_sems.at[k, d],
                    device_id=_peer(bits), device_id_type=MESH_ID)
                cp.start()
                copies.append(cp)
        for cp in copies:
            cp.wait_recv()
        for cp in copies:
            cp.wait_send()
        for cp in local:
            cp.wait()

    any_spec = pl.BlockSpec(memory_space=pl.ANY)
    return pl.pallas_call(
        body, name=name,
        in_specs=[any_spec] * n, out_specs=[any_spec] * n,
        out_shape=[jax.ShapeDtypeStruct(p.shape, p.dtype) for p in partials],
        scratch_shapes=[pltpu.SemaphoreType.DMA((n, 7)), pltpu.SemaphoreType.DMA((n, 7)),
                        pltpu.SemaphoreType.DMA((n,))],
    )(*partials)


HBM_SPEC = pl.BlockSpec(memory_space=pltpu.HBM)
SEM_SPEC = pl.BlockSpec(memory_space=pltpu.SEMAPHORE)
ALL_FLIPS = [(dx, dy, dc) for dx in (0, 1) for dy in (0, 1) for dc in (0, 1)][1:]


def _spread_copies(srcs, lands, send_sems, recv_sems, to_x):
    me = _my_index()
    my_x = lax.axis_index("x")
    copies = []
    for k, land in enumerate(lands):
        for d, bits in enumerate(ALL_FLIPS):
            if not srcs:
                src = land.at[me]
            elif to_x is None:
                src = srcs[k].at[_peer_index(bits)]
            else:
                _, py, pc = _peer(bits)
                src = srcs[k].at[2 * py + pc]
            cp = pltpu.make_async_remote_copy(
                src_ref=src, dst_ref=land.at[me],
                send_sem=send_sems.at[k * 7 + d], recv_sem=recv_sems.at[k * 7 + d],
                device_id=_peer(bits), device_id_type=MESH_ID)
            sends = True if to_x is None else my_x == (to_x ^ bits[0])
            receives = True if to_x is None else my_x == to_x
            copies.append((cp, sends, receives))
    return copies


def _when(cond, fn):
    if cond is True:
        fn()
    else:
        pl.when(cond)(fn)


def _spread_start(srcs, lands, name, to_x=None):
    ns, n = len(srcs), len(lands)

    def body(*refs):
        src_refs, land_refs = refs[:ns], refs[ns:ns + n]
        send_sems, recv_sems = refs[ns + n], refs[ns + n + 1]
        token = refs[-1]
        for cp, sends, _ in _spread_copies(src_refs, land_refs, send_sems, recv_sems, to_x):
            _when(sends, cp.start)
        token[...] = jnp.zeros_like(token)

    bufs = list(srcs) + list(lands)
    out = pl.pallas_call(
        body, name=name,
        out_shape=(pltpu.SemaphoreType.DMA((7 * n,)), pltpu.SemaphoreType.DMA((7 * n,)),
                   *[pltpu.HBM(b.shape, b.dtype) for b in bufs], jax.ShapeDtypeStruct((8, LANES), F32)),
        in_specs=[HBM_SPEC] * (ns + n),
        out_specs=(SEM_SPEC, SEM_SPEC, *[HBM_SPEC] * (ns + n), pl.BlockSpec(memory_space=pltpu.VMEM)),
        input_output_aliases={i: 2 + i for i in range(ns + n)},
        compiler_params=pltpu.CompilerParams(has_side_effects=pltpu.SideEffectType.DATAFLOW_SIDE_EFFECTING),
    )(*[pltpu.with_memory_space_constraint(b, pltpu.HBM) for b in bufs])
    return out[0], out[1], list(out[2:2 + ns]), list(out[2 + ns:2 + ns + n]), out[-1]


def _spread_wait(send_sems, recv_sems, srcs, lands, after, name, to_x=None):
    ns, n = len(srcs), len(lands)

    def body(*refs):
        src_refs, land_refs = refs[:ns], refs[ns:ns + n]
        send_sems, recv_sems = refs[ns + n], refs[ns + n + 1]
        for cp, sends, receives in _spread_copies(src_refs, land_refs, send_sems, recv_sems, to_x):
            _when(sends, cp.wait_send)
            _when(receives, cp.wait_recv)

    bufs = list(srcs) + list(lands)
    out = pl.pallas_call(
        body, name=name,
        out_shape=tuple(pltpu.HBM(b.shape, b.dtype) for b in bufs),
        in_specs=[HBM_SPEC] * (ns + n) + [SEM_SPEC, SEM_SPEC, pl.BlockSpec(memory_space=pl.ANY)],
        out_specs=tuple([HBM_SPEC] * (ns + n)),
        input_output_aliases={i: i for i in range(ns + n)},
        compiler_params=pltpu.CompilerParams(has_side_effects=pltpu.SideEffectType.DATAFLOW_SIDE_EFFECTING),
    )(*bufs, send_sems, recv_sems, after)
    return list(out[ns:])


def _own_slot_only(block, me):
    return lax.dynamic_update_slice(lax.empty((N_DEV,) + block.shape, block.dtype), block[None], (me, 0, 0))


def _sum_slots(landed, name):
    _, R, C = landed.shape
    tb = _row_tile(R, 128)

    def body(l_ref, o_ref):
        acc = l_ref[0].astype(F32)
        for s in range(1, N_DEV):
            acc = acc + l_ref[s].astype(F32)
        o_ref[...] = acc

    return pl.pallas_call(
        body, name=name, grid=(R // tb,),
        in_specs=[pl.BlockSpec((N_DEV, tb, C), lambda i: (0, i, 0))],
        out_specs=pl.BlockSpec((tb, C), lambda i: (i, 0)),
        out_shape=jax.ShapeDtypeStruct((R, C), F32),
    )(landed)


def _pack_rows(flat_list, width=LANES):
    flat = jnp.concatenate([a.reshape(-1) for a in flat_list])
    n = flat.shape[0]
    rows = -(-n // width)
    rows = -(-rows // 8) * 8
    return jnp.pad(flat, (0, rows * width - n)).reshape(rows, width)


def _unpack(packed, shapes):
    flat = packed.reshape(-1)
    out, off = [], 0
    for s in shapes:
        n = 1
        for d in s:
            n *= d
        out.append(flat[off:off + n].reshape(s))
        off += n
    return out


def kernel(x, norm1_w, w_in, hg_lb_logits, hg_norm_w, rw_shift_mu, rw_w0, rw_w2, rw_a0, rw_a2, rw_g2, rw_k_k, rw_k_a, rw_r_k, rw_ln_w, rw_ln_b, w_out, norm2_w, w_up, conv_w, conv_b, w_down, final_norm_w, loss_target, m_norm1_w, m_w_in, m_hg_lb_logits, m_hg_norm_w, m_rw_shift_mu, m_rw_w0, m_rw_w2, m_rw_a0, m_rw_a2, m_rw_g2, m_rw_k_k, m_rw_k_a, m_rw_r_k, m_rw_ln_w, m_rw_ln_b, m_w_out, m_norm2_w, m_w_up, m_conv_w, m_conv_b, m_w_down, m_final_norm_w, v_norm1_w, v_w_in, v_hg_lb_logits, v_hg_norm_w, v_rw_shift_mu, v_rw_w0, v_rw_w2, v_rw_a0, v_rw_a2, v_rw_g2, v_rw_k_k, v_rw_k_a, v_rw_r_k, v_rw_ln_w, v_rw_ln_b, v_w_out, v_norm2_w, v_w_up, v_conv_w, v_conv_b, v_w_down, v_final_norm_w):
    weights = dict(norm1_w=norm1_w, w_in=w_in, hg_lb_logits=hg_lb_logits, hg_norm_w=hg_norm_w,
                   rw_shift_mu=rw_shift_mu, rw_w0=rw_w0, rw_w2=rw_w2, rw_a0=rw_a0, rw_a2=rw_a2, rw_g2=rw_g2,
                   rw_k_k=rw_k_k, rw_k_a=rw_k_a, rw_r_k=rw_r_k, rw_ln_w=rw_ln_w, rw_ln_b=rw_ln_b, w_out=w_out,
                   norm2_w=norm2_w, w_up=w_up, conv_w=conv_w, conv_b=conv_b, w_down=w_down,
                   final_norm_w=final_norm_w)
    m_in = dict(norm1_w=m_norm1_w, w_in=m_w_in, hg_lb_logits=m_hg_lb_logits, hg_norm_w=m_hg_norm_w,
                rw_shift_mu=m_rw_shift_mu, rw_w0=m_rw_w0, rw_w2=m_rw_w2, rw_a0=m_rw_a0, rw_a2=m_rw_a2,
                rw_g2=m_rw_g2, rw_k_k=m_rw_k_k, rw_k_a=m_rw_k_a, rw_r_k=m_rw_r_k, rw_ln_w=m_rw_ln_w,
                rw_ln_b=m_rw_ln_b, w_out=m_w_out, norm2_w=m_norm2_w, w_up=m_w_up, conv_w=m_conv_w,
                conv_b=m_conv_b, w_down=m_w_down, final_norm_w=m_final_norm_w)
    v_in = dict(norm1_w=v_norm1_w, w_in=v_w_in, hg_lb_logits=v_hg_lb_logits, hg_norm_w=v_hg_norm_w,
                rw_shift_mu=v_rw_shift_mu, rw_w0=v_rw_w0, rw_w2=v_rw_w2, rw_a0=v_rw_a0, rw_a2=v_rw_a2,
                rw_g2=v_rw_g2, rw_k_k=v_rw_k_k, rw_k_a=v_rw_k_a, rw_r_k=v_rw_r_k, rw_ln_w=v_rw_ln_w,
                rw_ln_b=v_rw_ln_b, w_out=v_w_out, norm2_w=v_norm2_w, w_up=v_w_up, conv_w=v_conv_w,
                conv_b=v_conv_b, w_down=v_w_down, final_norm_w=v_final_norm_w)
    names = list(weights)
    sharded_small = ["rw_w2", "rw_a2", "rw_g2", "conv_w"]
    replicated = [n for n in names if n not in sharded_small + ["w_in", "w_out", "w_up", "w_down"]]

    xs = x[0]
    tgt = loss_target[0]

    small_shard = _pack_rows([weights[n] for n in sharded_small])
    g_win_t, g_small = _all_gather([w_in[0].T.astype(BF16), small_shard], "gather_weights")
    me = _my_index()
    later = (w_up[0].T.astype(BF16), w_out[0].astype(BF16), w_down[0].astype(BF16))
    later, _ = lax.optimization_barrier((later, g_small))
    later = [_own_slot_only(z, me) for z in later]
    g_send, g_recv, _, later, g_token = _spread_start([], later, "gather_later_start")
    small_shapes = [weights[n].shape for n in sharded_small]
    per_dev = [_unpack(g_small.reshape(N_DEV, -1)[j], small_shapes) for j in range(N_DEV)]
    w2_full, a2_full, g2_full, convw_full = [jnp.concatenate([per_dev[j][i][0] for j in range(N_DEV)], axis=-1)
                                             for i in range(4)]
    zeros64 = jnp.zeros((RW_PAIRS, 64, LANES), F32)
    by_pair = lambda z: z.reshape(z.shape[0], RW_PAIRS, LANES).transpose(1, 0, 2)
    w2p = jnp.concatenate([by_pair(w2_full), zeros64], axis=1)
    a2p = jnp.concatenate([zeros64, by_pair(a2_full)], axis=1)
    g2p = by_pair(g2_full)

    l0, l1 = hg_lb_logits[0:1], hg_lb_logits[1:2]
    h1 = _rms_fwd(xs, norm1_w + g_token[0:1, 0:1], "norm1")
    proj = _mm_nt(h1, g_win_t, "proj_in")
    o_hg, hg_states = _hg_fwd(proj, l0, l1, hg_norm_w, "hgrn2_fwd")
    rws = _shift_fwd(proj, rw_shift_mu, "token_shift")
    rw_vecs = [rw_w0, rw_a0, rw_k_k, rw_k_a, rw_r_k, rw_ln_w, rw_ln_b]
    rw_mats = [w2p, a2p, g2p]
    o_rw, rw_states = _rw_fwd(rws, rw_vecs, rw_mats, "rwkv7_fwd")
    o_mix = jnp.concatenate([o_hg, o_rw], axis=-1).astype(BF16)
    g_wup_t, g_wout, g_wdown = [z.reshape(-1, z.shape[-1])
                                for z in _spread_wait(g_send, g_recv, [], later, o_mix, "gather_later_wait")]
    x1 = _mm_nn(o_mix, g_wout, xs, "proj_out")
    h2 = _rms_fwd(x1, norm2_w, "norm2")
    u = _mm_nt(h2, g_wup_t, "ffn_up")
    act = _ffn_act_fwd(u, convw_full, conv_b, "ffn_act")
    x2 = _mm_nn(act, g_wdown, x1, "ffn_down")
    loss_part, dx2, d_final_w = _loss_head(x2, final_norm_w.reshape(1, -1), tgt, "loss_head")

    d_wdown = _mm_tn(act, dx2, 1408, "ffn_down_dw", BF16)
    dact = _mm_nt(dx2, g_wdown, "ffn_down_dx", BF16)
    du_g, du_v, dcw_g, dcw_v, dcb_g, dcb_v = _ffn_act_bwd(u, dact, convw_full, conv_b, "ffn_act_bwd")
    d_convw = jnp.concatenate([dcw_g, dcw_v], axis=-1)
    d_convb = jnp.concatenate([dcb_g, dcb_v], axis=-1)
    d_wup_t = jnp.concatenate([_mm_tn(du_g, h2, 1408, "ffn_up_dw_gate", BF16),
                               _mm_tn(du_v, h2, 1408, "ffn_up_dw_value", BF16)], axis=0)
    dx1, d_norm2 = _mm_nn_rms_bwd([du_g, du_v], g_wup_t, x1, norm2_w, dx2, "ffn_up_dx_norm2_bwd")
    d_wout = _mm_tn(o_mix, dx1, 512, "proj_out_dw", BF16)
    do = _mm_nt(dx1, g_wout, "proj_out_dx")
    early = [z.reshape(N_DEV, z.shape[0] // N_DEV, z.shape[1]) for z in (d_wup_t, d_wout, d_wdown)]
    early_land = [_own_slot_only(lax.dynamic_index_in_dim(z, me, 0, keepdims=False), me) for z in early]
    e_send, e_recv, early, early_land, e_token = _spread_start(early, early_land, "exchange_early_start")
    hg_norm_w_t = hg_norm_w + e_token[0:1, 0:1]
    dq, df, di, dg, d_l0, d_l1, d_hg_nw = _hg_bwd(proj, hg_states, do, 0, l0, l1, hg_norm_w_t, "hgrn2_bwd")
    half = N_DEV // 2
    own_half_block = lambda z: _own_slot_only(lax.dynamic_index_in_dim(z, me % half, 0, keepdims=False), me)
    n_lo = half * w_in.shape[2]
    d_win_lo = _mm_tn(jnp.concatenate([dq, df, di, dg[:, :n_lo - 3 * HG_WIDTH]], axis=-1), h1, 640,
                      "proj_in_dw_low", BF16).reshape(half, -1, D_MODEL)
    m_send, m_recv, mid, mid_land, m_token = _spread_start([d_win_lo], [own_half_block(d_win_lo)],
                                                            "exchange_mid_start", to_x=0)
    rw_vecs_t = [rw_vecs[0] + m_token[0:1, 0:1]] + rw_vecs[1:]
    rw_out = _rw_bwd(rws, rw_states, do, 1, rw_vecs_t, rw_mats, "rwkv7_bwd")
    d_rw_vecs = rw_out[4:4 + N_RW_VEC]
    d_w2p, d_a2p, d_g2p = rw_out[4 + N_RW_VEC:]
    dp_parts, dmu_parts = [], []
    for i, z in enumerate(rw_out[:4]):
        dp, dmu = _shift_bwd(z, proj, rw_shift_mu, i * RW_WIDTH, "token_shift_bwd_%d" % i)
        dp_parts.append(dp)
        dmu_parts.append(dmu)
    d_mu = jnp.concatenate(dmu_parts, axis=-1)
    d_win_hi = _mm_tn(jnp.concatenate([dg[:, n_lo - 3 * HG_WIDTH:]] + dp_parts, axis=-1), h1, 640,
                      "proj_in_dw_high", BF16).reshape(half, -1, D_MODEL)
    from_pairs = lambda z: z.transpose(1, 0, 2).reshape(z.shape[1], RW_WIDTH)
    d_w2 = from_pairs(d_w2p[:, :64])
    d_a2 = from_pairs(d_a2p[:, 64:])
    d_g2 = from_pairs(d_g2p)
    col_blocks = lambda z: z.reshape(z.shape[0], N_DEV, -1).transpose(1, 0, 2)
    small_part = jnp.stack([
        _pack_rows([col_blocks(d_w2)[j], col_blocks(d_a2)[j], col_blocks(d_g2)[j], col_blocks(d_convw)[j]])
        for j in range(N_DEV)])
    l_send, l_recv, late, late_land, l_token = _spread_start([d_win_hi], [own_half_block(d_win_hi)],
                                                             "exchange_late_start", to_x=1)
    grad_x, d_norm1 = _mm_nn_rms_bwd([dq, df, di, dg] + dp_parts, g_win_t, xs, norm1_w + l_token[0:1, 0:1], dx1,
                                     "proj_in_dx_norm1_bwd")

    rep_grads = dict(norm1_w=d_norm1, hg_lb_logits=jnp.concatenate([d_l0, d_l1], axis=0), hg_norm_w=d_hg_nw,
                     rw_shift_mu=d_mu, rw_w0=d_rw_vecs[0], rw_a0=d_rw_vecs[1], rw_k_k=d_rw_vecs[2],
                     rw_k_a=d_rw_vecs[3], rw_r_k=d_rw_vecs[4], rw_ln_w=d_rw_vecs[5], rw_ln_b=d_rw_vecs[6],
                     norm2_w=d_norm2, conv_b=d_convb, final_norm_w=d_final_w)
    rep_pack = _pack_rows([loss_part] + [rep_grads[n] for n in replicated])
    rep_part = jnp.broadcast_to(rep_pack[None], (N_DEV,) + rep_pack.shape)
    grads, delta, new_m, new_v = {}, {}, {}, {}

    def adamw_big(n, g):
        shp = weights[n].shape
        as2d = lambda z: z.reshape(shp[1], shp[2])
        grads[n] = g[None]
        d, nm, nv = _adamw(as2d(weights[n]), g, as2d(m_in[n]), as2d(v_in[n]), "adamw_" + n)
        delta[n], new_m[n], new_v[n] = d.reshape(shp), nm.reshape(shp), nv.reshape(shp)

    landed_early = _spread_wait(e_send, e_recv, early, early_land, grad_x, "exchange_early_wait")
    adamw_big("w_up", _sum_slots(landed_early[0], "sum_grads_w_up").T)
    adamw_big("w_out", _sum_slots(landed_early[1], "sum_grads_w_out"))
    adamw_big("w_down", _sum_slots(landed_early[2], "sum_grads_w_down"))
    (landed_mid,) = _spread_wait(m_send, m_recv, mid, mid_land, grad_x, "exchange_mid_wait", to_x=0)
    (landed_late,) = _spread_wait(l_send, l_recv, late, late_land, delta["w_down"], "exchange_late_wait", to_x=1)
    g_win = jnp.where(lax.axis_index("x") == 0, _sum_slots(landed_mid, "sum_grads_w_in_low"),
                      _sum_slots(landed_late, "sum_grads_w_in_high"))
    adamw_big("w_in", g_win.T)
    landed_rep, landed_small = _exchange([rep_part, small_part], "exchange_grads")
    g_small_sum = _unpack(_sum_slots(landed_small, "sum_grads_small"), small_shapes)
    rep_sum = _unpack(_sum_slots(landed_rep, "sum_grads_replicated"), [(1, 1)] + [weights[n].shape for n in replicated])
    loss = rep_sum[0].reshape(())
    grads.update(dict(zip(replicated, rep_sum[1:])))
    grads.update(dict(zip(sharded_small, g_small_sum)))

    small_names = replicated + sharded_small
    packs = [_pack_rows([src[n] for n in small_names]) for src in (weights, grads, m_in, v_in)]
    outs = _adamw(*packs, "adamw_small")
    small_shapes_all = [weights[n].shape for n in small_names]
    for dst, packed in zip((delta, new_m, new_v), outs):
        dst.update(dict(zip(small_names, _unpack(packed, small_shapes_all))))

    return (loss, grad_x[None], *[grads[n] for n in names], *[delta[n] for n in names],
            *[new_m[n] for n in names], *[new_v[n] for n in names])
```

```python
import functools

import jax
import jax.numpy as jnp
from jax import lax
from jax.experimental import pallas as pl
from jax.experimental.pallas import tpu as pltpu

F32 = jnp.float32
BF16 = jnp.bfloat16
SUM_PRECISION = lax.Precision.HIGH
SCAN_PRECISION = None
MESH_ID = pl.DeviceIdType.MESH

N_DEV = 8
D_MODEL = 1024
HG_WIDTH = 512
HG_HEAD_DIM = 128
HG_HEADS = 4
RW_WIDTH = 512
RW_PAIRS = 4
RW_HEAD_DIM = 64
HG_COLS = 2048
RW_COLS = 1792
D_FF = 2816
NORM_EPS = 1e-6
RW_GN_EPS = 64e-5
L2_EPS = 1e-12
ADAM_LR, ADAM_B1, ADAM_B2, ADAM_EPS, ADAM_WD, ADAM_STEP = 0.001, 0.9, 0.999, 1e-08, 0.01, 10

HG_CHUNK = 32
HG_HALF = 16
RW_CHUNK = 64
SCAN_ROWS = 256
LANES = 128

NN = ((1,), (0,))
NT = ((1,), (1,))
TN = ((0,), (0,))


def _dot(a, b, dims=NN, precision=SCAN_PRECISION):
    if precision is None:
        a, b = a.astype(BF16), b.astype(BF16)
    return lax.dot_general(a, b, (dims, ((), ())), precision=precision, preferred_element_type=F32)


def _iota2(shape, dim):
    return lax.broadcasted_iota(jnp.int32, shape, dim)


def _sigmoid(z):
    return 0.5 * jnp.tanh(0.5 * z) + 0.5


def _row_tile(n, want):
    t = min(n, want)
    while n % t:
        t //= 2
    return t


def _rms_fwd(x, w, name):
    T, D = x.shape
    tb = _row_tile(T, 512)

    def body(x_ref, w_ref, h_ref):
        xv = x_ref[...]
        r = lax.rsqrt(jnp.mean(xv * xv, axis=-1, keepdims=True) + NORM_EPS)
        h_ref[...] = (xv * r * w_ref[...]).astype(h_ref.dtype)

    return pl.pallas_call(
        body, name=name, grid=(T // tb,),
        in_specs=[pl.BlockSpec((tb, D), lambda i: (i, 0)), pl.BlockSpec((1, D), lambda i: (0, 0))],
        out_specs=pl.BlockSpec((tb, D), lambda i: (i, 0)),
        out_shape=jax.ShapeDtypeStruct((T, D), BF16),
    )(x, w)


def _mm_nn_rms_bwd(a, b, x, w, dres, name):
    parts = list(a)
    T, D = x.shape
    K = b.shape[0]
    tm = _row_tile(T, 256)
    widths = [p.shape[1] for p in parts]
    n = len(parts)

    def body(*refs):
        b_ref, x_ref, w_ref, dres_ref, dx_ref, dw_ref = refs[n:]

        @pl.when(pl.program_id(0) == 0)
        def _():
            dw_ref[...] = jnp.zeros_like(dw_ref)

        dy, off = None, 0
        for a_ref, wd in zip(refs[:n], widths):
            d = _dot(a_ref[...].astype(BF16), b_ref[off:off + wd, :].astype(BF16), NN, None)
            dy = d if dy is None else dy + d
            off += wd
        xv = x_ref[...]
        r = lax.rsqrt(jnp.mean(xv * xv, axis=-1, keepdims=True) + NORM_EPS)
        xn = xv * r
        dxn = dy * w_ref[...]
        dx_ref[...] = dres_ref[...] + r * (dxn - xn * jnp.mean(dxn * xn, axis=-1, keepdims=True))
        dw_ref[...] += jnp.sum(dy * xn, axis=0, keepdims=True)

    row = pl.BlockSpec((tm, D), lambda i: (i, 0))
    vec = pl.BlockSpec((1, D), lambda i: (0, 0))
    return pl.pallas_call(
        body, name=name, grid=(T // tm,),
        in_specs=[pl.BlockSpec((tm, wd), lambda i: (i, 0)) for wd in widths]
        + [pl.BlockSpec((K, D), lambda i: (0, 0)), row, vec, row],
        out_specs=[row, vec],
        out_shape=[jax.ShapeDtypeStruct((T, D), F32), jax.ShapeDtypeStruct((1, D), F32)],
    )(*parts, b, x, w, dres)


def _mm_nt(a, bt, name, out_dtype=F32):
    T, K = a.shape
    N = bt.shape[0]
    tm = _row_tile(T, 256)

    def body(a_ref, b_ref, o_ref):
        o_ref[...] = _dot(a_ref[...].astype(BF16), b_ref[...].astype(BF16), NT, None).astype(o_ref.dtype)

    return pl.pallas_call(
        body, name=name, grid=(T // tm,),
        in_specs=[pl.BlockSpec((tm, K), lambda i: (i, 0)), pl.BlockSpec((N, K), lambda i: (0, 0))],
        out_specs=pl.BlockSpec((tm, N), lambda i: (i, 0)),
        out_shape=jax.ShapeDtypeStruct((T, N), out_dtype),
    )(a, bt)


def _mm_nn(a, b, res, name, out_dtype=F32):
    parts = list(a) if isinstance(a, (list, tuple)) else [a]
    T = parts[0].shape[0]
    K, N = b.shape
    tm = _row_tile(T, 256)
    widths = [p.shape[1] for p in parts]
    n = len(parts)

    def body(*refs):
        b_ref, o_ref = refs[n], refs[-1]
        acc, off = None, 0
        for a_ref, w in zip(refs[:n], widths):
            d = _dot(a_ref[...].astype(BF16), b_ref[off:off + w, :].astype(BF16), NN, None)
            acc = d if acc is None else acc + d
            off += w
        if res is not None:
            acc = acc + refs[n + 1][...]
        o_ref[...] = acc.astype(o_ref.dtype)

    in_specs = [pl.BlockSpec((tm, w), lambda i: (i, 0)) for w in widths] + [pl.BlockSpec((K, N), lambda i: (0, 0))]
    args = parts + [b]
    if res is not None:
        in_specs.append(pl.BlockSpec((tm, N), lambda i: (i, 0)))
        args.append(res)
    return pl.pallas_call(
        body, name=name, grid=(T // tm,), in_specs=in_specs,
        out_specs=pl.BlockSpec((tm, N), lambda i: (i, 0)),
        out_shape=jax.ShapeDtypeStruct((T, N), out_dtype),
    )(*args)


def _mm_tn(a, b, tmm, name, out_dtype=F32):
    T, M = a.shape
    N = b.shape[1]
    tk = _row_tile(T, 512)
    nk = T // tk

    def body(a_ref, b_ref, o_ref, acc_ref):
        @pl.when(pl.program_id(1) == 0)
        def _():
            acc_ref[...] = jnp.zeros_like(acc_ref)

        acc_ref[...] += _dot(a_ref[...].astype(BF16), b_ref[...].astype(BF16), TN, None)

        @pl.when(pl.program_id(1) == nk - 1)
        def _():
            o_ref[...] = acc_ref[...].astype(o_ref.dtype)

    return pl.pallas_call(
        body, name=name, grid=(M // tmm, nk),
        in_specs=[pl.BlockSpec((tk, tmm), lambda m, k: (k, m)), pl.BlockSpec((tk, N), lambda m, k: (k, 0))],
        out_specs=pl.BlockSpec((tmm, N), lambda m, k: (m, 0)),
        out_shape=jax.ShapeDtypeStruct((M, N), out_dtype),
        scratch_shapes=[pltpu.VMEM((tmm, N), F32)],
    )(a, b)


class _RowShifts:
    def __init__(self, shape):
        index = _iota2(shape, 0)
        self.rows = shape[0]
        self.first = {n: index < n for n in (1, 2)}
        self.last = {n: index >= shape[0] - n for n in (1, 2)}

    def down(self, z, n):
        return jnp.where(self.first[n], 0.0, pltpu.roll(z, n, 0))

    def up(self, z, n):
        return jnp.where(self.last[n], 0.0, pltpu.roll(z, self.rows - n, 0))


def _shift_fwd(proj, mu, name):
    T = proj.shape[0]
    nblk = RW_COLS // LANES
    first = HG_COLS // LANES

    def body(p_ref, mu_ref, o_ref):
        p = p_ref[...]
        o_ref[...] = p + (_RowShifts(p.shape).down(p, 1) - p) * mu_ref[...]

    return pl.pallas_call(
        body, name=name, grid=(nblk,),
        in_specs=[pl.BlockSpec((T, LANES), lambda j: (0, first + j)), pl.BlockSpec((1, LANES), lambda j: (0, j))],
        out_specs=pl.BlockSpec((T, LANES), lambda j: (0, j)),
        out_shape=jax.ShapeDtypeStruct((T, RW_COLS), F32),
    )(proj, mu)


def _shift_bwd(ds, proj, mu, col0, name):
    T, width = ds.shape
    nblk = width // LANES
    first = (HG_COLS + col0) // LANES
    mu0 = col0 // LANES

    def body(ds_ref, p_ref, mu_ref, dp_ref, dmu_ref):
        dsv = ds_ref[...]
        p = p_ref[...]
        m = mu_ref[...]
        shifts = _RowShifts(p.shape)
        dp_ref[...] = (dsv * (1.0 - m) + shifts.up(dsv * m, 1)).astype(dp_ref.dtype)
        dmu_ref[...] = jnp.sum(dsv * (shifts.down(p, 1) - p), axis=0, keepdims=True)

    return pl.pallas_call(
        body, name=name, grid=(nblk,),
        in_specs=[pl.BlockSpec((T, LANES), lambda j: (0, j)),
                  pl.BlockSpec((T, LANES), lambda j: (0, first + j)),
                  pl.BlockSpec((1, LANES), lambda j: (0, mu0 + j))],
        out_specs=[pl.BlockSpec((T, LANES), lambda j: (0, j)), pl.BlockSpec((1, LANES), lambda j: (0, j))],
        out_shape=[jax.ShapeDtypeStruct((T, width), BF16), jax.ShapeDtypeStruct((1, width), F32)],
    )(ds, proj, mu)


def _conv3(z, w_ref, shifts):
    return w_ref[0:1, :] * shifts.down(z, 2) + w_ref[1:2, :] * shifts.down(z, 1) + w_ref[2:3, :] * z


def _ffn_act_fwd(u, conv_w, conv_b, name):
    T = u.shape[0]
    nblk = D_FF // LANES

    def body(ug_ref, uv_ref, wg_ref, wv_ref, bg_ref, bv_ref, act_ref):
        shifts = _RowShifts((T, LANES))
        gate = _conv3(ug_ref[...], wg_ref, shifts) + bg_ref[...]
        val = _conv3(uv_ref[...], wv_ref, shifts) + bv_ref[...]
        act_ref[...] = (gate * _sigmoid(gate) * val).astype(act_ref.dtype)

    col = lambda off: pl.BlockSpec((T, LANES), lambda j: (0, off + j))
    wsp = lambda off: pl.BlockSpec((3, LANES), lambda j: (0, off + j))
    bsp = lambda off: pl.BlockSpec((1, LANES), lambda j: (0, off + j))
    return pl.pallas_call(
        body, name=name, grid=(nblk,),
        in_specs=[col(0), col(nblk), wsp(0), wsp(nblk), bsp(0), bsp(nblk)],
        out_specs=pl.BlockSpec((T, LANES), lambda j: (0, j)),
        out_shape=jax.ShapeDtypeStruct((T, D_FF), BF16),
    )(u, u, conv_w, conv_w, conv_b, conv_b)


def _ffn_act_bwd(u, dact, conv_w, conv_b, name):
    T = u.shape[0]
    nblk = D_FF // LANES

    def conv_bwd(z, dzc, w_ref, du_ref, dw_ref, db_ref, shifts):
        up1, up2 = shifts.up(dzc, 1), shifts.up(dzc, 2)
        du = w_ref[2:3, :] * dzc + w_ref[1:2, :] * up1 + w_ref[0:1, :] * up2
        du_ref[...] = du.astype(du_ref.dtype)
        dw_ref[0:1, :] = jnp.sum(up2 * z, axis=0, keepdims=True)
        dw_ref[1:2, :] = jnp.sum(up1 * z, axis=0, keepdims=True)
        dw_ref[2:3, :] = jnp.sum(dzc * z, axis=0, keepdims=True)
        db_ref[...] = jnp.sum(dzc, axis=0, keepdims=True)

    def body(ug_ref, uv_ref, da_ref, wg_ref, wv_ref, bg_ref, bv_ref,
             dug_ref, duv_ref, dwg_ref, dwv_ref, dbg_ref, dbv_ref):
        ug, uv = ug_ref[...], uv_ref[...]
        shifts = _RowShifts((T, LANES))
        gate = _conv3(ug, wg_ref, shifts) + bg_ref[...]
        val = _conv3(uv, wv_ref, shifts) + bv_ref[...]
        da = da_ref[...].astype(F32)
        sg = _sigmoid(gate)
        dgate = da * val * (sg * (1.0 + gate * (1.0 - sg)))
        dval = da * gate * sg
        conv_bwd(ug, dgate, wg_ref, dug_ref, dwg_ref, dbg_ref, shifts)
        conv_bwd(uv, dval, wv_ref, duv_ref, dwv_ref, dbv_ref, shifts)

    col = lambda off: pl.BlockSpec((T, LANES), lambda j: (0, off + j))
    wsp = lambda off: pl.BlockSpec((3, LANES), lambda j: (0, off + j))
    bsp = lambda off: pl.BlockSpec((1, LANES), lambda j: (0, off + j))
    half = lambda r, dt: jax.ShapeDtypeStruct((r, D_FF), dt)
    return pl.pallas_call(
        body, name=name, grid=(nblk,),
        in_specs=[col(0), col(nblk), col(0), wsp(0), wsp(nblk), bsp(0), bsp(nblk)],
        out_specs=[col(0), col(0), wsp(0), wsp(0), bsp(0), bsp(0)],
        out_shape=[half(T, BF16), half(T, BF16), half(3, F32), half(3, F32), half(1, F32), half(1, F32)],
    )(u, u, dact, conv_w, conv_w, conv_b, conv_b)


def _mm_nn_loss_head(a, b, res, w, target, name):
    T, K = a.shape
    D = b.shape[1]
    tm = _row_tile(T, 256)

    def body(a_ref, b_ref, res_ref, w_ref, t_ref, loss_ref, dx_ref, dw_ref):
        @pl.when(pl.program_id(0) == 0)
        def _():
            loss_ref[...] = jnp.zeros_like(loss_ref)
            dw_ref[...] = jnp.zeros_like(dw_ref)

        xv = res_ref[...] + _dot(a_ref[...].astype(BF16), b_ref[...].astype(BF16), NN, None)
        r = lax.rsqrt(jnp.mean(xv * xv, axis=-1, keepdims=True) + NORM_EPS)
        xn = xv * r
        err = xn * w_ref[...] - t_ref[...]
        row_loss = jnp.sum(err * err, axis=-1, keepdims=True) * (0.5 / D)
        loss_ref[...] += jnp.sum(row_loss, axis=0, keepdims=True)
        dy = err * (1.0 / D)
        dxn = dy * w_ref[...]
        dx_ref[...] = r * (dxn - xn * jnp.mean(dxn * xn, axis=-1, keepdims=True))
        dw_ref[...] += jnp.sum(dy * xn, axis=0, keepdims=True)

    row = pl.BlockSpec((tm, D), lambda i: (i, 0))
    vec = pl.BlockSpec((1, D), lambda i: (0, 0))
    return pl.pallas_call(
        body, name=name, grid=(T // tm,),
        in_specs=[pl.BlockSpec((tm, K), lambda i: (i, 0)), pl.BlockSpec((K, D), lambda i: (0, 0)), row, vec, row],
        out_specs=[pl.BlockSpec((1, 1), lambda i: (0, 0)), row, vec],
        out_shape=[jax.ShapeDtypeStruct((1, 1), F32), jax.ShapeDtypeStruct((T, D), F32),
                   jax.ShapeDtypeStruct((1, D), F32)],
    )(a, b, res, w, target)


def _adamw(w, g, m, v, name):
    R, C = w.shape
    tb = _row_tile(R, 256) if R % 8 == 0 else R

    def body(w_ref, g_ref, m_ref, v_ref, d_ref, nm_ref, nv_ref):
        gv = g_ref[...]
        nm = ADAM_B1 * m_ref[...] + (1.0 - ADAM_B1) * gv
        nv = ADAM_B2 * v_ref[...] + (1.0 - ADAM_B2) * (gv * gv)
        m_hat = nm / (1.0 - ADAM_B1 ** ADAM_STEP)
        v_hat = nv / (1.0 - ADAM_B2 ** ADAM_STEP)
        d_ref[...] = -ADAM_LR * (m_hat / (jnp.sqrt(v_hat) + ADAM_EPS) + ADAM_WD * w_ref[...])
        nm_ref[...] = nm
        nv_ref[...] = nv

    blk = pl.BlockSpec((tb, C), lambda i: (i, 0))
    sd = jax.ShapeDtypeStruct((R, C), F32)
    return pl.pallas_call(
        body, name=name, grid=(R // tb,), in_specs=[blk] * 4, out_specs=[blk] * 3, out_shape=[sd] * 3,
    )(w, g, m, v)


def _chunk_masks(rows, chunk):
    shift = chunk.bit_length() - 1
    i, j = _iota2((rows, rows), 0), _iota2((rows, rows), 1)
    same = jnp.right_shift(i, shift) == jnp.right_shift(j, shift)
    return same.astype(F32), (same & (j <= i)).astype(F32), (same & (j < i)).astype(F32)


def _head_lanes(h):
    return slice(h * LANES, (h + 1) * LANES)


def _chunk_rows(c, chunk):
    return pl.ds(pl.multiple_of(c * chunk, chunk), chunk)


def _hg_consts(rows):
    same, tril, _ = _chunk_masks(rows, HG_CHUNK)
    half_same, half_tril, _ = _chunk_masks(rows, HG_HALF)
    i, j = _iota2((rows, rows), 0), _iota2((rows, rows), 1)
    half_shift, shift = HG_HALF.bit_length() - 1, HG_CHUNK.bit_length() - 1
    mid_row = jnp.left_shift(jnp.right_shift(i, half_shift), half_shift) + (HG_HALF // 2 - 1)
    bound_row = jnp.left_shift(jnp.right_shift(i, shift), shift) + (HG_HALF - 1)
    upto_mid = ((same > 0) & (j <= mid_row)).astype(F32)
    upto_bound = ((same > 0) & (j <= bound_row)).astype(F32)
    lower_left = tril * (1.0 - half_same)
    return jnp.concatenate([tril, same, upto_mid, upto_bound], axis=0), half_tril, lower_left


N_HG_IN = 5


def _hg_prep(consts, *flat):
    sums, half_tril, lower_left = consts
    rows = half_tril.shape[0]
    heads, logs = [], []
    for h in range(len(flat) // N_HG_IN):
        qr, fr, ir, l0, l1 = flat[N_HG_IN * h:N_HG_IN * (h + 1)]
        lb = _sigmoid(l0 - l1)
        f = lb + (1.0 - lb) * _sigmoid(fr)
        heads.append((qr * _sigmoid(qr) * (HG_HEAD_DIM ** -0.5), 1.0 - f, ir))
        logs.append(jnp.log(f))
    acc = _dot(sums, jnp.concatenate(logs, axis=1), NN, SUM_PRECISION)
    sums_of = []
    for h in range(len(heads)):
        acc_h = acc[:, h * LANES:(h + 1) * LANES]
        sums_of.append(tuple(acc_h[n * rows:(n + 1) * rows] for n in range(4)))
    near = [_dot(q * jnp.exp(a - mid), k * jnp.exp(mid - a), NT) * half_tril
            for (q, k, _), (a, _, mid, _) in zip(heads, sums_of)]
    far = [_dot(q * jnp.exp(jnp.minimum(a - bound, 0.0)), k * jnp.exp(jnp.minimum(bound - a, 0.0)), NT) * lower_left
           for (q, k, _), (a, _, _, bound) in zip(heads, sums_of)]
    intra = [_dot(n + f, ir) for n, f, (_, _, ir) in zip(near, far, heads)]
    return tuple((q * jnp.exp(a), o_intra, k * jnp.exp(tot - a), jnp.exp(tot))
                 for (q, k, _), (a, tot, _, _), o_intra in zip(heads, sums_of, intra))


def _hg_prep_args(q_ref, f_ref, i_ref, l0_ref, l1_ref):
    flat = []
    for h in range(HG_HEADS):
        ln = _head_lanes(h)
        flat += [q_ref[:, ln], f_ref[:, ln], i_ref[:, ln], l0_ref[:, ln], l1_ref[:, ln]]
    return flat


def _hg_post(o, gr, nw):
    on = o * lax.rsqrt(jnp.mean(o * o, axis=-1, keepdims=True) + NORM_EPS)
    return on * nw * (gr * _sigmoid(gr))


def _hg_specs(T, tb, rev):
    nT = T // tb
    tix = (lambda t: nT - 1 - t) if rev else (lambda t: t)
    col = lambda blk: pl.BlockSpec((tb, HG_WIDTH), lambda t: (tix(t), blk))
    vec = pl.BlockSpec((1, HG_WIDTH), lambda t: (0, 0))
    st = pl.BlockSpec((HG_HEADS, tb // HG_CHUNK, HG_HEAD_DIM, HG_HEAD_DIM), lambda t: (0, tix(t), 0, 0))
    return nT, col, vec, st


def _hg_fwd(proj, l0, l1, nw, name):
    T = proj.shape[0]
    tb = _row_tile(T, SCAN_ROWS)
    nsub = tb // HG_CHUNK
    nT, col, vec, st = _hg_specs(T, tb, False)

    def body(q_ref, f_ref, i_ref, g_ref, l0_ref, l1_ref, nw_ref, o_ref, st_ref, s_ref, qe_ref, kd_ref, dec_ref):
        @pl.when(pl.program_id(0) == 0)
        def _():
            s_ref[...] = jnp.zeros_like(s_ref)

        consts = _hg_consts(tb)
        outs = _hg_prep(consts, *_hg_prep_args(q_ref, f_ref, i_ref, l0_ref, l1_ref))
        for h, (qe, o_intra, kd, dec) in enumerate(outs):
            qe_ref[h], kd_ref[h], dec_ref[h] = qe, kd, dec
            o_ref[:, _head_lanes(h)] = o_intra

        def step(c, carry):
            rows = _chunk_rows(c, HG_CHUNK)
            heads = range(HG_HEADS)
            states = [s_ref[h] for h in heads]
            inter = [_dot(qe_ref[h, rows, :], states[h], NT) for h in heads]
            updates = [_dot(i_ref[rows, _head_lanes(h)], kd_ref[h, rows, :], TN) for h in heads]
            for h in heads:
                st_ref[h, c] = states[h]
                o_ref[rows, _head_lanes(h)] += inter[h]
                s_ref[h] = states[h] * dec_ref[h, pl.ds(c * HG_CHUNK, 1), :] + updates[h]
            return carry

        lax.fori_loop(0, nsub, step, 0)
        for h in range(HG_HEADS):
            ln = _head_lanes(h)
            o_ref[:, ln] = _hg_post(o_ref[:, ln], g_ref[:, ln], nw_ref[:, ln])

    blk = pltpu.VMEM((HG_HEADS, tb, LANES), F32)
    return pl.pallas_call(
        body, name=name, grid=(nT,),
        in_specs=[col(0), col(1), col(2), col(3), vec, vec, vec],
        out_specs=[col(0), st],
        out_shape=[jax.ShapeDtypeStruct((T, HG_WIDTH), F32),
                   jax.ShapeDtypeStruct((HG_HEADS, T // HG_CHUNK, HG_HEAD_DIM, HG_HEAD_DIM), F32)],
        scratch_shapes=[pltpu.VMEM((HG_HEADS, HG_HEAD_DIM, HG_HEAD_DIM), F32), blk, blk, blk],
    )(proj, proj, proj, proj, l0, l1, nw)


def _hg_bwd(proj, states, do, do_blk, l0, l1, nw, name):
    T = proj.shape[0]
    tb = _row_tile(T, SCAN_ROWS)
    nsub = tb // HG_CHUNK
    nT, col, vec, st = _hg_specs(T, tb, True)

    def body(q_ref, f_ref, i_ref, g_ref, st_ref, do_ref, l0_ref, l1_ref, nw_ref,
             dq_ref, df_ref, di_ref, dg_ref, dl0_ref, dl1_ref, dnw_ref,
             ds_ref, qe_ref, kd_ref, dec_ref, o_ref, dqe_ref, dkd_ref, ddec_ref, dis_ref):
        @pl.when(pl.program_id(0) == 0)
        def _():
            ds_ref[...] = jnp.zeros_like(ds_ref)
            dl0_ref[...] = jnp.zeros_like(dl0_ref)
            dl1_ref[...] = jnp.zeros_like(dl1_ref)
            dnw_ref[...] = jnp.zeros_like(dnw_ref)

        consts = _hg_consts(tb)
        outs, prep_vjp = jax.vjp(functools.partial(_hg_prep, consts),
                                 *_hg_prep_args(q_ref, f_ref, i_ref, l0_ref, l1_ref))
        for h, (qe, o_intra, kd, dec) in enumerate(outs):
            qe_ref[h], kd_ref[h], dec_ref[h], o_ref[h] = qe, kd, dec, o_intra

        def redo(c, carry):
            rows = _chunk_rows(c, HG_CHUNK)
            inter = [_dot(qe_ref[h, rows, :], st_ref[h, c], NT) for h in range(HG_HEADS)]
            for h in range(HG_HEADS):
                o_ref[h, rows, :] += inter[h]
            return carry

        lax.fori_loop(0, nsub, redo, 0)
        for h in range(HG_HEADS):
            ln = _head_lanes(h)
            _, vjp = jax.vjp(_hg_post, o_ref[h], g_ref[:, ln], nw_ref[:, ln])
            d_o, dgr, dnw = vjp(do_ref[:, ln])
            o_ref[h] = d_o
            dg_ref[:, ln] = dgr.astype(dg_ref.dtype)
            dnw_ref[:, ln] += dnw
        ddec_ref[...] = jnp.zeros_like(ddec_ref)

        def step(i, carry):
            c = nsub - 1 - i
            rows = _chunk_rows(c, HG_CHUNK)
            row0 = pl.ds(c * HG_CHUNK, 1)
            heads = range(HG_HEADS)
            Gs = [ds_ref[h] for h in heads]
            Ss = [st_ref[h, c] for h in heads]
            d_os = [o_ref[h, rows, :] for h in heads]
            dqe = [_dot(d_os[h], Ss[h]) for h in heads]
            dkd = [_dot(i_ref[rows, _head_lanes(h)], Gs[h]) for h in heads]
            dis = [_dot(kd_ref[h, rows, :], Gs[h], NT) for h in heads]
            back = [_dot(d_os[h], qe_ref[h, rows, :], TN) for h in heads]
            for h in heads:
                dqe_ref[h, rows, :] = dqe[h]
                dkd_ref[h, rows, :] = dkd[h]
                dis_ref[h, rows, :] = dis[h]
                ddec_ref[h, row0, :] = jnp.sum(Ss[h] * Gs[h], axis=0, keepdims=True)
                ds_ref[h] = Gs[h] * dec_ref[h, row0, :] + back[h]
            return carry

        lax.fori_loop(0, nsub, step, 0)
        grads = prep_vjp(tuple((dqe_ref[h], o_ref[h], dkd_ref[h], ddec_ref[h]) for h in range(HG_HEADS)))
        for h in range(HG_HEADS):
            ln = _head_lanes(h)
            dq, df, di, dl0, dl1 = grads[N_HG_IN * h:N_HG_IN * (h + 1)]
            dq_ref[:, ln] = dq.astype(dq_ref.dtype)
            df_ref[:, ln] = df.astype(df_ref.dtype)
            di_ref[:, ln] = (di + dis_ref[h]).astype(di_ref.dtype)
            dl0_ref[:, ln] += dl0
            dl1_ref[:, ln] += dl1

    dcol = jax.ShapeDtypeStruct((T, HG_WIDTH), BF16)
    dvec = jax.ShapeDtypeStruct((1, HG_WIDTH), F32)
    blk = pltpu.VMEM((HG_HEADS, tb, LANES), F32)
    return pl.pallas_call(
        body, name=name, grid=(nT,),
        in_specs=[col(0), col(1), col(2), col(3), st, col(do_blk), vec, vec, vec],
        out_specs=[col(0)] * 4 + [vec] * 3,
        out_shape=[dcol] * 4 + [dvec] * 3,
        scratch_shapes=[pltpu.VMEM((HG_HEADS, HG_HEAD_DIM, HG_HEAD_DIM), F32)] + [blk] * 8,
    )(proj, proj, proj, proj, states, do, l0, l1, nw)


def _rw_consts(rows):
    same, tril, stril = _chunk_masks(rows, RW_CHUNK)
    br, bc = _iota2((LANES, LANES), 0), _iota2((LANES, LANES), 1)
    blockdiag = ((br < RW_HEAD_DIM) == (bc < RW_HEAD_DIM)).astype(F32)
    m0 = (_iota2((1, LANES), 1) < RW_HEAD_DIM).astype(F32)
    return same, tril, stril, blockdiag, m0, 1.0 - m0


def _unit_lower_inverses_impl(lows):
    rows = lows[0].shape[0]
    eye = (_iota2(lows[0].shape, 0) == _iota2(lows[0].shape, 1)).astype(F32)
    xs = [low + eye for low in lows]
    ps = [_dot(low, low) for low in lows]
    n = 4
    while n < RW_CHUNK:
        zs = [_dot(jnp.concatenate([p, x], axis=0), p) for p, x in zip(ps, xs)]
        ps = [z[:rows] for z in zs]
        xs = [x + z[rows:] for x, z in zip(xs, zs)]
        n *= 2
    return tuple(x + _dot(x, p) for x, p in zip(xs, ps))


@jax.custom_vjp
def _unit_lower_inverses(lows):
    return _unit_lower_inverses_impl(lows)


def _unit_lower_inverses_fwd(lows):
    xs = _unit_lower_inverses_impl(lows)
    return xs, xs


def _unit_lower_inverses_bwd(xs, dxs):
    ts = [_dot(x, dx, TN) for x, dx in zip(xs, dxs)]
    return (tuple(_dot(t, x, NT) for t, x in zip(ts, xs)),)


_unit_lower_inverses.defvjp(_unit_lower_inverses_fwd, _unit_lower_inverses_bwd)


N_PREP_IN = 12
RW_GROUP = 2


def _rw_prep(consts, *flat):
    same, tril, stril, blockdiag, m0, m1 = consts
    rows = tril.shape[0]
    masks = (m0, m1)
    pairs = [flat[N_PREP_IN * i:N_PREP_IN * (i + 1)] for i in range(len(flat) // N_PREP_IN)]
    lora = [(_dot(jnp.tanh(lw), w2p), _dot(lw, a2p), _dot(_sigmoid(gd), g2), _dot(jnp.square(kx * k_k), blockdiag))
            for _, kx, _, lw, gd, _, _, k_k, _, w2p, a2p, g2 in pairs]
    mid = []
    for (r, kx, v, lw, gd, w0, a0, k_k, k_a, w2p, a2p, g2), (xw, xa, g, kk_sq) in zip(pairs, lora):
        xw = w0 + xw
        w = jnp.minimum(xw, 0.0) - jnp.log(1.0 + jnp.exp(-jnp.abs(xw))) - 0.5
        a_s = _sigmoid(a0 + xa)
        kk = kx * k_k / jnp.maximum(jnp.sqrt(kk_sq), L2_EPS)
        mid.append((-jnp.exp(w), a_s, kk, kx * (1.0 + (a_s - 1.0) * k_a), g))
    accs = [_dot(jnp.concatenate([tril, same], axis=0), ld, NN, SUM_PRECISION) for ld, _, _, _, _ in mid]
    pre = []
    for (r, _, v, *_), (ld, a_s, kk, k2, g), acc in zip(pairs, mid, accs):
        bv = kk * a_s
        cum, tot = acc[:rows], acc[rows:]
        ecn = jnp.exp(-cum)
        a_t = -kk * jnp.exp(cum - ld)
        r_t = r * jnp.exp(cum)
        rem = jnp.exp(tot - cum)
        pre.append((v, a_t, r_t, (bv * ecn, k2 * ecn), (bv * rem, k2 * rem, jnp.exp(tot), k2, g)))
    zs = [_dot(jnp.concatenate([a_t * m0, a_t * m1, r_t * m0, r_t * m1], axis=0), jnp.concatenate(bk, axis=0), NT)
          for _, a_t, r_t, bk, _ in pre]
    pre = [(v, a_t, r_t, z, out) for (v, a_t, r_t, _, out), z in zip(pre, zs)]
    heads = [(i, h) for i in range(len(pre)) for h in range(2)]
    za = {ih: pre[ih[0]][3][ih[1] * rows:(ih[1] + 1) * rows] for ih in heads}
    zr = {ih: pre[ih[0]][3][(2 + ih[1]) * rows:(3 + ih[1]) * rows] for ih in heads}
    tinv = dict(zip(heads, _unit_lower_inverses(tuple(za[ih][:, :rows] * stril for ih in heads))))
    lv = {ih: _dot(jnp.concatenate([za[ih][:, rows:] * stril, zr[ih][:, rows:] * tril], axis=0), pre[ih[0]][0])
          for ih in heads}
    wu = {ih: _dot(tinv[ih], jnp.concatenate([pre[ih[0]][1] * masks[ih[1]], lv[ih][:rows]], axis=1)) for ih in heads}
    w_m = {ih: wu[ih][:, :LANES] for ih in heads}
    u_m = {ih: masks[ih[1]] * wu[ih][:, LANES:] for ih in heads}
    qy = {ih: _dot(zr[ih][:, :rows] * tril, jnp.concatenate([w_m[ih], u_m[ih]], axis=1)) for ih in heads}
    outs = []
    for i in range(len(pre)):
        a, b = (i, 0), (i, 1)
        W = w_m[a] + w_m[b]
        U = u_m[a] + u_m[b]
        Q = pre[i][2] + qy[a][:, :LANES] + qy[b][:, :LANES]
        Y0 = qy[a][:, LANES:] + qy[b][:, LANES:] + m0 * lv[a][rows:] + m1 * lv[b][rows:]
        outs.append((W, U, Q, Y0) + pre[i][4])
    return tuple(outs)


N_POST_IN = 8


def _rw_post(blockdiag, *flat):
    inv_n = 1.0 / RW_HEAD_DIM
    pairs = [flat[N_POST_IN * i:N_POST_IN * (i + 1)] for i in range(len(flat) // N_POST_IN)]
    sums = [(_dot(y, blockdiag), _dot(r * k2 * r_k, blockdiag)) for y, r, _, k2, _, r_k, _, _ in pairs]
    centred = [p[0] - s[0] * inv_n for p, s in zip(pairs, sums)]
    variances = [_dot(yc * yc, blockdiag) * inv_n for yc in centred]
    return tuple((yc * lax.rsqrt(var + RW_GN_EPS) * ln_w + ln_b + s[1] * v) * g
                 for (_, _, v, _, g, _, ln_w, ln_b), s, yc, var in zip(pairs, sums, centred, variances))


N_RW_VEC = 7
N_RW_MAT = 3


def _rw_specs(T, tb, rev):
    nT = T // tb
    tix = (lambda t: nT - 1 - t) if rev else (lambda t: t)
    wide = lambda blk: pl.BlockSpec((tb, RW_WIDTH), lambda t: (tix(t), blk))
    narrow = lambda blk: pl.BlockSpec((tb, LANES), lambda t: (tix(t), blk))
    vec = pl.BlockSpec((1, RW_WIDTH), lambda t: (0, 0))
    mat = pl.BlockSpec((RW_PAIRS, LANES, LANES), lambda t: (0, 0, 0))
    st = pl.BlockSpec((RW_PAIRS, tb // RW_CHUNK, LANES, LANES), lambda t: (0, tix(t), 0, 0))
    lora0 = 3 * RW_WIDTH // LANES
    ins = [wide(0), wide(1), wide(2), narrow(lora0), narrow(lora0 + 1)]
    return nT, wide, vec, mat, st, ins


def _rw_prep_args(p, r_ref, k_ref, v_ref, lw_ref, gd_ref, vrefs, mrefs):
    ln = _head_lanes(p)
    w0, a0, k_k, k_a = [x[:, ln] for x in vrefs[:4]]
    return (r_ref[:, ln], k_ref[:, ln], v_ref[:, ln], lw_ref[...], gd_ref[...], w0, a0, k_k, k_a,
            *[x[p] for x in mrefs])


def _stack_chunks(ref, top, bottom):
    C = RW_CHUNK
    for c in range(ref.shape[0]):
        ref[c, 0:C, :] = top[c * C:(c + 1) * C]
        ref[c, C:2 * C, :] = bottom[c * C:(c + 1) * C]


def _group_args(p0, r_ref, k_ref, v_ref, lw_ref, gd_ref, vrefs, mrefs):
    flat = []
    for p in range(p0, p0 + RW_GROUP):
        flat += list(_rw_prep_args(p, r_ref, k_ref, v_ref, lw_ref, gd_ref, vrefs, mrefs))
    return flat


def _rw_fwd(rws, vecs, mats, name):
    T = rws.shape[0]
    tb = _row_tile(T, SCAN_ROWS)
    nsub = tb // RW_CHUNK
    C = RW_CHUNK
    nT, wide, vec, mat, st, ins = _rw_specs(T, tb, False)

    def body(*refs):
        r_ref, k_ref, v_ref, lw_ref, gd_ref = refs[:5]
        vrefs = refs[5:5 + N_RW_VEC]
        mrefs = refs[5 + N_RW_VEC:5 + N_RW_VEC + N_RW_MAT]
        o_ref, st_ref, s_ref, wq_ref, uy_ref, bk_ref, misc_ref, y_ref = refs[-8:]

        @pl.when(pl.program_id(0) == 0)
        def _():
            s_ref[...] = jnp.zeros_like(s_ref)

        consts = _rw_consts(tb)
        blockdiag = consts[3]
        for p0 in range(0, RW_PAIRS, RW_GROUP):
            outs = _rw_prep(consts, *_group_args(p0, r_ref, k_ref, v_ref, lw_ref, gd_ref, vrefs, mrefs))
            for p, (W, U, Q, Y0, Bg, Kg, dec, k2, g) in zip(range(p0, p0 + RW_GROUP), outs):
                _stack_chunks(wq_ref.at[p], W, Q)
                _stack_chunks(uy_ref.at[p], U, Y0)
                _stack_chunks(bk_ref.at[p], Bg, Kg)
                misc_ref[0, p], misc_ref[1, p], misc_ref[2, p] = dec, k2, g

        def step(c, carry):
            rows = _chunk_rows(c, C)
            states = [s_ref[p] for p in range(RW_PAIRS)]
            for p, S in enumerate(states):
                st_ref[p, c] = S
            pys = [_dot(wq_ref[p, c], S, NT) + uy_ref[p, c] for p, S in enumerate(states)]
            pvs = [jnp.concatenate([py[:C], v_ref[rows, _head_lanes(p)]], axis=0) for p, py in enumerate(pys)]
            updates = [_dot(pv, bk_ref[p, c], TN) for p, pv in enumerate(pvs)]
            for p, S in enumerate(states):
                y_ref[p, rows, :] = pys[p][C:]
                s_ref[p] = (S * misc_ref[0, p, pl.ds(c * C, 1), :] + updates[p]) * blockdiag
            return carry

        lax.fori_loop(0, nsub, step, 0)
        flat = []
        for p in range(RW_PAIRS):
            ln = _head_lanes(p)
            flat += [y_ref[p], r_ref[:, ln], v_ref[:, ln], misc_ref[1, p], misc_ref[2, p]] + [x[:, ln] for x in vrefs[4:]]
        for p, out in enumerate(_rw_post(blockdiag, *flat)):
            o_ref[:, _head_lanes(p)] = out

    stacked = pltpu.VMEM((RW_PAIRS, nsub, 2 * C, LANES), F32)
    return pl.pallas_call(
        body, name=name, grid=(nT,),
        in_specs=ins + [vec] * N_RW_VEC + [mat] * N_RW_MAT,
        out_specs=[wide(0), st],
        out_shape=[jax.ShapeDtypeStruct((T, RW_WIDTH), F32),
                   jax.ShapeDtypeStruct((RW_PAIRS, T // RW_CHUNK, LANES, LANES), F32)],
        scratch_shapes=[pltpu.VMEM((RW_PAIRS, LANES, LANES), F32), stacked, stacked, stacked,
                        pltpu.VMEM((3, RW_PAIRS, tb, LANES), F32), pltpu.VMEM((RW_PAIRS, tb, LANES), F32)],
    )(rws, rws, rws, rws, rws, *vecs, *mats)


def _rw_bwd(rws, states, do, do_blk, vecs, mats, name):
    T = rws.shape[0]
    tb = _row_tile(T, SCAN_ROWS)
    nsub = tb // RW_CHUNK
    C = RW_CHUNK
    G = RW_GROUP
    nT, wide, vec, mat, st, ins = _rw_specs(T, tb, True)
    nin = 5 + 1 + 1 + N_RW_VEC + N_RW_MAT

    def body(*refs):
        r_ref, k_ref, v_ref, lw_ref, gd_ref = refs[:5]
        st_ref, do_ref = refs[5], refs[6]
        vrefs = refs[7:7 + N_RW_VEC]
        mrefs = refs[7 + N_RW_VEC:nin]
        dr_ref, dk_ref, dv_ref, dlo_ref = refs[nin:nin + 4]
        dvec = refs[nin + 4:nin + 4 + N_RW_VEC]
        dmat = refs[nin + 4 + N_RW_VEC:nin + 4 + N_RW_VEC + N_RW_MAT]
        ds_ref, wq_ref, uy_ref, bk_ref, pv_ref, dec_ref, y_ref, dpre_ref, dvs_ref = refs[-9:]

        @pl.when(pl.program_id(0) == 0)
        def _():
            ds_ref[...] = jnp.zeros_like(ds_ref)
            for x in dvec + dmat:
                x[...] = jnp.zeros_like(x)

        consts = _rw_consts(tb)
        blockdiag = consts[3]
        dlw, dgd = 0.0, 0.0
        for p0 in range(0, RW_PAIRS, G):
            outs, prep_vjp = jax.vjp(functools.partial(_rw_prep, consts),
                                     *_group_args(p0, r_ref, k_ref, v_ref, lw_ref, gd_ref, vrefs, mrefs))
            for q, (W, U, Q, Y0, Bg, Kg, dec, _, _) in enumerate(outs):
                _stack_chunks(wq_ref.at[q], W, Q)
                _stack_chunks(uy_ref.at[q], U, Y0)
                _stack_chunks(bk_ref.at[q], Bg, Kg)
                dec_ref[q] = dec

            def redo(c, carry, p0=p0):
                rows = _chunk_rows(c, C)
                pys = [_dot(wq_ref[q, c], st_ref[p0 + q, c], NT) + uy_ref[q, c] for q in range(G)]
                for q, py in enumerate(pys):
                    y_ref[q, rows, :] = py[C:]
                    pv_ref[q, c, 0:C, :] = py[:C]
                    pv_ref[q, c, C:2 * C, :] = v_ref[rows, _head_lanes(p0 + q)]
                return carry

            lax.fori_loop(0, nsub, redo, 0)
            flat = []
            for q in range(G):
                ln = _head_lanes(p0 + q)
                flat += [y_ref[q], r_ref[:, ln], v_ref[:, ln], outs[q][7], outs[q][8]] + [x[:, ln] for x in vrefs[4:]]
            _, post_vjp = jax.vjp(functools.partial(_rw_post, blockdiag), *flat)
            post_grads = post_vjp(tuple(do_ref[:, _head_lanes(p0 + q)] for q in range(G)))
            post = []
            for q in range(G):
                ln = _head_lanes(p0 + q)
                dy, dr2, dv2, dk2, dg, dr_k, dln_w, dln_b = post_grads[N_POST_IN * q:N_POST_IN * (q + 1)]
                dpre_ref[q, 3] = dy
                dvs_ref[q] = dv2
                for x, gx in zip(dvec[4:], (dr_k, dln_w, dln_b)):
                    x[:, ln] += gx
                dpre_ref[q, 6] = jnp.zeros_like(dpre_ref[q, 6])
                post.append((dr2, dk2, dg))

            def step(i, carry, p0=p0):
                c = nsub - 1 - i
                rows = _chunk_rows(c, C)
                row0 = pl.ds(c * C, 1)
                qs = range(G)
                Gs = [ds_ref[p0 + q] * blockdiag for q in qs]
                Ss = [st_ref[p0 + q, c] for q in qs]
                t1 = [_dot(bk_ref[q, c], Gs[q], NT) for q in qs]
                t3 = [_dot(pv_ref[q, c], Gs[q]) for q in qs]
                dpy = [jnp.concatenate([t1[q][:C], dpre_ref[q, 3, rows, :]], axis=0) for q in qs]
                t2 = [_dot(dpy[q], Ss[q]) for q in qs]
                back = [_dot(dpy[q], wq_ref[q, c], TN) for q in qs]
                for q in qs:
                    dvs_ref[q, rows, :] += t1[q][C:]
                    dpre_ref[q, 0, rows, :] = t2[q][:C]
                    dpre_ref[q, 1, rows, :] = t1[q][:C]
                    dpre_ref[q, 2, rows, :] = t2[q][C:]
                    dpre_ref[q, 4, rows, :] = t3[q][:C]
                    dpre_ref[q, 5, rows, :] = t3[q][C:]
                    dpre_ref[q, 6, row0, :] = jnp.sum(Ss[q] * Gs[q], axis=0, keepdims=True)
                    ds_ref[p0 + q] = Gs[q] * dec_ref[q, row0, :] + back[q]
                return carry

            lax.fori_loop(0, nsub, step, 0)
            grads = prep_vjp(tuple(tuple(dpre_ref[q, i] for i in range(7)) + post[q][1:] for q in range(G)))
            for q in range(G):
                ln = _head_lanes(p0 + q)
                gq = grads[N_PREP_IN * q:N_PREP_IN * (q + 1)]
                dr_ref[:, ln] = gq[0] + post[q][0]
                dk_ref[:, ln] = gq[1]
                dv_ref[:, ln] = gq[2] + dvs_ref[q]
                dlw = dlw + gq[3]
                dgd = dgd + gq[4]
                for x, gx in zip(dvec[:4], gq[5:9]):
                    x[:, ln] += gx
                for x, gx in zip(dmat, gq[9:]):
                    x[p0 + q] += gx
        dlo_ref[:, 0:LANES] = dlw
        dlo_ref[:, LANES:2 * LANES] = dgd

    dcol = jax.ShapeDtypeStruct((T, RW_WIDTH), F32)
    dlo_spec = pl.BlockSpec((tb, 2 * LANES), lambda t: (nT - 1 - t, 0))
    blk = pltpu.VMEM((G, tb, LANES), F32)
    stacked = pltpu.VMEM((G, nsub, 2 * C, LANES), F32)
    return pl.pallas_call(
        body, name=name, grid=(nT,),
        in_specs=ins + [st, wide(do_blk)] + [vec] * N_RW_VEC + [mat] * N_RW_MAT,
        out_specs=[wide(0)] * 3 + [dlo_spec] + [vec] * N_RW_VEC + [mat] * N_RW_MAT,
        out_shape=[dcol] * 3 + [jax.ShapeDtypeStruct((T, 2 * LANES), F32)]
        + [jax.ShapeDtypeStruct((1, RW_WIDTH), F32)] * N_RW_VEC
        + [jax.ShapeDtypeStruct((RW_PAIRS, LANES, LANES), F32)] * N_RW_MAT,
        scratch_shapes=[pltpu.VMEM((RW_PAIRS, LANES, LANES), F32), stacked, stacked, stacked, stacked, blk, blk,
                        pltpu.VMEM((G, 7, tb, LANES), F32), blk],
    )(rws, rws, rws, rws, rws, states, do, *vecs, *mats)


def _my_index():
    return 4 * lax.axis_index("x") + 2 * lax.axis_index("y") + lax.axis_index("c")


def _peer(bits):
    pos = []
    for name, flip in zip(("x", "y", "c"), bits):
        i = lax.axis_index(name)
        pos.append(1 - i if flip else i)
    return tuple(pos)


def _peer_index(bits):
    x, y, c = _peer(bits)
    return 4 * x + 2 * y + c


def _all_gather(shards, name):
    n = len(shards)
    chips = [(1, 0, 0), (0, 1, 0), (1, 1, 0)]
    sib = (0, 0, 1)

    def body(*refs):
        ins, outs = refs[:n], refs[n:2 * n]
        send_sems, recv_sems, local_sems = refs[2 * n:]

        def rows(k, dev):
            r = ins[k].shape[0]
            return outs[k].at[pl.ds(dev * r, r), :]

        def copy(k, slot, block_dev, to_bits, src=None):
            return pltpu.make_async_remote_copy(
                src_ref=rows(k, block_dev) if src is None else src, dst_ref=rows(k, block_dev),
                send_sem=send_sems.at[k, slot], recv_sem=recv_sems.at[k, slot],
                device_id=_peer(to_bits), device_id_type=MESH_ID)

        me = _my_index()
        started = []
        for k in range(n):
            mine = pltpu.make_async_copy(ins[k], rows(k, me), local_sems.at[k])
            mine.start()
            started.append(mine)
        sends = []
        for k in range(n):
            first = [copy(k, 0, me, sib, src=ins[k])]
            first += [copy(k, 1 + j, me, chip, src=ins[k]) for j, chip in enumerate(chips)]
            for cp in first:
                cp.start()
            sends += first
        for k in range(n):
            for j, chip in enumerate(chips):
                copy(k, 1 + j, _peer_index(chip), chip).wait_recv()
                fwd = copy(k, 4 + j, _peer_index(chip), sib)
                fwd.start()
                sends.append(fwd)
        for k in range(n):
            copy(k, 0, _peer_index(sib), sib).wait_recv()
            for j, chip in enumerate(chips):
                both = (chip[0], chip[1], 1)
                copy(k, 4 + j, _peer_index(both), sib).wait_recv()
        for cp in sends:
            cp.wait_send()
        for cp in started:
            cp.wait()

    any_spec = pl.BlockSpec(memory_space=pl.ANY)
    return pl.pallas_call(
        body, name=name,
        in_specs=[any_spec] * n, out_specs=[any_spec] * n,
        out_shape=[jax.ShapeDtypeStruct((N_DEV * s.shape[0], s.shape[1]), s.dtype) for s in shards],
        scratch_shapes=[pltpu.SemaphoreType.DMA((n, 7)), pltpu.SemaphoreType.DMA((n, 7)),
                        pltpu.SemaphoreType.DMA((n,))],
    )(*shards)


def _exchange(partials, name):
    n = len(partials)
    flips = [(dx, dy, dc) for dx in (0, 1) for dy in (0, 1) for dc in (0, 1)][1:]

    def body(*refs):
        ins, outs = refs[:n], refs[n:2 * n]
        send_sems, recv_sems, local_sems = refs[2 * n:]
        me = _my_index()
        local = []
        for k in range(n):
            cp = pltpu.make_async_copy(ins[k].at[me], outs[k].at[me], local_sems.at[k])
            cp.start()
            local.append(cp)
        copies = []
        for k in range(n):
            for d, bits in enumerate(flips):
                cp = pltpu.make_async_remote_copy(
                    src_ref=ins[k].at[_peer_index(bits)], dst_ref=outs[k].at[me],
                    send_sem=send_sems.at[k, d], recv_sem=recv_sems.at[k, d],
                    device_id=_peer(bits), device_id_type=MESH_ID)
                cp.start()
                copies.append(cp)
        for cp in copies:
            cp.wait_recv()
        for cp in copies:
            cp.wait_send()
        for cp in local:
            cp.wait()

    any_spec = pl.BlockSpec(memory_space=pl.ANY)
    return pl.pallas_call(
        body, name=name,
        in_specs=[any_spec] * n, out_specs=[any_spec] * n,
        out_shape=[jax.ShapeDtypeStruct(p.shape, p.dtype) for p in partials],
        scratch_shapes=[pltpu.SemaphoreType.DMA((n, 7)), pltpu.SemaphoreType.DMA((n, 7)),
                        pltpu.SemaphoreType.DMA((n,))],
    )(*partials)


HBM_SPEC = pl.BlockSpec(memory_space=pltpu.HBM)
SEM_SPEC = pl.BlockSpec(memory_space=pltpu.SEMAPHORE)
ALL_FLIPS = [(dx, dy, dc) for dx in (0, 1) for dy in (0, 1) for dc in (0, 1)][1:]


def _spread_copies(srcs, lands, send_sems, recv_sems, to_x):
    me = _my_index()
    my_x = lax.axis_index("x")
    copies = []
    for k, land in enumerate(lands):
        for d, bits in enumerate(ALL_FLIPS):
            if not srcs:
                src = land.at[me]
            elif to_x is None:
                src = srcs[k].at[_peer_index(bits)]
            else:
                _, py, pc = _peer(bits)
                src = srcs[k].at[2 * py + pc]
            cp = pltpu.make_async_remote_copy(
                src_ref=src, dst_ref=land.at[me],
                send_sem=send_sems.at[k * 7 + d], recv_sem=recv_sems.at[k * 7 + d],
                device_id=_peer(bits), device_id_type=MESH_ID)
            sends = True if to_x is None else my_x == (to_x ^ bits[0])
            receives = True if to_x is None else my_x == to_x
            copies.append((cp, sends, receives))
    return copies


def _when(cond, fn):
    if cond is True:
        fn()
    else:
        pl.when(cond)(fn)


def _spread_start(srcs, lands, name, to_x=None):
    ns, n = len(srcs), len(lands)

    def body(*refs):
        src_refs, land_refs = refs[:ns], refs[ns:ns + n]
        send_sems, recv_sems = refs[ns + n], refs[ns + n + 1]
        token = refs[-1]
        for cp, sends, _ in _spread_copies(src_refs, land_refs, send_sems, recv_sems, to_x):
            _when(sends, cp.start)
        token[...] = jnp.zeros_like(token)

    bufs = list(srcs) + list(lands)
    out = pl.pallas_call(
        body, name=name,
        out_shape=(pltpu.SemaphoreType.DMA((7 * n,)), pltpu.SemaphoreType.DMA((7 * n,)),
                   *[pltpu.HBM(b.shape, b.dtype) for b in bufs], jax.ShapeDtypeStruct((8, LANES), F32)),
        in_specs=[HBM_SPEC] * (ns + n),
        out_specs=(SEM_SPEC, SEM_SPEC, *[HBM_SPEC] * (ns + n), pl.BlockSpec(memory_space=pltpu.VMEM)),
        input_output_aliases={i: 2 + i for i in range(ns + n)},
        compiler_params=pltpu.CompilerParams(has_side_effects=pltpu.SideEffectType.DATAFLOW_SIDE_EFFECTING),
    )(*[pltpu.with_memory_space_constraint(b, pltpu.HBM) for b in bufs])
    return out[0], out[1], list(out[2:2 + ns]), list(out[2 + ns:2 + ns + n]), out[-1]


def _spread_wait(send_sems, recv_sems, srcs, lands, after, name, to_x=None):
    ns, n = len(srcs), len(lands)

    def body(*refs):
        src_refs, land_refs = refs[:ns], refs[ns:ns + n]
        send_sems, recv_sems = refs[ns + n], refs[ns + n + 1]
        for cp, sends, receives in _spread_copies(src_refs, land_refs, send_sems, recv_sems, to_x):
            _when(sends, cp.wait_send)
            _when(receives, cp.wait_recv)

    bufs = list(srcs) + list(lands)
    out = pl.pallas_call(
        body, name=name,
        out_shape=tuple(pltpu.HBM(b.shape, b.dtype) for b in bufs),
        in_specs=[HBM_SPEC] * (ns + n) + [SEM_SPEC, SEM_SPEC, pl.BlockSpec(memory_space=pl.ANY)],
        out_specs=tuple([HBM_SPEC] * (ns + n)),
        input_output_aliases={i: i for i in range(ns + n)},
        compiler_params=pltpu.CompilerParams(has_side_effects=pltpu.SideEffectType.DATAFLOW_SIDE_EFFECTING),
    )(*bufs, send_sems, recv_sems, after)
    return list(out[ns:])


def _own_slot_only(block, me):
    return lax.dynamic_update_slice(lax.empty((N_DEV,) + block.shape, block.dtype), block[None], (me, 0, 0))


def _sum_slots(landed, name):
    _, R, C = landed.shape
    tb = _row_tile(R, 128)

    def body(l_ref, o_ref):
        acc = l_ref[0].astype(F32)
        for s in range(1, N_DEV):
            acc = acc + l_ref[s].astype(F32)
        o_ref[...] = acc

    return pl.pallas_call(
        body, name=name, grid=(R // tb,),
        in_specs=[pl.BlockSpec((N_DEV, tb, C), lambda i: (0, i, 0))],
        out_specs=pl.BlockSpec((tb, C), lambda i: (i, 0)),
        out_shape=jax.ShapeDtypeStruct((R, C), F32),
    )(landed)


def _pack_rows(flat_list, width=LANES):
    flat = jnp.concatenate([a.reshape(-1) for a in flat_list])
    n = flat.shape[0]
    rows = -(-n // width)
    rows = -(-rows // 8) * 8
    return jnp.pad(flat, (0, rows * width - n)).reshape(rows, width)


def _unpack(packed, shapes):
    flat = packed.reshape(-1)
    out, off = [], 0
    for s in shapes:
        n = 1
        for d in s:
            n *= d
        out.append(flat[off:off + n].reshape(s))
        off += n
    return out


def kernel(x, norm1_w, w_in, hg_lb_logits, hg_norm_w, rw_shift_mu, rw_w0, rw_w2, rw_a0, rw_a2, rw_g2, rw_k_k, rw_k_a, rw_r_k, rw_ln_w, rw_ln_b, w_out, norm2_w, w_up, conv_w, conv_b, w_down, final_norm_w, loss_target, m_norm1_w, m_w_in, m_hg_lb_logits, m_hg_norm_w, m_rw_shift_mu, m_rw_w0, m_rw_w2, m_rw_a0, m_rw_a2, m_rw_g2, m_rw_k_k, m_rw_k_a, m_rw_r_k, m_rw_ln_w, m_rw_ln_b, m_w_out, m_norm2_w, m_w_up, m_conv_w, m_conv_b, m_w_down, m_final_norm_w, v_norm1_w, v_w_in, v_hg_lb_logits, v_hg_norm_w, v_rw_shift_mu, v_rw_w0, v_rw_w2, v_rw_a0, v_rw_a2, v_rw_g2, v_rw_k_k, v_rw_k_a, v_rw_r_k, v_rw_ln_w, v_rw_ln_b, v_w_out, v_norm2_w, v_w_up, v_conv_w, v_conv_b, v_w_down, v_final_norm_w):
    weights = dict(norm1_w=norm1_w, w_in=w_in, hg_lb_logits=hg_lb_logits, hg_norm_w=hg_norm_w,
                   rw_shift_mu=rw_shift_mu, rw_w0=rw_w0, rw_w2=rw_w2, rw_a0=rw_a0, rw_a2=rw_a2, rw_g2=rw_g2,
                   rw_k_k=rw_k_k, rw_k_a=rw_k_a, rw_r_k=rw_r_k, rw_ln_w=rw_ln_w, rw_ln_b=rw_ln_b, w_out=w_out,
                   norm2_w=norm2_w, w_up=w_up, conv_w=conv_w, conv_b=conv_b, w_down=w_down,
                   final_norm_w=final_norm_w)
    m_in = dict(norm1_w=m_norm1_w, w_in=m_w_in, hg_lb_logits=m_hg_lb_logits, hg_norm_w=m_hg_norm_w,
                rw_shift_mu=m_rw_shift_mu, rw_w0=m_rw_w0, rw_w2=m_rw_w2, rw_a0=m_rw_a0, rw_a2=m_rw_a2,
                rw_g2=m_rw_g2, rw_k_k=m_rw_k_k, rw_k_a=m_rw_k_a, rw_r_k=m_rw_r_k, rw_ln_w=m_rw_ln_w,
                rw_ln_b=m_rw_ln_b, w_out=m_w_out, norm2_w=m_norm2_w, w_up=m_w_up, conv_w=m_conv_w,
                conv_b=m_conv_b, w_down=m_w_down, final_norm_w=m_final_norm_w)
    v_in = dict(norm1_w=v_norm1_w, w_in=v_w_in, hg_lb_logits=v_hg_lb_logits, hg_norm_w=v_hg_norm_w,
                rw_shift_mu=v_rw_shift_mu, rw_w0=v_rw_w0, rw_w2=v_rw_w2, rw_a0=v_rw_a0, rw_a2=v_rw_a2,
                rw_g2=v_rw_g2, rw_k_k=v_rw_k_k, rw_k_a=v_rw_k_a, rw_r_k=v_rw_r_k, rw_ln_w=v_rw_ln_w,
                rw_ln_b=v_rw_ln_b, w_out=v_w_out, norm2_w=v_norm2_w, w_up=v_w_up, conv_w=v_conv_w,
                conv_b=v_conv_b, w_down=v_w_down, final_norm_w=v_final_norm_w)
    names = list(weights)
    sharded_small = ["rw_w2", "rw_a2", "rw_g2", "conv_w"]
    replicated = [n for n in names if n not in sharded_small + ["w_in", "w_out", "w_up", "w_down"]]

    xs = x[0]
    tgt = loss_target[0]

    small_shard = _pack_rows([weights[n] for n in sharded_small])
    g_win_t, g_small = _all_gather([w_in[0].T.astype(BF16), small_shard], "gather_weights")
    me = _my_index()
    later = (w_up[0].T.astype(BF16), w_out[0].astype(BF16), w_down[0].astype(BF16))
    later, _ = lax.optimization_barrier((later, g_small))
    later = [_own_slot_only(z, me) for z in later]
    g_send, g_recv, _, later, g_token = _spread_start([], later, "gather_later_start")
    small_shapes = [weights[n].shape for n in sharded_small]
    per_dev = [_unpack(g_small.reshape(N_DEV, -1)[j], small_shapes) for j in range(N_DEV)]
    w2_full, a2_full, g2_full, convw_full = [jnp.concatenate([per_dev[j][i][0] for j in range(N_DEV)], axis=-1)
                                             for i in range(4)]
    zeros64 = jnp.zeros((RW_PAIRS, 64, LANES), F32)
    by_pair = lambda z: z.reshape(z.shape[0], RW_PAIRS, LANES).transpose(1, 0, 2)
    w2p = jnp.concatenate([by_pair(w2_full), zeros64], axis=1)
    a2p = jnp.concatenate([zeros64, by_pair(a2_full)], axis=1)
    g2p = by_pair(g2_full)

    l0, l1 = hg_lb_logits[0:1], hg_lb_logits[1:2]
    h1 = _rms_fwd(xs, norm1_w + g_token[0:1, 0:1], "norm1")
    proj = _mm_nt(h1, g_win_t, "proj_in")
    o_hg, hg_states = _hg_fwd(proj, l0, l1, hg_norm_w, "hgrn2_fwd")
    rws = _shift_fwd(proj, rw_shift_mu, "token_shift")
    rw_vecs = [rw_w0, rw_a0, rw_k_k, rw_k_a, rw_r_k, rw_ln_w, rw_ln_b]
    rw_mats = [w2p, a2p, g2p]
    o_rw, rw_states = _rw_fwd(rws, rw_vecs, rw_mats, "rwkv7_fwd")
    o_mix = jnp.concatenate([o_hg, o_rw], axis=-1).astype(BF16)
    g_wup_t, g_wout, g_wdown = [z.reshape(-1, z.shape[-1])
                                for z in _spread_wait(g_send, g_recv, [], later, o_mix, "gather_later_wait")]
    x1 = _mm_nn(o_mix, g_wout, xs, "proj_out")
    h2 = _rms_fwd(x1, norm2_w, "norm2")
    u = _mm_nt(h2, g_wup_t, "ffn_up")
    act = _ffn_act_fwd(u, convw_full, conv_b, "ffn_act")
    loss_part, dx2, d_final_w = _mm_nn_loss_head(act, g_wdown, x1, final_norm_w.reshape(1, -1), tgt,
                                                 "ffn_down_loss_head")

    d_wdown = _mm_tn(act, dx2, 1408, "ffn_down_dw", BF16)
    dact = _mm_nt(dx2, g_wdown, "ffn_down_dx", BF16)
    du_g, du_v, dcw_g, dcw_v, dcb_g, dcb_v = _ffn_act_bwd(u, dact, convw_full, conv_b, "ffn_act_bwd")
    d_convw = jnp.concatenate([dcw_g, dcw_v], axis=-1)
    d_convb = jnp.concatenate([dcb_g, dcb_v], axis=-1)
    d_wup_t = jnp.concatenate([_mm_tn(du_g, h2, 1408, "ffn_up_dw_gate", BF16),
                               _mm_tn(du_v, h2, 1408, "ffn_up_dw_value", BF16)], axis=0)
    dx1, d_norm2 = _mm_nn_rms_bwd([du_g, du_v], g_wup_t, x1, norm2_w, dx2, "ffn_up_dx_norm2_bwd")
    d_wout = _mm_tn(o_mix, dx1, 512, "proj_out_dw", BF16)
    do = _mm_nt(dx1, g_wout, "proj_out_dx")
    early = [z.reshape(N_DEV, z.shape[0] // N_DEV, z.shape[1]) for z in (d_wup_t, d_wout, d_wdown)]
    early_land = [_own_slot_only(lax.dynamic_index_in_dim(z, me, 0, keepdims=False), me) for z in early]
    e_send, e_recv, early, early_land, e_token = _spread_start(early, early_land, "exchange_early_start")
    hg_norm_w_t = hg_norm_w + e_token[0:1, 0:1]
    dq, df, di, dg, d_l0, d_l1, d_hg_nw = _hg_bwd(proj, hg_states, do, 0, l0, l1, hg_norm_w_t, "hgrn2_bwd")
    half = N_DEV // 2
    own_half_block = lambda z: _own_slot_only(lax.dynamic_index_in_dim(z, me % half, 0, keepdims=False), me)
    n_lo = half * w_in.shape[2]
    d_win_lo = _mm_tn(jnp.concatenate([dq, df, di, dg[:, :n_lo - 3 * HG_WIDTH]], axis=-1), h1, 640,
                      "proj_in_dw_low", BF16).reshape(half, -1, D_MODEL)
    m_send, m_recv, mid, mid_land, m_token = _spread_start([d_win_lo], [own_half_block(d_win_lo)],
                                                            "exchange_mid_start", to_x=0)
    rw_vecs_t = [rw_vecs[0] + m_token[0:1, 0:1]] + rw_vecs[1:]
    rw_out = _rw_bwd(rws, rw_states, do, 1, rw_vecs_t, rw_mats, "rwkv7_bwd")
    d_rw_vecs = rw_out[4:4 + N_RW_VEC]
    d_w2p, d_a2p, d_g2p = rw_out[4 + N_RW_VEC:]
    dp_parts, dmu_parts = [], []
    for i, z in enumerate(rw_out[:4]):
        dp, dmu = _shift_bwd(z, proj, rw_shift_mu, i * RW_WIDTH, "token_shift_bwd_%d" % i)
        dp_parts.append(dp)
        dmu_parts.append(dmu)
    d_mu = jnp.concatenate(dmu_parts, axis=-1)
    d_win_hi = _mm_tn(jnp.concatenate([dg[:, n_lo - 3 * HG_WIDTH:]] + dp_parts, axis=-1), h1, 640,
                      "proj_in_dw_high", BF16).reshape(half, -1, D_MODEL)
    from_pairs = lambda z: z.transpose(1, 0, 2).reshape(z.shape[1], RW_WIDTH)
    d_w2 = from_pairs(d_w2p[:, :64])
    d_a2 = from_pairs(d_a2p[:, 64:])
    d_g2 = from_pairs(d_g2p)
    col_blocks = lambda z: z.reshape(z.shape[0], N_DEV, -1).transpose(1, 0, 2)
    small_part = jnp.stack([
        _pack_rows([col_blocks(d_w2)[j], col_blocks(d_a2)[j], col_blocks(d_g2)[j], col_blocks(d_convw)[j]])
        for j in range(N_DEV)])
    l_send, l_recv, late, late_land, l_token = _spread_start([d_win_hi], [own_half_block(d_win_hi)],
                                                             "exchange_late_start", to_x=1)
    grad_x, d_norm1 = _mm_nn_rms_bwd([dq, df, di, dg] + dp_parts, g_win_t, xs, norm1_w + l_token[0:1, 0:1], dx1,
                                     "proj_in_dx_norm1_bwd")

    rep_grads = dict(norm1_w=d_norm1, hg_lb_logits=jnp.concatenate([d_l0, d_l1], axis=0), hg_norm_w=d_hg_nw,
                     rw_shift_mu=d_mu, rw_w0=d_rw_vecs[0], rw_a0=d_rw_vecs[1], rw_k_k=d_rw_vecs[2],
                     rw_k_a=d_rw_vecs[3], rw_r_k=d_rw_vecs[4], rw_ln_w=d_rw_vecs[5], rw_ln_b=d_rw_vecs[6],
                     norm2_w=d_norm2, conv_b=d_convb, final_norm_w=d_final_w)
    rep_pack = _pack_rows([loss_part] + [rep_grads[n] for n in replicated])
    rep_part = jnp.broadcast_to(rep_pack[None], (N_DEV,) + rep_pack.shape)
    grads, delta, new_m, new_v = {}, {}, {}, {}

    def adamw_big(n, g):
        shp = weights[n].shape
        as2d = lambda z: z.reshape(shp[1], shp[2])
        grads[n] = g[None]
        d, nm, nv = _adamw(as2d(weights[n]), g, as2d(m_in[n]), as2d(v_in[n]), "adamw_" + n)
        delta[n], new_m[n], new_v[n] = d.reshape(shp), nm.reshape(shp), nv.reshape(shp)

    landed_early = _spread_wait(e_send, e_recv, early, early_land, grad_x, "exchange_early_wait")
    adamw_big("w_up", _sum_slots(landed_early[0], "sum_grads_w_up").T)
    adamw_big("w_out", _sum_slots(landed_early[1], "sum_grads_w_out"))
    adamw_big("w_down", _sum_slots(landed_early[2], "sum_grads_w_down"))
    (landed_mid,) = _spread_wait(m_send, m_recv, mid, mid_land, grad_x, "exchange_mid_wait", to_x=0)
    (landed_late,) = _spread_wait(l_send, l_recv, late, late_land, delta["w_down"], "exchange_late_wait", to_x=1)
    g_win = jnp.where(lax.axis_index("x") == 0, _sum_slots(landed_mid, "sum_grads_w_in_low"),
                      _sum_slots(landed_late, "sum_grads_w_in_high"))
    adamw_big("w_in", g_win.T)
    landed_rep, landed_small = _exchange([rep_part, small_part], "exchange_grads")
    g_small_sum = _unpack(_sum_slots(landed_small, "sum_grads_small"), small_shapes)
    rep_sum = _unpack(_sum_slots(landed_rep, "sum_grads_replicated"), [(1, 1)] + [weights[n].shape for n in replicated])
    loss = rep_sum[0].reshape(())
    grads.update(dict(zip(replicated, rep_sum[1:])))
    grads.update(dict(zip(sharded_small, g_small_sum)))

    small_names = replicated + sharded_small
    packs = [_pack_rows([src[n] for n in small_names]) for src in (weights, grads, m_in, v_in)]
    outs = _adamw(*packs, "adamw_small")
    small_shapes_all = [weights[n].shape for n in small_names]
    for dst, packed in zip((delta, new_m, new_v), outs):
        dst.update(dict(zip(small_names, _unpack(packed, small_shapes_all))))

    return (loss, grad_x[None], *[grads[n] for n in names], *[delta[n] for n in names],
            *[new_m[n] for n in names], *[new_v[n] for n in names])
```

```python
import functools

import jax
import jax.numpy as jnp
from jax import lax
from jax.experimental import pallas as pl
from jax.experimental.pallas import tpu as pltpu

F32 = jnp.float32
BF16 = jnp.bfloat16
SUM_PRECISION = lax.Precision.HIGH
SCAN_PRECISION = None
MESH_ID = pl.DeviceIdType.MESH

N_DEV = 8
D_MODEL = 1024
HG_WIDTH = 512
HG_HEAD_DIM = 128
HG_HEADS = 4
RW_WIDTH = 512
RW_PAIRS = 4
RW_HEAD_DIM = 64
HG_COLS = 2048
RW_COLS = 1792
D_FF = 2816
NORM_EPS = 1e-6
RW_GN_EPS = 64e-5
L2_EPS = 1e-12
ADAM_LR, ADAM_B1, ADAM_B2, ADAM_EPS, ADAM_WD, ADAM_STEP = 0.001, 0.9, 0.999, 1e-08, 0.01, 10

HG_CHUNK = 32
HG_HALF = 16
RW_CHUNK = 64
SCAN_ROWS = 256
LANES = 128

NN = ((1,), (0,))
NT = ((1,), (1,))
TN = ((0,), (0,))


def _dot(a, b, dims=NN, precision=SCAN_PRECISION):
    if precision is None:
        a, b = a.astype(BF16), b.astype(BF16)
    return lax.dot_general(a, b, (dims, ((), ())), precision=precision, preferred_element_type=F32)


def _iota2(shape, dim):
    return lax.broadcasted_iota(jnp.int32, shape, dim)


def _sigmoid(z):
    return 0.5 * jnp.tanh(0.5 * z) + 0.5


def _row_tile(n, want):
    t = min(n, want)
    while n % t:
        t //= 2
    return t


def _mm_nn_rms_bwd(a, b, x, w, dres, name):
    parts = list(a)
    T, D = x.shape
    K = b.shape[0]
    tm = _row_tile(T, 256)
    widths = [p.shape[1] for p in parts]
    n = len(parts)

    def body(*refs):
        b_ref, x_ref, w_ref, dres_ref, dx_ref, dw_ref = refs[n:]

        @pl.when(pl.program_id(0) == 0)
        def _():
            dw_ref[...] = jnp.zeros_like(dw_ref)

        dy, off = None, 0
        for a_ref, wd in zip(refs[:n], widths):
            d = _dot(a_ref[...].astype(BF16), b_ref[off:off + wd, :].astype(BF16), NN, None)
            dy = d if dy is None else dy + d
            off += wd
        xv = x_ref[...]
        r = lax.rsqrt(jnp.mean(xv * xv, axis=-1, keepdims=True) + NORM_EPS)
        xn = xv * r
        dxn = dy * w_ref[...]
        dx_ref[...] = dres_ref[...] + r * (dxn - xn * jnp.mean(dxn * xn, axis=-1, keepdims=True))
        dw_ref[...] += jnp.sum(dy * xn, axis=0, keepdims=True)

    row = pl.BlockSpec((tm, D), lambda i: (i, 0))
    vec = pl.BlockSpec((1, D), lambda i: (0, 0))
    return pl.pallas_call(
        body, name=name, grid=(T // tm,),
        in_specs=[pl.BlockSpec((tm, wd), lambda i: (i, 0)) for wd in widths]
        + [pl.BlockSpec((K, D), lambda i: (0, 0)), row, vec, row],
        out_specs=[row, vec],
        out_shape=[jax.ShapeDtypeStruct((T, D), F32), jax.ShapeDtypeStruct((1, D), F32)],
    )(*parts, b, x, w, dres)


def _rms_mm_nt(x, w, bt, name):
    T, K = x.shape
    N = bt.shape[0]
    tm = _row_tile(T, 256)

    def body(x_ref, w_ref, b_ref, h_ref, o_ref):
        xv = x_ref[...]
        r = lax.rsqrt(jnp.mean(xv * xv, axis=-1, keepdims=True) + NORM_EPS)
        h = (xv * r * w_ref[...]).astype(BF16)
        h_ref[...] = h
        o_ref[...] = _dot(h, b_ref[...].astype(BF16), NT, None)

    row = pl.BlockSpec((tm, K), lambda i: (i, 0))
    return pl.pallas_call(
        body, name=name, grid=(T // tm,),
        in_specs=[row, pl.BlockSpec((1, K), lambda i: (0, 0)), pl.BlockSpec((N, K), lambda i: (0, 0))],
        out_specs=[row, pl.BlockSpec((tm, N), lambda i: (i, 0))],
        out_shape=[jax.ShapeDtypeStruct((T, K), BF16), jax.ShapeDtypeStruct((T, N), F32)],
    )(x, w, bt)


def _mm_nt(a, bt, name, out_dtype=F32):
    T, K = a.shape
    N = bt.shape[0]
    tm = _row_tile(T, 256)

    def body(a_ref, b_ref, o_ref):
        o_ref[...] = _dot(a_ref[...].astype(BF16), b_ref[...].astype(BF16), NT, None).astype(o_ref.dtype)

    return pl.pallas_call(
        body, name=name, grid=(T // tm,),
        in_specs=[pl.BlockSpec((tm, K), lambda i: (i, 0)), pl.BlockSpec((N, K), lambda i: (0, 0))],
        out_specs=pl.BlockSpec((tm, N), lambda i: (i, 0)),
        out_shape=jax.ShapeDtypeStruct((T, N), out_dtype),
    )(a, bt)


def _mm_nn(a, b, res, name, out_dtype=F32):
    parts = list(a) if isinstance(a, (list, tuple)) else [a]
    T = parts[0].shape[0]
    K, N = b.shape
    tm = _row_tile(T, 256)
    widths = [p.shape[1] for p in parts]
    n = len(parts)

    def body(*refs):
        b_ref, o_ref = refs[n], refs[-1]
        acc, off = None, 0
        for a_ref, w in zip(refs[:n], widths):
            d = _dot(a_ref[...].astype(BF16), b_ref[off:off + w, :].astype(BF16), NN, None)
            acc = d if acc is None else acc + d
            off += w
        if res is not None:
            acc = acc + refs[n + 1][...]
        o_ref[...] = acc.astype(o_ref.dtype)

    in_specs = [pl.BlockSpec((tm, w), lambda i: (i, 0)) for w in widths] + [pl.BlockSpec((K, N), lambda i: (0, 0))]
    args = parts + [b]
    if res is not None:
        in_specs.append(pl.BlockSpec((tm, N), lambda i: (i, 0)))
        args.append(res)
    return pl.pallas_call(
        body, name=name, grid=(T // tm,), in_specs=in_specs,
        out_specs=pl.BlockSpec((tm, N), lambda i: (i, 0)),
        out_shape=jax.ShapeDtypeStruct((T, N), out_dtype),
    )(*args)


def _mm_tn(a, b, tmm, name, out_dtype=F32):
    T, M = a.shape
    N = b.shape[1]
    tk = _row_tile(T, 512)
    nk = T // tk

    def body(a_ref, b_ref, o_ref, acc_ref):
        @pl.when(pl.program_id(1) == 0)
        def _():
            acc_ref[...] = jnp.zeros_like(acc_ref)

        acc_ref[...] += _dot(a_ref[...].astype(BF16), b_ref[...].astype(BF16), TN, None)

        @pl.when(pl.program_id(1) == nk - 1)
        def _():
            o_ref[...] = acc_ref[...].astype(o_ref.dtype)

    return pl.pallas_call(
        body, name=name, grid=(M // tmm, nk),
        in_specs=[pl.BlockSpec((tk, tmm), lambda m, k: (k, m)), pl.BlockSpec((tk, N), lambda m, k: (k, 0))],
        out_specs=pl.BlockSpec((tmm, N), lambda m, k: (m, 0)),
        out_shape=jax.ShapeDtypeStruct((M, N), out_dtype),
        scratch_shapes=[pltpu.VMEM((tmm, N), F32)],
    )(a, b)


class _RowShifts:
    def __init__(self, shape):
        index = _iota2(shape, 0)
        self.rows = shape[0]
        self.first = {n: index < n for n in (1, 2)}
        self.last = {n: index >= shape[0] - n for n in (1, 2)}

    def down(self, z, n):
        return jnp.where(self.first[n], 0.0, pltpu.roll(z, n, 0))

    def up(self, z, n):
        return jnp.where(self.last[n], 0.0, pltpu.roll(z, self.rows - n, 0))


def _shift_fwd(proj, mu, name):
    T = proj.shape[0]
    nblk = RW_COLS // LANES
    first = HG_COLS // LANES

    def body(p_ref, mu_ref, o_ref):
        p = p_ref[...]
        o_ref[...] = p + (_RowShifts(p.shape).down(p, 1) - p) * mu_ref[...]

    return pl.pallas_call(
        body, name=name, grid=(nblk,),
        in_specs=[pl.BlockSpec((T, LANES), lambda j: (0, first + j)), pl.BlockSpec((1, LANES), lambda j: (0, j))],
        out_specs=pl.BlockSpec((T, LANES), lambda j: (0, j)),
        out_shape=jax.ShapeDtypeStruct((T, RW_COLS), F32),
    )(proj, mu)


def _shift_bwd(ds, proj, mu, col0, name):
    T, width = ds.shape
    nblk = width // LANES
    first = (HG_COLS + col0) // LANES
    mu0 = col0 // LANES

    def body(ds_ref, p_ref, mu_ref, dp_ref, dmu_ref):
        dsv = ds_ref[...]
        p = p_ref[...]
        m = mu_ref[...]
        shifts = _RowShifts(p.shape)
        dp_ref[...] = (dsv * (1.0 - m) + shifts.up(dsv * m, 1)).astype(dp_ref.dtype)
        dmu_ref[...] = jnp.sum(dsv * (shifts.down(p, 1) - p), axis=0, keepdims=True)

    return pl.pallas_call(
        body, name=name, grid=(nblk,),
        in_specs=[pl.BlockSpec((T, LANES), lambda j: (0, j)),
                  pl.BlockSpec((T, LANES), lambda j: (0, first + j)),
                  pl.BlockSpec((1, LANES), lambda j: (0, mu0 + j))],
        out_specs=[pl.BlockSpec((T, LANES), lambda j: (0, j)), pl.BlockSpec((1, LANES), lambda j: (0, j))],
        out_shape=[jax.ShapeDtypeStruct((T, width), BF16), jax.ShapeDtypeStruct((1, width), F32)],
    )(ds, proj, mu)


def _conv3(z, w_ref, shifts):
    return w_ref[0:1, :] * shifts.down(z, 2) + w_ref[1:2, :] * shifts.down(z, 1) + w_ref[2:3, :] * z


def _ffn_act_fwd(u, conv_w, conv_b, name):
    T = u.shape[0]
    nblk = D_FF // LANES

    def body(ug_ref, uv_ref, wg_ref, wv_ref, bg_ref, bv_ref, act_ref):
        shifts = _RowShifts((T, LANES))
        gate = _conv3(ug_ref[...], wg_ref, shifts) + bg_ref[...]
        val = _conv3(uv_ref[...], wv_ref, shifts) + bv_ref[...]
        act_ref[...] = (gate * _sigmoid(gate) * val).astype(act_ref.dtype)

    col = lambda off: pl.BlockSpec((T, LANES), lambda j: (0, off + j))
    wsp = lambda off: pl.BlockSpec((3, LANES), lambda j: (0, off + j))
    bsp = lambda off: pl.BlockSpec((1, LANES), lambda j: (0, off + j))
    return pl.pallas_call(
        body, name=name, grid=(nblk,),
        in_specs=[col(0), col(nblk), wsp(0), wsp(nblk), bsp(0), bsp(nblk)],
        out_specs=pl.BlockSpec((T, LANES), lambda j: (0, j)),
        out_shape=jax.ShapeDtypeStruct((T, D_FF), BF16),
    )(u, u, conv_w, conv_w, conv_b, conv_b)


def _ffn_act_bwd(u, dact, conv_w, conv_b, name):
    T = u.shape[0]
    nblk = D_FF // LANES

    def conv_bwd(z, dzc, w_ref, du_ref, dw_ref, db_ref, shifts):
        up1, up2 = shifts.up(dzc, 1), shifts.up(dzc, 2)
        du = w_ref[2:3, :] * dzc + w_ref[1:2, :] * up1 + w_ref[0:1, :] * up2
        du_ref[...] = du.astype(du_ref.dtype)
        dw_ref[0:1, :] = jnp.sum(up2 * z, axis=0, keepdims=True)
        dw_ref[1:2, :] = jnp.sum(up1 * z, axis=0, keepdims=True)
        dw_ref[2:3, :] = jnp.sum(dzc * z, axis=0, keepdims=True)
        db_ref[...] = jnp.sum(dzc, axis=0, keepdims=True)

    def body(ug_ref, uv_ref, da_ref, wg_ref, wv_ref, bg_ref, bv_ref,
             dug_ref, duv_ref, dwg_ref, dwv_ref, dbg_ref, dbv_ref):
        ug, uv = ug_ref[...], uv_ref[...]
        shifts = _RowShifts((T, LANES))
        gate = _conv3(ug, wg_ref, shifts) + bg_ref[...]
        val = _conv3(uv, wv_ref, shifts) + bv_ref[...]
        da = da_ref[...].astype(F32)
        sg = _sigmoid(gate)
        dgate = da * val * (sg * (1.0 + gate * (1.0 - sg)))
        dval = da * gate * sg
        conv_bwd(ug, dgate, wg_ref, dug_ref, dwg_ref, dbg_ref, shifts)
        conv_bwd(uv, dval, wv_ref, duv_ref, dwv_ref, dbv_ref, shifts)

    col = lambda off: pl.BlockSpec((T, LANES), lambda j: (0, off + j))
    wsp = lambda off: pl.BlockSpec((3, LANES), lambda j: (0, off + j))
    bsp = lambda off: pl.BlockSpec((1, LANES), lambda j: (0, off + j))
    half = lambda r, dt: jax.ShapeDtypeStruct((r, D_FF), dt)
    return pl.pallas_call(
        body, name=name, grid=(nblk,),
        in_specs=[col(0), col(nblk), col(0), wsp(0), wsp(nblk), bsp(0), bsp(nblk)],
        out_specs=[col(0), col(0), wsp(0), wsp(0), bsp(0), bsp(0)],
        out_shape=[half(T, BF16), half(T, BF16), half(3, F32), half(3, F32), half(1, F32), half(1, F32)],
    )(u, u, dact, conv_w, conv_w, conv_b, conv_b)


def _mm_nn_loss_head(a, b, res, w, target, name):
    T, K = a.shape
    D = b.shape[1]
    tm = _row_tile(T, 256)

    def body(a_ref, b_ref, res_ref, w_ref, t_ref, loss_ref, dx_ref, dw_ref):
        @pl.when(pl.program_id(0) == 0)
        def _():
            loss_ref[...] = jnp.zeros_like(loss_ref)
            dw_ref[...] = jnp.zeros_like(dw_ref)

        xv = res_ref[...] + _dot(a_ref[...].astype(BF16), b_ref[...].astype(BF16), NN, None)
        r = lax.rsqrt(jnp.mean(xv * xv, axis=-1, keepdims=True) + NORM_EPS)
        xn = xv * r
        err = xn * w_ref[...] - t_ref[...]
        row_loss = jnp.sum(err * err, axis=-1, keepdims=True) * (0.5 / D)
        loss_ref[...] += jnp.sum(row_loss, axis=0, keepdims=True)
        dy = err * (1.0 / D)
        dxn = dy * w_ref[...]
        dx_ref[...] = r * (dxn - xn * jnp.mean(dxn * xn, axis=-1, keepdims=True))
        dw_ref[...] += jnp.sum(dy * xn, axis=0, keepdims=True)

    row = pl.BlockSpec((tm, D), lambda i: (i, 0))
    vec = pl.BlockSpec((1, D), lambda i: (0, 0))
    return pl.pallas_call(
        body, name=name, grid=(T // tm,),
        in_specs=[pl.BlockSpec((tm, K), lambda i: (i, 0)), pl.BlockSpec((K, D), lambda i: (0, 0)), row, vec, row],
        out_specs=[pl.BlockSpec((1, 1), lambda i: (0, 0)), row, vec],
        out_shape=[jax.ShapeDtypeStruct((1, 1), F32), jax.ShapeDtypeStruct((T, D), F32),
                   jax.ShapeDtypeStruct((1, D), F32)],
    )(a, b, res, w, target)


def _adamw(w, g, m, v, name):
    R, C = w.shape
    tb = _row_tile(R, 256) if R % 8 == 0 else R

    def body(w_ref, g_ref, m_ref, v_ref, d_ref, nm_ref, nv_ref):
        gv = g_ref[...]
        nm = ADAM_B1 * m_ref[...] + (1.0 - ADAM_B1) * gv
        nv = ADAM_B2 * v_ref[...] + (1.0 - ADAM_B2) * (gv * gv)
        m_hat = nm / (1.0 - ADAM_B1 ** ADAM_STEP)
        v_hat = nv / (1.0 - ADAM_B2 ** ADAM_STEP)
        d_ref[...] = -ADAM_LR * (m_hat / (jnp.sqrt(v_hat) + ADAM_EPS) + ADAM_WD * w_ref[...])
        nm_ref[...] = nm
        nv_ref[...] = nv

    blk = pl.BlockSpec((tb, C), lambda i: (i, 0))
    sd = jax.ShapeDtypeStruct((R, C), F32)
    return pl.pallas_call(
        body, name=name, grid=(R // tb,), in_specs=[blk] * 4, out_specs=[blk] * 3, out_shape=[sd] * 3,
    )(w, g, m, v)


def _chunk_masks(rows, chunk):
    shift = chunk.bit_length() - 1
    i, j = _iota2((rows, rows), 0), _iota2((rows, rows), 1)
    same = jnp.right_shift(i, shift) == jnp.right_shift(j, shift)
    return same.astype(F32), (same & (j <= i)).astype(F32), (same & (j < i)).astype(F32)


def _head_lanes(h):
    return slice(h * LANES, (h + 1) * LANES)


def _chunk_rows(c, chunk):
    return pl.ds(pl.multiple_of(c * chunk, chunk), chunk)


def _hg_consts(rows):
    same, tril, _ = _chunk_masks(rows, HG_CHUNK)
    half_same, half_tril, _ = _chunk_masks(rows, HG_HALF)
    i, j = _iota2((rows, rows), 0), _iota2((rows, rows), 1)
    half_shift, shift = HG_HALF.bit_length() - 1, HG_CHUNK.bit_length() - 1
    mid_row = jnp.left_shift(jnp.right_shift(i, half_shift), half_shift) + (HG_HALF // 2 - 1)
    bound_row = jnp.left_shift(jnp.right_shift(i, shift), shift) + (HG_HALF - 1)
    upto_mid = ((same > 0) & (j <= mid_row)).astype(F32)
    upto_bound = ((same > 0) & (j <= bound_row)).astype(F32)
    lower_left = tril * (1.0 - half_same)
    return jnp.concatenate([tril, same, upto_mid, upto_bound], axis=0), half_tril, lower_left


N_HG_IN = 5


def _hg_prep(consts, *flat):
    sums, half_tril, lower_left = consts
    rows = half_tril.shape[0]
    heads, logs = [], []
    for h in range(len(flat) // N_HG_IN):
        qr, fr, ir, l0, l1 = flat[N_HG_IN * h:N_HG_IN * (h + 1)]
        lb = _sigmoid(l0 - l1)
        f = lb + (1.0 - lb) * _sigmoid(fr)
        heads.append((qr * _sigmoid(qr) * (HG_HEAD_DIM ** -0.5), 1.0 - f, ir))
        logs.append(jnp.log(f))
    acc = _dot(sums, jnp.concatenate(logs, axis=1), NN, SUM_PRECISION)
    sums_of = []
    for h in range(len(heads)):
        acc_h = acc[:, h * LANES:(h + 1) * LANES]
        sums_of.append(tuple(acc_h[n * rows:(n + 1) * rows] for n in range(4)))
    near = [_dot(q * jnp.exp(a - mid), k * jnp.exp(mid - a), NT) * half_tril
            for (q, k, _), (a, _, mid, _) in zip(heads, sums_of)]
    far = [_dot(q * jnp.exp(jnp.minimum(a - bound, 0.0)), k * jnp.exp(jnp.minimum(bound - a, 0.0)), NT) * lower_left
           for (q, k, _), (a, _, _, bound) in zip(heads, sums_of)]
    intra = [_dot(n + f, ir) for n, f, (_, _, ir) in zip(near, far, heads)]
    return tuple((q * jnp.exp(a), o_intra, k * jnp.exp(tot - a), jnp.exp(tot))
                 for (q, k, _), (a, tot, _, _), o_intra in zip(heads, sums_of, intra))


def _hg_prep_args(q_ref, f_ref, i_ref, l0_ref, l1_ref):
    flat = []
    for h in range(HG_HEADS):
        ln = _head_lanes(h)
        flat += [q_ref[:, ln], f_ref[:, ln], i_ref[:, ln], l0_ref[:, ln], l1_ref[:, ln]]
    return flat


def _hg_post(o, gr, nw):
    on = o * lax.rsqrt(jnp.mean(o * o, axis=-1, keepdims=True) + NORM_EPS)
    return on * nw * (gr * _sigmoid(gr))


def _hg_specs(T, tb, rev):
    nT = T // tb
    tix = (lambda t: nT - 1 - t) if rev else (lambda t: t)
    col = lambda blk: pl.BlockSpec((tb, HG_WIDTH), lambda t: (tix(t), blk))
    vec = pl.BlockSpec((1, HG_WIDTH), lambda t: (0, 0))
    st = pl.BlockSpec((HG_HEADS, tb // HG_CHUNK, HG_HEAD_DIM, HG_HEAD_DIM), lambda t: (0, tix(t), 0, 0))
    return nT, col, vec, st


def _hg_fwd(proj, l0, l1, nw, name):
    T = proj.shape[0]
    tb = _row_tile(T, SCAN_ROWS)
    nsub = tb // HG_CHUNK
    nT, col, vec, st = _hg_specs(T, tb, False)

    def body(q_ref, f_ref, i_ref, g_ref, l0_ref, l1_ref, nw_ref, o_ref, st_ref, s_ref, qe_ref, kd_ref, dec_ref):
        @pl.when(pl.program_id(0) == 0)
        def _():
            s_ref[...] = jnp.zeros_like(s_ref)

        consts = _hg_consts(tb)
        outs = _hg_prep(consts, *_hg_prep_args(q_ref, f_ref, i_ref, l0_ref, l1_ref))
        for h, (qe, o_intra, kd, dec) in enumerate(outs):
            qe_ref[h], kd_ref[h], dec_ref[h] = qe, kd, dec
            o_ref[:, _head_lanes(h)] = o_intra

        def step(c, carry):
            rows = _chunk_rows(c, HG_CHUNK)
            heads = range(HG_HEADS)
            states = [s_ref[h] for h in heads]
            inter = [_dot(qe_ref[h, rows, :], states[h], NT) for h in heads]
            updates = [_dot(i_ref[rows, _head_lanes(h)], kd_ref[h, rows, :], TN) for h in heads]
            for h in heads:
                st_ref[h, c] = states[h]
                o_ref[rows, _head_lanes(h)] += inter[h]
                s_ref[h] = states[h] * dec_ref[h, pl.ds(c * HG_CHUNK, 1), :] + updates[h]
            return carry

        lax.fori_loop(0, nsub, step, 0)
        for h in range(HG_HEADS):
            ln = _head_lanes(h)
            o_ref[:, ln] = _hg_post(o_ref[:, ln], g_ref[:, ln], nw_ref[:, ln])

    blk = pltpu.VMEM((HG_HEADS, tb, LANES), F32)
    return pl.pallas_call(
        body, name=name, grid=(nT,),
        in_specs=[col(0), col(1), col(2), col(3), vec, vec, vec],
        out_specs=[col(0), st],
        out_shape=[jax.ShapeDtypeStruct((T, HG_WIDTH), F32),
                   jax.ShapeDtypeStruct((HG_HEADS, T // HG_CHUNK, HG_HEAD_DIM, HG_HEAD_DIM), F32)],
        scratch_shapes=[pltpu.VMEM((HG_HEADS, HG_HEAD_DIM, HG_HEAD_DIM), F32), blk, blk, blk],
    )(proj, proj, proj, proj, l0, l1, nw)


def _hg_bwd(proj, states, do, do_blk, l0, l1, nw, name):
    T = proj.shape[0]
    tb = _row_tile(T, SCAN_ROWS)
    nsub = tb // HG_CHUNK
    nT, col, vec, st = _hg_specs(T, tb, True)

    def body(q_ref, f_ref, i_ref, g_ref, st_ref, do_ref, l0_ref, l1_ref, nw_ref,
             dq_ref, df_ref, di_ref, dg_ref, dl0_ref, dl1_ref, dnw_ref,
             ds_ref, qe_ref, kd_ref, dec_ref, o_ref, dqe_ref, dkd_ref, ddec_ref, dis_ref):
        @pl.when(pl.program_id(0) == 0)
        def _():
            ds_ref[...] = jnp.zeros_like(ds_ref)
            dl0_ref[...] = jnp.zeros_like(dl0_ref)
            dl1_ref[...] = jnp.zeros_like(dl1_ref)
            dnw_ref[...] = jnp.zeros_like(dnw_ref)

        consts = _hg_consts(tb)
        outs, prep_vjp = jax.vjp(functools.partial(_hg_prep, consts),
                                 *_hg_prep_args(q_ref, f_ref, i_ref, l0_ref, l1_ref))
        for h, (qe, o_intra, kd, dec) in enumerate(outs):
            qe_ref[h], kd_ref[h], dec_ref[h], o_ref[h] = qe, kd, dec, o_intra

        def redo(c, carry):
            rows = _chunk_rows(c, HG_CHUNK)
            inter = [_dot(qe_ref[h, rows, :], st_ref[h, c], NT) for h in range(HG_HEADS)]
            for h in range(HG_HEADS):
                o_ref[h, rows, :] += inter[h]
            return carry

        lax.fori_loop(0, nsub, redo, 0)
        for h in range(HG_HEADS):
            ln = _head_lanes(h)
            _, vjp = jax.vjp(_hg_post, o_ref[h], g_ref[:, ln], nw_ref[:, ln])
            d_o, dgr, dnw = vjp(do_ref[:, ln])
            o_ref[h] = d_o
            dg_ref[:, ln] = dgr.astype(dg_ref.dtype)
            dnw_ref[:, ln] += dnw
        ddec_ref[...] = jnp.zeros_like(ddec_ref)

        def step(i, carry):
            c = nsub - 1 - i
            rows = _chunk_rows(c, HG_CHUNK)
            row0 = pl.ds(c * HG_CHUNK, 1)
            heads = range(HG_HEADS)
            Gs = [ds_ref[h] for h in heads]
            Ss = [st_ref[h, c] for h in heads]
            d_os = [o_ref[h, rows, :] for h in heads]
            dqe = [_dot(d_os[h], Ss[h]) for h in heads]
            dkd = [_dot(i_ref[rows, _head_lanes(h)], Gs[h]) for h in heads]
            dis = [_dot(kd_ref[h, rows, :], Gs[h], NT) for h in heads]
            back = [_dot(d_os[h], qe_ref[h, rows, :], TN) for h in heads]
            for h in heads:
                dqe_ref[h, rows, :] = dqe[h]
                dkd_ref[h, rows, :] = dkd[h]
                dis_ref[h, rows, :] = dis[h]
                ddec_ref[h, row0, :] = jnp.sum(Ss[h] * Gs[h], axis=0, keepdims=True)
                ds_ref[h] = Gs[h] * dec_ref[h, row0, :] + back[h]
            return carry

        lax.fori_loop(0, nsub, step, 0)
        grads = prep_vjp(tuple((dqe_ref[h], o_ref[h], dkd_ref[h], ddec_ref[h]) for h in range(HG_HEADS)))
        for h in range(HG_HEADS):
            ln = _head_lanes(h)
            dq, df, di, dl0, dl1 = grads[N_HG_IN * h:N_HG_IN * (h + 1)]
            dq_ref[:, ln] = dq.astype(dq_ref.dtype)
            df_ref[:, ln] = df.astype(df_ref.dtype)
            di_ref[:, ln] = (di + dis_ref[h]).astype(di_ref.dtype)
            dl0_ref[:, ln] += dl0
            dl1_ref[:, ln] += dl1

    dcol = jax.ShapeDtypeStruct((T, HG_WIDTH), BF16)
    dvec = jax.ShapeDtypeStruct((1, HG_WIDTH), F32)
    blk = pltpu.VMEM((HG_HEADS, tb, LANES), F32)
    return pl.pallas_call(
        body, name=name, grid=(nT,),
        in_specs=[col(0), col(1), col(2), col(3), st, col(do_blk), vec, vec, vec],
        out_specs=[col(0)] * 4 + [vec] * 3,
        out_shape=[dcol] * 4 + [dvec] * 3,
        scratch_shapes=[pltpu.VMEM((HG_HEADS, HG_HEAD_DIM, HG_HEAD_DIM), F32)] + [blk] * 8,
    )(proj, proj, proj, proj, states, do, l0, l1, nw)


def _rw_consts(rows):
    same, tril, stril = _chunk_masks(rows, RW_CHUNK)
    br, bc = _iota2((LANES, LANES), 0), _iota2((LANES, LANES), 1)
    blockdiag = ((br < RW_HEAD_DIM) == (bc < RW_HEAD_DIM)).astype(F32)
    m0 = (_iota2((1, LANES), 1) < RW_HEAD_DIM).astype(F32)
    return same, tril, stril, blockdiag, m0, 1.0 - m0


def _unit_lower_inverses_impl(lows):
    rows = lows[0].shape[0]
    eye = (_iota2(lows[0].shape, 0) == _iota2(lows[0].shape, 1)).astype(F32)
    xs = [low + eye for low in lows]
    ps = [_dot(low, low) for low in lows]
    n = 4
    while n < RW_CHUNK:
        zs = [_dot(jnp.concatenate([p, x], axis=0), p) for p, x in zip(ps, xs)]
        ps = [z[:rows] for z in zs]
        xs = [x + z[rows:] for x, z in zip(xs, zs)]
        n *= 2
    return tuple(x + _dot(x, p) for x, p in zip(xs, ps))


@jax.custom_vjp
def _unit_lower_inverses(lows):
    return _unit_lower_inverses_impl(lows)


def _unit_lower_inverses_fwd(lows):
    xs = _unit_lower_inverses_impl(lows)
    return xs, xs


def _unit_lower_inverses_bwd(xs, dxs):
    ts = [_dot(x, dx, TN) for x, dx in zip(xs, dxs)]
    return (tuple(_dot(t, x, NT) for t, x in zip(ts, xs)),)


_unit_lower_inverses.defvjp(_unit_lower_inverses_fwd, _unit_lower_inverses_bwd)


N_PREP_IN = 12
RW_GROUP = 2


def _rw_prep(consts, *flat):
    same, tril, stril, blockdiag, m0, m1 = consts
    rows = tril.shape[0]
    masks = (m0, m1)
    pairs = [flat[N_PREP_IN * i:N_PREP_IN * (i + 1)] for i in range(len(flat) // N_PREP_IN)]
    lora = [(_dot(jnp.tanh(lw), w2p), _dot(lw, a2p), _dot(_sigmoid(gd), g2), _dot(jnp.square(kx * k_k), blockdiag))
            for _, kx, _, lw, gd, _, _, k_k, _, w2p, a2p, g2 in pairs]
    mid = []
    for (r, kx, v, lw, gd, w0, a0, k_k, k_a, w2p, a2p, g2), (xw, xa, g, kk_sq) in zip(pairs, lora):
        xw = w0 + xw
        w = jnp.minimum(xw, 0.0) - jnp.log(1.0 + jnp.exp(-jnp.abs(xw))) - 0.5
        a_s = _sigmoid(a0 + xa)
        kk = kx * k_k / jnp.maximum(jnp.sqrt(kk_sq), L2_EPS)
        mid.append((-jnp.exp(w), a_s, kk, kx * (1.0 + (a_s - 1.0) * k_a), g))
    accs = [_dot(jnp.concatenate([tril, same], axis=0), ld, NN, SUM_PRECISION) for ld, _, _, _, _ in mid]
    pre = []
    for (r, _, v, *_), (ld, a_s, kk, k2, g), acc in zip(pairs, mid, accs):
        bv = kk * a_s
        cum, tot = acc[:rows], acc[rows:]
        ecn = jnp.exp(-cum)
        a_t = -kk * jnp.exp(cum - ld)
        r_t = r * jnp.exp(cum)
        rem = jnp.exp(tot - cum)
        pre.append((v, a_t, r_t, (bv * ecn, k2 * ecn), (bv * rem, k2 * rem, jnp.exp(tot), k2, g)))
    zs = [_dot(jnp.concatenate([a_t * m0, a_t * m1, r_t * m0, r_t * m1], axis=0), jnp.concatenate(bk, axis=0), NT)
          for _, a_t, r_t, bk, _ in pre]
    pre = [(v, a_t, r_t, z, out) for (v, a_t, r_t, _, out), z in zip(pre, zs)]
    heads = [(i, h) for i in range(len(pre)) for h in range(2)]
    za = {ih: pre[ih[0]][3][ih[1] * rows:(ih[1] + 1) * rows] for ih in heads}
    zr = {ih: pre[ih[0]][3][(2 + ih[1]) * rows:(3 + ih[1]) * rows] for ih in heads}
    tinv = dict(zip(heads, _unit_lower_inverses(tuple(za[ih][:, :rows] * stril for ih in heads))))
    lv = {ih: _dot(jnp.concatenate([za[ih][:, rows:] * stril, zr[ih][:, rows:] * tril], axis=0), pre[ih[0]][0])
          for ih in heads}
    wu = {ih: _dot(tinv[ih], jnp.concatenate([pre[ih[0]][1] * masks[ih[1]], lv[ih][:rows]], axis=1)) for ih in heads}
    w_m = {ih: wu[ih][:, :LANES] for ih in heads}
    u_m = {ih: masks[ih[1]] * wu[ih][:, LANES:] for ih in heads}
    qy = {ih: _dot(zr[ih][:, :rows] * tril, jnp.concatenate([w_m[ih], u_m[ih]], axis=1)) for ih in heads}
    outs = []
    for i in range(len(pre)):
        a, b = (i, 0), (i, 1)
        W = w_m[a] + w_m[b]
        U = u_m[a] + u_m[b]
        Q = pre[i][2] + qy[a][:, :LANES] + qy[b][:, :LANES]
        Y0 = qy[a][:, LANES:] + qy[b][:, LANES:] + m0 * lv[a][rows:] + m1 * lv[b][rows:]
        outs.append((W, U, Q, Y0) + pre[i][4])
    return tuple(outs)


N_POST_IN = 8


def _rw_post(blockdiag, *flat):
    inv_n = 1.0 / RW_HEAD_DIM
    pairs = [flat[N_POST_IN * i:N_POST_IN * (i + 1)] for i in range(len(flat) // N_POST_IN)]
    sums = [(_dot(y, blockdiag), _dot(r * k2 * r_k, blockdiag)) for y, r, _, k2, _, r_k, _, _ in pairs]
    centred = [p[0] - s[0] * inv_n for p, s in zip(pairs, sums)]
    variances = [_dot(yc * yc, blockdiag) * inv_n for yc in centred]
    return tuple((yc * lax.rsqrt(var + RW_GN_EPS) * ln_w + ln_b + s[1] * v) * g
                 for (_, _, v, _, g, _, ln_w, ln_b), s, yc, var in zip(pairs, sums, centred, variances))


N_RW_VEC = 7
N_RW_MAT = 3


def _rw_specs(T, tb, rev):
    nT = T // tb
    tix = (lambda t: nT - 1 - t) if rev else (lambda t: t)
    wide = lambda blk: pl.BlockSpec((tb, RW_WIDTH), lambda t: (tix(t), blk))
    narrow = lambda blk: pl.BlockSpec((tb, LANES), lambda t: (tix(t), blk))
    vec = pl.BlockSpec((1, RW_WIDTH), lambda t: (0, 0))
    mat = pl.BlockSpec((RW_PAIRS, LANES, LANES), lambda t: (0, 0, 0))
    st = pl.BlockSpec((RW_PAIRS, tb // RW_CHUNK, LANES, LANES), lambda t: (0, tix(t), 0, 0))
    lora0 = 3 * RW_WIDTH // LANES
    ins = [wide(0), wide(1), wide(2), narrow(lora0), narrow(lora0 + 1)]
    return nT, wide, vec, mat, st, ins


def _rw_prep_args(p, r_ref, k_ref, v_ref, lw_ref, gd_ref, vrefs, mrefs):
    ln = _head_lanes(p)
    w0, a0, k_k, k_a = [x[:, ln] for x in vrefs[:4]]
    return (r_ref[:, ln], k_ref[:, ln], v_ref[:, ln], lw_ref[...], gd_ref[...], w0, a0, k_k, k_a,
            *[x[p] for x in mrefs])


def _stack_chunks(ref, top, bottom):
    C = RW_CHUNK
    for c in range(ref.shape[0]):
        ref[c, 0:C, :] = top[c * C:(c + 1) * C]
        ref[c, C:2 * C, :] = bottom[c * C:(c + 1) * C]


def _group_args(p0, r_ref, k_ref, v_ref, lw_ref, gd_ref, vrefs, mrefs):
    flat = []
    for p in range(p0, p0 + RW_GROUP):
        flat += list(_rw_prep_args(p, r_ref, k_ref, v_ref, lw_ref, gd_ref, vrefs, mrefs))
    return flat


def _rw_fwd(rws, vecs, mats, name):
    T = rws.shape[0]
    tb = _row_tile(T, SCAN_ROWS)
    nsub = tb // RW_CHUNK
    C = RW_CHUNK
    nT, wide, vec, mat, st, ins = _rw_specs(T, tb, False)

    def body(*refs):
        r_ref, k_ref, v_ref, lw_ref, gd_ref = refs[:5]
        vrefs = refs[5:5 + N_RW_VEC]
        mrefs = refs[5 + N_RW_VEC:5 + N_RW_VEC + N_RW_MAT]
        o_ref, st_ref, s_ref, wq_ref, uy_ref, bk_ref, misc_ref, y_ref = refs[-8:]

        @pl.when(pl.program_id(0) == 0)
        def _():
            s_ref[...] = jnp.zeros_like(s_ref)

        consts = _rw_consts(tb)
        blockdiag = consts[3]
        for p0 in range(0, RW_PAIRS, RW_GROUP):
            outs = _rw_prep(consts, *_group_args(p0, r_ref, k_ref, v_ref, lw_ref, gd_ref, vrefs, mrefs))
            for p, (W, U, Q, Y0, Bg, Kg, dec, k2, g) in zip(range(p0, p0 + RW_GROUP), outs):
                _stack_chunks(wq_ref.at[p], W, Q)
                _stack_chunks(uy_ref.at[p], U, Y0)
                _stack_chunks(bk_ref.at[p], Bg, Kg)
                misc_ref[0, p], misc_ref[1, p], misc_ref[2, p] = dec, k2, g

        def step(c, carry):
            rows = _chunk_rows(c, C)
            states = [s_ref[p] for p in range(RW_PAIRS)]
            for p, S in enumerate(states):
                st_ref[p, c] = S
            pys = [_dot(wq_ref[p, c], S, NT) + uy_ref[p, c] for p, S in enumerate(states)]
            pvs = [jnp.concatenate([py[:C], v_ref[rows, _head_lanes(p)]], axis=0) for p, py in enumerate(pys)]
            updates = [_dot(pv, bk_ref[p, c], TN) for p, pv in enumerate(pvs)]
            for p, S in enumerate(states):
                y_ref[p, rows, :] = pys[p][C:]
                s_ref[p] = (S * misc_ref[0, p, pl.ds(c * C, 1), :] + updates[p]) * blockdiag
            return carry

        lax.fori_loop(0, nsub, step, 0)
        flat = []
        for p in range(RW_PAIRS):
            ln = _head_lanes(p)
            flat += [y_ref[p], r_ref[:, ln], v_ref[:, ln], misc_ref[1, p], misc_ref[2, p]] + [x[:, ln] for x in vrefs[4:]]
        for p, out in enumerate(_rw_post(blockdiag, *flat)):
            o_ref[:, _head_lanes(p)] = out

    stacked = pltpu.VMEM((RW_PAIRS, nsub, 2 * C, LANES), F32)
    return pl.pallas_call(
        body, name=name, grid=(nT,),
        in_specs=ins + [vec] * N_RW_VEC + [mat] * N_RW_MAT,
        out_specs=[wide(0), st],
        out_shape=[jax.ShapeDtypeStruct((T, RW_WIDTH), F32),
                   jax.ShapeDtypeStruct((RW_PAIRS, T // RW_CHUNK, LANES, LANES), F32)],
        scratch_shapes=[pltpu.VMEM((RW_PAIRS, LANES, LANES), F32), stacked, stacked, stacked,
                        pltpu.VMEM((3, RW_PAIRS, tb, LANES), F32), pltpu.VMEM((RW_PAIRS, tb, LANES), F32)],
    )(rws, rws, rws, rws, rws, *vecs, *mats)


def _rw_bwd(rws, states, do, do_blk, vecs, mats, name):
    T = rws.shape[0]
    tb = _row_tile(T, SCAN_ROWS)
    nsub = tb // RW_CHUNK
    C = RW_CHUNK
    G = RW_GROUP
    nT, wide, vec, mat, st, ins = _rw_specs(T, tb, True)
    nin = 5 + 1 + 1 + N_RW_VEC + N_RW_MAT

    def body(*refs):
        r_ref, k_ref, v_ref, lw_ref, gd_ref = refs[:5]
        st_ref, do_ref = refs[5], refs[6]
        vrefs = refs[7:7 + N_RW_VEC]
        mrefs = refs[7 + N_RW_VEC:nin]
        dr_ref, dk_ref, dv_ref, dlo_ref = refs[nin:nin + 4]
        dvec = refs[nin + 4:nin + 4 + N_RW_VEC]
        dmat = refs[nin + 4 + N_RW_VEC:nin + 4 + N_RW_VEC + N_RW_MAT]
        ds_ref, wq_ref, uy_ref, bk_ref, pv_ref, dec_ref, y_ref, dpre_ref, dvs_ref = refs[-9:]

        @pl.when(pl.program_id(0) == 0)
        def _():
            ds_ref[...] = jnp.zeros_like(ds_ref)
            for x in dvec + dmat:
                x[...] = jnp.zeros_like(x)

        consts = _rw_consts(tb)
        blockdiag = consts[3]
        dlw, dgd = 0.0, 0.0
        for p0 in range(0, RW_PAIRS, G):
            outs, prep_vjp = jax.vjp(functools.partial(_rw_prep, consts),
                                     *_group_args(p0, r_ref, k_ref, v_ref, lw_ref, gd_ref, vrefs, mrefs))
            for q, (W, U, Q, Y0, Bg, Kg, dec, _, _) in enumerate(outs):
                _stack_chunks(wq_ref.at[q], W, Q)
                _stack_chunks(uy_ref.at[q], U, Y0)
                _stack_chunks(bk_ref.at[q], Bg, Kg)
                dec_ref[q] = dec

            def redo(c, carry, p0=p0):
                rows = _chunk_rows(c, C)
                pys = [_dot(wq_ref[q, c], st_ref[p0 + q, c], NT) + uy_ref[q, c] for q in range(G)]
                for q, py in enumerate(pys):
                    y_ref[q, rows, :] = py[C:]
                    pv_ref[q, c, 0:C, :] = py[:C]
                    pv_ref[q, c, C:2 * C, :] = v_ref[rows, _head_lanes(p0 + q)]
                return carry

            lax.fori_loop(0, nsub, redo, 0)
            flat = []
            for q in range(G):
                ln = _head_lanes(p0 + q)
                flat += [y_ref[q], r_ref[:, ln], v_ref[:, ln], outs[q][7], outs[q][8]] + [x[:, ln] for x in vrefs[4:]]
            _, post_vjp = jax.vjp(functools.partial(_rw_post, blockdiag), *flat)
            post_grads = post_vjp(tuple(do_ref[:, _head_lanes(p0 + q)] for q in range(G)))
            post = []
            for q in range(G):
                ln = _head_lanes(p0 + q)
                dy, dr2, dv2, dk2, dg, dr_k, dln_w, dln_b = post_grads[N_POST_IN * q:N_POST_IN * (q + 1)]
                dpre_ref[q, 3] = dy
                dvs_ref[q] = dv2
                for x, gx in zip(dvec[4:], (dr_k, dln_w, dln_b)):
                    x[:, ln] += gx
                dpre_ref[q, 6] = jnp.zeros_like(dpre_ref[q, 6])
                post.append((dr2, dk2, dg))

            def step(i, carry, p0=p0):
                c = nsub - 1 - i
                rows = _chunk_rows(c, C)
                row0 = pl.ds(c * C, 1)
                qs = range(G)
                Gs = [ds_ref[p0 + q] * blockdiag for q in qs]
                Ss = [st_ref[p0 + q, c] for q in qs]
                t1 = [_dot(bk_ref[q, c], Gs[q], NT) for q in qs]
                t3 = [_dot(pv_ref[q, c], Gs[q]) for q in qs]
                dpy = [jnp.concatenate([t1[q][:C], dpre_ref[q, 3, rows, :]], axis=0) for q in qs]
                t2 = [_dot(dpy[q], Ss[q]) for q in qs]
                back = [_dot(dpy[q], wq_ref[q, c], TN) for q in qs]
                for q in qs:
                    dvs_ref[q, rows, :] += t1[q][C:]
                    dpre_ref[q, 0, rows, :] = t2[q][:C]
                    dpre_ref[q, 1, rows, :] = t1[q][:C]
                    dpre_ref[q, 2, rows, :] = t2[q][C:]
                    dpre_ref[q, 4, rows, :] = t3[q][:C]
                    dpre_ref[q, 5, rows, :] = t3[q][C:]
                    dpre_ref[q, 6, row0, :] = jnp.sum(Ss[q] * Gs[q], axis=0, keepdims=True)
                    ds_ref[p0 + q] = Gs[q] * dec_ref[q, row0, :] + back[q]
                return carry

            lax.fori_loop(0, nsub, step, 0)
            grads = prep_vjp(tuple(tuple(dpre_ref[q, i] for i in range(7)) + post[q][1:] for q in range(G)))
            for q in range(G):
                ln = _head_lanes(p0 + q)
                gq = grads[N_PREP_IN * q:N_PREP_IN * (q + 1)]
                dr_ref[:, ln] = gq[0] + post[q][0]
                dk_ref[:, ln] = gq[1]
                dv_ref[:, ln] = gq[2] + dvs_ref[q]
                dlw = dlw + gq[3]
                dgd = dgd + gq[4]
                for x, gx in zip(dvec[:4], gq[5:9]):
                    x[:, ln] += gx
                for x, gx in zip(dmat, gq[9:]):
                    x[p0 + q] += gx
        dlo_ref[:, 0:LANES] = dlw
        dlo_ref[:, LANES:2 * LANES] = dgd

    dcol = jax.ShapeDtypeStruct((T, RW_WIDTH), F32)
    dlo_spec = pl.BlockSpec((tb, 2 * LANES), lambda t: (nT - 1 - t, 0))
    blk = pltpu.VMEM((G, tb, LANES), F32)
    stacked = pltpu.VMEM((G, nsub, 2 * C, LANES), F32)
    return pl.pallas_call(
        body, name=name, grid=(nT,),
        in_specs=ins + [st, wide(do_blk)] + [vec] * N_RW_VEC + [mat] * N_RW_MAT,
        out_specs=[wide(0)] * 3 + [dlo_spec] + [vec] * N_RW_VEC + [mat] * N_RW_MAT,
        out_shape=[dcol] * 3 + [jax.ShapeDtypeStruct((T, 2 * LANES), F32)]
        + [jax.ShapeDtypeStruct((1, RW_WIDTH), F32)] * N_RW_VEC
        + [jax.ShapeDtypeStruct((RW_PAIRS, LANES, LANES), F32)] * N_RW_MAT,
        scratch_shapes=[pltpu.VMEM((RW_PAIRS, LANES, LANES), F32), stacked, stacked, stacked, stacked, blk, blk,
                        pltpu.VMEM((G, 7, tb, LANES), F32), blk],
    )(rws, rws, rws, rws, rws, states, do, *vecs, *mats)


def _my_index():
    return 4 * lax.axis_index("x") + 2 * lax.axis_index("y") + lax.axis_index("c")


def _peer(bits):
    pos = []
    for name, flip in zip(("x", "y", "c"), bits):
        i = lax.axis_index(name)
        pos.append(1 - i if flip else i)
    return tuple(pos)


def _peer_index(bits):
    x, y, c = _peer(bits)
    return 4 * x + 2 * y + c


def _all_gather(shards, name):
    n = len(shards)
    chips = [(1, 0, 0), (0, 1, 0), (1, 1, 0)]
    sib = (0, 0, 1)

    def body(*refs):
        ins, outs = refs[:n], refs[n:2 * n]
        send_sems, recv_sems, local_sems = refs[2 * n:]

        def rows(k, dev):
            r = ins[k].shape[0]
            return outs[k].at[pl.ds(dev * r, r), :]

        def copy(k, slot, block_dev, to_bits, src=None):
            return pltpu.make_async_remote_copy(
                src_ref=rows(k, block_dev) if src is None else src, dst_ref=rows(k, block_dev),
                send_sem=send_sems.at[k, slot], recv_sem=recv_sems.at[k, slot],
                device_id=_peer(to_bits), device_id_type=MESH_ID)

        me = _my_index()
        started = []
        for k in range(n):
            mine = pltpu.make_async_copy(ins[k], rows(k, me), local_sems.at[k])
            mine.start()
            started.append(mine)
        sends = []
        for k in range(n):
            first = [copy(k, 0, me, sib, src=ins[k])]
            first += [copy(k, 1 + j, me, chip, src=ins[k]) for j, chip in enumerate(chips)]
            for cp in first:
                cp.start()
            sends += first
        for k in range(n):
            for j, chip in enumerate(chips):
                copy(k, 1 + j, _peer_index(chip), chip).wait_recv()
                fwd = copy(k, 4 + j, _peer_index(chip), sib)
                fwd.start()
                sends.append(fwd)
        for k in range(n):
            copy(k, 0, _peer_index(sib), sib).wait_recv()
            for j, chip in enumerate(chips):
                both = (chip[0], chip[1], 1)
                copy(k, 4 + j, _peer_index(both), sib).wait_recv()
        for cp in sends:
            cp.wait_send()
        for cp in started:
            cp.wait()

    any_spec = pl.BlockSpec(memory_space=pl.ANY)
    return pl.pallas_call(
        body, name=name,
        in_specs=[any_spec] * n, out_specs=[any_spec] * n,
        out_shape=[jax.ShapeDtypeStruct((N_DEV * s.shape[0], s.shape[1]), s.dtype) for s in shards],
        scratch_shapes=[pltpu.SemaphoreType.DMA((n, 7)), pltpu.SemaphoreType.DMA((n, 7)),
                        pltpu.SemaphoreType.DMA((n,))],
    )(*shards)


def _exchange(partials, name):
    n = len(partials)
    flips = [(dx, dy, dc) for dx in (0, 1) for dy in (0, 1) for dc in (0, 1)][1:]

    def body(*refs):
        ins, outs = refs[:n], refs[n:2 * n]
        send_sems, recv_sems, local_sems = refs[2 * n:]
        me = _my_index()
        local = []
        for k in range(n):
            cp = pltpu.make_async_copy(ins[k].at[me], outs[k].at[me], local_sems.at[k])
            cp.start()
            local.append(cp)
        copies = []
        for k in range(n):
            for d, bits in enumerate(flips):
                cp = pltpu.make_async_remote_copy(
                    src_ref=ins[k].at[_peer_index(bits)], dst_ref=outs[k].at[me],
                    send_sem=send_sems.at[k, d], recv_sem=recv_sems.at[k, d],
                    device_id=_peer(bits), device_id_type=MESH_ID)
                cp.start()
                copies.append(cp)
        for cp in copies:
            cp.wait_recv()
        for cp in copies:
            cp.wait_send()
        for cp in local:
            cp.wait()

    any_spec = pl.BlockSpec(memory_space=pl.ANY)
    return pl.pallas_call(
        body, name=name,
        in_specs=[any_spec] * n, out_specs=[any_spec] * n,
        out_shape=[jax.ShapeDtypeStruct(p.shape, p.dtype) for p in partials],
        scratch_shapes=[pltpu.SemaphoreType.DMA((n, 7)), pltpu.SemaphoreType.DMA((n, 7)),
                        pltpu.SemaphoreType.DMA((n,))],
    )(*partials)


HBM_SPEC = pl.BlockSpec(memory_space=pltpu.HBM)
SEM_SPEC = pl.BlockSpec(memory_space=pltpu.SEMAPHORE)
ALL_FLIPS = [(dx, dy, dc) for dx in (0, 1) for dy in (0, 1) for dc in (0, 1)][1:]


def _spread_copies(srcs, lands, send_sems, recv_sems, to_x):
    me = _my_index()
    my_x = lax.axis_index("x")
    copies = []
    for k, land in enumerate(lands):
        for d, bits in enumerate(ALL_FLIPS):
            if not srcs:
                src = land.at[me]
            elif to_x is None:
                src = srcs[k].at[_peer_index(bits)]
            else:
                _, py, pc = _peer(bits)
                src = srcs[k].at[2 * py + pc]
            cp = pltpu.make_async_remote_copy(
                src_ref=src, dst_ref=land.at[me],
                send_sem=send_sems.at[k * 7 + d], recv_sem=recv_sems.at[k * 7 + d],
                device_id=_peer(bits), device_id_type=MESH_ID)
            sends = True if to_x is None else my_x == (to_x ^ bits[0])
            receives = True if to_x is None else my_x == to_x
            copies.append((cp, sends, receives))
    return copies


def _when(cond, fn):
    if cond is True:
        fn()
    else:
        pl.when(cond)(fn)


def _spread_start(srcs, lands, name, to_x=None):
    ns, n = len(srcs), len(lands)

    def body(*refs):
        src_refs, land_refs = refs[:ns], refs[ns:ns + n]
        send_sems, recv_sems = refs[ns + n], refs[ns + n + 1]
        token = refs[-1]
        for cp, sends, _ in _spread_copies(src_refs, land_refs, send_sems, recv_sems, to_x):
            _when(sends, cp.start)
        token[...] = jnp.zeros_like(token)

    bufs = list(srcs) + list(lands)
    out = pl.pallas_call(
        body, name=name,
        out_shape=(pltpu.SemaphoreType.DMA((7 * n,)), pltpu.SemaphoreType.DMA((7 * n,)),
                   *[pltpu.HBM(b.shape, b.dtype) for b in bufs], jax.ShapeDtypeStruct((8, LANES), F32)),
        in_specs=[HBM_SPEC] * (ns + n),
        out_specs=(SEM_SPEC, SEM_SPEC, *[HBM_SPEC] * (ns + n), pl.BlockSpec(memory_space=pltpu.VMEM)),
        input_output_aliases={i: 2 + i for i in range(ns + n)},
        compiler_params=pltpu.CompilerParams(has_side_effects=pltpu.SideEffectType.DATAFLOW_SIDE_EFFECTING),
    )(*[pltpu.with_memory_space_constraint(b, pltpu.HBM) for b in bufs])
    return out[0], out[1], list(out[2:2 + ns]), list(out[2 + ns:2 + ns + n]), out[-1]


def _spread_wait(send_sems, recv_sems, srcs, lands, after, name, to_x=None):
    ns, n = len(srcs), len(lands)

    def body(*refs):
        src_refs, land_refs = refs[:ns], refs[ns:ns + n]
        send_sems, recv_sems = refs[ns + n], refs[ns + n + 1]
        for cp, sends, receives in _spread_copies(src_refs, land_refs, send_sems, recv_sems, to_x):
            _when(sends, cp.wait_send)
            _when(receives, cp.wait_recv)

    bufs = list(srcs) + list(lands)
    out = pl.pallas_call(
        body, name=name,
        out_shape=tuple(pltpu.HBM(b.shape, b.dtype) for b in bufs),
        in_specs=[HBM_SPEC] * (ns + n) + [SEM_SPEC, SEM_SPEC, pl.BlockSpec(memory_space=pl.ANY)],
        out_specs=tuple([HBM_SPEC] * (ns + n)),
        input_output_aliases={i: i for i in range(ns + n)},
        compiler_params=pltpu.CompilerParams(has_side_effects=pltpu.SideEffectType.DATAFLOW_SIDE_EFFECTING),
    )(*bufs, send_sems, recv_sems, after)
    return list(out[ns:])


def _own_slot_only(block, me):
    return lax.dynamic_update_slice(lax.empty((N_DEV,) + block.shape, block.dtype), block[None], (me, 0, 0))


def _sum_slots(landed, name):
    _, R, C = landed.shape
    tb = _row_tile(R, 128)

    def body(l_ref, o_ref):
        acc = l_ref[0].astype(F32)
        for s in range(1, N_DEV):
            acc = acc + l_ref[s].astype(F32)
        o_ref[...] = acc

    return pl.pallas_call(
        body, name=name, grid=(R // tb,),
        in_specs=[pl.BlockSpec((N_DEV, tb, C), lambda i: (0, i, 0))],
        out_specs=pl.BlockSpec((tb, C), lambda i: (i, 0)),
        out_shape=jax.ShapeDtypeStruct((R, C), F32),
    )(landed)


def _pack_rows(flat_list, width=LANES):
    flat = jnp.concatenate([a.reshape(-1) for a in flat_list])
    n = flat.shape[0]
    rows = -(-n // width)
    rows = -(-rows // 8) * 8
    return jnp.pad(flat, (0, rows * width - n)).reshape(rows, width)


def _unpack(packed, shapes):
    flat = packed.reshape(-1)
    out, off = [], 0
    for s in shapes:
        n = 1
        for d in s:
            n *= d
        out.append(flat[off:off + n].reshape(s))
        off += n
    return out


def kernel(x, norm1_w, w_in, hg_lb_logits, hg_norm_w, rw_shift_mu, rw_w0, rw_w2, rw_a0, rw_a2, rw_g2, rw_k_k, rw_k_a, rw_r_k, rw_ln_w, rw_ln_b, w_out, norm2_w, w_up, conv_w, conv_b, w_down, final_norm_w, loss_target, m_norm1_w, m_w_in, m_hg_lb_logits, m_hg_norm_w, m_rw_shift_mu, m_rw_w0, m_rw_w2, m_rw_a0, m_rw_a2, m_rw_g2, m_rw_k_k, m_rw_k_a, m_rw_r_k, m_rw_ln_w, m_rw_ln_b, m_w_out, m_norm2_w, m_w_up, m_conv_w, m_conv_b, m_w_down, m_final_norm_w, v_norm1_w, v_w_in, v_hg_lb_logits, v_hg_norm_w, v_rw_shift_mu, v_rw_w0, v_rw_w2, v_rw_a0, v_rw_a2, v_rw_g2, v_rw_k_k, v_rw_k_a, v_rw_r_k, v_rw_ln_w, v_rw_ln_b, v_w_out, v_norm2_w, v_w_up, v_conv_w, v_conv_b, v_w_down, v_final_norm_w):
    weights = dict(norm1_w=norm1_w, w_in=w_in, hg_lb_logits=hg_lb_logits, hg_norm_w=hg_norm_w,
                   rw_shift_mu=rw_shift_mu, rw_w0=rw_w0, rw_w2=rw_w2, rw_a0=rw_a0, rw_a2=rw_a2, rw_g2=rw_g2,
                   rw_k_k=rw_k_k, rw_k_a=rw_k_a, rw_r_k=rw_r_k, rw_ln_w=rw_ln_w, rw_ln_b=rw_ln_b, w_out=w_out,
                   norm2_w=norm2_w, w_up=w_up, conv_w=conv_w, conv_b=conv_b, w_down=w_down,
                   final_norm_w=final_norm_w)
    m_in = dict(norm1_w=m_norm1_w, w_in=m_w_in, hg_lb_logits=m_hg_lb_logits, hg_norm_w=m_hg_norm_w,
                rw_shift_mu=m_rw_shift_mu, rw_w0=m_rw_w0, rw_w2=m_rw_w2, rw_a0=m_rw_a0, rw_a2=m_rw_a2,
                rw_g2=m_rw_g2, rw_k_k=m_rw_k_k, rw_k_a=m_rw_k_a, rw_r_k=m_rw_r_k, rw_ln_w=m_rw_ln_w,
                rw_ln_b=m_rw_ln_b, w_out=m_w_out, norm2_w=m_norm2_w, w_up=m_w_up, conv_w=m_conv_w,
                conv_b=m_conv_b, w_down=m_w_down, final_norm_w=m_final_norm_w)
    v_in = dict(norm1_w=v_norm1_w, w_in=v_w_in, hg_lb_logits=v_hg_lb_logits, hg_norm_w=v_hg_norm_w,
                rw_shift_mu=v_rw_shift_mu, rw_w0=v_rw_w0, rw_w2=v_rw_w2, rw_a0=v_rw_a0, rw_a2=v_rw_a2,
                rw_g2=v_rw_g2, rw_k_k=v_rw_k_k, rw_k_a=v_rw_k_a, rw_r_k=v_rw_r_k, rw_ln_w=v_rw_ln_w,
                rw_ln_b=v_rw_ln_b, w_out=v_w_out, norm2_w=v_norm2_w, w_up=v_w_up, conv_w=v_conv_w,
                conv_b=v_conv_b, w_down=v_w_down, final_norm_w=v_final_norm_w)
    names = list(weights)
    sharded_small = ["rw_w2", "rw_a2", "rw_g2", "conv_w"]
    replicated = [n for n in names if n not in sharded_small + ["w_in", "w_out", "w_up", "w_down"]]

    xs = x[0]
    tgt = loss_target[0]

    small_shard = _pack_rows([weights[n] for n in sharded_small])
    g_win_t, g_small = _all_gather([w_in[0].T.astype(BF16), small_shard], "gather_weights")
    me = _my_index()
    later = (w_up[0].T.astype(BF16), w_out[0].astype(BF16), w_down[0].astype(BF16))
    later, _ = lax.optimization_barrier((later, g_small))
    later = [_own_slot_only(z, me) for z in later]
    g_send, g_recv, _, later, g_token = _spread_start([], later, "gather_later_start")
    small_shapes = [weights[n].shape for n in sharded_small]
    per_dev = [_unpack(g_small.reshape(N_DEV, -1)[j], small_shapes) for j in range(N_DEV)]
    w2_full, a2_full, g2_full, convw_full = [jnp.concatenate([per_dev[j][i][0] for j in range(N_DEV)], axis=-1)
                                             for i in range(4)]
    zeros64 = jnp.zeros((RW_PAIRS, 64, LANES), F32)
    by_pair = lambda z: z.reshape(z.shape[0], RW_PAIRS, LANES).transpose(1, 0, 2)
    w2p = jnp.concatenate([by_pair(w2_full), zeros64], axis=1)
    a2p = jnp.concatenate([zeros64, by_pair(a2_full)], axis=1)
    g2p = by_pair(g2_full)

    l0, l1 = hg_lb_logits[0:1], hg_lb_logits[1:2]
    h1, proj = _rms_mm_nt(xs, norm1_w + g_token[0:1, 0:1], g_win_t, "norm1_proj_in")
    o_hg, hg_states = _hg_fwd(proj, l0, l1, hg_norm_w, "hgrn2_fwd")
    rws = _shift_fwd(proj, rw_shift_mu, "token_shift")
    rw_vecs = [rw_w0, rw_a0, rw_k_k, rw_k_a, rw_r_k, rw_ln_w, rw_ln_b]
    rw_mats = [w2p, a2p, g2p]
    o_rw, rw_states = _rw_fwd(rws, rw_vecs, rw_mats, "rwkv7_fwd")
    o_mix = jnp.concatenate([o_hg, o_rw], axis=-1).astype(BF16)
    g_wup_t, g_wout, g_wdown = [z.reshape(-1, z.shape[-1])
                                for z in _spread_wait(g_send, g_recv, [], later, o_mix, "gather_later_wait")]
    x1 = _mm_nn(o_mix, g_wout, xs, "proj_out")
    h2, u = _rms_mm_nt(x1, norm2_w, g_wup_t, "norm2_ffn_up")
    act = _ffn_act_fwd(u, convw_full, conv_b, "ffn_act")
    loss_part, dx2, d_final_w = _mm_nn_loss_head(act, g_wdown, x1, final_norm_w.reshape(1, -1), tgt,
                                                 "ffn_down_loss_head")

    d_wdown = _mm_tn(act, dx2, 1408, "ffn_down_dw", BF16)
    dact = _mm_nt(dx2, g_wdown, "ffn_down_dx", BF16)
    du_g, du_v, dcw_g, dcw_v, dcb_g, dcb_v = _ffn_act_bwd(u, dact, convw_full, conv_b, "ffn_act_bwd")
    d_convw = jnp.concatenate([dcw_g, dcw_v], axis=-1)
    d_convb = jnp.concatenate([dcb_g, dcb_v], axis=-1)
    d_wup_t = jnp.concatenate([_mm_tn(du_g, h2, 1408, "ffn_up_dw_gate", BF16),
                               _mm_tn(du_v, h2, 1408, "ffn_up_dw_value", BF16)], axis=0)
    dx1, d_norm2 = _mm_nn_rms_bwd([du_g, du_v], g_wup_t, x1, norm2_w, dx2, "ffn_up_dx_norm2_bwd")
    d_wout = _mm_tn(o_mix, dx1, 512, "proj_out_dw", BF16)
    do = _mm_nt(dx1, g_wout, "proj_out_dx")
    early = [z.reshape(N_DEV, z.shape[0] // N_DEV, z.shape[1]) for z in (d_wup_t, d_wout, d_wdown)]
    early_land = [_own_slot_only(lax.dynamic_index_in_dim(z, me, 0, keepdims=False), me) for z in early]
    e_send, e_recv, early, early_land, e_token = _spread_start(early, early_land, "exchange_early_start")
    hg_norm_w_t = hg_norm_w + e_token[0:1, 0:1]
    dq, df, di, dg, d_l0, d_l1, d_hg_nw = _hg_bwd(proj, hg_states, do, 0, l0, l1, hg_norm_w_t, "hgrn2_bwd")
    half = N_DEV // 2
    own_half_block = lambda z: _own_slot_only(lax.dynamic_index_in_dim(z, me % half, 0, keepdims=False), me)
    n_lo = half * w_in.shape[2]
    d_win_lo = _mm_tn(jnp.concatenate([dq, df, di, dg[:, :n_lo - 3 * HG_WIDTH]], axis=-1), h1, 640,
                      "proj_in_dw_low", BF16).reshape(half, -1, D_MODEL)
    m_send, m_recv, mid, mid_land, m_token = _spread_start([d_win_lo], [own_half_block(d_win_lo)],
                                                            "exchange_mid_start", to_x=0)
    rw_vecs_t = [rw_vecs[0] + m_token[0:1, 0:1]] + rw_vecs[1:]
    rw_out = _rw_bwd(rws, rw_states, do, 1, rw_vecs_t, rw_mats, "rwkv7_bwd")
    d_rw_vecs = rw_out[4:4 + N_RW_VEC]
    d_w2p, d_a2p, d_g2p = rw_out[4 + N_RW_VEC:]
    dp_parts, dmu_parts = [], []
    for i, z in enumerate(rw_out[:4]):
        dp, dmu = _shift_bwd(z, proj, rw_shift_mu, i * RW_WIDTH, "token_shift_bwd_%d" % i)
        dp_parts.append(dp)
        dmu_parts.append(dmu)
    d_mu = jnp.concatenate(dmu_parts, axis=-1)
    d_win_hi = _mm_tn(jnp.concatenate([dg[:, n_lo - 3 * HG_WIDTH:]] + dp_parts, axis=-1), h1, 640,
                      "proj_in_dw_high", BF16).reshape(half, -1, D_MODEL)
    from_pairs = lambda z: z.transpose(1, 0, 2).reshape(z.shape[1], RW_WIDTH)
    d_w2 = from_pairs(d_w2p[:, :64])
    d_a2 = from_pairs(d_a2p[:, 64:])
    d_g2 = from_pairs(d_g2p)
    col_blocks = lambda z: z.reshape(z.shape[0], N_DEV, -1).transpose(1, 0, 2)
    small_part = jnp.stack([
        _pack_rows([col_blocks(d_w2)[j], col_blocks(d_a2)[j], col_blocks(d_g2)[j], col_blocks(d_convw)[j]])
        for j in range(N_DEV)])
    l_send, l_recv, late, late_land, l_token = _spread_start([d_win_hi], [own_half_block(d_win_hi)],
                                                             "exchange_late_start", to_x=1)
    grad_x, d_norm1 = _mm_nn_rms_bwd([dq, df, di, dg] + dp_parts, g_win_t, xs, norm1_w + l_token[0:1, 0:1], dx1,
                                     "proj_in_dx_norm1_bwd")

    rep_grads = dict(norm1_w=d_norm1, hg_lb_logits=jnp.concatenate([d_l0, d_l1], axis=0), hg_norm_w=d_hg_nw,
                     rw_shift_mu=d_mu, rw_w0=d_rw_vecs[0], rw_a0=d_rw_vecs[1], rw_k_k=d_rw_vecs[2],
                     rw_k_a=d_rw_vecs[3], rw_r_k=d_rw_vecs[4], rw_ln_w=d_rw_vecs[5], rw_ln_b=d_rw_vecs[6],
                     norm2_w=d_norm2, conv_b=d_convb, final_norm_w=d_final_w)
    rep_pack = _pack_rows([loss_part] + [rep_grads[n] for n in replicated])
    rep_part = jnp.broadcast_to(rep_pack[None], (N_DEV,) + rep_pack.shape)
    grads, delta, new_m, new_v = {}, {}, {}, {}

    def adamw_big(n, g):
        shp = weights[n].shape
        as2d = lambda z: z.reshape(shp[1], shp[2])
        grads[n] = g[None]
        d, nm, nv = _adamw(as2d(weights[n]), g, as2d(m_in[n]), as2d(v_in[n]), "adamw_" + n)
        delta[n], new_m[n], new_v[n] = d.reshape(shp), nm.reshape(shp), nv.reshape(shp)

    landed_early = _spread_wait(e_send, e_recv, early, early_land, grad_x, "exchange_early_wait")
    adamw_big("w_up", _sum_slots(landed_early[0], "sum_grads_w_up").T)
    adamw_big("w_out", _sum_slots(landed_early[1], "sum_grads_w_out"))
    adamw_big("w_down", _sum_slots(landed_early[2], "sum_grads_w_down"))
    (landed_mid,) = _spread_wait(m_send, m_recv, mid, mid_land, grad_x, "exchange_mid_wait", to_x=0)
    (landed_late,) = _spread_wait(l_send, l_recv, late, late_land, delta["w_down"], "exchange_late_wait", to_x=1)
    g_win = jnp.where(lax.axis_index("x") == 0, _sum_slots(landed_mid, "sum_grads_w_in_low"),
                      _sum_slots(landed_late, "sum_grads_w_in_high"))
    adamw_big("w_in", g_win.T)
    landed_rep, landed_small = _exchange([rep_part, small_part], "exchange_grads")
    g_small_sum = _unpack(_sum_slots(landed_small, "sum_grads_small"), small_shapes)
    rep_sum = _unpack(_sum_slots(landed_rep, "sum_grads_replicated"), [(1, 1)] + [weights[n].shape for n in replicated])
    loss = rep_sum[0].reshape(())
    grads.update(dict(zip(replicated, rep_sum[1:])))
    grads.update(dict(zip(sharded_small, g_small_sum)))

    small_names = replicated + sharded_small
    packs = [_pack_rows([src[n] for n in small_names]) for src in (weights, grads, m_in, v_in)]
    outs = _adamw(*packs, "adamw_small")
    small_shapes_all = [weights[n].shape for n in small_names]
    for dst, packed in zip((delta, new_m, new_v), outs):
        dst.update(dict(zip(small_names, _unpack(packed, small_shapes_all))))

    return (loss, grad_x[None], *[grads[n] for n in names], *[delta[n] for n in names],
            *[new_m[n] for n in names], *[new_v[n] for n in names])
```

```python
import functools

import jax
import jax.numpy as jnp
from jax import lax
from jax.experimental import pallas as pl
from jax.experimental.pallas import tpu as pltpu

F32 = jnp.float32
BF16 = jnp.bfloat16
SUM_PRECISION = lax.Precision.HIGH
SCAN_PRECISION = None
MESH_ID = pl.DeviceIdType.MESH

N_DEV = 8
D_MODEL = 1024
HG_WIDTH = 512
HG_HEAD_DIM = 128
HG_HEADS = 4
RW_WIDTH = 512
RW_PAIRS = 4
RW_HEAD_DIM = 64
HG_COLS = 2048
RW_COLS = 1792
D_FF = 2816
NORM_EPS = 1e-6
RW_GN_EPS = 64e-5
L2_EPS = 1e-12
ADAM_LR, ADAM_B1, ADAM_B2, ADAM_EPS, ADAM_WD, ADAM_STEP = 0.001, 0.9, 0.999, 1e-08, 0.01, 10

HG_CHUNK = 32
HG_HALF = 16
RW_CHUNK = 64
SCAN_ROWS = 256
LANES = 128

NN = ((1,), (0,))
NT = ((1,), (1,))
TN = ((0,), (0,))


def _dot(a, b, dims=NN, precision=SCAN_PRECISION):
    if precision is None:
        a, b = a.astype(BF16), b.astype(BF16)
    return lax.dot_general(a, b, (dims, ((), ())), precision=precision, preferred_element_type=F32)


def _iota2(shape, dim):
    return lax.broadcasted_iota(jnp.int32, shape, dim)


def _sigmoid(z):
    return 0.5 * jnp.tanh(0.5 * z) + 0.5


def _row_tile(n, want):
    t = min(n, want)
    while n % t:
        t //= 2
    return t


def _mm_nn_rms_bwd(a, b, x, w, dres, name):
    parts = list(a)
    T, D = x.shape
    K = b.shape[0]
    tm = _row_tile(T, 256)
    widths = [p.shape[1] for p in parts]
    n = len(parts)

    def body(*refs):
        b_ref, x_ref, w_ref, dres_ref, dx_ref, dw_ref = refs[n:]

        @pl.when(pl.program_id(0) == 0)
        def _():
            dw_ref[...] = jnp.zeros_like(dw_ref)

        dy, off = None, 0
        for a_ref, wd in zip(refs[:n], widths):
            d = _dot(a_ref[...].astype(BF16), b_ref[off:off + wd, :].astype(BF16), NN, None)
            dy = d if dy is None else dy + d
            off += wd
        xv = x_ref[...]
        r = lax.rsqrt(jnp.mean(xv * xv, axis=-1, keepdims=True) + NORM_EPS)
        xn = xv * r
        dxn = dy * w_ref[...]
        dx_ref[...] = dres_ref[...] + r * (dxn - xn * jnp.mean(dxn * xn, axis=-1, keepdims=True))
        dw_ref[...] += jnp.sum(dy * xn, axis=0, keepdims=True)

    row = pl.BlockSpec((tm, D), lambda i: (i, 0))
    vec = pl.BlockSpec((1, D), lambda i: (0, 0))
    return pl.pallas_call(
        body, name=name, grid=(T // tm,),
        in_specs=[pl.BlockSpec((tm, wd), lambda i: (i, 0)) for wd in widths]
        + [pl.BlockSpec((K, D), lambda i: (0, 0)), row, vec, row],
        out_specs=[row, vec],
        out_shape=[jax.ShapeDtypeStruct((T, D), F32), jax.ShapeDtypeStruct((1, D), F32)],
    )(*parts, b, x, w, dres)


def _rms_mm_nt(x, w, bt, name):
    T, K = x.shape
    N = bt.shape[0]
    tm = _row_tile(T, 256)

    def body(x_ref, w_ref, b_ref, h_ref, o_ref):
        xv = x_ref[...]
        r = lax.rsqrt(jnp.mean(xv * xv, axis=-1, keepdims=True) + NORM_EPS)
        h = (xv * r * w_ref[...]).astype(BF16)
        h_ref[...] = h
        o_ref[...] = _dot(h, b_ref[...].astype(BF16), NT, None)

    row = pl.BlockSpec((tm, K), lambda i: (i, 0))
    return pl.pallas_call(
        body, name=name, grid=(T // tm,),
        in_specs=[row, pl.BlockSpec((1, K), lambda i: (0, 0)), pl.BlockSpec((N, K), lambda i: (0, 0))],
        out_specs=[row, pl.BlockSpec((tm, N), lambda i: (i, 0))],
        out_shape=[jax.ShapeDtypeStruct((T, K), BF16), jax.ShapeDtypeStruct((T, N), F32)],
    )(x, w, bt)


def _mm_nt(a, bt, name, out_dtype=F32):
    T, K = a.shape
    N = bt.shape[0]
    tm = _row_tile(T, 256)

    def body(a_ref, b_ref, o_ref):
        o_ref[...] = _dot(a_ref[...].astype(BF16), b_ref[...].astype(BF16), NT, None).astype(o_ref.dtype)

    return pl.pallas_call(
        body, name=name, grid=(T // tm,),
        in_specs=[pl.BlockSpec((tm, K), lambda i: (i, 0)), pl.BlockSpec((N, K), lambda i: (0, 0))],
        out_specs=pl.BlockSpec((tm, N), lambda i: (i, 0)),
        out_shape=jax.ShapeDtypeStruct((T, N), out_dtype),
    )(a, bt)


def _mm_nn(a, b, res, name, out_dtype=F32):
    parts = list(a) if isinstance(a, (list, tuple)) else [a]
    T = parts[0].shape[0]
    K, N = b.shape
    tm = _row_tile(T, 256)
    widths = [p.shape[1] for p in parts]
    n = len(parts)

    def body(*refs):
        b_ref, o_ref = refs[n], refs[-1]
        acc, off = None, 0
        for a_ref, w in zip(refs[:n], widths):
            d = _dot(a_ref[...].astype(BF16), b_ref[off:off + w, :].astype(BF16), NN, None)
            acc = d if acc is None else acc + d
            off += w
        if res is not None:
            acc = acc + refs[n + 1][...]
        o_ref[...] = acc.astype(o_ref.dtype)

    in_specs = [pl.BlockSpec((tm, w), lambda i: (i, 0)) for w in widths] + [pl.BlockSpec((K, N), lambda i: (0, 0))]
    args = parts + [b]
    if res is not None:
        in_specs.append(pl.BlockSpec((tm, N), lambda i: (i, 0)))
        args.append(res)
    return pl.pallas_call(
        body, name=name, grid=(T // tm,), in_specs=in_specs,
        out_specs=pl.BlockSpec((tm, N), lambda i: (i, 0)),
        out_shape=jax.ShapeDtypeStruct((T, N), out_dtype),
    )(*args)


def _mm_tn(a, b, tmm, name, out_dtype=F32):
    T, M = a.shape
    N = b.shape[1]
    tk = _row_tile(T, 512)
    nk = T // tk

    def body(a_ref, b_ref, o_ref, acc_ref):
        @pl.when(pl.program_id(1) == 0)
        def _():
            acc_ref[...] = jnp.zeros_like(acc_ref)

        acc_ref[...] += _dot(a_ref[...].astype(BF16), b_ref[...].astype(BF16), TN, None)

        @pl.when(pl.program_id(1) == nk - 1)
        def _():
            o_ref[...] = acc_ref[...].astype(o_ref.dtype)

    return pl.pallas_call(
        body, name=name, grid=(M // tmm, nk),
        in_specs=[pl.BlockSpec((tk, tmm), lambda m, k: (k, m)), pl.BlockSpec((tk, N), lambda m, k: (k, 0))],
        out_specs=pl.BlockSpec((tmm, N), lambda m, k: (m, 0)),
        out_shape=jax.ShapeDtypeStruct((M, N), out_dtype),
        scratch_shapes=[pltpu.VMEM((tmm, N), F32)],
    )(a, b)


class _RowShifts:
    def __init__(self, shape):
        index = _iota2(shape, 0)
        self.rows = shape[0]
        self.first = {n: index < n for n in (1, 2)}
        self.last = {n: index >= shape[0] - n for n in (1, 2)}

    def down(self, z, n):
        return jnp.where(self.first[n], 0.0, pltpu.roll(z, n, 0))

    def up(self, z, n):
        return jnp.where(self.last[n], 0.0, pltpu.roll(z, self.rows - n, 0))


def _shift_fwd(proj, mu, name):
    T = proj.shape[0]
    nblk = RW_COLS // LANES
    first = HG_COLS // LANES

    def body(p_ref, mu_ref, o_ref):
        p = p_ref[...]
        o_ref[...] = p + (_RowShifts(p.shape).down(p, 1) - p) * mu_ref[...]

    return pl.pallas_call(
        body, name=name, grid=(nblk,),
        in_specs=[pl.BlockSpec((T, LANES), lambda j: (0, first + j)), pl.BlockSpec((1, LANES), lambda j: (0, j))],
        out_specs=pl.BlockSpec((T, LANES), lambda j: (0, j)),
        out_shape=jax.ShapeDtypeStruct((T, RW_COLS), F32),
    )(proj, mu)


def _shift_bwd(ds, proj, mu, col0, name):
    T, width = ds.shape
    nblk = width // LANES
    first = (HG_COLS + col0) // LANES
    mu0 = col0 // LANES

    def body(ds_ref, p_ref, mu_ref, dp_ref, dmu_ref):
        dsv = ds_ref[...]
        p = p_ref[...]
        m = mu_ref[...]
        shifts = _RowShifts(p.shape)
        dp_ref[...] = (dsv * (1.0 - m) + shifts.up(dsv * m, 1)).astype(dp_ref.dtype)
        dmu_ref[...] = jnp.sum(dsv * (shifts.down(p, 1) - p), axis=0, keepdims=True)

    return pl.pallas_call(
        body, name=name, grid=(nblk,),
        in_specs=[pl.BlockSpec((T, LANES), lambda j: (0, j)),
                  pl.BlockSpec((T, LANES), lambda j: (0, first + j)),
                  pl.BlockSpec((1, LANES), lambda j: (0, mu0 + j))],
        out_specs=[pl.BlockSpec((T, LANES), lambda j: (0, j)), pl.BlockSpec((1, LANES), lambda j: (0, j))],
        out_shape=[jax.ShapeDtypeStruct((T, width), BF16), jax.ShapeDtypeStruct((1, width), F32)],
    )(ds, proj, mu)


def _conv3(z, w_ref, shifts):
    return w_ref[0:1, :] * shifts.down(z, 2) + w_ref[1:2, :] * shifts.down(z, 1) + w_ref[2:3, :] * z


def _ffn_act_fwd(u, conv_w, conv_b, name):
    T = u.shape[0]
    nblk = D_FF // LANES

    def body(ug_ref, uv_ref, wg_ref, wv_ref, bg_ref, bv_ref, act_ref):
        shifts = _RowShifts((T, LANES))
        gate = _conv3(ug_ref[...], wg_ref, shifts) + bg_ref[...]
        val = _conv3(uv_ref[...], wv_ref, shifts) + bv_ref[...]
        act_ref[...] = (gate * _sigmoid(gate) * val).astype(act_ref.dtype)

    col = lambda off: pl.BlockSpec((T, LANES), lambda j: (0, off + j))
    wsp = lambda off: pl.BlockSpec((3, LANES), lambda j: (0, off + j))
    bsp = lambda off: pl.BlockSpec((1, LANES), lambda j: (0, off + j))
    return pl.pallas_call(
        body, name=name, grid=(nblk,),
        in_specs=[col(0), col(nblk), wsp(0), wsp(nblk), bsp(0), bsp(nblk)],
        out_specs=pl.BlockSpec((T, LANES), lambda j: (0, j)),
        out_shape=jax.ShapeDtypeStruct((T, D_FF), BF16),
    )(u, u, conv_w, conv_w, conv_b, conv_b)


def _ffn_act_bwd(u, dact, conv_w, conv_b, name):
    T = u.shape[0]
    nblk = D_FF // LANES

    def conv_bwd(z, dzc, w_ref, du_ref, dw_ref, db_ref, shifts):
        up1, up2 = shifts.up(dzc, 1), shifts.up(dzc, 2)
        du = w_ref[2:3, :] * dzc + w_ref[1:2, :] * up1 + w_ref[0:1, :] * up2
        du_ref[...] = du.astype(du_ref.dtype)
        dw_ref[0:1, :] = jnp.sum(up2 * z, axis=0, keepdims=True)
        dw_ref[1:2, :] = jnp.sum(up1 * z, axis=0, keepdims=True)
        dw_ref[2:3, :] = jnp.sum(dzc * z, axis=0, keepdims=True)
        db_ref[...] = jnp.sum(dzc, axis=0, keepdims=True)

    def body(ug_ref, uv_ref, da_ref, wg_ref, wv_ref, bg_ref, bv_ref,
             dug_ref, duv_ref, dwg_ref, dwv_ref, dbg_ref, dbv_ref):
        ug, uv = ug_ref[...], uv_ref[...]
        shifts = _RowShifts((T, LANES))
        gate = _conv3(ug, wg_ref, shifts) + bg_ref[...]
        val = _conv3(uv, wv_ref, shifts) + bv_ref[...]
        da = da_ref[...].astype(F32)
        sg = _sigmoid(gate)
        dgate = da * val * (sg * (1.0 + gate * (1.0 - sg)))
        dval = da * gate * sg
        conv_bwd(ug, dgate, wg_ref, dug_ref, dwg_ref, dbg_ref, shifts)
        conv_bwd(uv, dval, wv_ref, duv_ref, dwv_ref, dbv_ref, shifts)

    col = lambda off: pl.BlockSpec((T, LANES), lambda j: (0, off + j))
    wsp = lambda off: pl.BlockSpec((3, LANES), lambda j: (0, off + j))
    bsp = lambda off: pl.BlockSpec((1, LANES), lambda j: (0, off + j))
    half = lambda r, dt: jax.ShapeDtypeStruct((r, D_FF), dt)
    return pl.pallas_call(
        body, name=name, grid=(nblk,),
        in_specs=[col(0), col(nblk), col(0), wsp(0), wsp(nblk), bsp(0), bsp(nblk)],
        out_specs=[col(0), col(0), wsp(0), wsp(0), bsp(0), bsp(0)],
        out_shape=[half(T, BF16), half(T, BF16), half(3, F32), half(3, F32), half(1, F32), half(1, F32)],
    )(u, u, dact, conv_w, conv_w, conv_b, conv_b)


def _mm_nn_loss_head(a, b, res, w, target, name):
    T, K = a.shape
    D = b.shape[1]
    tm = _row_tile(T, 256)

    def body(a_ref, b_ref, res_ref, w_ref, t_ref, loss_ref, dx_ref, dw_ref):
        @pl.when(pl.program_id(0) == 0)
        def _():
            loss_ref[...] = jnp.zeros_like(loss_ref)
            dw_ref[...] = jnp.zeros_like(dw_ref)

        xv = res_ref[...] + _dot(a_ref[...].astype(BF16), b_ref[...].astype(BF16), NN, None)
        r = lax.rsqrt(jnp.mean(xv * xv, axis=-1, keepdims=True) + NORM_EPS)
        xn = xv * r
        err = xn * w_ref[...] - t_ref[...]
        row_loss = jnp.sum(err * err, axis=-1, keepdims=True) * (0.5 / D)
        loss_ref[...] += jnp.sum(row_loss, axis=0, keepdims=True)
        dy = err * (1.0 / D)
        dxn = dy * w_ref[...]
        dx_ref[...] = r * (dxn - xn * jnp.mean(dxn * xn, axis=-1, keepdims=True))
        dw_ref[...] += jnp.sum(dy * xn, axis=0, keepdims=True)

    row = pl.BlockSpec((tm, D), lambda i: (i, 0))
    vec = pl.BlockSpec((1, D), lambda i: (0, 0))
    return pl.pallas_call(
        body, name=name, grid=(T // tm,),
        in_specs=[pl.BlockSpec((tm, K), lambda i: (i, 0)), pl.BlockSpec((K, D), lambda i: (0, 0)), row, vec, row],
        out_specs=[pl.BlockSpec((1, 1), lambda i: (0, 0)), row, vec],
        out_shape=[jax.ShapeDtypeStruct((1, 1), F32), jax.ShapeDtypeStruct((T, D), F32),
                   jax.ShapeDtypeStruct((1, D), F32)],
    )(a, b, res, w, target)


def _adamw(w, g, m, v, name):
    R, C = w.shape
    tb = _row_tile(R, 256) if R % 8 == 0 else R

    def body(w_ref, g_ref, m_ref, v_ref, d_ref, nm_ref, nv_ref):
        gv = g_ref[...]
        nm = ADAM_B1 * m_ref[...] + (1.0 - ADAM_B1) * gv
        nv = ADAM_B2 * v_ref[...] + (1.0 - ADAM_B2) * (gv * gv)
        m_hat = nm / (1.0 - ADAM_B1 ** ADAM_STEP)
        v_hat = nv / (1.0 - ADAM_B2 ** ADAM_STEP)
        d_ref[...] = -ADAM_LR * (m_hat / (jnp.sqrt(v_hat) + ADAM_EPS) + ADAM_WD * w_ref[...])
        nm_ref[...] = nm
        nv_ref[...] = nv

    blk = pl.BlockSpec((tb, C), lambda i: (i, 0))
    sd = jax.ShapeDtypeStruct((R, C), F32)
    return pl.pallas_call(
        body, name=name, grid=(R // tb,), in_specs=[blk] * 4, out_specs=[blk] * 3, out_shape=[sd] * 3,
    )(w, g, m, v)


def _chunk_masks(rows, chunk):
    shift = chunk.bit_length() - 1
    i, j = _iota2((rows, rows), 0), _iota2((rows, rows), 1)
    same = jnp.right_shift(i, shift) == jnp.right_shift(j, shift)
    return same.astype(F32), (same & (j <= i)).astype(F32), (same & (j < i)).astype(F32)


def _head_lanes(h):
    return slice(h * LANES, (h + 1) * LANES)


def _chunk_rows(c, chunk):
    return pl.ds(pl.multiple_of(c * chunk, chunk), chunk)


def _hg_consts(rows):
    same, tril, _ = _chunk_masks(rows, HG_CHUNK)
    half_same, half_tril, _ = _chunk_masks(rows, HG_HALF)
    i, j = _iota2((rows, rows), 0), _iota2((rows, rows), 1)
    half_shift, shift = HG_HALF.bit_length() - 1, HG_CHUNK.bit_length() - 1
    mid_row = jnp.left_shift(jnp.right_shift(i, half_shift), half_shift) + (HG_HALF // 2 - 1)
    bound_row = jnp.left_shift(jnp.right_shift(i, shift), shift) + (HG_HALF - 1)
    upto_mid = ((same > 0) & (j <= mid_row)).astype(F32)
    upto_bound = ((same > 0) & (j <= bound_row)).astype(F32)
    lower_left = tril * (1.0 - half_same)
    return jnp.concatenate([tril, same, upto_mid, upto_bound], axis=0), half_tril, lower_left


N_HG_IN = 5


def _hg_prep(consts, *flat):
    sums, half_tril, lower_left = consts
    rows = half_tril.shape[0]
    heads, logs = [], []
    for h in range(len(flat) // N_HG_IN):
        qr, fr, ir, l0, l1 = flat[N_HG_IN * h:N_HG_IN * (h + 1)]
        lb = _sigmoid(l0 - l1)
        f = lb + (1.0 - lb) * _sigmoid(fr)
        heads.append((qr * _sigmoid(qr) * (HG_HEAD_DIM ** -0.5), 1.0 - f, ir))
        logs.append(jnp.log(f))
    acc = _dot(sums, jnp.concatenate(logs, axis=1), NN, SUM_PRECISION)
    sums_of = []
    for h in range(len(heads)):
        acc_h = acc[:, h * LANES:(h + 1) * LANES]
        sums_of.append(tuple(acc_h[n * rows:(n + 1) * rows] for n in range(4)))
    near = [_dot(q * jnp.exp(a - mid), k * jnp.exp(mid - a), NT) * half_tril
            for (q, k, _), (a, _, mid, _) in zip(heads, sums_of)]
    far = [_dot(q * jnp.exp(jnp.minimum(a - bound, 0.0)), k * jnp.exp(jnp.minimum(bound - a, 0.0)), NT) * lower_left
           for (q, k, _), (a, _, _, bound) in zip(heads, sums_of)]
    intra = [_dot(n + f, ir) for n, f, (_, _, ir) in zip(near, far, heads)]
    return tuple((q * jnp.exp(a), o_intra, k * jnp.exp(tot - a), jnp.exp(tot))
                 for (q, k, _), (a, tot, _, _), o_intra in zip(heads, sums_of, intra))


def _hg_prep_args(q_ref, f_ref, i_ref, l0_ref, l1_ref):
    flat = []
    for h in range(HG_HEADS):
        ln = _head_lanes(h)
        flat += [q_ref[:, ln], f_ref[:, ln], i_ref[:, ln], l0_ref[:, ln], l1_ref[:, ln]]
    return flat


def _hg_post(o, gr, nw):
    on = o * lax.rsqrt(jnp.mean(o * o, axis=-1, keepdims=True) + NORM_EPS)
    return on * nw * (gr * _sigmoid(gr))


def _hg_specs(T, tb, rev):
    nT = T // tb
    tix = (lambda t: nT - 1 - t) if rev else (lambda t: t)
    col = lambda blk: pl.BlockSpec((tb, HG_WIDTH), lambda t: (tix(t), blk))
    vec = pl.BlockSpec((1, HG_WIDTH), lambda t: (0, 0))
    st = pl.BlockSpec((HG_HEADS, tb // HG_CHUNK, HG_HEAD_DIM, HG_HEAD_DIM), lambda t: (0, tix(t), 0, 0))
    return nT, col, vec, st


def _hg_fwd(proj, l0, l1, nw, name):
    T = proj.shape[0]
    tb = _row_tile(T, SCAN_ROWS)
    nsub = tb // HG_CHUNK
    nT, col, vec, st = _hg_specs(T, tb, False)

    def body(q_ref, f_ref, i_ref, g_ref, l0_ref, l1_ref, nw_ref, o_ref, st_ref, s_ref, qe_ref, kd_ref, dec_ref):
        @pl.when(pl.program_id(0) == 0)
        def _():
            s_ref[...] = jnp.zeros_like(s_ref)

        consts = _hg_consts(tb)
        outs = _hg_prep(consts, *_hg_prep_args(q_ref, f_ref, i_ref, l0_ref, l1_ref))
        for h, (qe, o_intra, kd, dec) in enumerate(outs):
            qe_ref[h], kd_ref[h], dec_ref[h] = qe, kd, dec
            o_ref[:, _head_lanes(h)] = o_intra

        def step(c, carry):
            rows = _chunk_rows(c, HG_CHUNK)
            heads = range(HG_HEADS)
            states = [s_ref[h] for h in heads]
            inter = [_dot(qe_ref[h, rows, :], states[h], NT) for h in heads]
            updates = [_dot(i_ref[rows, _head_lanes(h)], kd_ref[h, rows, :], TN) for h in heads]
            for h in heads:
                st_ref[h, c] = states[h]
                o_ref[rows, _head_lanes(h)] += inter[h]
                s_ref[h] = states[h] * dec_ref[h, pl.ds(c * HG_CHUNK, 1), :] + updates[h]
            return carry

        lax.fori_loop(0, nsub, step, 0)
        for h in range(HG_HEADS):
            ln = _head_lanes(h)
            o_ref[:, ln] = _hg_post(o_ref[:, ln], g_ref[:, ln], nw_ref[:, ln])

    blk = pltpu.VMEM((HG_HEADS, tb, LANES), F32)
    return pl.pallas_call(
        body, name=name, grid=(nT,),
        in_specs=[col(0), col(1), col(2), col(3), vec, vec, vec],
        out_specs=[col(0), st],
        out_shape=[jax.ShapeDtypeStruct((T, HG_WIDTH), F32),
                   jax.ShapeDtypeStruct((HG_HEADS, T // HG_CHUNK, HG_HEAD_DIM, HG_HEAD_DIM), F32)],
        scratch_shapes=[pltpu.VMEM((HG_HEADS, HG_HEAD_DIM, HG_HEAD_DIM), F32), blk, blk, blk],
    )(proj, proj, proj, proj, l0, l1, nw)


def _hg_bwd(proj, states, do, do_blk, l0, l1, nw, name):
    T = proj.shape[0]
    tb = _row_tile(T, SCAN_ROWS)
    nsub = tb // HG_CHUNK
    nT, col, vec, st = _hg_specs(T, tb, True)

    def body(q_ref, f_ref, i_ref, g_ref, st_ref, do_ref, l0_ref, l1_ref, nw_ref,
             dq_ref, df_ref, di_ref, dg_ref, dl0_ref, dl1_ref, dnw_ref,
             ds_ref, qe_ref, kd_ref, dec_ref, o_ref, dqe_ref, dkd_ref, ddec_ref, dis_ref):
        @pl.when(pl.program_id(0) == 0)
        def _():
            ds_ref[...] = jnp.zeros_like(ds_ref)
            dl0_ref[...] = jnp.zeros_like(dl0_ref)
            dl1_ref[...] = jnp.zeros_like(dl1_ref)
            dnw_ref[...] = jnp.zeros_like(dnw_ref)

        consts = _hg_consts(tb)
        outs, prep_vjp = jax.vjp(functools.partial(_hg_prep, consts),
                                 *_hg_prep_args(q_ref, f_ref, i_ref, l0_ref, l1_ref))
        for h, (qe, o_intra, kd, dec) in enumerate(outs):
            qe_ref[h], kd_ref[h], dec_ref[h], o_ref[h] = qe, kd, dec, o_intra

        def redo(c, carry):
            rows = _chunk_rows(c, HG_CHUNK)
            inter = [_dot(qe_ref[h, rows, :], st_ref[h, c], NT) for h in range(HG_HEADS)]
            for h in range(HG_HEADS):
                o_ref[h, rows, :] += inter[h]
            return carry

        lax.fori_loop(0, nsub, redo, 0)
        for h in range(HG_HEADS):
            ln = _head_lanes(h)
            _, vjp = jax.vjp(_hg_post, o_ref[h], g_ref[:, ln], nw_ref[:, ln])
            d_o, dgr, dnw = vjp(do_ref[:, ln])
            o_ref[h] = d_o
            dg_ref[:, ln] = dgr.astype(dg_ref.dtype)
            dnw_ref[:, ln] += dnw
        ddec_ref[...] = jnp.zeros_like(ddec_ref)

        def step(i, carry):
            c = nsub - 1 - i
            rows = _chunk_rows(c, HG_CHUNK)
            row0 = pl.ds(c * HG_CHUNK, 1)
            heads = range(HG_HEADS)
            Gs = [ds_ref[h] for h in heads]
            Ss = [st_ref[h, c] for h in heads]
            d_os = [o_ref[h, rows, :] for h in heads]
            dqe = [_dot(d_os[h], Ss[h]) for h in heads]
            dkd = [_dot(i_ref[rows, _head_lanes(h)], Gs[h]) for h in heads]
            dis = [_dot(kd_ref[h, rows, :], Gs[h], NT) for h in heads]
            back = [_dot(d_os[h], qe_ref[h, rows, :], TN) for h in heads]
            for h in heads:
                dqe_ref[h, rows, :] = dqe[h]
                dkd_ref[h, rows, :] = dkd[h]
                dis_ref[h, rows, :] = dis[h]
                ddec_ref[h, row0, :] = jnp.sum(Ss[h] * Gs[h], axis=0, keepdims=True)
                ds_ref[h] = Gs[h] * dec_ref[h, row0, :] + back[h]
            return carry

        lax.fori_loop(0, nsub, step, 0)
        grads = prep_vjp(tuple((dqe_ref[h], o_ref[h], dkd_ref[h], ddec_ref[h]) for h in range(HG_HEADS)))
        for h in range(HG_HEADS):
            ln = _head_lanes(h)
            dq, df, di, dl0, dl1 = grads[N_HG_IN * h:N_HG_IN * (h + 1)]
            dq_ref[:, ln] = dq.astype(dq_ref.dtype)
            df_ref[:, ln] = df.astype(df_ref.dtype)
            di_ref[:, ln] = (di + dis_ref[h]).astype(di_ref.dtype)
            dl0_ref[:, ln] += dl0
            dl1_ref[:, ln] += dl1

    dcol = jax.ShapeDtypeStruct((T, HG_WIDTH), BF16)
    dvec = jax.ShapeDtypeStruct((1, HG_WIDTH), F32)
    blk = pltpu.VMEM((HG_HEADS, tb, LANES), F32)
    return pl.pallas_call(
        body, name=name, grid=(nT,),
        in_specs=[col(0), col(1), col(2), col(3), st, col(do_blk), vec, vec, vec],
        out_specs=[col(0)] * 4 + [vec] * 3,
        out_shape=[dcol] * 4 + [dvec] * 3,
        scratch_shapes=[pltpu.VMEM((HG_HEADS, HG_HEAD_DIM, HG_HEAD_DIM), F32)] + [blk] * 8,
    )(proj, proj, proj, proj, states, do, l0, l1, nw)


def _rw_consts(rows):
    same, tril, stril = _chunk_masks(rows, RW_CHUNK)
    br, bc = _iota2((LANES, LANES), 0), _iota2((LANES, LANES), 1)
    blockdiag = ((br < RW_HEAD_DIM) == (bc < RW_HEAD_DIM)).astype(F32)
    m0 = (_iota2((1, LANES), 1) < RW_HEAD_DIM).astype(F32)
    return same, tril, stril, blockdiag, m0, 1.0 - m0


def _unit_lower_inverses_impl(lows):
    rows = lows[0].shape[0]
    eye = (_iota2(lows[0].shape, 0) == _iota2(lows[0].shape, 1)).astype(F32)
    xs = [low + eye for low in lows]
    ps = [_dot(low, low) for low in lows]
    n = 4
    while n < RW_CHUNK:
        zs = [_dot(jnp.concatenate([p, x], axis=0), p) for p, x in zip(ps, xs)]
        ps = [z[:rows] for z in zs]
        xs = [x + z[rows:] for x, z in zip(xs, zs)]
        n *= 2
    return tuple(x + _dot(x, p) for x, p in zip(xs, ps))


@jax.custom_vjp
def _unit_lower_inverses(lows):
    return _unit_lower_inverses_impl(lows)


def _unit_lower_inverses_fwd(lows):
    xs = _unit_lower_inverses_impl(lows)
    return xs, xs


def _unit_lower_inverses_bwd(xs, dxs):
    ts = [_dot(x, dx, TN) for x, dx in zip(xs, dxs)]
    return (tuple(_dot(t, x, NT) for t, x in zip(ts, xs)),)


_unit_lower_inverses.defvjp(_unit_lower_inverses_fwd, _unit_lower_inverses_bwd)


N_PREP_IN = 12
RW_GROUP = 2


def _rw_prep(consts, *flat):
    same, tril, stril, blockdiag, m0, m1 = consts
    rows = tril.shape[0]
    masks = (m0, m1)
    pairs = [flat[N_PREP_IN * i:N_PREP_IN * (i + 1)] for i in range(len(flat) // N_PREP_IN)]
    lora = [(_dot(jnp.tanh(lw), w2p), _dot(lw, a2p), _dot(_sigmoid(gd), g2), _dot(jnp.square(kx * k_k), blockdiag))
            for _, kx, _, lw, gd, _, _, k_k, _, w2p, a2p, g2 in pairs]
    mid = []
    for (r, kx, v, lw, gd, w0, a0, k_k, k_a, w2p, a2p, g2), (xw, xa, g, kk_sq) in zip(pairs, lora):
        xw = w0 + xw
        w = jnp.minimum(xw, 0.0) - jnp.log(1.0 + jnp.exp(-jnp.abs(xw))) - 0.5
        a_s = _sigmoid(a0 + xa)
        kk = kx * k_k / jnp.maximum(jnp.sqrt(kk_sq), L2_EPS)
        mid.append((-jnp.exp(w), a_s, kk, kx * (1.0 + (a_s - 1.0) * k_a), g))
    accs = [_dot(jnp.concatenate([tril, same], axis=0), ld, NN, SUM_PRECISION) for ld, _, _, _, _ in mid]
    pre = []
    for (r, _, v, *_), (ld, a_s, kk, k2, g), acc in zip(pairs, mid, accs):
        bv = kk * a_s
        cum, tot = acc[:rows], acc[rows:]
        ecn = jnp.exp(-cum)
        a_t = -kk * jnp.exp(cum - ld)
        r_t = r * jnp.exp(cum)
        rem = jnp.exp(tot - cum)
        pre.append((v, a_t, r_t, (bv * ecn, k2 * ecn), (bv * rem, k2 * rem, jnp.exp(tot), k2, g)))
    zs = [_dot(jnp.concatenate([a_t * m0, a_t * m1, r_t * m0, r_t * m1], axis=0), jnp.concatenate(bk, axis=0), NT)
          for _, a_t, r_t, bk, _ in pre]
    pre = [(v, a_t, r_t, z, out) for (v, a_t, r_t, _, out), z in zip(pre, zs)]
    heads = [(i, h) for i in range(len(pre)) for h in range(2)]
    za = {ih: pre[ih[0]][3][ih[1] * rows:(ih[1] + 1) * rows] for ih in heads}
    zr = {ih: pre[ih[0]][3][(2 + ih[1]) * rows:(3 + ih[1]) * rows] for ih in heads}
    tinv = dict(zip(heads, _unit_lower_inverses(tuple(za[ih][:, :rows] * stril for ih in heads))))
    lv = {ih: _dot(jnp.concatenate([za[ih][:, rows:] * stril, zr[ih][:, rows:] * tril], axis=0), pre[ih[0]][0])
          for ih in heads}
    wu = {ih: _dot(tinv[ih], jnp.concatenate([pre[ih[0]][1] * masks[ih[1]], lv[ih][:rows]], axis=1)) for ih in heads}
    w_m = {ih: wu[ih][:, :LANES] for ih in heads}
    u_m = {ih: masks[ih[1]] * wu[ih][:, LANES:] for ih in heads}
    qy = {ih: _dot(zr[ih][:, :rows] * tril, jnp.concatenate([w_m[ih], u_m[ih]], axis=1)) for ih in heads}
    outs = []
    for i in range(len(pre)):
        a, b = (i, 0), (i, 1)
        W = w_m[a] + w_m[b]
        U = u_m[a] + u_m[b]
        Q = pre[i][2] + qy[a][:, :LANES] + qy[b][:, :LANES]
        Y0 = qy[a][:, LANES:] + qy[b][:, LANES:] + m0 * lv[a][rows:] + m1 * lv[b][rows:]
        outs.append((W, U, Q, Y0) + pre[i][4])
    return tuple(outs)


N_POST_IN = 8


def _rw_post(blockdiag, *flat):
    inv_n = 1.0 / RW_HEAD_DIM
    pairs = [flat[N_POST_IN * i:N_POST_IN * (i + 1)] for i in range(len(flat) // N_POST_IN)]
    sums = [(_dot(y, blockdiag), _dot(r * k2 * r_k, blockdiag)) for y, r, _, k2, _, r_k, _, _ in pairs]
    centred = [p[0] - s[0] * inv_n for p, s in zip(pairs, sums)]
    variances = [_dot(yc * yc, blockdiag) * inv_n for yc in centred]
    return tuple((yc * lax.rsqrt(var + RW_GN_EPS) * ln_w + ln_b + s[1] * v) * g
                 for (_, _, v, _, g, _, ln_w, ln_b), s, yc, var in zip(pairs, sums, centred, variances))


N_RW_VEC = 7
N_RW_MAT = 3


def _rw_specs(T, tb, rev):
    nT = T // tb
    tix = (lambda t: nT - 1 - t) if rev else (lambda t: t)
    wide = lambda blk: pl.BlockSpec((tb, RW_WIDTH), lambda t: (tix(t), blk))
    narrow = lambda blk: pl.BlockSpec((tb, LANES), lambda t: (tix(t), blk))
    vec = pl.BlockSpec((1, RW_WIDTH), lambda t: (0, 0))
    mat = pl.BlockSpec((RW_PAIRS, LANES, LANES), lambda t: (0, 0, 0))
    st = pl.BlockSpec((RW_PAIRS, tb // RW_CHUNK, LANES, LANES), lambda t: (0, tix(t), 0, 0))
    lora0 = 3 * RW_WIDTH // LANES
    ins = [wide(0), wide(1), wide(2), narrow(lora0), narrow(lora0 + 1)]
    return nT, wide, vec, mat, st, ins


def _rw_prep_args(p, r_ref, k_ref, v_ref, lw_ref, gd_ref, vrefs, mrefs):
    ln = _head_lanes(p)
    w0, a0, k_k, k_a = [x[:, ln] for x in vrefs[:4]]
    return (r_ref[:, ln], k_ref[:, ln], v_ref[:, ln], lw_ref[...], gd_ref[...], w0, a0, k_k, k_a,
            *[x[p] for x in mrefs])


def _stack_chunks(ref, top, bottom):
    C = RW_CHUNK
    for c in range(ref.shape[0]):
        ref[c, 0:C, :] = top[c * C:(c + 1) * C]
        ref[c, C:2 * C, :] = bottom[c * C:(c + 1) * C]


def _group_args(p0, r_ref, k_ref, v_ref, lw_ref, gd_ref, vrefs, mrefs):
    flat = []
    for p in range(p0, p0 + RW_GROUP):
        flat += list(_rw_prep_args(p, r_ref, k_ref, v_ref, lw_ref, gd_ref, vrefs, mrefs))
    return flat


def _rw_fwd(rws, vecs, mats, name):
    T = rws.shape[0]
    tb = _row_tile(T, SCAN_ROWS)
    nsub = tb // RW_CHUNK
    C = RW_CHUNK
    nT, wide, vec, mat, st, ins = _rw_specs(T, tb, False)

    def body(*refs):
        r_ref, k_ref, v_ref, lw_ref, gd_ref = refs[:5]
        vrefs = refs[5:5 + N_RW_VEC]
        mrefs = refs[5 + N_RW_VEC:5 + N_RW_VEC + N_RW_MAT]
        o_ref, st_ref, s_ref, wq_ref, uy_ref, bk_ref, misc_ref, y_ref = refs[-8:]

        @pl.when(pl.program_id(0) == 0)
        def _():
            s_ref[...] = jnp.zeros_like(s_ref)

        consts = _rw_consts(tb)
        blockdiag = consts[3]
        for p0 in range(0, RW_PAIRS, RW_PAIRS):
            flat = []
            for p in range(RW_PAIRS):
                flat += list(_rw_prep_args(p, r_ref, k_ref, v_ref, lw_ref, gd_ref, vrefs, mrefs))
            outs = _rw_prep(consts, *flat)
            for p, (W, U, Q, Y0, Bg, Kg, dec, k2, g) in zip(range(RW_PAIRS), outs):
                _stack_chunks(wq_ref.at[p], W, Q)
                _stack_chunks(uy_ref.at[p], U, Y0)
                _stack_chunks(bk_ref.at[p], Bg, Kg)
                misc_ref[0, p], misc_ref[1, p], misc_ref[2, p] = dec, k2, g

        def step(c, carry):
            rows = _chunk_rows(c, C)
            states = [s_ref[p] for p in range(RW_PAIRS)]
            for p, S in enumerate(states):
                st_ref[p, c] = S
            pys = [_dot(wq_ref[p, c], S, NT) + uy_ref[p, c] for p, S in enumerate(states)]
            pvs = [jnp.concatenate([py[:C], v_ref[rows, _head_lanes(p)]], axis=0) for p, py in enumerate(pys)]
            updates = [_dot(pv, bk_ref[p, c], TN) for p, pv in enumerate(pvs)]
            for p, S in enumerate(states):
                y_ref[p, rows, :] = pys[p][C:]
                s_ref[p] = (S * misc_ref[0, p, pl.ds(c * C, 1), :] + updates[p]) * blockdiag
            return carry

        lax.fori_loop(0, nsub, step, 0)
        flat = []
        for p in range(RW_PAIRS):
            ln = _head_lanes(p)
            flat += [y_ref[p], r_ref[:, ln], v_ref[:, ln], misc_ref[1, p], misc_ref[2, p]] + [x[:, ln] for x in vrefs[4:]]
        for p, out in enumerate(_rw_post(blockdiag, *flat)):
            o_ref[:, _head_lanes(p)] = out

    stacked = pltpu.VMEM((RW_PAIRS, nsub, 2 * C, LANES), F32)
    return pl.pallas_call(
        body, name=name, grid=(nT,),
        in_specs=ins + [vec] * N_RW_VEC + [mat] * N_RW_MAT,
        out_specs=[wide(0), st],
        out_shape=[jax.ShapeDtypeStruct((T, RW_WIDTH), F32),
                   jax.ShapeDtypeStruct((RW_PAIRS, T // RW_CHUNK, LANES, LANES), F32)],
        scratch_shapes=[pltpu.VMEM((RW_PAIRS, LANES, LANES), F32), stacked, stacked, stacked,
                        pltpu.VMEM((3, RW_PAIRS, tb, LANES), F32), pltpu.VMEM((RW_PAIRS, tb, LANES), F32)],
    )(rws, rws, rws, rws, rws, *vecs, *mats)


def _rw_bwd(rws, states, do, do_blk, vecs, mats, name):
    T = rws.shape[0]
    tb = _row_tile(T, SCAN_ROWS)
    nsub = tb // RW_CHUNK
    C = RW_CHUNK
    G = RW_GROUP
    nT, wide, vec, mat, st, ins = _rw_specs(T, tb, True)
    nin = 5 + 1 + 1 + N_RW_VEC + N_RW_MAT

    def body(*refs):
        r_ref, k_ref, v_ref, lw_ref, gd_ref = refs[:5]
        st_ref, do_ref = refs[5], refs[6]
        vrefs = refs[7:7 + N_RW_VEC]
        mrefs = refs[7 + N_RW_VEC:nin]
        dr_ref, dk_ref, dv_ref, dlo_ref = refs[nin:nin + 4]
        dvec = refs[nin + 4:nin + 4 + N_RW_VEC]
        dmat = refs[nin + 4 + N_RW_VEC:nin + 4 + N_RW_VEC + N_RW_MAT]
        ds_ref, wq_ref, uy_ref, bk_ref, pv_ref, dec_ref, y_ref, dpre_ref, dvs_ref = refs[-9:]

        @pl.when(pl.program_id(0) == 0)
        def _():
            ds_ref[...] = jnp.zeros_like(ds_ref)
            for x in dvec + dmat:
                x[...] = jnp.zeros_like(x)

        consts = _rw_consts(tb)
        blockdiag = consts[3]
        dlw, dgd = 0.0, 0.0
        for p0 in range(0, RW_PAIRS, G):
            outs, prep_vjp = jax.vjp(functools.partial(_rw_prep, consts),
                                     *_group_args(p0, r_ref, k_ref, v_ref, lw_ref, gd_ref, vrefs, mrefs))
            for q, (W, U, Q, Y0, Bg, Kg, dec, _, _) in enumerate(outs):
                _stack_chunks(wq_ref.at[q], W, Q)
                _stack_chunks(uy_ref.at[q], U, Y0)
                _stack_chunks(bk_ref.at[q], Bg, Kg)
                dec_ref[q] = dec

            def redo(c, carry, p0=p0):
                rows = _chunk_rows(c, C)
                pys = [_dot(wq_ref[q, c], st_ref[p0 + q, c], NT) + uy_ref[q, c] for q in range(G)]
                for q, py in enumerate(pys):
                    y_ref[q, rows, :] = py[C:]
                    pv_ref[q, c, 0:C, :] = py[:C]
                    pv_ref[q, c, C:2 * C, :] = v_ref[rows, _head_lanes(p0 + q)]
                return carry

            lax.fori_loop(0, nsub, redo, 0)
            flat = []
            for q in range(G):
                ln = _head_lanes(p0 + q)
                flat += [y_ref[q], r_ref[:, ln], v_ref[:, ln], outs[q][7], outs[q][8]] + [x[:, ln] for x in vrefs[4:]]
            _, post_vjp = jax.vjp(functools.partial(_rw_post, blockdiag), *flat)
            post_grads = post_vjp(tuple(do_ref[:, _head_lanes(p0 + q)] for q in range(G)))
            post = []
            for q in range(G):
                ln = _head_lanes(p0 + q)
                dy, dr2, dv2, dk2, dg, dr_k, dln_w, dln_b = post_grads[N_POST_IN * q:N_POST_IN * (q + 1)]
                dpre_ref[q, 3] = dy
                dvs_ref[q] = dv2
                for x, gx in zip(dvec[4:], (dr_k, dln_w, dln_b)):
                    x[:, ln] += gx
                dpre_ref[q, 6] = jnp.zeros_like(dpre_ref[q, 6])
                post.append((dr2, dk2, dg))

            def step(i, carry, p0=p0):
                c = nsub - 1 - i
                rows = _chunk_rows(c, C)
                row0 = pl.ds(c * C, 1)
                qs = range(G)
                Gs = [ds_ref[p0 + q] * blockdiag for q in qs]
                Ss = [st_ref[p0 + q, c] for q in qs]
                t1 = [_dot(bk_ref[q, c], Gs[q], NT) for q in qs]
                t3 = [_dot(pv_ref[q, c], Gs[q]) for q in qs]
                dpy = [jnp.concatenate([t1[q][:C], dpre_ref[q, 3, rows, :]], axis=0) for q in qs]
                t2 = [_dot(dpy[q], Ss[q]) for q in qs]
                back = [_dot(dpy[q], wq_ref[q, c], TN) for q in qs]
                for q in qs:
                    dvs_ref[q, rows, :] += t1[q][C:]
                    dpre_ref[q, 0, rows, :] = t2[q][:C]
                    dpre_ref[q, 1, rows, :] = t1[q][:C]
                    dpre_ref[q, 2, rows, :] = t2[q][C:]
                    dpre_ref[q, 4, rows, :] = t3[q][:C]
                    dpre_ref[q, 5, rows, :] = t3[q][C:]
                    dpre_ref[q, 6, row0, :] = jnp.sum(Ss[q] * Gs[q], axis=0, keepdims=True)
                    ds_ref[p0 + q] = Gs[q] * dec_ref[q, row0, :] + back[q]
                return carry

            lax.fori_loop(0, nsub, step, 0)
            grads = prep_vjp(tuple(tuple(dpre_ref[q, i] for i in range(7)) + post[q][1:] for q in range(G)))
            for q in range(G):
                ln = _head_lanes(p0 + q)
                gq = grads[N_PREP_IN * q:N_PREP_IN * (q + 1)]
                dr_ref[:, ln] = gq[0] + post[q][0]
                dk_ref[:, ln] = gq[1]
                dv_ref[:, ln] = gq[2] + dvs_ref[q]
                dlw = dlw + gq[3]
                dgd = dgd + gq[4]
                for x, gx in zip(dvec[:4], gq[5:9]):
                    x[:, ln] += gx
                for x, gx in zip(dmat, gq[9:]):
                    x[p0 + q] += gx
        dlo_ref[:, 0:LANES] = dlw
        dlo_ref[:, LANES:2 * LANES] = dgd

    dcol = jax.ShapeDtypeStruct((T, RW_WIDTH), F32)
    dlo_spec = pl.BlockSpec((tb, 2 * LANES), lambda t: (nT - 1 - t, 0))
    blk = pltpu.VMEM((G, tb, LANES), F32)
    stacked = pltpu.VMEM((G, nsub, 2 * C, LANES), F32)
    return pl.pallas_call(
        body, name=name, grid=(nT,),
        in_specs=ins + [st, wide(do_blk)] + [vec] * N_RW_VEC + [mat] * N_RW_MAT,
        out_specs=[wide(0)] * 3 + [dlo_spec] + [vec] * N_RW_VEC + [mat] * N_RW_MAT,
        out_shape=[dcol] * 3 + [jax.ShapeDtypeStruct((T, 2 * LANES), F32)]
        + [jax.ShapeDtypeStruct((1, RW_WIDTH), F32)] * N_RW_VEC
        + [jax.ShapeDtypeStruct((RW_PAIRS, LANES, LANES), F32)] * N_RW_MAT,
        scratch_shapes=[pltpu.VMEM((RW_PAIRS, LANES, LANES), F32), stacked, stacked, stacked, stacked, blk, blk,
                        pltpu.VMEM((G, 7, tb, LANES), F32), blk],
    )(rws, rws, rws, rws, rws, states, do, *vecs, *mats)


def _my_index():
    return 4 * lax.axis_index("x") + 2 * lax.axis_index("y") + lax.axis_index("c")


def _peer(bits):
    pos = []
    for name, flip in zip(("x", "y", "c"), bits):
        i = lax.axis_index(name)
        pos.append(1 - i if flip else i)
    return tuple(pos)


def _peer_index(bits):
    x, y, c = _peer(bits)
    return 4 * x + 2 * y + c


def _all_gather(shards, name):
    n = len(shards)
    chips = [(1, 0, 0), (0, 1, 0), (1, 1, 0)]
    sib = (0, 0, 1)

    def body(*refs):
        ins, outs = refs[:n], refs[n:2 * n]
        send_sems, recv_sems, local_sems = refs[2 * n:]

        def rows(k, dev):
            r = ins[k].shape[0]
            return outs[k].at[pl.ds(dev * r, r), :]

        def copy(k, slot, block_dev, to_bits, src=None):
            return pltpu.make_async_remote_copy(
                src_ref=rows(k, block_dev) if src is None else src, dst_ref=rows(k, block_dev),
                send_sem=send_sems.at[k, slot], recv_sem=recv_sems.at[k, slot],
                device_id=_peer(to_bits), device_id_type=MESH_ID)

        me = _my_index()
        started = []
        for k in range(n):
            mine = pltpu.make_async_copy(ins[k], rows(k, me), local_sems.at[k])
            mine.start()
            started.append(mine)
        sends = []
        for k in range(n):
            first = [copy(k, 0, me, sib, src=ins[k])]
            first += [copy(k, 1 + j, me, chip, src=ins[k]) for j, chip in enumerate(chips)]
            for cp in first:
                cp.start()
            sends += first
        for k in range(n):
            for j, chip in enumerate(chips):
                copy(k, 1 + j, _peer_index(chip), chip).wait_recv()
                fwd = copy(k, 4 + j, _peer_index(chip), sib)
                fwd.start()
                sends.append(fwd)
        for k in range(n):
            copy(k, 0, _peer_index(sib), sib).wait_recv()
            for j, chip in enumerate(chips):
                both = (chip[0], chip[1], 1)
                copy(k, 4 + j, _peer_index(both), sib).wait_recv()
        for cp in sends:
            cp.wait_send()
        for cp in started:
            cp.wait()

    any_spec = pl.BlockSpec(memory_space=pl.ANY)
    return pl.pallas_call(
        body, name=name,
        in_specs=[any_spec] * n, out_specs=[any_spec] * n,
        out_shape=[jax.ShapeDtypeStruct((N_DEV * s.shape[0], s.shape[1]), s.dtype) for s in shards],
        scratch_shapes=[pltpu.SemaphoreType.DMA((n, 7)), pltpu.SemaphoreType.DMA((n, 7)),
                        pltpu.SemaphoreType.DMA((n,))],
    )(*shards)


def _exchange(partials, name):
    n = len(partials)
    flips = [(dx, dy, dc) for dx in (0, 1) for dy in (0, 1) for dc in (0, 1)][1:]

    def body(*refs):
        ins, outs = refs[:n], refs[n:2 * n]
        send_sems, recv_sems, local_sems = refs[2 * n:]
        me = _my_index()
        local = []
        for k in range(n):
            cp = pltpu.make_async_copy(ins[k].at[me], outs[k].at[me], local_sems.at[k])
            cp.start()
            local.append(cp)
        copies = []
        for k in range(n):
            for d, bits in enumerate(flips):
                cp = pltpu.make_async_remote_copy(
                    src_ref=ins[k].at[_peer_index(bits)], dst_ref=outs[k].at[me],
                    send_sem=send_sems.at[k, d], recv_sem=recv_sems.at[k, d],
                    device_id=_peer(bits), device_id_type=MESH_ID)
                cp.start()
                copies.append(cp)
        for cp in copies:
            cp.wait_recv()
        for cp in copies:
            cp.wait_send()
        for cp in local:
            cp.wait()

    any_spec = pl.BlockSpec(memory_space=pl.ANY)
    return pl.pallas_call(
        body, name=name,
        in_specs=[any_spec] * n, out_specs=[any_spec] * n,
        out_shape=[jax.ShapeDtypeStruct(p.shape, p.dtype) for p in partials],
        scratch_shapes=[pltpu.SemaphoreType.DMA((n, 7)), pltpu.SemaphoreType.DMA((n, 7)),
                        pltpu.SemaphoreType.DMA((n,))],
    )(*partials)


HBM_SPEC = pl.BlockSpec(memory_space=pltpu.HBM)
SEM_SPEC = pl.BlockSpec(memory_space=pltpu.SEMAPHORE)
ALL_FLIPS = [(dx, dy, dc) for dx in (0, 1) for dy in (0, 1) for dc in (0, 1)][1:]


def _spread_copies(srcs, lands, send_sems, recv_sems, to_x):
    me = _my_index()
    my_x = lax.axis_index("x")
    copies = []
    for k, land in enumerate(lands):
        for d, bits in enumerate(ALL_FLIPS):
            if not srcs:
                src = land.at[me]
            elif to_x is None:
                src = srcs[k].at[_peer_index(bits)]
            else:
                _, py, pc = _peer(bits)
                src = srcs[k].at[2 * py + pc]
            cp = pltpu.make_async_remote_copy(
                src_ref=src, dst_ref=land.at[me],
                send_sem=send_sems.at[k * 7 + d], recv_sem=recv_sems.at[k * 7 + d],
                device_id=_peer(bits), device_id_type=MESH_ID)
            sends = True if to_x is None else my_x == (to_x ^ bits[0])
            receives = True if to_x is None else my_x == to_x
            copies.append((cp, sends, receives))
    return copies


def _when(cond, fn):
    if cond is True:
        fn()
    else:
        pl.when(cond)(fn)


def _spread_start(srcs, lands, name, to_x=None):
    ns, n = len(srcs), len(lands)

    def body(*refs):
        src_refs, land_refs = refs[:ns], refs[ns:ns + n]
        send_sems, recv_sems = refs[ns + n], refs[ns + n + 1]
        token = refs[-1]
        for cp, sends, _ in _spread_copies(src_refs, land_refs, send_sems, recv_sems, to_x):
            _when(sends, cp.start)
        token[...] = jnp.zeros_like(token)

    bufs = list(srcs) + list(lands)
    out = pl.pallas_call(
        body, name=name,
        out_shape=(pltpu.SemaphoreType.DMA((7 * n,)), pltpu.SemaphoreType.DMA((7 * n,)),
                   *[pltpu.HBM(b.shape, b.dtype) for b in bufs], jax.ShapeDtypeStruct((8, LANES), F32)),
        in_specs=[HBM_SPEC] * (ns + n),
        out_specs=(SEM_SPEC, SEM_SPEC, *[HBM_SPEC] * (ns + n), pl.BlockSpec(memory_space=pltpu.VMEM)),
        input_output_aliases={i: 2 + i for i in range(ns + n)},
        compiler_params=pltpu.CompilerParams(has_side_effects=pltpu.SideEffectType.DATAFLOW_SIDE_EFFECTING),
    )(*[pltpu.with_memory_space_constraint(b, pltpu.HBM) for b in bufs])
    return out[0], out[1], list(out[2:2 + ns]), list(out[2 + ns:2 + ns + n]), out[-1]


def _spread_wait(send_sems, recv_sems, srcs, lands, after, name, to_x=None):
    ns, n = len(srcs), len(lands)

    def body(*refs):
        src_refs, land_refs = refs[:ns], refs[ns:ns + n]
        send_sems, recv_sems = refs[ns + n], refs[ns + n + 1]
        for cp, sends, receives in _spread_copies(src_refs, land_refs, send_sems, recv_sems, to_x):
            _when(sends, cp.wait_send)
            _when(receives, cp.wait_recv)

    bufs = list(srcs) + list(lands)
    out = pl.pallas_call(
        body, name=name,
        out_shape=tuple(pltpu.HBM(b.shape, b.dtype) for b in bufs),
        in_specs=[HBM_SPEC] * (ns + n) + [SEM_SPEC, SEM_SPEC, pl.BlockSpec(memory_space=pl.ANY)],
        out_specs=tuple([HBM_SPEC] * (ns + n)),
        input_output_aliases={i: i for i in range(ns + n)},
        compiler_params=pltpu.CompilerParams(has_side_effects=pltpu.SideEffectType.DATAFLOW_SIDE_EFFECTING),
    )(*bufs, send_sems, recv_sems, after)
    return list(out[ns:])


def _own_slot_only(block, me):
    return lax.dynamic_update_slice(lax.empty((N_DEV,) + block.shape, block.dtype), block[None], (me, 0, 0))


def _sum_slots(landed, name):
    _, R, C = landed.shape
    tb = _row_tile(R, 128)

    def body(l_ref, o_ref):
        acc = l_ref[0].astype(F32)
        for s in range(1, N_DEV):
            acc = acc + l_ref[s].astype(F32)
        o_ref[...] = acc

    return pl.pallas_call(
        body, name=name, grid=(R // tb,),
        in_specs=[pl.BlockSpec((N_DEV, tb, C), lambda i: (0, i, 0))],
        out_specs=pl.BlockSpec((tb, C), lambda i: (i, 0)),
        out_shape=jax.ShapeDtypeStruct((R, C), F32),
    )(landed)


def _pack_rows(flat_list, width=LANES):
    flat = jnp.concatenate([a.reshape(-1) for a in flat_list])
    n = flat.shape[0]
    rows = -(-n // width)
    rows = -(-rows // 8) * 8
    return jnp.pad(flat, (0, rows * width - n)).reshape(rows, width)


def _unpack(packed, shapes):
    flat = packed.reshape(-1)
    out, off = [], 0
    for s in shapes:
        n = 1
        for d in s:
            n *= d
        out.append(flat[off:off + n].reshape(s))
        off += n
    return out


def kernel(x, norm1_w, w_in, hg_lb_logits, hg_norm_w, rw_shift_mu, rw_w0, rw_w2, rw_a0, rw_a2, rw_g2, rw_k_k, rw_k_a, rw_r_k, rw_ln_w, rw_ln_b, w_out, norm2_w, w_up, conv_w, conv_b, w_down, final_norm_w, loss_target, m_norm1_w, m_w_in, m_hg_lb_logits, m_hg_norm_w, m_rw_shift_mu, m_rw_w0, m_rw_w2, m_rw_a0, m_rw_a2, m_rw_g2, m_rw_k_k, m_rw_k_a, m_rw_r_k, m_rw_ln_w, m_rw_ln_b, m_w_out, m_norm2_w, m_w_up, m_conv_w, m_conv_b, m_w_down, m_final_norm_w, v_norm1_w, v_w_in, v_hg_lb_logits, v_hg_norm_w, v_rw_shift_mu, v_rw_w0, v_rw_w2, v_rw_a0, v_rw_a2, v_rw_g2, v_rw_k_k, v_rw_k_a, v_rw_r_k, v_rw_ln_w, v_rw_ln_b, v_w_out, v_norm2_w, v_w_up, v_conv_w, v_conv_b, v_w_down, v_final_norm_w):
    weights = dict(norm1_w=norm1_w, w_in=w_in, hg_lb_logits=hg_lb_logits, hg_norm_w=hg_norm_w,
                   rw_shift_mu=rw_shift_mu, rw_w0=rw_w0, rw_w2=rw_w2, rw_a0=rw_a0, rw_a2=rw_a2, rw_g2=rw_g2,
                   rw_k_k=rw_k_k, rw_k_a=rw_k_a, rw_r_k=rw_r_k, rw_ln_w=rw_ln_w, rw_ln_b=rw_ln_b, w_out=w_out,
                   norm2_w=norm2_w, w_up=w_up, conv_w=conv_w, conv_b=conv_b, w_down=w_down,
                   final_norm_w=final_norm_w)
    m_in = dict(norm1_w=m_norm1_w, w_in=m_w_in, hg_lb_logits=m_hg_lb_logits, hg_norm_w=m_hg_norm_w,
                rw_shift_mu=m_rw_shift_mu, rw_w0=m_rw_w0, rw_w2=m_rw_w2, rw_a0=m_rw_a0, rw_a2=m_rw_a2,
                rw_g2=m_rw_g2, rw_k_k=m_rw_k_k, rw_k_a=m_rw_k_a, rw_r_k=m_rw_r_k, rw_ln_w=m_rw_ln_w,
                rw_ln_b=m_rw_ln_b, w_out=m_w_out, norm2_w=m_norm2_w, w_up=m_w_up, conv_w=m_conv_w,
                conv_b=m_conv_b, w_down=m_w_down, final_norm_w=m_final_norm_w)
    v_in = dict(norm1_w=v_norm1_w, w_in=v_w_in, hg_lb_logits=v_hg_lb_logits, hg_norm_w=v_hg_norm_w,
                rw_shift_mu=v_rw_shift_mu, rw_w0=v_rw_w0, rw_w2=v_rw_w2, rw_a0=v_rw_a0, rw_a2=v_rw_a2,
                rw_g2=v_rw_g2, rw_k_k=v_rw_k_k, rw_k_a=v_rw_k_a, rw_r_k=v_rw_r_k, rw_ln_w=v_rw_ln_w,
                rw_ln_b=v_rw_ln_b, w_out=v_w_out, norm2_w=v_norm2_w, w_up=v_w_up, conv_w=v_conv_w,
                conv_b=v_conv_b, w_down=v_w_down, final_norm_w=v_final_norm_w)
    names = list(weights)
    sharded_small = ["rw_w2", "rw_a2", "rw_g2", "conv_w"]
    replicated = [n for n in names if n not in sharded_small + ["w_in", "w_out", "w_up", "w_down"]]

    xs = x[0]
    tgt = loss_target[0]

    small_shard = _pack_rows([weights[n] for n in sharded_small])
    g_win_t, g_small = _all_gather([w_in[0].T.astype(BF16), small_shard], "gather_weights")
    me = _my_index()
    later = (w_up[0].T.astype(BF16), w_out[0].astype(BF16), w_down[0].astype(BF16))
    later, _ = lax.optimization_barrier((later, g_small))
    later = [_own_slot_only(z, me) for z in later]
    g_send, g_recv, _, later, g_token = _spread_start([], later, "gather_later_start")
    small_shapes = [weights[n].shape for n in sharded_small]
    per_dev = [_unpack(g_small.reshape(N_DEV, -1)[j], small_shapes) for j in range(N_DEV)]
    w2_full, a2_full, g2_full, convw_full = [jnp.concatenate([per_dev[j][i][0] for j in range(N_DEV)], axis=-1)
                                             for i in range(4)]
    zeros64 = jnp.zeros((RW_PAIRS, 64, LANES), F32)
    by_pair = lambda z: z.reshape(z.shape[0], RW_PAIRS, LANES).transpose(1, 0, 2)
    w2p = jnp.concatenate([by_pair(w2_full), zeros64], axis=1)
    a2p = jnp.concatenate([zeros64, by_pair(a2_full)], axis=1)
    g2p = by_pair(g2_full)

    l0, l1 = hg_lb_logits[0:1], hg_lb_logits[1:2]
    h1, proj = _rms_mm_nt(xs, norm1_w + g_token[0:1, 0:1], g_win_t, "norm1_proj_in")
    o_hg, hg_states = _hg_fwd(proj, l0, l1, hg_norm_w, "hgrn2_fwd")
    rws = _shift_fwd(proj, rw_shift_mu, "token_shift")
    rw_vecs = [rw_w0, rw_a0, rw_k_k, rw_k_a, rw_r_k, rw_ln_w, rw_ln_b]
    rw_mats = [w2p, a2p, g2p]
    o_rw, rw_states = _rw_fwd(rws, rw_vecs, rw_mats, "rwkv7_fwd")
    o_mix = jnp.concatenate([o_hg, o_rw], axis=-1).astype(BF16)
    g_wup_t, g_wout, g_wdown = [z.reshape(-1, z.shape[-1])
                                for z in _spread_wait(g_send, g_recv, [], later, o_mix, "gather_later_wait")]
    x1 = _mm_nn(o_mix, g_wout, xs, "proj_out")
    h2, u = _rms_mm_nt(x1, norm2_w, g_wup_t, "norm2_ffn_up")
    act = _ffn_act_fwd(u, convw_full, conv_b, "ffn_act")
    loss_part, dx2, d_final_w = _mm_nn_loss_head(act, g_wdown, x1, final_norm_w.reshape(1, -1), tgt,
                                                 "ffn_down_loss_head")

    d_wdown = _mm_tn(act, dx2, 1408, "ffn_down_dw", BF16)
    dact = _mm_nt(dx2, g_wdown, "ffn_down_dx", BF16)
    du_g, du_v, dcw_g, dcw_v, dcb_g, dcb_v = _ffn_act_bwd(u, dact, convw_full, conv_b, "ffn_act_bwd")
    d_convw = jnp.concatenate([dcw_g, dcw_v], axis=-1)
    d_convb = jnp.concatenate([dcb_g, dcb_v], axis=-1)
    d_wup_t = jnp.concatenate([_mm_tn(du_g, h2, 1408, "ffn_up_dw_gate", BF16),
                               _mm_tn(du_v, h2, 1408, "ffn_up_dw_value", BF16)], axis=0)
    dx1, d_norm2 = _mm_nn_rms_bwd([du_g, du_v], g_wup_t, x1, norm2_w, dx2, "ffn_up_dx_norm2_bwd")
    d_wout = _mm_tn(o_mix, dx1, 512, "proj_out_dw", BF16)
    do = _mm_nt(dx1, g_wout, "proj_out_dx")
    early = [z.reshape(N_DEV, z.shape[0] // N_DEV, z.shape[1]) for z in (d_wup_t, d_wout, d_wdown)]
    early_land = [_own_slot_only(lax.dynamic_index_in_dim(z, me, 0, keepdims=False), me) for z in early]
    e_send, e_recv, early, early_land, e_token = _spread_start(early, early_land, "exchange_early_start")
    hg_norm_w_t = hg_norm_w + e_token[0:1, 0:1]
    dq, df, di, dg, d_l0, d_l1, d_hg_nw = _hg_bwd(proj, hg_states, do, 0, l0, l1, hg_norm_w_t, "hgrn2_bwd")
    half = N_DEV // 2
    own_half_block = lambda z: _own_slot_only(lax.dynamic_index_in_dim(z, me % half, 0, keepdims=False), me)
    n_lo = half * w_in.shape[2]
    d_win_lo = _mm_tn(jnp.concatenate([dq, df, di, dg[:, :n_lo - 3 * HG_WIDTH]], axis=-1), h1, 640,
                      "proj_in_dw_low", BF16).reshape(half, -1, D_MODEL)
    m_send, m_recv, mid, mid_land, m_token = _spread_start([d_win_lo], [own_half_block(d_win_lo)],
                                                            "exchange_mid_start", to_x=0)
    rw_vecs_t = [rw_vecs[0] + m_token[0:1, 0:1]] + rw_vecs[1:]
    rw_out = _rw_bwd(rws, rw_states, do, 1, rw_vecs_t, rw_mats, "rwkv7_bwd")
    d_rw_vecs = rw_out[4:4 + N_RW_VEC]
    d_w2p, d_a2p, d_g2p = rw_out[4 + N_RW_VEC:]
    dp_parts, dmu_parts = [], []
    for i, z in enumerate(rw_out[:4]):
        dp, dmu = _shift_bwd(z, proj, rw_shift_mu, i * RW_WIDTH, "token_shift_bwd_%d" % i)
        dp_parts.append(dp)
        dmu_parts.append(dmu)
    d_mu = jnp.concatenate(dmu_parts, axis=-1)
    d_win_hi = _mm_tn(jnp.concatenate([dg[:, n_lo - 3 * HG_WIDTH:]] + dp_parts, axis=-1), h1, 640,
                      "proj_in_dw_high", BF16).reshape(half, -1, D_MODEL)
    from_pairs = lambda z: z.transpose(1, 0, 2).reshape(z.shape[1], RW_WIDTH)
    d_w2 = from_pairs(d_w2p[:, :64])
    d_a2 = from_pairs(d_a2p[:, 64:])
    d_g2 = from_pairs(d_g2p)
    col_blocks = lambda z: z.reshape(z.shape[0], N_DEV, -1).transpose(1, 0, 2)
    small_part = jnp.stack([
        _pack_rows([col_blocks(d_w2)[j], col_blocks(d_a2)[j], col_blocks(d_g2)[j], col_blocks(d_convw)[j]])
        for j in range(N_DEV)])
    l_send, l_recv, late, late_land, l_token = _spread_start([d_win_hi], [own_half_block(d_win_hi)],
                                                             "exchange_late_start", to_x=1)
    grad_x, d_norm1 = _mm_nn_rms_bwd([dq, df, di, dg] + dp_parts, g_win_t, xs, norm1_w + l_token[0:1, 0:1], dx1,
                                     "proj_in_dx_norm1_bwd")

    rep_grads = dict(norm1_w=d_norm1, hg_lb_logits=jnp.concatenate([d_l0, d_l1], axis=0), hg_norm_w=d_hg_nw,
                     rw_shift_mu=d_mu, rw_w0=d_rw_vecs[0], rw_a0=d_rw_vecs[1], rw_k_k=d_rw_vecs[2],
                     rw_k_a=d_rw_vecs[3], rw_r_k=d_rw_vecs[4], rw_ln_w=d_rw_vecs[5], rw_ln_b=d_rw_vecs[6],
                     norm2_w=d_norm2, conv_b=d_convb, final_norm_w=d_final_w)
    rep_pack = _pack_rows([loss_part] + [rep_grads[n] for n in replicated])
    rep_part = jnp.broadcast_to(rep_pack[None], (N_DEV,) + rep_pack.shape)
    grads, delta, new_m, new_v = {}, {}, {}, {}

    def adamw_big(n, g):
        shp = weights[n].shape
        as2d = lambda z: z.reshape(shp[1], shp[2])
        grads[n] = g[None]
        d, nm, nv = _adamw(as2d(weights[n]), g, as2d(m_in[n]), as2d(v_in[n]), "adamw_" + n)
        delta[n], new_m[n], new_v[n] = d.reshape(shp), nm.reshape(shp), nv.reshape(shp)

    landed_early = _spread_wait(e_send, e_recv, early, early_land, grad_x, "exchange_early_wait")
    adamw_big("w_up", _sum_slots(landed_early[0], "sum_grads_w_up").T)
    adamw_big("w_out", _sum_slots(landed_early[1], "sum_grads_w_out"))
    adamw_big("w_down", _sum_slots(landed_early[2], "sum_grads_w_down"))
    (landed_mid,) = _spread_wait(m_send, m_recv, mid, mid_land, grad_x, "exchange_mid_wait", to_x=0)
    (landed_late,) = _spread_wait(l_send, l_recv, late, late_land, delta["w_down"], "exchange_late_wait", to_x=1)
    g_win = jnp.where(lax.axis_index("x") == 0, _sum_slots(landed_mid, "sum_grads_w_in_low"),
                      _sum_slots(landed_late, "sum_grads_w_in_high"))
    adamw_big("w_in", g_win.T)
    landed_rep, landed_small = _exchange([rep_part, small_part], "exchange_grads")
    g_small_sum = _unpack(_sum_slots(landed_small, "sum_grads_small"), small_shapes)
    rep_sum = _unpack(_sum_slots(landed_rep, "sum_grads_replicated"), [(1, 1)] + [weights[n].shape for n in replicated])
    loss = rep_sum[0].reshape(())
    grads.update(dict(zip(replicated, rep_sum[1:])))
    grads.update(dict(zip(sharded_small, g_small_sum)))

    small_names = replicated + sharded_small
    packs = [_pack_rows([src[n] for n in small_names]) for src in (weights, grads, m_in, v_in)]
    outs = _adamw(*packs, "adamw_small")
    small_shapes_all = [weights[n].shape for n in small_names]
    for dst, packed in zip((delta, new_m, new_v), outs):
        dst.update(dict(zip(small_names, _unpack(packed, small_shapes_all))))

    return (loss, grad_x[None], *[grads[n] for n in names], *[delta[n] for n in names],
            *[new_m[n] for n in names], *[new_v[n] for n in names])
```
